```python
import jax, jax.numpy as jnp
from jax import lax
import numpy as np

D_MODEL = 1024
BATCH = 16
SEQ = 256
DEPTH = 4
DEC_BATCH = 4
DEC_SEQ = 4096
PAST_LEN = 512

GRID_W = 64
EPS = 1e-6
ROPE_THETA = 10000.0
QBLOCK = 128
NEG_INF = -1e30

NA_HEADS = 6
NA_DH = 64
NA_WIN_R = 8
NA_WIN_C = 16
NA_QB = NA_WIN_C
NA_KB = 2 * NA_WIN_C
NA_WIDTH = NA_HEADS * NA_DH

POOL_WINDOWS = (2, 4, 8, 16)
POOL_G = 64
POOL_WIDTH = POOL_G * len(POOL_WINDOWS)

MLA_HEADS = 6
MLA_NOPE = 64
MLA_ROPE = 32
MLA_QK = MLA_NOPE + MLA_ROPE
MLA_V = 64
MLA_Q_RANK = 256
MLA_KV_RANK = 128
MLA_WIDTH = MLA_HEADS * MLA_V

MIX_WIDTH = NA_WIDTH + POOL_WIDTH + MLA_WIDTH
IN_SPLITS = (NA_WIDTH, NA_WIDTH, NA_WIDTH, POOL_WIDTH, MLA_Q_RANK, MLA_KV_RANK, MLA_ROPE)
IN_WIDTH = sum(IN_SPLITS)

PEER_HEADS = 8
PEER_NKEYS = 128
PEER_NEXP = PEER_NKEYS * PEER_NKEYS
PEER_DKEY = 128
PEER_TOPK = 16
PEER_TOK_BLOCK = 128

kernel_name = 'hybrid_flow_backbone_step'


def rms_norm(x, g):
    xf = x.astype(jnp.float32)
    y = xf * lax.rsqrt(jnp.mean(xf * xf, axis=-1, keepdims=True) + EPS)
    return (y * g.astype(jnp.float32)).astype(x.dtype)


def modulation(cond, w_mod, b_mod):
    m = jax.nn.silu(cond) @ w_mod + b_mod
    return jnp.split(m[:, None, :], 6, axis=-1)


def modulate(x, shift, scale):
    return x * (1 + scale) + shift


def axial_rope(x):
    T = x.shape[1]
    t = jnp.arange(T)
    half = MLA_ROPE // 2
    inv = ROPE_THETA ** (-jnp.arange(0, half, 2, dtype=jnp.float32) / half)
    xf = x.astype(jnp.float32)

    def rot(xa, pos):
        ang = pos.astype(jnp.float32)[:, None] * inv[None, :]
        cos = jnp.cos(ang)[None, :, None, :]
        sin = jnp.sin(ang)[None, :, None, :]
        x1, x2 = jnp.split(xa, 2, axis=-1)
        return jnp.concatenate([x1 * cos - x2 * sin, x1 * sin + x2 * cos], axis=-1)

    out = jnp.concatenate([rot(xf[..., :half], t // GRID_W), rot(xf[..., half:], t % GRID_W)], axis=-1)
    return out.astype(x.dtype)


def rope_tail(x):
    return jnp.concatenate([x[..., :MLA_NOPE], axial_rope(x[..., MLA_NOPE:])], axis=-1)


def dense_block_attn(q, k, v, scale):
    B, T, H, dq = q.shape
    dv = v.shape[-1]
    nb = T // QBLOCK
    qb = jnp.moveaxis(q.reshape(B, nb, QBLOCK, H, dq), 1, 0)

    def one_block(qi):
        s = jnp.einsum('bqhd,bshd->bhqs', qi, k, preferred_element_type=jnp.float32) * scale
        p = jax.nn.softmax(s, axis=-1)
        return jnp.einsum('bhqs,bshd->bqhd', p.astype(v.dtype), v)

    o = lax.map(one_block, qb)
    return jnp.moveaxis(o, 0, 1).reshape(B, T, H * dv)


def nat_latent_attn(q, k, v, kc, vc, rel_bias):
    B, T, H, dh = q.shape
    rows = T // GRID_W
    wr = min(NA_WIN_R, rows)
    ncb = GRID_W // NA_QB
    scale = dh ** -0.5
    qcol = np.arange(GRID_W).reshape(ncb, NA_QB)
    cstart = np.clip(qcol - NA_WIN_C // 2, 0, GRID_W - NA_WIN_C)
    kb = np.clip(np.arange(ncb) * NA_QB - NA_WIN_C // 2, 0, GRID_W - NA_KB)
    kcol = kb[:, None] + np.arange(NA_KB)
    col_ok = (kcol[:, None, :] >= cstart[..., None]) & (kcol[:, None, :] < cstart[..., None] + NA_WIN_C)
    dc_idx = np.clip(kcol[:, None, :] - qcol[..., None] + NA_WIN_C - 1, 0, 2 * NA_WIN_C - 2)
    kg = k.reshape(B, rows, GRID_W, H, dh)
    vg = v.reshape(B, rows, GRID_W, H, dh)
    kcb = kg[:, :, kcol]
    vcb = vg[:, :, kcol]
    bias_c = rel_bias[:, :, dc_idx]
    qr = jnp.moveaxis(q.reshape(B, rows, ncb, NA_QB, H, dh), 1, 0)
    mask = jnp.asarray(col_ok)[None, None, :, :, None, :]

    def one_row(args):
        r, q_r = args
        rs = jnp.clip(r - NA_WIN_R // 2, 0, rows - wr)
        k_band = lax.dynamic_slice_in_dim(kcb, rs, wr, axis=1)
        v_band = lax.dynamic_slice_in_dim(vcb, rs, wr, axis=1)
        dr_idx = rs + jnp.arange(wr) - r + NA_WIN_R - 1
        bias = jnp.transpose(bias_c[:, dr_idx], (0, 2, 3, 1, 4))
        s_loc = jnp.einsum('bnqhd,bwnkhd->bhnqwk', q_r, k_band, preferred_element_type=jnp.float32) * scale
        s_loc = jnp.where(mask, s_loc + bias[None].astype(jnp.float32), NEG_INF)
        s_ctx = jnp.einsum('bnqhd,blhd->bhnql', q_r, kc, preferred_element_type=jnp.float32) * scale
        s = jnp.concatenate([s_loc.reshape(B, H, ncb, NA_QB, wr * NA_KB), s_ctx], axis=-1)
        p = jax.nn.softmax(s, axis=-1).astype(v.dtype)
        p_loc = p[..., :wr * NA_KB].reshape(B, H, ncb, NA_QB, wr, NA_KB)
        p_ctx = p[..., wr * NA_KB:]
        return (jnp.einsum('bhnqwk,bwnkhd->bnqhd', p_loc, v_band)
                + jnp.einsum('bhnql,blhd->bnqhd', p_ctx, vc))

    o = lax.map(one_row, (jnp.arange(rows), qr))
    return jnp.moveaxis(o, 0, 1).reshape(B, T, H * dh)


def pool_mixer(p, w_pool, pool_scale):
    B, L, _ = p.shape
    ng = len(POOL_WINDOWS)
    pf = p.astype(jnp.float32).reshape(B, L, ng, POOL_G)
    csum = jnp.concatenate([jnp.zeros((B, 1, ng, POOL_G), jnp.float32), jnp.cumsum(pf, axis=1)], axis=1)
    t = jnp.arange(L)
    diffs = []
    for gi, w in enumerate(POOL_WINDOWS):
        lo = jnp.clip(t - w // 2, 0, L)
        hi = jnp.clip(t + (w - w // 2), 0, L)
        cg = csum[:, :, gi]
        mean = (cg[:, hi] - cg[:, lo]) / (hi - lo).astype(jnp.float32)[None, :, None]
        diffs.append(mean - pf[:, :, gi])
    d = jnp.stack(diffs, axis=2).astype(p.dtype)
    y = jnp.einsum('blgc,gcd->blgd', d, w_pool).reshape(B, L, POOL_WIDTH)
    return y * pool_scale


def peer(h, wq, sub_keys, u, v):
    B, L, D = h.shape
    half = PEER_DKEY // 2
    xb_all = h.reshape(-1, PEER_TOK_BLOCK, D)

    def one_block(xb):
        n = xb.shape[0]
        q = (xb @ wq).reshape(n, PEER_HEADS, 2, half)
        s = jnp.einsum('nhpk,pmk->nhpm', q, sub_keys, preferred_element_type=jnp.float32)
        sv, si = lax.top_k(s, PEER_TOPK)
        cand_s = (sv[:, :, 0, :, None] + sv[:, :, 1, None, :]).reshape(n, PEER_HEADS, PEER_TOPK * PEER_TOPK)
        cand_i = (si[:, :, 0, :, None] * PEER_NKEYS + si[:, :, 1, None, :]).reshape(n, PEER_HEADS, PEER_TOPK * PEER_TOPK)
        top_s, top_j = lax.top_k(cand_s, PEER_TOPK)
        e = jnp.take_along_axis(cand_i, top_j, axis=-1)
        g = jax.nn.softmax(top_s, axis=-1)
        act = jax.nn.gelu(jnp.einsum('nd,nhkd->nhk', xb, u[e], preferred_element_type=jnp.float32))
        return jnp.einsum('nhk,nhkd->nd', (g * act).astype(v.dtype), v[e])

    y = lax.map(one_block, xb_all)
    return y.reshape(B, L, D)


def branch_inputs(h, lp):
    B, L, _ = h.shape
    z = h @ lp['w_in']
    idx = np.cumsum(IN_SPLITS)[:-1].tolist()
    q_na, k_na, v_na, p, cq, ckv, krope = jnp.split(z, idx, axis=-1)
    q_na = rms_norm(q_na.reshape(B, L, NA_HEADS, NA_DH), lp['na_q_norm'])
    k_na = rms_norm(k_na.reshape(B, L, NA_HEADS, NA_DH), lp['na_k_norm'])
    v_na = v_na.reshape(B, L, NA_HEADS, NA_DH)
    cq = rms_norm(cq, lp['mla_cq_norm'])
    q_mla = rms_norm(jnp.einsum('blr,rhe->blhe', cq, lp['mla_w_uq']), lp['mla_q_norm'])
    ckv = rms_norm(ckv, lp['mla_ckv_norm'])
    return q_na, k_na, v_na, p, q_mla, ckv, krope


def mla_kv(ckv, krope, w_ukv, k_norm):
    kv = jnp.einsum('blr,rhe->blhe', ckv, w_ukv)
    k_nope, v = kv[..., :MLA_NOPE], kv[..., MLA_NOPE:]
    k_rope = jnp.broadcast_to(krope[:, :, None, :], k_nope.shape[:-1] + (MLA_ROPE,))
    k = rms_norm(jnp.concatenate([k_nope, k_rope], axis=-1), k_norm)
    return k, v


def context_layer(x, c_ctx, lp):
    sh1, sc1, g1, sh2, sc2, g2 = modulation(c_ctx[None, :], lp['w_mod'], lp['b_mod'])
    h = modulate(rms_norm(x, lp['norm1']), sh1, sc1)
    q_na, k_na, v_na, p, q_mla, ckv, krope = branch_inputs(h, lp)
    o_na = dense_block_attn(q_na, k_na, v_na, NA_DH ** -0.5)
    o_pool = pool_mixer(p, lp['pool_w'], lp['pool_scale'])
    k_mla, v_mla = mla_kv(ckv, krope, lp['mla_w_ukv'], lp['mla_k_norm'])
    o_mla = dense_block_attn(q_mla, k_mla, v_mla, MLA_QK ** -0.5)
    x = x + g1 * (jnp.concatenate([o_na, o_pool, o_mla], axis=-1) @ lp['w_out'])
    h2 = modulate(rms_norm(x, lp['norm2']), sh2, sc2)
    x = x + g2 * peer(h2, lp['peer_wq'], lp['peer_subkeys'], lp['peer_u'], lp['peer_v'])
    return x, k_na, v_na, ckv, krope


def latent_layer(x, c, kc_na, vc_na, ckv_c, krope_c, lp):
    sh1, sc1, g1, sh2, sc2, g2 = modulation(c, lp['w_mod'], lp['b_mod'])
    h = modulate(rms_norm(x, lp['norm1']), sh1, sc1)
    q_na, k_na, v_na, p, q_mla, ckv, krope = branch_inputs(h, lp)
    o_na = nat_latent_attn(q_na, k_na, v_na, kc_na, vc_na, lp['na_rel_bias'])
    o_pool = pool_mixer(p, lp['pool_w'], lp['pool_scale'])
    k_lat, v_lat = mla_kv(ckv, krope, lp['mla_w_ukv'], lp['mla_k_norm'])
    k_ctx, v_ctx = mla_kv(ckv_c, krope_c, lp['mla_w_ukv'], lp['mla_k_norm'])
    q_mla = rope_tail(q_mla)
    k_lat = rope_tail(k_lat)
    o_mla = dense_block_attn(q_mla, jnp.concatenate([k_lat, k_ctx], axis=1),
                             jnp.concatenate([v_lat, v_ctx], axis=1), MLA_QK ** -0.5)
    x = x + g1 * (jnp.concatenate([o_na, o_pool, o_mla], axis=-1) @ lp['w_out'])
    h2 = modulate(rms_norm(x, lp['norm2']), sh2, sc2)
    x = x + g2 * peer(h2, lp['peer_wq'], lp['peer_subkeys'], lp['peer_u'], lp['peer_v'])
    return x


def setup_inputs(seed: int = 0) -> dict:
    key = jax.random.key(seed)
    keys = iter(jax.random.split(key, 32))

    def nrm(shape, std):
        return jax.random.normal(next(keys), shape, jnp.float32) * std

    def gain(shape):
        return 1.0 + nrm(shape, 0.05)

    D = D_MODEL
    return {
        'x_prompt': nrm((BATCH, SEQ, D), 1.0),
        'x_sample': nrm((DEC_BATCH, DEC_SEQ, D), 1.0),
        'c': nrm((DEC_BATCH, D), 1.0),
        'cache_nat_k': nrm((DEC_BATCH, DEPTH, PAST_LEN, NA_HEADS, NA_DH), 1.0),
        'cache_nat_v': nrm((DEC_BATCH, DEPTH, PAST_LEN, NA_HEADS, NA_DH), 1.0),
        'cache_mla_ckv': nrm((DEC_BATCH, DEPTH, PAST_LEN, MLA_KV_RANK), 1.0),
        'cache_mla_krope': nrm((DEC_BATCH, DEPTH, PAST_LEN, MLA_ROPE), 1.0),
        'c_ctx': nrm((D,), 1.0),
        'w_mod': nrm((DEPTH, D, 6 * D), 0.5 * D ** -0.5),
        'b_mod': nrm((DEPTH, 6 * D), 0.02),
        'norm1': gain((DEPTH, D)),
        'norm2': gain((DEPTH, D)),
        'w_in': nrm((DEPTH, D, IN_WIDTH), D ** -0.5),
        'na_q_norm': gain((DEPTH, NA_DH)),
        'na_k_norm': gain((DEPTH, NA_DH)),
        'na_rel_bias': nrm((DEPTH, NA_HEADS, 2 * NA_WIN_R - 1, 2 * NA_WIN_C - 1), 0.2),
        'pool_w': nrm((DEPTH, len(POOL_WINDOWS), POOL_G, POOL_G), POOL_G ** -0.5),
        'pool_scale': gain((DEPTH, POOL_WIDTH)),
        'mla_cq_norm': gain((DEPTH, MLA_Q_RANK)),
        'mla_ckv_norm': gain((DEPTH, MLA_KV_RANK)),
        'mla_w_uq': nrm((DEPTH, MLA_Q_RANK, MLA_HEADS, MLA_QK), MLA_Q_RANK ** -0.5),
        'mla_w_ukv': nrm((DEPTH, MLA_KV_RANK, MLA_HEADS, MLA_NOPE + MLA_V), MLA_KV_RANK ** -0.5),
        'mla_q_norm': gain((DEPTH, MLA_QK)),
        'mla_k_norm': gain((DEPTH, MLA_QK)),
        'w_out': nrm((DEPTH, MIX_WIDTH, D), MIX_WIDTH ** -0.5),
        'peer_wq': nrm((DEPTH, D, PEER_HEADS * PEER_DKEY), D ** -0.5),
        'peer_subkeys': nrm((DEPTH, 2, PEER_NKEYS, PEER_DKEY // 2), (PEER_DKEY // 2) ** -0.5),
        'peer_u': nrm((DEPTH, PEER_NEXP, D), D ** -0.5),
        'peer_v': nrm((DEPTH, PEER_NEXP, D), 0.5),
    }


def reference(x_prompt, x_sample, c, cache_nat_k, cache_nat_v, cache_mla_ckv, cache_mla_krope, c_ctx,
              w_mod, b_mod, norm1, norm2, w_in, na_q_norm, na_k_norm, na_rel_bias, pool_w, pool_scale,
              mla_cq_norm, mla_ckv_norm, mla_w_uq, mla_w_ukv, mla_q_norm, mla_k_norm, w_out,
              peer_wq, peer_subkeys, peer_u, peer_v):
    y_prompt = x_prompt
    y_sample = x_sample
    ks, vs, ckvs, krs = [], [], [], []
    for l in range(DEPTH):
        lp = dict(w_mod=w_mod[l], b_mod=b_mod[l], norm1=norm1[l], norm2=norm2[l], w_in=w_in[l],
                  na_q_norm=na_q_norm[l], na_k_norm=na_k_norm[l], na_rel_bias=na_rel_bias[l],
                  pool_w=pool_w[l], pool_scale=pool_scale[l], mla_cq_norm=mla_cq_norm[l],
                  mla_ckv_norm=mla_ckv_norm[l], mla_w_uq=mla_w_uq[l], mla_w_ukv=mla_w_ukv[l],
                  mla_q_norm=mla_q_norm[l], mla_k_norm=mla_k_norm[l], w_out=w_out[l],
                  peer_wq=peer_wq[l], peer_subkeys=peer_subkeys[l], peer_u=peer_u[l], peer_v=peer_v[l])
        y_prompt, k_l, v_l, ckv_l, kr_l = context_layer(y_prompt, c_ctx, lp)
        ks.append(k_l)
        vs.append(v_l)
        ckvs.append(ckv_l)
        krs.append(kr_l)
        y_sample = latent_layer(y_sample, c, cache_nat_k[:, l], cache_nat_v[:, l],
                                cache_mla_ckv[:, l], cache_mla_krope[:, l], lp)
    new_nat_k = jnp.stack(ks, axis=1)
    new_nat_v = jnp.stack(vs, axis=1)
    new_mla_ckv = jnp.stack(ckvs, axis=1)
    new_mla_krope = jnp.stack(krs, axis=1)
    return (y_prompt, y_sample, new_nat_k, new_nat_v, new_mla_ckv, new_mla_krope)
```

```python
import functools

import numpy as np
import jax
import jax.numpy as jnp
from jax import lax
from jax.experimental import pallas as pl
from jax.experimental.pallas import tpu as pltpu

F32 = jnp.float32
BF16 = jnp.bfloat16

EPS = 1e-6
ROPE_THETA = 10000.0
NEG_INF = -1e30
GRID_W = 64
HEADS = 6
NA_DH = 64
WIN_R = 8
WIN_C = 16
POOL_WINDOWS = (2, 4, 8, 16)
POOL_G = 64
MLA_NOPE = 64
MLA_ROPE = 32
MLA_QK = MLA_NOPE + MLA_ROPE
MLA_V = 64
PEER_HEADS = 8
PEER_NKEYS = 128
PEER_TOPK = 16
LANE = 128
HW = HEADS * LANE
TB = 256
TQ = 256
PEER_TB = 128
PEER_SUB = 8
VMEM_LIMIT = 56 * 1024 * 1024

_CQ, _CK, _CV = 0, HW, 2 * HW
_CP = 3 * HW
_CCQ = _CP + 256
_CCKV = _CCQ + 256
_CKR = _CCKV + 128
IN_W = _CKR + 128


def _params(sem, vmem=VMEM_LIMIT):
    return pltpu.CompilerParams(dimension_semantics=sem, vmem_limit_bytes=vmem)


def _const_spec(shape):
    n = len(shape)
    return pl.BlockSpec(shape, lambda *_: (0,) * n)


def _nt_dot(a, b):
    return lax.dot_general(a, b, (((1,), (1,)), ((), ())), preferred_element_type=F32)


def _mod_kernel(c_ref, w_ref, b_ref, o_ref):
    c = c_ref[...]
    s = c / (1.0 + jnp.exp(-c))
    o_ref[0] = jnp.dot(s, w_ref[0], preferred_element_type=F32,
                       precision=lax.Precision.HIGHEST) + b_ref[0]


def _modulation(cond8, w_mod, b_mod):
    depth, d, n6 = w_mod.shape
    tn = n6 // 4
    return pl.pallas_call(
        _mod_kernel,
        grid=(depth, n6 // tn),
        in_specs=[_const_spec((8, d)),
                  pl.BlockSpec((1, d, tn), lambda l, j: (l, 0, j)),
                  pl.BlockSpec((1, 1, tn), lambda l, j: (l, 0, j))],
        out_specs=pl.BlockSpec((1, 8, tn), lambda l, j: (l, 0, j)),
        out_shape=jax.ShapeDtypeStruct((depth, 8, n6), F32),
        compiler_params=_params(("arbitrary", "arbitrary")),
        name="modulation",
    )(cond8, w_mod, b_mod.reshape(depth, 1, n6))


def _rms(z, gain):
    return z * lax.rsqrt(jnp.mean(z * z, axis=-1, keepdims=True) + EPS) * gain


def _head_rms(zh, gain_h, n_real):
    ms = jnp.sum(zh * zh, axis=-1, keepdims=True) * (1.0 / n_real)
    return zh * lax.rsqrt(ms + EPS) * gain_h


def _rope(zh, cos, sin, is_x1):
    rot = jnp.where(is_x1, pltpu.roll(zh, LANE - 8, 1), pltpu.roll(zh, 8, 1))
    return zh * cos + rot * sin


def _is_x1(rows):
    lane = lax.broadcasted_iota(jnp.int32, (rows, LANE), 1)
    first = jnp.where(lane >= MLA_NOPE, jnp.where(lane < MLA_NOPE + 8, 1, 0), 0)
    second = jnp.where(lane >= MLA_NOPE + 16, jnp.where(lane < MLA_NOPE + 24, 1, 0), 0)
    return (first + second) > 0


def _mla_kv(ck, wk_ref, wv_ref, gk_ref, cos, sin, km_ref, vm_ref):
    rows = ck.shape[0]
    kk = jnp.dot(ck, wk_ref[...], preferred_element_type=F32)
    is_x1 = _is_x1(rows)
    for h in range(HEADS):
        sl = slice(h * LANE, (h + 1) * LANE)
        kh = _head_rms(kk[:, sl], gk_ref[:, sl], MLA_QK)
        km_ref[:, sl] = _rope(kh, cos, sin, is_x1).astype(BF16)
    vm_ref[...] = jnp.dot(ck, wv_ref[...], preferred_element_type=F32).astype(BF16)


def _in_kernel(x_ref, mod_ref, n1_ref, w_ref, wuq_ref, wk_ref, wv_ref,
               gq_ref, gk_ref, gcq_ref, gckv_ref, gqm_ref, gkm_ref, cos_ref, sin_ref,
               qn_ref, kn_ref, vn_ref, knf_ref, vnf_ref, p_ref,
               qm_ref, km_ref, vm_ref, ckv_ref, kr_ref):
    d = x_ref.shape[1]
    rows = x_ref.shape[0]
    mod = mod_ref[0]
    sh1 = mod[:, 0:d]
    sc1 = mod[:, d:2 * d]
    h = _rms(x_ref[...], n1_ref[...]) * (1.0 + sc1) + sh1
    hb = h.astype(BF16)

    def proj(lo, hi):
        return jnp.dot(hb, w_ref[:, lo:hi], preferred_element_type=F32)

    cos = cos_ref[...]
    sin = sin_ref[...]
    is_x1 = _is_x1(rows)

    zq = proj(_CQ, _CQ + HW)
    zk = proj(_CK, _CK + HW)
    for hh in range(HEADS):
        sl = slice(hh * LANE, (hh + 1) * LANE)
        qn_ref[:, sl] = (_head_rms(zq[:, sl], gq_ref[:, sl], NA_DH) * (NA_DH ** -0.5)).astype(BF16)
        kh = _head_rms(zk[:, sl], gk_ref[:, sl], NA_DH)
        knf_ref[:, sl] = kh
        kn_ref[:, sl] = kh.astype(BF16)
    zv = proj(_CV, _CV + HW)
    vnf_ref[...] = zv
    vn_ref[...] = zv.astype(BF16)
    p_ref[...] = proj(_CP, _CP + 256)

    cq = _rms(proj(_CCQ, _CCQ + 256), gcq_ref[...])
    zqm = jnp.dot(cq.astype(BF16), wuq_ref[...], preferred_element_type=F32)
    for hh in range(HEADS):
        sl = slice(hh * LANE, (hh + 1) * LANE)
        qh = _head_rms(zqm[:, sl], gqm_ref[:, sl], MLA_QK)
        qm_ref[:, sl] = (_rope(qh, cos, sin, is_x1) * (MLA_QK ** -0.5)).astype(BF16)

    ckv = _rms(proj(_CCKV, _CCKV + 128), gckv_ref[...])
    kr = proj(_CKR, _CKR + 128)
    ckv_ref[...] = ckv
    kr_ref[...] = kr
    ck = jnp.concatenate([ckv, kr], axis=-1).astype(BF16)
    _mla_kv(ck, wk_ref, wv_ref, gkm_ref, cos, sin, km_ref, vm_ref)


def _in_proj(x, mod_l, row_off, bpm, lw, cos_t, sin_t, rope_blocks):
    n, d = x.shape
    nb = n // TB
    tok = lambda w: pl.BlockSpec((TB, w), lambda i: (i, 0))
    rope_spec = pl.BlockSpec((TB, LANE), lambda i: (i % rope_blocks, 0))
    in_specs = [tok(d),
                pl.BlockSpec((1, 1, mod_l.shape[-1]), lambda i: (row_off + i // bpm, 0, 0)),
                _const_spec((1, d)), _const_spec((d, IN_W)), _const_spec((256, HW)),
                _const_spec((256, HW)), _const_spec((256, HW)),
                _const_spec((1, HW)), _const_spec((1, HW)), _const_spec((1, 256)),
                _const_spec((1, 128)), _const_spec((1, HW)), _const_spec((1, HW)),
                rope_spec, rope_spec]
    widths = [(HW, BF16), (HW, BF16), (HW, BF16), (HW, F32), (HW, F32), (256, F32),
              (HW, BF16), (HW, BF16), (HW, BF16), (128, F32), (128, F32)]
    return pl.pallas_call(
        _in_kernel,
        grid=(nb,),
        in_specs=in_specs,
        out_specs=[tok(w) for w, _ in widths],
        out_shape=[jax.ShapeDtypeStruct((n, w), dt) for w, dt in widths],
        compiler_params=_params(("arbitrary",)),
        name="in_proj",
    )(x, mod_l, lw["norm1"], lw["w_in"], lw["w_uq"], lw["w_k"], lw["w_v"],
      lw["g_q"], lw["g_k"], lw["g_cq"], lw["g_ckv"], lw["g_qm"], lw["g_km"], cos_t, sin_t)


def _cache_kernel(ck_ref, wk_ref, wv_ref, gk_ref, km_ref, vm_ref):
    rows = ck_ref.shape[2]
    cos = jnp.ones((rows, LANE), F32)
    sin = jnp.zeros((rows, LANE), F32)
    _mla_kv(ck_ref[0, 0], wk_ref.at[0], wv_ref.at[0], gk_ref.at[0], cos, sin,
            km_ref.at[0, 0], vm_ref.at[0, 0])


def _cache_kv(ck, w_k, w_v, g_km):
    db, depth, p, _ = ck.shape
    spec = lambda w: pl.BlockSpec((1, 1, p, w), lambda b, l: (b, l, 0, 0))
    wspec = lambda r: pl.BlockSpec((1, r, HW), lambda b, l: (l, 0, 0))
    return pl.pallas_call(
        _cache_kernel,
        grid=(db, depth),
        in_specs=[spec(256), wspec(256), wspec(256), wspec(1)],
        out_specs=[spec(HW), spec(HW)],
        out_shape=[jax.ShapeDtypeStruct((db, depth, p, HW), BF16)] * 2,
        compiler_params=_params(("arbitrary", "arbitrary")),
        name="cache_kv",
    )(ck, w_k, w_v, g_km)


def _softmax_av(s_list, v_list):
    m = s_list[0].max(axis=-1, keepdims=True)
    for s in s_list[1:]:
        m = jnp.maximum(m, s.max(axis=-1, keepdims=True))
    acc = None
    den = None
    for s, v in zip(s_list, v_list):
        p = jnp.exp(s - m)
        l = p.sum(axis=-1, keepdims=True)
        o = jnp.dot(p.astype(BF16), v, preferred_element_type=F32)
        acc = o if acc is None else acc + o
        den = l if den is None else den + l
    return acc / den


def _ctx_attn_kernel(qn, kn, vn, qm, km, vm, on, om):
    for q, k, v, o in ((qn, kn, vn, on), (qm, km, vm, om)):
        for h in range(HEADS):
            sl = slice(h * LANE, (h + 1) * LANE)
            s = _nt_dot(q[:, sl], k[:, sl])
            o[:, sl] = _softmax_av([s], [v[:, sl]]).astype(BF16)


def _ctx_attn(qn, kn, vn, qm, km, vm, seq):
    n = qn.shape[0]
    spec = pl.BlockSpec((seq, HW), lambda i: (i, 0))
    return pl.pallas_call(
        _ctx_attn_kernel,
        grid=(n // seq,),
        in_specs=[spec] * 6,
        out_specs=[spec] * 2,
        out_shape=[jax.ShapeDtypeStruct((n, HW), BF16)] * 2,
        compiler_params=_params(("arbitrary",)),
        name="ctx_attn",
    )(qn, kn, vn, qm, km, vm)


def _lat_mla_kernel(q, k, v, kc, vc, o):
    s1 = _nt_dot(q[...], k[...])
    s2 = _nt_dot(q[...], kc[0])
    o[...] = _softmax_av([s1, s2], [v[...], vc[0]]).astype(BF16)


def _lat_mla(qm, km, vm, kc, vc, db):
    n = qm.shape[0]
    ds = n // db
    nq = ds // TQ
    qspec = pl.BlockSpec((TQ, LANE), lambda b, h, i: (b * nq + i, h))
    kspec = pl.BlockSpec((ds, LANE), lambda b, h, i: (b, h))
    cspec = pl.BlockSpec((1, kc.shape[1], LANE), lambda b, h, i: (b, 0, h))
    return pl.pallas_call(
        _lat_mla_kernel,
        grid=(db, HEADS, nq),
        in_specs=[qspec, kspec, kspec, cspec, cspec],
        out_specs=qspec,
        out_shape=jax.ShapeDtypeStruct((n, HW), BF16),
        compiler_params=_params(("arbitrary",) * 3),
        name="lat_mla",
    )(qm, km, vm, kc, vc)


def _nat_kernel(q, k, v, kc, vc, bias, o, *, rows):
    r = pl.program_id(1)
    rs = jnp.clip(r - WIN_R // 2, 0, rows - WIN_R)
    start = pl.multiple_of(rs * GRID_W, GRID_W)
    band = WIN_R * GRID_W
    for h in range(HEADS):
        sl = slice(h * LANE, (h + 1) * LANE)
        qh = q[:, sl]
        s1 = _nt_dot(qh, k[pl.ds(start, band), sl]) + bias[0, h]
        s2 = _nt_dot(qh, kc[0, :, sl])
        o[:, sl] = _softmax_av([s1, s2], [v[pl.ds(start, band), sl], vc[0, :, sl]]).astype(BF16)


def _nat_attn(qn, kn, vn, kc, vc, bias, db):
    n = qn.shape[0]
    ds = n // db
    rows = ds // GRID_W
    band = WIN_R * GRID_W

    def variant(r):
        return jnp.where(r < WIN_R // 2, r, jnp.where(r > rows - WIN_R // 2, r - (rows - WIN_R), WIN_R // 2))

    qspec = pl.BlockSpec((GRID_W, HW), lambda b, r: (b * rows + r, 0))
    kspec = pl.BlockSpec((ds, HW), lambda b, r: (b, 0))
    cspec = pl.BlockSpec((1, kc.shape[1], HW), lambda b, r: (b, 0, 0))
    bspec = pl.BlockSpec((1, HEADS, GRID_W, band), lambda b, r: (variant(r), 0, 0, 0))
    return pl.pallas_call(
        functools.partial(_nat_kernel, rows=rows),
        grid=(db, rows),
        in_specs=[qspec, kspec, kspec, cspec, cspec, bspec],
        out_specs=qspec,
        out_shape=jax.ShapeDtypeStruct((n, HW), BF16),
        compiler_params=_params(("arbitrary", "arbitrary")),
        name="nat_attn",
    )(qn, kn, vn, kc, vc, bias)


def _split3(x):
    hi = x.astype(BF16)
    r = x - hi.astype(F32)
    mid = r.astype(BF16)
    lo = (r - mid.astype(F32)).astype(BF16)
    return hi, mid, lo


def _pool(p_prev, p_cur, p_next, posb, seq_len):
    rows = p_cur.shape[0]
    halo = p_prev.shape[0]
    ext = rows + 2 * halo
    pext = jnp.concatenate([p_prev, p_cur, p_next], axis=0)
    parts = _split3(pext)
    t = posb + lax.broadcasted_iota(jnp.int32, (rows, ext), 0)
    s = posb - halo + lax.broadcasted_iota(jnp.int32, (rows, ext), 1)
    tcol = posb + lax.broadcasted_iota(jnp.int32, (rows, 1), 0)
    grp = lax.broadcasted_iota(jnp.int32, (rows, 256), 1) // POOL_G
    d = jnp.zeros((rows, 256), F32)
    for gi, w in enumerate(POOL_WINDOWS):
        lo = jnp.maximum(t - w // 2, 0)
        hi = jnp.minimum(t + (w - w // 2), seq_len)
        sel = jnp.where(s >= lo, jnp.where(s < hi, 1.0, 0.0), 0.0).astype(BF16)
        tot = sum(jnp.dot(sel, part, preferred_element_type=F32) for part in parts)
        cnt = (jnp.minimum(tcol + (w - w // 2), seq_len) - jnp.maximum(tcol - w // 2, 0)).astype(F32)
        d = jnp.where(grp == gi, tot / cnt - p_cur, d)
    return d


def _first_max(x, pos, sentinel):
    m = jnp.max(x, axis=0, keepdims=True)
    idx = jnp.min(jnp.where(x == m, pos, sentinel), axis=0, keepdims=True)
    return m, idx


def _topk_head(qh, sk_ref):
    c = qh.shape[0]
    key_pos = lax.broadcasted_iota(jnp.int32, (PEER_NKEYS, c), 0).astype(F32)
    row16 = lax.broadcasted_iota(jnp.int32, (PEER_TOPK, c), 0)
    neg = jnp.float32(-jnp.inf)
    s0 = _nt_dot(sk_ref[0], qh)
    s1 = _nt_dot(sk_ref[1], qh)

    def stage1(a, carry):
        out = []
        for s, sv, si in (carry[0:3], carry[3:6]):
            m, idx = _first_max(s, key_pos, float(PEER_NKEYS))
            out += [jnp.where(key_pos == idx, neg, s),
                    jnp.where(row16 == a, m, sv), jnp.where(row16 == a, idx, si)]
        return tuple(out)

    zf = jnp.zeros((PEER_TOPK, c), F32)
    _, sv0, si0, _, sv1, si1 = lax.fori_loop(0, PEER_TOPK, stage1, (s0, zf, zf, s1, zf, zf))

    sub8 = lax.broadcasted_iota(jnp.int32, (8, c), 0)
    sub8f = sub8.astype(F32)
    cs, ci, cf = [], [], []

    def piece(a_vals, a_ids, a_flat, b_vals, b_ids, b_flat, nb):
        val = a_vals + b_vals
        if nb < 8:
            val = jnp.where(sub8 < nb, val, neg)
        cs.append(val)
        ci.append(a_ids * float(PEER_NKEYS) + b_ids)
        cf.append(jnp.broadcast_to(a_flat * float(PEER_TOPK) + b_flat, (8, c)))

    for a in range(8):
        nb = PEER_TOPK // (a + 1)
        for b0 in range(0, nb, 8):
            piece(sv0[a:a + 1], si0[a:a + 1], float(a), sv1[b0:b0 + 8], si1[b0:b0 + 8],
                  sub8f + float(b0), min(nb - b0, 8))
    piece(sv0[8:16], si0[8:16], sub8f + 8.0, sv1[0:1], si1[0:1], jnp.zeros((8, c), F32), 8)
    npc = len(cs)
    nflat = float(PEER_TOPK * PEER_TOPK)

    def stage2(k, carry):
        vals = list(carry[:npc])
        tv, te = carry[npc], carry[npc + 1]
        m = vals[0]
        for v in vals[1:]:
            m = jnp.maximum(m, v)
        m = jnp.max(m, axis=0, keepdims=True)
        pos = None
        for v, f in zip(vals, cf):
            cand = jnp.where(v == m, f, nflat)
            pos = cand if pos is None else jnp.minimum(pos, cand)
        pos = jnp.min(pos, axis=0, keepdims=True)
        e = None
        for i, f in zip(ci, cf):
            cand = jnp.where(f == pos, i, -1.0)
            e = cand if e is None else jnp.maximum(e, cand)
        e = jnp.max(e, axis=0, keepdims=True)
        vals = [jnp.where(f == pos, neg, v) for v, f in zip(vals, cf)]
        return tuple(vals) + (jnp.where(row16 == k, m, tv), jnp.where(row16 == k, e, te))

    res = lax.fori_loop(0, PEER_TOPK, stage2, tuple(cs) + (zf, zf))
    return res[npc], res[npc + 1]


def _out_kernel(on_ref, om_ref, pc_ref, pp_ref, pn_ref, x_ref, mod_ref,
                won_ref, wop_ref, wom_ref, pw_ref, ps_ref, n2_ref, wq_ref, sk_ref,
                x1_ref, h2_ref, ids_ref, gt_ref, q_scr, idt_scr, *, bps, seq_len):
    d = x_ref.shape[1]
    rows = x_ref.shape[0]
    i = pl.program_id(0)
    mod = mod_ref[0]
    g1 = mod[:, 2 * d:3 * d]
    sh2 = mod[:, 3 * d:4 * d]
    sc2 = mod[:, 4 * d:5 * d]

    posb = (i % bps) * rows
    dpool = _pool(pp_ref[...], pc_ref[...], pn_ref[...], posb, seq_len)
    ypool = jnp.dot(dpool.astype(BF16), pw_ref[...], preferred_element_type=F32) * ps_ref[...]
    mix = (jnp.dot(on_ref[...], won_ref[...], preferred_element_type=F32)
           + jnp.dot(ypool.astype(BF16), wop_ref[...], preferred_element_type=F32)
           + jnp.dot(om_ref[...], wom_ref[...], preferred_element_type=F32))
    x1 = x_ref[...] + g1 * mix
    x1_ref[...] = x1
    h2 = _rms(x1, n2_ref[...]) * (1.0 + sc2) + sh2
    h2_ref[...] = h2

    q = jnp.dot(h2.astype(BF16), wq_ref[...], preferred_element_type=F32)
    for hh in range(PEER_HEADS):
        q_scr[hh] = q[:, hh * LANE:(hh + 1) * LANE].astype(BF16)

    for c0 in range(0, rows, LANE):
        def head(hh, _):
            tv, te = _topk_head(q_scr[hh, c0:c0 + LANE, :], sk_ref)
            ex = jnp.exp(tv - tv[0:1])
            gates = ex / jnp.sum(ex, axis=0, keepdims=True)
            r0 = pl.multiple_of(hh * PEER_TOPK, PEER_TOPK)
            gt_ref[pl.ds(r0, PEER_TOPK), c0:c0 + LANE] = gates
            idt_scr[pl.ds(r0, PEER_TOPK), c0:c0 + LANE] = te
            return 0

        lax.fori_loop(0, PEER_HEADS, head, 0)
    ids_ref[...] = idt_scr[...].T.astype(jnp.int32)


def _out_proj(x, on, om, p, mod_l, row_off, bpm, lw, seq_len):
    n, d = x.shape
    nb = n // TB
    bps = seq_len // TB
    halo = 8
    hb = TB // halo
    tok = lambda w: pl.BlockSpec((TB, w), lambda i: (i, 0))
    in_specs = [tok(HW), tok(HW), tok(256),
                pl.BlockSpec((halo, 256), lambda i: (jnp.maximum(i * hb - 1, 0), 0)),
                pl.BlockSpec((halo, 256), lambda i: (jnp.minimum((i + 1) * hb, n // halo - 1), 0)),
                tok(d),
                pl.BlockSpec((1, 1, mod_l.shape[-1]), lambda i: (row_off + i // bpm, 0, 0)),
                _const_spec((HW, d)), _const_spec((256, d)), _const_spec((HW, d)),
                _const_spec((256, 256)), _const_spec((1, 256)), _const_spec((1, d)),
                _const_spec((d, PEER_HEADS * LANE)), _const_spec((2, PEER_NKEYS, LANE))]
    nk = PEER_HEADS * PEER_TOPK
    return pl.pallas_call(
        functools.partial(_out_kernel, bps=bps, seq_len=seq_len),
        grid=(nb,),
        in_specs=in_specs,
        out_specs=[tok(d), tok(d), tok(nk), pl.BlockSpec((nk, TB), lambda i: (0, i))],
        out_shape=[jax.ShapeDtypeStruct((n, d), F32), jax.ShapeDtypeStruct((n, d), F32),
                   jax.ShapeDtypeStruct((n, nk), jnp.int32), jax.ShapeDtypeStruct((nk, n), F32)],
        scratch_shapes=[pltpu.VMEM((PEER_HEADS, TB, LANE), BF16), pltpu.VMEM((nk, TB), F32)],
        compiler_params=_params(("arbitrary",)),
        name="out_proj",
    )(on, om, p, p, p, x, mod_l, lw["w_o_na"], lw["w_o_pool"], lw["w_o_mla"],
      lw["pool_w"], lw["pool_scale"], lw["norm2"], lw["peer_wq"], lw["peer_sk"])


def _gelu_tanh(x):
    return x * (0.5 * (1.0 + jnp.tanh(0.7978845608028654 * (x + 0.044715 * (x * x * x)))))


def _peer_kernel(ids_hbm, gt_ref, h2_ref, x1_ref, mod_ref, u_hbm, v_hbm, o_ref,
                 ids_s, ubuf, vbuf, sem_i, sem_u, sem_v, *, layer):
    d = x1_ref.shape[1]
    nsub = x1_ref.shape[0] // PEER_SUB
    nk = gt_ref.shape[0]
    rows_per_sub = PEER_SUB * nk
    i = pl.program_id(0)
    g2 = mod_ref[0][:, 5 * d:6 * d]
    tok_lane = lax.broadcasted_iota(jnp.int32, gt_ref.shape, 1)

    def ids_copy(j, slot):
        row0 = pl.multiple_of((i * nsub + j) * PEER_SUB, PEER_SUB)
        return pltpu.make_async_copy(ids_hbm.at[pl.ds(row0, PEER_SUB), :], ids_s.at[slot], sem_i.at[slot])

    def row_copy(tab, buf, sem, slot, e, f):
        return pltpu.make_async_copy(tab.at[layer, pl.ds(e, 1), :], buf.at[slot, pl.ds(f, 1), :], sem.at[slot])

    def issue_rows(slot):
        for t in range(PEER_SUB):
            def body(kk, _):
                for r in range(8):
                    k = kk * 8 + r
                    e = ids_s[slot, t, k]
                    row_copy(u_hbm, ubuf, sem_u, slot, e, t * nk + k).start()
                    row_copy(v_hbm, vbuf, sem_v, slot, e, t * nk + k).start()
                return 0

            lax.fori_loop(0, nk // 8, body, 0)

    def wait_rows(slot):
        pltpu.make_async_copy(ubuf.at[slot], ubuf.at[slot], sem_u.at[slot]).wait()
        pltpu.make_async_copy(vbuf.at[slot], vbuf.at[slot], sem_v.at[slot]).wait()

    def compute(slot, j):
        base = pl.multiple_of(j * PEER_SUB, PEER_SUB)
        h8 = h2_ref[pl.ds(base, PEER_SUB), :]
        ys = []
        for t in range(PEER_SUB):
            ut = ubuf[slot, t * nk:(t + 1) * nk, :]
            s = jnp.sum(ut * h8[t:t + 1, :], axis=-1, keepdims=True)
            gcol = jnp.sum(jnp.where(tok_lane == base + t, gt_ref[...], 0.0), axis=-1, keepdims=True)
            wgt = gcol * _gelu_tanh(s)
            vt = vbuf[slot, t * nk:(t + 1) * nk, :]
            ys.append(jnp.sum(vt * wgt, axis=0, keepdims=True))
        y8 = jnp.concatenate(ys, axis=0)
        o_ref[pl.ds(base, PEER_SUB), :] = x1_ref[pl.ds(base, PEER_SUB), :] + g2 * y8

    first = ids_copy(0, 0)
    first.start()
    first.wait()
    issue_rows(0)
    ids_copy(1, 1).start()

    def pair(jj, _):
        j0 = 2 * jj
        ids_copy(j0 + 1, 1).wait()
        issue_rows(1)

        @pl.when(j0 + 2 < nsub)
        def _():
            ids_copy(j0 + 2, 0).start()

        wait_rows(0)
        compute(0, j0)

        @pl.when(j0 + 2 < nsub)
        def _():
            ids_copy(j0 + 2, 0).wait()
            issue_rows(0)

        @pl.when(j0 + 3 < nsub)
        def _():
            ids_copy(j0 + 3, 1).start()

        wait_rows(1)
        compute(1, j0 + 1)
        return 0

    lax.fori_loop(0, nsub // 2, pair, 0)


def _peer(x1, h2, ids, gt, mod_l, row_off, bpm, peer_u, peer_v, layer):
    n, d = x1.shape
    nk = gt.shape[0]
    nb = n // PEER_TB
    tok = pl.BlockSpec((PEER_TB, d), lambda i: (i, 0))
    any_spec = pl.BlockSpec(memory_space=pl.ANY)
    return pl.pallas_call(
        functools.partial(_peer_kernel, layer=layer),
        grid=(nb,),
        in_specs=[any_spec,
                  pl.BlockSpec((nk, PEER_TB), lambda i: (0, i)),
                  tok, tok,
                  pl.BlockSpec((1, 1, mod_l.shape[-1]), lambda i: (row_off + i // bpm, 0, 0)),
                  any_spec, any_spec],
        out_specs=tok,
        out_shape=jax.ShapeDtypeStruct((n, d), F32),
        scratch_shapes=[pltpu.SMEM((2, PEER_SUB, nk), jnp.int32),
                        pltpu.VMEM((2, PEER_SUB * nk, d), F32),
                        pltpu.VMEM((2, PEER_SUB * nk, d), F32),
                        pltpu.SemaphoreType.DMA((2,)),
                        pltpu.SemaphoreType.DMA((2,)),
                        pltpu.SemaphoreType.DMA((2,))],
        compiler_params=_params(("arbitrary",)),
        name="peer",
    )(ids, gt, h2, x1, mod_l, peer_u, peer_v)


def _pad_heads(w, width):
    pad = [(0, 0)] * (w.ndim - 1) + [(0, LANE - width)]
    w = jnp.pad(w, pad)
    return w.reshape(w.shape[:-2] + (HW,))


def _head_gain(g, width):
    depth = g.shape[0]
    g = jnp.pad(g, ((0, 0), (0, LANE - width)))
    return jnp.tile(g, (1, HEADS)).reshape(depth, 1, HW)


def _rope_tables(seq):
    t = np.arange(seq)
    half = MLA_ROPE // 2
    inv = ROPE_THETA ** (-np.arange(0, half, 2, dtype=np.float32) / half)
    cos = np.ones((seq, LANE), np.float32)
    sin = np.zeros((seq, LANE), np.float32)
    for off, pos in ((MLA_NOPE, t // GRID_W), (MLA_NOPE + half, t % GRID_W)):
        ang = pos.astype(np.float32)[:, None] * inv[None, :]
        q = half // 2
        cos[:, off:off + q] = np.cos(ang)
        cos[:, off + q:off + half] = np.cos(ang)
        sin[:, off:off + q] = -np.sin(ang)
        sin[:, off + q:off + half] = np.sin(ang)
    return jnp.asarray(cos), jnp.asarray(sin)


def _nat_bias(rel_bias):
    v = np.arange(WIN_R)[:, None]
    j = np.arange(WIN_R)[None, :]
    dr = j - v + WIN_R - 1
    cq = np.arange(GRID_W)[:, None]
    kc = np.arange(GRID_W)[None, :]
    cstart = np.clip(cq - WIN_C // 2, 0, GRID_W - WIN_C)
    ok = (kc >= cstart) & (kc < cstart + WIN_C)
    dc = np.clip(kc - cq + WIN_C - 1, 0, 2 * WIN_C - 2)
    b = rel_bias[:, :, dr]
    b = b[..., dc]
    b = jnp.where(jnp.asarray(ok)[None, None, None, None], b, NEG_INF)
    b = jnp.transpose(b, (0, 2, 1, 4, 3, 5))
    return b.reshape(b.shape[0], WIN_R, HEADS, GRID_W, WIN_R * GRID_W)


def _layer_weights(w_in, na_q_norm, na_k_norm, mla_cq_norm, mla_ckv_norm, mla_w_uq, mla_w_ukv,
                   mla_q_norm, mla_k_norm, w_out, pool_w, pool_scale, norm1, norm2,
                   peer_wq, peer_subkeys):
    depth, d, _ = w_in.shape
    na_w = HEADS * NA_DH
    segs = np.cumsum([0, na_w, na_w, na_w, 256, 256, 128, MLA_ROPE])
    part = lambda i: w_in[:, :, segs[i]:segs[i + 1]]
    heads = lambda w: _pad_heads(w.reshape(depth, d, HEADS, NA_DH), NA_DH)
    w_in_p = jnp.concatenate(
        [heads(part(0)), heads(part(1)), heads(part(2)), part(3), part(4), part(5),
         jnp.pad(part(6), ((0, 0), (0, 0), (0, LANE - MLA_ROPE)))], axis=-1).astype(BF16)

    w_uq = _pad_heads(mla_w_uq, MLA_QK).astype(BF16)
    k_nope = _pad_heads(mla_w_ukv[..., :MLA_NOPE], MLA_NOPE)
    eye = np.zeros((MLA_ROPE, HEADS, LANE), np.float32)
    for h in range(HEADS):
        eye[np.arange(MLA_ROPE), h, MLA_NOPE + np.arange(MLA_ROPE)] = 1.0
    eye = jnp.broadcast_to(jnp.asarray(eye.reshape(MLA_ROPE, HW)), (depth, MLA_ROPE, HW))
    zer = jnp.zeros((depth, 256 - 128 - MLA_ROPE, HW), F32)
    w_k = jnp.concatenate([k_nope, eye, zer], axis=1).astype(BF16)
    w_v = jnp.concatenate([_pad_heads(mla_w_ukv[..., MLA_NOPE:], MLA_V),
                           jnp.zeros((depth, 128, HW), F32)], axis=1).astype(BF16)

    mix_w = HEADS * NA_DH
    w_o_na = jnp.pad(w_out[:, :mix_w].reshape(depth, HEADS, NA_DH, d),
                     ((0, 0), (0, 0), (0, LANE - NA_DH), (0, 0))).reshape(depth, HW, d).astype(BF16)
    w_o_pool = w_out[:, mix_w:mix_w + 256].astype(BF16)
    w_o_mla = jnp.pad(w_out[:, mix_w + 256:].reshape(depth, HEADS, MLA_V, d),
                      ((0, 0), (0, 0), (0, LANE - MLA_V), (0, 0))).reshape(depth, HW, d).astype(BF16)
    ng = len(POOL_WINDOWS)
    pw = jnp.zeros((depth, ng * POOL_G, ng * POOL_G), F32)
    for g in range(ng):
        pw = pw.at[:, g * POOL_G:(g + 1) * POOL_G, g * POOL_G:(g + 1) * POOL_G].set(pool_w[:, g])

    half = peer_subkeys.shape[-1]
    sk = jnp.stack([jnp.pad(peer_subkeys[:, 0], ((0, 0), (0, 0), (0, LANE - half))),
                    jnp.pad(peer_subkeys[:, 1], ((0, 0), (0, 0), (LANE - half, 0)))], axis=1).astype(BF16)

    return dict(
        w_in=w_in_p, w_uq=w_uq, w_k=w_k, w_v=w_v,
        g_q=_head_gain(na_q_norm, NA_DH), g_k=_head_gain(na_k_norm, NA_DH),
        g_cq=mla_cq_norm[:, None, :], g_ckv=mla_ckv_norm[:, None, :],
        g_qm=_head_gain(mla_q_norm, MLA_QK), g_km=_head_gain(mla_k_norm, MLA_QK),
        w_o_na=w_o_na, w_o_pool=w_o_pool, w_o_mla=w_o_mla,
        pool_w=pw.astype(BF16), pool_scale=pool_scale[:, None, :],
        norm1=norm1[:, None, :], norm2=norm2[:, None, :],
        peer_wq=peer_wq.astype(BF16), peer_sk=sk)


def kernel(x_prompt, x_sample, c, cache_nat_k, cache_nat_v, cache_mla_ckv, cache_mla_krope, c_ctx, w_mod, b_mod, norm1, norm2, w_in, na_q_norm, na_k_norm, na_rel_bias, pool_w, pool_scale, mla_cq_norm, mla_ckv_norm, mla_w_uq, mla_w_ukv, mla_q_norm, mla_k_norm, w_out, peer_wq, peer_subkeys, peer_u, peer_v):
    batch, seq, d = x_prompt.shape
    db, ds, _ = x_sample.shape
    depth = w_mod.shape[0]
    past = cache_nat_k.shape[2]
    assert seq == TB and ds % TB == 0 and ds % (GRID_W * WIN_R) == 0 and db + 1 <= 8

    cond8 = jnp.concatenate([c_ctx[None, :], c, jnp.zeros((8 - 1 - db, d), F32)], axis=0)
    mod = _modulation(cond8, w_mod, b_mod).reshape(depth, 8, 1, 6 * d)

    lw_all = _layer_weights(w_in, na_q_norm, na_k_norm, mla_cq_norm, mla_ckv_norm, mla_w_uq,
                            mla_w_ukv, mla_q_norm, mla_k_norm, w_out, pool_w, pool_scale,
                            norm1, norm2, peer_wq, peer_subkeys)
    bias_all = _nat_bias(na_rel_bias)
    cos_lat, sin_lat = _rope_tables(ds)
    cos_ctx = jnp.ones((TB, LANE), F32)
    sin_ctx = jnp.zeros((TB, LANE), F32)

    ck = jnp.concatenate([cache_mla_ckv, cache_mla_krope,
                          jnp.zeros(cache_mla_ckv.shape[:-1] + (256 - 128 - MLA_ROPE,), F32)],
                         axis=-1).astype(BF16)
    kc_mla, vc_mla = _cache_kv(ck, lw_all["w_k"], lw_all["w_v"], lw_all["g_km"])
    kc_na = _pad_heads(cache_nat_k, NA_DH).astype(BF16)
    vc_na = _pad_heads(cache_nat_v, NA_DH).astype(BF16)

    xc = x_prompt.reshape(batch * seq, d)
    xl = x_sample.reshape(db * ds, d)
    lat_bpm = ds // TB
    ctx_bpm = batch * seq // TB + 1
    ks, vs, ckvs, krs = [], [], [], []
    for l in range(depth):
        lw = {k: v[l] for k, v in lw_all.items()}
        mod_l = mod[l]

        (qn, kn, vn, knf, vnf, p, qm, km, vm, ckv, kr) = _in_proj(
            xc, mod_l, 0, ctx_bpm, lw, cos_ctx, sin_ctx, 1)
        on, om = _ctx_attn(qn, kn, vn, qm, km, vm, seq)
        x1, h2, ids, gt = _out_proj(xc, on, om, p, mod_l, 0, ctx_bpm, lw, seq)
        xc = _peer(x1, h2, ids, gt, mod_l, 0, batch * seq // PEER_TB + 1, peer_u, peer_v, l)
        ks.append(knf.reshape(batch, seq, HEADS, LANE)[..., :NA_DH])
        vs.append(vnf.reshape(batch, seq, HEADS, LANE)[..., :NA_DH])
        ckvs.append(ckv.reshape(batch, seq, 128))
        krs.append(kr.reshape(batch, seq, LANE)[..., :MLA_ROPE])

        (qn, kn, vn, _, _, p, qm, km, vm, _, _) = _in_proj(
            xl, mod_l, 1, lat_bpm, lw, cos_lat, sin_lat, lat_bpm)
        on = _nat_attn(qn, kn, vn, kc_na[:, l], vc_na[:, l], bias_all[l], db)
        om = _lat_mla(qm, km, vm, kc_mla[:, l], vc_mla[:, l], db)
        x1, h2, ids, gt = _out_proj(xl, on, om, p, mod_l, 1, lat_bpm, lw, ds)
        xl = _peer(x1, h2, ids, gt, mod_l, 1, ds // PEER_TB, peer_u, peer_v, l)

    return (xc.reshape(batch, seq, d), xl.reshape(db, ds, d),
            jnp.stack(ks, axis=1), jnp.stack(vs, axis=1),
            jnp.stack(ckvs, axis=1), jnp.stack(krs, axis=1))
```

```python
import functools

import numpy as np
import jax
import jax.numpy as jnp
from jax import lax
from jax.experimental import pallas as pl
from jax.experimental.pallas import tpu as pltpu

F32 = jnp.float32
BF16 = jnp.bfloat16

EPS = 1e-6
ROPE_THETA = 10000.0
NEG_INF = -1e30
GRID_W = 64
HEADS = 6
NA_DH = 64
WIN_R = 8
WIN_C = 16
POOL_WINDOWS = (2, 4, 8, 16)
POOL_G = 64
MLA_NOPE = 64
MLA_ROPE = 32
MLA_QK = MLA_NOPE + MLA_ROPE
MLA_V = 64
PEER_HEADS = 8
PEER_NKEYS = 128
PEER_TOPK = 16
LANE = 128
HW = HEADS * LANE
TB = 256
TQ = 256
PEER_TB = 128
PEER_SUB = 8
VMEM_LIMIT = 56 * 1024 * 1024

_CQ, _CK, _CV = 0, HW, 2 * HW
_CP = 3 * HW
_CCQ = _CP + 256
_CCKV = _CCQ + 256
_CKR = _CCKV + 128
IN_W = _CKR + 128


def _params(sem, vmem=VMEM_LIMIT):
    return pltpu.CompilerParams(dimension_semantics=sem, vmem_limit_bytes=vmem)


def _const_spec(shape):
    n = len(shape)
    return pl.BlockSpec(shape, lambda *_: (0,) * n)


def _nt_dot(a, b):
    return lax.dot_general(a, b, (((1,), (1,)), ((), ())), preferred_element_type=F32)


def _mod_kernel(c_ref, w_ref, b_ref, o_ref):
    c = c_ref[...]
    s = c / (1.0 + jnp.exp(-c))
    o_ref[0] = jnp.dot(s, w_ref[0], preferred_element_type=F32,
                       precision=lax.Precision.HIGHEST) + b_ref[0]


def _modulation(cond8, w_mod, b_mod):
    depth, d, n6 = w_mod.shape
    tn = n6 // 4
    return pl.pallas_call(
        _mod_kernel,
        grid=(depth, n6 // tn),
        in_specs=[_const_spec((8, d)),
                  pl.BlockSpec((1, d, tn), lambda l, j: (l, 0, j)),
                  pl.BlockSpec((1, 1, tn), lambda l, j: (l, 0, j))],
        out_specs=pl.BlockSpec((1, 8, tn), lambda l, j: (l, 0, j)),
        out_shape=jax.ShapeDtypeStruct((depth, 8, n6), F32),
        compiler_params=_params(("arbitrary", "arbitrary")),
        name="modulation",
    )(cond8, w_mod, b_mod.reshape(depth, 1, n6))


def _rms(z, gain):
    return z * lax.rsqrt(jnp.mean(z * z, axis=-1, keepdims=True) + EPS) * gain


def _head_rms(zh, gain_h, n_real):
    ms = jnp.sum(zh * zh, axis=-1, keepdims=True) * (1.0 / n_real)
    return zh * lax.rsqrt(ms + EPS) * gain_h


def _rope(zh, cos, sin, is_x1):
    rot = jnp.where(is_x1, pltpu.roll(zh, LANE - 8, 1), pltpu.roll(zh, 8, 1))
    return zh * cos + rot * sin


def _is_x1(rows):
    lane = lax.broadcasted_iota(jnp.int32, (rows, LANE), 1)
    first = jnp.where(lane >= MLA_NOPE, jnp.where(lane < MLA_NOPE + 8, 1, 0), 0)
    second = jnp.where(lane >= MLA_NOPE + 16, jnp.where(lane < MLA_NOPE + 24, 1, 0), 0)
    return (first + second) > 0


def _mla_kv(ck, wk_ref, wv_ref, gk_ref, cos, sin, km_ref, vm_ref):
    rows = ck.shape[0]
    kk = jnp.dot(ck, wk_ref[...], preferred_element_type=F32)
    is_x1 = _is_x1(rows)
    for h in range(HEADS):
        sl = slice(h * LANE, (h + 1) * LANE)
        kh = _head_rms(kk[:, sl], gk_ref[:, sl], MLA_QK)
        km_ref[:, sl] = _rope(kh, cos, sin, is_x1).astype(BF16)
    vm_ref[...] = jnp.dot(ck, wv_ref[...], preferred_element_type=F32).astype(BF16)


def _in_kernel(x_ref, mod_ref, n1_ref, w_ref, wuq_ref, wk_ref, wv_ref,
               gq_ref, gk_ref, gcq_ref, gckv_ref, gqm_ref, gkm_ref, cos_ref, sin_ref,
               qn_ref, kn_ref, vn_ref, knf_ref, vnf_ref, p_ref,
               qm_ref, km_ref, vm_ref, ckv_ref, kr_ref):
    d = x_ref.shape[1]
    rows = x_ref.shape[0]
    mod = mod_ref[0]
    sh1 = mod[:, 0:d]
    sc1 = mod[:, d:2 * d]
    h = _rms(x_ref[...], n1_ref[...]) * (1.0 + sc1) + sh1
    hb = h.astype(BF16)

    def proj(lo, hi):
        return jnp.dot(hb, w_ref[:, lo:hi], preferred_element_type=F32)

    cos = cos_ref[...]
    sin = sin_ref[...]
    is_x1 = _is_x1(rows)

    zq = proj(_CQ, _CQ + HW)
    zk = proj(_CK, _CK + HW)
    for hh in range(HEADS):
        sl = slice(hh * LANE, (hh + 1) * LANE)
        qn_ref[:, sl] = (_head_rms(zq[:, sl], gq_ref[:, sl], NA_DH) * (NA_DH ** -0.5)).astype(BF16)
        kh = _head_rms(zk[:, sl], gk_ref[:, sl], NA_DH)
        knf_ref[:, sl] = kh
        kn_ref[:, sl] = kh.astype(BF16)
    zv = proj(_CV, _CV + HW)
    vnf_ref[...] = zv
    vn_ref[...] = zv.astype(BF16)
    p_ref[...] = proj(_CP, _CP + 256)

    cq = _rms(proj(_CCQ, _CCQ + 256), gcq_ref[...])
    zqm = jnp.dot(cq.astype(BF16), wuq_ref[...], preferred_element_type=F32)
    for hh in range(HEADS):
        sl = slice(hh * LANE, (hh + 1) * LANE)
        qh = _head_rms(zqm[:, sl], gqm_ref[:, sl], MLA_QK)
        qm_ref[:, sl] = (_rope(qh, cos, sin, is_x1) * (MLA_QK ** -0.5)).astype(BF16)

    ckv = _rms(proj(_CCKV, _CCKV + 128), gckv_ref[...])
    kr = proj(_CKR, _CKR + 128)
    ckv_ref[...] = ckv
    kr_ref[...] = kr
    ck = jnp.concatenate([ckv, kr], axis=-1).astype(BF16)
    _mla_kv(ck, wk_ref, wv_ref, gkm_ref, cos, sin, km_ref, vm_ref)


def _in_proj(x, mod_l, row_off, bpm, lw, cos_t, sin_t, rope_blocks):
    n, d = x.shape
    nb = n // TB
    tok = lambda w: pl.BlockSpec((TB, w), lambda i: (i, 0))
    rope_spec = pl.BlockSpec((TB, LANE), lambda i: (i % rope_blocks, 0))
    in_specs = [tok(d),
                pl.BlockSpec((1, 1, mod_l.shape[-1]), lambda i: (row_off + i // bpm, 0, 0)),
                _const_spec((1, d)), _const_spec((d, IN_W)), _const_spec((256, HW)),
                _const_spec((256, HW)), _const_spec((256, HW)),
                _const_spec((1, HW)), _const_spec((1, HW)), _const_spec((1, 256)),
                _const_spec((1, 128)), _const_spec((1, HW)), _const_spec((1, HW)),
                rope_spec, rope_spec]
    widths = [(HW, BF16), (HW, BF16), (HW, BF16), (HW, F32), (HW, F32), (256, F32),
              (HW, BF16), (HW, BF16), (HW, BF16), (128, F32), (128, F32)]
    return pl.pallas_call(
        _in_kernel,
        grid=(nb,),
        in_specs=in_specs,
        out_specs=[tok(w) for w, _ in widths],
        out_shape=[jax.ShapeDtypeStruct((n, w), dt) for w, dt in widths],
        compiler_params=_params(("arbitrary",)),
        name="in_proj",
    )(x, mod_l, lw["norm1"], lw["w_in"], lw["w_uq"], lw["w_k"], lw["w_v"],
      lw["g_q"], lw["g_k"], lw["g_cq"], lw["g_ckv"], lw["g_qm"], lw["g_km"], cos_t, sin_t)


def _cache_kernel(ck_ref, wk_ref, wv_ref, gk_ref, km_ref, vm_ref):
    rows = ck_ref.shape[2]
    cos = jnp.ones((rows, LANE), F32)
    sin = jnp.zeros((rows, LANE), F32)
    _mla_kv(ck_ref[0, 0], wk_ref.at[0], wv_ref.at[0], gk_ref.at[0], cos, sin,
            km_ref.at[0, 0], vm_ref.at[0, 0])


def _cache_kv(ck, w_k, w_v, g_km):
    db, depth, p, _ = ck.shape
    spec = lambda w: pl.BlockSpec((1, 1, p, w), lambda b, l: (b, l, 0, 0))
    wspec = lambda r: pl.BlockSpec((1, r, HW), lambda b, l: (l, 0, 0))
    return pl.pallas_call(
        _cache_kernel,
        grid=(db, depth),
        in_specs=[spec(256), wspec(256), wspec(256), wspec(1)],
        out_specs=[spec(HW), spec(HW)],
        out_shape=[jax.ShapeDtypeStruct((db, depth, p, HW), BF16)] * 2,
        compiler_params=_params(("arbitrary", "arbitrary")),
        name="cache_kv",
    )(ck, w_k, w_v, g_km)


def _softmax_av(s_list, v_list):
    m = s_list[0].max(axis=-1, keepdims=True)
    for s in s_list[1:]:
        m = jnp.maximum(m, s.max(axis=-1, keepdims=True))
    acc = None
    den = None
    for s, v in zip(s_list, v_list):
        p = jnp.exp(s - m)
        l = p.sum(axis=-1, keepdims=True)
        o = jnp.dot(p.astype(BF16), v, preferred_element_type=F32)
        acc = o if acc is None else acc + o
        den = l if den is None else den + l
    return acc / den


def _ctx_attn_kernel(qn, kn, vn, qm, km, vm, on, om):
    for q, k, v, o in ((qn, kn, vn, on), (qm, km, vm, om)):
        for h in range(HEADS):
            sl = slice(h * LANE, (h + 1) * LANE)
            s = _nt_dot(q[:, sl], k[:, sl])
            o[:, sl] = _softmax_av([s], [v[:, sl]]).astype(BF16)


def _ctx_attn(qn, kn, vn, qm, km, vm, seq):
    n = qn.shape[0]
    spec = pl.BlockSpec((seq, HW), lambda i: (i, 0))
    return pl.pallas_call(
        _ctx_attn_kernel,
        grid=(n // seq,),
        in_specs=[spec] * 6,
        out_specs=[spec] * 2,
        out_shape=[jax.ShapeDtypeStruct((n, HW), BF16)] * 2,
        compiler_params=_params(("arbitrary",)),
        name="ctx_attn",
    )(qn, kn, vn, qm, km, vm)


def _lat_mla_kernel(q, k, v, kc, vc, o):
    s1 = _nt_dot(q[...], k[...])
    s2 = _nt_dot(q[...], kc[0])
    o[...] = _softmax_av([s1, s2], [v[...], vc[0]]).astype(BF16)


def _lat_mla(qm, km, vm, kc, vc, db):
    n = qm.shape[0]
    ds = n // db
    nq = ds // TQ
    qspec = pl.BlockSpec((TQ, LANE), lambda b, h, i: (b * nq + i, h))
    kspec = pl.BlockSpec((ds, LANE), lambda b, h, i: (b, h))
    cspec = pl.BlockSpec((1, kc.shape[1], LANE), lambda b, h, i: (b, 0, h))
    return pl.pallas_call(
        _lat_mla_kernel,
        grid=(db, HEADS, nq),
        in_specs=[qspec, kspec, kspec, cspec, cspec],
        out_specs=qspec,
        out_shape=jax.ShapeDtypeStruct((n, HW), BF16),
        compiler_params=_params(("arbitrary",) * 3),
        name="lat_mla",
    )(qm, km, vm, kc, vc)


def _nat_kernel(q, k, v, kc, vc, bias, o, *, rows):
    r = pl.program_id(1)
    rs = jnp.clip(r - WIN_R // 2, 0, rows - WIN_R)
    start = pl.multiple_of(rs * GRID_W, GRID_W)
    band = WIN_R * GRID_W
    for h in range(HEADS):
        sl = slice(h * LANE, (h + 1) * LANE)
        qh = q[:, sl]
        s1 = _nt_dot(qh, k[pl.ds(start, band), sl]) + bias[0, h]
        s2 = _nt_dot(qh, kc[0, :, sl])
        o[:, sl] = _softmax_av([s1, s2], [v[pl.ds(start, band), sl], vc[0, :, sl]]).astype(BF16)


def _nat_attn(qn, kn, vn, kc, vc, bias, db):
    n = qn.shape[0]
    ds = n // db
    rows = ds // GRID_W
    band = WIN_R * GRID_W

    def variant(r):
        return jnp.where(r < WIN_R // 2, r, jnp.where(r > rows - WIN_R // 2, r - (rows - WIN_R), WIN_R // 2))

    qspec = pl.BlockSpec((GRID_W, HW), lambda b, r: (b * rows + r, 0))
    kspec = pl.BlockSpec((ds, HW), lambda b, r: (b, 0))
    cspec = pl.BlockSpec((1, kc.shape[1], HW), lambda b, r: (b, 0, 0))
    bspec = pl.BlockSpec((1, HEADS, GRID_W, band), lambda b, r: (variant(r), 0, 0, 0))
    return pl.pallas_call(
        functools.partial(_nat_kernel, rows=rows),
        grid=(db, rows),
        in_specs=[qspec, kspec, kspec, cspec, cspec, bspec],
        out_specs=qspec,
        out_shape=jax.ShapeDtypeStruct((n, HW), BF16),
        compiler_params=_params(("arbitrary", "arbitrary")),
        name="nat_attn",
    )(qn, kn, vn, kc, vc, bias)


def _split3(x):
    hi = x.astype(BF16)
    r = x - hi.astype(F32)
    mid = r.astype(BF16)
    lo = (r - mid.astype(F32)).astype(BF16)
    return hi, mid, lo


def _pool(p_prev, p_cur, p_next, posb, seq_len):
    rows = p_cur.shape[0]
    halo = p_prev.shape[0]
    ext = rows + 2 * halo
    pext = jnp.concatenate([p_prev, p_cur, p_next], axis=0)
    parts = _split3(pext)
    t = posb + lax.broadcasted_iota(jnp.int32, (rows, ext), 0)
    s = posb - halo + lax.broadcasted_iota(jnp.int32, (rows, ext), 1)
    tcol = posb + lax.broadcasted_iota(jnp.int32, (rows, 1), 0)
    grp = lax.broadcasted_iota(jnp.int32, (rows, 256), 1) // POOL_G
    d = jnp.zeros((rows, 256), F32)
    for gi, w in enumerate(POOL_WINDOWS):
        lo = jnp.maximum(t - w // 2, 0)
        hi = jnp.minimum(t + (w - w // 2), seq_len)
        sel = jnp.where(s >= lo, jnp.where(s < hi, 1.0, 0.0), 0.0).astype(BF16)
        tot = sum(jnp.dot(sel, part, preferred_element_type=F32) for part in parts)
        cnt = (jnp.minimum(tcol + (w - w // 2), seq_len) - jnp.maximum(tcol - w // 2, 0)).astype(F32)
        d = jnp.where(grp == gi, tot / cnt - p_cur, d)
    return d


def _first_max(x, pos, sentinel):
    m = jnp.max(x, axis=0, keepdims=True)
    idx = jnp.min(jnp.where(x == m, pos, sentinel), axis=0, keepdims=True)
    return m, idx


def _topk_head(qh, sk_ref):
    c = qh.shape[0]
    key_pos = lax.broadcasted_iota(jnp.int32, (PEER_NKEYS, c), 0).astype(F32)
    row16 = lax.broadcasted_iota(jnp.int32, (PEER_TOPK, c), 0)
    neg = jnp.float32(-jnp.inf)
    s0 = _nt_dot(sk_ref[0], qh)
    s1 = _nt_dot(sk_ref[1], qh)

    def stage1(a, carry):
        out = []
        for s, sv, si in (carry[0:3], carry[3:6]):
            m, idx = _first_max(s, key_pos, float(PEER_NKEYS))
            out += [jnp.where(key_pos == idx, neg, s),
                    jnp.where(row16 == a, m, sv), jnp.where(row16 == a, idx, si)]
        return tuple(out)

    zf = jnp.zeros((PEER_TOPK, c), F32)
    _, sv0, si0, _, sv1, si1 = lax.fori_loop(0, PEER_TOPK, stage1, (s0, zf, zf, s1, zf, zf))

    sub8 = lax.broadcasted_iota(jnp.int32, (8, c), 0)
    sub8f = sub8.astype(F32)
    cs, ci, cf = [], [], []

    def piece(a_vals, a_ids, a_flat, b_vals, b_ids, b_flat, nb):
        val = a_vals + b_vals
        if nb < 8:
            val = jnp.where(sub8 < nb, val, neg)
        cs.append(val)
        ci.append(a_ids * float(PEER_NKEYS) + b_ids)
        cf.append(jnp.broadcast_to(a_flat * float(PEER_TOPK) + b_flat, (8, c)))

    for a in range(8):
        nb = PEER_TOPK // (a + 1)
        for b0 in range(0, nb, 8):
            piece(sv0[a:a + 1], si0[a:a + 1], float(a), sv1[b0:b0 + 8], si1[b0:b0 + 8],
                  sub8f + float(b0), min(nb - b0, 8))
    piece(sv0[8:16], si0[8:16], sub8f + 8.0, sv1[0:1], si1[0:1], jnp.zeros((8, c), F32), 8)
    npc = len(cs)
    nflat = float(PEER_TOPK * PEER_TOPK)

    def stage2(k, carry):
        vals = list(carry[:npc])
        tv, te = carry[npc], carry[npc + 1]
        m = vals[0]
        for v in vals[1:]:
            m = jnp.maximum(m, v)
        m = jnp.max(m, axis=0, keepdims=True)
        pos = None
        for v, f in zip(vals, cf):
            cand = jnp.where(v == m, f, nflat)
            pos = cand if pos is None else jnp.minimum(pos, cand)
        pos = jnp.min(pos, axis=0, keepdims=True)
        e = None
        for i, f in zip(ci, cf):
            cand = jnp.where(f == pos, i, -1.0)
            e = cand if e is None else jnp.maximum(e, cand)
        e = jnp.max(e, axis=0, keepdims=True)
        vals = [jnp.where(f == pos, neg, v) for v, f in zip(vals, cf)]
        return tuple(vals) + (jnp.where(row16 == k, m, tv), jnp.where(row16 == k, e, te))

    res = lax.fori_loop(0, PEER_TOPK, stage2, tuple(cs) + (zf, zf))
    return res[npc], res[npc + 1]


def _out_kernel(on_ref, om_ref, pc_ref, pp_ref, pn_ref, x_ref, mod_ref,
                won_ref, wop_ref, wom_ref, pw_ref, ps_ref, n2_ref, wq_ref, sk_ref,
                x1_ref, h2_ref, ids_ref, gt_ref, q_scr, idt_scr, *, bps, seq_len):
    d = x_ref.shape[1]
    rows = x_ref.shape[0]
    i = pl.program_id(0)
    mod = mod_ref[0]
    g1 = mod[:, 2 * d:3 * d]
    sh2 = mod[:, 3 * d:4 * d]
    sc2 = mod[:, 4 * d:5 * d]

    posb = (i % bps) * rows
    dpool = _pool(pp_ref[...], pc_ref[...], pn_ref[...], posb, seq_len)
    ypool = jnp.dot(dpool.astype(BF16), pw_ref[...], preferred_element_type=F32) * ps_ref[...]
    mix = (jnp.dot(on_ref[...], won_ref[...], preferred_element_type=F32)
           + jnp.dot(ypool.astype(BF16), wop_ref[...], preferred_element_type=F32)
           + jnp.dot(om_ref[...], wom_ref[...], preferred_element_type=F32))
    x1 = x_ref[...] + g1 * mix
    x1_ref[...] = x1
    h2 = _rms(x1, n2_ref[...]) * (1.0 + sc2) + sh2
    h2_ref[...] = h2

    q = jnp.dot(h2.astype(BF16), wq_ref[...], preferred_element_type=F32)
    for hh in range(PEER_HEADS):
        q_scr[hh] = q[:, hh * LANE:(hh + 1) * LANE].astype(BF16)

    for c0 in range(0, rows, LANE):
        def head(hh, _):
            tv, te = _topk_head(q_scr[hh, c0:c0 + LANE, :], sk_ref)
            ex = jnp.exp(tv - tv[0:1])
            gates = ex / jnp.sum(ex, axis=0, keepdims=True)
            r0 = pl.multiple_of(hh * PEER_TOPK, PEER_TOPK)
            gt_ref[pl.ds(r0, PEER_TOPK), c0:c0 + LANE] = gates
            idt_scr[pl.ds(r0, PEER_TOPK), c0:c0 + LANE] = te
            return 0

        lax.fori_loop(0, PEER_HEADS, head, 0)
    ids_ref[...] = idt_scr[...].T.astype(jnp.int32)


def _out_proj(x, on, om, p, mod_l, row_off, bpm, lw, seq_len):
    n, d = x.shape
    nb = n // TB
    bps = seq_len // TB
    halo = 8
    hb = TB // halo
    tok = lambda w: pl.BlockSpec((TB, w), lambda i: (i, 0))
    in_specs = [tok(HW), tok(HW), tok(256),
                pl.BlockSpec((halo, 256), lambda i: (jnp.maximum(i * hb - 1, 0), 0)),
                pl.BlockSpec((halo, 256), lambda i: (jnp.minimum((i + 1) * hb, n // halo - 1), 0)),
                tok(d),
                pl.BlockSpec((1, 1, mod_l.shape[-1]), lambda i: (row_off + i // bpm, 0, 0)),
                _const_spec((HW, d)), _const_spec((256, d)), _const_spec((HW, d)),
                _const_spec((256, 256)), _const_spec((1, 256)), _const_spec((1, d)),
                _const_spec((d, PEER_HEADS * LANE)), _const_spec((2, PEER_NKEYS, LANE))]
    nk = PEER_HEADS * PEER_TOPK
    return pl.pallas_call(
        functools.partial(_out_kernel, bps=bps, seq_len=seq_len),
        grid=(nb,),
        in_specs=in_specs,
        out_specs=[tok(d), tok(d), tok(nk), pl.BlockSpec((nk, TB), lambda i: (0, i))],
        out_shape=[jax.ShapeDtypeStruct((n, d), F32), jax.ShapeDtypeStruct((n, d), F32),
                   jax.ShapeDtypeStruct((n, nk), jnp.int32), jax.ShapeDtypeStruct((nk, n), F32)],
        scratch_shapes=[pltpu.VMEM((PEER_HEADS, TB, LANE), BF16), pltpu.VMEM((nk, TB), F32)],
        compiler_params=_params(("arbitrary",)),
        name="out_proj",
    )(on, om, p, p, p, x, mod_l, lw["w_o_na"], lw["w_o_pool"], lw["w_o_mla"],
      lw["pool_w"], lw["pool_scale"], lw["norm2"], lw["peer_wq"], lw["peer_sk"])


def _gelu_tanh(x):
    return x * (0.5 * (1.0 + jnp.tanh(0.7978845608028654 * (x + 0.044715 * (x * x * x)))))


def _peer_kernel(ids_hbm, gt_ref, h2_ref, x1_ref, mod_ref, tab_hbm, o_ref,
                 ids_s, buf, sem_i, sem_r):
    d = x1_ref.shape[1]
    ch = d // LANE
    pitch = ch + 1
    nsub = x1_ref.shape[0] // PEER_SUB
    nk = gt_ref.shape[0]
    nids = PEER_SUB * nk
    i = pl.program_id(0)
    g2 = mod_ref[0][:, 5 * d:6 * d]
    tok_lane = lax.broadcasted_iota(jnp.int32, gt_ref.shape, 1)

    def ids_copy(j, slot):
        start = pl.multiple_of((i * nsub + j) * nids, nids)
        return pltpu.make_async_copy(ids_hbm.at[pl.ds(start, nids)],
                                     ids_s.at[pl.ds(slot * nids, nids)], sem_i.at[slot])

    def row_copy(slot, e, f):
        src = tab_hbm.at[pl.ds(pl.multiple_of(e * ch, ch), ch), :]
        dst = buf.at[slot, pl.ds(f * pitch, ch), :]
        return pltpu.make_async_copy(src, dst, sem_r.at[slot])

    def issue_rows(slot):
        for t in range(PEER_SUB):
            def body(kk, _):
                for r in range(8):
                    f = t * nk + kk * 8 + r
                    row_copy(slot, ids_s[slot * nids + f], f).start(priority=r % 2)
                return 0

            lax.fori_loop(0, nk // 8, body, 0)

    def wait_rows(slot):
        done = buf.at[slot, pl.ds(0, nids * ch), :]
        pltpu.make_async_copy(done, done, sem_r.at[slot]).wait()

    def compute(slot, j):
        base = pl.multiple_of(j * PEER_SUB, PEER_SUB)
        h8 = h2_ref[pl.ds(base, PEER_SUB), :]
        ys = []
        for t in range(PEER_SUB):
            chunk = lambda s: buf[slot, pl.ds(t * nk * pitch + s, nk, stride=pitch), :]
            acc = None
            for s in range(ch):
                us = lax.bitcast_convert_type(chunk(s) & jnp.int32(-65536), F32)
                term = us * h8[t:t + 1, s * LANE:(s + 1) * LANE]
                acc = term if acc is None else acc + term
            sc = jnp.sum(acc, axis=-1, keepdims=True)
            gcol = jnp.sum(jnp.where(tok_lane == base + t, gt_ref[...], 0.0), axis=-1, keepdims=True)
            wgt = gcol * _gelu_tanh(sc)
            parts = []
            for s in range(ch):
                vs = lax.bitcast_convert_type(chunk(s) << 16, F32)
                parts.append(jnp.sum(vs * wgt, axis=0, keepdims=True))
            ys.append(jnp.concatenate(parts, axis=-1))
        y8 = jnp.concatenate(ys, axis=0)
        o_ref[pl.ds(base, PEER_SUB), :] = x1_ref[pl.ds(base, PEER_SUB), :] + g2 * y8

    first = ids_copy(0, 0)
    first.start()
    first.wait()
    issue_rows(0)
    ids_copy(1, 1).start()

    def pair(jj, _):
        j0 = 2 * jj
        ids_copy(j0 + 1, 1).wait()
        issue_rows(1)

        @pl.when(j0 + 2 < nsub)
        def _():
            ids_copy(j0 + 2, 0).start()

        wait_rows(0)
        compute(0, j0)

        @pl.when(j0 + 2 < nsub)
        def _():
            ids_copy(j0 + 2, 0).wait()
            issue_rows(0)

        @pl.when(j0 + 3 < nsub)
        def _():
            ids_copy(j0 + 3, 1).start()

        wait_rows(1)
        compute(1, j0 + 1)
        return 0

    lax.fori_loop(0, nsub // 2, pair, 0)


def _pack_tables(peer_u, peer_v):
    e, d = peer_u.shape
    ub = lax.bitcast_convert_type(peer_u.astype(BF16), jnp.uint16).astype(jnp.uint32)
    vb = lax.bitcast_convert_type(peer_v.astype(BF16), jnp.uint16).astype(jnp.uint32)
    words = lax.bitcast_convert_type((ub << 16) | vb, jnp.int32)
    return words.reshape(e * (d // LANE), LANE)


def _peer(x1, h2, ids, gt, mod_l, row_off, bpm, table):
    n, d = x1.shape
    nk = gt.shape[0]
    nb = n // PEER_TB
    tok = pl.BlockSpec((PEER_TB, d), lambda i: (i, 0))
    any_spec = pl.BlockSpec(memory_space=pl.ANY)
    return pl.pallas_call(
        _peer_kernel,
        grid=(nb,),
        in_specs=[any_spec,
                  pl.BlockSpec((nk, PEER_TB), lambda i: (0, i)),
                  tok, tok,
                  pl.BlockSpec((1, 1, mod_l.shape[-1]), lambda i: (row_off + i // bpm, 0, 0)),
                  any_spec],
        out_specs=tok,
        out_shape=jax.ShapeDtypeStruct((n, d), F32),
        scratch_shapes=[pltpu.SMEM((2 * PEER_SUB * nk,), jnp.int32),
                        pltpu.VMEM((2, PEER_SUB * nk * (d // LANE + 1), LANE), jnp.int32),
                        pltpu.SemaphoreType.DMA((2,)),
                        pltpu.SemaphoreType.DMA((2,))],
        compiler_params=_params(("arbitrary",)),
        name="peer",
    )(ids.reshape(n * nk), gt, h2, x1, mod_l, table)


def _pad_heads(w, width):
    pad = [(0, 0)] * (w.ndim - 1) + [(0, LANE - width)]
    w = jnp.pad(w, pad)
    return w.reshape(w.shape[:-2] + (HW,))


def _head_gain(g, width):
    depth = g.shape[0]
    g = jnp.pad(g, ((0, 0), (0, LANE - width)))
    return jnp.tile(g, (1, HEADS)).reshape(depth, 1, HW)


def _rope_tables(seq):
    t = np.arange(seq)
    half = MLA_ROPE // 2
    inv = ROPE_THETA ** (-np.arange(0, half, 2, dtype=np.float32) / half)
    cos = np.ones((seq, LANE), np.float32)
    sin = np.zeros((seq, LANE), np.float32)
    for off, pos in ((MLA_NOPE, t // GRID_W), (MLA_NOPE + half, t % GRID_W)):
        ang = pos.astype(np.float32)[:, None] * inv[None, :]
        q = half // 2
        cos[:, off:off + q] = np.cos(ang)
        cos[:, off + q:off + half] = np.cos(ang)
        sin[:, off:off + q] = -np.sin(ang)
        sin[:, off + q:off + half] = np.sin(ang)
    return jnp.asarray(cos), jnp.asarray(sin)


def _nat_bias(rel_bias):
    v = np.arange(WIN_R)[:, None]
    j = np.arange(WIN_R)[None, :]
    dr = j - v + WIN_R - 1
    cq = np.arange(GRID_W)[:, None]
    kc = np.arange(GRID_W)[None, :]
    cstart = np.clip(cq - WIN_C // 2, 0, GRID_W - WIN_C)
    ok = (kc >= cstart) & (kc < cstart + WIN_C)
    dc = np.clip(kc - cq + WIN_C - 1, 0, 2 * WIN_C - 2)
    b = rel_bias[:, :, dr]
    b = b[..., dc]
    b = jnp.where(jnp.asarray(ok)[None, None, None, None], b, NEG_INF)
    b = jnp.transpose(b, (0, 2, 1, 4, 3, 5))
    return b.reshape(b.shape[0], WIN_R, HEADS, GRID_W, WIN_R * GRID_W)


def _layer_weights(w_in, na_q_norm, na_k_norm, mla_cq_norm, mla_ckv_norm, mla_w_uq, mla_w_ukv,
                   mla_q_norm, mla_k_norm, w_out, pool_w, pool_scale, norm1, norm2,
                   peer_wq, peer_subkeys):
    depth, d, _ = w_in.shape
    na_w = HEADS * NA_DH
    segs = np.cumsum([0, na_w, na_w, na_w, 256, 256, 128, MLA_ROPE])
    part = lambda i: w_in[:, :, segs[i]:segs[i + 1]]
    heads = lambda w: _pad_heads(w.reshape(depth, d, HEADS, NA_DH), NA_DH)
    w_in_p = jnp.concatenate(
        [heads(part(0)), heads(part(1)), heads(part(2)), part(3), part(4), part(5),
         jnp.pad(part(6), ((0, 0), (0, 0), (0, LANE - MLA_ROPE)))], axis=-1).astype(BF16)

    w_uq = _pad_heads(mla_w_uq, MLA_QK).astype(BF16)
    k_nope = _pad_heads(mla_w_ukv[..., :MLA_NOPE], MLA_NOPE)
    eye = np.zeros((MLA_ROPE, HEADS, LANE), np.float32)
    for h in range(HEADS):
        eye[np.arange(MLA_ROPE), h, MLA_NOPE + np.arange(MLA_ROPE)] = 1.0
    eye = jnp.broadcast_to(jnp.asarray(eye.reshape(MLA_ROPE, HW)), (depth, MLA_ROPE, HW))
    zer = jnp.zeros((depth, 256 - 128 - MLA_ROPE, HW), F32)
    w_k = jnp.concatenate([k_nope, eye, zer], axis=1).astype(BF16)
    w_v = jnp.concatenate([_pad_heads(mla_w_ukv[..., MLA_NOPE:], MLA_V),
                           jnp.zeros((depth, 128, HW), F32)], axis=1).astype(BF16)

    mix_w = HEADS * NA_DH
    w_o_na = jnp.pad(w_out[:, :mix_w].reshape(depth, HEADS, NA_DH, d),
                     ((0, 0), (0, 0), (0, LANE - NA_DH), (0, 0))).reshape(depth, HW, d).astype(BF16)
    w_o_pool = w_out[:, mix_w:mix_w + 256].astype(BF16)
    w_o_mla = jnp.pad(w_out[:, mix_w + 256:].reshape(depth, HEADS, MLA_V, d),
                      ((0, 0), (0, 0), (0, LANE - MLA_V), (0, 0))).reshape(depth, HW, d).astype(BF16)
    ng = len(POOL_WINDOWS)
    pw = jnp.zeros((depth, ng * POOL_G, ng * POOL_G), F32)
    for g in range(ng):
        pw = pw.at[:, g * POOL_G:(g + 1) * POOL_G, g * POOL_G:(g + 1) * POOL_G].set(pool_w[:, g])

    half = peer_subkeys.shape[-1]
    sk = jnp.stack([jnp.pad(peer_subkeys[:, 0], ((0, 0), (0, 0), (0, LANE - half))),
                    jnp.pad(peer_subkeys[:, 1], ((0, 0), (0, 0), (LANE - half, 0)))], axis=1).astype(BF16)

    return dict(
        w_in=w_in_p, w_uq=w_uq, w_k=w_k, w_v=w_v,
        g_q=_head_gain(na_q_norm, NA_DH), g_k=_head_gain(na_k_norm, NA_DH),
        g_cq=mla_cq_norm[:, None, :], g_ckv=mla_ckv_norm[:, None, :],
        g_qm=_head_gain(mla_q_norm, MLA_QK), g_km=_head_gain(mla_k_norm, MLA_QK),
        w_o_na=w_o_na, w_o_pool=w_o_pool, w_o_mla=w_o_mla,
        pool_w=pw.astype(BF16), pool_scale=pool_scale[:, None, :],
        norm1=norm1[:, None, :], norm2=norm2[:, None, :],
        peer_wq=peer_wq.astype(BF16), peer_sk=sk)


def kernel(x_prompt, x_sample, c, cache_nat_k, cache_nat_v, cache_mla_ckv, cache_mla_krope, c_ctx, w_mod, b_mod, norm1, norm2, w_in, na_q_norm, na_k_norm, na_rel_bias, pool_w, pool_scale, mla_cq_norm, mla_ckv_norm, mla_w_uq, mla_w_ukv, mla_q_norm, mla_k_norm, w_out, peer_wq, peer_subkeys, peer_u, peer_v):
    batch, seq, d = x_prompt.shape
    db, ds, _ = x_sample.shape
    depth = w_mod.shape[0]
    past = cache_nat_k.shape[2]
    assert seq == TB and ds % TB == 0 and ds % (GRID_W * WIN_R) == 0 and db + 1 <= 8

    cond8 = jnp.concatenate([c_ctx[None, :], c, jnp.zeros((8 - 1 - db, d), F32)], axis=0)
    mod = _modulation(cond8, w_mod, b_mod).reshape(depth, 8, 1, 6 * d)

    lw_all = _layer_weights(w_in, na_q_norm, na_k_norm, mla_cq_norm, mla_ckv_norm, mla_w_uq,
                            mla_w_ukv, mla_q_norm, mla_k_norm, w_out, pool_w, pool_scale,
                            norm1, norm2, peer_wq, peer_subkeys)
    bias_all = _nat_bias(na_rel_bias)
    tables = [_pack_tables(peer_u[l], peer_v[l]) for l in range(depth)]
    cos_lat, sin_lat = _rope_tables(ds)
    cos_ctx = jnp.ones((TB, LANE), F32)
    sin_ctx = jnp.zeros((TB, LANE), F32)

    ck = jnp.concatenate([cache_mla_ckv, cache_mla_krope,
                          jnp.zeros(cache_mla_ckv.shape[:-1] + (256 - 128 - MLA_ROPE,), F32)],
                         axis=-1).astype(BF16)
    kc_mla, vc_mla = _cache_kv(ck, lw_all["w_k"], lw_all["w_v"], lw_all["g_km"])
    kc_na = _pad_heads(cache_nat_k, NA_DH).astype(BF16)
    vc_na = _pad_heads(cache_nat_v, NA_DH).astype(BF16)

    xc = x_prompt.reshape(batch * seq, d)
    xl = x_sample.reshape(db * ds, d)
    lat_bpm = ds // TB
    ctx_bpm = batch * seq // TB + 1
    ks, vs, ckvs, krs = [], [], [], []
    for l in range(depth):
        lw = {k: v[l] for k, v in lw_all.items()}
        mod_l = mod[l]

        (qn, kn, vn, knf, vnf, p, qm, km, vm, ckv, kr) = _in_proj(
            xc, mod_l, 0, ctx_bpm, lw, cos_ctx, sin_ctx, 1)
        on, om = _ctx_attn(qn, kn, vn, qm, km, vm, seq)
        x1, h2, ids, gt = _out_proj(xc, on, om, p, mod_l, 0, ctx_bpm, lw, seq)
        xc = _peer(x1, h2, ids, gt, mod_l, 0, batch * seq // PEER_TB + 1, tables[l])
        ks.append(knf.reshape(batch, seq, HEADS, LANE)[..., :NA_DH])
        vs.append(vnf.reshape(batch, seq, HEADS, LANE)[..., :NA_DH])
        ckvs.append(ckv.reshape(batch, seq, 128))
        krs.append(kr.reshape(batch, seq, LANE)[..., :MLA_ROPE])

        (qn, kn, vn, _, _, p, qm, km, vm, _, _) = _in_proj(
            xl, mod_l, 1, lat_bpm, lw, cos_lat, sin_lat, lat_bpm)
        on = _nat_attn(qn, kn, vn, kc_na[:, l], vc_na[:, l], bias_all[l], db)
        om = _lat_mla(qm, km, vm, kc_mla[:, l], vc_mla[:, l], db)
        x1, h2, ids, gt = _out_proj(xl, on, om, p, mod_l, 1, lat_bpm, lw, ds)
        xl = _peer(x1, h2, ids, gt, mod_l, 1, ds // PEER_TB, tables[l])

    return (xc.reshape(batch, seq, d), xl.reshape(db, ds, d),
            jnp.stack(ks, axis=1), jnp.stack(vs, axis=1),
            jnp.stack(ckvs, axis=1), jnp.stack(krs, axis=1))
```

```python
import functools

import numpy as np
import jax
import jax.numpy as jnp
from jax import lax
from jax.experimental import pallas as pl
from jax.experimental.pallas import tpu as pltpu
from jax.experimental.pallas import tpu_sc as plsc

F32 = jnp.float32
BF16 = jnp.bfloat16

EPS = 1e-6
ROPE_THETA = 10000.0
NEG_INF = -1e30
GRID_W = 64
HEADS = 6
NA_DH = 64
WIN_R = 8
WIN_C = 16
POOL_WINDOWS = (2, 4, 8, 16)
POOL_G = 64
MLA_NOPE = 64
MLA_ROPE = 32
MLA_QK = MLA_NOPE + MLA_ROPE
MLA_V = 64
PEER_HEADS = 8
PEER_NKEYS = 128
PEER_TOPK = 16
LANE = 128
HW = HEADS * LANE
TB = 256
TQ = 256
PEER_TB = 128
PEER_SUB = 8
VMEM_LIMIT = 56 * 1024 * 1024

_CQ, _CK, _CV = 0, HW, 2 * HW
_CP = 3 * HW
_CCQ = _CP + 256
_CCKV = _CCQ + 256
_CKR = _CCKV + 128
IN_W = _CKR + 128


def _params(sem, vmem=VMEM_LIMIT):
    return pltpu.CompilerParams(dimension_semantics=sem, vmem_limit_bytes=vmem)


def _const_spec(shape):
    n = len(shape)
    return pl.BlockSpec(shape, lambda *_: (0,) * n)


def _nt_dot(a, b):
    return lax.dot_general(a, b, (((1,), (1,)), ((), ())), preferred_element_type=F32)


def _mod_kernel(c_ref, w_ref, b_ref, o_ref):
    c = c_ref[...]
    s = c / (1.0 + jnp.exp(-c))
    o_ref[0] = jnp.dot(s, w_ref[0], preferred_element_type=F32,
                       precision=lax.Precision.HIGHEST) + b_ref[0]


def _modulation(cond8, w_mod, b_mod):
    depth, d, n6 = w_mod.shape
    tn = n6 // 4
    return pl.pallas_call(
        _mod_kernel,
        grid=(depth, n6 // tn),
        in_specs=[_const_spec((8, d)),
                  pl.BlockSpec((1, d, tn), lambda l, j: (l, 0, j)),
                  pl.BlockSpec((1, 1, tn), lambda l, j: (l, 0, j))],
        out_specs=pl.BlockSpec((1, 8, tn), lambda l, j: (l, 0, j)),
        out_shape=jax.ShapeDtypeStruct((depth, 8, n6), F32),
        compiler_params=_params(("arbitrary", "arbitrary")),
        name="modulation",
    )(cond8, w_mod, b_mod.reshape(depth, 1, n6))


def _rms(z, gain):
    return z * lax.rsqrt(jnp.mean(z * z, axis=-1, keepdims=True) + EPS) * gain


def _head_rms(zh, gain_h, n_real):
    ms = jnp.sum(zh * zh, axis=-1, keepdims=True) * (1.0 / n_real)
    return zh * lax.rsqrt(ms + EPS) * gain_h


def _rope(zh, cos, sin, is_x1):
    rot = jnp.where(is_x1, pltpu.roll(zh, LANE - 8, 1), pltpu.roll(zh, 8, 1))
    return zh * cos + rot * sin


def _is_x1(rows):
    lane = lax.broadcasted_iota(jnp.int32, (rows, LANE), 1)
    first = jnp.where(lane >= MLA_NOPE, jnp.where(lane < MLA_NOPE + 8, 1, 0), 0)
    second = jnp.where(lane >= MLA_NOPE + 16, jnp.where(lane < MLA_NOPE + 24, 1, 0), 0)
    return (first + second) > 0


def _mla_kv(ck, wk_ref, wv_ref, gk_ref, cos, sin, km_ref, vm_ref):
    rows = ck.shape[0]
    kk = jnp.dot(ck, wk_ref[...], preferred_element_type=F32)
    is_x1 = _is_x1(rows)
    for h in range(HEADS):
        sl = slice(h * LANE, (h + 1) * LANE)
        kh = _head_rms(kk[:, sl], gk_ref[:, sl], MLA_QK)
        km_ref[:, sl] = _rope(kh, cos, sin, is_x1).astype(BF16)
    vm_ref[...] = jnp.dot(ck, wv_ref[...], preferred_element_type=F32).astype(BF16)


def _in_kernel(x_ref, mod_ref, n1_ref, w_ref, wuq_ref, wk_ref, wv_ref,
               gq_ref, gk_ref, gcq_ref, gckv_ref, gqm_ref, gkm_ref, cos_ref, sin_ref,
               qn_ref, kn_ref, vn_ref, knf_ref, vnf_ref, p_ref,
               qm_ref, km_ref, vm_ref, ckv_ref, kr_ref):
    d = x_ref.shape[1]
    rows = x_ref.shape[0]
    mod = mod_ref[0]
    sh1 = mod[:, 0:d]
    sc1 = mod[:, d:2 * d]
    h = _rms(x_ref[...], n1_ref[...]) * (1.0 + sc1) + sh1
    hb = h.astype(BF16)

    def proj(lo, hi):
        return jnp.dot(hb, w_ref[:, lo:hi], preferred_element_type=F32)

    cos = cos_ref[...]
    sin = sin_ref[...]
    is_x1 = _is_x1(rows)

    zq = proj(_CQ, _CQ + HW)
    zk = proj(_CK, _CK + HW)
    for hh in range(HEADS):
        sl = slice(hh * LANE, (hh + 1) * LANE)
        qn_ref[:, sl] = (_head_rms(zq[:, sl], gq_ref[:, sl], NA_DH) * (NA_DH ** -0.5)).astype(BF16)
        kh = _head_rms(zk[:, sl], gk_ref[:, sl], NA_DH)
        knf_ref[:, sl] = kh
        kn_ref[:, sl] = kh.astype(BF16)
    zv = proj(_CV, _CV + HW)
    vnf_ref[...] = zv
    vn_ref[...] = zv.astype(BF16)
    p_ref[...] = proj(_CP, _CP + 256)

    cq = _rms(proj(_CCQ, _CCQ + 256), gcq_ref[...])
    zqm = jnp.dot(cq.astype(BF16), wuq_ref[...], preferred_element_type=F32)
    for hh in range(HEADS):
        sl = slice(hh * LANE, (hh + 1) * LANE)
        qh = _head_rms(zqm[:, sl], gqm_ref[:, sl], MLA_QK)
        qm_ref[:, sl] = (_rope(qh, cos, sin, is_x1) * (MLA_QK ** -0.5)).astype(BF16)

    ckv = _rms(proj(_CCKV, _CCKV + 128), gckv_ref[...])
    kr = proj(_CKR, _CKR + 128)
    ckv_ref[...] = ckv
    kr_ref[...] = kr
    ck = jnp.concatenate([ckv, kr], axis=-1).astype(BF16)
    _mla_kv(ck, wk_ref, wv_ref, gkm_ref, cos, sin, km_ref, vm_ref)


def _in_proj(x, mod_l, row_off, bpm, lw, cos_t, sin_t, rope_blocks):
    n, d = x.shape
    nb = n // TB
    tok = lambda w: pl.BlockSpec((TB, w), lambda i: (i, 0))
    rope_spec = pl.BlockSpec((TB, LANE), lambda i: (i % rope_blocks, 0))
    in_specs = [tok(d),
                pl.BlockSpec((1, 1, mod_l.shape[-1]), lambda i: (row_off + i // bpm, 0, 0)),
                _const_spec((1, d)), _const_spec((d, IN_W)), _const_spec((256, HW)),
                _const_spec((256, HW)), _const_spec((256, HW)),
                _const_spec((1, HW)), _const_spec((1, HW)), _const_spec((1, 256)),
                _const_spec((1, 128)), _const_spec((1, HW)), _const_spec((1, HW)),
                rope_spec, rope_spec]
    widths = [(HW, BF16), (HW, BF16), (HW, BF16), (HW, F32), (HW, F32), (256, F32),
              (HW, BF16), (HW, BF16), (HW, BF16), (128, F32), (128, F32)]
    return pl.pallas_call(
        _in_kernel,
        grid=(nb,),
        in_specs=in_specs,
        out_specs=[tok(w) for w, _ in widths],
        out_shape=[jax.ShapeDtypeStruct((n, w), dt) for w, dt in widths],
        compiler_params=_params(("arbitrary",)),
        name="in_proj",
    )(x, mod_l, lw["norm1"], lw["w_in"], lw["w_uq"], lw["w_k"], lw["w_v"],
      lw["g_q"], lw["g_k"], lw["g_cq"], lw["g_ckv"], lw["g_qm"], lw["g_km"], cos_t, sin_t)


def _cache_kernel(ck_ref, wk_ref, wv_ref, gk_ref, km_ref, vm_ref):
    rows = ck_ref.shape[2]
    cos = jnp.ones((rows, LANE), F32)
    sin = jnp.zeros((rows, LANE), F32)
    _mla_kv(ck_ref[0, 0], wk_ref.at[0], wv_ref.at[0], gk_ref.at[0], cos, sin,
            km_ref.at[0, 0], vm_ref.at[0, 0])


def _cache_kv(ck, w_k, w_v, g_km):
    db, depth, p, _ = ck.shape
    spec = lambda w: pl.BlockSpec((1, 1, p, w), lambda b, l: (b, l, 0, 0))
    wspec = lambda r: pl.BlockSpec((1, r, HW), lambda b, l: (l, 0, 0))
    return pl.pallas_call(
        _cache_kernel,
        grid=(db, depth),
        in_specs=[spec(256), wspec(256), wspec(256), wspec(1)],
        out_specs=[spec(HW), spec(HW)],
        out_shape=[jax.ShapeDtypeStruct((db, depth, p, HW), BF16)] * 2,
        compiler_params=_params(("arbitrary", "arbitrary")),
        name="cache_kv",
    )(ck, w_k, w_v, g_km)


def _softmax_av(s_list, v_list):
    m = s_list[0].max(axis=-1, keepdims=True)
    for s in s_list[1:]:
        m = jnp.maximum(m, s.max(axis=-1, keepdims=True))
    acc = None
    den = None
    for s, v in zip(s_list, v_list):
        p = jnp.exp(s - m)
        l = p.sum(axis=-1, keepdims=True)
        o = jnp.dot(p.astype(BF16), v, preferred_element_type=F32)
        acc = o if acc is None else acc + o
        den = l if den is None else den + l
    return acc / den


def _ctx_attn_kernel(qn, kn, vn, qm, km, vm, on, om):
    for q, k, v, o in ((qn, kn, vn, on), (qm, km, vm, om)):
        for h in range(HEADS):
            sl = slice(h * LANE, (h + 1) * LANE)
            s = _nt_dot(q[:, sl], k[:, sl])
            o[:, sl] = _softmax_av([s], [v[:, sl]]).astype(BF16)


def _ctx_attn(qn, kn, vn, qm, km, vm, seq):
    n = qn.shape[0]
    spec = pl.BlockSpec((seq, HW), lambda i: (i, 0))
    return pl.pallas_call(
        _ctx_attn_kernel,
        grid=(n // seq,),
        in_specs=[spec] * 6,
        out_specs=[spec] * 2,
        out_shape=[jax.ShapeDtypeStruct((n, HW), BF16)] * 2,
        compiler_params=_params(("arbitrary",)),
        name="ctx_attn",
    )(qn, kn, vn, qm, km, vm)


def _lat_mla_kernel(q, k, v, kc, vc, o):
    s1 = _nt_dot(q[...], k[...])
    s2 = _nt_dot(q[...], kc[0])
    o[...] = _softmax_av([s1, s2], [v[...], vc[0]]).astype(BF16)


def _lat_mla(qm, km, vm, kc, vc, db):
    n = qm.shape[0]
    ds = n // db
    nq = ds // TQ
    qspec = pl.BlockSpec((TQ, LANE), lambda b, h, i: (b * nq + i, h))
    kspec = pl.BlockSpec((ds, LANE), lambda b, h, i: (b, h))
    cspec = pl.BlockSpec((1, kc.shape[1], LANE), lambda b, h, i: (b, 0, h))
    return pl.pallas_call(
        _lat_mla_kernel,
        grid=(db, HEADS, nq),
        in_specs=[qspec, kspec, kspec, cspec, cspec],
        out_specs=qspec,
        out_shape=jax.ShapeDtypeStruct((n, HW), BF16),
        compiler_params=_params(("arbitrary",) * 3),
        name="lat_mla",
    )(qm, km, vm, kc, vc)


def _nat_kernel(q, k, v, kc, vc, bias, o, *, rows):
    r = pl.program_id(1)
    rs = jnp.clip(r - WIN_R // 2, 0, rows - WIN_R)
    start = pl.multiple_of(rs * GRID_W, GRID_W)
    band = WIN_R * GRID_W
    for h in range(HEADS):
        sl = slice(h * LANE, (h + 1) * LANE)
        qh = q[:, sl]
        s1 = _nt_dot(qh, k[pl.ds(start, band), sl]) + bias[0, h]
        s2 = _nt_dot(qh, kc[0, :, sl])
        o[:, sl] = _softmax_av([s1, s2], [v[pl.ds(start, band), sl], vc[0, :, sl]]).astype(BF16)


def _nat_attn(qn, kn, vn, kc, vc, bias, db):
    n = qn.shape[0]
    ds = n // db
    rows = ds // GRID_W
    band = WIN_R * GRID_W

    def variant(r):
        return jnp.where(r < WIN_R // 2, r, jnp.where(r > rows - WIN_R // 2, r - (rows - WIN_R), WIN_R // 2))

    qspec = pl.BlockSpec((GRID_W, HW), lambda b, r: (b * rows + r, 0))
    kspec = pl.BlockSpec((ds, HW), lambda b, r: (b, 0))
    cspec = pl.BlockSpec((1, kc.shape[1], HW), lambda b, r: (b, 0, 0))
    bspec = pl.BlockSpec((1, HEADS, GRID_W, band), lambda b, r: (variant(r), 0, 0, 0))
    return pl.pallas_call(
        functools.partial(_nat_kernel, rows=rows),
        grid=(db, rows),
        in_specs=[qspec, kspec, kspec, cspec, cspec, bspec],
        out_specs=qspec,
        out_shape=jax.ShapeDtypeStruct((n, HW), BF16),
        compiler_params=_params(("arbitrary", "arbitrary")),
        name="nat_attn",
    )(qn, kn, vn, kc, vc, bias)


def _split3(x):
    hi = x.astype(BF16)
    r = x - hi.astype(F32)
    mid = r.astype(BF16)
    lo = (r - mid.astype(F32)).astype(BF16)
    return hi, mid, lo


def _pool(p_prev, p_cur, p_next, posb, seq_len):
    rows = p_cur.shape[0]
    halo = p_prev.shape[0]
    ext = rows + 2 * halo
    pext = jnp.concatenate([p_prev, p_cur, p_next], axis=0)
    parts = _split3(pext)
    t = posb + lax.broadcasted_iota(jnp.int32, (rows, ext), 0)
    s = posb - halo + lax.broadcasted_iota(jnp.int32, (rows, ext), 1)
    tcol = posb + lax.broadcasted_iota(jnp.int32, (rows, 1), 0)
    grp = lax.broadcasted_iota(jnp.int32, (rows, 256), 1) // POOL_G
    d = jnp.zeros((rows, 256), F32)
    for gi, w in enumerate(POOL_WINDOWS):
        lo = jnp.maximum(t - w // 2, 0)
        hi = jnp.minimum(t + (w - w // 2), seq_len)
        sel = jnp.where(s >= lo, jnp.where(s < hi, 1.0, 0.0), 0.0).astype(BF16)
        tot = sum(jnp.dot(sel, part, preferred_element_type=F32) for part in parts)
        cnt = (jnp.minimum(tcol + (w - w // 2), seq_len) - jnp.maximum(tcol - w // 2, 0)).astype(F32)
        d = jnp.where(grp == gi, tot / cnt - p_cur, d)
    return d


def _first_max(x, pos, sentinel):
    m = jnp.max(x, axis=0, keepdims=True)
    idx = jnp.min(jnp.where(x == m, pos, sentinel), axis=0, keepdims=True)
    return m, idx


def _topk_head(qh, sk_ref):
    c = qh.shape[0]
    key_pos = lax.broadcasted_iota(jnp.int32, (PEER_NKEYS, c), 0).astype(F32)
    row16 = lax.broadcasted_iota(jnp.int32, (PEER_TOPK, c), 0)
    neg = jnp.float32(-jnp.inf)
    s0 = _nt_dot(sk_ref[0], qh)
    s1 = _nt_dot(sk_ref[1], qh)

    def stage1(a, carry):
        out = []
        for s, sv, si in (carry[0:3], carry[3:6]):
            m, idx = _first_max(s, key_pos, float(PEER_NKEYS))
            out += [jnp.where(key_pos == idx, neg, s),
                    jnp.where(row16 == a, m, sv), jnp.where(row16 == a, idx, si)]
        return tuple(out)

    zf = jnp.zeros((PEER_TOPK, c), F32)
    _, sv0, si0, _, sv1, si1 = lax.fori_loop(0, PEER_TOPK, stage1, (s0, zf, zf, s1, zf, zf))

    sub8 = lax.broadcasted_iota(jnp.int32, (8, c), 0)
    sub8f = sub8.astype(F32)
    cs, ci, cf = [], [], []

    def piece(a_vals, a_ids, a_flat, b_vals, b_ids, b_flat, nb):
        val = a_vals + b_vals
        if nb < 8:
            val = jnp.where(sub8 < nb, val, neg)
        cs.append(val)
        ci.append(a_ids * float(PEER_NKEYS) + b_ids)
        cf.append(jnp.broadcast_to(a_flat * float(PEER_TOPK) + b_flat, (8, c)))

    for a in range(8):
        nb = PEER_TOPK // (a + 1)
        for b0 in range(0, nb, 8):
            piece(sv0[a:a + 1], si0[a:a + 1], float(a), sv1[b0:b0 + 8], si1[b0:b0 + 8],
                  sub8f + float(b0), min(nb - b0, 8))
    piece(sv0[8:16], si0[8:16], sub8f + 8.0, sv1[0:1], si1[0:1], jnp.zeros((8, c), F32), 8)
    npc = len(cs)
    nflat = float(PEER_TOPK * PEER_TOPK)

    def stage2(k, carry):
        vals = list(carry[:npc])
        tv, te = carry[npc], carry[npc + 1]
        m = vals[0]
        for v in vals[1:]:
            m = jnp.maximum(m, v)
        m = jnp.max(m, axis=0, keepdims=True)
        pos = None
        for v, f in zip(vals, cf):
            cand = jnp.where(v == m, f, nflat)
            pos = cand if pos is None else jnp.minimum(pos, cand)
        pos = jnp.min(pos, axis=0, keepdims=True)
        e = None
        for i, f in zip(ci, cf):
            cand = jnp.where(f == pos, i, -1.0)
            e = cand if e is None else jnp.maximum(e, cand)
        e = jnp.max(e, axis=0, keepdims=True)
        vals = [jnp.where(f == pos, neg, v) for v, f in zip(vals, cf)]
        return tuple(vals) + (jnp.where(row16 == k, m, tv), jnp.where(row16 == k, e, te))

    res = lax.fori_loop(0, PEER_TOPK, stage2, tuple(cs) + (zf, zf))
    return res[npc], res[npc + 1]


def _out_kernel(on_ref, om_ref, pc_ref, pp_ref, pn_ref, x_ref, mod_ref,
                won_ref, wop_ref, wom_ref, pw_ref, ps_ref, n2_ref, wq_ref, sk_ref,
                x1_ref, h2_ref, ids_ref, gt_ref, q_scr, idt_scr, *, bps, seq_len):
    d = x_ref.shape[1]
    rows = x_ref.shape[0]
    i = pl.program_id(0)
    mod = mod_ref[0]
    g1 = mod[:, 2 * d:3 * d]
    sh2 = mod[:, 3 * d:4 * d]
    sc2 = mod[:, 4 * d:5 * d]

    posb = (i % bps) * rows
    dpool = _pool(pp_ref[...], pc_ref[...], pn_ref[...], posb, seq_len)
    ypool = jnp.dot(dpool.astype(BF16), pw_ref[...], preferred_element_type=F32) * ps_ref[...]
    mix = (jnp.dot(on_ref[...], won_ref[...], preferred_element_type=F32)
           + jnp.dot(ypool.astype(BF16), wop_ref[...], preferred_element_type=F32)
           + jnp.dot(om_ref[...], wom_ref[...], preferred_element_type=F32))
    x1 = x_ref[...] + g1 * mix
    x1_ref[...] = x1
    h2 = _rms(x1, n2_ref[...]) * (1.0 + sc2) + sh2
    h2_ref[...] = h2

    q = jnp.dot(h2.astype(BF16), wq_ref[...], preferred_element_type=F32)
    for hh in range(PEER_HEADS):
        q_scr[hh] = q[:, hh * LANE:(hh + 1) * LANE].astype(BF16)

    for c0 in range(0, rows, LANE):
        def head(hh, _):
            tv, te = _topk_head(q_scr[hh, c0:c0 + LANE, :], sk_ref)
            ex = jnp.exp(tv - tv[0:1])
            gates = ex / jnp.sum(ex, axis=0, keepdims=True)
            r0 = pl.multiple_of(hh * PEER_TOPK, PEER_TOPK)
            gt_ref[pl.ds(r0, PEER_TOPK), c0:c0 + LANE] = gates
            idt_scr[pl.ds(r0, PEER_TOPK), c0:c0 + LANE] = te
            return 0

        lax.fori_loop(0, PEER_HEADS, head, 0)
    ids_ref[...] = idt_scr[...].T.astype(jnp.int32)


def _out_proj(x, on, om, p, mod_l, row_off, bpm, lw, seq_len):
    n, d = x.shape
    nb = n // TB
    bps = seq_len // TB
    halo = 8
    hb = TB // halo
    tok = lambda w: pl.BlockSpec((TB, w), lambda i: (i, 0))
    in_specs = [tok(HW), tok(HW), tok(256),
                pl.BlockSpec((halo, 256), lambda i: (jnp.maximum(i * hb - 1, 0), 0)),
                pl.BlockSpec((halo, 256), lambda i: (jnp.minimum((i + 1) * hb, n // halo - 1), 0)),
                tok(d),
                pl.BlockSpec((1, 1, mod_l.shape[-1]), lambda i: (row_off + i // bpm, 0, 0)),
                _const_spec((HW, d)), _const_spec((256, d)), _const_spec((HW, d)),
                _const_spec((256, 256)), _const_spec((1, 256)), _const_spec((1, d)),
                _const_spec((d, PEER_HEADS * LANE)), _const_spec((2, PEER_NKEYS, LANE))]
    nk = PEER_HEADS * PEER_TOPK
    return pl.pallas_call(
        functools.partial(_out_kernel, bps=bps, seq_len=seq_len),
        grid=(nb,),
        in_specs=in_specs,
        out_specs=[tok(d), tok(d), tok(nk), pl.BlockSpec((nk, TB), lambda i: (0, i))],
        out_shape=[jax.ShapeDtypeStruct((n, d), F32), jax.ShapeDtypeStruct((n, d), F32),
                   jax.ShapeDtypeStruct((n, nk), jnp.int32), jax.ShapeDtypeStruct((nk, n), F32)],
        scratch_shapes=[pltpu.VMEM((PEER_HEADS, TB, LANE), BF16), pltpu.VMEM((nk, TB), F32)],
        compiler_params=_params(("arbitrary",)),
        name="out_proj",
    )(on, om, p, p, p, x, mod_l, lw["w_o_na"], lw["w_o_pool"], lw["w_o_mla"],
      lw["pool_w"], lw["pool_scale"], lw["norm2"], lw["peer_wq"], lw["peer_sk"])


def _gelu_tanh(x):
    return x * (0.5 * (1.0 + jnp.tanh(0.7978845608028654 * (x + 0.044715 * (x * x * x)))))


def _peer_token_mix(chunk, hrow, gcol, ch):
    acc = None
    for s in range(ch):
        us = lax.bitcast_convert_type(chunk(s) & jnp.int32(-65536), F32)
        term = us * hrow[:, s * LANE:(s + 1) * LANE]
        acc = term if acc is None else acc + term
    wgt = gcol * _gelu_tanh(jnp.sum(acc, axis=-1, keepdims=True))
    parts = []
    for s in range(ch):
        vs = lax.bitcast_convert_type(chunk(s) << 16, F32)
        parts.append(jnp.sum(vs * wgt, axis=0, keepdims=True))
    return jnp.concatenate(parts, axis=-1)


def _peer_staged_kernel(rows_ref, gt_ref, h2_ref, x1_ref, mod_ref, o_ref):
    d = x1_ref.shape[1]
    ch = d // LANE
    nk = gt_ref.shape[0]
    g2 = mod_ref[0][:, 5 * d:6 * d]
    tok_lane = lax.broadcasted_iota(jnp.int32, gt_ref.shape, 1)
    base = (pl.program_id(0) % (PEER_TB // PEER_SUB)) * PEER_SUB
    h8 = h2_ref[...]
    ys = []
    for t in range(PEER_SUB):
        chunk = lambda s: rows_ref[pl.ds(t * nk * ch + s, nk, stride=ch), :]
        gcol = jnp.sum(jnp.where(tok_lane == base + t, gt_ref[...], 0.0), axis=-1, keepdims=True)
        ys.append(_peer_token_mix(chunk, h8[t:t + 1, :], gcol, ch))
    o_ref[...] = x1_ref[...] + g2 * jnp.concatenate(ys, axis=0)


def _peer_staged(x1, h2, rows, gt, mod_l, row_off, tpm):
    n, d = x1.shape
    nk = gt.shape[0]
    per = PEER_SUB * nk * (d // LANE)
    sub_per_tb = PEER_TB // PEER_SUB
    tok = pl.BlockSpec((PEER_SUB, d), lambda j: (j, 0))
    return pl.pallas_call(
        _peer_staged_kernel,
        grid=(n // PEER_SUB,),
        in_specs=[pl.BlockSpec((per, LANE), lambda j: (j, 0)),
                  pl.BlockSpec((nk, PEER_TB), lambda j: (0, j // sub_per_tb)),
                  tok, tok,
                  pl.BlockSpec((1, 1, mod_l.shape[-1]), lambda j: (row_off + (j * PEER_SUB) // tpm, 0, 0))],
        out_specs=tok,
        out_shape=jax.ShapeDtypeStruct((n, d), F32),
        compiler_params=_params(("arbitrary",)),
        name="peer_staged",
    )(rows, gt, h2, x1, mod_l)


def _sc_gather(table3, ids_flat):
    m = ids_flat.shape[0]
    _, ch, lane = table3.shape
    info = plsc.get_sparse_core_info()
    nc, nw = info.num_cores, info.num_cores * info.num_subcores
    idx_win = 128
    win = 32
    per_w = m // nw
    assert m % (nw * idx_win) == 0
    mesh = plsc.VectorSubcoreMesh(core_axis_name="core", subcore_axis_name="subcore")

    @functools.partial(
        pl.kernel, mesh=mesh,
        out_type=jax.ShapeDtypeStruct((m, ch, lane), table3.dtype),
        scratch_types=[pltpu.VMEM((idx_win,), jnp.int32),
                       pltpu.VMEM((win, ch, lane), table3.dtype),
                       pltpu.VMEM((win, ch, lane), table3.dtype),
                       pltpu.SemaphoreType.DMA, pltpu.SemaphoreType.DMA])
    def gather(tab_hbm, idx_hbm, out_hbm, idx_v, rows_a, rows_b, sem_a, sem_b):
        wid = lax.axis_index("subcore") * nc + lax.axis_index("core")
        bufs = ((rows_a, sem_a), (rows_b, sem_b))
        nq = idx_win // win

        def fetch(q):
            rows, sem = bufs[q % 2]
            return pltpu.make_async_copy(tab_hbm.at[idx_v.at[pl.ds(q * win, win)]], rows, sem)

        @pl.loop(0, per_w // idx_win)
        def _(g):
            base = pl.multiple_of(wid * per_w + g * idx_win, idx_win)
            pltpu.sync_copy(idx_hbm.at[pl.ds(base, idx_win)], idx_v)
            fetch(0).start()
            for q in range(nq):
                fetch(q).wait()
                if q + 1 < nq:
                    fetch(q + 1).start()
                pltpu.sync_copy(bufs[q % 2][0], out_hbm.at[pl.ds(base + q * win, win)])

    return gather(table3, ids_flat)


def _peer_kernel(ids_hbm, gt_ref, h2_ref, x1_ref, mod_ref, tab_hbm, o_ref,
                 ids_s, buf, sem_i, sem_r):
    d = x1_ref.shape[1]
    ch = d // LANE
    pitch = ch + 1
    nsub = x1_ref.shape[0] // PEER_SUB
    nk = gt_ref.shape[0]
    nids = PEER_SUB * nk
    i = pl.program_id(0)
    g2 = mod_ref[0][:, 5 * d:6 * d]
    tok_lane = lax.broadcasted_iota(jnp.int32, gt_ref.shape, 1)

    def ids_copy(j, slot):
        start = pl.multiple_of((i * nsub + j) * nids, nids)
        return pltpu.make_async_copy(ids_hbm.at[pl.ds(start, nids)],
                                     ids_s.at[pl.ds(slot * nids, nids)], sem_i.at[slot])

    def row_copy(slot, e, f):
        src = tab_hbm.at[pl.ds(pl.multiple_of(e * ch, ch), ch), :]
        dst = buf.at[slot, pl.ds(f * pitch, ch), :]
        return pltpu.make_async_copy(src, dst, sem_r.at[slot])

    def issue_rows(slot):
        for t in range(PEER_SUB):
            def body(kk, _):
                for r in range(8):
                    f = t * nk + kk * 8 + r
                    row_copy(slot, ids_s[slot * nids + f], f).start(priority=r % 2)
                return 0

            lax.fori_loop(0, nk // 8, body, 0)

    def wait_rows(slot):
        done = buf.at[slot, pl.ds(0, nids * ch), :]
        pltpu.make_async_copy(done, done, sem_r.at[slot]).wait()

    def compute(slot, j):
        base = pl.multiple_of(j * PEER_SUB, PEER_SUB)
        h8 = h2_ref[pl.ds(base, PEER_SUB), :]
        ys = []
        for t in range(PEER_SUB):
            chunk = lambda s: buf[slot, pl.ds(t * nk * pitch + s, nk, stride=pitch), :]
            gcol = jnp.sum(jnp.where(tok_lane == base + t, gt_ref[...], 0.0), axis=-1, keepdims=True)
            ys.append(_peer_token_mix(chunk, h8[t:t + 1, :], gcol, ch))
        y8 = jnp.concatenate(ys, axis=0)
        o_ref[pl.ds(base, PEER_SUB), :] = x1_ref[pl.ds(base, PEER_SUB), :] + g2 * y8

    first = ids_copy(0, 0)
    first.start()
    first.wait()
    issue_rows(0)
    ids_copy(1, 1).start()

    def pair(jj, _):
        j0 = 2 * jj
        ids_copy(j0 + 1, 1).wait()
        issue_rows(1)

        @pl.when(j0 + 2 < nsub)
        def _():
            ids_copy(j0 + 2, 0).start()

        wait_rows(0)
        compute(0, j0)

        @pl.when(j0 + 2 < nsub)
        def _():
            ids_copy(j0 + 2, 0).wait()
            issue_rows(0)

        @pl.when(j0 + 3 < nsub)
        def _():
            ids_copy(j0 + 3, 1).start()

        wait_rows(1)
        compute(1, j0 + 1)
        return 0

    lax.fori_loop(0, nsub // 2, pair, 0)


def _pack_tables(peer_u, peer_v):
    e, d = peer_u.shape
    ub = lax.bitcast_convert_type(peer_u.astype(BF16), jnp.uint16).astype(jnp.uint32)
    vb = lax.bitcast_convert_type(peer_v.astype(BF16), jnp.uint16).astype(jnp.uint32)
    words = lax.bitcast_convert_type((ub << 16) | vb, jnp.int32)
    return words.reshape(e, d // LANE, LANE)


def _peer(x1, h2, ids, gt, mod_l, row_off, bpm, table):
    n, d = x1.shape
    nk = gt.shape[0]
    nb = n // PEER_TB
    tok = pl.BlockSpec((PEER_TB, d), lambda i: (i, 0))
    any_spec = pl.BlockSpec(memory_space=pl.ANY)
    return pl.pallas_call(
        _peer_kernel,
        grid=(nb,),
        in_specs=[any_spec,
                  pl.BlockSpec((nk, PEER_TB), lambda i: (0, i)),
                  tok, tok,
                  pl.BlockSpec((1, 1, mod_l.shape[-1]), lambda i: (row_off + i // bpm, 0, 0)),
                  any_spec],
        out_specs=tok,
        out_shape=jax.ShapeDtypeStruct((n, d), F32),
        scratch_shapes=[pltpu.SMEM((2 * PEER_SUB * nk,), jnp.int32),
                        pltpu.VMEM((2, PEER_SUB * nk * (d // LANE + 1), LANE), jnp.int32),
                        pltpu.SemaphoreType.DMA((2,)),
                        pltpu.SemaphoreType.DMA((2,))],
        compiler_params=_params(("arbitrary",)),
        name="peer",
    )(ids.reshape(n * nk), gt, h2, x1, mod_l, table.reshape(-1, LANE))


def _pad_heads(w, width):
    pad = [(0, 0)] * (w.ndim - 1) + [(0, LANE - width)]
    w = jnp.pad(w, pad)
    return w.reshape(w.shape[:-2] + (HW,))


def _head_gain(g, width):
    depth = g.shape[0]
    g = jnp.pad(g, ((0, 0), (0, LANE - width)))
    return jnp.tile(g, (1, HEADS)).reshape(depth, 1, HW)


def _rope_tables(seq):
    t = np.arange(seq)
    half = MLA_ROPE // 2
    inv = ROPE_THETA ** (-np.arange(0, half, 2, dtype=np.float32) / half)
    cos = np.ones((seq, LANE), np.float32)
    sin = np.zeros((seq, LANE), np.float32)
    for off, pos in ((MLA_NOPE, t // GRID_W), (MLA_NOPE + half, t % GRID_W)):
        ang = pos.astype(np.float32)[:, None] * inv[None, :]
        q = half // 2
        cos[:, off:off + q] = np.cos(ang)
        cos[:, off + q:off + half] = np.cos(ang)
        sin[:, off:off + q] = -np.sin(ang)
        sin[:, off + q:off + half] = np.sin(ang)
    return jnp.asarray(cos), jnp.asarray(sin)


def _nat_bias(rel_bias):
    v = np.arange(WIN_R)[:, None]
    j = np.arange(WIN_R)[None, :]
    dr = j - v + WIN_R - 1
    cq = np.arange(GRID_W)[:, None]
    kc = np.arange(GRID_W)[None, :]
    cstart = np.clip(cq - WIN_C // 2, 0, GRID_W - WIN_C)
    ok = (kc >= cstart) & (kc < cstart + WIN_C)
    dc = np.clip(kc - cq + WIN_C - 1, 0, 2 * WIN_C - 2)
    b = rel_bias[:, :, dr]
    b = b[..., dc]
    b = jnp.where(jnp.asarray(ok)[None, None, None, None], b, NEG_INF)
    b = jnp.transpose(b, (0, 2, 1, 4, 3, 5))
    return b.reshape(b.shape[0], WIN_R, HEADS, GRID_W, WIN_R * GRID_W)


def _layer_weights(w_in, na_q_norm, na_k_norm, mla_cq_norm, mla_ckv_norm, mla_w_uq, mla_w_ukv,
                   mla_q_norm, mla_k_norm, w_out, pool_w, pool_scale, norm1, norm2,
                   peer_wq, peer_subkeys):
    depth, d, _ = w_in.shape
    na_w = HEADS * NA_DH
    segs = np.cumsum([0, na_w, na_w, na_w, 256, 256, 128, MLA_ROPE])
    part = lambda i: w_in[:, :, segs[i]:segs[i + 1]]
    heads = lambda w: _pad_heads(w.reshape(depth, d, HEADS, NA_DH), NA_DH)
    w_in_p = jnp.concatenate(
        [heads(part(0)), heads(part(1)), heads(part(2)), part(3), part(4), part(5),
         jnp.pad(part(6), ((0, 0), (0, 0), (0, LANE - MLA_ROPE)))], axis=-1).astype(BF16)

    w_uq = _pad_heads(mla_w_uq, MLA_QK).astype(BF16)
    k_nope = _pad_heads(mla_w_ukv[..., :MLA_NOPE], MLA_NOPE)
    eye = np.zeros((MLA_ROPE, HEADS, LANE), np.float32)
    for h in range(HEADS):
        eye[np.arange(MLA_ROPE), h, MLA_NOPE + np.arange(MLA_ROPE)] = 1.0
    eye = jnp.broadcast_to(jnp.asarray(eye.reshape(MLA_ROPE, HW)), (depth, MLA_ROPE, HW))
    zer = jnp.zeros((depth, 256 - 128 - MLA_ROPE, HW), F32)
    w_k = jnp.concatenate([k_nope, eye, zer], axis=1).astype(BF16)
    w_v = jnp.concatenate([_pad_heads(mla_w_ukv[..., MLA_NOPE:], MLA_V),
                           jnp.zeros((depth, 128, HW), F32)], axis=1).astype(BF16)

    mix_w = HEADS * NA_DH
    w_o_na = jnp.pad(w_out[:, :mix_w].reshape(depth, HEADS, NA_DH, d),
                     ((0, 0), (0, 0), (0, LANE - NA_DH), (0, 0))).reshape(depth, HW, d).astype(BF16)
    w_o_pool = w_out[:, mix_w:mix_w + 256].astype(BF16)
    w_o_mla = jnp.pad(w_out[:, mix_w + 256:].reshape(depth, HEADS, MLA_V, d),
                      ((0, 0), (0, 0), (0, LANE - MLA_V), (0, 0))).reshape(depth, HW, d).astype(BF16)
    ng = len(POOL_WINDOWS)
    pw = jnp.zeros((depth, ng * POOL_G, ng * POOL_G), F32)
    for g in range(ng):
        pw = pw.at[:, g * POOL_G:(g + 1) * POOL_G, g * POOL_G:(g + 1) * POOL_G].set(pool_w[:, g])

    half = peer_subkeys.shape[-1]
    sk = jnp.stack([jnp.pad(peer_subkeys[:, 0], ((0, 0), (0, 0), (0, LANE - half))),
                    jnp.pad(peer_subkeys[:, 1], ((0, 0), (0, 0), (LANE - half, 0)))], axis=1).astype(BF16)

    return dict(
        w_in=w_in_p, w_uq=w_uq, w_k=w_k, w_v=w_v,
        g_q=_head_gain(na_q_norm, NA_DH), g_k=_head_gain(na_k_norm, NA_DH),
        g_cq=mla_cq_norm[:, None, :], g_ckv=mla_ckv_norm[:, None, :],
        g_qm=_head_gain(mla_q_norm, MLA_QK), g_km=_head_gain(mla_k_norm, MLA_QK),
        w_o_na=w_o_na, w_o_pool=w_o_pool, w_o_mla=w_o_mla,
        pool_w=pw.astype(BF16), pool_scale=pool_scale[:, None, :],
        norm1=norm1[:, None, :], norm2=norm2[:, None, :],
        peer_wq=peer_wq.astype(BF16), peer_sk=sk)


def kernel(x_prompt, x_sample, c, cache_nat_k, cache_nat_v, cache_mla_ckv, cache_mla_krope, c_ctx, w_mod, b_mod, norm1, norm2, w_in, na_q_norm, na_k_norm, na_rel_bias, pool_w, pool_scale, mla_cq_norm, mla_ckv_norm, mla_w_uq, mla_w_ukv, mla_q_norm, mla_k_norm, w_out, peer_wq, peer_subkeys, peer_u, peer_v):
    batch, seq, d = x_prompt.shape
    db, ds, _ = x_sample.shape
    depth = w_mod.shape[0]
    past = cache_nat_k.shape[2]
    assert seq == TB and ds % TB == 0 and ds % (GRID_W * WIN_R) == 0 and db + 1 <= 8

    cond8 = jnp.concatenate([c_ctx[None, :], c, jnp.zeros((8 - 1 - db, d), F32)], axis=0)
    mod = _modulation(cond8, w_mod, b_mod).reshape(depth, 8, 1, 6 * d)

    lw_all = _layer_weights(w_in, na_q_norm, na_k_norm, mla_cq_norm, mla_ckv_norm, mla_w_uq,
                            mla_w_ukv, mla_q_norm, mla_k_norm, w_out, pool_w, pool_scale,
                            norm1, norm2, peer_wq, peer_subkeys)
    bias_all = _nat_bias(na_rel_bias)
    tables = [_pack_tables(peer_u[l], peer_v[l]) for l in range(depth)]
    cos_lat, sin_lat = _rope_tables(ds)
    cos_ctx = jnp.ones((TB, LANE), F32)
    sin_ctx = jnp.zeros((TB, LANE), F32)

    ck = jnp.concatenate([cache_mla_ckv, cache_mla_krope,
                          jnp.zeros(cache_mla_ckv.shape[:-1] + (256 - 128 - MLA_ROPE,), F32)],
                         axis=-1).astype(BF16)
    kc_mla, vc_mla = _cache_kv(ck, lw_all["w_k"], lw_all["w_v"], lw_all["g_km"])
    kc_na = _pad_heads(cache_nat_k, NA_DH).astype(BF16)
    vc_na = _pad_heads(cache_nat_v, NA_DH).astype(BF16)

    xc = x_prompt.reshape(batch * seq, d)
    xl = x_sample.reshape(db * ds, d)
    lat_bpm = ds // TB
    ctx_bpm = batch * seq // TB + 1
    ks, vs, ckvs, krs = [], [], [], []
    for l in range(depth):
        lw = {k: v[l] for k, v in lw_all.items()}
        mod_l = mod[l]

        (qn, kn, vn, knf, vnf, p, qm, km, vm, ckv, kr) = _in_proj(
            xc, mod_l, 0, ctx_bpm, lw, cos_ctx, sin_ctx, 1)
        on, om = _ctx_attn(qn, kn, vn, qm, km, vm, seq)
        x1, h2, ids, gt = _out_proj(xc, on, om, p, mod_l, 0, ctx_bpm, lw, seq)
        rows = _sc_gather(tables[l], ids.reshape(-1))
        xc = _peer_staged(x1, h2, rows.reshape(-1, LANE), gt, mod_l, 0, batch * seq + 1)
        ks.append(knf.reshape(batch, seq, HEADS, LANE)[..., :NA_DH])
        vs.append(vnf.reshape(batch, seq, HEADS, LANE)[..., :NA_DH])
        ckvs.append(ckv.reshape(batch, seq, 128))
        krs.append(kr.reshape(batch, seq, LANE)[..., :MLA_ROPE])

        (qn, kn, vn, _, _, p, qm, km, vm, _, _) = _in_proj(
            xl, mod_l, 1, lat_bpm, lw, cos_lat, sin_lat, lat_bpm)
        on = _nat_attn(qn, kn, vn, kc_na[:, l], vc_na[:, l], bias_all[l], db)
        om = _lat_mla(qm, km, vm, kc_mla[:, l], vc_mla[:, l], db)
        x1, h2, ids, gt = _out_proj(xl, on, om, p, mod_l, 1, lat_bpm, lw, ds)
        xl = _peer(x1, h2, ids, gt, mod_l, 1, ds // PEER_TB, tables[l])

    return (xc.reshape(batch, seq, d), xl.reshape(db, ds, d),
            jnp.stack(ks, axis=1), jnp.stack(vs, axis=1),
            jnp.stack(ckvs, axis=1), jnp.stack(krs, axis=1))
```

```python
import functools

import numpy as np
import jax
import jax.numpy as jnp
from jax import lax
from jax.experimental import pallas as pl
from jax.experimental.pallas import tpu as pltpu
from jax.experimental.pallas import tpu_sc as plsc

F32 = jnp.float32
BF16 = jnp.bfloat16

EPS = 1e-6
ROPE_THETA = 10000.0
NEG_INF = -1e30
GRID_W = 64
HEADS = 6
NA_DH = 64
WIN_R = 8
WIN_C = 16
POOL_WINDOWS = (2, 4, 8, 16)
POOL_G = 64
MLA_NOPE = 64
MLA_ROPE = 32
MLA_QK = MLA_NOPE + MLA_ROPE
MLA_V = 64
PEER_HEADS = 8
PEER_NKEYS = 128
PEER_TOPK = 16
LANE = 128
HW = HEADS * LANE
TB = 256
TQ = 256
PEER_TB = 128
PEER_SUB = 8
VMEM_LIMIT = 56 * 1024 * 1024

_CQ, _CK, _CV = 0, HW, 2 * HW
_CP = 3 * HW
_CCQ = _CP + 256
_CCKV = _CCQ + 256
_CKR = _CCKV + 128
IN_W = _CKR + 128


def _params(sem, vmem=VMEM_LIMIT):
    return pltpu.CompilerParams(dimension_semantics=sem, vmem_limit_bytes=vmem)


def _const_spec(shape):
    n = len(shape)
    return pl.BlockSpec(shape, lambda *_: (0,) * n)


def _nt_dot(a, b):
    return lax.dot_general(a, b, (((1,), (1,)), ((), ())), preferred_element_type=F32)


def _mod_kernel(c_ref, w_ref, b_ref, o_ref):
    c = c_ref[...]
    s = c / (1.0 + jnp.exp(-c))
    o_ref[0] = jnp.dot(s, w_ref[0], preferred_element_type=F32,
                       precision=lax.Precision.HIGHEST) + b_ref[0]


def _modulation(cond8, w_mod, b_mod):
    depth, d, n6 = w_mod.shape
    tn = n6 // 4
    return pl.pallas_call(
        _mod_kernel,
        grid=(depth, n6 // tn),
        in_specs=[_const_spec((8, d)),
                  pl.BlockSpec((1, d, tn), lambda l, j: (l, 0, j)),
                  pl.BlockSpec((1, 1, tn), lambda l, j: (l, 0, j))],
        out_specs=pl.BlockSpec((1, 8, tn), lambda l, j: (l, 0, j)),
        out_shape=jax.ShapeDtypeStruct((depth, 8, n6), F32),
        compiler_params=_params(("arbitrary", "arbitrary")),
        name="modulation",
    )(cond8, w_mod, b_mod.reshape(depth, 1, n6))


def _rms(z, gain):
    return z * lax.rsqrt(jnp.mean(z * z, axis=-1, keepdims=True) + EPS) * gain


def _head_rms(zh, gain_h, n_real):
    ms = jnp.sum(zh * zh, axis=-1, keepdims=True) * (1.0 / n_real)
    return zh * lax.rsqrt(ms + EPS) * gain_h


def _rope(zh, cos, sin, is_x1):
    rot = jnp.where(is_x1, pltpu.roll(zh, LANE - 8, 1), pltpu.roll(zh, 8, 1))
    return zh * cos + rot * sin


def _is_x1(rows):
    lane = lax.broadcasted_iota(jnp.int32, (rows, LANE), 1)
    first = jnp.where(lane >= MLA_NOPE, jnp.where(lane < MLA_NOPE + 8, 1, 0), 0)
    second = jnp.where(lane >= MLA_NOPE + 16, jnp.where(lane < MLA_NOPE + 24, 1, 0), 0)
    return (first + second) > 0


def _mla_kv(ck, wk_ref, wv_ref, gk_ref, cos, sin, km_ref, vm_ref):
    rows = ck.shape[0]
    kk = jnp.dot(ck, wk_ref[...], preferred_element_type=F32)
    is_x1 = _is_x1(rows)
    for h in range(HEADS):
        sl = slice(h * LANE, (h + 1) * LANE)
        kh = _head_rms(kk[:, sl], gk_ref[:, sl], MLA_QK)
        km_ref[:, sl] = _rope(kh, cos, sin, is_x1).astype(BF16)
    vm_ref[...] = jnp.dot(ck, wv_ref[...], preferred_element_type=F32).astype(BF16)


def _in_kernel(x_ref, mod_ref, n1_ref, w_ref, wuq_ref, wk_ref, wv_ref,
               gq_ref, gk_ref, gcq_ref, gckv_ref, gqm_ref, gkm_ref, cos_ref, sin_ref,
               qn_ref, kn_ref, vn_ref, knf_ref, vnf_ref, p_ref,
               qm_ref, km_ref, vm_ref, ckv_ref, kr_ref):
    d = x_ref.shape[1]
    rows = x_ref.shape[0]
    mod = mod_ref[0]
    sh1 = mod[:, 0:d]
    sc1 = mod[:, d:2 * d]
    h = _rms(x_ref[...], n1_ref[...]) * (1.0 + sc1) + sh1
    hb = h.astype(BF16)

    def proj(lo, hi):
        return jnp.dot(hb, w_ref[:, lo:hi], preferred_element_type=F32)

    cos = cos_ref[...]
    sin = sin_ref[...]
    is_x1 = _is_x1(rows)

    zq = proj(_CQ, _CQ + HW)
    zk = proj(_CK, _CK + HW)
    for hh in range(HEADS):
        sl = slice(hh * LANE, (hh + 1) * LANE)
        qn_ref[:, sl] = (_head_rms(zq[:, sl], gq_ref[:, sl], NA_DH) * (NA_DH ** -0.5)).astype(BF16)
        kh = _head_rms(zk[:, sl], gk_ref[:, sl], NA_DH)
        knf_ref[:, sl] = kh
        kn_ref[:, sl] = kh.astype(BF16)
    zv = proj(_CV, _CV + HW)
    vnf_ref[...] = zv
    vn_ref[...] = zv.astype(BF16)
    p_ref[...] = proj(_CP, _CP + 256)

    cq = _rms(proj(_CCQ, _CCQ + 256), gcq_ref[...])
    zqm = jnp.dot(cq.astype(BF16), wuq_ref[...], preferred_element_type=F32)
    for hh in range(HEADS):
        sl = slice(hh * LANE, (hh + 1) * LANE)
        qh = _head_rms(zqm[:, sl], gqm_ref[:, sl], MLA_QK)
        qm_ref[:, sl] = (_rope(qh, cos, sin, is_x1) * (MLA_QK ** -0.5)).astype(BF16)

    ckv = _rms(proj(_CCKV, _CCKV + 128), gckv_ref[...])
    kr = proj(_CKR, _CKR + 128)
    ckv_ref[...] = ckv
    kr_ref[...] = kr
    ck = jnp.concatenate([ckv, kr], axis=-1).astype(BF16)
    _mla_kv(ck, wk_ref, wv_ref, gkm_ref, cos, sin, km_ref, vm_ref)


def _in_proj(x, mod_l, row_off, bpm, lw, cos_t, sin_t, rope_blocks):
    n, d = x.shape
    nb = n // TB
    tok = lambda w: pl.BlockSpec((TB, w), lambda i: (i, 0))
    rope_spec = pl.BlockSpec((TB, LANE), lambda i: (i % rope_blocks, 0))
    in_specs = [tok(d),
                pl.BlockSpec((1, 1, mod_l.shape[-1]), lambda i: (row_off + i // bpm, 0, 0)),
                _const_spec((1, d)), _const_spec((d, IN_W)), _const_spec((256, HW)),
                _const_spec((256, HW)), _const_spec((256, HW)),
                _const_spec((1, HW)), _const_spec((1, HW)), _const_spec((1, 256)),
                _const_spec((1, 128)), _const_spec((1, HW)), _const_spec((1, HW)),
                rope_spec, rope_spec]
    widths = [(HW, BF16), (HW, BF16), (HW, BF16), (HW, F32), (HW, F32), (256, F32),
              (HW, BF16), (HW, BF16), (HW, BF16), (128, F32), (128, F32)]
    return pl.pallas_call(
        _in_kernel,
        grid=(nb,),
        in_specs=in_specs,
        out_specs=[tok(w) for w, _ in widths],
        out_shape=[jax.ShapeDtypeStruct((n, w), dt) for w, dt in widths],
        compiler_params=_params(("arbitrary",)),
        name="in_proj",
    )(x, mod_l, lw["norm1"], lw["w_in"], lw["w_uq"], lw["w_k"], lw["w_v"],
      lw["g_q"], lw["g_k"], lw["g_cq"], lw["g_ckv"], lw["g_qm"], lw["g_km"], cos_t, sin_t)


def _cache_kernel(ck_ref, wk_ref, wv_ref, gk_ref, km_ref, vm_ref):
    rows = ck_ref.shape[2]
    cos = jnp.ones((rows, LANE), F32)
    sin = jnp.zeros((rows, LANE), F32)
    _mla_kv(ck_ref[0, 0], wk_ref.at[0], wv_ref.at[0], gk_ref.at[0], cos, sin,
            km_ref.at[0, 0], vm_ref.at[0, 0])


def _cache_kv(ck, w_k, w_v, g_km):
    db, depth, p, _ = ck.shape
    spec = lambda w: pl.BlockSpec((1, 1, p, w), lambda b, l: (b, l, 0, 0))
    wspec = lambda r: pl.BlockSpec((1, r, HW), lambda b, l: (l, 0, 0))
    return pl.pallas_call(
        _cache_kernel,
        grid=(db, depth),
        in_specs=[spec(256), wspec(256), wspec(256), wspec(1)],
        out_specs=[spec(HW), spec(HW)],
        out_shape=[jax.ShapeDtypeStruct((db, depth, p, HW), BF16)] * 2,
        compiler_params=_params(("arbitrary", "arbitrary")),
        name="cache_kv",
    )(ck, w_k, w_v, g_km)


def _softmax_av(s_list, v_list):
    m = s_list[0].max(axis=-1, keepdims=True)
    for s in s_list[1:]:
        m = jnp.maximum(m, s.max(axis=-1, keepdims=True))
    acc = None
    den = None
    for s, v in zip(s_list, v_list):
        p = jnp.exp(s - m)
        l = p.sum(axis=-1, keepdims=True)
        o = jnp.dot(p.astype(BF16), v, preferred_element_type=F32)
        acc = o if acc is None else acc + o
        den = l if den is None else den + l
    return acc / den


def _ctx_attn_kernel(qn, kn, vn, qm, km, vm, on, om):
    for q, k, v, o in ((qn, kn, vn, on), (qm, km, vm, om)):
        for h in range(HEADS):
            sl = slice(h * LANE, (h + 1) * LANE)
            s = _nt_dot(q[:, sl], k[:, sl])
            o[:, sl] = _softmax_av([s], [v[:, sl]]).astype(BF16)


def _ctx_attn(qn, kn, vn, qm, km, vm, seq):
    n = qn.shape[0]
    spec = pl.BlockSpec((seq, HW), lambda i: (i, 0))
    return pl.pallas_call(
        _ctx_attn_kernel,
        grid=(n // seq,),
        in_specs=[spec] * 6,
        out_specs=[spec] * 2,
        out_shape=[jax.ShapeDtypeStruct((n, HW), BF16)] * 2,
        compiler_params=_params(("arbitrary",)),
        name="ctx_attn",
    )(qn, kn, vn, qm, km, vm)


def _lat_mla_kernel(q, k, v, kc, vc, o):
    s1 = _nt_dot(q[...], k[...])
    s2 = _nt_dot(q[...], kc[0])
    o[...] = _softmax_av([s1, s2], [v[...], vc[0]]).astype(BF16)


def _lat_mla(qm, km, vm, kc, vc, db):
    n = qm.shape[0]
    ds = n // db
    nq = ds // TQ
    qspec = pl.BlockSpec((TQ, LANE), lambda b, h, i: (b * nq + i, h))
    kspec = pl.BlockSpec((ds, LANE), lambda b, h, i: (b, h))
    cspec = pl.BlockSpec((1, kc.shape[1], LANE), lambda b, h, i: (b, 0, h))
    return pl.pallas_call(
        _lat_mla_kernel,
        grid=(db, HEADS, nq),
        in_specs=[qspec, kspec, kspec, cspec, cspec],
        out_specs=qspec,
        out_shape=jax.ShapeDtypeStruct((n, HW), BF16),
        compiler_params=_params(("arbitrary",) * 3),
        name="lat_mla",
    )(qm, km, vm, kc, vc)


def _nat_kernel(q, k, v, kc, vc, bias, o, *, rows):
    r = pl.program_id(1)
    rs = jnp.clip(r - WIN_R // 2, 0, rows - WIN_R)
    start = pl.multiple_of(rs * GRID_W, GRID_W)
    band = WIN_R * GRID_W
    for h in range(HEADS):
        sl = slice(h * LANE, (h + 1) * LANE)
        qh = q[:, sl]
        s1 = _nt_dot(qh, k[pl.ds(start, band), sl]) + bias[0, h]
        s2 = _nt_dot(qh, kc[0, :, sl])
        o[:, sl] = _softmax_av([s1, s2], [v[pl.ds(start, band), sl], vc[0, :, sl]]).astype(BF16)


def _nat_attn(qn, kn, vn, kc, vc, bias, db):
    n = qn.shape[0]
    ds = n // db
    rows = ds // GRID_W
    band = WIN_R * GRID_W

    def variant(r):
        return jnp.where(r < WIN_R // 2, r, jnp.where(r > rows - WIN_R // 2, r - (rows - WIN_R), WIN_R // 2))

    qspec = pl.BlockSpec((GRID_W, HW), lambda b, r: (b * rows + r, 0))
    kspec = pl.BlockSpec((ds, HW), lambda b, r: (b, 0))
    cspec = pl.BlockSpec((1, kc.shape[1], HW), lambda b, r: (b, 0, 0))
    bspec = pl.BlockSpec((1, HEADS, GRID_W, band), lambda b, r: (variant(r), 0, 0, 0))
    return pl.pallas_call(
        functools.partial(_nat_kernel, rows=rows),
        grid=(db, rows),
        in_specs=[qspec, kspec, kspec, cspec, cspec, bspec],
        out_specs=qspec,
        out_shape=jax.ShapeDtypeStruct((n, HW), BF16),
        compiler_params=_params(("arbitrary", "arbitrary")),
        name="nat_attn",
    )(qn, kn, vn, kc, vc, bias)


def _split3(x):
    hi = x.astype(BF16)
    r = x - hi.astype(F32)
    mid = r.astype(BF16)
    lo = (r - mid.astype(F32)).astype(BF16)
    return hi, mid, lo


def _pool(p_prev, p_cur, p_next, posb, seq_len):
    rows = p_cur.shape[0]
    halo = p_prev.shape[0]
    ext = rows + 2 * halo
    pext = jnp.concatenate([p_prev, p_cur, p_next], axis=0)
    parts = _split3(pext)
    t = posb + lax.broadcasted_iota(jnp.int32, (rows, ext), 0)
    s = posb - halo + lax.broadcasted_iota(jnp.int32, (rows, ext), 1)
    tcol = posb + lax.broadcasted_iota(jnp.int32, (rows, 1), 0)
    grp = lax.broadcasted_iota(jnp.int32, (rows, 256), 1) // POOL_G
    d = jnp.zeros((rows, 256), F32)
    for gi, w in enumerate(POOL_WINDOWS):
        lo = jnp.maximum(t - w // 2, 0)
        hi = jnp.minimum(t + (w - w // 2), seq_len)
        sel = jnp.where(s >= lo, jnp.where(s < hi, 1.0, 0.0), 0.0).astype(BF16)
        tot = sum(jnp.dot(sel, part, preferred_element_type=F32) for part in parts)
        cnt = (jnp.minimum(tcol + (w - w // 2), seq_len) - jnp.maximum(tcol - w // 2, 0)).astype(F32)
        d = jnp.where(grp == gi, tot / cnt - p_cur, d)
    return d


def _first_max(x, pos, sentinel):
    m = jnp.max(x, axis=0, keepdims=True)
    idx = jnp.min(jnp.where(x == m, pos, sentinel), axis=0, keepdims=True)
    return m, idx


def _topk_head(qh, sk_ref):
    c = qh.shape[0]
    key_pos = lax.broadcasted_iota(jnp.int32, (PEER_NKEYS, c), 0).astype(F32)
    row16 = lax.broadcasted_iota(jnp.int32, (PEER_TOPK, c), 0)
    neg = jnp.float32(-jnp.inf)
    s0 = _nt_dot(sk_ref[0], qh)
    s1 = _nt_dot(sk_ref[1], qh)

    def stage1(a, carry):
        out = []
        for s, sv, si in (carry[0:3], carry[3:6]):
            m, idx = _first_max(s, key_pos, float(PEER_NKEYS))
            out += [jnp.where(key_pos == idx, neg, s),
                    jnp.where(row16 == a, m, sv), jnp.where(row16 == a, idx, si)]
        return tuple(out)

    zf = jnp.zeros((PEER_TOPK, c), F32)
    _, sv0, si0, _, sv1, si1 = lax.fori_loop(0, PEER_TOPK, stage1, (s0, zf, zf, s1, zf, zf))

    sub8 = lax.broadcasted_iota(jnp.int32, (8, c), 0)
    sub8f = sub8.astype(F32)
    cs, ci, cf = [], [], []

    def piece(a_vals, a_ids, a_flat, b_vals, b_ids, b_flat, nb):
        val = a_vals + b_vals
        if nb < 8:
            val = jnp.where(sub8 < nb, val, neg)
        cs.append(val)
        ci.append(a_ids * float(PEER_NKEYS) + b_ids)
        cf.append(jnp.broadcast_to(a_flat * float(PEER_TOPK) + b_flat, (8, c)))

    for a in range(8):
        nb = PEER_TOPK // (a + 1)
        for b0 in range(0, nb, 8):
            piece(sv0[a:a + 1], si0[a:a + 1], float(a), sv1[b0:b0 + 8], si1[b0:b0 + 8],
                  sub8f + float(b0), min(nb - b0, 8))
    piece(sv0[8:16], si0[8:16], sub8f + 8.0, sv1[0:1], si1[0:1], jnp.zeros((8, c), F32), 8)
    npc = len(cs)
    nflat = float(PEER_TOPK * PEER_TOPK)

    def stage2(k, carry):
        vals = list(carry[:npc])
        tv, te = carry[npc], carry[npc + 1]
        m = vals[0]
        for v in vals[1:]:
            m = jnp.maximum(m, v)
        m = jnp.max(m, axis=0, keepdims=True)
        pos = None
        for v, f in zip(vals, cf):
            cand = jnp.where(v == m, f, nflat)
            pos = cand if pos is None else jnp.minimum(pos, cand)
        pos = jnp.min(pos, axis=0, keepdims=True)
        e = None
        for i, f in zip(ci, cf):
            cand = jnp.where(f == pos, i, -1.0)
            e = cand if e is None else jnp.maximum(e, cand)
        e = jnp.max(e, axis=0, keepdims=True)
        vals = [jnp.where(f == pos, neg, v) for v, f in zip(vals, cf)]
        return tuple(vals) + (jnp.where(row16 == k, m, tv), jnp.where(row16 == k, e, te))

    res = lax.fori_loop(0, PEER_TOPK, stage2, tuple(cs) + (zf, zf))
    return res[npc], res[npc + 1]


def _out_kernel(on_ref, om_ref, pc_ref, pp_ref, pn_ref, x_ref, mod_ref,
                won_ref, wop_ref, wom_ref, pw_ref, ps_ref, n2_ref, wq_ref, sk_ref,
                x1_ref, h2_ref, ids_ref, gt_ref, q_scr, idt_scr, *, bps, seq_len):
    d = x_ref.shape[1]
    rows = x_ref.shape[0]
    i = pl.program_id(0)
    mod = mod_ref[0]
    g1 = mod[:, 2 * d:3 * d]
    sh2 = mod[:, 3 * d:4 * d]
    sc2 = mod[:, 4 * d:5 * d]

    posb = (i % bps) * rows
    dpool = _pool(pp_ref[...], pc_ref[...], pn_ref[...], posb, seq_len)
    ypool = jnp.dot(dpool.astype(BF16), pw_ref[...], preferred_element_type=F32) * ps_ref[...]
    mix = (jnp.dot(on_ref[...], won_ref[...], preferred_element_type=F32)
           + jnp.dot(ypool.astype(BF16), wop_ref[...], preferred_element_type=F32)
           + jnp.dot(om_ref[...], wom_ref[...], preferred_element_type=F32))
    x1 = x_ref[...] + g1 * mix
    x1_ref[...] = x1
    h2 = _rms(x1, n2_ref[...]) * (1.0 + sc2) + sh2
    h2_ref[...] = h2

    q = jnp.dot(h2.astype(BF16), wq_ref[...], preferred_element_type=F32)
    for hh in range(PEER_HEADS):
        q_scr[hh] = q[:, hh * LANE:(hh + 1) * LANE].astype(BF16)

    for c0 in range(0, rows, LANE):
        def head(hh, _):
            tv, te = _topk_head(q_scr[hh, c0:c0 + LANE, :], sk_ref)
            ex = jnp.exp(tv - tv[0:1])
            gates = ex / jnp.sum(ex, axis=0, keepdims=True)
            r0 = pl.multiple_of(hh * PEER_TOPK, PEER_TOPK)
            gt_ref[pl.ds(r0, PEER_TOPK), c0:c0 + LANE] = gates
            idt_scr[pl.ds(r0, PEER_TOPK), c0:c0 + LANE] = te
            return 0

        lax.fori_loop(0, PEER_HEADS, head, 0)
    ids_ref[...] = idt_scr[...].T.astype(jnp.int32)


def _out_proj(x, on, om, p, mod_l, row_off, bpm, lw, seq_len):
    n, d = x.shape
    nb = n // TB
    bps = seq_len // TB
    halo = 8
    hb = TB // halo
    tok = lambda w: pl.BlockSpec((TB, w), lambda i: (i, 0))
    in_specs = [tok(HW), tok(HW), tok(256),
                pl.BlockSpec((halo, 256), lambda i: (jnp.maximum(i * hb - 1, 0), 0)),
                pl.BlockSpec((halo, 256), lambda i: (jnp.minimum((i + 1) * hb, n // halo - 1), 0)),
                tok(d),
                pl.BlockSpec((1, 1, mod_l.shape[-1]), lambda i: (row_off + i // bpm, 0, 0)),
                _const_spec((HW, d)), _const_spec((256, d)), _const_spec((HW, d)),
                _const_spec((256, 256)), _const_spec((1, 256)), _const_spec((1, d)),
                _const_spec((d, PEER_HEADS * LANE)), _const_spec((2, PEER_NKEYS, LANE))]
    nk = PEER_HEADS * PEER_TOPK
    return pl.pallas_call(
        functools.partial(_out_kernel, bps=bps, seq_len=seq_len),
        grid=(nb,),
        in_specs=in_specs,
        out_specs=[tok(d), tok(d), tok(nk), pl.BlockSpec((nk, TB), lambda i: (0, i))],
        out_shape=[jax.ShapeDtypeStruct((n, d), F32), jax.ShapeDtypeStruct((n, d), F32),
                   jax.ShapeDtypeStruct((n, nk), jnp.int32), jax.ShapeDtypeStruct((nk, n), F32)],
        scratch_shapes=[pltpu.VMEM((PEER_HEADS, TB, LANE), BF16), pltpu.VMEM((nk, TB), F32)],
        compiler_params=_params(("arbitrary",)),
        name="out_proj",
    )(on, om, p, p, p, x, mod_l, lw["w_o_na"], lw["w_o_pool"], lw["w_o_mla"],
      lw["pool_w"], lw["pool_scale"], lw["norm2"], lw["peer_wq"], lw["peer_sk"])


def _gelu_tanh(x):
    return x * (0.5 * (1.0 + jnp.tanh(0.7978845608028654 * (x + 0.044715 * (x * x * x)))))


def _peer_token_mix(chunk, hrow, gcol, ch):
    acc = None
    for s in range(ch):
        us = lax.bitcast_convert_type(chunk(s) & jnp.int32(-65536), F32)
        term = us * hrow[:, s * LANE:(s + 1) * LANE]
        acc = term if acc is None else acc + term
    wgt = gcol * _gelu_tanh(jnp.sum(acc, axis=-1, keepdims=True))
    parts = []
    for s in range(ch):
        vs = lax.bitcast_convert_type(chunk(s) << 16, F32)
        parts.append(jnp.sum(vs * wgt, axis=0, keepdims=True))
    return jnp.concatenate(parts, axis=-1)


def _peer_staged_kernel(rows_ref, gt_ref, h2_ref, x1_ref, mod_ref, o_ref):
    d = x1_ref.shape[1]
    ch = d // LANE
    nk = gt_ref.shape[0]
    g2 = mod_ref[0][:, 5 * d:6 * d]
    tok_lane = lax.broadcasted_iota(jnp.int32, gt_ref.shape, 1)
    base = (pl.program_id(0) % (PEER_TB // PEER_SUB)) * PEER_SUB
    h8 = h2_ref[...]
    ys = []
    for t in range(PEER_SUB):
        chunk = lambda s: rows_ref[pl.ds(t * nk * ch + s, nk, stride=ch), :]
        gcol = jnp.sum(jnp.where(tok_lane == base + t, gt_ref[...], 0.0), axis=-1, keepdims=True)
        ys.append(_peer_token_mix(chunk, h8[t:t + 1, :], gcol, ch))
    o_ref[...] = x1_ref[...] + g2 * jnp.concatenate(ys, axis=0)


def _peer_staged(x1, h2, rows, gt, mod_l, row_off, tpm):
    n, d = x1.shape
    nk = gt.shape[0]
    per = PEER_SUB * nk * (d // LANE)
    sub_per_tb = PEER_TB // PEER_SUB
    tok = pl.BlockSpec((PEER_SUB, d), lambda j: (j, 0))
    return pl.pallas_call(
        _peer_staged_kernel,
        grid=(n // PEER_SUB,),
        in_specs=[pl.BlockSpec((per, LANE), lambda j: (j, 0)),
                  pl.BlockSpec((nk, PEER_TB), lambda j: (0, j // sub_per_tb)),
                  tok, tok,
                  pl.BlockSpec((1, 1, mod_l.shape[-1]), lambda j: (row_off + (j * PEER_SUB) // tpm, 0, 0))],
        out_specs=tok,
        out_shape=jax.ShapeDtypeStruct((n, d), F32),
        compiler_params=_params(("arbitrary",)),
        name="peer_staged",
    )(rows, gt, h2, x1, mod_l)


def _sc_gather(table3, ids_flat):
    m = ids_flat.shape[0]
    _, ch, lane = table3.shape
    info = plsc.get_sparse_core_info()
    nc, nw = info.num_cores, info.num_cores * info.num_subcores
    idx_win = 128
    win = 32
    per_w = m // nw
    assert m % (nw * idx_win) == 0
    mesh = plsc.VectorSubcoreMesh(core_axis_name="core", subcore_axis_name="subcore")

    @functools.partial(
        pl.kernel, mesh=mesh,
        out_type=jax.ShapeDtypeStruct((m, ch, lane), table3.dtype),
        scratch_types=[pltpu.VMEM((idx_win,), jnp.int32),
                       pltpu.VMEM((win, ch, lane), table3.dtype),
                       pltpu.VMEM((win, ch, lane), table3.dtype),
                       pltpu.SemaphoreType.DMA, pltpu.SemaphoreType.DMA])
    def gather(tab_hbm, idx_hbm, out_hbm, idx_v, rows_a, rows_b, sem_a, sem_b):
        wid = lax.axis_index("subcore") * nc + lax.axis_index("core")
        bufs = ((rows_a, sem_a), (rows_b, sem_b))
        nq = idx_win // win

        def fetch(q):
            rows, sem = bufs[q % 2]
            return pltpu.make_async_copy(tab_hbm.at[idx_v.at[pl.ds(q * win, win)]], rows, sem)

        @pl.loop(0, per_w // idx_win)
        def _(g):
            base = pl.multiple_of(wid * per_w + g * idx_win, idx_win)
            pltpu.sync_copy(idx_hbm.at[pl.ds(base, idx_win)], idx_v)
            fetch(0).start()
            for q in range(nq):
                fetch(q).wait()
                if q + 1 < nq:
                    fetch(q + 1).start()
                pltpu.sync_copy(bufs[q % 2][0], out_hbm.at[pl.ds(base + q * win, win)])

    return gather(table3, ids_flat)


def _peer_kernel(ids_hbm, gt_ref, h2_ref, x1_ref, mod_ref, tab_hbm, o_ref,
                 ids_s, buf, sem_i, sem_r):
    d = x1_ref.shape[1]
    ch = d // LANE
    pitch = ch + 1
    nsub = x1_ref.shape[0] // PEER_SUB
    nk = gt_ref.shape[0]
    nids = PEER_SUB * nk
    i = pl.program_id(0)
    g2 = mod_ref[0][:, 5 * d:6 * d]
    tok_lane = lax.broadcasted_iota(jnp.int32, gt_ref.shape, 1)

    def ids_copy(j, slot):
        start = pl.multiple_of((i * nsub + j) * nids, nids)
        return pltpu.make_async_copy(ids_hbm.at[pl.ds(start, nids)],
                                     ids_s.at[pl.ds(slot * nids, nids)], sem_i.at[slot])

    def row_copy(slot, e, f):
        src = tab_hbm.at[pl.ds(pl.multiple_of(e * ch, ch), ch), :]
        dst = buf.at[slot, pl.ds(f * pitch, ch), :]
        return pltpu.make_async_copy(src, dst, sem_r.at[slot])

    def issue_rows(slot):
        for t in range(PEER_SUB):
            def body(kk, _):
                for r in range(8):
                    f = t * nk + kk * 8 + r
                    row_copy(slot, ids_s[slot * nids + f], f).start(priority=r % 2)
                return 0

            lax.fori_loop(0, nk // 8, body, 0)

    def wait_rows(slot):
        done = buf.at[slot, pl.ds(0, nids * ch), :]
        pltpu.make_async_copy(done, done, sem_r.at[slot]).wait()

    def compute(slot, j):
        base = pl.multiple_of(j * PEER_SUB, PEER_SUB)
        h8 = h2_ref[pl.ds(base, PEER_SUB), :]
        ys = []
        for t in range(PEER_SUB):
            chunk = lambda s: buf[slot, pl.ds(t * nk * pitch + s, nk, stride=pitch), :]
            gcol = jnp.sum(jnp.where(tok_lane == base + t, gt_ref[...], 0.0), axis=-1, keepdims=True)
            ys.append(_peer_token_mix(chunk, h8[t:t + 1, :], gcol, ch))
        y8 = jnp.concatenate(ys, axis=0)
        o_ref[pl.ds(base, PEER_SUB), :] = x1_ref[pl.ds(base, PEER_SUB), :] + g2 * y8

    first = ids_copy(0, 0)
    first.start()
    first.wait()
    issue_rows(0)
    ids_copy(1, 1).start()

    def pair(jj, _):
        j0 = 2 * jj
        ids_copy(j0 + 1, 1).wait()
        issue_rows(1)

        @pl.when(j0 + 2 < nsub)
        def _():
            ids_copy(j0 + 2, 0).start()

        wait_rows(0)
        compute(0, j0)

        @pl.when(j0 + 2 < nsub)
        def _():
            ids_copy(j0 + 2, 0).wait()
            issue_rows(0)

        @pl.when(j0 + 3 < nsub)
        def _():
            ids_copy(j0 + 3, 1).start()

        wait_rows(1)
        compute(1, j0 + 1)
        return 0

    lax.fori_loop(0, nsub // 2, pair, 0)


def _pack_tables(peer_u, peer_v):
    e, d = peer_u.shape
    ub = lax.bitcast_convert_type(peer_u.astype(BF16), jnp.uint16).astype(jnp.uint32)
    vb = lax.bitcast_convert_type(peer_v.astype(BF16), jnp.uint16).astype(jnp.uint32)
    words = lax.bitcast_convert_type((ub << 16) | vb, jnp.int32)
    return words.reshape(e, d // LANE, LANE)


def _peer(x1, h2, ids, gt, mod_l, row_off, bpm, table):
    n, d = x1.shape
    nk = gt.shape[0]
    nb = n // PEER_TB
    tok = pl.BlockSpec((PEER_TB, d), lambda i: (i, 0))
    any_spec = pl.BlockSpec(memory_space=pl.ANY)
    return pl.pallas_call(
        _peer_kernel,
        grid=(nb,),
        in_specs=[any_spec,
                  pl.BlockSpec((nk, PEER_TB), lambda i: (0, i)),
                  tok, tok,
                  pl.BlockSpec((1, 1, mod_l.shape[-1]), lambda i: (row_off + i // bpm, 0, 0)),
                  any_spec],
        out_specs=tok,
        out_shape=jax.ShapeDtypeStruct((n, d), F32),
        scratch_shapes=[pltpu.SMEM((2 * PEER_SUB * nk,), jnp.int32),
                        pltpu.VMEM((2, PEER_SUB * nk * (d // LANE + 1), LANE), jnp.int32),
                        pltpu.SemaphoreType.DMA((2,)),
                        pltpu.SemaphoreType.DMA((2,))],
        compiler_params=_params(("arbitrary",)),
        name="peer",
    )(ids.reshape(n * nk), gt, h2, x1, mod_l, table.reshape(-1, LANE))


def _pad_heads(w, width):
    pad = [(0, 0)] * (w.ndim - 1) + [(0, LANE - width)]
    w = jnp.pad(w, pad)
    return w.reshape(w.shape[:-2] + (HW,))


def _head_gain(g, width):
    depth = g.shape[0]
    g = jnp.pad(g, ((0, 0), (0, LANE - width)))
    return jnp.tile(g, (1, HEADS)).reshape(depth, 1, HW)


def _rope_tables(seq):
    t = np.arange(seq)
    half = MLA_ROPE // 2
    inv = ROPE_THETA ** (-np.arange(0, half, 2, dtype=np.float32) / half)
    cos = np.ones((seq, LANE), np.float32)
    sin = np.zeros((seq, LANE), np.float32)
    for off, pos in ((MLA_NOPE, t // GRID_W), (MLA_NOPE + half, t % GRID_W)):
        ang = pos.astype(np.float32)[:, None] * inv[None, :]
        q = half // 2
        cos[:, off:off + q] = np.cos(ang)
        cos[:, off + q:off + half] = np.cos(ang)
        sin[:, off:off + q] = -np.sin(ang)
        sin[:, off + q:off + half] = np.sin(ang)
    return jnp.asarray(cos), jnp.asarray(sin)


def _nat_bias(rel_bias):
    v = np.arange(WIN_R)[:, None]
    j = np.arange(WIN_R)[None, :]
    dr = j - v + WIN_R - 1
    cq = np.arange(GRID_W)[:, None]
    kc = np.arange(GRID_W)[None, :]
    cstart = np.clip(cq - WIN_C // 2, 0, GRID_W - WIN_C)
    ok = (kc >= cstart) & (kc < cstart + WIN_C)
    dc = np.clip(kc - cq + WIN_C - 1, 0, 2 * WIN_C - 2)
    b = rel_bias[:, :, dr]
    b = b[..., dc]
    b = jnp.where(jnp.asarray(ok)[None, None, None, None], b, NEG_INF)
    b = jnp.transpose(b, (0, 2, 1, 4, 3, 5))
    return b.reshape(b.shape[0], WIN_R, HEADS, GRID_W, WIN_R * GRID_W)


def _layer_weights(w_in, na_q_norm, na_k_norm, mla_cq_norm, mla_ckv_norm, mla_w_uq, mla_w_ukv,
                   mla_q_norm, mla_k_norm, w_out, pool_w, pool_scale, norm1, norm2,
                   peer_wq, peer_subkeys):
    depth, d, _ = w_in.shape
    na_w = HEADS * NA_DH
    segs = np.cumsum([0, na_w, na_w, na_w, 256, 256, 128, MLA_ROPE])
    part = lambda i: w_in[:, :, segs[i]:segs[i + 1]]
    heads = lambda w: _pad_heads(w.reshape(depth, d, HEADS, NA_DH), NA_DH)
    w_in_p = jnp.concatenate(
        [heads(part(0)), heads(part(1)), heads(part(2)), part(3), part(4), part(5),
         jnp.pad(part(6), ((0, 0), (0, 0), (0, LANE - MLA_ROPE)))], axis=-1).astype(BF16)

    w_uq = _pad_heads(mla_w_uq, MLA_QK).astype(BF16)
    k_nope = _pad_heads(mla_w_ukv[..., :MLA_NOPE], MLA_NOPE)
    eye = np.zeros((MLA_ROPE, HEADS, LANE), np.float32)
    for h in range(HEADS):
        eye[np.arange(MLA_ROPE), h, MLA_NOPE + np.arange(MLA_ROPE)] = 1.0
    eye = jnp.broadcast_to(jnp.asarray(eye.reshape(MLA_ROPE, HW)), (depth, MLA_ROPE, HW))
    zer = jnp.zeros((depth, 256 - 128 - MLA_ROPE, HW), F32)
    w_k = jnp.concatenate([k_nope, eye, zer], axis=1).astype(BF16)
    w_v = jnp.concatenate([_pad_heads(mla_w_ukv[..., MLA_NOPE:], MLA_V),
                           jnp.zeros((depth, 128, HW), F32)], axis=1).astype(BF16)

    mix_w = HEADS * NA_DH
    w_o_na = jnp.pad(w_out[:, :mix_w].reshape(depth, HEADS, NA_DH, d),
                     ((0, 0), (0, 0), (0, LANE - NA_DH), (0, 0))).reshape(depth, HW, d).astype(BF16)
    w_o_pool = w_out[:, mix_w:mix_w + 256].astype(BF16)
    w_o_mla = jnp.pad(w_out[:, mix_w + 256:].reshape(depth, HEADS, MLA_V, d),
                      ((0, 0), (0, 0), (0, LANE - MLA_V), (0, 0))).reshape(depth, HW, d).astype(BF16)
    ng = len(POOL_WINDOWS)
    pw = jnp.zeros((depth, ng * POOL_G, ng * POOL_G), F32)
    for g in range(ng):
        pw = pw.at[:, g * POOL_G:(g + 1) * POOL_G, g * POOL_G:(g + 1) * POOL_G].set(pool_w[:, g])

    half = peer_subkeys.shape[-1]
    sk = jnp.stack([jnp.pad(peer_subkeys[:, 0], ((0, 0), (0, 0), (0, LANE - half))),
                    jnp.pad(peer_subkeys[:, 1], ((0, 0), (0, 0), (LANE - half, 0)))], axis=1).astype(BF16)

    return dict(
        w_in=w_in_p, w_uq=w_uq, w_k=w_k, w_v=w_v,
        g_q=_head_gain(na_q_norm, NA_DH), g_k=_head_gain(na_k_norm, NA_DH),
        g_cq=mla_cq_norm[:, None, :], g_ckv=mla_ckv_norm[:, None, :],
        g_qm=_head_gain(mla_q_norm, MLA_QK), g_km=_head_gain(mla_k_norm, MLA_QK),
        w_o_na=w_o_na, w_o_pool=w_o_pool, w_o_mla=w_o_mla,
        pool_w=pw.astype(BF16), pool_scale=pool_scale[:, None, :],
        norm1=norm1[:, None, :], norm2=norm2[:, None, :],
        peer_wq=peer_wq.astype(BF16), peer_sk=sk)


def kernel(x_prompt, x_sample, c, cache_nat_k, cache_nat_v, cache_mla_ckv, cache_mla_krope, c_ctx, w_mod, b_mod, norm1, norm2, w_in, na_q_norm, na_k_norm, na_rel_bias, pool_w, pool_scale, mla_cq_norm, mla_ckv_norm, mla_w_uq, mla_w_ukv, mla_q_norm, mla_k_norm, w_out, peer_wq, peer_subkeys, peer_u, peer_v):
    batch, seq, d = x_prompt.shape
    db, ds, _ = x_sample.shape
    depth = w_mod.shape[0]
    past = cache_nat_k.shape[2]
    assert seq == TB and ds % TB == 0 and ds % (GRID_W * WIN_R) == 0 and db + 1 <= 8

    cond8 = jnp.concatenate([c_ctx[None, :], c, jnp.zeros((8 - 1 - db, d), F32)], axis=0)
    mod = _modulation(cond8, w_mod, b_mod).reshape(depth, 8, 1, 6 * d)

    lw_all = _layer_weights(w_in, na_q_norm, na_k_norm, mla_cq_norm, mla_ckv_norm, mla_w_uq,
                            mla_w_ukv, mla_q_norm, mla_k_norm, w_out, pool_w, pool_scale,
                            norm1, norm2, peer_wq, peer_subkeys)
    bias_all = _nat_bias(na_rel_bias)
    tables = [_pack_tables(peer_u[l], peer_v[l]) for l in range(depth)]
    cos_lat, sin_lat = _rope_tables(ds)
    cos_ctx = jnp.ones((TB, LANE), F32)
    sin_ctx = jnp.zeros((TB, LANE), F32)

    ck = jnp.concatenate([cache_mla_ckv, cache_mla_krope,
                          jnp.zeros(cache_mla_ckv.shape[:-1] + (256 - 128 - MLA_ROPE,), F32)],
                         axis=-1).astype(BF16)
    kc_mla, vc_mla = _cache_kv(ck, lw_all["w_k"], lw_all["w_v"], lw_all["g_km"])
    kc_na = _pad_heads(cache_nat_k, NA_DH).astype(BF16)
    vc_na = _pad_heads(cache_nat_v, NA_DH).astype(BF16)

    xs = [x_prompt.reshape(batch * seq, d)] + [x_sample[b] for b in range(db)]
    one_row = max(batch * seq, ds) + 1
    lat_bpm = ds // TB
    ks, vs, ckvs, krs = [], [], [], []
    pending = None

    def mix(item):
        si, x1, h2, rows, gt, mod_l = item
        xs[si] = _peer_staged(x1, h2, rows.reshape(-1, LANE), gt, mod_l, si, one_row)

    for l in range(depth):
        lw = {k: v[l] for k, v in lw_all.items()}
        mod_l = mod[l]
        for si in range(db + 1):
            x = xs[si]
            if si == 0:
                (qn, kn, vn, knf, vnf, p, qm, km, vm, ckv, kr) = _in_proj(
                    x, mod_l, 0, one_row, lw, cos_ctx, sin_ctx, 1)
                on, om = _ctx_attn(qn, kn, vn, qm, km, vm, seq)
                x1, h2, ids, gt = _out_proj(x, on, om, p, mod_l, 0, one_row, lw, seq)
                ks.append(knf.reshape(batch, seq, HEADS, LANE)[..., :NA_DH])
                vs.append(vnf.reshape(batch, seq, HEADS, LANE)[..., :NA_DH])
                ckvs.append(ckv.reshape(batch, seq, 128))
                krs.append(kr.reshape(batch, seq, LANE)[..., :MLA_ROPE])
            else:
                b = si - 1
                (qn, kn, vn, _, _, p, qm, km, vm, _, _) = _in_proj(
                    x, mod_l, si, one_row, lw, cos_lat, sin_lat, lat_bpm)
                on = _nat_attn(qn, kn, vn, kc_na[b:b + 1, l], vc_na[b:b + 1, l], bias_all[l], 1)
                om = _lat_mla(qm, km, vm, kc_mla[b:b + 1, l], vc_mla[b:b + 1, l], 1)
                x1, h2, ids, gt = _out_proj(x, on, om, p, mod_l, si, one_row, lw, ds)
            rows = _sc_gather(tables[l], ids.reshape(-1))
            if pending is not None:
                mix(pending)
            pending = (si, x1, h2, rows, gt, mod_l)
    mix(pending)

    return (xs[0].reshape(batch, seq, d), jnp.stack(xs[1:], axis=0),
            jnp.stack(ks, axis=1), jnp.stack(vs, axis=1),
            jnp.stack(ckvs, axis=1), jnp.stack(krs, axis=1))
```

```python
import functools

import numpy as np
import jax
import jax.numpy as jnp
from jax import lax
from jax.experimental import pallas as pl
from jax.experimental.pallas import tpu as pltpu
from jax.experimental.pallas import tpu_sc as plsc

F32 = jnp.float32
BF16 = jnp.bfloat16

EPS = 1e-6
ROPE_THETA = 10000.0
NEG_INF = -1e30
GRID_W = 64
HEADS = 6
NA_DH = 64
WIN_R = 8
WIN_C = 16
POOL_WINDOWS = (2, 4, 8, 16)
POOL_G = 64
MLA_NOPE = 64
MLA_ROPE = 32
MLA_QK = MLA_NOPE + MLA_ROPE
MLA_V = 64
PEER_HEADS = 8
PEER_NKEYS = 128
PEER_TOPK = 16
LANE = 128
HW = HEADS * LANE
TB = 256
TQ = 256
PEER_TB = 128
PEER_SUB = 8
VMEM_LIMIT = 56 * 1024 * 1024
SC_MIX_STREAMS = (0,)

_CQ, _CK, _CV = 0, HW, 2 * HW
_CP = 3 * HW
_CCQ = _CP + 256
_CCKV = _CCQ + 256
_CKR = _CCKV + 128
IN_W = _CKR + 128


def _params(sem, vmem=VMEM_LIMIT):
    return pltpu.CompilerParams(dimension_semantics=sem, vmem_limit_bytes=vmem)


def _const_spec(shape):
    n = len(shape)
    return pl.BlockSpec(shape, lambda *_: (0,) * n)


def _nt_dot(a, b):
    return lax.dot_general(a, b, (((1,), (1,)), ((), ())), preferred_element_type=F32)


def _mod_kernel(c_ref, w_ref, b_ref, o_ref):
    c = c_ref[...]
    s = c / (1.0 + jnp.exp(-c))
    o_ref[0] = jnp.dot(s, w_ref[0], preferred_element_type=F32,
                       precision=lax.Precision.HIGHEST) + b_ref[0]


def _modulation(cond8, w_mod, b_mod):
    depth, d, n6 = w_mod.shape
    tn = n6 // 4
    return pl.pallas_call(
        _mod_kernel,
        grid=(depth, n6 // tn),
        in_specs=[_const_spec((8, d)),
                  pl.BlockSpec((1, d, tn), lambda l, j: (l, 0, j)),
                  pl.BlockSpec((1, 1, tn), lambda l, j: (l, 0, j))],
        out_specs=pl.BlockSpec((1, 8, tn), lambda l, j: (l, 0, j)),
        out_shape=jax.ShapeDtypeStruct((depth, 8, n6), F32),
        compiler_params=_params(("arbitrary", "arbitrary")),
        name="modulation",
    )(cond8, w_mod, b_mod.reshape(depth, 1, n6))


def _rms(z, gain):
    return z * lax.rsqrt(jnp.mean(z * z, axis=-1, keepdims=True) + EPS) * gain


def _head_rms(zh, gain_h, n_real):
    ms = jnp.sum(zh * zh, axis=-1, keepdims=True) * (1.0 / n_real)
    return zh * lax.rsqrt(ms + EPS) * gain_h


def _rope(zh, cos, sin, is_x1):
    rot = jnp.where(is_x1, pltpu.roll(zh, LANE - 8, 1), pltpu.roll(zh, 8, 1))
    return zh * cos + rot * sin


def _is_x1(rows):
    lane = lax.broadcasted_iota(jnp.int32, (rows, LANE), 1)
    first = jnp.where(lane >= MLA_NOPE, jnp.where(lane < MLA_NOPE + 8, 1, 0), 0)
    second = jnp.where(lane >= MLA_NOPE + 16, jnp.where(lane < MLA_NOPE + 24, 1, 0), 0)
    return (first + second) > 0


def _mla_kv(ck, wk_ref, wv_ref, gk_ref, cos, sin, km_ref, vm_ref):
    rows = ck.shape[0]
    kk = jnp.dot(ck, wk_ref[...], preferred_element_type=F32)
    is_x1 = _is_x1(rows)
    for h in range(HEADS):
        sl = slice(h * LANE, (h + 1) * LANE)
        kh = _head_rms(kk[:, sl], gk_ref[:, sl], MLA_QK)
        km_ref[:, sl] = _rope(kh, cos, sin, is_x1).astype(BF16)
    vm_ref[...] = jnp.dot(ck, wv_ref[...], preferred_element_type=F32).astype(BF16)


def _in_kernel(x_ref, mod_ref, n1_ref, w_ref, wuq_ref, wk_ref, wv_ref,
               gq_ref, gk_ref, gcq_ref, gckv_ref, gqm_ref, gkm_ref, cos_ref, sin_ref,
               qn_ref, kn_ref, vn_ref, knf_ref, vnf_ref, p_ref,
               qm_ref, km_ref, vm_ref, ckv_ref, kr_ref):
    d = x_ref.shape[1]
    rows = x_ref.shape[0]
    mod = mod_ref[0]
    sh1 = mod[:, 0:d]
    sc1 = mod[:, d:2 * d]
    h = _rms(x_ref[...], n1_ref[...]) * (1.0 + sc1) + sh1
    hb = h.astype(BF16)

    def proj(lo, hi):
        return jnp.dot(hb, w_ref[:, lo:hi], preferred_element_type=F32)

    cos = cos_ref[...]
    sin = sin_ref[...]
    is_x1 = _is_x1(rows)

    zq = proj(_CQ, _CQ + HW)
    zk = proj(_CK, _CK + HW)
    for hh in range(HEADS):
        sl = slice(hh * LANE, (hh + 1) * LANE)
        qn_ref[:, sl] = (_head_rms(zq[:, sl], gq_ref[:, sl], NA_DH) * (NA_DH ** -0.5)).astype(BF16)
        kh = _head_rms(zk[:, sl], gk_ref[:, sl], NA_DH)
        knf_ref[:, sl] = kh
        kn_ref[:, sl] = kh.astype(BF16)
    zv = proj(_CV, _CV + HW)
    vnf_ref[...] = zv
    vn_ref[...] = zv.astype(BF16)
    p_ref[...] = proj(_CP, _CP + 256)

    cq = _rms(proj(_CCQ, _CCQ + 256), gcq_ref[...])
    zqm = jnp.dot(cq.astype(BF16), wuq_ref[...], preferred_element_type=F32)
    for hh in range(HEADS):
        sl = slice(hh * LANE, (hh + 1) * LANE)
        qh = _head_rms(zqm[:, sl], gqm_ref[:, sl], MLA_QK)
        qm_ref[:, sl] = (_rope(qh, cos, sin, is_x1) * (MLA_QK ** -0.5)).astype(BF16)

    ckv = _rms(proj(_CCKV, _CCKV + 128), gckv_ref[...])
    kr = proj(_CKR, _CKR + 128)
    ckv_ref[...] = ckv
    kr_ref[...] = kr
    ck = jnp.concatenate([ckv, kr], axis=-1).astype(BF16)
    _mla_kv(ck, wk_ref, wv_ref, gkm_ref, cos, sin, km_ref, vm_ref)


def _in_proj(x, mod_l, row_off, bpm, lw, cos_t, sin_t, rope_blocks):
    n, d = x.shape
    nb = n // TB
    tok = lambda w: pl.BlockSpec((TB, w), lambda i: (i, 0))
    rope_spec = pl.BlockSpec((TB, LANE), lambda i: (i % rope_blocks, 0))
    in_specs = [tok(d),
                pl.BlockSpec((1, 1, mod_l.shape[-1]), lambda i: (row_off + i // bpm, 0, 0)),
                _const_spec((1, d)), _const_spec((d, IN_W)), _const_spec((256, HW)),
                _const_spec((256, HW)), _const_spec((256, HW)),
                _const_spec((1, HW)), _const_spec((1, HW)), _const_spec((1, 256)),
                _const_spec((1, 128)), _const_spec((1, HW)), _const_spec((1, HW)),
                rope_spec, rope_spec]
    widths = [(HW, BF16), (HW, BF16), (HW, BF16), (HW, F32), (HW, F32), (256, F32),
              (HW, BF16), (HW, BF16), (HW, BF16), (128, F32), (128, F32)]
    return pl.pallas_call(
        _in_kernel,
        grid=(nb,),
        in_specs=in_specs,
        out_specs=[tok(w) for w, _ in widths],
        out_shape=[jax.ShapeDtypeStruct((n, w), dt) for w, dt in widths],
        compiler_params=_params(("arbitrary",)),
        name="in_proj",
    )(x, mod_l, lw["norm1"], lw["w_in"], lw["w_uq"], lw["w_k"], lw["w_v"],
      lw["g_q"], lw["g_k"], lw["g_cq"], lw["g_ckv"], lw["g_qm"], lw["g_km"], cos_t, sin_t)


def _cache_kernel(ck_ref, wk_ref, wv_ref, gk_ref, km_ref, vm_ref):
    rows = ck_ref.shape[2]
    cos = jnp.ones((rows, LANE), F32)
    sin = jnp.zeros((rows, LANE), F32)
    _mla_kv(ck_ref[0, 0], wk_ref.at[0], wv_ref.at[0], gk_ref.at[0], cos, sin,
            km_ref.at[0, 0], vm_ref.at[0, 0])


def _cache_kv(ck, w_k, w_v, g_km):
    db, depth, p, _ = ck.shape
    spec = lambda w: pl.BlockSpec((1, 1, p, w), lambda b, l: (b, l, 0, 0))
    wspec = lambda r: pl.BlockSpec((1, r, HW), lambda b, l: (l, 0, 0))
    return pl.pallas_call(
        _cache_kernel,
        grid=(db, depth),
        in_specs=[spec(256), wspec(256), wspec(256), wspec(1)],
        out_specs=[spec(HW), spec(HW)],
        out_shape=[jax.ShapeDtypeStruct((db, depth, p, HW), BF16)] * 2,
        compiler_params=_params(("arbitrary", "arbitrary")),
        name="cache_kv",
    )(ck, w_k, w_v, g_km)


def _softmax_av(s_list, v_list):
    m = s_list[0].max(axis=-1, keepdims=True)
    for s in s_list[1:]:
        m = jnp.maximum(m, s.max(axis=-1, keepdims=True))
    acc = None
    den = None
    for s, v in zip(s_list, v_list):
        p = jnp.exp(s - m)
        l = p.sum(axis=-1, keepdims=True)
        o = jnp.dot(p.astype(BF16), v, preferred_element_type=F32)
        acc = o if acc is None else acc + o
        den = l if den is None else den + l
    return acc / den


def _ctx_attn_kernel(qn, kn, vn, qm, km, vm, on, om):
    for q, k, v, o in ((qn, kn, vn, on), (qm, km, vm, om)):
        for h in range(HEADS):
            sl = slice(h * LANE, (h + 1) * LANE)
            s = _nt_dot(q[:, sl], k[:, sl])
            o[:, sl] = _softmax_av([s], [v[:, sl]]).astype(BF16)


def _ctx_attn(qn, kn, vn, qm, km, vm, seq):
    n = qn.shape[0]
    spec = pl.BlockSpec((seq, HW), lambda i: (i, 0))
    return pl.pallas_call(
        _ctx_attn_kernel,
        grid=(n // seq,),
        in_specs=[spec] * 6,
        out_specs=[spec] * 2,
        out_shape=[jax.ShapeDtypeStruct((n, HW), BF16)] * 2,
        compiler_params=_params(("arbitrary",)),
        name="ctx_attn",
    )(qn, kn, vn, qm, km, vm)


def _lat_mla_kernel(q, k, v, kc, vc, o):
    s1 = _nt_dot(q[...], k[...])
    s2 = _nt_dot(q[...], kc[0])
    o[...] = _softmax_av([s1, s2], [v[...], vc[0]]).astype(BF16)


def _lat_mla(qm, km, vm, kc, vc, db):
    n = qm.shape[0]
    ds = n // db
    nq = ds // TQ
    qspec = pl.BlockSpec((TQ, LANE), lambda b, h, i: (b * nq + i, h))
    kspec = pl.BlockSpec((ds, LANE), lambda b, h, i: (b, h))
    cspec = pl.BlockSpec((1, kc.shape[1], LANE), lambda b, h, i: (b, 0, h))
    return pl.pallas_call(
        _lat_mla_kernel,
        grid=(db, HEADS, nq),
        in_specs=[qspec, kspec, kspec, cspec, cspec],
        out_specs=qspec,
        out_shape=jax.ShapeDtypeStruct((n, HW), BF16),
        compiler_params=_params(("arbitrary",) * 3),
        name="lat_mla",
    )(qm, km, vm, kc, vc)


def _nat_kernel(q, k, v, kc, vc, bias, o, *, rows):
    r = pl.program_id(1)
    rs = jnp.clip(r - WIN_R // 2, 0, rows - WIN_R)
    start = pl.multiple_of(rs * GRID_W, GRID_W)
    band = WIN_R * GRID_W
    for h in range(HEADS):
        sl = slice(h * LANE, (h + 1) * LANE)
        qh = q[:, sl]
        s1 = _nt_dot(qh, k[pl.ds(start, band), sl]) + bias[0, h]
        s2 = _nt_dot(qh, kc[0, :, sl])
        o[:, sl] = _softmax_av([s1, s2], [v[pl.ds(start, band), sl], vc[0, :, sl]]).astype(BF16)


def _nat_attn(qn, kn, vn, kc, vc, bias, db):
    n = qn.shape[0]
    ds = n // db
    rows = ds // GRID_W
    band = WIN_R * GRID_W

    def variant(r):
        return jnp.where(r < WIN_R // 2, r, jnp.where(r > rows - WIN_R // 2, r - (rows - WIN_R), WIN_R // 2))

    qspec = pl.BlockSpec((GRID_W, HW), lambda b, r: (b * rows + r, 0))
    kspec = pl.BlockSpec((ds, HW), lambda b, r: (b, 0))
    cspec = pl.BlockSpec((1, kc.shape[1], HW), lambda b, r: (b, 0, 0))
    bspec = pl.BlockSpec((1, HEADS, GRID_W, band), lambda b, r: (variant(r), 0, 0, 0))
    return pl.pallas_call(
        functools.partial(_nat_kernel, rows=rows),
        grid=(db, rows),
        in_specs=[qspec, kspec, kspec, cspec, cspec, bspec],
        out_specs=qspec,
        out_shape=jax.ShapeDtypeStruct((n, HW), BF16),
        compiler_params=_params(("arbitrary", "arbitrary")),
        name="nat_attn",
    )(qn, kn, vn, kc, vc, bias)


def _split3(x):
    hi = x.astype(BF16)
    r = x - hi.astype(F32)
    mid = r.astype(BF16)
    lo = (r - mid.astype(F32)).astype(BF16)
    return hi, mid, lo


def _pool(p_prev, p_cur, p_next, posb, seq_len):
    rows = p_cur.shape[0]
    halo = p_prev.shape[0]
    ext = rows + 2 * halo
    pext = jnp.concatenate([p_prev, p_cur, p_next], axis=0)
    parts = _split3(pext)
    t = posb + lax.broadcasted_iota(jnp.int32, (rows, ext), 0)
    s = posb - halo + lax.broadcasted_iota(jnp.int32, (rows, ext), 1)
    tcol = posb + lax.broadcasted_iota(jnp.int32, (rows, 1), 0)
    grp = lax.broadcasted_iota(jnp.int32, (rows, 256), 1) // POOL_G
    d = jnp.zeros((rows, 256), F32)
    for gi, w in enumerate(POOL_WINDOWS):
        lo = jnp.maximum(t - w // 2, 0)
        hi = jnp.minimum(t + (w - w // 2), seq_len)
        sel = jnp.where(s >= lo, jnp.where(s < hi, 1.0, 0.0), 0.0).astype(BF16)
        tot = sum(jnp.dot(sel, part, preferred_element_type=F32) for part in parts)
        cnt = (jnp.minimum(tcol + (w - w // 2), seq_len) - jnp.maximum(tcol - w // 2, 0)).astype(F32)
        d = jnp.where(grp == gi, tot / cnt - p_cur, d)
    return d


def _first_max(x, pos, sentinel):
    m = jnp.max(x, axis=0, keepdims=True)
    idx = jnp.min(jnp.where(x == m, pos, sentinel), axis=0, keepdims=True)
    return m, idx


def _topk_head(qh, sk_ref):
    c = qh.shape[0]
    key_pos = lax.broadcasted_iota(jnp.int32, (PEER_NKEYS, c), 0).astype(F32)
    row16 = lax.broadcasted_iota(jnp.int32, (PEER_TOPK, c), 0)
    neg = jnp.float32(-jnp.inf)
    s0 = _nt_dot(sk_ref[0], qh)
    s1 = _nt_dot(sk_ref[1], qh)

    def stage1(a, carry):
        out = []
        for s, sv, si in (carry[0:3], carry[3:6]):
            m, idx = _first_max(s, key_pos, float(PEER_NKEYS))
            out += [jnp.where(key_pos == idx, neg, s),
                    jnp.where(row16 == a, m, sv), jnp.where(row16 == a, idx, si)]
        return tuple(out)

    zf = jnp.zeros((PEER_TOPK, c), F32)
    _, sv0, si0, _, sv1, si1 = lax.fori_loop(0, PEER_TOPK, stage1, (s0, zf, zf, s1, zf, zf))

    sub8 = lax.broadcasted_iota(jnp.int32, (8, c), 0)
    sub8f = sub8.astype(F32)
    cs, ci, cf = [], [], []

    def piece(a_vals, a_ids, a_flat, b_vals, b_ids, b_flat, nb):
        val = a_vals + b_vals
        if nb < 8:
            val = jnp.where(sub8 < nb, val, neg)
        cs.append(val)
        ci.append(a_ids * float(PEER_NKEYS) + b_ids)
        cf.append(jnp.broadcast_to(a_flat * float(PEER_TOPK) + b_flat, (8, c)))

    for a in range(8):
        nb = PEER_TOPK // (a + 1)
        for b0 in range(0, nb, 8):
            piece(sv0[a:a + 1], si0[a:a + 1], float(a), sv1[b0:b0 + 8], si1[b0:b0 + 8],
                  sub8f + float(b0), min(nb - b0, 8))
    piece(sv0[8:16], si0[8:16], sub8f + 8.0, sv1[0:1], si1[0:1], jnp.zeros((8, c), F32), 8)
    npc = len(cs)
    nflat = float(PEER_TOPK * PEER_TOPK)

    def stage2(k, carry):
        vals = list(carry[:npc])
        tv, te = carry[npc], carry[npc + 1]
        m = vals[0]
        for v in vals[1:]:
            m = jnp.maximum(m, v)
        m = jnp.max(m, axis=0, keepdims=True)
        pos = None
        for v, f in zip(vals, cf):
            cand = jnp.where(v == m, f, nflat)
            pos = cand if pos is None else jnp.minimum(pos, cand)
        pos = jnp.min(pos, axis=0, keepdims=True)
        e = None
        for i, f in zip(ci, cf):
            cand = jnp.where(f == pos, i, -1.0)
            e = cand if e is None else jnp.maximum(e, cand)
        e = jnp.max(e, axis=0, keepdims=True)
        vals = [jnp.where(f == pos, neg, v) for v, f in zip(vals, cf)]
        return tuple(vals) + (jnp.where(row16 == k, m, tv), jnp.where(row16 == k, e, te))

    res = lax.fori_loop(0, PEER_TOPK, stage2, tuple(cs) + (zf, zf))
    return res[npc], res[npc + 1]


def _out_kernel(on_ref, om_ref, pc_ref, pp_ref, pn_ref, x_ref, mod_ref,
                won_ref, wop_ref, wom_ref, pw_ref, ps_ref, n2_ref, wq_ref, sk_ref,
                x1_ref, h2_ref, ids_ref, gt_ref, gn_ref, q_scr, idt_scr, *, bps, seq_len):
    d = x_ref.shape[1]
    rows = x_ref.shape[0]
    i = pl.program_id(0)
    mod = mod_ref[0]
    g1 = mod[:, 2 * d:3 * d]
    sh2 = mod[:, 3 * d:4 * d]
    sc2 = mod[:, 4 * d:5 * d]

    posb = (i % bps) * rows
    dpool = _pool(pp_ref[...], pc_ref[...], pn_ref[...], posb, seq_len)
    ypool = jnp.dot(dpool.astype(BF16), pw_ref[...], preferred_element_type=F32) * ps_ref[...]
    mix = (jnp.dot(on_ref[...], won_ref[...], preferred_element_type=F32)
           + jnp.dot(ypool.astype(BF16), wop_ref[...], preferred_element_type=F32)
           + jnp.dot(om_ref[...], wom_ref[...], preferred_element_type=F32))
    x1 = x_ref[...] + g1 * mix
    x1_ref[...] = x1
    h2 = _rms(x1, n2_ref[...]) * (1.0 + sc2) + sh2
    h2_ref[...] = h2

    q = jnp.dot(h2.astype(BF16), wq_ref[...], preferred_element_type=F32)
    for hh in range(PEER_HEADS):
        q_scr[hh] = q[:, hh * LANE:(hh + 1) * LANE].astype(BF16)

    for c0 in range(0, rows, LANE):
        def head(hh, _):
            tv, te = _topk_head(q_scr[hh, c0:c0 + LANE, :], sk_ref)
            ex = jnp.exp(tv - tv[0:1])
            gates = ex / jnp.sum(ex, axis=0, keepdims=True)
            r0 = pl.multiple_of(hh * PEER_TOPK, PEER_TOPK)
            gt_ref[pl.ds(r0, PEER_TOPK), c0:c0 + LANE] = gates
            idt_scr[pl.ds(r0, PEER_TOPK), c0:c0 + LANE] = te
            return 0

        lax.fori_loop(0, PEER_HEADS, head, 0)
    ids_ref[...] = idt_scr[...].T.astype(jnp.int32)
    gn_ref[...] = gt_ref[...].T


def _out_proj(x, on, om, p, mod_l, row_off, bpm, lw, seq_len):
    n, d = x.shape
    nb = n // TB
    bps = seq_len // TB
    halo = 8
    hb = TB // halo
    tok = lambda w: pl.BlockSpec((TB, w), lambda i: (i, 0))
    in_specs = [tok(HW), tok(HW), tok(256),
                pl.BlockSpec((halo, 256), lambda i: (jnp.maximum(i * hb - 1, 0), 0)),
                pl.BlockSpec((halo, 256), lambda i: (jnp.minimum((i + 1) * hb, n // halo - 1), 0)),
                tok(d),
                pl.BlockSpec((1, 1, mod_l.shape[-1]), lambda i: (row_off + i // bpm, 0, 0)),
                _const_spec((HW, d)), _const_spec((256, d)), _const_spec((HW, d)),
                _const_spec((256, 256)), _const_spec((1, 256)), _const_spec((1, d)),
                _const_spec((d, PEER_HEADS * LANE)), _const_spec((2, PEER_NKEYS, LANE))]
    nk = PEER_HEADS * PEER_TOPK
    return pl.pallas_call(
        functools.partial(_out_kernel, bps=bps, seq_len=seq_len),
        grid=(nb,),
        in_specs=in_specs,
        out_specs=[tok(d), tok(d), tok(nk), pl.BlockSpec((nk, TB), lambda i: (0, i)), tok(nk)],
        out_shape=[jax.ShapeDtypeStruct((n, d), F32), jax.ShapeDtypeStruct((n, d), F32),
                   jax.ShapeDtypeStruct((n, nk), jnp.int32), jax.ShapeDtypeStruct((nk, n), F32),
                   jax.ShapeDtypeStruct((n, nk), F32)],
        scratch_shapes=[pltpu.VMEM((PEER_HEADS, TB, LANE), BF16), pltpu.VMEM((nk, TB), F32)],
        compiler_params=_params(("arbitrary",)),
        name="out_proj",
    )(on, om, p, p, p, x, mod_l, lw["w_o_na"], lw["w_o_pool"], lw["w_o_mla"],
      lw["pool_w"], lw["pool_scale"], lw["norm2"], lw["peer_wq"], lw["peer_sk"])


def _gelu_tanh(x):
    return x * (0.5 * (1.0 + jnp.tanh(0.7978845608028654 * (x + 0.044715 * (x * x * x)))))


def _peer_token_mix(chunk, hrow, gcol, ch):
    acc = None
    for s in range(ch):
        us = lax.bitcast_convert_type(chunk(s) & jnp.int32(-65536), F32)
        term = us * hrow[:, s * LANE:(s + 1) * LANE]
        acc = term if acc is None else acc + term
    wgt = gcol * _gelu_tanh(jnp.sum(acc, axis=-1, keepdims=True))
    parts = []
    for s in range(ch):
        vs = lax.bitcast_convert_type(chunk(s) << 16, F32)
        parts.append(jnp.sum(vs * wgt, axis=0, keepdims=True))
    return jnp.concatenate(parts, axis=-1)


def _peer_staged_kernel(rows_ref, gt_ref, h2_ref, x1_ref, mod_ref, o_ref):
    d = x1_ref.shape[1]
    ch = d // LANE
    nk = gt_ref.shape[0]
    g2 = mod_ref[0][:, 5 * d:6 * d]
    tok_lane = lax.broadcasted_iota(jnp.int32, gt_ref.shape, 1)
    base = (pl.program_id(0) % (PEER_TB // PEER_SUB)) * PEER_SUB
    h8 = h2_ref[...]
    ys = []
    for t in range(PEER_SUB):
        chunk = lambda s: rows_ref[pl.ds(t * nk * ch + s, nk, stride=ch), :]
        gcol = jnp.sum(jnp.where(tok_lane == base + t, gt_ref[...], 0.0), axis=-1, keepdims=True)
        ys.append(_peer_token_mix(chunk, h8[t:t + 1, :], gcol, ch))
    o_ref[...] = x1_ref[...] + g2 * jnp.concatenate(ys, axis=0)


def _peer_staged(x1, h2, rows, gt, mod_l, row_off, tpm):
    n, d = x1.shape
    nk = gt.shape[0]
    per = PEER_SUB * nk * (d // LANE)
    sub_per_tb = PEER_TB // PEER_SUB
    tok = pl.BlockSpec((PEER_SUB, d), lambda j: (j, 0))
    return pl.pallas_call(
        _peer_staged_kernel,
        grid=(n // PEER_SUB,),
        in_specs=[pl.BlockSpec((per, LANE), lambda j: (j, 0)),
                  pl.BlockSpec((nk, PEER_TB), lambda j: (0, j // sub_per_tb)),
                  tok, tok,
                  pl.BlockSpec((1, 1, mod_l.shape[-1]), lambda j: (row_off + (j * PEER_SUB) // tpm, 0, 0))],
        out_specs=tok,
        out_shape=jax.ShapeDtypeStruct((n, d), F32),
        compiler_params=_params(("arbitrary",)),
        name="peer_staged",
    )(rows, gt, h2, x1, mod_l)


def _sc_gather(table3, ids_flat):
    m = ids_flat.shape[0]
    _, ch, lane = table3.shape
    info = plsc.get_sparse_core_info()
    nc, nw = info.num_cores, info.num_cores * info.num_subcores
    idx_win = 128
    win = 32
    per_w = m // nw
    assert m % (nw * idx_win) == 0
    mesh = plsc.VectorSubcoreMesh(core_axis_name="core", subcore_axis_name="subcore")

    @functools.partial(
        pl.kernel, mesh=mesh,
        out_type=jax.ShapeDtypeStruct((m, ch, lane), table3.dtype),
        scratch_types=[pltpu.VMEM((idx_win,), jnp.int32),
                       pltpu.VMEM((win, ch, lane), table3.dtype),
                       pltpu.VMEM((win, ch, lane), table3.dtype),
                       pltpu.SemaphoreType.DMA, pltpu.SemaphoreType.DMA])
    def gather(tab_hbm, idx_hbm, out_hbm, idx_v, rows_a, rows_b, sem_a, sem_b):
        wid = lax.axis_index("subcore") * nc + lax.axis_index("core")
        bufs = ((rows_a, sem_a), (rows_b, sem_b))
        nq = idx_win // win

        def fetch(q):
            rows, sem = bufs[q % 2]
            return pltpu.make_async_copy(tab_hbm.at[idx_v.at[pl.ds(q * win, win)]], rows, sem)

        @pl.loop(0, per_w // idx_win)
        def _(g):
            base = pl.multiple_of(wid * per_w + g * idx_win, idx_win)
            pltpu.sync_copy(idx_hbm.at[pl.ds(base, idx_win)], idx_v)
            fetch(0).start()
            for q in range(nq):
                fetch(q).wait()
                if q + 1 < nq:
                    fetch(q + 1).start()
                pltpu.sync_copy(bufs[q % 2][0], out_hbm.at[pl.ds(base + q * win, win)])

    return gather(table3, ids_flat)


def _sc_peer(table3, ids_flat, gates_flat, h2):
    n, ch, lane = h2.shape
    nk = ids_flat.shape[0] // n
    info = plsc.get_sparse_core_info()
    nc, nw, nl = info.num_cores, info.num_cores * info.num_subcores, info.num_lanes
    tpw = n // nw
    win = 32
    nq = nk // win
    cpr = lane // nl
    nchunk = ch * cpr
    hc = nchunk // 2
    assert n % nw == 0 and nk % win == 0 and win % nl == 0
    mesh = plsc.VectorSubcoreMesh(core_axis_name="core", subcore_axis_name="subcore")
    hi_mask = jnp.int32(-65536)

    @functools.partial(
        pl.kernel, mesh=mesh,
        out_type=jax.ShapeDtypeStruct((n, ch, lane), F32),
        compiler_params=pltpu.CompilerParams(needs_layout_passes=False),
        scratch_types=[pltpu.VMEM((nk,), jnp.int32), pltpu.VMEM((nk,), F32),
                       pltpu.VMEM((ch, lane), F32), pltpu.VMEM((ch, lane), F32),
                       pltpu.VMEM((win, ch, lane), jnp.int32), pltpu.VMEM((win, ch, lane), jnp.int32),
                       pltpu.VMEM((win * nl,), F32), pltpu.VMEM((win,), F32),
                       pltpu.SemaphoreType.DMA, pltpu.SemaphoreType.DMA])
    def peer(tab_hbm, ids_hbm, g_hbm, h2_hbm, y_hbm,
             idx_v, g_v, x_v, y_v, rows_a, rows_b, part_v, w_v, sem_a, sem_b):
        wid = lax.axis_index("subcore") * nc + lax.axis_index("core")
        bufs = ((rows_a, sem_a), (rows_b, sem_b))
        lanes = lax.iota(jnp.int32, nl)
        zero = jnp.zeros((nl,), F32)

        def fetch(q):
            rows, sem = bufs[q % 2]
            return pltpu.make_async_copy(tab_hbm.at[idx_v.at[pl.ds(q * win, win)]], rows, sem)

        def word(rows, r, cc):
            return rows[r, cc // cpr, pl.ds((cc % cpr) * nl, nl)]

        @pl.loop(0, tpw)
        def _(ti):
            tok = wid * tpw + ti
            off = pl.multiple_of(tok * nk, nk)
            pltpu.sync_copy(ids_hbm.at[pl.ds(off, nk)], idx_v)
            pltpu.sync_copy(g_hbm.at[pl.ds(off, nk)], g_v)
            pltpu.sync_copy(h2_hbm.at[tok], x_v)
            for cc in range(nchunk):
                y_v[cc // cpr, pl.ds((cc % cpr) * nl, nl)] = zero
            fetch(0).start()
            for q in range(nq):
                rows = bufs[q % 2][0]
                fetch(q).wait()
                if q + 1 < nq:
                    fetch(q + 1).start()

                for half in range(2):
                    xs = [x_v[(half * hc + c) // cpr, pl.ds(((half * hc + c) % cpr) * nl, nl)]
                          for c in range(hc)]

                    @pl.loop(0, win)
                    def _(r):
                        accs = [None] * 4
                        for c in range(hc):
                            u = lax.bitcast_convert_type(word(rows, r, half * hc + c) & hi_mask, F32)
                            t = u * xs[c]
                            accs[c % 4] = t if accs[c % 4] is None else accs[c % 4] + t
                        acc = (accs[0] + accs[1]) + (accs[2] + accs[3])
                        po = pl.multiple_of(r * nl, nl)
                        if half == 0:
                            part_v[pl.ds(po, nl)] = acc
                        else:
                            part_v[pl.ds(po, nl)] = part_v[pl.ds(po, nl)] + acc

                for grp in range(win // nl):
                    s = zero
                    for rr in range(nl):
                        tot = jnp.sum(part_v[pl.ds((grp * nl + rr) * nl, nl)])
                        s = jnp.where(lanes == rr, tot, s)
                    z = 0.7978845608028654 * (s + 0.044715 * (s * s * s))
                    tanh = 1.0 - 2.0 / (jnp.exp(2.0 * z) + 1.0)
                    gate = g_v[pl.ds(q * win + grp * nl, nl)]
                    w_v[pl.ds(grp * nl, nl)] = gate * (s * (0.5 * (1.0 + tanh)))

                for half in range(2):
                    def body(r, yacc):
                        wr = plsc.load_gather(w_v, [jnp.full((nl,), r, jnp.int32)])
                        out = []
                        for c in range(hc):
                            v = lax.bitcast_convert_type(word(rows, r, half * hc + c) << 16, F32)
                            out.append(yacc[c] + wr * v)
                        return tuple(out)

                    yacc = lax.fori_loop(0, win, body, tuple(zero for _ in range(hc)))
                    for c in range(hc):
                        cc = half * hc + c
                        sl = (cc // cpr, pl.ds((cc % cpr) * nl, nl))
                        y_v[sl] = y_v[sl] + yacc[c]
            pltpu.sync_copy(y_v, y_hbm.at[tok])

    return peer(table3, ids_flat, gates_flat, h2)


def _residual_kernel(x1_ref, y_ref, mod_ref, o_ref):
    d = x1_ref.shape[1]
    o_ref[...] = x1_ref[...] + mod_ref[0][:, 5 * d:6 * d] * y_ref[...]


def _residual(x1, y, mod_l, row):
    n, d = x1.shape
    tok = pl.BlockSpec((TB, d), lambda i: (i, 0))
    return pl.pallas_call(
        _residual_kernel,
        grid=(n // TB,),
        in_specs=[tok, tok, pl.BlockSpec((1, 1, mod_l.shape[-1]), lambda i: (row, 0, 0))],
        out_specs=tok,
        out_shape=jax.ShapeDtypeStruct((n, d), F32),
        compiler_params=_params(("arbitrary",)),
        name="residual",
    )(x1, y, mod_l)


def _peer_kernel(ids_hbm, gt_ref, h2_ref, x1_ref, mod_ref, tab_hbm, o_ref,
                 ids_s, buf, sem_i, sem_r):
    d = x1_ref.shape[1]
    ch = d // LANE
    pitch = ch + 1
    nsub = x1_ref.shape[0] // PEER_SUB
    nk = gt_ref.shape[0]
    nids = PEER_SUB * nk
    i = pl.program_id(0)
    g2 = mod_ref[0][:, 5 * d:6 * d]
    tok_lane = lax.broadcasted_iota(jnp.int32, gt_ref.shape, 1)

    def ids_copy(j, slot):
        start = pl.multiple_of((i * nsub + j) * nids, nids)
        return pltpu.make_async_copy(ids_hbm.at[pl.ds(start, nids)],
                                     ids_s.at[pl.ds(slot * nids, nids)], sem_i.at[slot])

    def row_copy(slot, e, f):
        src = tab_hbm.at[pl.ds(pl.multiple_of(e * ch, ch), ch), :]
        dst = buf.at[slot, pl.ds(f * pitch, ch), :]
        return pltpu.make_async_copy(src, dst, sem_r.at[slot])

    def issue_rows(slot):
        for t in range(PEER_SUB):
            def body(kk, _):
                for r in range(8):
                    f = t * nk + kk * 8 + r
                    row_copy(slot, ids_s[slot * nids + f], f).start(priority=r % 2)
                return 0

            lax.fori_loop(0, nk // 8, body, 0)

    def wait_rows(slot):
        done = buf.at[slot, pl.ds(0, nids * ch), :]
        pltpu.make_async_copy(done, done, sem_r.at[slot]).wait()

    def compute(slot, j):
        base = pl.multiple_of(j * PEER_SUB, PEER_SUB)
        h8 = h2_ref[pl.ds(base, PEER_SUB), :]
        ys = []
        for t in range(PEER_SUB):
            chunk = lambda s: buf[slot, pl.ds(t * nk * pitch + s, nk, stride=pitch), :]
            gcol = jnp.sum(jnp.where(tok_lane == base + t, gt_ref[...], 0.0), axis=-1, keepdims=True)
            ys.append(_peer_token_mix(chunk, h8[t:t + 1, :], gcol, ch))
        y8 = jnp.concatenate(ys, axis=0)
        o_ref[pl.ds(base, PEER_SUB), :] = x1_ref[pl.ds(base, PEER_SUB), :] + g2 * y8

    first = ids_copy(0, 0)
    first.start()
    first.wait()
    issue_rows(0)
    ids_copy(1, 1).start()

    def pair(jj, _):
        j0 = 2 * jj
        ids_copy(j0 + 1, 1).wait()
        issue_rows(1)

        @pl.when(j0 + 2 < nsub)
        def _():
            ids_copy(j0 + 2, 0).start()

        wait_rows(0)
        compute(0, j0)

        @pl.when(j0 + 2 < nsub)
        def _():
            ids_copy(j0 + 2, 0).wait()
            issue_rows(0)

        @pl.when(j0 + 3 < nsub)
        def _():
            ids_copy(j0 + 3, 1).start()

        wait_rows(1)
        compute(1, j0 + 1)
        return 0

    lax.fori_loop(0, nsub // 2, pair, 0)


def _pack_tables(peer_u, peer_v):
    e, d = peer_u.shape
    ub = lax.bitcast_convert_type(peer_u.astype(BF16), jnp.uint16).astype(jnp.uint32)
    vb = lax.bitcast_convert_type(peer_v.astype(BF16), jnp.uint16).astype(jnp.uint32)
    words = lax.bitcast_convert_type((ub << 16) | vb, jnp.int32)
    return words.reshape(e, d // LANE, LANE)


def _peer(x1, h2, ids, gt, mod_l, row_off, bpm, table):
    n, d = x1.shape
    nk = gt.shape[0]
    nb = n // PEER_TB
    tok = pl.BlockSpec((PEER_TB, d), lambda i: (i, 0))
    any_spec = pl.BlockSpec(memory_space=pl.ANY)
    return pl.pallas_call(
        _peer_kernel,
        grid=(nb,),
        in_specs=[any_spec,
                  pl.BlockSpec((nk, PEER_TB), lambda i: (0, i)),
                  tok, tok,
                  pl.BlockSpec((1, 1, mod_l.shape[-1]), lambda i: (row_off + i // bpm, 0, 0)),
                  any_spec],
        out_specs=tok,
        out_shape=jax.ShapeDtypeStruct((n, d), F32),
        scratch_shapes=[pltpu.SMEM((2 * PEER_SUB * nk,), jnp.int32),
                        pltpu.VMEM((2, PEER_SUB * nk * (d // LANE + 1), LANE), jnp.int32),
                        pltpu.SemaphoreType.DMA((2,)),
                        pltpu.SemaphoreType.DMA((2,))],
        compiler_params=_params(("arbitrary",)),
        name="peer",
    )(ids.reshape(n * nk), gt, h2, x1, mod_l, table.reshape(-1, LANE))


def _pad_heads(w, width):
    pad = [(0, 0)] * (w.ndim - 1) + [(0, LANE - width)]
    w = jnp.pad(w, pad)
    return w.reshape(w.shape[:-2] + (HW,))


def _head_gain(g, width):
    depth = g.shape[0]
    g = jnp.pad(g, ((0, 0), (0, LANE - width)))
    return jnp.tile(g, (1, HEADS)).reshape(depth, 1, HW)


def _rope_tables(seq):
    t = np.arange(seq)
    half = MLA_ROPE // 2
    inv = ROPE_THETA ** (-np.arange(0, half, 2, dtype=np.float32) / half)
    cos = np.ones((seq, LANE), np.float32)
    sin = np.zeros((seq, LANE), np.float32)
    for off, pos in ((MLA_NOPE, t // GRID_W), (MLA_NOPE + half, t % GRID_W)):
        ang = pos.astype(np.float32)[:, None] * inv[None, :]
        q = half // 2
        cos[:, off:off + q] = np.cos(ang)
        cos[:, off + q:off + half] = np.cos(ang)
        sin[:, off:off + q] = -np.sin(ang)
        sin[:, off + q:off + half] = np.sin(ang)
    return jnp.asarray(cos), jnp.asarray(sin)


def _nat_bias(rel_bias):
    v = np.arange(WIN_R)[:, None]
    j = np.arange(WIN_R)[None, :]
    dr = j - v + WIN_R - 1
    cq = np.arange(GRID_W)[:, None]
    kc = np.arange(GRID_W)[None, :]
    cstart = np.clip(cq - WIN_C // 2, 0, GRID_W - WIN_C)
    ok = (kc >= cstart) & (kc < cstart + WIN_C)
    dc = np.clip(kc - cq + WIN_C - 1, 0, 2 * WIN_C - 2)
    b = rel_bias[:, :, dr]
    b = b[..., dc]
    b = jnp.where(jnp.asarray(ok)[None, None, None, None], b, NEG_INF)
    b = jnp.transpose(b, (0, 2, 1, 4, 3, 5))
    return b.reshape(b.shape[0], WIN_R, HEADS, GRID_W, WIN_R * GRID_W)


def _layer_weights(w_in, na_q_norm, na_k_norm, mla_cq_norm, mla_ckv_norm, mla_w_uq, mla_w_ukv,
                   mla_q_norm, mla_k_norm, w_out, pool_w, pool_scale, norm1, norm2,
                   peer_wq, peer_subkeys):
    depth, d, _ = w_in.shape
    na_w = HEADS * NA_DH
    segs = np.cumsum([0, na_w, na_w, na_w, 256, 256, 128, MLA_ROPE])
    part = lambda i: w_in[:, :, segs[i]:segs[i + 1]]
    heads = lambda w: _pad_heads(w.reshape(depth, d, HEADS, NA_DH), NA_DH)
    w_in_p = jnp.concatenate(
        [heads(part(0)), heads(part(1)), heads(part(2)), part(3), part(4), part(5),
         jnp.pad(part(6), ((0, 0), (0, 0), (0, LANE - MLA_ROPE)))], axis=-1).astype(BF16)

    w_uq = _pad_heads(mla_w_uq, MLA_QK).astype(BF16)
    k_nope = _pad_heads(mla_w_ukv[..., :MLA_NOPE], MLA_NOPE)
    eye = np.zeros((MLA_ROPE, HEADS, LANE), np.float32)
    for h in range(HEADS):
        eye[np.arange(MLA_ROPE), h, MLA_NOPE + np.arange(MLA_ROPE)] = 1.0
    eye = jnp.broadcast_to(jnp.asarray(eye.reshape(MLA_ROPE, HW)), (depth, MLA_ROPE, HW))
    zer = jnp.zeros((depth, 256 - 128 - MLA_ROPE, HW), F32)
    w_k = jnp.concatenate([k_nope, eye, zer], axis=1).astype(BF16)
    w_v = jnp.concatenate([_pad_heads(mla_w_ukv[..., MLA_NOPE:], MLA_V),
                           jnp.zeros((depth, 128, HW), F32)], axis=1).astype(BF16)

    mix_w = HEADS * NA_DH
    w_o_na = jnp.pad(w_out[:, :mix_w].reshape(depth, HEADS, NA_DH, d),
                     ((0, 0), (0, 0), (0, LANE - NA_DH), (0, 0))).reshape(depth, HW, d).astype(BF16)
    w_o_pool = w_out[:, mix_w:mix_w + 256].astype(BF16)
    w_o_mla = jnp.pad(w_out[:, mix_w + 256:].reshape(depth, HEADS, MLA_V, d),
                      ((0, 0), (0, 0), (0, LANE - MLA_V), (0, 0))).reshape(depth, HW, d).astype(BF16)
    ng = len(POOL_WINDOWS)
    pw = jnp.zeros((depth, ng * POOL_G, ng * POOL_G), F32)
    for g in range(ng):
        pw = pw.at[:, g * POOL_G:(g + 1) * POOL_G, g * POOL_G:(g + 1) * POOL_G].set(pool_w[:, g])

    half = peer_subkeys.shape[-1]
    sk = jnp.stack([jnp.pad(peer_subkeys[:, 0], ((0, 0), (0, 0), (0, LANE - half))),
                    jnp.pad(peer_subkeys[:, 1], ((0, 0), (0, 0), (LANE - half, 0)))], axis=1).astype(BF16)

    return dict(
        w_in=w_in_p, w_uq=w_uq, w_k=w_k, w_v=w_v,
        g_q=_head_gain(na_q_norm, NA_DH), g_k=_head_gain(na_k_norm, NA_DH),
        g_cq=mla_cq_norm[:, None, :], g_ckv=mla_ckv_norm[:, None, :],
        g_qm=_head_gain(mla_q_norm, MLA_QK), g_km=_head_gain(mla_k_norm, MLA_QK),
        w_o_na=w_o_na, w_o_pool=w_o_pool, w_o_mla=w_o_mla,
        pool_w=pw.astype(BF16), pool_scale=pool_scale[:, None, :],
        norm1=norm1[:, None, :], norm2=norm2[:, None, :],
        peer_wq=peer_wq.astype(BF16), peer_sk=sk)


def kernel(x_prompt, x_sample, c, cache_nat_k, cache_nat_v, cache_mla_ckv, cache_mla_krope, c_ctx, w_mod, b_mod, norm1, norm2, w_in, na_q_norm, na_k_norm, na_rel_bias, pool_w, pool_scale, mla_cq_norm, mla_ckv_norm, mla_w_uq, mla_w_ukv, mla_q_norm, mla_k_norm, w_out, peer_wq, peer_subkeys, peer_u, peer_v):
    batch, seq, d = x_prompt.shape
    db, ds, _ = x_sample.shape
    depth = w_mod.shape[0]
    past = cache_nat_k.shape[2]
    assert seq == TB and ds % TB == 0 and ds % (GRID_W * WIN_R) == 0 and db + 1 <= 8

    cond8 = jnp.concatenate([c_ctx[None, :], c, jnp.zeros((8 - 1 - db, d), F32)], axis=0)
    mod = _modulation(cond8, w_mod, b_mod).reshape(depth, 8, 1, 6 * d)

    lw_all = _layer_weights(w_in, na_q_norm, na_k_norm, mla_cq_norm, mla_ckv_norm, mla_w_uq,
                            mla_w_ukv, mla_q_norm, mla_k_norm, w_out, pool_w, pool_scale,
                            norm1, norm2, peer_wq, peer_subkeys)
    bias_all = _nat_bias(na_rel_bias)
    tables = [_pack_tables(peer_u[l], peer_v[l]) for l in range(depth)]
    cos_lat, sin_lat = _rope_tables(ds)
    cos_ctx = jnp.ones((TB, LANE), F32)
    sin_ctx = jnp.zeros((TB, LANE), F32)

    ck = jnp.concatenate([cache_mla_ckv, cache_mla_krope,
                          jnp.zeros(cache_mla_ckv.shape[:-1] + (256 - 128 - MLA_ROPE,), F32)],
                         axis=-1).astype(BF16)
    kc_mla, vc_mla = _cache_kv(ck, lw_all["w_k"], lw_all["w_v"], lw_all["g_km"])
    kc_na = _pad_heads(cache_nat_k, NA_DH).astype(BF16)
    vc_na = _pad_heads(cache_nat_v, NA_DH).astype(BF16)

    xs = [x_prompt.reshape(batch * seq, d)] + [x_sample[b] for b in range(db)]
    one_row = max(batch * seq, ds) + 1
    lat_bpm = ds // TB
    ks, vs, ckvs, krs = [], [], [], []
    pending = None

    def mix(item):
        si, x1, h2, rows, gt, mod_l = item
        if si in SC_MIX_STREAMS:
            xs[si] = _residual(x1, rows.reshape(x1.shape), mod_l, si)
        else:
            xs[si] = _peer_staged(x1, h2, rows.reshape(-1, LANE), gt, mod_l, si, one_row)

    for l in range(depth):
        lw = {k: v[l] for k, v in lw_all.items()}
        mod_l = mod[l]
        for si in range(db + 1):
            x = xs[si]
            if si == 0:
                (qn, kn, vn, knf, vnf, p, qm, km, vm, ckv, kr) = _in_proj(
                    x, mod_l, 0, one_row, lw, cos_ctx, sin_ctx, 1)
                on, om = _ctx_attn(qn, kn, vn, qm, km, vm, seq)
                x1, h2, ids, gt, gn = _out_proj(x, on, om, p, mod_l, 0, one_row, lw, seq)
                ks.append(knf.reshape(batch, seq, HEADS, LANE)[..., :NA_DH])
                vs.append(vnf.reshape(batch, seq, HEADS, LANE)[..., :NA_DH])
                ckvs.append(ckv.reshape(batch, seq, 128))
                krs.append(kr.reshape(batch, seq, LANE)[..., :MLA_ROPE])
            else:
                b = si - 1
                (qn, kn, vn, _, _, p, qm, km, vm, _, _) = _in_proj(
                    x, mod_l, si, one_row, lw, cos_lat, sin_lat, lat_bpm)
                on = _nat_attn(qn, kn, vn, kc_na[b:b + 1, l], vc_na[b:b + 1, l], bias_all[l], 1)
                om = _lat_mla(qm, km, vm, kc_mla[b:b + 1, l], vc_mla[b:b + 1, l], 1)
                x1, h2, ids, gt, gn = _out_proj(x, on, om, p, mod_l, si, one_row, lw, ds)
            if si in SC_MIX_STREAMS:
                rows = _sc_peer(tables[l], ids.reshape(-1), gn.reshape(-1),
                                h2.reshape(-1, d // LANE, LANE))
            else:
                rows = _sc_gather(tables[l], ids.reshape(-1))
            if pending is not None:
                mix(pending)
            pending = (si, x1, h2, rows, gt, mod_l)
    mix(pending)

    return (xs[0].reshape(batch, seq, d), jnp.stack(xs[1:], axis=0),
            jnp.stack(ks, axis=1), jnp.stack(vs, axis=1),
            jnp.stack(ckvs, axis=1), jnp.stack(krs, axis=1))
```

```python
import functools

import numpy as np
import jax
import jax.numpy as jnp
from jax import lax
from jax.experimental import pallas as pl
from jax.experimental.pallas import tpu as pltpu
from jax.experimental.pallas import tpu_sc as plsc

F32 = jnp.float32
BF16 = jnp.bfloat16

EPS = 1e-6
ROPE_THETA = 10000.0
NEG_INF = -1e30
GRID_W = 64
HEADS = 6
NA_DH = 64
WIN_R = 8
WIN_C = 16
POOL_WINDOWS = (2, 4, 8, 16)
POOL_G = 64
MLA_NOPE = 64
MLA_ROPE = 32
MLA_QK = MLA_NOPE + MLA_ROPE
MLA_V = 64
PEER_HEADS = 8
PEER_NKEYS = 128
PEER_TOPK = 16
LANE = 128
HW = HEADS * LANE
TB = 256
TQ = 256
PEER_TB = 128
PEER_SUB = 8
VMEM_LIMIT = 56 * 1024 * 1024
SC_SHARE = (11, 16)

_CQ, _CK, _CV = 0, HW, 2 * HW
_CP = 3 * HW
_CCQ = _CP + 256
_CCKV = _CCQ + 256
_CKR = _CCKV + 128
IN_W = _CKR + 128


def _params(sem, vmem=VMEM_LIMIT):
    return pltpu.CompilerParams(dimension_semantics=sem, vmem_limit_bytes=vmem)


def _const_spec(shape):
    n = len(shape)
    return pl.BlockSpec(shape, lambda *_: (0,) * n)


def _nt_dot(a, b):
    return lax.dot_general(a, b, (((1,), (1,)), ((), ())), preferred_element_type=F32)


def _mod_kernel(c_ref, w_ref, b_ref, o_ref):
    c = c_ref[...]
    s = c / (1.0 + jnp.exp(-c))
    o_ref[0] = jnp.dot(s, w_ref[0], preferred_element_type=F32,
                       precision=lax.Precision.HIGHEST) + b_ref[0]


def _modulation(cond8, w_mod, b_mod):
    depth, d, n6 = w_mod.shape
    tn = n6 // 4
    return pl.pallas_call(
        _mod_kernel,
        grid=(depth, n6 // tn),
        in_specs=[_const_spec((8, d)),
                  pl.BlockSpec((1, d, tn), lambda l, j: (l, 0, j)),
                  pl.BlockSpec((1, 1, tn), lambda l, j: (l, 0, j))],
        out_specs=pl.BlockSpec((1, 8, tn), lambda l, j: (l, 0, j)),
        out_shape=jax.ShapeDtypeStruct((depth, 8, n6), F32),
        compiler_params=_params(("arbitrary", "arbitrary")),
        name="modulation",
    )(cond8, w_mod, b_mod.reshape(depth, 1, n6))


def _rms(z, gain):
    return z * lax.rsqrt(jnp.mean(z * z, axis=-1, keepdims=True) + EPS) * gain


def _head_rms(zh, gain_h, n_real):
    ms = jnp.sum(zh * zh, axis=-1, keepdims=True) * (1.0 / n_real)
    return zh * lax.rsqrt(ms + EPS) * gain_h


def _rope(zh, cos, sin, is_x1):
    rot = jnp.where(is_x1, pltpu.roll(zh, LANE - 8, 1), pltpu.roll(zh, 8, 1))
    return zh * cos + rot * sin


def _is_x1(rows):
    lane = lax.broadcasted_iota(jnp.int32, (rows, LANE), 1)
    first = jnp.where(lane >= MLA_NOPE, jnp.where(lane < MLA_NOPE + 8, 1, 0), 0)
    second = jnp.where(lane >= MLA_NOPE + 16, jnp.where(lane < MLA_NOPE + 24, 1, 0), 0)
    return (first + second) > 0


def _mla_kv(ck, wk_ref, wv_ref, gk_ref, cos, sin, km_ref, vm_ref):
    rows = ck.shape[0]
    kk = jnp.dot(ck, wk_ref[...], preferred_element_type=F32)
    is_x1 = _is_x1(rows)
    for h in range(HEADS):
        sl = slice(h * LANE, (h + 1) * LANE)
        kh = _head_rms(kk[:, sl], gk_ref[:, sl], MLA_QK)
        km_ref[:, sl] = _rope(kh, cos, sin, is_x1).astype(BF16)
    vm_ref[...] = jnp.dot(ck, wv_ref[...], preferred_element_type=F32).astype(BF16)


def _in_kernel(x_ref, mod_ref, n1_ref, w_ref, wuq_ref, wk_ref, wv_ref,
               gq_ref, gk_ref, gcq_ref, gckv_ref, gqm_ref, gkm_ref, cos_ref, sin_ref,
               qn_ref, kn_ref, vn_ref, knf_ref, vnf_ref, p_ref,
               qm_ref, km_ref, vm_ref, ckv_ref, kr_ref):
    d = x_ref.shape[1]
    rows = x_ref.shape[0]
    mod = mod_ref[0]
    sh1 = mod[:, 0:d]
    sc1 = mod[:, d:2 * d]
    h = _rms(x_ref[...], n1_ref[...]) * (1.0 + sc1) + sh1
    hb = h.astype(BF16)

    def proj(lo, hi):
        return jnp.dot(hb, w_ref[:, lo:hi], preferred_element_type=F32)

    cos = cos_ref[...]
    sin = sin_ref[...]
    is_x1 = _is_x1(rows)

    zq = proj(_CQ, _CQ + HW)
    zk = proj(_CK, _CK + HW)
    for hh in range(HEADS):
        sl = slice(hh * LANE, (hh + 1) * LANE)
        qn_ref[:, sl] = (_head_rms(zq[:, sl], gq_ref[:, sl], NA_DH) * (NA_DH ** -0.5)).astype(BF16)
        kh = _head_rms(zk[:, sl], gk_ref[:, sl], NA_DH)
        knf_ref[:, sl] = kh
        kn_ref[:, sl] = kh.astype(BF16)
    zv = proj(_CV, _CV + HW)
    vnf_ref[...] = zv
    vn_ref[...] = zv.astype(BF16)
    p_ref[...] = proj(_CP, _CP + 256)

    cq = _rms(proj(_CCQ, _CCQ + 256), gcq_ref[...])
    zqm = jnp.dot(cq.astype(BF16), wuq_ref[...], preferred_element_type=F32)
    for hh in range(HEADS):
        sl = slice(hh * LANE, (hh + 1) * LANE)
        qh = _head_rms(zqm[:, sl], gqm_ref[:, sl], MLA_QK)
        qm_ref[:, sl] = (_rope(qh, cos, sin, is_x1) * (MLA_QK ** -0.5)).astype(BF16)

    ckv = _rms(proj(_CCKV, _CCKV + 128), gckv_ref[...])
    kr = proj(_CKR, _CKR + 128)
    ckv_ref[...] = ckv
    kr_ref[...] = kr
    ck = jnp.concatenate([ckv, kr], axis=-1).astype(BF16)
    _mla_kv(ck, wk_ref, wv_ref, gkm_ref, cos, sin, km_ref, vm_ref)


def _in_proj(x, mod_l, row_off, bpm, lw, cos_t, sin_t, rope_blocks):
    n, d = x.shape
    nb = n // TB
    tok = lambda w: pl.BlockSpec((TB, w), lambda i: (i, 0))
    rope_spec = pl.BlockSpec((TB, LANE), lambda i: (i % rope_blocks, 0))
    in_specs = [tok(d),
                pl.BlockSpec((1, 1, mod_l.shape[-1]), lambda i: (row_off + i // bpm, 0, 0)),
                _const_spec((1, d)), _const_spec((d, IN_W)), _const_spec((256, HW)),
                _const_spec((256, HW)), _const_spec((256, HW)),
                _const_spec((1, HW)), _const_spec((1, HW)), _const_spec((1, 256)),
                _const_spec((1, 128)), _const_spec((1, HW)), _const_spec((1, HW)),
                rope_spec, rope_spec]
    widths = [(HW, BF16), (HW, BF16), (HW, BF16), (HW, F32), (HW, F32), (256, F32),
              (HW, BF16), (HW, BF16), (HW, BF16), (128, F32), (128, F32)]
    return pl.pallas_call(
        _in_kernel,
        grid=(nb,),
        in_specs=in_specs,
        out_specs=[tok(w) for w, _ in widths],
        out_shape=[jax.ShapeDtypeStruct((n, w), dt) for w, dt in widths],
        compiler_params=_params(("arbitrary",)),
        name="in_proj",
    )(x, mod_l, lw["norm1"], lw["w_in"], lw["w_uq"], lw["w_k"], lw["w_v"],
      lw["g_q"], lw["g_k"], lw["g_cq"], lw["g_ckv"], lw["g_qm"], lw["g_km"], cos_t, sin_t)


def _cache_kernel(ck_ref, wk_ref, wv_ref, gk_ref, km_ref, vm_ref):
    rows = ck_ref.shape[2]
    cos = jnp.ones((rows, LANE), F32)
    sin = jnp.zeros((rows, LANE), F32)
    _mla_kv(ck_ref[0, 0], wk_ref.at[0], wv_ref.at[0], gk_ref.at[0], cos, sin,
            km_ref.at[0, 0], vm_ref.at[0, 0])


def _cache_kv(ck, w_k, w_v, g_km):
    db, depth, p, _ = ck.shape
    spec = lambda w: pl.BlockSpec((1, 1, p, w), lambda b, l: (b, l, 0, 0))
    wspec = lambda r: pl.BlockSpec((1, r, HW), lambda b, l: (l, 0, 0))
    return pl.pallas_call(
        _cache_kernel,
        grid=(db, depth),
        in_specs=[spec(256), wspec(256), wspec(256), wspec(1)],
        out_specs=[spec(HW), spec(HW)],
        out_shape=[jax.ShapeDtypeStruct((db, depth, p, HW), BF16)] * 2,
        compiler_params=_params(("arbitrary", "arbitrary")),
        name="cache_kv",
    )(ck, w_k, w_v, g_km)


def _softmax_av(s_list, v_list):
    m = s_list[0].max(axis=-1, keepdims=True)
    for s in s_list[1:]:
        m = jnp.maximum(m, s.max(axis=-1, keepdims=True))
    acc = None
    den = None
    for s, v in zip(s_list, v_list):
        p = jnp.exp(s - m)
        l = p.sum(axis=-1, keepdims=True)
        o = jnp.dot(p.astype(BF16), v, preferred_element_type=F32)
        acc = o if acc is None else acc + o
        den = l if den is None else den + l
    return acc / den


def _ctx_attn_kernel(qn, kn, vn, qm, km, vm, on, om):
    for q, k, v, o in ((qn, kn, vn, on), (qm, km, vm, om)):
        for h in range(HEADS):
            sl = slice(h * LANE, (h + 1) * LANE)
            s = _nt_dot(q[:, sl], k[:, sl])
            o[:, sl] = _softmax_av([s], [v[:, sl]]).astype(BF16)


def _ctx_attn(qn, kn, vn, qm, km, vm, seq):
    n = qn.shape[0]
    spec = pl.BlockSpec((seq, HW), lambda i: (i, 0))
    return pl.pallas_call(
        _ctx_attn_kernel,
        grid=(n // seq,),
        in_specs=[spec] * 6,
        out_specs=[spec] * 2,
        out_shape=[jax.ShapeDtypeStruct((n, HW), BF16)] * 2,
        compiler_params=_params(("arbitrary",)),
        name="ctx_attn",
    )(qn, kn, vn, qm, km, vm)


def _lat_mla_kernel(q, k, v, kc, vc, o):
    s1 = _nt_dot(q[...], k[...])
    s2 = _nt_dot(q[...], kc[0])
    o[...] = _softmax_av([s1, s2], [v[...], vc[0]]).astype(BF16)


def _lat_mla(qm, km, vm, kc, vc, db):
    n = qm.shape[0]
    ds = n // db
    nq = ds // TQ
    qspec = pl.BlockSpec((TQ, LANE), lambda b, h, i: (b * nq + i, h))
    kspec = pl.BlockSpec((ds, LANE), lambda b, h, i: (b, h))
    cspec = pl.BlockSpec((1, kc.shape[1], LANE), lambda b, h, i: (b, 0, h))
    return pl.pallas_call(
        _lat_mla_kernel,
        grid=(db, HEADS, nq),
        in_specs=[qspec, kspec, kspec, cspec, cspec],
        out_specs=qspec,
        out_shape=jax.ShapeDtypeStruct((n, HW), BF16),
        compiler_params=_params(("arbitrary",) * 3),
        name="lat_mla",
    )(qm, km, vm, kc, vc)


def _nat_kernel(q, k, v, kc, vc, bias, o, *, rows):
    r = pl.program_id(1)
    rs = jnp.clip(r - WIN_R // 2, 0, rows - WIN_R)
    start = pl.multiple_of(rs * GRID_W, GRID_W)
    band = WIN_R * GRID_W
    for h in range(HEADS):
        sl = slice(h * LANE, (h + 1) * LANE)
        qh = q[:, sl]
        s1 = _nt_dot(qh, k[pl.ds(start, band), sl]) + bias[0, h]
        s2 = _nt_dot(qh, kc[0, :, sl])
        o[:, sl] = _softmax_av([s1, s2], [v[pl.ds(start, band), sl], vc[0, :, sl]]).astype(BF16)


def _nat_attn(qn, kn, vn, kc, vc, bias, db):
    n = qn.shape[0]
    ds = n // db
    rows = ds // GRID_W
    band = WIN_R * GRID_W

    def variant(r):
        return jnp.where(r < WIN_R // 2, r, jnp.where(r > rows - WIN_R // 2, r - (rows - WIN_R), WIN_R // 2))

    qspec = pl.BlockSpec((GRID_W, HW), lambda b, r: (b * rows + r, 0))
    kspec = pl.BlockSpec((ds, HW), lambda b, r: (b, 0))
    cspec = pl.BlockSpec((1, kc.shape[1], HW), lambda b, r: (b, 0, 0))
    bspec = pl.BlockSpec((1, HEADS, GRID_W, band), lambda b, r: (variant(r), 0, 0, 0))
    return pl.pallas_call(
        functools.partial(_nat_kernel, rows=rows),
        grid=(db, rows),
        in_specs=[qspec, kspec, kspec, cspec, cspec, bspec],
        out_specs=qspec,
        out_shape=jax.ShapeDtypeStruct((n, HW), BF16),
        compiler_params=_params(("arbitrary", "arbitrary")),
        name="nat_attn",
    )(qn, kn, vn, kc, vc, bias)


def _split3(x):
    hi = x.astype(BF16)
    r = x - hi.astype(F32)
    mid = r.astype(BF16)
    lo = (r - mid.astype(F32)).astype(BF16)
    return hi, mid, lo


def _pool(p_prev, p_cur, p_next, posb, seq_len):
    rows = p_cur.shape[0]
    halo = p_prev.shape[0]
    ext = rows + 2 * halo
    pext = jnp.concatenate([p_prev, p_cur, p_next], axis=0)
    parts = _split3(pext)
    t = posb + lax.broadcasted_iota(jnp.int32, (rows, ext), 0)
    s = posb - halo + lax.broadcasted_iota(jnp.int32, (rows, ext), 1)
    tcol = posb + lax.broadcasted_iota(jnp.int32, (rows, 1), 0)
    grp = lax.broadcasted_iota(jnp.int32, (rows, 256), 1) // POOL_G
    d = jnp.zeros((rows, 256), F32)
    for gi, w in enumerate(POOL_WINDOWS):
        lo = jnp.maximum(t - w // 2, 0)
        hi = jnp.minimum(t + (w - w // 2), seq_len)
        sel = jnp.where(s >= lo, jnp.where(s < hi, 1.0, 0.0), 0.0).astype(BF16)
        tot = sum(jnp.dot(sel, part, preferred_element_type=F32) for part in parts)
        cnt = (jnp.minimum(tcol + (w - w // 2), seq_len) - jnp.maximum(tcol - w // 2, 0)).astype(F32)
        d = jnp.where(grp == gi, tot / cnt - p_cur, d)
    return d


def _first_max(x, pos, sentinel):
    m = jnp.max(x, axis=0, keepdims=True)
    idx = jnp.min(jnp.where(x == m, pos, sentinel), axis=0, keepdims=True)
    return m, idx


def _topk_head(qh, sk_ref):
    c = qh.shape[0]
    key_pos = lax.broadcasted_iota(jnp.int32, (PEER_NKEYS, c), 0).astype(F32)
    row16 = lax.broadcasted_iota(jnp.int32, (PEER_TOPK, c), 0)
    neg = jnp.float32(-jnp.inf)
    s0 = _nt_dot(sk_ref[0], qh)
    s1 = _nt_dot(sk_ref[1], qh)

    def stage1(a, carry):
        out = []
        for s, sv, si in (carry[0:3], carry[3:6]):
            m, idx = _first_max(s, key_pos, float(PEER_NKEYS))
            out += [jnp.where(key_pos == idx, neg, s),
                    jnp.where(row16 == a, m, sv), jnp.where(row16 == a, idx, si)]
        return tuple(out)

    zf = jnp.zeros((PEER_TOPK, c), F32)
    _, sv0, si0, _, sv1, si1 = lax.fori_loop(0, PEER_TOPK, stage1, (s0, zf, zf, s1, zf, zf))

    sub8 = lax.broadcasted_iota(jnp.int32, (8, c), 0)
    sub8f = sub8.astype(F32)
    cs, ci, cf = [], [], []

    def piece(a_vals, a_ids, a_flat, b_vals, b_ids, b_flat, nb):
        val = a_vals + b_vals
        if nb < 8:
            val = jnp.where(sub8 < nb, val, neg)
        cs.append(val)
        ci.append(a_ids * float(PEER_NKEYS) + b_ids)
        cf.append(jnp.broadcast_to(a_flat * float(PEER_TOPK) + b_flat, (8, c)))

    for a in range(8):
        nb = PEER_TOPK // (a + 1)
        for b0 in range(0, nb, 8):
            piece(sv0[a:a + 1], si0[a:a + 1], float(a), sv1[b0:b0 + 8], si1[b0:b0 + 8],
                  sub8f + float(b0), min(nb - b0, 8))
    piece(sv0[8:16], si0[8:16], sub8f + 8.0, sv1[0:1], si1[0:1], jnp.zeros((8, c), F32), 8)
    npc = len(cs)
    nflat = float(PEER_TOPK * PEER_TOPK)

    def stage2(k, carry):
        vals = list(carry[:npc])
        tv, te = carry[npc], carry[npc + 1]
        m = vals[0]
        for v in vals[1:]:
            m = jnp.maximum(m, v)
        m = jnp.max(m, axis=0, keepdims=True)
        pos = None
        for v, f in zip(vals, cf):
            cand = jnp.where(v == m, f, nflat)
            pos = cand if pos is None else jnp.minimum(pos, cand)
        pos = jnp.min(pos, axis=0, keepdims=True)
        e = None
        for i, f in zip(ci, cf):
            cand = jnp.where(f == pos, i, -1.0)
            e = cand if e is None else jnp.maximum(e, cand)
        e = jnp.max(e, axis=0, keepdims=True)
        vals = [jnp.where(f == pos, neg, v) for v, f in zip(vals, cf)]
        return tuple(vals) + (jnp.where(row16 == k, m, tv), jnp.where(row16 == k, e, te))

    res = lax.fori_loop(0, PEER_TOPK, stage2, tuple(cs) + (zf, zf))
    return res[npc], res[npc + 1]


def _out_kernel(on_ref, om_ref, pc_ref, pp_ref, pn_ref, x_ref, mod_ref,
                won_ref, wop_ref, wom_ref, pw_ref, ps_ref, n2_ref, wq_ref, sk_ref,
                x1_ref, h2_ref, ids_ref, gt_ref, gn_ref, q_scr, idt_scr, *, bps, seq_len):
    d = x_ref.shape[1]
    rows = x_ref.shape[0]
    i = pl.program_id(0)
    mod = mod_ref[0]
    g1 = mod[:, 2 * d:3 * d]
    sh2 = mod[:, 3 * d:4 * d]
    sc2 = mod[:, 4 * d:5 * d]

    posb = (i % bps) * rows
    dpool = _pool(pp_ref[...], pc_ref[...], pn_ref[...], posb, seq_len)
    ypool = jnp.dot(dpool.astype(BF16), pw_ref[...], preferred_element_type=F32) * ps_ref[...]
    mix = (jnp.dot(on_ref[...], won_ref[...], preferred_element_type=F32)
           + jnp.dot(ypool.astype(BF16), wop_ref[...], preferred_element_type=F32)
           + jnp.dot(om_ref[...], wom_ref[...], preferred_element_type=F32))
    x1 = x_ref[...] + g1 * mix
    x1_ref[...] = x1
    h2 = _rms(x1, n2_ref[...]) * (1.0 + sc2) + sh2
    h2_ref[...] = h2

    q = jnp.dot(h2.astype(BF16), wq_ref[...], preferred_element_type=F32)
    for hh in range(PEER_HEADS):
        q_scr[hh] = q[:, hh * LANE:(hh + 1) * LANE].astype(BF16)

    for c0 in range(0, rows, LANE):
        def head(hh, _):
            tv, te = _topk_head(q_scr[hh, c0:c0 + LANE, :], sk_ref)
            ex = jnp.exp(tv - tv[0:1])
            gates = ex / jnp.sum(ex, axis=0, keepdims=True)
            r0 = pl.multiple_of(hh * PEER_TOPK, PEER_TOPK)
            gt_ref[pl.ds(r0, PEER_TOPK), c0:c0 + LANE] = gates
            idt_scr[pl.ds(r0, PEER_TOPK), c0:c0 + LANE] = te
            return 0

        lax.fori_loop(0, PEER_HEADS, head, 0)
    ids_ref[...] = idt_scr[...].T.astype(jnp.int32)
    gn_ref[...] = gt_ref[...].T


def _out_proj(x, on, om, p, mod_l, row_off, bpm, lw, seq_len):
    n, d = x.shape
    nb = n // TB
    bps = seq_len // TB
    halo = 8
    hb = TB // halo
    tok = lambda w: pl.BlockSpec((TB, w), lambda i: (i, 0))
    in_specs = [tok(HW), tok(HW), tok(256),
                pl.BlockSpec((halo, 256), lambda i: (jnp.maximum(i * hb - 1, 0), 0)),
                pl.BlockSpec((halo, 256), lambda i: (jnp.minimum((i + 1) * hb, n // halo - 1), 0)),
                tok(d),
                pl.BlockSpec((1, 1, mod_l.shape[-1]), lambda i: (row_off + i // bpm, 0, 0)),
                _const_spec((HW, d)), _const_spec((256, d)), _const_spec((HW, d)),
                _const_spec((256, 256)), _const_spec((1, 256)), _const_spec((1, d)),
                _const_spec((d, PEER_HEADS * LANE)), _const_spec((2, PEER_NKEYS, LANE))]
    nk = PEER_HEADS * PEER_TOPK
    return pl.pallas_call(
        functools.partial(_out_kernel, bps=bps, seq_len=seq_len),
        grid=(nb,),
        in_specs=in_specs,
        out_specs=[tok(d), tok(d), tok(nk), pl.BlockSpec((nk, TB), lambda i: (0, i)), tok(nk)],
        out_shape=[jax.ShapeDtypeStruct((n, d), F32), jax.ShapeDtypeStruct((n, d), F32),
                   jax.ShapeDtypeStruct((n, nk), jnp.int32), jax.ShapeDtypeStruct((nk, n), F32),
                   jax.ShapeDtypeStruct((n, nk), F32)],
        scratch_shapes=[pltpu.VMEM((PEER_HEADS, TB, LANE), BF16), pltpu.VMEM((nk, TB), F32)],
        compiler_params=_params(("arbitrary",)),
        name="out_proj",
    )(on, om, p, p, p, x, mod_l, lw["w_o_na"], lw["w_o_pool"], lw["w_o_mla"],
      lw["pool_w"], lw["pool_scale"], lw["norm2"], lw["peer_wq"], lw["peer_sk"])


def _gelu_tanh(x):
    return x * (0.5 * (1.0 + jnp.tanh(0.7978845608028654 * (x + 0.044715 * (x * x * x)))))


def _peer_token_mix(chunk, hrow, gcol, ch):
    acc = None
    for s in range(ch):
        us = lax.bitcast_convert_type(chunk(s) & jnp.int32(-65536), F32)
        term = us * hrow[:, s * LANE:(s + 1) * LANE]
        acc = term if acc is None else acc + term
    wgt = gcol * _gelu_tanh(jnp.sum(acc, axis=-1, keepdims=True))
    parts = []
    for s in range(ch):
        vs = lax.bitcast_convert_type(chunk(s) << 16, F32)
        parts.append(jnp.sum(vs * wgt, axis=0, keepdims=True))
    return jnp.concatenate(parts, axis=-1)


def _peer_staged_kernel(rows_ref, gt_ref, h2_ref, x1_ref, mod_ref, o_ref):
    d = x1_ref.shape[1]
    ch = d // LANE
    nk = gt_ref.shape[0]
    g2 = mod_ref[0][:, 5 * d:6 * d]
    tok_lane = lax.broadcasted_iota(jnp.int32, gt_ref.shape, 1)
    base = (pl.program_id(0) % (PEER_TB // PEER_SUB)) * PEER_SUB
    h8 = h2_ref[...]
    ys = []
    for t in range(PEER_SUB):
        chunk = lambda s: rows_ref[pl.ds(t * nk * ch + s, nk, stride=ch), :]
        gcol = jnp.sum(jnp.where(tok_lane == base + t, gt_ref[...], 0.0), axis=-1, keepdims=True)
        ys.append(_peer_token_mix(chunk, h8[t:t + 1, :], gcol, ch))
    o_ref[...] = x1_ref[...] + g2 * jnp.concatenate(ys, axis=0)


def _peer_staged(x1, h2, rows, gt, mod_l, row_off, tpm):
    n, d = x1.shape
    nk = gt.shape[0]
    per = PEER_SUB * nk * (d // LANE)
    sub_per_tb = PEER_TB // PEER_SUB
    tok = pl.BlockSpec((PEER_SUB, d), lambda j: (j, 0))
    return pl.pallas_call(
        _peer_staged_kernel,
        grid=(n // PEER_SUB,),
        in_specs=[pl.BlockSpec((per, LANE), lambda j: (j, 0)),
                  pl.BlockSpec((nk, PEER_TB), lambda j: (0, j // sub_per_tb)),
                  tok, tok,
                  pl.BlockSpec((1, 1, mod_l.shape[-1]), lambda j: (row_off + (j * PEER_SUB) // tpm, 0, 0))],
        out_specs=tok,
        out_shape=jax.ShapeDtypeStruct((n, d), F32),
        compiler_params=_params(("arbitrary",)),
        name="peer_staged",
    )(rows, gt, h2, x1, mod_l)


def _sc_gather(table3, ids_flat):
    m = ids_flat.shape[0]
    _, ch, lane = table3.shape
    info = plsc.get_sparse_core_info()
    nc, nw = info.num_cores, info.num_cores * info.num_subcores
    idx_win = 128
    win = 32
    per_w = m // nw
    assert m % (nw * idx_win) == 0
    mesh = plsc.VectorSubcoreMesh(core_axis_name="core", subcore_axis_name="subcore")

    @functools.partial(
        pl.kernel, mesh=mesh,
        out_type=jax.ShapeDtypeStruct((m, ch, lane), table3.dtype),
        scratch_types=[pltpu.VMEM((idx_win,), jnp.int32),
                       pltpu.VMEM((win, ch, lane), table3.dtype),
                       pltpu.VMEM((win, ch, lane), table3.dtype),
                       pltpu.SemaphoreType.DMA, pltpu.SemaphoreType.DMA])
    def gather(tab_hbm, idx_hbm, out_hbm, idx_v, rows_a, rows_b, sem_a, sem_b):
        wid = lax.axis_index("subcore") * nc + lax.axis_index("core")
        bufs = ((rows_a, sem_a), (rows_b, sem_b))
        nq = idx_win // win

        def fetch(q):
            rows, sem = bufs[q % 2]
            return pltpu.make_async_copy(tab_hbm.at[idx_v.at[pl.ds(q * win, win)]], rows, sem)

        @pl.loop(0, per_w // idx_win)
        def _(g):
            base = pl.multiple_of(wid * per_w + g * idx_win, idx_win)
            pltpu.sync_copy(idx_hbm.at[pl.ds(base, idx_win)], idx_v)
            fetch(0).start()
            for q in range(nq):
                fetch(q).wait()
                if q + 1 < nq:
                    fetch(q + 1).start()
                pltpu.sync_copy(bufs[q % 2][0], out_hbm.at[pl.ds(base + q * win, win)])

    return gather(table3, ids_flat)


def _sc_peer(table3, ids_flat, gates_flat, h2, n):
    _, ch, lane = h2.shape
    nk = ids_flat.shape[0] // h2.shape[0]
    info = plsc.get_sparse_core_info()
    nc, nw, nl = info.num_cores, info.num_cores * info.num_subcores, info.num_lanes
    tpw = n // nw
    win = 32
    nq = nk // win
    cpr = lane // nl
    nchunk = ch * cpr
    hc = nchunk // 2
    assert n % nw == 0 and nk % win == 0 and win % nl == 0
    mesh = plsc.VectorSubcoreMesh(core_axis_name="core", subcore_axis_name="subcore")
    hi_mask = jnp.int32(-65536)

    @functools.partial(
        pl.kernel, mesh=mesh,
        out_type=jax.ShapeDtypeStruct((n, ch, lane), F32),
        compiler_params=pltpu.CompilerParams(needs_layout_passes=False),
        scratch_types=[pltpu.VMEM((nk,), jnp.int32), pltpu.VMEM((nk,), F32),
                       pltpu.VMEM((ch, lane), F32), pltpu.VMEM((ch, lane), F32),
                       pltpu.VMEM((win, ch, lane), jnp.int32), pltpu.VMEM((win, ch, lane), jnp.int32),
                       pltpu.VMEM((win * nl,), F32), pltpu.VMEM((win,), F32),
                       pltpu.SemaphoreType.DMA, pltpu.SemaphoreType.DMA])
    def peer(tab_hbm, ids_hbm, g_hbm, h2_hbm, y_hbm,
             idx_v, g_v, x_v, y_v, rows_a, rows_b, part_v, w_v, sem_a, sem_b):
        wid = lax.axis_index("subcore") * nc + lax.axis_index("core")
        bufs = ((rows_a, sem_a), (rows_b, sem_b))
        lanes = lax.iota(jnp.int32, nl)
        zero = jnp.zeros((nl,), F32)

        def fetch(q):
            rows, sem = bufs[q % 2]
            return pltpu.make_async_copy(tab_hbm.at[idx_v.at[pl.ds(q * win, win)]], rows, sem)

        def word(rows, r, cc):
            return rows[r, cc // cpr, pl.ds((cc % cpr) * nl, nl)]

        @pl.loop(0, tpw)
        def _(ti):
            tok = wid * tpw + ti
            off = pl.multiple_of(tok * nk, nk)
            pltpu.sync_copy(ids_hbm.at[pl.ds(off, nk)], idx_v)
            pltpu.sync_copy(g_hbm.at[pl.ds(off, nk)], g_v)
            pltpu.sync_copy(h2_hbm.at[tok], x_v)
            for cc in range(nchunk):
                y_v[cc // cpr, pl.ds((cc % cpr) * nl, nl)] = zero
            fetch(0).start()
            for q in range(nq):
                rows = bufs[q % 2][0]
                fetch(q).wait()
                if q + 1 < nq:
                    fetch(q + 1).start()

                for half in range(2):
                    xs = [x_v[(half * hc + c) // cpr, pl.ds(((half * hc + c) % cpr) * nl, nl)]
                          for c in range(hc)]

                    @pl.loop(0, win)
                    def _(r):
                        accs = [None] * 4
                        for c in range(hc):
                            u = lax.bitcast_convert_type(word(rows, r, half * hc + c) & hi_mask, F32)
                            t = u * xs[c]
                            accs[c % 4] = t if accs[c % 4] is None else accs[c % 4] + t
                        acc = (accs[0] + accs[1]) + (accs[2] + accs[3])
                        po = pl.multiple_of(r * nl, nl)
                        if half == 0:
                            part_v[pl.ds(po, nl)] = acc
                        else:
                            part_v[pl.ds(po, nl)] = part_v[pl.ds(po, nl)] + acc

                for grp in range(win // nl):
                    s = zero
                    for rr in range(nl):
                        tot = jnp.sum(part_v[pl.ds((grp * nl + rr) * nl, nl)])
                        s = jnp.where(lanes == rr, tot, s)
                    z = 0.7978845608028654 * (s + 0.044715 * (s * s * s))
                    tanh = 1.0 - 2.0 / (jnp.exp(2.0 * z) + 1.0)
                    gate = g_v[pl.ds(q * win + grp * nl, nl)]
                    w_v[pl.ds(grp * nl, nl)] = gate * (s * (0.5 * (1.0 + tanh)))

                for half in range(2):
                    def body(r, yacc):
                        wr = plsc.load_gather(w_v, [jnp.full((nl,), r, jnp.int32)])
                        out = []
                        for c in range(hc):
                            v = lax.bitcast_convert_type(word(rows, r, half * hc + c) << 16, F32)
                            out.append(yacc[c] + wr * v)
                        return tuple(out)

                    yacc = lax.fori_loop(0, win, body, tuple(zero for _ in range(hc)))
                    for c in range(hc):
                        cc = half * hc + c
                        sl = (cc // cpr, pl.ds((cc % cpr) * nl, nl))
                        y_v[sl] = y_v[sl] + yacc[c]
            pltpu.sync_copy(y_v, y_hbm.at[tok])

    return peer(table3, ids_flat, gates_flat, h2)


def _residual_kernel(x1_ref, y_ref, mod_ref, x2_hbm, o_ref):
    del x2_hbm
    d = x1_ref.shape[1]
    o_ref[...] = x1_ref[...] + mod_ref[0][:, 5 * d:6 * d] * y_ref[...]


def _residual(x1, y, mod_l, row, x2):
    n, d = x1.shape
    tok = pl.BlockSpec((PEER_TB, d), lambda i: (i, 0))
    return pl.pallas_call(
        _residual_kernel,
        grid=(y.shape[0] // PEER_TB,),
        in_specs=[tok, tok, pl.BlockSpec((1, 1, mod_l.shape[-1]), lambda i: (row, 0, 0)),
                  pl.BlockSpec(memory_space=pl.ANY)],
        out_specs=tok,
        out_shape=jax.ShapeDtypeStruct((n, d), F32),
        input_output_aliases={3: 0},
        compiler_params=_params(("arbitrary",)),
        name="residual",
    )(x1, y, mod_l, x2)


def _peer_kernel(ids_hbm, gt_ref, h2_ref, x1_ref, mod_ref, tab_hbm, o_ref,
                 ids_s, buf, sem_i, sem_r, *, first_block):
    d = x1_ref.shape[1]
    ch = d // LANE
    pitch = ch + 1
    nsub = x1_ref.shape[0] // PEER_SUB
    nk = gt_ref.shape[0]
    nids = PEER_SUB * nk
    i = pl.program_id(0) + first_block
    g2 = mod_ref[0][:, 5 * d:6 * d]
    tok_lane = lax.broadcasted_iota(jnp.int32, gt_ref.shape, 1)

    def ids_copy(j, slot):
        start = pl.multiple_of((i * nsub + j) * nids, nids)
        return pltpu.make_async_copy(ids_hbm.at[pl.ds(start, nids)],
                                     ids_s.at[pl.ds(slot * nids, nids)], sem_i.at[slot])

    def row_copy(slot, e, f):
        src = tab_hbm.at[pl.ds(pl.multiple_of(e * ch, ch), ch), :]
        dst = buf.at[slot, pl.ds(f * pitch, ch), :]
        return pltpu.make_async_copy(src, dst, sem_r.at[slot])

    def issue_rows(slot):
        for t in range(PEER_SUB):
            def body(kk, _):
                for r in range(8):
                    f = t * nk + kk * 8 + r
                    row_copy(slot, ids_s[slot * nids + f], f).start(priority=r % 2)
                return 0

            lax.fori_loop(0, nk // 8, body, 0)

    def wait_rows(slot):
        done = buf.at[slot, pl.ds(0, nids * ch), :]
        pltpu.make_async_copy(done, done, sem_r.at[slot]).wait()

    def compute(slot, j):
        base = pl.multiple_of(j * PEER_SUB, PEER_SUB)
        h8 = h2_ref[pl.ds(base, PEER_SUB), :]
        ys = []
        for t in range(PEER_SUB):
            chunk = lambda s: buf[slot, pl.ds(t * nk * pitch + s, nk, stride=pitch), :]
            gcol = jnp.sum(jnp.where(tok_lane == base + t, gt_ref[...], 0.0), axis=-1, keepdims=True)
            ys.append(_peer_token_mix(chunk, h8[t:t + 1, :], gcol, ch))
        y8 = jnp.concatenate(ys, axis=0)
        o_ref[pl.ds(base, PEER_SUB), :] = x1_ref[pl.ds(base, PEER_SUB), :] + g2 * y8

    first = ids_copy(0, 0)
    first.start()
    first.wait()
    issue_rows(0)
    ids_copy(1, 1).start()

    def pair(jj, _):
        j0 = 2 * jj
        ids_copy(j0 + 1, 1).wait()
        issue_rows(1)

        @pl.when(j0 + 2 < nsub)
        def _():
            ids_copy(j0 + 2, 0).start()

        wait_rows(0)
        compute(0, j0)

        @pl.when(j0 + 2 < nsub)
        def _():
            ids_copy(j0 + 2, 0).wait()
            issue_rows(0)

        @pl.when(j0 + 3 < nsub)
        def _():
            ids_copy(j0 + 3, 1).start()

        wait_rows(1)
        compute(1, j0 + 1)
        return 0

    lax.fori_loop(0, nsub // 2, pair, 0)


def _pack_tables(peer_u, peer_v):
    e, d = peer_u.shape
    ub = lax.bitcast_convert_type(peer_u.astype(BF16), jnp.uint16).astype(jnp.uint32)
    vb = lax.bitcast_convert_type(peer_v.astype(BF16), jnp.uint16).astype(jnp.uint32)
    words = lax.bitcast_convert_type((ub << 16) | vb, jnp.int32)
    return words.reshape(e, d // LANE, LANE)


def _peer(x1, h2, ids, gt, mod_l, row, table, tok0):
    n, d = x1.shape
    nk = gt.shape[0]
    b0 = tok0 // PEER_TB
    nb = n // PEER_TB - b0
    tok = pl.BlockSpec((PEER_TB, d), lambda i: (i + b0, 0))
    any_spec = pl.BlockSpec(memory_space=pl.ANY)
    return pl.pallas_call(
        functools.partial(_peer_kernel, first_block=b0),
        grid=(nb,),
        in_specs=[any_spec,
                  pl.BlockSpec((nk, PEER_TB), lambda i: (0, i + b0)),
                  tok, tok,
                  pl.BlockSpec((1, 1, mod_l.shape[-1]), lambda i: (row, 0, 0)),
                  any_spec],
        out_specs=tok,
        out_shape=jax.ShapeDtypeStruct((n, d), F32),
        scratch_shapes=[pltpu.SMEM((2 * PEER_SUB * nk,), jnp.int32),
                        pltpu.VMEM((2, PEER_SUB * nk * (d // LANE + 1), LANE), jnp.int32),
                        pltpu.SemaphoreType.DMA((2,)),
                        pltpu.SemaphoreType.DMA((2,))],
        compiler_params=_params(("arbitrary",)),
        name="peer",
    )(ids.reshape(n * nk), gt, h2, x1, mod_l, table.reshape(-1, LANE))


def _pad_heads(w, width):
    pad = [(0, 0)] * (w.ndim - 1) + [(0, LANE - width)]
    w = jnp.pad(w, pad)
    return w.reshape(w.shape[:-2] + (HW,))


def _head_gain(g, width):
    depth = g.shape[0]
    g = jnp.pad(g, ((0, 0), (0, LANE - width)))
    return jnp.tile(g, (1, HEADS)).reshape(depth, 1, HW)


def _rope_tables(seq):
    t = np.arange(seq)
    half = MLA_ROPE // 2
    inv = ROPE_THETA ** (-np.arange(0, half, 2, dtype=np.float32) / half)
    cos = np.ones((seq, LANE), np.float32)
    sin = np.zeros((seq, LANE), np.float32)
    for off, pos in ((MLA_NOPE, t // GRID_W), (MLA_NOPE + half, t % GRID_W)):
        ang = pos.astype(np.float32)[:, None] * inv[None, :]
        q = half // 2
        cos[:, off:off + q] = np.cos(ang)
        cos[:, off + q:off + half] = np.cos(ang)
        sin[:, off:off + q] = -np.sin(ang)
        sin[:, off + q:off + half] = np.sin(ang)
    return jnp.asarray(cos), jnp.asarray(sin)


def _nat_bias(rel_bias):
    v = np.arange(WIN_R)[:, None]
    j = np.arange(WIN_R)[None, :]
    dr = j - v + WIN_R - 1
    cq = np.arange(GRID_W)[:, None]
    kc = np.arange(GRID_W)[None, :]
    cstart = np.clip(cq - WIN_C // 2, 0, GRID_W - WIN_C)
    ok = (kc >= cstart) & (kc < cstart + WIN_C)
    dc = np.clip(kc - cq + WIN_C - 1, 0, 2 * WIN_C - 2)
    b = rel_bias[:, :, dr]
    b = b[..., dc]
    b = jnp.where(jnp.asarray(ok)[None, None, None, None], b, NEG_INF)
    b = jnp.transpose(b, (0, 2, 1, 4, 3, 5))
    return b.reshape(b.shape[0], WIN_R, HEADS, GRID_W, WIN_R * GRID_W)


def _layer_weights(w_in, na_q_norm, na_k_norm, mla_cq_norm, mla_ckv_norm, mla_w_uq, mla_w_ukv,
                   mla_q_norm, mla_k_norm, w_out, pool_w, pool_scale, norm1, norm2,
                   peer_wq, peer_subkeys):
    depth, d, _ = w_in.shape
    na_w = HEADS * NA_DH
    segs = np.cumsum([0, na_w, na_w, na_w, 256, 256, 128, MLA_ROPE])
    part = lambda i: w_in[:, :, segs[i]:segs[i + 1]]
    heads = lambda w: _pad_heads(w.reshape(depth, d, HEADS, NA_DH), NA_DH)
    w_in_p = jnp.concatenate(
        [heads(part(0)), heads(part(1)), heads(part(2)), part(3), part(4), part(5),
         jnp.pad(part(6), ((0, 0), (0, 0), (0, LANE - MLA_ROPE)))], axis=-1).astype(BF16)

    w_uq = _pad_heads(mla_w_uq, MLA_QK).astype(BF16)
    k_nope = _pad_heads(mla_w_ukv[..., :MLA_NOPE], MLA_NOPE)
    eye = np.zeros((MLA_ROPE, HEADS, LANE), np.float32)
    for h in range(HEADS):
        eye[np.arange(MLA_ROPE), h, MLA_NOPE + np.arange(MLA_ROPE)] = 1.0
    eye = jnp.broadcast_to(jnp.asarray(eye.reshape(MLA_ROPE, HW)), (depth, MLA_ROPE, HW))
    zer = jnp.zeros((depth, 256 - 128 - MLA_ROPE, HW), F32)
    w_k = jnp.concatenate([k_nope, eye, zer], axis=1).astype(BF16)
    w_v = jnp.concatenate([_pad_heads(mla_w_ukv[..., MLA_NOPE:], MLA_V),
                           jnp.zeros((depth, 128, HW), F32)], axis=1).astype(BF16)

    mix_w = HEADS * NA_DH
    w_o_na = jnp.pad(w_out[:, :mix_w].reshape(depth, HEADS, NA_DH, d),
                     ((0, 0), (0, 0), (0, LANE - NA_DH), (0, 0))).reshape(depth, HW, d).astype(BF16)
    w_o_pool = w_out[:, mix_w:mix_w + 256].astype(BF16)
    w_o_mla = jnp.pad(w_out[:, mix_w + 256:].reshape(depth, HEADS, MLA_V, d),
                      ((0, 0), (0, 0), (0, LANE - MLA_V), (0, 0))).reshape(depth, HW, d).astype(BF16)
    ng = len(POOL_WINDOWS)
    pw = jnp.zeros((depth, ng * POOL_G, ng * POOL_G), F32)
    for g in range(ng):
        pw = pw.at[:, g * POOL_G:(g + 1) * POOL_G, g * POOL_G:(g + 1) * POOL_G].set(pool_w[:, g])

    half = peer_subkeys.shape[-1]
    sk = jnp.stack([jnp.pad(peer_subkeys[:, 0], ((0, 0), (0, 0), (0, LANE - half))),
                    jnp.pad(peer_subkeys[:, 1], ((0, 0), (0, 0), (LANE - half, 0)))], axis=1).astype(BF16)

    return dict(
        w_in=w_in_p, w_uq=w_uq, w_k=w_k, w_v=w_v,
        g_q=_head_gain(na_q_norm, NA_DH), g_k=_head_gain(na_k_norm, NA_DH),
        g_cq=mla_cq_norm[:, None, :], g_ckv=mla_ckv_norm[:, None, :],
        g_qm=_head_gain(mla_q_norm, MLA_QK), g_km=_head_gain(mla_k_norm, MLA_QK),
        w_o_na=w_o_na, w_o_pool=w_o_pool, w_o_mla=w_o_mla,
        pool_w=pw.astype(BF16), pool_scale=pool_scale[:, None, :],
        norm1=norm1[:, None, :], norm2=norm2[:, None, :],
        peer_wq=peer_wq.astype(BF16), peer_sk=sk)


def kernel(x_prompt, x_sample, c, cache_nat_k, cache_nat_v, cache_mla_ckv, cache_mla_krope, c_ctx, w_mod, b_mod, norm1, norm2, w_in, na_q_norm, na_k_norm, na_rel_bias, pool_w, pool_scale, mla_cq_norm, mla_ckv_norm, mla_w_uq, mla_w_ukv, mla_q_norm, mla_k_norm, w_out, peer_wq, peer_subkeys, peer_u, peer_v):
    batch, seq, d = x_prompt.shape
    db, ds, _ = x_sample.shape
    depth = w_mod.shape[0]
    past = cache_nat_k.shape[2]
    assert seq == TB and ds % TB == 0 and ds % (GRID_W * WIN_R) == 0 and db + 1 <= 8

    cond8 = jnp.concatenate([c_ctx[None, :], c, jnp.zeros((8 - 1 - db, d), F32)], axis=0)
    mod = _modulation(cond8, w_mod, b_mod).reshape(depth, 8, 1, 6 * d)

    lw_all = _layer_weights(w_in, na_q_norm, na_k_norm, mla_cq_norm, mla_ckv_norm, mla_w_uq,
                            mla_w_ukv, mla_q_norm, mla_k_norm, w_out, pool_w, pool_scale,
                            norm1, norm2, peer_wq, peer_subkeys)
    bias_all = _nat_bias(na_rel_bias)
    tables = [_pack_tables(peer_u[l], peer_v[l]) for l in range(depth)]
    cos_lat, sin_lat = _rope_tables(ds)
    cos_ctx = jnp.ones((TB, LANE), F32)
    sin_ctx = jnp.zeros((TB, LANE), F32)

    ck = jnp.concatenate([cache_mla_ckv, cache_mla_krope,
                          jnp.zeros(cache_mla_ckv.shape[:-1] + (256 - 128 - MLA_ROPE,), F32)],
                         axis=-1).astype(BF16)
    kc_mla, vc_mla = _cache_kv(ck, lw_all["w_k"], lw_all["w_v"], lw_all["g_km"])
    kc_na = _pad_heads(cache_nat_k, NA_DH).astype(BF16)
    vc_na = _pad_heads(cache_nat_v, NA_DH).astype(BF16)

    xs = [x_prompt.reshape(batch * seq, d)] + [x_sample[b] for b in range(db)]
    one_row = max(batch * seq, ds) + 1
    lat_bpm = ds // TB
    ks, vs, ckvs, krs = [], [], [], []
    pending = None

    def join(item):
        si, x1, y_sc, x2, mod_l = item
        xs[si] = _residual(x1, y_sc.reshape(-1, d), mod_l, si, x2)

    for l in range(depth):
        lw = {k: v[l] for k, v in lw_all.items()}
        mod_l = mod[l]
        for si in range(db + 1):
            x = xs[si]
            if si == 0:
                (qn, kn, vn, knf, vnf, p, qm, km, vm, ckv, kr) = _in_proj(
                    x, mod_l, 0, one_row, lw, cos_ctx, sin_ctx, 1)
                on, om = _ctx_attn(qn, kn, vn, qm, km, vm, seq)
                x1, h2, ids, gt, gn = _out_proj(x, on, om, p, mod_l, 0, one_row, lw, seq)
                ks.append(knf.reshape(batch, seq, HEADS, LANE)[..., :NA_DH])
                vs.append(vnf.reshape(batch, seq, HEADS, LANE)[..., :NA_DH])
                ckvs.append(ckv.reshape(batch, seq, 128))
                krs.append(kr.reshape(batch, seq, LANE)[..., :MLA_ROPE])
            else:
                b = si - 1
                (qn, kn, vn, _, _, p, qm, km, vm, _, _) = _in_proj(
                    x, mod_l, si, one_row, lw, cos_lat, sin_lat, lat_bpm)
                on = _nat_attn(qn, kn, vn, kc_na[b:b + 1, l], vc_na[b:b + 1, l], bias_all[l], 1)
                om = _lat_mla(qm, km, vm, kc_mla[b:b + 1, l], vc_mla[b:b + 1, l], 1)
                x1, h2, ids, gt, gn = _out_proj(x, on, om, p, mod_l, si, one_row, lw, ds)
            n_sc = x.shape[0] * SC_SHARE[0] // SC_SHARE[1] // PEER_TB * PEER_TB
            y_sc = _sc_peer(tables[l], ids.reshape(-1), gn.reshape(-1),
                            h2.reshape(-1, d // LANE, LANE), n_sc)
            x2 = _peer(x1, h2, ids, gt, mod_l, si, tables[l], n_sc)
            if pending is not None:
                join(pending)
            pending = (si, x1, y_sc, x2, mod_l)
    join(pending)

    return (xs[0].reshape(batch, seq, d), jnp.stack(xs[1:], axis=0),
            jnp.stack(ks, axis=1), jnp.stack(vs, axis=1),
            jnp.stack(ckvs, axis=1), jnp.stack(krs, axis=1))
```

```python
import functools

import numpy as np
import jax
import jax.numpy as jnp
from jax import lax
from jax.experimental import pallas as pl
from jax.experimental.pallas import tpu as pltpu
from jax.experimental.pallas import tpu_sc as plsc

F32 = jnp.float32
BF16 = jnp.bfloat16

EPS = 1e-6
ROPE_THETA = 10000.0
NEG_INF = -1e30
GRID_W = 64
HEADS = 6
NA_DH = 64
WIN_R = 8
WIN_C = 16
POOL_WINDOWS = (2, 4, 8, 16)
POOL_G = 64
MLA_NOPE = 64
MLA_ROPE = 32
MLA_QK = MLA_NOPE + MLA_ROPE
MLA_V = 64
PEER_HEADS = 8
PEER_NKEYS = 128
PEER_TOPK = 16
LANE = 128
HW = HEADS * LANE
TB = 256
TQ = 256
PEER_TB = 128
PEER_SUB = 8
VMEM_LIMIT = 56 * 1024 * 1024
SC_SHARE = (11, 16)

_CQ, _CK, _CV = 0, HW, 2 * HW
_CP = 3 * HW
_CCQ = _CP + 256
_CCKV = _CCQ + 256
_CKR = _CCKV + 128
IN_W = _CKR + 128


def _params(sem, vmem=VMEM_LIMIT):
    return pltpu.CompilerParams(dimension_semantics=sem, vmem_limit_bytes=vmem)


def _const_spec(shape):
    n = len(shape)
    return pl.BlockSpec(shape, lambda *_: (0,) * n)


def _nt_dot(a, b):
    return lax.dot_general(a, b, (((1,), (1,)), ((), ())), preferred_element_type=F32)


def _mod_kernel(c_ref, w_ref, b_ref, o_ref):
    c = c_ref[...]
    s = c / (1.0 + jnp.exp(-c))
    o_ref[0] = jnp.dot(s, w_ref[0], preferred_element_type=F32,
                       precision=lax.Precision.HIGHEST) + b_ref[0]


def _modulation(cond8, w_mod, b_mod):
    depth, d, n6 = w_mod.shape
    tn = n6 // 4
    return pl.pallas_call(
        _mod_kernel,
        grid=(depth, n6 // tn),
        in_specs=[_const_spec((8, d)),
                  pl.BlockSpec((1, d, tn), lambda l, j: (l, 0, j)),
                  pl.BlockSpec((1, 1, tn), lambda l, j: (l, 0, j))],
        out_specs=pl.BlockSpec((1, 8, tn), lambda l, j: (l, 0, j)),
        out_shape=jax.ShapeDtypeStruct((depth, 8, n6), F32),
        compiler_params=_params(("arbitrary", "arbitrary")),
        name="modulation",
    )(cond8, w_mod, b_mod.reshape(depth, 1, n6))


def _rms(z, gain):
    return z * lax.rsqrt(jnp.mean(z * z, axis=-1, keepdims=True) + EPS) * gain


def _head_rms(zh, gain_h, n_real):
    ms = jnp.sum(zh * zh, axis=-1, keepdims=True) * (1.0 / n_real)
    return zh * lax.rsqrt(ms + EPS) * gain_h


def _rope(zh, cos, sin, is_x1):
    rot = jnp.where(is_x1, pltpu.roll(zh, LANE - 8, 1), pltpu.roll(zh, 8, 1))
    return zh * cos + rot * sin


def _is_x1(rows):
    lane = lax.broadcasted_iota(jnp.int32, (rows, LANE), 1)
    first = jnp.where(lane >= MLA_NOPE, jnp.where(lane < MLA_NOPE + 8, 1, 0), 0)
    second = jnp.where(lane >= MLA_NOPE + 16, jnp.where(lane < MLA_NOPE + 24, 1, 0), 0)
    return (first + second) > 0


def _mla_kv(ck, wk_ref, wv_ref, gk_ref, cos, sin, km_ref, vm_ref):
    rows = ck.shape[0]
    kk = jnp.dot(ck, wk_ref[...], preferred_element_type=F32)
    is_x1 = _is_x1(rows)
    for h in range(HEADS):
        sl = slice(h * LANE, (h + 1) * LANE)
        kh = _head_rms(kk[:, sl], gk_ref[:, sl], MLA_QK)
        km_ref[:, sl] = _rope(kh, cos, sin, is_x1).astype(BF16)
    vm_ref[...] = jnp.dot(ck, wv_ref[...], preferred_element_type=F32).astype(BF16)


def _in_kernel(x_ref, mod_ref, n1_ref, w_ref, wuq_ref, wk_ref, wv_ref,
               gq_ref, gk_ref, gcq_ref, gckv_ref, gqm_ref, gkm_ref, cos_ref, sin_ref, after_hbm,
               qn_ref, kn_ref, vn_ref, knf_ref, vnf_ref, p_ref,
               qm_ref, km_ref, vm_ref, ckv_ref, kr_ref):
    del after_hbm
    d = x_ref.shape[1]
    rows = x_ref.shape[0]
    mod = mod_ref[0]
    sh1 = mod[:, 0:d]
    sc1 = mod[:, d:2 * d]
    h = _rms(x_ref[...], n1_ref[...]) * (1.0 + sc1) + sh1
    hb = h.astype(BF16)

    def proj(lo, hi):
        return jnp.dot(hb, w_ref[:, lo:hi], preferred_element_type=F32)

    cos = cos_ref[...]
    sin = sin_ref[...]
    is_x1 = _is_x1(rows)

    zq = proj(_CQ, _CQ + HW)
    zk = proj(_CK, _CK + HW)
    for hh in range(HEADS):
        sl = slice(hh * LANE, (hh + 1) * LANE)
        qn_ref[:, sl] = (_head_rms(zq[:, sl], gq_ref[:, sl], NA_DH) * (NA_DH ** -0.5)).astype(BF16)
        kh = _head_rms(zk[:, sl], gk_ref[:, sl], NA_DH)
        knf_ref[:, sl] = kh
        kn_ref[:, sl] = kh.astype(BF16)
    zv = proj(_CV, _CV + HW)
    vnf_ref[...] = zv
    vn_ref[...] = zv.astype(BF16)
    p_ref[...] = proj(_CP, _CP + 256)

    cq = _rms(proj(_CCQ, _CCQ + 256), gcq_ref[...])
    zqm = jnp.dot(cq.astype(BF16), wuq_ref[...], preferred_element_type=F32)
    for hh in range(HEADS):
        sl = slice(hh * LANE, (hh + 1) * LANE)
        qh = _head_rms(zqm[:, sl], gqm_ref[:, sl], MLA_QK)
        qm_ref[:, sl] = (_rope(qh, cos, sin, is_x1) * (MLA_QK ** -0.5)).astype(BF16)

    ckv = _rms(proj(_CCKV, _CCKV + 128), gckv_ref[...])
    kr = proj(_CKR, _CKR + 128)
    ckv_ref[...] = ckv
    kr_ref[...] = kr
    ck = jnp.concatenate([ckv, kr], axis=-1).astype(BF16)
    _mla_kv(ck, wk_ref, wv_ref, gkm_ref, cos, sin, km_ref, vm_ref)


def _in_proj(x, mod_l, row_off, bpm, lw, cos_t, sin_t, rope_blocks, after):
    n, d = x.shape
    nb = n // TB
    tok = lambda w: pl.BlockSpec((TB, w), lambda i: (i, 0))
    rope_spec = pl.BlockSpec((TB, LANE), lambda i: (i % rope_blocks, 0))
    in_specs = [tok(d),
                pl.BlockSpec((1, 1, mod_l.shape[-1]), lambda i: (row_off + i // bpm, 0, 0)),
                _const_spec((1, d)), _const_spec((d, IN_W)), _const_spec((256, HW)),
                _const_spec((256, HW)), _const_spec((256, HW)),
                _const_spec((1, HW)), _const_spec((1, HW)), _const_spec((1, 256)),
                _const_spec((1, 128)), _const_spec((1, HW)), _const_spec((1, HW)),
                rope_spec, rope_spec, pl.BlockSpec(memory_space=pl.ANY)]
    widths = [(HW, BF16), (HW, BF16), (HW, BF16), (HW, F32), (HW, F32), (256, F32),
              (HW, BF16), (HW, BF16), (HW, BF16), (128, F32), (128, F32)]
    return pl.pallas_call(
        _in_kernel,
        grid=(nb,),
        in_specs=in_specs,
        out_specs=[tok(w) for w, _ in widths],
        out_shape=[jax.ShapeDtypeStruct((n, w), dt) for w, dt in widths],
        compiler_params=_params(("arbitrary",)),
        name="in_proj",
    )(x, mod_l, lw["norm1"], lw["w_in"], lw["w_uq"], lw["w_k"], lw["w_v"],
      lw["g_q"], lw["g_k"], lw["g_cq"], lw["g_ckv"], lw["g_qm"], lw["g_km"], cos_t, sin_t, after)


def _cache_kernel(ck_ref, wk_ref, wv_ref, gk_ref, km_ref, vm_ref):
    rows = ck_ref.shape[2]
    cos = jnp.ones((rows, LANE), F32)
    sin = jnp.zeros((rows, LANE), F32)
    _mla_kv(ck_ref[0, 0], wk_ref.at[0], wv_ref.at[0], gk_ref.at[0], cos, sin,
            km_ref.at[0, 0], vm_ref.at[0, 0])


def _cache_kv(ck, w_k, w_v, g_km):
    db, depth, p, _ = ck.shape
    spec = lambda w: pl.BlockSpec((1, 1, p, w), lambda b, l: (b, l, 0, 0))
    wspec = lambda r: pl.BlockSpec((1, r, HW), lambda b, l: (l, 0, 0))
    return pl.pallas_call(
        _cache_kernel,
        grid=(db, depth),
        in_specs=[spec(256), wspec(256), wspec(256), wspec(1)],
        out_specs=[spec(HW), spec(HW)],
        out_shape=[jax.ShapeDtypeStruct((db, depth, p, HW), BF16)] * 2,
        compiler_params=_params(("arbitrary", "arbitrary")),
        name="cache_kv",
    )(ck, w_k, w_v, g_km)


def _softmax_av(s_list, v_list):
    m = s_list[0].max(axis=-1, keepdims=True)
    for s in s_list[1:]:
        m = jnp.maximum(m, s.max(axis=-1, keepdims=True))
    acc = None
    den = None
    for s, v in zip(s_list, v_list):
        p = jnp.exp(s - m)
        l = p.sum(axis=-1, keepdims=True)
        o = jnp.dot(p.astype(BF16), v, preferred_element_type=F32)
        acc = o if acc is None else acc + o
        den = l if den is None else den + l
    return acc / den


def _ctx_attn_kernel(qn, kn, vn, qm, km, vm, on, om):
    for q, k, v, o in ((qn, kn, vn, on), (qm, km, vm, om)):
        for h in range(HEADS):
            sl = slice(h * LANE, (h + 1) * LANE)
            s = _nt_dot(q[:, sl], k[:, sl])
            o[:, sl] = _softmax_av([s], [v[:, sl]]).astype(BF16)


def _ctx_attn(qn, kn, vn, qm, km, vm, seq):
    n = qn.shape[0]
    spec = pl.BlockSpec((seq, HW), lambda i: (i, 0))
    return pl.pallas_call(
        _ctx_attn_kernel,
        grid=(n // seq,),
        in_specs=[spec] * 6,
        out_specs=[spec] * 2,
        out_shape=[jax.ShapeDtypeStruct((n, HW), BF16)] * 2,
        compiler_params=_params(("arbitrary",)),
        name="ctx_attn",
    )(qn, kn, vn, qm, km, vm)


def _lat_mla_kernel(q, k, v, kc, vc, o):
    s1 = _nt_dot(q[...], k[...])
    s2 = _nt_dot(q[...], kc[0])
    o[...] = _softmax_av([s1, s2], [v[...], vc[0]]).astype(BF16)


def _lat_mla(qm, km, vm, kc, vc, db):
    n = qm.shape[0]
    ds = n // db
    nq = ds // TQ
    qspec = pl.BlockSpec((TQ, LANE), lambda b, h, i: (b * nq + i, h))
    kspec = pl.BlockSpec((ds, LANE), lambda b, h, i: (b, h))
    cspec = pl.BlockSpec((1, kc.shape[1], LANE), lambda b, h, i: (b, 0, h))
    return pl.pallas_call(
        _lat_mla_kernel,
        grid=(db, HEADS, nq),
        in_specs=[qspec, kspec, kspec, cspec, cspec],
        out_specs=qspec,
        out_shape=jax.ShapeDtypeStruct((n, HW), BF16),
        compiler_params=_params(("arbitrary",) * 3),
        name="lat_mla",
    )(qm, km, vm, kc, vc)


def _nat_kernel(q, k, v, kc, vc, bias, o, *, rows):
    r = pl.program_id(1)
    rs = jnp.clip(r - WIN_R // 2, 0, rows - WIN_R)
    start = pl.multiple_of(rs * GRID_W, GRID_W)
    band = WIN_R * GRID_W
    for h in range(HEADS):
        sl = slice(h * LANE, (h + 1) * LANE)
        qh = q[:, sl]
        s1 = _nt_dot(qh, k[pl.ds(start, band), sl]) + bias[0, h]
        s2 = _nt_dot(qh, kc[0, :, sl])
        o[:, sl] = _softmax_av([s1, s2], [v[pl.ds(start, band), sl], vc[0, :, sl]]).astype(BF16)


def _nat_attn(qn, kn, vn, kc, vc, bias, db):
    n = qn.shape[0]
    ds = n // db
    rows = ds // GRID_W
    band = WIN_R * GRID_W

    def variant(r):
        return jnp.where(r < WIN_R // 2, r, jnp.where(r > rows - WIN_R // 2, r - (rows - WIN_R), WIN_R // 2))

    qspec = pl.BlockSpec((GRID_W, HW), lambda b, r: (b * rows + r, 0))
    kspec = pl.BlockSpec((ds, HW), lambda b, r: (b, 0))
    cspec = pl.BlockSpec((1, kc.shape[1], HW), lambda b, r: (b, 0, 0))
    bspec = pl.BlockSpec((1, HEADS, GRID_W, band), lambda b, r: (variant(r), 0, 0, 0))
    return pl.pallas_call(
        functools.partial(_nat_kernel, rows=rows),
        grid=(db, rows),
        in_specs=[qspec, kspec, kspec, cspec, cspec, bspec],
        out_specs=qspec,
        out_shape=jax.ShapeDtypeStruct((n, HW), BF16),
        compiler_params=_params(("arbitrary", "arbitrary")),
        name="nat_attn",
    )(qn, kn, vn, kc, vc, bias)


def _split3(x):
    hi = x.astype(BF16)
    r = x - hi.astype(F32)
    mid = r.astype(BF16)
    lo = (r - mid.astype(F32)).astype(BF16)
    return hi, mid, lo


def _pool(p_prev, p_cur, p_next, posb, seq_len):
    rows = p_cur.shape[0]
    halo = p_prev.shape[0]
    ext = rows + 2 * halo
    pext = jnp.concatenate([p_prev, p_cur, p_next], axis=0)
    parts = _split3(pext)
    t = posb + lax.broadcasted_iota(jnp.int32, (rows, ext), 0)
    s = posb - halo + lax.broadcasted_iota(jnp.int32, (rows, ext), 1)
    tcol = posb + lax.broadcasted_iota(jnp.int32, (rows, 1), 0)
    grp = lax.broadcasted_iota(jnp.int32, (rows, 256), 1) // POOL_G
    d = jnp.zeros((rows, 256), F32)
    for gi, w in enumerate(POOL_WINDOWS):
        lo = jnp.maximum(t - w // 2, 0)
        hi = jnp.minimum(t + (w - w // 2), seq_len)
        sel = jnp.where(s >= lo, jnp.where(s < hi, 1.0, 0.0), 0.0).astype(BF16)
        tot = sum(jnp.dot(sel, part, preferred_element_type=F32) for part in parts)
        cnt = (jnp.minimum(tcol + (w - w // 2), seq_len) - jnp.maximum(tcol - w // 2, 0)).astype(F32)
        d = jnp.where(grp == gi, tot / cnt - p_cur, d)
    return d


def _first_max(x, pos, sentinel):
    m = jnp.max(x, axis=0, keepdims=True)
    idx = jnp.min(jnp.where(x == m, pos, sentinel), axis=0, keepdims=True)
    return m, idx


def _topk_head(qh, sk_ref):
    c = qh.shape[0]
    key_pos = lax.broadcasted_iota(jnp.int32, (PEER_NKEYS, c), 0).astype(F32)
    row16 = lax.broadcasted_iota(jnp.int32, (PEER_TOPK, c), 0)
    neg = jnp.float32(-jnp.inf)
    s0 = _nt_dot(sk_ref[0], qh)
    s1 = _nt_dot(sk_ref[1], qh)

    def stage1(a, carry):
        out = []
        for s, sv, si in (carry[0:3], carry[3:6]):
            m, idx = _first_max(s, key_pos, float(PEER_NKEYS))
            out += [jnp.where(key_pos == idx, neg, s),
                    jnp.where(row16 == a, m, sv), jnp.where(row16 == a, idx, si)]
        return tuple(out)

    zf = jnp.zeros((PEER_TOPK, c), F32)
    _, sv0, si0, _, sv1, si1 = lax.fori_loop(0, PEER_TOPK, stage1, (s0, zf, zf, s1, zf, zf))

    sub8 = lax.broadcasted_iota(jnp.int32, (8, c), 0)
    sub8f = sub8.astype(F32)
    cs, ci, cf = [], [], []

    def piece(a_vals, a_ids, a_flat, b_vals, b_ids, b_flat, nb):
        val = a_vals + b_vals
        if nb < 8:
            val = jnp.where(sub8 < nb, val, neg)
        cs.append(val)
        ci.append(a_ids * float(PEER_NKEYS) + b_ids)
        cf.append(jnp.broadcast_to(a_flat * float(PEER_TOPK) + b_flat, (8, c)))

    for a in range(8):
        nb = PEER_TOPK // (a + 1)
        for b0 in range(0, nb, 8):
            piece(sv0[a:a + 1], si0[a:a + 1], float(a), sv1[b0:b0 + 8], si1[b0:b0 + 8],
                  sub8f + float(b0), min(nb - b0, 8))
    piece(sv0[8:16], si0[8:16], sub8f + 8.0, sv1[0:1], si1[0:1], jnp.zeros((8, c), F32), 8)
    npc = len(cs)
    nflat = float(PEER_TOPK * PEER_TOPK)

    def stage2(k, carry):
        vals = list(carry[:npc])
        tv, te = carry[npc], carry[npc + 1]
        m = vals[0]
        for v in vals[1:]:
            m = jnp.maximum(m, v)
        m = jnp.max(m, axis=0, keepdims=True)
        pos = None
        for v, f in zip(vals, cf):
            cand = jnp.where(v == m, f, nflat)
            pos = cand if pos is None else jnp.minimum(pos, cand)
        pos = jnp.min(pos, axis=0, keepdims=True)
        e = None
        for i, f in zip(ci, cf):
            cand = jnp.where(f == pos, i, -1.0)
            e = cand if e is None else jnp.maximum(e, cand)
        e = jnp.max(e, axis=0, keepdims=True)
        vals = [jnp.where(f == pos, neg, v) for v, f in zip(vals, cf)]
        return tuple(vals) + (jnp.where(row16 == k, m, tv), jnp.where(row16 == k, e, te))

    res = lax.fori_loop(0, PEER_TOPK, stage2, tuple(cs) + (zf, zf))
    return res[npc], res[npc + 1]


def _out_kernel(on_ref, om_ref, pc_ref, pp_ref, pn_ref, x_ref, mod_ref,
                won_ref, wop_ref, wom_ref, pw_ref, ps_ref, n2_ref, wq_ref, sk_ref,
                x1_ref, h2_ref, ids_ref, gt_ref, gn_ref, q_scr, idt_scr, *, bps, seq_len):
    d = x_ref.shape[1]
    rows = x_ref.shape[0]
    i = pl.program_id(0)
    mod = mod_ref[0]
    g1 = mod[:, 2 * d:3 * d]
    sh2 = mod[:, 3 * d:4 * d]
    sc2 = mod[:, 4 * d:5 * d]

    posb = (i % bps) * rows
    dpool = _pool(pp_ref[...], pc_ref[...], pn_ref[...], posb, seq_len)
    ypool = jnp.dot(dpool.astype(BF16), pw_ref[...], preferred_element_type=F32) * ps_ref[...]
    mix = (jnp.dot(on_ref[...], won_ref[...], preferred_element_type=F32)
           + jnp.dot(ypool.astype(BF16), wop_ref[...], preferred_element_type=F32)
           + jnp.dot(om_ref[...], wom_ref[...], preferred_element_type=F32))
    x1 = x_ref[...] + g1 * mix
    x1_ref[...] = x1
    h2 = _rms(x1, n2_ref[...]) * (1.0 + sc2) + sh2
    h2_ref[...] = h2

    q = jnp.dot(h2.astype(BF16), wq_ref[...], preferred_element_type=F32)
    for hh in range(PEER_HEADS):
        q_scr[hh] = q[:, hh * LANE:(hh + 1) * LANE].astype(BF16)

    for c0 in range(0, rows, LANE):
        def head(hh, _):
            tv, te = _topk_head(q_scr[hh, c0:c0 + LANE, :], sk_ref)
            ex = jnp.exp(tv - tv[0:1])
            gates = ex / jnp.sum(ex, axis=0, keepdims=True)
            r0 = pl.multiple_of(hh * PEER_TOPK, PEER_TOPK)
            gt_ref[pl.ds(r0, PEER_TOPK), c0:c0 + LANE] = gates
            idt_scr[pl.ds(r0, PEER_TOPK), c0:c0 + LANE] = te
            return 0

        lax.fori_loop(0, PEER_HEADS, head, 0)
    ids_ref[...] = idt_scr[...].T.astype(jnp.int32)
    gn_ref[...] = gt_ref[...].T


def _out_proj(x, on, om, p, mod_l, row_off, bpm, lw, seq_len):
    n, d = x.shape
    nb = n // TB
    bps = seq_len // TB
    halo = 8
    hb = TB // halo
    tok = lambda w: pl.BlockSpec((TB, w), lambda i: (i, 0))
    in_specs = [tok(HW), tok(HW), tok(256),
                pl.BlockSpec((halo, 256), lambda i: (jnp.maximum(i * hb - 1, 0), 0)),
                pl.BlockSpec((halo, 256), lambda i: (jnp.minimum((i + 1) * hb, n // halo - 1), 0)),
                tok(d),
                pl.BlockSpec((1, 1, mod_l.shape[-1]), lambda i: (row_off + i // bpm, 0, 0)),
                _const_spec((HW, d)), _const_spec((256, d)), _const_spec((HW, d)),
                _const_spec((256, 256)), _const_spec((1, 256)), _const_spec((1, d)),
                _const_spec((d, PEER_HEADS * LANE)), _const_spec((2, PEER_NKEYS, LANE))]
    nk = PEER_HEADS * PEER_TOPK
    return pl.pallas_call(
        functools.partial(_out_kernel, bps=bps, seq_len=seq_len),
        grid=(nb,),
        in_specs=in_specs,
        out_specs=[tok(d), tok(d), tok(nk), pl.BlockSpec((nk, TB), lambda i: (0, i)), tok(nk)],
        out_shape=[jax.ShapeDtypeStruct((n, d), F32), jax.ShapeDtypeStruct((n, d), F32),
                   jax.ShapeDtypeStruct((n, nk), jnp.int32), jax.ShapeDtypeStruct((nk, n), F32),
                   jax.ShapeDtypeStruct((n, nk), F32)],
        scratch_shapes=[pltpu.VMEM((PEER_HEADS, TB, LANE), BF16), pltpu.VMEM((nk, TB), F32)],
        compiler_params=_params(("arbitrary",)),
        name="out_proj",
    )(on, om, p, p, p, x, mod_l, lw["w_o_na"], lw["w_o_pool"], lw["w_o_mla"],
      lw["pool_w"], lw["pool_scale"], lw["norm2"], lw["peer_wq"], lw["peer_sk"])


def _gelu_tanh(x):
    return x * (0.5 * (1.0 + jnp.tanh(0.7978845608028654 * (x + 0.044715 * (x * x * x)))))


def _peer_token_mix(chunk, hrow, gcol, ch):
    acc = None
    for s in range(ch):
        us = lax.bitcast_convert_type(chunk(s) & jnp.int32(-65536), F32)
        term = us * hrow[:, s * LANE:(s + 1) * LANE]
        acc = term if acc is None else acc + term
    wgt = gcol * _gelu_tanh(jnp.sum(acc, axis=-1, keepdims=True))
    parts = []
    for s in range(ch):
        vs = lax.bitcast_convert_type(chunk(s) << 16, F32)
        parts.append(jnp.sum(vs * wgt, axis=0, keepdims=True))
    return jnp.concatenate(parts, axis=-1)


def _peer_staged_kernel(rows_ref, gt_ref, h2_ref, x1_ref, mod_ref, o_ref):
    d = x1_ref.shape[1]
    ch = d // LANE
    nk = gt_ref.shape[0]
    g2 = mod_ref[0][:, 5 * d:6 * d]
    tok_lane = lax.broadcasted_iota(jnp.int32, gt_ref.shape, 1)
    base = (pl.program_id(0) % (PEER_TB // PEER_SUB)) * PEER_SUB
    h8 = h2_ref[...]
    ys = []
    for t in range(PEER_SUB):
        chunk = lambda s: rows_ref[pl.ds(t * nk * ch + s, nk, stride=ch), :]
        gcol = jnp.sum(jnp.where(tok_lane == base + t, gt_ref[...], 0.0), axis=-1, keepdims=True)
        ys.append(_peer_token_mix(chunk, h8[t:t + 1, :], gcol, ch))
    o_ref[...] = x1_ref[...] + g2 * jnp.concatenate(ys, axis=0)


def _peer_staged(x1, h2, rows, gt, mod_l, row_off, tpm):
    n, d = x1.shape
    nk = gt.shape[0]
    per = PEER_SUB * nk * (d // LANE)
    sub_per_tb = PEER_TB // PEER_SUB
    tok = pl.BlockSpec((PEER_SUB, d), lambda j: (j, 0))
    return pl.pallas_call(
        _peer_staged_kernel,
        grid=(n // PEER_SUB,),
        in_specs=[pl.BlockSpec((per, LANE), lambda j: (j, 0)),
                  pl.BlockSpec((nk, PEER_TB), lambda j: (0, j // sub_per_tb)),
                  tok, tok,
                  pl.BlockSpec((1, 1, mod_l.shape[-1]), lambda j: (row_off + (j * PEER_SUB) // tpm, 0, 0))],
        out_specs=tok,
        out_shape=jax.ShapeDtypeStruct((n, d), F32),
        compiler_params=_params(("arbitrary",)),
        name="peer_staged",
    )(rows, gt, h2, x1, mod_l)


def _sc_gather(table3, ids_flat):
    m = ids_flat.shape[0]
    _, ch, lane = table3.shape
    info = plsc.get_sparse_core_info()
    nc, nw = info.num_cores, info.num_cores * info.num_subcores
    idx_win = 128
    win = 32
    per_w = m // nw
    assert m % (nw * idx_win) == 0
    mesh = plsc.VectorSubcoreMesh(core_axis_name="core", subcore_axis_name="subcore")

    @functools.partial(
        pl.kernel, mesh=mesh,
        out_type=jax.ShapeDtypeStruct((m, ch, lane), table3.dtype),
        scratch_types=[pltpu.VMEM((idx_win,), jnp.int32),
                       pltpu.VMEM((win, ch, lane), table3.dtype),
                       pltpu.VMEM((win, ch, lane), table3.dtype),
                       pltpu.SemaphoreType.DMA, pltpu.SemaphoreType.DMA])
    def gather(tab_hbm, idx_hbm, out_hbm, idx_v, rows_a, rows_b, sem_a, sem_b):
        wid = lax.axis_index("subcore") * nc + lax.axis_index("core")
        bufs = ((rows_a, sem_a), (rows_b, sem_b))
        nq = idx_win // win

        def fetch(q):
            rows, sem = bufs[q % 2]
            return pltpu.make_async_copy(tab_hbm.at[idx_v.at[pl.ds(q * win, win)]], rows, sem)

        @pl.loop(0, per_w // idx_win)
        def _(g):
            base = pl.multiple_of(wid * per_w + g * idx_win, idx_win)
            pltpu.sync_copy(idx_hbm.at[pl.ds(base, idx_win)], idx_v)
            fetch(0).start()
            for q in range(nq):
                fetch(q).wait()
                if q + 1 < nq:
                    fetch(q + 1).start()
                pltpu.sync_copy(bufs[q % 2][0], out_hbm.at[pl.ds(base + q * win, win)])

    return gather(table3, ids_flat)


def _sc_peer(table3, ids_flat, gates_flat, h2, n):
    _, ch, lane = h2.shape
    nk = ids_flat.shape[0] // h2.shape[0]
    info = plsc.get_sparse_core_info()
    nc, nw, nl = info.num_cores, info.num_cores * info.num_subcores, info.num_lanes
    tpw = n // nw
    win = 32
    nq = nk // win
    cpr = lane // nl
    nchunk = ch * cpr
    hc = nchunk // 2
    assert n % nw == 0 and nk % win == 0 and win % nl == 0
    mesh = plsc.VectorSubcoreMesh(core_axis_name="core", subcore_axis_name="subcore")
    hi_mask = jnp.int32(-65536)

    @functools.partial(
        pl.kernel, mesh=mesh,
        out_type=jax.ShapeDtypeStruct((n, ch, lane), F32),
        compiler_params=pltpu.CompilerParams(needs_layout_passes=False),
        scratch_types=[pltpu.VMEM((nk,), jnp.int32), pltpu.VMEM((nk,), F32),
                       pltpu.VMEM((ch, lane), F32), pltpu.VMEM((ch, lane), F32),
                       pltpu.VMEM((win, ch, lane), jnp.int32), pltpu.VMEM((win, ch, lane), jnp.int32),
                       pltpu.VMEM((win * nl,), F32), pltpu.VMEM((win,), F32),
                       pltpu.SemaphoreType.DMA, pltpu.SemaphoreType.DMA])
    def peer(tab_hbm, ids_hbm, g_hbm, h2_hbm, y_hbm,
             idx_v, g_v, x_v, y_v, rows_a, rows_b, part_v, w_v, sem_a, sem_b):
        wid = lax.axis_index("subcore") * nc + lax.axis_index("core")
        bufs = ((rows_a, sem_a), (rows_b, sem_b))
        lanes = lax.iota(jnp.int32, nl)
        zero = jnp.zeros((nl,), F32)

        def fetch(q):
            rows, sem = bufs[q % 2]
            return pltpu.make_async_copy(tab_hbm.at[idx_v.at[pl.ds(q * win, win)]], rows, sem)

        def word(rows, r, cc):
            return rows[r, cc // cpr, pl.ds((cc % cpr) * nl, nl)]

        @pl.loop(0, tpw)
        def _(ti):
            tok = wid * tpw + ti
            off = pl.multiple_of(tok * nk, nk)
            pltpu.sync_copy(ids_hbm.at[pl.ds(off, nk)], idx_v)
            pltpu.sync_copy(g_hbm.at[pl.ds(off, nk)], g_v)
            pltpu.sync_copy(h2_hbm.at[tok], x_v)
            for cc in range(nchunk):
                y_v[cc // cpr, pl.ds((cc % cpr) * nl, nl)] = zero
            fetch(0).start()
            for q in range(nq):
                rows = bufs[q % 2][0]
                fetch(q).wait()
                if q + 1 < nq:
                    fetch(q + 1).start()

                for half in range(2):
                    xs = [x_v[(half * hc + c) // cpr, pl.ds(((half * hc + c) % cpr) * nl, nl)]
                          for c in range(hc)]

                    @pl.loop(0, win)
                    def _(r):
                        accs = [None] * 4
                        for c in range(hc):
                            u = lax.bitcast_convert_type(word(rows, r, half * hc + c) & hi_mask, F32)
                            t = u * xs[c]
                            accs[c % 4] = t if accs[c % 4] is None else accs[c % 4] + t
                        acc = (accs[0] + accs[1]) + (accs[2] + accs[3])
                        po = pl.multiple_of(r * nl, nl)
                        if half == 0:
                            part_v[pl.ds(po, nl)] = acc
                        else:
                            part_v[pl.ds(po, nl)] = part_v[pl.ds(po, nl)] + acc

                for grp in range(win // nl):
                    s = zero
                    for rr in range(nl):
                        tot = jnp.sum(part_v[pl.ds((grp * nl + rr) * nl, nl)])
                        s = jnp.where(lanes == rr, tot, s)
                    z = 0.7978845608028654 * (s + 0.044715 * (s * s * s))
                    tanh = 1.0 - 2.0 / (jnp.exp(2.0 * z) + 1.0)
                    gate = g_v[pl.ds(q * win + grp * nl, nl)]
                    w_v[pl.ds(grp * nl, nl)] = gate * (s * (0.5 * (1.0 + tanh)))

                for half in range(2):
                    def body(r, yacc):
                        wr = plsc.load_gather(w_v, [jnp.full((nl,), r, jnp.int32)])
                        out = []
                        for c in range(hc):
                            v = lax.bitcast_convert_type(word(rows, r, half * hc + c) << 16, F32)
                            out.append(yacc[c] + wr * v)
                        return tuple(out)

                    yacc = lax.fori_loop(0, win, body, tuple(zero for _ in range(hc)))
                    for c in range(hc):
                        cc = half * hc + c
                        sl = (cc // cpr, pl.ds((cc % cpr) * nl, nl))
                        y_v[sl] = y_v[sl] + yacc[c]
            pltpu.sync_copy(y_v, y_hbm.at[tok])

    return peer(table3, ids_flat, gates_flat, h2)


def _residual_kernel(x1_ref, y_ref, mod_ref, x2_hbm, o_ref):
    del x2_hbm
    d = x1_ref.shape[1]
    o_ref[...] = x1_ref[...] + mod_ref[0][:, 5 * d:6 * d] * y_ref[...]


def _residual(x1, y, mod_l, row, x2):
    n, d = x1.shape
    tok = pl.BlockSpec((PEER_TB, d), lambda i: (i, 0))
    return pl.pallas_call(
        _residual_kernel,
        grid=(y.shape[0] // PEER_TB,),
        in_specs=[tok, tok, pl.BlockSpec((1, 1, mod_l.shape[-1]), lambda i: (row, 0, 0)),
                  pl.BlockSpec(memory_space=pl.ANY)],
        out_specs=tok,
        out_shape=jax.ShapeDtypeStruct((n, d), F32),
        input_output_aliases={3: 0},
        compiler_params=_params(("arbitrary",)),
        name="residual",
    )(x1, y, mod_l, x2)


def _peer_kernel(ids_hbm, gt_ref, h2_ref, x1_ref, mod_ref, tab_hbm, o_ref,
                 ids_s, buf, sem_i, sem_r, *, first_block):
    d = x1_ref.shape[1]
    ch = d // LANE
    pitch = ch + 1
    nsub = x1_ref.shape[0] // PEER_SUB
    nk = gt_ref.shape[0]
    nids = PEER_SUB * nk
    i = pl.program_id(0) + first_block
    g2 = mod_ref[0][:, 5 * d:6 * d]
    tok_lane = lax.broadcasted_iota(jnp.int32, gt_ref.shape, 1)

    def ids_copy(j, slot):
        start = pl.multiple_of((i * nsub + j) * nids, nids)
        return pltpu.make_async_copy(ids_hbm.at[pl.ds(start, nids)],
                                     ids_s.at[pl.ds(slot * nids, nids)], sem_i.at[slot])

    def row_copy(slot, e, f):
        src = tab_hbm.at[pl.ds(pl.multiple_of(e * ch, ch), ch), :]
        dst = buf.at[slot, pl.ds(f * pitch, ch), :]
        return pltpu.make_async_copy(src, dst, sem_r.at[slot])

    def issue_rows(slot):
        for t in range(PEER_SUB):
            def body(kk, _):
                for r in range(8):
                    f = t * nk + kk * 8 + r
                    row_copy(slot, ids_s[slot * nids + f], f).start(priority=r % 2)
                return 0

            lax.fori_loop(0, nk // 8, body, 0)

    def wait_rows(slot):
        done = buf.at[slot, pl.ds(0, nids * ch), :]
        pltpu.make_async_copy(done, done, sem_r.at[slot]).wait()

    def compute(slot, j):
        base = pl.multiple_of(j * PEER_SUB, PEER_SUB)
        h8 = h2_ref[pl.ds(base, PEER_SUB), :]
        ys = []
        for t in range(PEER_SUB):
            chunk = lambda s: buf[slot, pl.ds(t * nk * pitch + s, nk, stride=pitch), :]
            gcol = jnp.sum(jnp.where(tok_lane == base + t, gt_ref[...], 0.0), axis=-1, keepdims=True)
            ys.append(_peer_token_mix(chunk, h8[t:t + 1, :], gcol, ch))
        y8 = jnp.concatenate(ys, axis=0)
        o_ref[pl.ds(base, PEER_SUB), :] = x1_ref[pl.ds(base, PEER_SUB), :] + g2 * y8

    first = ids_copy(0, 0)
    first.start()
    first.wait()
    issue_rows(0)
    ids_copy(1, 1).start()

    def pair(jj, _):
        j0 = 2 * jj
        ids_copy(j0 + 1, 1).wait()
        issue_rows(1)

        @pl.when(j0 + 2 < nsub)
        def _():
            ids_copy(j0 + 2, 0).start()

        wait_rows(0)
        compute(0, j0)

        @pl.when(j0 + 2 < nsub)
        def _():
            ids_copy(j0 + 2, 0).wait()
            issue_rows(0)

        @pl.when(j0 + 3 < nsub)
        def _():
            ids_copy(j0 + 3, 1).start()

        wait_rows(1)
        compute(1, j0 + 1)
        return 0

    lax.fori_loop(0, nsub // 2, pair, 0)


def _pack_tables(peer_u, peer_v):
    e, d = peer_u.shape
    ub = lax.bitcast_convert_type(peer_u.astype(BF16), jnp.uint16).astype(jnp.uint32)
    vb = lax.bitcast_convert_type(peer_v.astype(BF16), jnp.uint16).astype(jnp.uint32)
    words = lax.bitcast_convert_type((ub << 16) | vb, jnp.int32)
    return words.reshape(e, d // LANE, LANE)


def _peer(x1, h2, ids, gt, mod_l, row, table, tok0):
    n, d = x1.shape
    nk = gt.shape[0]
    b0 = tok0 // PEER_TB
    nb = n // PEER_TB - b0
    tok = pl.BlockSpec((PEER_TB, d), lambda i: (i + b0, 0))
    any_spec = pl.BlockSpec(memory_space=pl.ANY)
    return pl.pallas_call(
        functools.partial(_peer_kernel, first_block=b0),
        grid=(nb,),
        in_specs=[any_spec,
                  pl.BlockSpec((nk, PEER_TB), lambda i: (0, i + b0)),
                  tok, tok,
                  pl.BlockSpec((1, 1, mod_l.shape[-1]), lambda i: (row, 0, 0)),
                  any_spec],
        out_specs=tok,
        out_shape=jax.ShapeDtypeStruct((n, d), F32),
        scratch_shapes=[pltpu.SMEM((2 * PEER_SUB * nk,), jnp.int32),
                        pltpu.VMEM((2, PEER_SUB * nk * (d // LANE + 1), LANE), jnp.int32),
                        pltpu.SemaphoreType.DMA((2,)),
                        pltpu.SemaphoreType.DMA((2,))],
        compiler_params=_params(("arbitrary",)),
        name="peer",
    )(ids.reshape(n * nk), gt, h2, x1, mod_l, table.reshape(-1, LANE))


def _pad_heads(w, width):
    pad = [(0, 0)] * (w.ndim - 1) + [(0, LANE - width)]
    w = jnp.pad(w, pad)
    return w.reshape(w.shape[:-2] + (HW,))


def _head_gain(g, width):
    depth = g.shape[0]
    g = jnp.pad(g, ((0, 0), (0, LANE - width)))
    return jnp.tile(g, (1, HEADS)).reshape(depth, 1, HW)


def _rope_tables(seq):
    t = np.arange(seq)
    half = MLA_ROPE // 2
    inv = ROPE_THETA ** (-np.arange(0, half, 2, dtype=np.float32) / half)
    cos = np.ones((seq, LANE), np.float32)
    sin = np.zeros((seq, LANE), np.float32)
    for off, pos in ((MLA_NOPE, t // GRID_W), (MLA_NOPE + half, t % GRID_W)):
        ang = pos.astype(np.float32)[:, None] * inv[None, :]
        q = half // 2
        cos[:, off:off + q] = np.cos(ang)
        cos[:, off + q:off + half] = np.cos(ang)
        sin[:, off:off + q] = -np.sin(ang)
        sin[:, off + q:off + half] = np.sin(ang)
    return jnp.asarray(cos), jnp.asarray(sin)


def _nat_bias(rel_bias):
    v = np.arange(WIN_R)[:, None]
    j = np.arange(WIN_R)[None, :]
    dr = j - v + WIN_R - 1
    cq = np.arange(GRID_W)[:, None]
    kc = np.arange(GRID_W)[None, :]
    cstart = np.clip(cq - WIN_C // 2, 0, GRID_W - WIN_C)
    ok = (kc >= cstart) & (kc < cstart + WIN_C)
    dc = np.clip(kc - cq + WIN_C - 1, 0, 2 * WIN_C - 2)
    b = rel_bias[:, :, dr]
    b = b[..., dc]
    b = jnp.where(jnp.asarray(ok)[None, None, None, None], b, NEG_INF)
    b = jnp.transpose(b, (0, 2, 1, 4, 3, 5))
    return b.reshape(b.shape[0], WIN_R, HEADS, GRID_W, WIN_R * GRID_W)


def _layer_weights(w_in, na_q_norm, na_k_norm, mla_cq_norm, mla_ckv_norm, mla_w_uq, mla_w_ukv,
                   mla_q_norm, mla_k_norm, w_out, pool_w, pool_scale, norm1, norm2,
                   peer_wq, peer_subkeys):
    depth, d, _ = w_in.shape
    na_w = HEADS * NA_DH
    segs = np.cumsum([0, na_w, na_w, na_w, 256, 256, 128, MLA_ROPE])
    part = lambda i: w_in[:, :, segs[i]:segs[i + 1]]
    heads = lambda w: _pad_heads(w.reshape(depth, d, HEADS, NA_DH), NA_DH)
    w_in_p = jnp.concatenate(
        [heads(part(0)), heads(part(1)), heads(part(2)), part(3), part(4), part(5),
         jnp.pad(part(6), ((0, 0), (0, 0), (0, LANE - MLA_ROPE)))], axis=-1).astype(BF16)

    w_uq = _pad_heads(mla_w_uq, MLA_QK).astype(BF16)
    k_nope = _pad_heads(mla_w_ukv[..., :MLA_NOPE], MLA_NOPE)
    eye = np.zeros((MLA_ROPE, HEADS, LANE), np.float32)
    for h in range(HEADS):
        eye[np.arange(MLA_ROPE), h, MLA_NOPE + np.arange(MLA_ROPE)] = 1.0
    eye = jnp.broadcast_to(jnp.asarray(eye.reshape(MLA_ROPE, HW)), (depth, MLA_ROPE, HW))
    zer = jnp.zeros((depth, 256 - 128 - MLA_ROPE, HW), F32)
    w_k = jnp.concatenate([k_nope, eye, zer], axis=1).astype(BF16)
    w_v = jnp.concatenate([_pad_heads(mla_w_ukv[..., MLA_NOPE:], MLA_V),
                           jnp.zeros((depth, 128, HW), F32)], axis=1).astype(BF16)

    mix_w = HEADS * NA_DH
    w_o_na = jnp.pad(w_out[:, :mix_w].reshape(depth, HEADS, NA_DH, d),
                     ((0, 0), (0, 0), (0, LANE - NA_DH), (0, 0))).reshape(depth, HW, d).astype(BF16)
    w_o_pool = w_out[:, mix_w:mix_w + 256].astype(BF16)
    w_o_mla = jnp.pad(w_out[:, mix_w + 256:].reshape(depth, HEADS, MLA_V, d),
                      ((0, 0), (0, 0), (0, LANE - MLA_V), (0, 0))).reshape(depth, HW, d).astype(BF16)
    ng = len(POOL_WINDOWS)
    pw = jnp.zeros((depth, ng * POOL_G, ng * POOL_G), F32)
    for g in range(ng):
        pw = pw.at[:, g * POOL_G:(g + 1) * POOL_G, g * POOL_G:(g + 1) * POOL_G].set(pool_w[:, g])

    half = peer_subkeys.shape[-1]
    sk = jnp.stack([jnp.pad(peer_subkeys[:, 0], ((0, 0), (0, 0), (0, LANE - half))),
                    jnp.pad(peer_subkeys[:, 1], ((0, 0), (0, 0), (LANE - half, 0)))], axis=1).astype(BF16)

    return dict(
        w_in=w_in_p, w_uq=w_uq, w_k=w_k, w_v=w_v,
        g_q=_head_gain(na_q_norm, NA_DH), g_k=_head_gain(na_k_norm, NA_DH),
        g_cq=mla_cq_norm[:, None, :], g_ckv=mla_ckv_norm[:, None, :],
        g_qm=_head_gain(mla_q_norm, MLA_QK), g_km=_head_gain(mla_k_norm, MLA_QK),
        w_o_na=w_o_na, w_o_pool=w_o_pool, w_o_mla=w_o_mla,
        pool_w=pw.astype(BF16), pool_scale=pool_scale[:, None, :],
        norm1=norm1[:, None, :], norm2=norm2[:, None, :],
        peer_wq=peer_wq.astype(BF16), peer_sk=sk)


def kernel(x_prompt, x_sample, c, cache_nat_k, cache_nat_v, cache_mla_ckv, cache_mla_krope, c_ctx, w_mod, b_mod, norm1, norm2, w_in, na_q_norm, na_k_norm, na_rel_bias, pool_w, pool_scale, mla_cq_norm, mla_ckv_norm, mla_w_uq, mla_w_ukv, mla_q_norm, mla_k_norm, w_out, peer_wq, peer_subkeys, peer_u, peer_v):
    batch, seq, d = x_prompt.shape
    db, ds, _ = x_sample.shape
    depth = w_mod.shape[0]
    past = cache_nat_k.shape[2]
    assert seq == TB and ds % TB == 0 and ds % (GRID_W * WIN_R) == 0 and db + 1 <= 8

    cond8 = jnp.concatenate([c_ctx[None, :], c, jnp.zeros((8 - 1 - db, d), F32)], axis=0)
    mod = _modulation(cond8, w_mod, b_mod).reshape(depth, 8, 1, 6 * d)

    lw_all = _layer_weights(w_in, na_q_norm, na_k_norm, mla_cq_norm, mla_ckv_norm, mla_w_uq,
                            mla_w_ukv, mla_q_norm, mla_k_norm, w_out, pool_w, pool_scale,
                            norm1, norm2, peer_wq, peer_subkeys)
    bias_all = _nat_bias(na_rel_bias)
    tables = [_pack_tables(peer_u[l], peer_v[l]) for l in range(depth)]
    cos_lat, sin_lat = _rope_tables(ds)
    cos_ctx = jnp.ones((TB, LANE), F32)
    sin_ctx = jnp.zeros((TB, LANE), F32)

    ck = jnp.concatenate([cache_mla_ckv, cache_mla_krope,
                          jnp.zeros(cache_mla_ckv.shape[:-1] + (256 - 128 - MLA_ROPE,), F32)],
                         axis=-1).astype(BF16)
    kc_mla, vc_mla = _cache_kv(ck, lw_all["w_k"], lw_all["w_v"], lw_all["g_km"])
    kc_na = _pad_heads(cache_nat_k, NA_DH).astype(BF16)
    vc_na = _pad_heads(cache_nat_v, NA_DH).astype(BF16)

    xs = [x_prompt.reshape(batch * seq, d)] + [x_sample[b] for b in range(db)]
    one_row = max(batch * seq, ds) + 1
    lat_bpm = ds // TB
    ks, vs, ckvs, krs = [], [], [], []
    pending = None

    def join(item):
        si, x1, y_sc, x2, mod_l = item
        xs[si] = _residual(x1, y_sc.reshape(-1, d), mod_l, si, x2)

    for l in range(depth):
        lw = {k: v[l] for k, v in lw_all.items()}
        mod_l = mod[l]
        for si in range(db + 1):
            x = xs[si]
            after = x if pending is None else pending[3]
            if si == 0:
                (qn, kn, vn, knf, vnf, p, qm, km, vm, ckv, kr) = _in_proj(
                    x, mod_l, 0, one_row, lw, cos_ctx, sin_ctx, 1, after)
                on, om = _ctx_attn(qn, kn, vn, qm, km, vm, seq)
                x1, h2, ids, gt, gn = _out_proj(x, on, om, p, mod_l, 0, one_row, lw, seq)
                ks.append(knf.reshape(batch, seq, HEADS, LANE)[..., :NA_DH])
                vs.append(vnf.reshape(batch, seq, HEADS, LANE)[..., :NA_DH])
                ckvs.append(ckv.reshape(batch, seq, 128))
                krs.append(kr.reshape(batch, seq, LANE)[..., :MLA_ROPE])
            else:
                b = si - 1
                (qn, kn, vn, _, _, p, qm, km, vm, _, _) = _in_proj(
                    x, mod_l, si, one_row, lw, cos_lat, sin_lat, lat_bpm, after)
                on = _nat_attn(qn, kn, vn, kc_na[b:b + 1, l], vc_na[b:b + 1, l], bias_all[l], 1)
                om = _lat_mla(qm, km, vm, kc_mla[b:b + 1, l], vc_mla[b:b + 1, l], 1)
                x1, h2, ids, gt, gn = _out_proj(x, on, om, p, mod_l, si, one_row, lw, ds)
            n_sc = x.shape[0] * SC_SHARE[0] // SC_SHARE[1] // PEER_TB * PEER_TB
            y_sc = _sc_peer(tables[l], ids.reshape(-1), gn.reshape(-1),
                            h2.reshape(-1, d // LANE, LANE), n_sc)
            x2 = _peer(x1, h2, ids, gt, mod_l, si, tables[l], n_sc)
            if pending is not None:
                join(pending)
            pending = (si, x1, y_sc, x2, mod_l)
    join(pending)

    return (xs[0].reshape(batch, seq, d), jnp.stack(xs[1:], axis=0),
            jnp.stack(ks, axis=1), jnp.stack(vs, axis=1),
            jnp.stack(ckvs, axis=1), jnp.stack(krs, axis=1))
```

```python
import functools

import numpy as np
import jax
import jax.numpy as jnp
from jax import lax
from jax.experimental import pallas as pl
from jax.experimental.pallas import tpu as pltpu
from jax.experimental.pallas import tpu_sc as plsc

F32 = jnp.float32
BF16 = jnp.bfloat16

EPS = 1e-6
ROPE_THETA = 10000.0
NEG_INF = -1e30
GRID_W = 64
HEADS = 6
NA_DH = 64
WIN_R = 8
WIN_C = 16
POOL_WINDOWS = (2, 4, 8, 16)
POOL_G = 64
MLA_NOPE = 64
MLA_ROPE = 32
MLA_QK = MLA_NOPE + MLA_ROPE
MLA_V = 64
PEER_HEADS = 8
PEER_NKEYS = 128
PEER_TOPK = 16
LANE = 128
HW = HEADS * LANE
TB = 256
TQ = 256
PEER_TB = 128
PEER_SUB = 8
VMEM_LIMIT = 56 * 1024 * 1024
SC_SHARE = (11, 16)

_CQ, _CK, _CV = 0, HW, 2 * HW
_CP = 3 * HW
_CCQ = _CP + 256
_CCKV = _CCQ + 256
_CKR = _CCKV + 128
IN_W = _CKR + 128


def _params(sem, vmem=VMEM_LIMIT):
    return pltpu.CompilerParams(dimension_semantics=sem, vmem_limit_bytes=vmem)


def _const_spec(shape):
    n = len(shape)
    return pl.BlockSpec(shape, lambda *_: (0,) * n)


def _nt_dot(a, b):
    return lax.dot_general(a, b, (((1,), (1,)), ((), ())), preferred_element_type=F32)


def _mod_kernel(c_ref, w_ref, b_ref, o_ref):
    c = c_ref[...]
    s = c / (1.0 + jnp.exp(-c))
    o_ref[0] = jnp.dot(s, w_ref[0], preferred_element_type=F32,
                       precision=lax.Precision.HIGHEST) + b_ref[0]


def _modulation(cond8, w_mod, b_mod):
    depth, d, n6 = w_mod.shape
    tn = n6 // 4
    return pl.pallas_call(
        _mod_kernel,
        grid=(depth, n6 // tn),
        in_specs=[_const_spec((8, d)),
                  pl.BlockSpec((1, d, tn), lambda l, j: (l, 0, j)),
                  pl.BlockSpec((1, 1, tn), lambda l, j: (l, 0, j))],
        out_specs=pl.BlockSpec((1, 8, tn), lambda l, j: (l, 0, j)),
        out_shape=jax.ShapeDtypeStruct((depth, 8, n6), F32),
        compiler_params=_params(("arbitrary", "arbitrary")),
        name="modulation",
    )(cond8, w_mod, b_mod.reshape(depth, 1, n6))


def _rms(z, gain):
    return z * lax.rsqrt(jnp.mean(z * z, axis=-1, keepdims=True) + EPS) * gain


def _head_rms(zh, gain_h, n_real):
    ms = jnp.sum(zh * zh, axis=-1, keepdims=True) * (1.0 / n_real)
    return zh * lax.rsqrt(ms + EPS) * gain_h


def _rope(zh, cos, sin, is_x1):
    rot = jnp.where(is_x1, pltpu.roll(zh, LANE - 8, 1), pltpu.roll(zh, 8, 1))
    return zh * cos + rot * sin


def _is_x1(rows):
    lane = lax.broadcasted_iota(jnp.int32, (rows, LANE), 1)
    first = jnp.where(lane >= MLA_NOPE, jnp.where(lane < MLA_NOPE + 8, 1, 0), 0)
    second = jnp.where(lane >= MLA_NOPE + 16, jnp.where(lane < MLA_NOPE + 24, 1, 0), 0)
    return (first + second) > 0


def _mla_kv(ck, wk_ref, wv_ref, gk_ref, cos, sin, km_ref, vm_ref):
    rows = ck.shape[0]
    kk = jnp.dot(ck, wk_ref[...], preferred_element_type=F32)
    is_x1 = _is_x1(rows)
    for h in range(HEADS):
        sl = slice(h * LANE, (h + 1) * LANE)
        kh = _head_rms(kk[:, sl], gk_ref[:, sl], MLA_QK)
        km_ref[:, sl] = _rope(kh, cos, sin, is_x1).astype(BF16)
    vm_ref[...] = jnp.dot(ck, wv_ref[...], preferred_element_type=F32).astype(BF16)


def _in_kernel(x_ref, mod_ref, n1_ref, w_ref, wuq_ref, wk_ref, wv_ref,
               gq_ref, gk_ref, gcq_ref, gckv_ref, gqm_ref, gkm_ref, cos_ref, sin_ref, after_hbm,
               qn_ref, kn_ref, vn_ref, knf_ref, vnf_ref, p_ref,
               qm_ref, km_ref, vm_ref, ckv_ref, kr_ref):
    del after_hbm
    d = x_ref.shape[1]
    rows = x_ref.shape[0]
    mod = mod_ref[0]
    sh1 = mod[:, 0:d]
    sc1 = mod[:, d:2 * d]
    h = _rms(x_ref[...], n1_ref[...]) * (1.0 + sc1) + sh1
    hb = h.astype(BF16)

    def proj(lo, hi):
        return jnp.dot(hb, w_ref[:, lo:hi], preferred_element_type=F32)

    cos = cos_ref[...]
    sin = sin_ref[...]
    is_x1 = _is_x1(rows)

    zq = proj(_CQ, _CQ + HW)
    zk = proj(_CK, _CK + HW)
    for hh in range(HEADS):
        sl = slice(hh * LANE, (hh + 1) * LANE)
        qn_ref[:, sl] = (_head_rms(zq[:, sl], gq_ref[:, sl], NA_DH) * (NA_DH ** -0.5)).astype(BF16)
        kh = _head_rms(zk[:, sl], gk_ref[:, sl], NA_DH)
        knf_ref[:, sl] = kh
        kn_ref[:, sl] = kh.astype(BF16)
    zv = proj(_CV, _CV + HW)
    vnf_ref[...] = zv
    vn_ref[...] = zv.astype(BF16)
    p_ref[...] = proj(_CP, _CP + 256)

    cq = _rms(proj(_CCQ, _CCQ + 256), gcq_ref[...])
    zqm = jnp.dot(cq.astype(BF16), wuq_ref[...], preferred_element_type=F32)
    for hh in range(HEADS):
        sl = slice(hh * LANE, (hh + 1) * LANE)
        qh = _head_rms(zqm[:, sl], gqm_ref[:, sl], MLA_QK)
        qm_ref[:, sl] = (_rope(qh, cos, sin, is_x1) * (MLA_QK ** -0.5)).astype(BF16)

    ckv = _rms(proj(_CCKV, _CCKV + 128), gckv_ref[...])
    kr = proj(_CKR, _CKR + 128)
    ckv_ref[...] = ckv
    kr_ref[...] = kr
    ck = jnp.concatenate([ckv, kr], axis=-1).astype(BF16)
    _mla_kv(ck, wk_ref, wv_ref, gkm_ref, cos, sin, km_ref, vm_ref)


def _in_proj(x, mod_l, row_off, bpm, lw, cos_t, sin_t, rope_blocks, after):
    n, d = x.shape
    nb = n // TB
    tok = lambda w: pl.BlockSpec((TB, w), lambda i: (i, 0))
    rope_spec = pl.BlockSpec((TB, LANE), lambda i: (i % rope_blocks, 0))
    in_specs = [tok(d),
                pl.BlockSpec((1, 1, mod_l.shape[-1]), lambda i: (row_off + i // bpm, 0, 0)),
                _const_spec((1, d)), _const_spec((d, IN_W)), _const_spec((256, HW)),
                _const_spec((256, HW)), _const_spec((256, HW)),
                _const_spec((1, HW)), _const_spec((1, HW)), _const_spec((1, 256)),
                _const_spec((1, 128)), _const_spec((1, HW)), _const_spec((1, HW)),
                rope_spec, rope_spec, pl.BlockSpec(memory_space=pl.ANY)]
    widths = [(HW, BF16), (HW, BF16), (HW, BF16), (HW, F32), (HW, F32), (256, F32),
              (HW, BF16), (HW, BF16), (HW, BF16), (128, F32), (128, F32)]
    return pl.pallas_call(
        _in_kernel,
        grid=(nb,),
        in_specs=in_specs,
        out_specs=[tok(w) for w, _ in widths],
        out_shape=[jax.ShapeDtypeStruct((n, w), dt) for w, dt in widths],
        compiler_params=_params(("arbitrary",)),
        name="in_proj",
    )(x, mod_l, lw["norm1"], lw["w_in"], lw["w_uq"], lw["w_k"], lw["w_v"],
      lw["g_q"], lw["g_k"], lw["g_cq"], lw["g_ckv"], lw["g_qm"], lw["g_km"], cos_t, sin_t, after)


def _cache_kernel(ck_ref, wk_ref, wv_ref, gk_ref, km_ref, vm_ref):
    rows = ck_ref.shape[2]
    cos = jnp.ones((rows, LANE), F32)
    sin = jnp.zeros((rows, LANE), F32)
    _mla_kv(ck_ref[0, 0], wk_ref.at[0], wv_ref.at[0], gk_ref.at[0], cos, sin,
            km_ref.at[0, 0], vm_ref.at[0, 0])


def _cache_kv(ck, w_k, w_v, g_km):
    db, depth, p, _ = ck.shape
    spec = lambda w: pl.BlockSpec((1, 1, p, w), lambda b, l: (b, l, 0, 0))
    wspec = lambda r: pl.BlockSpec((1, r, HW), lambda b, l: (l, 0, 0))
    return pl.pallas_call(
        _cache_kernel,
        grid=(db, depth),
        in_specs=[spec(256), wspec(256), wspec(256), wspec(1)],
        out_specs=[spec(HW), spec(HW)],
        out_shape=[jax.ShapeDtypeStruct((db, depth, p, HW), BF16)] * 2,
        compiler_params=_params(("arbitrary", "arbitrary")),
        name="cache_kv",
    )(ck, w_k, w_v, g_km)


def _softmax_av(s_list, v_list):
    m = s_list[0].max(axis=-1, keepdims=True)
    for s in s_list[1:]:
        m = jnp.maximum(m, s.max(axis=-1, keepdims=True))
    acc = None
    den = None
    for s, v in zip(s_list, v_list):
        p = jnp.exp(s - m)
        l = p.sum(axis=-1, keepdims=True)
        o = jnp.dot(p.astype(BF16), v, preferred_element_type=F32)
        acc = o if acc is None else acc + o
        den = l if den is None else den + l
    return acc / den


def _ctx_attn_kernel(qn, kn, vn, qm, km, vm, on, om):
    for q, k, v, o in ((qn, kn, vn, on), (qm, km, vm, om)):
        for h in range(HEADS):
            sl = slice(h * LANE, (h + 1) * LANE)
            s = _nt_dot(q[:, sl], k[:, sl])
            o[:, sl] = _softmax_av([s], [v[:, sl]]).astype(BF16)


def _ctx_attn(qn, kn, vn, qm, km, vm, seq):
    n = qn.shape[0]
    spec = pl.BlockSpec((seq, HW), lambda i: (i, 0))
    return pl.pallas_call(
        _ctx_attn_kernel,
        grid=(n // seq,),
        in_specs=[spec] * 6,
        out_specs=[spec] * 2,
        out_shape=[jax.ShapeDtypeStruct((n, HW), BF16)] * 2,
        compiler_params=_params(("arbitrary",)),
        name="ctx_attn",
    )(qn, kn, vn, qm, km, vm)


def _lat_mla_kernel(q, k, v, kc, vc, o):
    s1 = _nt_dot(q[...], k[...])
    s2 = _nt_dot(q[...], kc[0])
    o[...] = _softmax_av([s1, s2], [v[...], vc[0]]).astype(BF16)


def _lat_mla(qm, km, vm, kc, vc, db):
    n = qm.shape[0]
    ds = n // db
    nq = ds // TQ
    qspec = pl.BlockSpec((TQ, LANE), lambda b, h, i: (b * nq + i, h))
    kspec = pl.BlockSpec((ds, LANE), lambda b, h, i: (b, h))
    cspec = pl.BlockSpec((1, kc.shape[1], LANE), lambda b, h, i: (b, 0, h))
    return pl.pallas_call(
        _lat_mla_kernel,
        grid=(db, HEADS, nq),
        in_specs=[qspec, kspec, kspec, cspec, cspec],
        out_specs=qspec,
        out_shape=jax.ShapeDtypeStruct((n, HW), BF16),
        compiler_params=_params(("arbitrary",) * 3),
        name="lat_mla",
    )(qm, km, vm, kc, vc)


def _nat_kernel(q, k, v, kc, vc, bias, o, *, rows):
    r = pl.program_id(1)
    rs = jnp.clip(r - WIN_R // 2, 0, rows - WIN_R)
    start = pl.multiple_of(rs * GRID_W, GRID_W)
    band = WIN_R * GRID_W
    for h in range(HEADS):
        sl = slice(h * LANE, (h + 1) * LANE)
        qh = q[:, sl]
        s1 = _nt_dot(qh, k[pl.ds(start, band), sl]) + bias[0, h]
        s2 = _nt_dot(qh, kc[0, :, sl])
        o[:, sl] = _softmax_av([s1, s2], [v[pl.ds(start, band), sl], vc[0, :, sl]]).astype(BF16)


def _nat_attn(qn, kn, vn, kc, vc, bias, db):
    n = qn.shape[0]
    ds = n // db
    rows = ds // GRID_W
    band = WIN_R * GRID_W

    def variant(r):
        return jnp.where(r < WIN_R // 2, r, jnp.where(r > rows - WIN_R // 2, r - (rows - WIN_R), WIN_R // 2))

    qspec = pl.BlockSpec((GRID_W, HW), lambda b, r: (b * rows + r, 0))
    kspec = pl.BlockSpec((ds, HW), lambda b, r: (b, 0))
    cspec = pl.BlockSpec((1, kc.shape[1], HW), lambda b, r: (b, 0, 0))
    bspec = pl.BlockSpec((1, HEADS, GRID_W, band), lambda b, r: (variant(r), 0, 0, 0))
    return pl.pallas_call(
        functools.partial(_nat_kernel, rows=rows),
        grid=(db, rows),
        in_specs=[qspec, kspec, kspec, cspec, cspec, bspec],
        out_specs=qspec,
        out_shape=jax.ShapeDtypeStruct((n, HW), BF16),
        compiler_params=_params(("arbitrary", "arbitrary")),
        name="nat_attn",
    )(qn, kn, vn, kc, vc, bias)


def _split3(x):
    hi = x.astype(BF16)
    r = x - hi.astype(F32)
    mid = r.astype(BF16)
    lo = (r - mid.astype(F32)).astype(BF16)
    return hi, mid, lo


def _pool(p_prev, p_cur, p_next, posb, seq_len):
    rows = p_cur.shape[0]
    halo = p_prev.shape[0]
    ext = rows + 2 * halo
    pext = jnp.concatenate([p_prev, p_cur, p_next], axis=0)
    parts = _split3(pext)
    t = posb + lax.broadcasted_iota(jnp.int32, (rows, ext), 0)
    s = posb - halo + lax.broadcasted_iota(jnp.int32, (rows, ext), 1)
    tcol = posb + lax.broadcasted_iota(jnp.int32, (rows, 1), 0)
    grp = lax.broadcasted_iota(jnp.int32, (rows, 256), 1) // POOL_G
    d = jnp.zeros((rows, 256), F32)
    for gi, w in enumerate(POOL_WINDOWS):
        lo = jnp.maximum(t - w // 2, 0)
        hi = jnp.minimum(t + (w - w // 2), seq_len)
        sel = jnp.where(s >= lo, jnp.where(s < hi, 1.0, 0.0), 0.0).astype(BF16)
        tot = sum(jnp.dot(sel, part, preferred_element_type=F32) for part in parts)
        cnt = (jnp.minimum(tcol + (w - w // 2), seq_len) - jnp.maximum(tcol - w // 2, 0)).astype(F32)
        d = jnp.where(grp == gi, tot / cnt - p_cur, d)
    return d


def _first_max(x, pos, sentinel):
    m = jnp.max(x, axis=0, keepdims=True)
    idx = jnp.min(jnp.where(x == m, pos, sentinel), axis=0, keepdims=True)
    return m, idx


def _topk_head(qh, sk_ref):
    c = qh.shape[0]
    key_pos = lax.broadcasted_iota(jnp.int32, (PEER_NKEYS, c), 0).astype(F32)
    row16 = lax.broadcasted_iota(jnp.int32, (PEER_TOPK, c), 0)
    neg = jnp.float32(-jnp.inf)
    s0 = _nt_dot(sk_ref[0], qh)
    s1 = _nt_dot(sk_ref[1], qh)

    def stage1(a, carry):
        out = []
        for s, sv, si in (carry[0:3], carry[3:6]):
            m, idx = _first_max(s, key_pos, float(PEER_NKEYS))
            out += [jnp.where(key_pos == idx, neg, s),
                    jnp.where(row16 == a, m, sv), jnp.where(row16 == a, idx, si)]
        return tuple(out)

    zf = jnp.zeros((PEER_TOPK, c), F32)
    _, sv0, si0, _, sv1, si1 = lax.fori_loop(0, PEER_TOPK, stage1, (s0, zf, zf, s1, zf, zf))

    sub8 = lax.broadcasted_iota(jnp.int32, (8, c), 0)
    sub8f = sub8.astype(F32)
    cs, ci, cf = [], [], []

    def piece(a_vals, a_ids, a_flat, b_vals, b_ids, b_flat, nb):
        val = a_vals + b_vals
        if nb < 8:
            val = jnp.where(sub8 < nb, val, neg)
        cs.append(val)
        ci.append(a_ids * float(PEER_NKEYS) + b_ids)
        cf.append(jnp.broadcast_to(a_flat * float(PEER_TOPK) + b_flat, (8, c)))

    for a in range(8):
        nb = PEER_TOPK // (a + 1)
        for b0 in range(0, nb, 8):
            piece(sv0[a:a + 1], si0[a:a + 1], float(a), sv1[b0:b0 + 8], si1[b0:b0 + 8],
                  sub8f + float(b0), min(nb - b0, 8))
    piece(sv0[8:16], si0[8:16], sub8f + 8.0, sv1[0:1], si1[0:1], jnp.zeros((8, c), F32), 8)
    npc = len(cs)
    nflat = float(PEER_TOPK * PEER_TOPK)

    def stage2(k, carry):
        vals = list(carry[:npc])
        tv, te = carry[npc], carry[npc + 1]
        m = vals[0]
        for v in vals[1:]:
            m = jnp.maximum(m, v)
        m = jnp.max(m, axis=0, keepdims=True)
        pos = None
        for v, f in zip(vals, cf):
            cand = jnp.where(v == m, f, nflat)
            pos = cand if pos is None else jnp.minimum(pos, cand)
        pos = jnp.min(pos, axis=0, keepdims=True)
        e = None
        for i, f in zip(ci, cf):
            cand = jnp.where(f == pos, i, -1.0)
            e = cand if e is None else jnp.maximum(e, cand)
        e = jnp.max(e, axis=0, keepdims=True)
        vals = [jnp.where(f == pos, neg, v) for v, f in zip(vals, cf)]
        return tuple(vals) + (jnp.where(row16 == k, m, tv), jnp.where(row16 == k, e, te))

    res = lax.fori_loop(0, PEER_TOPK, stage2, tuple(cs) + (zf, zf))
    return res[npc], res[npc + 1]


def _out_kernel(on_ref, om_ref, pc_ref, pp_ref, pn_ref, x_ref, mod_ref,
                won_ref, wop_ref, wom_ref, pw_ref, ps_ref, n2_ref, wq_ref, sk_ref,
                x1_ref, h2_ref, ids_ref, gt_ref, gn_ref, h2c_ref, q_scr, idt_scr, *, bps, seq_len):
    d = x_ref.shape[1]
    rows = x_ref.shape[0]
    i = pl.program_id(0)
    mod = mod_ref[0]
    g1 = mod[:, 2 * d:3 * d]
    sh2 = mod[:, 3 * d:4 * d]
    sc2 = mod[:, 4 * d:5 * d]

    posb = (i % bps) * rows
    dpool = _pool(pp_ref[...], pc_ref[...], pn_ref[...], posb, seq_len)
    ypool = jnp.dot(dpool.astype(BF16), pw_ref[...], preferred_element_type=F32) * ps_ref[...]
    mix = (jnp.dot(on_ref[...], won_ref[...], preferred_element_type=F32)
           + jnp.dot(ypool.astype(BF16), wop_ref[...], preferred_element_type=F32)
           + jnp.dot(om_ref[...], wom_ref[...], preferred_element_type=F32))
    x1 = x_ref[...] + g1 * mix
    x1_ref[...] = x1
    h2 = _rms(x1, n2_ref[...]) * (1.0 + sc2) + sh2
    h2_ref[...] = h2
    for j in range(d // LANE):
        h2c_ref[j] = h2[:, j * LANE:(j + 1) * LANE]

    q = jnp.dot(h2.astype(BF16), wq_ref[...], preferred_element_type=F32)
    for hh in range(PEER_HEADS):
        q_scr[hh] = q[:, hh * LANE:(hh + 1) * LANE].astype(BF16)

    for c0 in range(0, rows, LANE):
        def head(hh, _):
            tv, te = _topk_head(q_scr[hh, c0:c0 + LANE, :], sk_ref)
            ex = jnp.exp(tv - tv[0:1])
            gates = ex / jnp.sum(ex, axis=0, keepdims=True)
            r0 = pl.multiple_of(hh * PEER_TOPK, PEER_TOPK)
            gt_ref[pl.ds(r0, PEER_TOPK), c0:c0 + LANE] = gates
            idt_scr[pl.ds(r0, PEER_TOPK), c0:c0 + LANE] = te
            return 0

        lax.fori_loop(0, PEER_HEADS, head, 0)
    ids_ref[...] = idt_scr[...].T.astype(jnp.int32)
    gn_ref[...] = gt_ref[...].T


def _out_proj(x, on, om, p, mod_l, row_off, bpm, lw, seq_len):
    n, d = x.shape
    nb = n // TB
    bps = seq_len // TB
    halo = 8
    hb = TB // halo
    tok = lambda w: pl.BlockSpec((TB, w), lambda i: (i, 0))
    in_specs = [tok(HW), tok(HW), tok(256),
                pl.BlockSpec((halo, 256), lambda i: (jnp.maximum(i * hb - 1, 0), 0)),
                pl.BlockSpec((halo, 256), lambda i: (jnp.minimum((i + 1) * hb, n // halo - 1), 0)),
                tok(d),
                pl.BlockSpec((1, 1, mod_l.shape[-1]), lambda i: (row_off + i // bpm, 0, 0)),
                _const_spec((HW, d)), _const_spec((256, d)), _const_spec((HW, d)),
                _const_spec((256, 256)), _const_spec((1, 256)), _const_spec((1, d)),
                _const_spec((d, PEER_HEADS * LANE)), _const_spec((2, PEER_NKEYS, LANE))]
    nk = PEER_HEADS * PEER_TOPK
    return pl.pallas_call(
        functools.partial(_out_kernel, bps=bps, seq_len=seq_len),
        grid=(nb,),
        in_specs=in_specs,
        out_specs=[tok(d), tok(d), tok(nk), pl.BlockSpec((nk, TB), lambda i: (0, i)), tok(nk),
                   pl.BlockSpec((d // LANE, TB, LANE), lambda i: (0, i, 0))],
        out_shape=[jax.ShapeDtypeStruct((n, d), F32), jax.ShapeDtypeStruct((n, d), F32),
                   jax.ShapeDtypeStruct((n, nk), jnp.int32), jax.ShapeDtypeStruct((nk, n), F32),
                   jax.ShapeDtypeStruct((n, nk), F32),
                   jax.ShapeDtypeStruct((d // LANE, n, LANE), F32)],
        scratch_shapes=[pltpu.VMEM((PEER_HEADS, TB, LANE), BF16), pltpu.VMEM((nk, TB), F32)],
        compiler_params=_params(("arbitrary",)),
        name="out_proj",
    )(on, om, p, p, p, x, mod_l, lw["w_o_na"], lw["w_o_pool"], lw["w_o_mla"],
      lw["pool_w"], lw["pool_scale"], lw["norm2"], lw["peer_wq"], lw["peer_sk"])


def _gelu_tanh(x):
    return x * (0.5 * (1.0 + jnp.tanh(0.7978845608028654 * (x + 0.044715 * (x * x * x)))))


def _peer_token_mix(chunk, hrow, gcol, ch):
    acc = None
    for s in range(ch):
        us = lax.bitcast_convert_type(chunk(s) & jnp.int32(-65536), F32)
        term = us * hrow[:, s * LANE:(s + 1) * LANE]
        acc = term if acc is None else acc + term
    wgt = gcol * _gelu_tanh(jnp.sum(acc, axis=-1, keepdims=True))
    parts = []
    for s in range(ch):
        vs = lax.bitcast_convert_type(chunk(s) << 16, F32)
        parts.append(jnp.sum(vs * wgt, axis=0, keepdims=True))
    return jnp.concatenate(parts, axis=-1)


def _peer_staged_kernel(rows_ref, gt_ref, h2_ref, x1_ref, mod_ref, o_ref):
    d = x1_ref.shape[1]
    ch = d // LANE
    nk = gt_ref.shape[0]
    g2 = mod_ref[0][:, 5 * d:6 * d]
    tok_lane = lax.broadcasted_iota(jnp.int32, gt_ref.shape, 1)
    base = (pl.program_id(0) % (PEER_TB // PEER_SUB)) * PEER_SUB
    h8 = h2_ref[...]
    ys = []
    for t in range(PEER_SUB):
        chunk = lambda s: rows_ref[pl.ds(t * nk * ch + s, nk, stride=ch), :]
        gcol = jnp.sum(jnp.where(tok_lane == base + t, gt_ref[...], 0.0), axis=-1, keepdims=True)
        ys.append(_peer_token_mix(chunk, h8[t:t + 1, :], gcol, ch))
    o_ref[...] = x1_ref[...] + g2 * jnp.concatenate(ys, axis=0)


def _peer_staged(x1, h2, rows, gt, mod_l, row_off, tpm):
    n, d = x1.shape
    nk = gt.shape[0]
    per = PEER_SUB * nk * (d // LANE)
    sub_per_tb = PEER_TB // PEER_SUB
    tok = pl.BlockSpec((PEER_SUB, d), lambda j: (j, 0))
    return pl.pallas_call(
        _peer_staged_kernel,
        grid=(n // PEER_SUB,),
        in_specs=[pl.BlockSpec((per, LANE), lambda j: (j, 0)),
                  pl.BlockSpec((nk, PEER_TB), lambda j: (0, j // sub_per_tb)),
                  tok, tok,
                  pl.BlockSpec((1, 1, mod_l.shape[-1]), lambda j: (row_off + (j * PEER_SUB) // tpm, 0, 0))],
        out_specs=tok,
        out_shape=jax.ShapeDtypeStruct((n, d), F32),
        compiler_params=_params(("arbitrary",)),
        name="peer_staged",
    )(rows, gt, h2, x1, mod_l)


def _sc_gather(table3, ids_flat):
    m = ids_flat.shape[0]
    _, ch, lane = table3.shape
    info = plsc.get_sparse_core_info()
    nc, nw = info.num_cores, info.num_cores * info.num_subcores
    idx_win = 128
    win = 32
    per_w = m // nw
    assert m % (nw * idx_win) == 0
    mesh = plsc.VectorSubcoreMesh(core_axis_name="core", subcore_axis_name="subcore")

    @functools.partial(
        pl.kernel, mesh=mesh,
        out_type=jax.ShapeDtypeStruct((m, ch, lane), table3.dtype),
        scratch_types=[pltpu.VMEM((idx_win,), jnp.int32),
                       pltpu.VMEM((win, ch, lane), table3.dtype),
                       pltpu.VMEM((win, ch, lane), table3.dtype),
                       pltpu.SemaphoreType.DMA, pltpu.SemaphoreType.DMA])
    def gather(tab_hbm, idx_hbm, out_hbm, idx_v, rows_a, rows_b, sem_a, sem_b):
        wid = lax.axis_index("subcore") * nc + lax.axis_index("core")
        bufs = ((rows_a, sem_a), (rows_b, sem_b))
        nq = idx_win // win

        def fetch(q):
            rows, sem = bufs[q % 2]
            return pltpu.make_async_copy(tab_hbm.at[idx_v.at[pl.ds(q * win, win)]], rows, sem)

        @pl.loop(0, per_w // idx_win)
        def _(g):
            base = pl.multiple_of(wid * per_w + g * idx_win, idx_win)
            pltpu.sync_copy(idx_hbm.at[pl.ds(base, idx_win)], idx_v)
            fetch(0).start()
            for q in range(nq):
                fetch(q).wait()
                if q + 1 < nq:
                    fetch(q + 1).start()
                pltpu.sync_copy(bufs[q % 2][0], out_hbm.at[pl.ds(base + q * win, win)])

    return gather(table3, ids_flat)


def _sc_peer(table3, ids, gates, h2c, n):
    ch, _, lane = h2c.shape
    nk = ids.shape[1]
    info = plsc.get_sparse_core_info()
    nc, nw, nl = info.num_cores, info.num_cores * info.num_subcores, info.num_lanes
    tpw = n // nw
    win = 32
    nq = nk // win
    cpr = lane // nl
    nchunk = ch * cpr
    hc = nchunk // 2
    assert n % nw == 0 and nk % win == 0 and win % nl == 0
    mesh = plsc.VectorSubcoreMesh(core_axis_name="core", subcore_axis_name="subcore")
    hi_mask = jnp.int32(-65536)

    @functools.partial(
        pl.kernel, mesh=mesh,
        out_type=jax.ShapeDtypeStruct((ch, n, lane), F32),
        compiler_params=pltpu.CompilerParams(needs_layout_passes=False),
        scratch_types=[pltpu.VMEM((nk,), jnp.int32), pltpu.VMEM((nk,), F32),
                       pltpu.VMEM((ch, lane), F32), pltpu.VMEM((ch, lane), F32),
                       pltpu.VMEM((win, ch, lane), jnp.int32), pltpu.VMEM((win, ch, lane), jnp.int32),
                       pltpu.VMEM((win * nl,), F32), pltpu.VMEM((win,), F32),
                       pltpu.SemaphoreType.DMA, pltpu.SemaphoreType.DMA, pltpu.SemaphoreType.DMA])
    def peer(tab_hbm, ids_hbm, g_hbm, h2_hbm, y_hbm,
             idx_v, g_v, x_v, y_v, rows_a, rows_b, part_v, w_v, sem_a, sem_b, sem_x):
        wid = lax.axis_index("subcore") * nc + lax.axis_index("core")
        bufs = ((rows_a, sem_a), (rows_b, sem_b))
        lanes = lax.iota(jnp.int32, nl)
        zero = jnp.zeros((nl,), F32)

        def chunk_copies(tok, to_hbm):
            if to_hbm:
                return [pltpu.make_async_copy(y_v.at[j], y_hbm.at[j, tok], sem_x) for j in range(ch)]
            return [pltpu.make_async_copy(h2_hbm.at[j, tok], x_v.at[j], sem_x) for j in range(ch)]

        def fetch(q):
            rows, sem = bufs[q % 2]
            return pltpu.make_async_copy(tab_hbm.at[idx_v.at[pl.ds(q * win, win)]], rows, sem)

        def word(rows, r, cc):
            return rows[r, cc // cpr, pl.ds((cc % cpr) * nl, nl)]

        @pl.loop(0, tpw)
        def _(ti):
            tok = wid * tpw + ti
            loads = chunk_copies(tok, False)
            for cp in loads:
                cp.start()
            pltpu.sync_copy(ids_hbm.at[tok], idx_v)
            pltpu.sync_copy(g_hbm.at[tok], g_v)
            for cp in loads:
                cp.wait()
            for cc in range(nchunk):
                y_v[cc // cpr, pl.ds((cc % cpr) * nl, nl)] = zero
            fetch(0).start()
            for q in range(nq):
                rows = bufs[q % 2][0]
                fetch(q).wait()
                if q + 1 < nq:
                    fetch(q + 1).start()

                for half in range(2):
                    xs = [x_v[(half * hc + c) // cpr, pl.ds(((half * hc + c) % cpr) * nl, nl)]
                          for c in range(hc)]

                    @pl.loop(0, win)
                    def _(r):
                        accs = [None] * 4
                        for c in range(hc):
                            u = lax.bitcast_convert_type(word(rows, r, half * hc + c) & hi_mask, F32)
                            t = u * xs[c]
                            accs[c % 4] = t if accs[c % 4] is None else accs[c % 4] + t
                        acc = (accs[0] + accs[1]) + (accs[2] + accs[3])
                        po = pl.multiple_of(r * nl, nl)
                        if half == 0:
                            part_v[pl.ds(po, nl)] = acc
                        else:
                            part_v[pl.ds(po, nl)] = part_v[pl.ds(po, nl)] + acc

                for grp in range(win // nl):
                    s = zero
                    for rr in range(nl):
                        tot = jnp.sum(part_v[pl.ds((grp * nl + rr) * nl, nl)])
                        s = jnp.where(lanes == rr, tot, s)
                    z = 0.7978845608028654 * (s + 0.044715 * (s * s * s))
                    tanh = 1.0 - 2.0 / (jnp.exp(2.0 * z) + 1.0)
                    gate = g_v[pl.ds(q * win + grp * nl, nl)]
                    w_v[pl.ds(grp * nl, nl)] = gate * (s * (0.5 * (1.0 + tanh)))

                for half in range(2):
                    def body(r, yacc):
                        wr = plsc.load_gather(w_v, [jnp.full((nl,), r, jnp.int32)])
                        out = []
                        for c in range(hc):
                            v = lax.bitcast_convert_type(word(rows, r, half * hc + c) << 16, F32)
                            out.append(yacc[c] + wr * v)
                        return tuple(out)

                    yacc = lax.fori_loop(0, win, body, tuple(zero for _ in range(hc)))
                    for c in range(hc):
                        cc = half * hc + c
                        sl = (cc // cpr, pl.ds((cc % cpr) * nl, nl))
                        y_v[sl] = y_v[sl] + yacc[c]
            stores = chunk_copies(tok, True)
            for cp in stores:
                cp.start()
            for cp in stores:
                cp.wait()

    return peer(table3, ids, gates, h2c)


def _residual_kernel(x1_ref, y_ref, mod_ref, x2_hbm, o_ref):
    del x2_hbm
    d = x1_ref.shape[1]
    g2 = mod_ref[0][:, 5 * d:6 * d]
    for j in range(d // LANE):
        sl = slice(j * LANE, (j + 1) * LANE)
        o_ref[:, sl] = x1_ref[:, sl] + g2[:, sl] * y_ref[j]


def _residual(x1, y, mod_l, row, x2):
    n, d = x1.shape
    tok = pl.BlockSpec((PEER_TB, d), lambda i: (i, 0))
    return pl.pallas_call(
        _residual_kernel,
        grid=(y.shape[1] // PEER_TB,),
        in_specs=[tok, pl.BlockSpec((d // LANE, PEER_TB, LANE), lambda i: (0, i, 0)),
                  pl.BlockSpec((1, 1, mod_l.shape[-1]), lambda i: (row, 0, 0)),
                  pl.BlockSpec(memory_space=pl.ANY)],
        out_specs=tok,
        out_shape=jax.ShapeDtypeStruct((n, d), F32),
        input_output_aliases={3: 0},
        compiler_params=_params(("arbitrary",)),
        name="residual",
    )(x1, y, mod_l, x2)


def _peer_kernel(ids_hbm, gt_ref, h2_ref, x1_ref, mod_ref, tab_hbm, o_ref,
                 ids_s, buf, sem_i, sem_r, *, first_block):
    d = x1_ref.shape[1]
    ch = d // LANE
    pitch = ch + 1
    nsub = x1_ref.shape[0] // PEER_SUB
    nk = gt_ref.shape[0]
    nids = PEER_SUB * nk
    i = pl.program_id(0) + first_block
    g2 = mod_ref[0][:, 5 * d:6 * d]
    tok_lane = lax.broadcasted_iota(jnp.int32, gt_ref.shape, 1)

    def ids_copy(j, slot):
        start = pl.multiple_of((i * nsub + j) * nids, nids)
        return pltpu.make_async_copy(ids_hbm.at[pl.ds(start, nids)],
                                     ids_s.at[pl.ds(slot * nids, nids)], sem_i.at[slot])

    def row_copy(slot, e, f):
        src = tab_hbm.at[pl.ds(pl.multiple_of(e * ch, ch), ch), :]
        dst = buf.at[slot, pl.ds(f * pitch, ch), :]
        return pltpu.make_async_copy(src, dst, sem_r.at[slot])

    def issue_rows(slot):
        for t in range(PEER_SUB):
            def body(kk, _):
                for r in range(8):
                    f = t * nk + kk * 8 + r
                    row_copy(slot, ids_s[slot * nids + f], f).start(priority=r % 2)
                return 0

            lax.fori_loop(0, nk // 8, body, 0)

    def wait_rows(slot):
        done = buf.at[slot, pl.ds(0, nids * ch), :]
        pltpu.make_async_copy(done, done, sem_r.at[slot]).wait()

    def compute(slot, j):
        base = pl.multiple_of(j * PEER_SUB, PEER_SUB)
        h8 = h2_ref[pl.ds(base, PEER_SUB), :]
        ys = []
        for t in range(PEER_SUB):
            chunk = lambda s: buf[slot, pl.ds(t * nk * pitch + s, nk, stride=pitch), :]
            gcol = jnp.sum(jnp.where(tok_lane == base + t, gt_ref[...], 0.0), axis=-1, keepdims=True)
            ys.append(_peer_token_mix(chunk, h8[t:t + 1, :], gcol, ch))
        y8 = jnp.concatenate(ys, axis=0)
        o_ref[pl.ds(base, PEER_SUB), :] = x1_ref[pl.ds(base, PEER_SUB), :] + g2 * y8

    first = ids_copy(0, 0)
    first.start()
    first.wait()
    issue_rows(0)
    ids_copy(1, 1).start()

    def pair(jj, _):
        j0 = 2 * jj
        ids_copy(j0 + 1, 1).wait()
        issue_rows(1)

        @pl.when(j0 + 2 < nsub)
        def _():
            ids_copy(j0 + 2, 0).start()

        wait_rows(0)
        compute(0, j0)

        @pl.when(j0 + 2 < nsub)
        def _():
            ids_copy(j0 + 2, 0).wait()
            issue_rows(0)

        @pl.when(j0 + 3 < nsub)
        def _():
            ids_copy(j0 + 3, 1).start()

        wait_rows(1)
        compute(1, j0 + 1)
        return 0

    lax.fori_loop(0, nsub // 2, pair, 0)


def _pack_tables(peer_u, peer_v):
    e, d = peer_u.shape
    ub = lax.bitcast_convert_type(peer_u.astype(BF16), jnp.uint16).astype(jnp.uint32)
    vb = lax.bitcast_convert_type(peer_v.astype(BF16), jnp.uint16).astype(jnp.uint32)
    words = lax.bitcast_convert_type((ub << 16) | vb, jnp.int32)
    return words.reshape(e, d // LANE, LANE)


def _peer(x1, h2, ids, gt, mod_l, row, table, tok0):
    n, d = x1.shape
    nk = gt.shape[0]
    b0 = tok0 // PEER_TB
    nb = n // PEER_TB - b0
    tok = pl.BlockSpec((PEER_TB, d), lambda i: (i + b0, 0))
    any_spec = pl.BlockSpec(memory_space=pl.ANY)
    return pl.pallas_call(
        functools.partial(_peer_kernel, first_block=b0),
        grid=(nb,),
        in_specs=[any_spec,
                  pl.BlockSpec((nk, PEER_TB), lambda i: (0, i + b0)),
                  tok, tok,
                  pl.BlockSpec((1, 1, mod_l.shape[-1]), lambda i: (row, 0, 0)),
                  any_spec],
        out_specs=tok,
        out_shape=jax.ShapeDtypeStruct((n, d), F32),
        scratch_shapes=[pltpu.SMEM((2 * PEER_SUB * nk,), jnp.int32),
                        pltpu.VMEM((2, PEER_SUB * nk * (d // LANE + 1), LANE), jnp.int32),
                        pltpu.SemaphoreType.DMA((2,)),
                        pltpu.SemaphoreType.DMA((2,))],
        compiler_params=_params(("arbitrary",)),
        name="peer",
    )(ids.reshape(n * nk), gt, h2, x1, mod_l, table.reshape(-1, LANE))


def _pad_heads(w, width):
    pad = [(0, 0)] * (w.ndim - 1) + [(0, LANE - width)]
    w = jnp.pad(w, pad)
    return w.reshape(w.shape[:-2] + (HW,))


def _head_gain(g, width):
    depth = g.shape[0]
    g = jnp.pad(g, ((0, 0), (0, LANE - width)))
    return jnp.tile(g, (1, HEADS)).reshape(depth, 1, HW)


def _rope_tables(seq):
    t = np.arange(seq)
    half = MLA_ROPE // 2
    inv = ROPE_THETA ** (-np.arange(0, half, 2, dtype=np.float32) / half)
    cos = np.ones((seq, LANE), np.float32)
    sin = np.zeros((seq, LANE), np.float32)
    for off, pos in ((MLA_NOPE, t // GRID_W), (MLA_NOPE + half, t % GRID_W)):
        ang = pos.astype(np.float32)[:, None] * inv[None, :]
        q = half // 2
        cos[:, off:off + q] = np.cos(ang)
        cos[:, off + q:off + half] = np.cos(ang)
        sin[:, off:off + q] = -np.sin(ang)
        sin[:, off + q:off + half] = np.sin(ang)
    return jnp.asarray(cos), jnp.asarray(sin)


def _nat_bias(rel_bias):
    v = np.arange(WIN_R)[:, None]
    j = np.arange(WIN_R)[None, :]
    dr = j - v + WIN_R - 1
    cq = np.arange(GRID_W)[:, None]
    kc = np.arange(GRID_W)[None, :]
    cstart = np.clip(cq - WIN_C // 2, 0, GRID_W - WIN_C)
    ok = (kc >= cstart) & (kc < cstart + WIN_C)
    dc = np.clip(kc - cq + WIN_C - 1, 0, 2 * WIN_C - 2)
    b = rel_bias[:, :, dr]
    b = b[..., dc]
    b = jnp.where(jnp.asarray(ok)[None, None, None, None], b, NEG_INF)
    b = jnp.transpose(b, (0, 2, 1, 4, 3, 5))
    return b.reshape(b.shape[0], WIN_R, HEADS, GRID_W, WIN_R * GRID_W)


def _layer_weights(w_in, na_q_norm, na_k_norm, mla_cq_norm, mla_ckv_norm, mla_w_uq, mla_w_ukv,
                   mla_q_norm, mla_k_norm, w_out, pool_w, pool_scale, norm1, norm2,
                   peer_wq, peer_subkeys):
    depth, d, _ = w_in.shape
    na_w = HEADS * NA_DH
    segs = np.cumsum([0, na_w, na_w, na_w, 256, 256, 128, MLA_ROPE])
    part = lambda i: w_in[:, :, segs[i]:segs[i + 1]]
    heads = lambda w: _pad_heads(w.reshape(depth, d, HEADS, NA_DH), NA_DH)
    w_in_p = jnp.concatenate(
        [heads(part(0)), heads(part(1)), heads(part(2)), part(3), part(4), part(5),
         jnp.pad(part(6), ((0, 0), (0, 0), (0, LANE - MLA_ROPE)))], axis=-1).astype(BF16)

    w_uq = _pad_heads(mla_w_uq, MLA_QK).astype(BF16)
    k_nope = _pad_heads(mla_w_ukv[..., :MLA_NOPE], MLA_NOPE)
    eye = np.zeros((MLA_ROPE, HEADS, LANE), np.float32)
    for h in range(HEADS):
        eye[np.arange(MLA_ROPE), h, MLA_NOPE + np.arange(MLA_ROPE)] = 1.0
    eye = jnp.broadcast_to(jnp.asarray(eye.reshape(MLA_ROPE, HW)), (depth, MLA_ROPE, HW))
    zer = jnp.zeros((depth, 256 - 128 - MLA_ROPE, HW), F32)
    w_k = jnp.concatenate([k_nope, eye, zer], axis=1).astype(BF16)
    w_v = jnp.concatenate([_pad_heads(mla_w_ukv[..., MLA_NOPE:], MLA_V),
                           jnp.zeros((depth, 128, HW), F32)], axis=1).astype(BF16)

    mix_w = HEADS * NA_DH
    w_o_na = jnp.pad(w_out[:, :mix_w].reshape(depth, HEADS, NA_DH, d),
                     ((0, 0), (0, 0), (0, LANE - NA_DH), (0, 0))).reshape(depth, HW, d).astype(BF16)
    w_o_pool = w_out[:, mix_w:mix_w + 256].astype(BF16)
    w_o_mla = jnp.pad(w_out[:, mix_w + 256:].reshape(depth, HEADS, MLA_V, d),
                      ((0, 0), (0, 0), (0, LANE - MLA_V), (0, 0))).reshape(depth, HW, d).astype(BF16)
    ng = len(POOL_WINDOWS)
    pw = jnp.zeros((depth, ng * POOL_G, ng * POOL_G), F32)
    for g in range(ng):
        pw = pw.at[:, g * POOL_G:(g + 1) * POOL_G, g * POOL_G:(g + 1) * POOL_G].set(pool_w[:, g])

    half = peer_subkeys.shape[-1]
    sk = jnp.stack([jnp.pad(peer_subkeys[:, 0], ((0, 0), (0, 0), (0, LANE - half))),
                    jnp.pad(peer_subkeys[:, 1], ((0, 0), (0, 0), (LANE - half, 0)))], axis=1).astype(BF16)

    return dict(
        w_in=w_in_p, w_uq=w_uq, w_k=w_k, w_v=w_v,
        g_q=_head_gain(na_q_norm, NA_DH), g_k=_head_gain(na_k_norm, NA_DH),
        g_cq=mla_cq_norm[:, None, :], g_ckv=mla_ckv_norm[:, None, :],
        g_qm=_head_gain(mla_q_norm, MLA_QK), g_km=_head_gain(mla_k_norm, MLA_QK),
        w_o_na=w_o_na, w_o_pool=w_o_pool, w_o_mla=w_o_mla,
        pool_w=pw.astype(BF16), pool_scale=pool_scale[:, None, :],
        norm1=norm1[:, None, :], norm2=norm2[:, None, :],
        peer_wq=peer_wq.astype(BF16), peer_sk=sk)


def kernel(x_prompt, x_sample, c, cache_nat_k, cache_nat_v, cache_mla_ckv, cache_mla_krope, c_ctx, w_mod, b_mod, norm1, norm2, w_in, na_q_norm, na_k_norm, na_rel_bias, pool_w, pool_scale, mla_cq_norm, mla_ckv_norm, mla_w_uq, mla_w_ukv, mla_q_norm, mla_k_norm, w_out, peer_wq, peer_subkeys, peer_u, peer_v):
    batch, seq, d = x_prompt.shape
    db, ds, _ = x_sample.shape
    depth = w_mod.shape[0]
    past = cache_nat_k.shape[2]
    assert seq == TB and ds % TB == 0 and ds % (GRID_W * WIN_R) == 0 and db + 1 <= 8

    cond8 = jnp.concatenate([c_ctx[None, :], c, jnp.zeros((8 - 1 - db, d), F32)], axis=0)
    mod = _modulation(cond8, w_mod, b_mod).reshape(depth, 8, 1, 6 * d)

    lw_all = _layer_weights(w_in, na_q_norm, na_k_norm, mla_cq_norm, mla_ckv_norm, mla_w_uq,
                            mla_w_ukv, mla_q_norm, mla_k_norm, w_out, pool_w, pool_scale,
                            norm1, norm2, peer_wq, peer_subkeys)
    bias_all = _nat_bias(na_rel_bias)
    tables = [_pack_tables(peer_u[l], peer_v[l]) for l in range(depth)]
    cos_lat, sin_lat = _rope_tables(ds)
    cos_ctx = jnp.ones((TB, LANE), F32)
    sin_ctx = jnp.zeros((TB, LANE), F32)

    ck = jnp.concatenate([cache_mla_ckv, cache_mla_krope,
                          jnp.zeros(cache_mla_ckv.shape[:-1] + (256 - 128 - MLA_ROPE,), F32)],
                         axis=-1).astype(BF16)
    kc_mla, vc_mla = _cache_kv(ck, lw_all["w_k"], lw_all["w_v"], lw_all["g_km"])
    kc_na = _pad_heads(cache_nat_k, NA_DH).astype(BF16)
    vc_na = _pad_heads(cache_nat_v, NA_DH).astype(BF16)

    xs = [x_prompt.reshape(batch * seq, d)] + [x_sample[b] for b in range(db)]
    one_row = max(batch * seq, ds) + 1
    lat_bpm = ds // TB
    ks, vs, ckvs, krs = [], [], [], []
    pending = None

    def join(item):
        si, x1, y_sc, x2, mod_l = item
        xs[si] = _residual(x1, y_sc, mod_l, si, x2)

    for l in range(depth):
        lw = {k: v[l] for k, v in lw_all.items()}
        mod_l = mod[l]
        for si in range(db + 1):
            x = xs[si]
            after = x if pending is None else pending[3]
            if si == 0:
                (qn, kn, vn, knf, vnf, p, qm, km, vm, ckv, kr) = _in_proj(
                    x, mod_l, 0, one_row, lw, cos_ctx, sin_ctx, 1, after)
                on, om = _ctx_attn(qn, kn, vn, qm, km, vm, seq)
                x1, h2, ids, gt, gn, h2c = _out_proj(x, on, om, p, mod_l, 0, one_row, lw, seq)
                ks.append(knf.reshape(batch, seq, HEADS, LANE)[..., :NA_DH])
                vs.append(vnf.reshape(batch, seq, HEADS, LANE)[..., :NA_DH])
                ckvs.append(ckv.reshape(batch, seq, 128))
                krs.append(kr.reshape(batch, seq, LANE)[..., :MLA_ROPE])
            else:
                b = si - 1
                (qn, kn, vn, _, _, p, qm, km, vm, _, _) = _in_proj(
                    x, mod_l, si, one_row, lw, cos_lat, sin_lat, lat_bpm, after)
                on = _nat_attn(qn, kn, vn, kc_na[b:b + 1, l], vc_na[b:b + 1, l], bias_all[l], 1)
                om = _lat_mla(qm, km, vm, kc_mla[b:b + 1, l], vc_mla[b:b + 1, l], 1)
                x1, h2, ids, gt, gn, h2c = _out_proj(x, on, om, p, mod_l, si, one_row, lw, ds)
            n_sc = x.shape[0] * SC_SHARE[0] // SC_SHARE[1] // PEER_TB * PEER_TB
            y_sc = _sc_peer(tables[l], ids, gn, h2c, n_sc)
            x2 = _peer(x1, h2, ids, gt, mod_l, si, tables[l], n_sc)
            if pending is not None:
                join(pending)
            pending = (si, x1, y_sc, x2, mod_l)
    join(pending)

    return (xs[0].reshape(batch, seq, d), jnp.stack(xs[1:], axis=0),
            jnp.stack(ks, axis=1), jnp.stack(vs, axis=1),
            jnp.stack(ckvs, axis=1), jnp.stack(krs, axis=1))
```

```python
import functools

import numpy as np
import jax
import jax.numpy as jnp
from jax import lax
from jax.experimental import pallas as pl
from jax.experimental.pallas import tpu as pltpu
from jax.experimental.pallas import tpu_sc as plsc

F32 = jnp.float32
BF16 = jnp.bfloat16

EPS = 1e-6
ROPE_THETA = 10000.0
NEG_INF = -1e30
GRID_W = 64
HEADS = 6
NA_DH = 64
WIN_R = 8
WIN_C = 16
POOL_WINDOWS = (2, 4, 8, 16)
POOL_G = 64
MLA_NOPE = 64
MLA_ROPE = 32
MLA_QK = MLA_NOPE + MLA_ROPE
MLA_V = 64
PEER_HEADS = 8
PEER_NKEYS = 128
PEER_TOPK = 16
LANE = 128
HW = HEADS * LANE
TB = 256
TQ = 256
PEER_TB = 128
PEER_SUB = 8
VMEM_LIMIT = 56 * 1024 * 1024
SC_SHARE = (12, 16)

_CQ, _CK, _CV = 0, HW, 2 * HW
_CP = 3 * HW
_CCQ = _CP + 256
_CCKV = _CCQ + 256
_CKR = _CCKV + 128
IN_W = _CKR + 128


def _params(sem, vmem=VMEM_LIMIT):
    return pltpu.CompilerParams(dimension_semantics=sem, vmem_limit_bytes=vmem)


def _const_spec(shape):
    n = len(shape)
    return pl.BlockSpec(shape, lambda *_: (0,) * n)


def _nt_dot(a, b):
    return lax.dot_general(a, b, (((1,), (1,)), ((), ())), preferred_element_type=F32)


def _mod_kernel(c_ref, w_ref, b_ref, o_ref):
    c = c_ref[...]
    s = c / (1.0 + jnp.exp(-c))
    o_ref[0] = jnp.dot(s, w_ref[0], preferred_element_type=F32,
                       precision=lax.Precision.HIGHEST) + b_ref[0]


def _modulation(cond8, w_mod, b_mod):
    depth, d, n6 = w_mod.shape
    tn = n6 // 4
    return pl.pallas_call(
        _mod_kernel,
        grid=(depth, n6 // tn),
        in_specs=[_const_spec((8, d)),
                  pl.BlockSpec((1, d, tn), lambda l, j: (l, 0, j)),
                  pl.BlockSpec((1, 1, tn), lambda l, j: (l, 0, j))],
        out_specs=pl.BlockSpec((1, 8, tn), lambda l, j: (l, 0, j)),
        out_shape=jax.ShapeDtypeStruct((depth, 8, n6), F32),
        compiler_params=_params(("arbitrary", "arbitrary")),
        name="modulation",
    )(cond8, w_mod, b_mod.reshape(depth, 1, n6))


def _rms(z, gain):
    return z * lax.rsqrt(jnp.mean(z * z, axis=-1, keepdims=True) + EPS) * gain


def _head_rms(zh, gain_h, n_real):
    ms = jnp.sum(zh * zh, axis=-1, keepdims=True) * (1.0 / n_real)
    return zh * lax.rsqrt(ms + EPS) * gain_h


def _rope(zh, cos, sin, is_x1):
    rot = jnp.where(is_x1, pltpu.roll(zh, LANE - 8, 1), pltpu.roll(zh, 8, 1))
    return zh * cos + rot * sin


def _is_x1(rows):
    lane = lax.broadcasted_iota(jnp.int32, (rows, LANE), 1)
    first = jnp.where(lane >= MLA_NOPE, jnp.where(lane < MLA_NOPE + 8, 1, 0), 0)
    second = jnp.where(lane >= MLA_NOPE + 16, jnp.where(lane < MLA_NOPE + 24, 1, 0), 0)
    return (first + second) > 0


def _mla_kv(ck, wk_ref, wv_ref, gk_ref, cos, sin, km_ref, vm_ref):
    rows = ck.shape[0]
    kk = jnp.dot(ck, wk_ref[...], preferred_element_type=F32)
    is_x1 = _is_x1(rows)
    for h in range(HEADS):
        sl = slice(h * LANE, (h + 1) * LANE)
        kh = _head_rms(kk[:, sl], gk_ref[:, sl], MLA_QK)
        km_ref[:, sl] = _rope(kh, cos, sin, is_x1).astype(BF16)
    vm_ref[...] = jnp.dot(ck, wv_ref[...], preferred_element_type=F32).astype(BF16)


def _in_kernel(x_ref, mod_ref, n1_ref, w_ref, wuq_ref, wk_ref, wv_ref,
               gq_ref, gk_ref, gcq_ref, gckv_ref, gqm_ref, gkm_ref, cos_ref, sin_ref, after_hbm,
               qn_ref, kn_ref, vn_ref, knf_ref, vnf_ref, p_ref,
               qm_ref, km_ref, vm_ref, ckv_ref, kr_ref):
    del after_hbm
    d = x_ref.shape[1]
    rows = x_ref.shape[0]
    mod = mod_ref[0]
    sh1 = mod[:, 0:d]
    sc1 = mod[:, d:2 * d]
    h = _rms(x_ref[...], n1_ref[...]) * (1.0 + sc1) + sh1
    hb = h.astype(BF16)

    def proj(lo, hi):
        return jnp.dot(hb, w_ref[:, lo:hi], preferred_element_type=F32)

    cos = cos_ref[...]
    sin = sin_ref[...]
    is_x1 = _is_x1(rows)

    zq = proj(_CQ, _CQ + HW)
    zk = proj(_CK, _CK + HW)
    for hh in range(HEADS):
        sl = slice(hh * LANE, (hh + 1) * LANE)
        qn_ref[:, sl] = (_head_rms(zq[:, sl], gq_ref[:, sl], NA_DH) * (NA_DH ** -0.5)).astype(BF16)
        kh = _head_rms(zk[:, sl], gk_ref[:, sl], NA_DH)
        knf_ref[:, sl] = kh
        kn_ref[:, sl] = kh.astype(BF16)
    zv = proj(_CV, _CV + HW)
    vnf_ref[...] = zv
    vn_ref[...] = zv.astype(BF16)
    p_ref[...] = proj(_CP, _CP + 256)

    cq = _rms(proj(_CCQ, _CCQ + 256), gcq_ref[...])
    zqm = jnp.dot(cq.astype(BF16), wuq_ref[...], preferred_element_type=F32)
    for hh in range(HEADS):
        sl = slice(hh * LANE, (hh + 1) * LANE)
        qh = _head_rms(zqm[:, sl], gqm_ref[:, sl], MLA_QK)
        qm_ref[:, sl] = (_rope(qh, cos, sin, is_x1) * (MLA_QK ** -0.5)).astype(BF16)

    ckv = _rms(proj(_CCKV, _CCKV + 128), gckv_ref[...])
    kr = proj(_CKR, _CKR + 128)
    ckv_ref[...] = ckv
    kr_ref[...] = kr
    ck = jnp.concatenate([ckv, kr], axis=-1).astype(BF16)
    _mla_kv(ck, wk_ref, wv_ref, gkm_ref, cos, sin, km_ref, vm_ref)


def _in_proj(x, mod_l, row_off, bpm, lw, cos_t, sin_t, rope_blocks, after):
    n, d = x.shape
    nb = n // TB
    tok = lambda w: pl.BlockSpec((TB, w), lambda i: (i, 0))
    rope_spec = pl.BlockSpec((TB, LANE), lambda i: (i % rope_blocks, 0))
    in_specs = [tok(d),
                pl.BlockSpec((1, 1, mod_l.shape[-1]), lambda i: (row_off + i // bpm, 0, 0)),
                _const_spec((1, d)), _const_spec((d, IN_W)), _const_spec((256, HW)),
                _const_spec((256, HW)), _const_spec((256, HW)),
                _const_spec((1, HW)), _const_spec((1, HW)), _const_spec((1, 256)),
                _const_spec((1, 128)), _const_spec((1, HW)), _const_spec((1, HW)),
                rope_spec, rope_spec, pl.BlockSpec(memory_space=pl.ANY)]
    widths = [(HW, BF16), (HW, BF16), (HW, BF16), (HW, F32), (HW, F32), (256, F32),
              (HW, BF16), (HW, BF16), (HW, BF16), (128, F32), (128, F32)]
    return pl.pallas_call(
        _in_kernel,
        grid=(nb,),
        in_specs=in_specs,
        out_specs=[tok(w) for w, _ in widths],
        out_shape=[jax.ShapeDtypeStruct((n, w), dt) for w, dt in widths],
        compiler_params=_params(("arbitrary",)),
        name="in_proj",
    )(x, mod_l, lw["norm1"], lw["w_in"], lw["w_uq"], lw["w_k"], lw["w_v"],
      lw["g_q"], lw["g_k"], lw["g_cq"], lw["g_ckv"], lw["g_qm"], lw["g_km"], cos_t, sin_t, after)


def _cache_kernel(ck_ref, wk_ref, wv_ref, gk_ref, km_ref, vm_ref):
    rows = ck_ref.shape[2]
    cos = jnp.ones((rows, LANE), F32)
    sin = jnp.zeros((rows, LANE), F32)
    _mla_kv(ck_ref[0, 0], wk_ref.at[0], wv_ref.at[0], gk_ref.at[0], cos, sin,
            km_ref.at[0, 0], vm_ref.at[0, 0])


def _cache_kv(ck, w_k, w_v, g_km):
    db, depth, p, _ = ck.shape
    spec = lambda w: pl.BlockSpec((1, 1, p, w), lambda b, l: (b, l, 0, 0))
    wspec = lambda r: pl.BlockSpec((1, r, HW), lambda b, l: (l, 0, 0))
    return pl.pallas_call(
        _cache_kernel,
        grid=(db, depth),
        in_specs=[spec(256), wspec(256), wspec(256), wspec(1)],
        out_specs=[spec(HW), spec(HW)],
        out_shape=[jax.ShapeDtypeStruct((db, depth, p, HW), BF16)] * 2,
        compiler_params=_params(("arbitrary", "arbitrary")),
        name="cache_kv",
    )(ck, w_k, w_v, g_km)


def _softmax_av(s_list, v_list):
    m = s_list[0].max(axis=-1, keepdims=True)
    for s in s_list[1:]:
        m = jnp.maximum(m, s.max(axis=-1, keepdims=True))
    acc = None
    den = None
    for s, v in zip(s_list, v_list):
        p = jnp.exp(s - m)
        l = p.sum(axis=-1, keepdims=True)
        o = jnp.dot(p.astype(BF16), v, preferred_element_type=F32)
        acc = o if acc is None else acc + o
        den = l if den is None else den + l
    return acc / den


def _ctx_attn_kernel(qn, kn, vn, qm, km, vm, on, om):
    for q, k, v, o in ((qn, kn, vn, on), (qm, km, vm, om)):
        for h in range(HEADS):
            sl = slice(h * LANE, (h + 1) * LANE)
            s = _nt_dot(q[:, sl], k[:, sl])
            o[:, sl] = _softmax_av([s], [v[:, sl]]).astype(BF16)


def _ctx_attn(qn, kn, vn, qm, km, vm, seq):
    n = qn.shape[0]
    spec = pl.BlockSpec((seq, HW), lambda i: (i, 0))
    return pl.pallas_call(
        _ctx_attn_kernel,
        grid=(n // seq,),
        in_specs=[spec] * 6,
        out_specs=[spec] * 2,
        out_shape=[jax.ShapeDtypeStruct((n, HW), BF16)] * 2,
        compiler_params=_params(("arbitrary",)),
        name="ctx_attn",
    )(qn, kn, vn, qm, km, vm)


def _lat_mla_kernel(q, k, v, kc, vc, o):
    s1 = _nt_dot(q[...], k[...])
    s2 = _nt_dot(q[...], kc[0])
    o[...] = _softmax_av([s1, s2], [v[...], vc[0]]).astype(BF16)


def _lat_mla(qm, km, vm, kc, vc, db):
    n = qm.shape[0]
    ds = n // db
    nq = ds // TQ
    qspec = pl.BlockSpec((TQ, LANE), lambda b, h, i: (b * nq + i, h))
    kspec = pl.BlockSpec((ds, LANE), lambda b, h, i: (b, h))
    cspec = pl.BlockSpec((1, kc.shape[1], LANE), lambda b, h, i: (b, 0, h))
    return pl.pallas_call(
        _lat_mla_kernel,
        grid=(db, HEADS, nq),
        in_specs=[qspec, kspec, kspec, cspec, cspec],
        out_specs=qspec,
        out_shape=jax.ShapeDtypeStruct((n, HW), BF16),
        compiler_params=_params(("arbitrary",) * 3),
        name="lat_mla",
    )(qm, km, vm, kc, vc)


def _nat_kernel(q, k, v, kc, vc, bias, o, *, rows):
    r = pl.program_id(1)
    rs = jnp.clip(r - WIN_R // 2, 0, rows - WIN_R)
    start = pl.multiple_of(rs * GRID_W, GRID_W)
    band = WIN_R * GRID_W
    for h in range(HEADS):
        sl = slice(h * LANE, (h + 1) * LANE)
        qh = q[:, sl]
        s1 = _nt_dot(qh, k[pl.ds(start, band), sl]) + bias[0, h]
        s2 = _nt_dot(qh, kc[0, :, sl])
        o[:, sl] = _softmax_av([s1, s2], [v[pl.ds(start, band), sl], vc[0, :, sl]]).astype(BF16)


def _nat_attn(qn, kn, vn, kc, vc, bias, db):
    n = qn.shape[0]
    ds = n // db
    rows = ds // GRID_W
    band = WIN_R * GRID_W

    def variant(r):
        return jnp.where(r < WIN_R // 2, r, jnp.where(r > rows - WIN_R // 2, r - (rows - WIN_R), WIN_R // 2))

    qspec = pl.BlockSpec((GRID_W, HW), lambda b, r: (b * rows + r, 0))
    kspec = pl.BlockSpec((ds, HW), lambda b, r: (b, 0))
    cspec = pl.BlockSpec((1, kc.shape[1], HW), lambda b, r: (b, 0, 0))
    bspec = pl.BlockSpec((1, HEADS, GRID_W, band), lambda b, r: (variant(r), 0, 0, 0))
    return pl.pallas_call(
        functools.partial(_nat_kernel, rows=rows),
        grid=(db, rows),
        in_specs=[qspec, kspec, kspec, cspec, cspec, bspec],
        out_specs=qspec,
        out_shape=jax.ShapeDtypeStruct((n, HW), BF16),
        compiler_params=_params(("arbitrary", "arbitrary")),
        name="nat_attn",
    )(qn, kn, vn, kc, vc, bias)


def _split3(x):
    hi = x.astype(BF16)
    r = x - hi.astype(F32)
    mid = r.astype(BF16)
    lo = (r - mid.astype(F32)).astype(BF16)
    return hi, mid, lo


def _pool(p_prev, p_cur, p_next, posb, seq_len):
    rows = p_cur.shape[0]
    halo = p_prev.shape[0]
    ext = rows + 2 * halo
    pext = jnp.concatenate([p_prev, p_cur, p_next], axis=0)
    parts = _split3(pext)
    t = posb + lax.broadcasted_iota(jnp.int32, (rows, ext), 0)
    s = posb - halo + lax.broadcasted_iota(jnp.int32, (rows, ext), 1)
    tcol = posb + lax.broadcasted_iota(jnp.int32, (rows, 1), 0)
    grp = lax.broadcasted_iota(jnp.int32, (rows, 256), 1) // POOL_G
    d = jnp.zeros((rows, 256), F32)
    for gi, w in enumerate(POOL_WINDOWS):
        lo = jnp.maximum(t - w // 2, 0)
        hi = jnp.minimum(t + (w - w // 2), seq_len)
        sel = jnp.where(s >= lo, jnp.where(s < hi, 1.0, 0.0), 0.0).astype(BF16)
        tot = sum(jnp.dot(sel, part, preferred_element_type=F32) for part in parts)
        cnt = (jnp.minimum(tcol + (w - w // 2), seq_len) - jnp.maximum(tcol - w // 2, 0)).astype(F32)
        d = jnp.where(grp == gi, tot / cnt - p_cur, d)
    return d


def _first_max(x, pos, sentinel):
    m = jnp.max(x, axis=0, keepdims=True)
    idx = jnp.min(jnp.where(x == m, pos, sentinel), axis=0, keepdims=True)
    return m, idx


def _topk_head(qh, sk_ref):
    c = qh.shape[0]
    key_pos = lax.broadcasted_iota(jnp.int32, (PEER_NKEYS, c), 0).astype(F32)
    row16 = lax.broadcasted_iota(jnp.int32, (PEER_TOPK, c), 0)
    neg = jnp.float32(-jnp.inf)
    s0 = _nt_dot(sk_ref[0], qh)
    s1 = _nt_dot(sk_ref[1], qh)

    def stage1(a, carry):
        out = []
        for s, sv, si in (carry[0:3], carry[3:6]):
            m, idx = _first_max(s, key_pos, float(PEER_NKEYS))
            out += [jnp.where(key_pos == idx, neg, s),
                    jnp.where(row16 == a, m, sv), jnp.where(row16 == a, idx, si)]
        return tuple(out)

    zf = jnp.zeros((PEER_TOPK, c), F32)
    _, sv0, si0, _, sv1, si1 = lax.fori_loop(0, PEER_TOPK, stage1, (s0, zf, zf, s1, zf, zf))

    sub8 = lax.broadcasted_iota(jnp.int32, (8, c), 0)
    sub8f = sub8.astype(F32)
    cs, ci, cf = [], [], []

    def piece(a_vals, a_ids, a_flat, b_vals, b_ids, b_flat, nb):
        val = a_vals + b_vals
        if nb < 8:
            val = jnp.where(sub8 < nb, val, neg)
        cs.append(val)
        ci.append(a_ids * float(PEER_NKEYS) + b_ids)
        cf.append(jnp.broadcast_to(a_flat * float(PEER_TOPK) + b_flat, (8, c)))

    for a in range(8):
        nb = PEER_TOPK // (a + 1)
        for b0 in range(0, nb, 8):
            piece(sv0[a:a + 1], si0[a:a + 1], float(a), sv1[b0:b0 + 8], si1[b0:b0 + 8],
                  sub8f + float(b0), min(nb - b0, 8))
    piece(sv0[8:16], si0[8:16], sub8f + 8.0, sv1[0:1], si1[0:1], jnp.zeros((8, c), F32), 8)
    npc = len(cs)
    nflat = float(PEER_TOPK * PEER_TOPK)

    def stage2(k, carry):
        vals = list(carry[:npc])
        tv, te = carry[npc], carry[npc + 1]
        m = vals[0]
        for v in vals[1:]:
            m = jnp.maximum(m, v)
        m = jnp.max(m, axis=0, keepdims=True)
        pos = None
        for v, f in zip(vals, cf):
            cand = jnp.where(v == m, f, nflat)
            pos = cand if pos is None else jnp.minimum(pos, cand)
        pos = jnp.min(pos, axis=0, keepdims=True)
        e = None
        for i, f in zip(ci, cf):
            cand = jnp.where(f == pos, i, -1.0)
            e = cand if e is None else jnp.maximum(e, cand)
        e = jnp.max(e, axis=0, keepdims=True)
        vals = [jnp.where(f == pos, neg, v) for v, f in zip(vals, cf)]
        return tuple(vals) + (jnp.where(row16 == k, m, tv), jnp.where(row16 == k, e, te))

    res = lax.fori_loop(0, PEER_TOPK, stage2, tuple(cs) + (zf, zf))
    return res[npc], res[npc + 1]


def _out_kernel(on_ref, om_ref, pc_ref, pp_ref, pn_ref, x_ref, mod_ref,
                won_ref, wop_ref, wom_ref, pw_ref, ps_ref, n2_ref, wq_ref, sk_ref,
                x1_ref, h2_ref, ids_ref, gt_ref, gn_ref, h2c_ref, q_scr, idt_scr, *, bps, seq_len):
    d = x_ref.shape[1]
    rows = x_ref.shape[0]
    i = pl.program_id(0)
    mod = mod_ref[0]
    g1 = mod[:, 2 * d:3 * d]
    sh2 = mod[:, 3 * d:4 * d]
    sc2 = mod[:, 4 * d:5 * d]

    posb = (i % bps) * rows
    dpool = _pool(pp_ref[...], pc_ref[...], pn_ref[...], posb, seq_len)
    ypool = jnp.dot(dpool.astype(BF16), pw_ref[...], preferred_element_type=F32) * ps_ref[...]
    mix = (jnp.dot(on_ref[...], won_ref[...], preferred_element_type=F32)
           + jnp.dot(ypool.astype(BF16), wop_ref[...], preferred_element_type=F32)
           + jnp.dot(om_ref[...], wom_ref[...], preferred_element_type=F32))
    x1 = x_ref[...] + g1 * mix
    x1_ref[...] = x1
    h2 = _rms(x1, n2_ref[...]) * (1.0 + sc2) + sh2
    h2_ref[...] = h2
    for j in range(d // LANE):
        h2c_ref[j] = h2[:, j * LANE:(j + 1) * LANE]

    q = jnp.dot(h2.astype(BF16), wq_ref[...], preferred_element_type=F32)
    for hh in range(PEER_HEADS):
        q_scr[hh] = q[:, hh * LANE:(hh + 1) * LANE].astype(BF16)

    for c0 in range(0, rows, LANE):
        def head(hh, _):
            tv, te = _topk_head(q_scr[hh, c0:c0 + LANE, :], sk_ref)
            ex = jnp.exp(tv - tv[0:1])
            gates = ex / jnp.sum(ex, axis=0, keepdims=True)
            r0 = pl.multiple_of(hh * PEER_TOPK, PEER_TOPK)
            gt_ref[pl.ds(r0, PEER_TOPK), c0:c0 + LANE] = gates
            idt_scr[pl.ds(r0, PEER_TOPK), c0:c0 + LANE] = te
            return 0

        lax.fori_loop(0, PEER_HEADS, head, 0)
    ids_ref[...] = idt_scr[...].T.astype(jnp.int32)
    gn_ref[...] = gt_ref[...].T


def _out_proj(x, on, om, p, mod_l, row_off, bpm, lw, seq_len):
    n, d = x.shape
    nb = n // TB
    bps = seq_len // TB
    halo = 8
    hb = TB // halo
    tok = lambda w: pl.BlockSpec((TB, w), lambda i: (i, 0))
    in_specs = [tok(HW), tok(HW), tok(256),
                pl.BlockSpec((halo, 256), lambda i: (jnp.maximum(i * hb - 1, 0), 0)),
                pl.BlockSpec((halo, 256), lambda i: (jnp.minimum((i + 1) * hb, n // halo - 1), 0)),
                tok(d),
                pl.BlockSpec((1, 1, mod_l.shape[-1]), lambda i: (row_off + i // bpm, 0, 0)),
                _const_spec((HW, d)), _const_spec((256, d)), _const_spec((HW, d)),
                _const_spec((256, 256)), _const_spec((1, 256)), _const_spec((1, d)),
                _const_spec((d, PEER_HEADS * LANE)), _const_spec((2, PEER_NKEYS, LANE))]
    nk = PEER_HEADS * PEER_TOPK
    return pl.pallas_call(
        functools.partial(_out_kernel, bps=bps, seq_len=seq_len),
        grid=(nb,),
        in_specs=in_specs,
        out_specs=[tok(d), tok(d), tok(nk), pl.BlockSpec((nk, TB), lambda i: (0, i)), tok(nk),
                   pl.BlockSpec((d // LANE, TB, LANE), lambda i: (0, i, 0))],
        out_shape=[jax.ShapeDtypeStruct((n, d), F32), jax.ShapeDtypeStruct((n, d), F32),
                   jax.ShapeDtypeStruct((n, nk), jnp.int32), jax.ShapeDtypeStruct((nk, n), F32),
                   jax.ShapeDtypeStruct((n, nk), F32),
                   jax.ShapeDtypeStruct((d // LANE, n, LANE), F32)],
        scratch_shapes=[pltpu.VMEM((PEER_HEADS, TB, LANE), BF16), pltpu.VMEM((nk, TB), F32)],
        compiler_params=_params(("arbitrary",)),
        name="out_proj",
    )(on, om, p, p, p, x, mod_l, lw["w_o_na"], lw["w_o_pool"], lw["w_o_mla"],
      lw["pool_w"], lw["pool_scale"], lw["norm2"], lw["peer_wq"], lw["peer_sk"])


def _gelu_tanh(x):
    return x * (0.5 * (1.0 + jnp.tanh(0.7978845608028654 * (x + 0.044715 * (x * x * x)))))


def _peer_token_mix(chunk, hrow, gcol, ch):
    acc = None
    for s in range(ch):
        us = lax.bitcast_convert_type(chunk(s) & jnp.int32(-65536), F32)
        term = us * hrow[:, s * LANE:(s + 1) * LANE]
        acc = term if acc is None else acc + term
    wgt = gcol * _gelu_tanh(jnp.sum(acc, axis=-1, keepdims=True))
    parts = []
    for s in range(ch):
        vs = lax.bitcast_convert_type(chunk(s) << 16, F32)
        parts.append(jnp.sum(vs * wgt, axis=0, keepdims=True))
    return jnp.concatenate(parts, axis=-1)


def _peer_staged_kernel(rows_ref, gt_ref, h2_ref, x1_ref, mod_ref, o_ref):
    d = x1_ref.shape[1]
    ch = d // LANE
    nk = gt_ref.shape[0]
    g2 = mod_ref[0][:, 5 * d:6 * d]
    tok_lane = lax.broadcasted_iota(jnp.int32, gt_ref.shape, 1)
    base = (pl.program_id(0) % (PEER_TB // PEER_SUB)) * PEER_SUB
    h8 = h2_ref[...]
    ys = []
    for t in range(PEER_SUB):
        chunk = lambda s: rows_ref[pl.ds(t * nk * ch + s, nk, stride=ch), :]
        gcol = jnp.sum(jnp.where(tok_lane == base + t, gt_ref[...], 0.0), axis=-1, keepdims=True)
        ys.append(_peer_token_mix(chunk, h8[t:t + 1, :], gcol, ch))
    o_ref[...] = x1_ref[...] + g2 * jnp.concatenate(ys, axis=0)


def _peer_staged(x1, h2, rows, gt, mod_l, row_off, tpm):
    n, d = x1.shape
    nk = gt.shape[0]
    per = PEER_SUB * nk * (d // LANE)
    sub_per_tb = PEER_TB // PEER_SUB
    tok = pl.BlockSpec((PEER_SUB, d), lambda j: (j, 0))
    return pl.pallas_call(
        _peer_staged_kernel,
        grid=(n // PEER_SUB,),
        in_specs=[pl.BlockSpec((per, LANE), lambda j: (j, 0)),
                  pl.BlockSpec((nk, PEER_TB), lambda j: (0, j // sub_per_tb)),
                  tok, tok,
                  pl.BlockSpec((1, 1, mod_l.shape[-1]), lambda j: (row_off + (j * PEER_SUB) // tpm, 0, 0))],
        out_specs=tok,
        out_shape=jax.ShapeDtypeStruct((n, d), F32),
        compiler_params=_params(("arbitrary",)),
        name="peer_staged",
    )(rows, gt, h2, x1, mod_l)


def _sc_gather(table3, ids_flat):
    m = ids_flat.shape[0]
    _, ch, lane = table3.shape
    info = plsc.get_sparse_core_info()
    nc, nw = info.num_cores, info.num_cores * info.num_subcores
    idx_win = 128
    win = 32
    per_w = m // nw
    assert m % (nw * idx_win) == 0
    mesh = plsc.VectorSubcoreMesh(core_axis_name="core", subcore_axis_name="subcore")

    @functools.partial(
        pl.kernel, mesh=mesh,
        out_type=jax.ShapeDtypeStruct((m, ch, lane), table3.dtype),
        scratch_types=[pltpu.VMEM((idx_win,), jnp.int32),
                       pltpu.VMEM((win, ch, lane), table3.dtype),
                       pltpu.VMEM((win, ch, lane), table3.dtype),
                       pltpu.SemaphoreType.DMA, pltpu.SemaphoreType.DMA])
    def gather(tab_hbm, idx_hbm, out_hbm, idx_v, rows_a, rows_b, sem_a, sem_b):
        wid = lax.axis_index("subcore") * nc + lax.axis_index("core")
        bufs = ((rows_a, sem_a), (rows_b, sem_b))
        nq = idx_win // win

        def fetch(q):
            rows, sem = bufs[q % 2]
            return pltpu.make_async_copy(tab_hbm.at[idx_v.at[pl.ds(q * win, win)]], rows, sem)

        @pl.loop(0, per_w // idx_win)
        def _(g):
            base = pl.multiple_of(wid * per_w + g * idx_win, idx_win)
            pltpu.sync_copy(idx_hbm.at[pl.ds(base, idx_win)], idx_v)
            fetch(0).start()
            for q in range(nq):
                fetch(q).wait()
                if q + 1 < nq:
                    fetch(q + 1).start()
                pltpu.sync_copy(bufs[q % 2][0], out_hbm.at[pl.ds(base + q * win, win)])

    return gather(table3, ids_flat)


def _sc_peer(table3, ids, gates, h2c, n):
    ch, _, lane = h2c.shape
    nk = ids.shape[1]
    info = plsc.get_sparse_core_info()
    nc, nw, nl = info.num_cores, info.num_cores * info.num_subcores, info.num_lanes
    tpw = n // nw
    win = 32
    nq = nk // win
    cpr = lane // nl
    nchunk = ch * cpr
    hc = nchunk // 2
    assert n % nw == 0 and nk % win == 0 and win % nl == 0
    mesh = plsc.VectorSubcoreMesh(core_axis_name="core", subcore_axis_name="subcore")
    hi_mask = jnp.int32(-65536)

    @functools.partial(
        pl.kernel, mesh=mesh,
        out_type=jax.ShapeDtypeStruct((ch, n, lane), F32),
        compiler_params=pltpu.CompilerParams(needs_layout_passes=False),
        scratch_types=[pltpu.VMEM((nk,), jnp.int32), pltpu.VMEM((nk,), F32),
                       pltpu.VMEM((ch, lane), F32), pltpu.VMEM((ch, lane), F32),
                       pltpu.VMEM((win, ch, lane), jnp.int32), pltpu.VMEM((win, ch, lane), jnp.int32),
                       pltpu.VMEM((win * nl,), F32), pltpu.VMEM((win,), F32),
                       pltpu.SemaphoreType.DMA, pltpu.SemaphoreType.DMA, pltpu.SemaphoreType.DMA])
    def peer(tab_hbm, ids_hbm, g_hbm, h2_hbm, y_hbm,
             idx_v, g_v, x_v, y_v, rows_a, rows_b, part_v, w_v, sem_a, sem_b, sem_x):
        wid = lax.axis_index("subcore") * nc + lax.axis_index("core")
        bufs = ((rows_a, sem_a), (rows_b, sem_b))
        lanes = lax.iota(jnp.int32, nl)
        zero = jnp.zeros((nl,), F32)

        def chunk_copies(tok, to_hbm):
            if to_hbm:
                return [pltpu.make_async_copy(y_v.at[j], y_hbm.at[j, tok], sem_x) for j in range(ch)]
            return [pltpu.make_async_copy(h2_hbm.at[j, tok], x_v.at[j], sem_x) for j in range(ch)]

        def fetch(q):
            rows, sem = bufs[q % 2]
            return pltpu.make_async_copy(tab_hbm.at[idx_v.at[pl.ds(q * win, win)]], rows, sem)

        def word(rows, r, cc):
            return rows[r, cc // cpr, pl.ds((cc % cpr) * nl, nl)]

        @pl.loop(0, tpw)
        def _(ti):
            tok = wid * tpw + ti
            loads = chunk_copies(tok, False)
            for cp in loads:
                cp.start()
            pltpu.sync_copy(ids_hbm.at[tok], idx_v)
            pltpu.sync_copy(g_hbm.at[tok], g_v)
            for cp in loads:
                cp.wait()
            for cc in range(nchunk):
                y_v[cc // cpr, pl.ds((cc % cpr) * nl, nl)] = zero
            fetch(0).start()
            for q in range(nq):
                rows = bufs[q % 2][0]
                fetch(q).wait()
                if q + 1 < nq:
                    fetch(q + 1).start()

                for half in range(2):
                    xs = [x_v[(half * hc + c) // cpr, pl.ds(((half * hc + c) % cpr) * nl, nl)]
                          for c in range(hc)]

                    @pl.loop(0, win)
                    def _(r):
                        accs = [None] * 4
                        for c in range(hc):
                            u = lax.bitcast_convert_type(word(rows, r, half * hc + c) & hi_mask, F32)
                            t = u * xs[c]
                            accs[c % 4] = t if accs[c % 4] is None else accs[c % 4] + t
                        acc = (accs[0] + accs[1]) + (accs[2] + accs[3])
                        po = pl.multiple_of(r * nl, nl)
                        if half == 0:
                            part_v[pl.ds(po, nl)] = acc
                        else:
                            part_v[pl.ds(po, nl)] = part_v[pl.ds(po, nl)] + acc

                for grp in range(win // nl):
                    s = zero
                    for rr in range(nl):
                        tot = jnp.sum(part_v[pl.ds((grp * nl + rr) * nl, nl)])
                        s = jnp.where(lanes == rr, tot, s)
                    z = 0.7978845608028654 * (s + 0.044715 * (s * s * s))
                    tanh = 1.0 - 2.0 / (jnp.exp(2.0 * z) + 1.0)
                    gate = g_v[pl.ds(q * win + grp * nl, nl)]
                    w_v[pl.ds(grp * nl, nl)] = gate * (s * (0.5 * (1.0 + tanh)))

                for half in range(2):
                    def body(r, yacc):
                        wr = plsc.load_gather(w_v, [jnp.full((nl,), r, jnp.int32)])
                        out = []
                        for c in range(hc):
                            v = lax.bitcast_convert_type(word(rows, r, half * hc + c) << 16, F32)
                            out.append(yacc[c] + wr * v)
                        return tuple(out)

                    yacc = lax.fori_loop(0, win, body, tuple(zero for _ in range(hc)))
                    for c in range(hc):
                        cc = half * hc + c
                        sl = (cc // cpr, pl.ds((cc % cpr) * nl, nl))
                        y_v[sl] = y_v[sl] + yacc[c]
            stores = chunk_copies(tok, True)
            for cp in stores:
                cp.start()
            for cp in stores:
                cp.wait()

    return peer(table3, ids, gates, h2c)


def _residual_kernel(x1_ref, y_ref, mod_ref, x2_hbm, after_hbm, o_ref):
    del x2_hbm
    del after_hbm
    d = x1_ref.shape[1]
    g2 = mod_ref[0][:, 5 * d:6 * d]
    for j in range(d // LANE):
        sl = slice(j * LANE, (j + 1) * LANE)
        o_ref[:, sl] = x1_ref[:, sl] + g2[:, sl] * y_ref[j]


def _residual(x1, y, mod_l, row, x2, after):
    n, d = x1.shape
    tok = pl.BlockSpec((PEER_TB, d), lambda i: (i, 0))
    any_spec = pl.BlockSpec(memory_space=pl.ANY)
    return pl.pallas_call(
        _residual_kernel,
        grid=(y.shape[1] // PEER_TB,),
        in_specs=[tok, pl.BlockSpec((d // LANE, PEER_TB, LANE), lambda i: (0, i, 0)),
                  pl.BlockSpec((1, 1, mod_l.shape[-1]), lambda i: (row, 0, 0)),
                  any_spec, any_spec],
        out_specs=tok,
        out_shape=jax.ShapeDtypeStruct((n, d), F32),
        input_output_aliases={3: 0},
        compiler_params=_params(("arbitrary",)),
        name="residual",
    )(x1, y, mod_l, x2, after)


def _peer_kernel(ids_hbm, gt_ref, h2_ref, x1_ref, mod_ref, tab_hbm, o_ref,
                 ids_s, buf, sem_i, sem_r, *, first_block):
    d = x1_ref.shape[1]
    ch = d // LANE
    pitch = ch + 1
    nsub = x1_ref.shape[0] // PEER_SUB
    nk = gt_ref.shape[0]
    nids = PEER_SUB * nk
    i = pl.program_id(0) + first_block
    g2 = mod_ref[0][:, 5 * d:6 * d]
    tok_lane = lax.broadcasted_iota(jnp.int32, gt_ref.shape, 1)

    def ids_copy(j, slot):
        start = pl.multiple_of((i * nsub + j) * nids, nids)
        return pltpu.make_async_copy(ids_hbm.at[pl.ds(start, nids)],
                                     ids_s.at[pl.ds(slot * nids, nids)], sem_i.at[slot])

    def row_copy(slot, e, f):
        src = tab_hbm.at[pl.ds(pl.multiple_of(e * ch, ch), ch), :]
        dst = buf.at[slot, pl.ds(f * pitch, ch), :]
        return pltpu.make_async_copy(src, dst, sem_r.at[slot])

    def issue_rows(slot):
        for t in range(PEER_SUB):
            def body(kk, _):
                for r in range(8):
                    f = t * nk + kk * 8 + r
                    row_copy(slot, ids_s[slot * nids + f], f).start(priority=r % 2)
                return 0

            lax.fori_loop(0, nk // 8, body, 0)

    def wait_rows(slot):
        done = buf.at[slot, pl.ds(0, nids * ch), :]
        pltpu.make_async_copy(done, done, sem_r.at[slot]).wait()

    def compute(slot, j):
        base = pl.multiple_of(j * PEER_SUB, PEER_SUB)
        h8 = h2_ref[pl.ds(base, PEER_SUB), :]
        ys = []
        for t in range(PEER_SUB):
            chunk = lambda s: buf[slot, pl.ds(t * nk * pitch + s, nk, stride=pitch), :]
            gcol = jnp.sum(jnp.where(tok_lane == base + t, gt_ref[...], 0.0), axis=-1, keepdims=True)
            ys.append(_peer_token_mix(chunk, h8[t:t + 1, :], gcol, ch))
        y8 = jnp.concatenate(ys, axis=0)
        o_ref[pl.ds(base, PEER_SUB), :] = x1_ref[pl.ds(base, PEER_SUB), :] + g2 * y8

    first = ids_copy(0, 0)
    first.start()
    first.wait()
    issue_rows(0)
    ids_copy(1, 1).start()

    def pair(jj, _):
        j0 = 2 * jj
        ids_copy(j0 + 1, 1).wait()
        issue_rows(1)

        @pl.when(j0 + 2 < nsub)
        def _():
            ids_copy(j0 + 2, 0).start()

        wait_rows(0)
        compute(0, j0)

        @pl.when(j0 + 2 < nsub)
        def _():
            ids_copy(j0 + 2, 0).wait()
            issue_rows(0)

        @pl.when(j0 + 3 < nsub)
        def _():
            ids_copy(j0 + 3, 1).start()

        wait_rows(1)
        compute(1, j0 + 1)
        return 0

    lax.fori_loop(0, nsub // 2, pair, 0)


def _pack_tables(peer_u, peer_v):
    e, d = peer_u.shape
    ub = lax.bitcast_convert_type(peer_u.astype(BF16), jnp.uint16).astype(jnp.uint32)
    vb = lax.bitcast_convert_type(peer_v.astype(BF16), jnp.uint16).astype(jnp.uint32)
    words = lax.bitcast_convert_type((ub << 16) | vb, jnp.int32)
    return words.reshape(e, d // LANE, LANE)


def _peer(x1, h2, ids, gt, mod_l, row, table, tok0):
    n, d = x1.shape
    nk = gt.shape[0]
    b0 = tok0 // PEER_TB
    nb = n // PEER_TB - b0
    tok = pl.BlockSpec((PEER_TB, d), lambda i: (i + b0, 0))
    any_spec = pl.BlockSpec(memory_space=pl.ANY)
    return pl.pallas_call(
        functools.partial(_peer_kernel, first_block=b0),
        grid=(nb,),
        in_specs=[any_spec,
                  pl.BlockSpec((nk, PEER_TB), lambda i: (0, i + b0)),
                  tok, tok,
                  pl.BlockSpec((1, 1, mod_l.shape[-1]), lambda i: (row, 0, 0)),
                  any_spec],
        out_specs=tok,
        out_shape=jax.ShapeDtypeStruct((n, d), F32),
        scratch_shapes=[pltpu.SMEM((2 * PEER_SUB * nk,), jnp.int32),
                        pltpu.VMEM((2, PEER_SUB * nk * (d // LANE + 1), LANE), jnp.int32),
                        pltpu.SemaphoreType.DMA((2,)),
                        pltpu.SemaphoreType.DMA((2,))],
        compiler_params=_params(("arbitrary",)),
        name="peer",
    )(ids.reshape(n * nk), gt, h2, x1, mod_l, table.reshape(-1, LANE))


def _pad_heads(w, width):
    pad = [(0, 0)] * (w.ndim - 1) + [(0, LANE - width)]
    w = jnp.pad(w, pad)
    return w.reshape(w.shape[:-2] + (HW,))


def _head_gain(g, width):
    depth = g.shape[0]
    g = jnp.pad(g, ((0, 0), (0, LANE - width)))
    return jnp.tile(g, (1, HEADS)).reshape(depth, 1, HW)


def _rope_tables(seq):
    t = np.arange(seq)
    half = MLA_ROPE // 2
    inv = ROPE_THETA ** (-np.arange(0, half, 2, dtype=np.float32) / half)
    cos = np.ones((seq, LANE), np.float32)
    sin = np.zeros((seq, LANE), np.float32)
    for off, pos in ((MLA_NOPE, t // GRID_W), (MLA_NOPE + half, t % GRID_W)):
        ang = pos.astype(np.float32)[:, None] * inv[None, :]
        q = half // 2
        cos[:, off:off + q] = np.cos(ang)
        cos[:, off + q:off + half] = np.cos(ang)
        sin[:, off:off + q] = -np.sin(ang)
        sin[:, off + q:off + half] = np.sin(ang)
    return jnp.asarray(cos), jnp.asarray(sin)


def _nat_bias(rel_bias):
    v = np.arange(WIN_R)[:, None]
    j = np.arange(WIN_R)[None, :]
    dr = j - v + WIN_R - 1
    cq = np.arange(GRID_W)[:, None]
    kc = np.arange(GRID_W)[None, :]
    cstart = np.clip(cq - WIN_C // 2, 0, GRID_W - WIN_C)
    ok = (kc >= cstart) & (kc < cstart + WIN_C)
    dc = np.clip(kc - cq + WIN_C - 1, 0, 2 * WIN_C - 2)
    b = rel_bias[:, :, dr]
    b = b[..., dc]
    b = jnp.where(jnp.asarray(ok)[None, None, None, None], b, NEG_INF)
    b = jnp.transpose(b, (0, 2, 1, 4, 3, 5))
    return b.reshape(b.shape[0], WIN_R, HEADS, GRID_W, WIN_R * GRID_W)


def _layer_weights(w_in, na_q_norm, na_k_norm, mla_cq_norm, mla_ckv_norm, mla_w_uq, mla_w_ukv,
                   mla_q_norm, mla_k_norm, w_out, pool_w, pool_scale, norm1, norm2,
                   peer_wq, peer_subkeys):
    depth, d, _ = w_in.shape
    na_w = HEADS * NA_DH
    segs = np.cumsum([0, na_w, na_w, na_w, 256, 256, 128, MLA_ROPE])
    part = lambda i: w_in[:, :, segs[i]:segs[i + 1]]
    heads = lambda w: _pad_heads(w.reshape(depth, d, HEADS, NA_DH), NA_DH)
    w_in_p = jnp.concatenate(
        [heads(part(0)), heads(part(1)), heads(part(2)), part(3), part(4), part(5),
         jnp.pad(part(6), ((0, 0), (0, 0), (0, LANE - MLA_ROPE)))], axis=-1).astype(BF16)

    w_uq = _pad_heads(mla_w_uq, MLA_QK).astype(BF16)
    k_nope = _pad_heads(mla_w_ukv[..., :MLA_NOPE], MLA_NOPE)
    eye = np.zeros((MLA_ROPE, HEADS, LANE), np.float32)
    for h in range(HEADS):
        eye[np.arange(MLA_ROPE), h, MLA_NOPE + np.arange(MLA_ROPE)] = 1.0
    eye = jnp.broadcast_to(jnp.asarray(eye.reshape(MLA_ROPE, HW)), (depth, MLA_ROPE, HW))
    zer = jnp.zeros((depth, 256 - 128 - MLA_ROPE, HW), F32)
    w_k = jnp.concatenate([k_nope, eye, zer], axis=1).astype(BF16)
    w_v = jnp.concatenate([_pad_heads(mla_w_ukv[..., MLA_NOPE:], MLA_V),
                           jnp.zeros((depth, 128, HW), F32)], axis=1).astype(BF16)

    mix_w = HEADS * NA_DH
    w_o_na = jnp.pad(w_out[:, :mix_w].reshape(depth, HEADS, NA_DH, d),
                     ((0, 0), (0, 0), (0, LANE - NA_DH), (0, 0))).reshape(depth, HW, d).astype(BF16)
    w_o_pool = w_out[:, mix_w:mix_w + 256].astype(BF16)
    w_o_mla = jnp.pad(w_out[:, mix_w + 256:].reshape(depth, HEADS, MLA_V, d),
                      ((0, 0), (0, 0), (0, LANE - MLA_V), (0, 0))).reshape(depth, HW, d).astype(BF16)
    ng = len(POOL_WINDOWS)
    pw = jnp.zeros((depth, ng * POOL_G, ng * POOL_G), F32)
    for g in range(ng):
        pw = pw.at[:, g * POOL_G:(g + 1) * POOL_G, g * POOL_G:(g + 1) * POOL_G].set(pool_w[:, g])

    half = peer_subkeys.shape[-1]
    sk = jnp.stack([jnp.pad(peer_subkeys[:, 0], ((0, 0), (0, 0), (0, LANE - half))),
                    jnp.pad(peer_subkeys[:, 1], ((0, 0), (0, 0), (LANE - half, 0)))], axis=1).astype(BF16)

    return dict(
        w_in=w_in_p, w_uq=w_uq, w_k=w_k, w_v=w_v,
        g_q=_head_gain(na_q_norm, NA_DH), g_k=_head_gain(na_k_norm, NA_DH),
        g_cq=mla_cq_norm[:, None, :], g_ckv=mla_ckv_norm[:, None, :],
        g_qm=_head_gain(mla_q_norm, MLA_QK), g_km=_head_gain(mla_k_norm, MLA_QK),
        w_o_na=w_o_na, w_o_pool=w_o_pool, w_o_mla=w_o_mla,
        pool_w=pw.astype(BF16), pool_scale=pool_scale[:, None, :],
        norm1=norm1[:, None, :], norm2=norm2[:, None, :],
        peer_wq=peer_wq.astype(BF16), peer_sk=sk)


def kernel(x_prompt, x_sample, c, cache_nat_k, cache_nat_v, cache_mla_ckv, cache_mla_krope, c_ctx, w_mod, b_mod, norm1, norm2, w_in, na_q_norm, na_k_norm, na_rel_bias, pool_w, pool_scale, mla_cq_norm, mla_ckv_norm, mla_w_uq, mla_w_ukv, mla_q_norm, mla_k_norm, w_out, peer_wq, peer_subkeys, peer_u, peer_v):
    batch, seq, d = x_prompt.shape
    db, ds, _ = x_sample.shape
    depth = w_mod.shape[0]
    past = cache_nat_k.shape[2]
    assert seq == TB and ds % TB == 0 and ds % (GRID_W * WIN_R) == 0 and db + 1 <= 8

    cond8 = jnp.concatenate([c_ctx[None, :], c, jnp.zeros((8 - 1 - db, d), F32)], axis=0)
    mod = _modulation(cond8, w_mod, b_mod).reshape(depth, 8, 1, 6 * d)

    lw_all = _layer_weights(w_in, na_q_norm, na_k_norm, mla_cq_norm, mla_ckv_norm, mla_w_uq,
                            mla_w_ukv, mla_q_norm, mla_k_norm, w_out, pool_w, pool_scale,
                            norm1, norm2, peer_wq, peer_subkeys)
    bias_all = _nat_bias(na_rel_bias)
    tables = [_pack_tables(peer_u[l], peer_v[l]) for l in range(depth)]
    cos_lat, sin_lat = _rope_tables(ds)
    cos_ctx = jnp.ones((TB, LANE), F32)
    sin_ctx = jnp.zeros((TB, LANE), F32)

    ck = jnp.concatenate([cache_mla_ckv, cache_mla_krope,
                          jnp.zeros(cache_mla_ckv.shape[:-1] + (256 - 128 - MLA_ROPE,), F32)],
                         axis=-1).astype(BF16)
    kc_mla, vc_mla = _cache_kv(ck, lw_all["w_k"], lw_all["w_v"], lw_all["g_km"])
    kc_na = _pad_heads(cache_nat_k, NA_DH).astype(BF16)
    vc_na = _pad_heads(cache_nat_v, NA_DH).astype(BF16)

    xs = [x_prompt.reshape(batch * seq, d)] + [x_sample[b] for b in range(db)]
    one_row = max(batch * seq, ds) + 1
    lat_bpm = ds // TB
    ks, vs, ckvs, krs = [], [], [], []
    pending = None
    after = xs[0]

    def join(item, follow):
        si, x1, y_sc, x2, mod_l = item
        xs[si] = _residual(x1, y_sc, mod_l, si, x2, follow)
        return xs[si]

    for l in range(depth):
        lw = {k: v[l] for k, v in lw_all.items()}
        mod_l = mod[l]
        for si in range(db + 1):
            x = xs[si]
            if si == 0:
                (qn, kn, vn, knf, vnf, p, qm, km, vm, ckv, kr) = _in_proj(
                    x, mod_l, 0, one_row, lw, cos_ctx, sin_ctx, 1, after)
                on, om = _ctx_attn(qn, kn, vn, qm, km, vm, seq)
                x1, h2, ids, gt, gn, h2c = _out_proj(x, on, om, p, mod_l, 0, one_row, lw, seq)
                ks.append(knf.reshape(batch, seq, HEADS, LANE)[..., :NA_DH])
                vs.append(vnf.reshape(batch, seq, HEADS, LANE)[..., :NA_DH])
                ckvs.append(ckv.reshape(batch, seq, 128))
                krs.append(kr.reshape(batch, seq, LANE)[..., :MLA_ROPE])
            else:
                b = si - 1
                (qn, kn, vn, _, _, p, qm, km, vm, _, _) = _in_proj(
                    x, mod_l, si, one_row, lw, cos_lat, sin_lat, lat_bpm, after)
                on = _nat_attn(qn, kn, vn, kc_na[b:b + 1, l], vc_na[b:b + 1, l], bias_all[l], 1)
                om = _lat_mla(qm, km, vm, kc_mla[b:b + 1, l], vc_mla[b:b + 1, l], 1)
                x1, h2, ids, gt, gn, h2c = _out_proj(x, on, om, p, mod_l, si, one_row, lw, ds)
            n_sc = x.shape[0] * SC_SHARE[0] // SC_SHARE[1] // PEER_TB * PEER_TB
            y_sc = _sc_peer(tables[l], ids, gn, h2c, n_sc)
            x2 = _peer(x1, h2, ids, gt, mod_l, si, tables[l], n_sc)
            after = x2 if pending is None else join(pending, x2)
            pending = (si, x1, y_sc, x2, mod_l)
    join(pending, pending[1])

    return (xs[0].reshape(batch, seq, d), jnp.stack(xs[1:], axis=0),
            jnp.stack(ks, axis=1), jnp.stack(vs, axis=1),
            jnp.stack(ckvs, axis=1), jnp.stack(krs, axis=1))
```

```python
import functools

import numpy as np
import jax
import jax.numpy as jnp
from jax import lax
from jax.experimental import pallas as pl
from jax.experimental.pallas import tpu as pltpu
from jax.experimental.pallas import tpu_sc as plsc

F32 = jnp.float32
BF16 = jnp.bfloat16

EPS = 1e-6
ROPE_THETA = 10000.0
NEG_INF = -1e30
GRID_W = 64
HEADS = 6
NA_DH = 64
WIN_R = 8
WIN_C = 16
POOL_WINDOWS = (2, 4, 8, 16)
POOL_G = 64
MLA_NOPE = 64
MLA_ROPE = 32
MLA_QK = MLA_NOPE + MLA_ROPE
MLA_V = 64
PEER_HEADS = 8
PEER_NKEYS = 128
PEER_TOPK = 16
LANE = 128
HW = HEADS * LANE
TB = 256
TQ = 256
PEER_TB = 128
PEER_SUB = 8
VMEM_LIMIT = 56 * 1024 * 1024
SC_SHARE = (12, 16)

_CQ, _CK, _CV = 0, HW, 2 * HW
_CP = 3 * HW
_CCQ = _CP + 256
_CCKV = _CCQ + 256
_CKR = _CCKV + 128
IN_W = _CKR + 128


def _params(sem, vmem=VMEM_LIMIT):
    return pltpu.CompilerParams(dimension_semantics=sem, vmem_limit_bytes=vmem)


def _const_spec(shape):
    n = len(shape)
    return pl.BlockSpec(shape, lambda *_: (0,) * n)


def _nt_dot(a, b):
    return lax.dot_general(a, b, (((1,), (1,)), ((), ())), preferred_element_type=F32)


def _mod_kernel(c_ref, w_ref, b_ref, o_ref):
    c = c_ref[...]
    s = c / (1.0 + jnp.exp(-c))
    o_ref[0] = jnp.dot(s, w_ref[0], preferred_element_type=F32,
                       precision=lax.Precision.HIGHEST) + b_ref[0]


def _modulation(cond8, w_mod, b_mod):
    depth, d, n6 = w_mod.shape
    tn = n6 // 4
    return pl.pallas_call(
        _mod_kernel,
        grid=(depth, n6 // tn),
        in_specs=[_const_spec((8, d)),
                  pl.BlockSpec((1, d, tn), lambda l, j: (l, 0, j)),
                  pl.BlockSpec((1, 1, tn), lambda l, j: (l, 0, j))],
        out_specs=pl.BlockSpec((1, 8, tn), lambda l, j: (l, 0, j)),
        out_shape=jax.ShapeDtypeStruct((depth, 8, n6), F32),
        compiler_params=_params(("arbitrary", "arbitrary")),
        name="modulation",
    )(cond8, w_mod, b_mod.reshape(depth, 1, n6))


def _rms(z, gain):
    return z * lax.rsqrt(jnp.mean(z * z, axis=-1, keepdims=True) + EPS) * gain


def _head_rms(zh, gain_h, n_real):
    ms = jnp.sum(zh * zh, axis=-1, keepdims=True) * (1.0 / n_real)
    return zh * lax.rsqrt(ms + EPS) * gain_h


def _rope(zh, cos, sin, is_x1):
    rot = jnp.where(is_x1, pltpu.roll(zh, LANE - 8, 1), pltpu.roll(zh, 8, 1))
    return zh * cos + rot * sin


def _is_x1(rows):
    lane = lax.broadcasted_iota(jnp.int32, (rows, LANE), 1)
    first = jnp.where(lane >= MLA_NOPE, jnp.where(lane < MLA_NOPE + 8, 1, 0), 0)
    second = jnp.where(lane >= MLA_NOPE + 16, jnp.where(lane < MLA_NOPE + 24, 1, 0), 0)
    return (first + second) > 0


def _mla_kv(ck, wk_ref, wv_ref, gk_ref, cos, sin, km_ref, vm_ref):
    rows = ck.shape[0]
    kk = jnp.dot(ck, wk_ref[...], preferred_element_type=F32)
    is_x1 = _is_x1(rows)
    for h in range(HEADS):
        sl = slice(h * LANE, (h + 1) * LANE)
        kh = _head_rms(kk[:, sl], gk_ref[:, sl], MLA_QK)
        km_ref[:, sl] = _rope(kh, cos, sin, is_x1).astype(BF16)
    vm_ref[...] = jnp.dot(ck, wv_ref[...], preferred_element_type=F32).astype(BF16)


def _in_kernel(x_ref, mod_ref, n1_ref, w_ref, wuq_ref, wk_ref, wv_ref,
               gq_ref, gk_ref, gcq_ref, gckv_ref, gqm_ref, gkm_ref, cos_ref, sin_ref, after_hbm,
               qn_ref, kn_ref, vn_ref, knf_ref, vnf_ref, p_ref,
               qm_ref, km_ref, vm_ref, ckv_ref, kr_ref):
    del after_hbm
    d = x_ref.shape[1]
    rows = x_ref.shape[0]
    mod = mod_ref[0]
    sh1 = mod[:, 0:d]
    sc1 = mod[:, d:2 * d]
    h = _rms(x_ref[...], n1_ref[...]) * (1.0 + sc1) + sh1
    hb = h.astype(BF16)

    def proj(lo, hi):
        return jnp.dot(hb, w_ref[:, lo:hi], preferred_element_type=F32)

    cos = cos_ref[...]
    sin = sin_ref[...]
    is_x1 = _is_x1(rows)

    zq = proj(_CQ, _CQ + HW)
    zk = proj(_CK, _CK + HW)
    for hh in range(HEADS):
        sl = slice(hh * LANE, (hh + 1) * LANE)
        qn_ref[:, sl] = (_head_rms(zq[:, sl], gq_ref[:, sl], NA_DH) * (NA_DH ** -0.5)).astype(BF16)
        kh = _head_rms(zk[:, sl], gk_ref[:, sl], NA_DH)
        knf_ref[:, sl] = kh
        kn_ref[:, sl] = kh.astype(BF16)
    zv = proj(_CV, _CV + HW)
    vnf_ref[...] = zv
    vn_ref[...] = zv.astype(BF16)
    p_ref[...] = proj(_CP, _CP + 256)

    cq = _rms(proj(_CCQ, _CCQ + 256), gcq_ref[...])
    zqm = jnp.dot(cq.astype(BF16), wuq_ref[...], preferred_element_type=F32)
    for hh in range(HEADS):
        sl = slice(hh * LANE, (hh + 1) * LANE)
        qh = _head_rms(zqm[:, sl], gqm_ref[:, sl], MLA_QK)
        qm_ref[:, sl] = (_rope(qh, cos, sin, is_x1) * (MLA_QK ** -0.5)).astype(BF16)

    ckv = _rms(proj(_CCKV, _CCKV + 128), gckv_ref[...])
    kr = proj(_CKR, _CKR + 128)
    ckv_ref[...] = ckv
    kr_ref[...] = kr
    ck = jnp.concatenate([ckv, kr], axis=-1).astype(BF16)
    _mla_kv(ck, wk_ref, wv_ref, gkm_ref, cos, sin, km_ref, vm_ref)


def _in_proj(x, mod_l, row_off, bpm, lw, cos_t, sin_t, rope_blocks, after):
    n, d = x.shape
    nb = n // TB
    tok = lambda w: pl.BlockSpec((TB, w), lambda i: (i, 0))
    rope_spec = pl.BlockSpec((TB, LANE), lambda i: (i % rope_blocks, 0))
    in_specs = [tok(d),
                pl.BlockSpec((1, 1, mod_l.shape[-1]), lambda i: (row_off + i // bpm, 0, 0)),
                _const_spec((1, d)), _const_spec((d, IN_W)), _const_spec((256, HW)),
                _const_spec((256, HW)), _const_spec((256, HW)),
                _const_spec((1, HW)), _const_spec((1, HW)), _const_spec((1, 256)),
                _const_spec((1, 128)), _const_spec((1, HW)), _const_spec((1, HW)),
                rope_spec, rope_spec, pl.BlockSpec(memory_space=pl.ANY)]
    widths = [(HW, BF16), (HW, BF16), (HW, BF16), (HW, F32), (HW, F32), (256, F32),
              (HW, BF16), (HW, BF16), (HW, BF16), (128, F32), (128, F32)]
    return pl.pallas_call(
        _in_kernel,
        grid=(nb,),
        in_specs=in_specs,
        out_specs=[tok(w) for w, _ in widths],
        out_shape=[jax.ShapeDtypeStruct((n, w), dt) for w, dt in widths],
        compiler_params=_params(("arbitrary",)),
        name="in_proj",
    )(x, mod_l, lw["norm1"], lw["w_in"], lw["w_uq"], lw["w_k"], lw["w_v"],
      lw["g_q"], lw["g_k"], lw["g_cq"], lw["g_ckv"], lw["g_qm"], lw["g_km"], cos_t, sin_t, after)


def _cache_kernel(ck_ref, wk_ref, wv_ref, gk_ref, km_ref, vm_ref):
    rows = ck_ref.shape[2]
    cos = jnp.ones((rows, LANE), F32)
    sin = jnp.zeros((rows, LANE), F32)
    _mla_kv(ck_ref[0, 0], wk_ref.at[0], wv_ref.at[0], gk_ref.at[0], cos, sin,
            km_ref.at[0, 0], vm_ref.at[0, 0])


def _cache_kv(ck, w_k, w_v, g_km):
    db, depth, p, _ = ck.shape
    spec = lambda w: pl.BlockSpec((1, 1, p, w), lambda b, l: (b, l, 0, 0))
    wspec = lambda r: pl.BlockSpec((1, r, HW), lambda b, l: (l, 0, 0))
    return pl.pallas_call(
        _cache_kernel,
        grid=(db, depth),
        in_specs=[spec(256), wspec(256), wspec(256), wspec(1)],
        out_specs=[spec(HW), spec(HW)],
        out_shape=[jax.ShapeDtypeStruct((db, depth, p, HW), BF16)] * 2,
        compiler_params=_params(("arbitrary", "arbitrary")),
        name="cache_kv",
    )(ck, w_k, w_v, g_km)


def _softmax_av(s_list, v_list):
    m = s_list[0].max(axis=-1, keepdims=True)
    for s in s_list[1:]:
        m = jnp.maximum(m, s.max(axis=-1, keepdims=True))
    acc = None
    den = None
    for s, v in zip(s_list, v_list):
        p = jnp.exp(s - m)
        l = p.sum(axis=-1, keepdims=True)
        o = jnp.dot(p.astype(BF16), v, preferred_element_type=F32)
        acc = o if acc is None else acc + o
        den = l if den is None else den + l
    return acc / den


def _ctx_attn_kernel(qn, kn, vn, qm, km, vm, on, om):
    for q, k, v, o in ((qn, kn, vn, on), (qm, km, vm, om)):
        for h in range(HEADS):
            sl = slice(h * LANE, (h + 1) * LANE)
            s = _nt_dot(q[:, sl], k[:, sl])
            o[:, sl] = _softmax_av([s], [v[:, sl]]).astype(BF16)


def _ctx_attn(qn, kn, vn, qm, km, vm, seq):
    n = qn.shape[0]
    spec = pl.BlockSpec((seq, HW), lambda i: (i, 0))
    return pl.pallas_call(
        _ctx_attn_kernel,
        grid=(n // seq,),
        in_specs=[spec] * 6,
        out_specs=[spec] * 2,
        out_shape=[jax.ShapeDtypeStruct((n, HW), BF16)] * 2,
        compiler_params=_params(("arbitrary",)),
        name="ctx_attn",
    )(qn, kn, vn, qm, km, vm)


def _lat_mla_kernel(q, k, v, kc, vc, o):
    s1 = _nt_dot(q[...], k[...])
    s2 = _nt_dot(q[...], kc[0])
    o[...] = _softmax_av([s1, s2], [v[...], vc[0]]).astype(BF16)


def _lat_mla(qm, km, vm, kc, vc, db):
    n = qm.shape[0]
    ds = n // db
    nq = ds // TQ
    qspec = pl.BlockSpec((TQ, LANE), lambda b, h, i: (b * nq + i, h))
    kspec = pl.BlockSpec((ds, LANE), lambda b, h, i: (b, h))
    cspec = pl.BlockSpec((1, kc.shape[1], LANE), lambda b, h, i: (b, 0, h))
    return pl.pallas_call(
        _lat_mla_kernel,
        grid=(db, HEADS, nq),
        in_specs=[qspec, kspec, kspec, cspec, cspec],
        out_specs=qspec,
        out_shape=jax.ShapeDtypeStruct((n, HW), BF16),
        compiler_params=_params(("arbitrary",) * 3),
        name="lat_mla",
    )(qm, km, vm, kc, vc)


def _nat_kernel(q, k, v, kc, vc, bias, o, *, rows):
    r = pl.program_id(1)
    rs = jnp.clip(r - WIN_R // 2, 0, rows - WIN_R)
    start = pl.multiple_of(rs * GRID_W, GRID_W)
    band = WIN_R * GRID_W
    for h in range(HEADS):
        sl = slice(h * LANE, (h + 1) * LANE)
        qh = q[:, sl]
        s1 = _nt_dot(qh, k[pl.ds(start, band), sl]) + bias[0, h]
        s2 = _nt_dot(qh, kc[0, :, sl])
        o[:, sl] = _softmax_av([s1, s2], [v[pl.ds(start, band), sl], vc[0, :, sl]]).astype(BF16)


def _nat_attn(qn, kn, vn, kc, vc, bias, db):
    n = qn.shape[0]
    ds = n // db
    rows = ds // GRID_W
    band = WIN_R * GRID_W

    def variant(r):
        return jnp.where(r < WIN_R // 2, r, jnp.where(r > rows - WIN_R // 2, r - (rows - WIN_R), WIN_R // 2))

    qspec = pl.BlockSpec((GRID_W, HW), lambda b, r: (b * rows + r, 0))
    kspec = pl.BlockSpec((ds, HW), lambda b, r: (b, 0))
    cspec = pl.BlockSpec((1, kc.shape[1], HW), lambda b, r: (b, 0, 0))
    bspec = pl.BlockSpec((1, HEADS, GRID_W, band), lambda b, r: (variant(r), 0, 0, 0))
    return pl.pallas_call(
        functools.partial(_nat_kernel, rows=rows),
        grid=(db, rows),
        in_specs=[qspec, kspec, kspec, cspec, cspec, bspec],
        out_specs=qspec,
        out_shape=jax.ShapeDtypeStruct((n, HW), BF16),
        compiler_params=_params(("arbitrary", "arbitrary")),
        name="nat_attn",
    )(qn, kn, vn, kc, vc, bias)


def _split3(x):
    hi = x.astype(BF16)
    r = x - hi.astype(F32)
    mid = r.astype(BF16)
    lo = (r - mid.astype(F32)).astype(BF16)
    return hi, mid, lo


def _pool(p_prev, p_cur, p_next, posb, seq_len):
    rows = p_cur.shape[0]
    halo = p_prev.shape[0]
    ext = rows + 2 * halo
    pext = jnp.concatenate([p_prev, p_cur, p_next], axis=0)
    parts = _split3(pext)
    t = posb + lax.broadcasted_iota(jnp.int32, (rows, ext), 0)
    s = posb - halo + lax.broadcasted_iota(jnp.int32, (rows, ext), 1)
    tcol = posb + lax.broadcasted_iota(jnp.int32, (rows, 1), 0)
    grp = lax.broadcasted_iota(jnp.int32, (rows, 256), 1) // POOL_G
    d = jnp.zeros((rows, 256), F32)
    for gi, w in enumerate(POOL_WINDOWS):
        lo = jnp.maximum(t - w // 2, 0)
        hi = jnp.minimum(t + (w - w // 2), seq_len)
        sel = jnp.where(s >= lo, jnp.where(s < hi, 1.0, 0.0), 0.0).astype(BF16)
        tot = sum(jnp.dot(sel, part, preferred_element_type=F32) for part in parts)
        cnt = (jnp.minimum(tcol + (w - w // 2), seq_len) - jnp.maximum(tcol - w // 2, 0)).astype(F32)
        d = jnp.where(grp == gi, tot / cnt - p_cur, d)
    return d


def _first_max(x, pos, sentinel):
    m = jnp.max(x, axis=0, keepdims=True)
    idx = jnp.min(jnp.where(x == m, pos, sentinel), axis=0, keepdims=True)
    return m, idx


def _topk_stage1(qh, sk_ref):
    c = qh.shape[0]
    key_pos = lax.broadcasted_iota(jnp.int32, (PEER_NKEYS, c), 0).astype(F32)
    row16 = lax.broadcasted_iota(jnp.int32, (PEER_TOPK, c), 0)
    neg = jnp.float32(-jnp.inf)
    s0 = _nt_dot(sk_ref[0], qh)
    s1 = _nt_dot(sk_ref[1], qh)

    def stage1(a, carry):
        out = []
        for s, sv, si in (carry[0:3], carry[3:6]):
            m, idx = _first_max(s, key_pos, float(PEER_NKEYS))
            out += [jnp.where(key_pos == idx, neg, s),
                    jnp.where(row16 == a, m, sv), jnp.where(row16 == a, idx, si)]
        return tuple(out)

    zf = jnp.zeros((PEER_TOPK, c), F32)
    _, sv0, si0, _, sv1, si1 = lax.fori_loop(0, PEER_TOPK, stage1, (s0, zf, zf, s1, zf, zf))
    return sv0, si0, sv1, si1


def _topk_pieces(sv0, sv1):
    c = sv0.shape[1]
    neg = jnp.float32(-jnp.inf)
    sub8 = lax.broadcasted_iota(jnp.int32, (8, c), 0)
    sub8f = sub8.astype(F32)
    cs, cf = [], []
    for a in range(8):
        nb = PEER_TOPK // (a + 1)
        for b0 in range(0, nb, 8):
            val = sv0[a:a + 1] + sv1[b0:b0 + 8]
            if nb - b0 < 8:
                val = jnp.where(sub8 < nb - b0, val, neg)
            cs.append(val)
            cf.append(sub8f + float(a * PEER_TOPK + b0))
    cs.append(sv0[8:16] + sv1[0:1])
    cf.append((sub8f + 8.0) * float(PEER_TOPK))
    return cs, cf


def _topk_stage2(chains, cf):
    npc = len(cf)
    c = cf[0].shape[1]
    row16 = lax.broadcasted_iota(jnp.int32, (PEER_TOPK, c), 0)
    neg = jnp.float32(-jnp.inf)
    nflat = float(PEER_TOPK * PEER_TOPK)
    zf = jnp.zeros((PEER_TOPK, c), F32)

    def step(k, carry):
        out = []
        for ch in range(len(chains)):
            vals = carry[ch * (npc + 2):ch * (npc + 2) + npc]
            tv, tp = carry[ch * (npc + 2) + npc], carry[ch * (npc + 2) + npc + 1]
            m = vals[0]
            for v in vals[1:]:
                m = jnp.maximum(m, v)
            m = jnp.max(m, axis=0, keepdims=True)
            pos = None
            for v, f in zip(vals, cf):
                cand = jnp.where(v == m, f, nflat)
                pos = cand if pos is None else jnp.minimum(pos, cand)
            pos = jnp.min(pos, axis=0, keepdims=True)
            out += [jnp.where(f == pos, neg, v) for v, f in zip(vals, cf)]
            out += [jnp.where(row16 == k, m, tv), jnp.where(row16 == k, pos, tp)]
        return tuple(out)

    init = []
    for cs in chains:
        init += list(cs) + [zf, zf]
    res = lax.fori_loop(0, PEER_TOPK, step, tuple(init))
    return [(res[ch * (npc + 2) + npc], res[ch * (npc + 2) + npc + 1]) for ch in range(len(chains))]


def _topk_ids(tp, si0, si1):
    a = jnp.floor(tp * (1.0 / PEER_TOPK))
    b = tp - a * float(PEER_TOPK)
    ea = jnp.zeros_like(tp)
    eb = jnp.zeros_like(tp)
    for j in range(PEER_TOPK):
        ea = jnp.where(a == float(j), si0[j:j + 1], ea)
        eb = jnp.where(b == float(j), si1[j:j + 1], eb)
    return ea * float(PEER_NKEYS) + eb


def _out_kernel(on_ref, om_ref, pc_ref, pp_ref, pn_ref, x_ref, mod_ref,
                won_ref, wop_ref, wom_ref, pw_ref, ps_ref, n2_ref, wq_ref, sk_ref,
                x1_ref, h2_ref, ids_ref, gt_ref, gn_ref, h2c_ref, q_scr, idt_scr, *, bps, seq_len):
    d = x_ref.shape[1]
    rows = x_ref.shape[0]
    i = pl.program_id(0)
    mod = mod_ref[0]
    g1 = mod[:, 2 * d:3 * d]
    sh2 = mod[:, 3 * d:4 * d]
    sc2 = mod[:, 4 * d:5 * d]

    posb = (i % bps) * rows
    dpool = _pool(pp_ref[...], pc_ref[...], pn_ref[...], posb, seq_len)
    ypool = jnp.dot(dpool.astype(BF16), pw_ref[...], preferred_element_type=F32) * ps_ref[...]
    mix = (jnp.dot(on_ref[...], won_ref[...], preferred_element_type=F32)
           + jnp.dot(ypool.astype(BF16), wop_ref[...], preferred_element_type=F32)
           + jnp.dot(om_ref[...], wom_ref[...], preferred_element_type=F32))
    x1 = x_ref[...] + g1 * mix
    x1_ref[...] = x1
    h2 = _rms(x1, n2_ref[...]) * (1.0 + sc2) + sh2
    h2_ref[...] = h2
    for j in range(d // LANE):
        h2c_ref[j] = h2[:, j * LANE:(j + 1) * LANE]

    q = jnp.dot(h2.astype(BF16), wq_ref[...], preferred_element_type=F32)
    for hh in range(PEER_HEADS):
        q_scr[hh] = q[:, hh * LANE:(hh + 1) * LANE].astype(BF16)

    chunks = range(0, rows, LANE)

    def head(hh, _):
        sorted_keys = [_topk_stage1(q_scr[hh, c0:c0 + LANE, :], sk_ref) for c0 in chunks]
        pieces = [_topk_pieces(sv0, sv1) for sv0, _, sv1, _ in sorted_keys]
        picked = _topk_stage2([cs for cs, _ in pieces], pieces[0][1])
        r0 = pl.multiple_of(hh * PEER_TOPK, PEER_TOPK)
        for c0, (tv, tp), (_, si0, _, si1) in zip(chunks, picked, sorted_keys):
            ex = jnp.exp(tv - tv[0:1])
            gt_ref[pl.ds(r0, PEER_TOPK), c0:c0 + LANE] = ex / jnp.sum(ex, axis=0, keepdims=True)
            idt_scr[pl.ds(r0, PEER_TOPK), c0:c0 + LANE] = _topk_ids(tp, si0, si1)
        return 0

    lax.fori_loop(0, PEER_HEADS, head, 0)
    ids_ref[...] = idt_scr[...].T.astype(jnp.int32)
    gn_ref[...] = gt_ref[...].T


def _out_proj(x, on, om, p, mod_l, row_off, bpm, lw, seq_len):
    n, d = x.shape
    nb = n // TB
    bps = seq_len // TB
    halo = 8
    hb = TB // halo
    tok = lambda w: pl.BlockSpec((TB, w), lambda i: (i, 0))
    in_specs = [tok(HW), tok(HW), tok(256),
                pl.BlockSpec((halo, 256), lambda i: (jnp.maximum(i * hb - 1, 0), 0)),
                pl.BlockSpec((halo, 256), lambda i: (jnp.minimum((i + 1) * hb, n // halo - 1), 0)),
                tok(d),
                pl.BlockSpec((1, 1, mod_l.shape[-1]), lambda i: (row_off + i // bpm, 0, 0)),
                _const_spec((HW, d)), _const_spec((256, d)), _const_spec((HW, d)),
                _const_spec((256, 256)), _const_spec((1, 256)), _const_spec((1, d)),
                _const_spec((d, PEER_HEADS * LANE)), _const_spec((2, PEER_NKEYS, LANE))]
    nk = PEER_HEADS * PEER_TOPK
    return pl.pallas_call(
        functools.partial(_out_kernel, bps=bps, seq_len=seq_len),
        grid=(nb,),
        in_specs=in_specs,
        out_specs=[tok(d), tok(d), tok(nk), pl.BlockSpec((nk, TB), lambda i: (0, i)), tok(nk),
                   pl.BlockSpec((d // LANE, TB, LANE), lambda i: (0, i, 0))],
        out_shape=[jax.ShapeDtypeStruct((n, d), F32), jax.ShapeDtypeStruct((n, d), F32),
                   jax.ShapeDtypeStruct((n, nk), jnp.int32), jax.ShapeDtypeStruct((nk, n), F32),
                   jax.ShapeDtypeStruct((n, nk), F32),
                   jax.ShapeDtypeStruct((d // LANE, n, LANE), F32)],
        scratch_shapes=[pltpu.VMEM((PEER_HEADS, TB, LANE), BF16), pltpu.VMEM((nk, TB), F32)],
        compiler_params=_params(("arbitrary",)),
        name="out_proj",
    )(on, om, p, p, p, x, mod_l, lw["w_o_na"], lw["w_o_pool"], lw["w_o_mla"],
      lw["pool_w"], lw["pool_scale"], lw["norm2"], lw["peer_wq"], lw["peer_sk"])


def _gelu_tanh(x):
    return x * (0.5 * (1.0 + jnp.tanh(0.7978845608028654 * (x + 0.044715 * (x * x * x)))))


def _peer_token_mix(chunk, hrow, gcol, ch):
    acc = None
    for s in range(ch):
        us = lax.bitcast_convert_type(chunk(s) & jnp.int32(-65536), F32)
        term = us * hrow[:, s * LANE:(s + 1) * LANE]
        acc = term if acc is None else acc + term
    wgt = gcol * _gelu_tanh(jnp.sum(acc, axis=-1, keepdims=True))
    parts = []
    for s in range(ch):
        vs = lax.bitcast_convert_type(chunk(s) << 16, F32)
        parts.append(jnp.sum(vs * wgt, axis=0, keepdims=True))
    return jnp.concatenate(parts, axis=-1)


def _sc_peer(table3, ids, gates, h2c, n):
    ch, _, lane = h2c.shape
    nk = ids.shape[1]
    info = plsc.get_sparse_core_info()
    nc, nw, nl = info.num_cores, info.num_cores * info.num_subcores, info.num_lanes
    tpw = n // nw
    win = 32
    nq = nk // win
    cpr = lane // nl
    nchunk = ch * cpr
    hc = nchunk // 2
    assert n % nw == 0 and nk % win == 0 and win % nl == 0
    mesh = plsc.VectorSubcoreMesh(core_axis_name="core", subcore_axis_name="subcore")
    hi_mask = jnp.int32(-65536)

    @functools.partial(
        pl.kernel, mesh=mesh,
        out_type=jax.ShapeDtypeStruct((ch, n, lane), F32),
        compiler_params=pltpu.CompilerParams(needs_layout_passes=False),
        scratch_types=[pltpu.VMEM((nk,), jnp.int32), pltpu.VMEM((nk,), F32),
                       pltpu.VMEM((ch, lane), F32), pltpu.VMEM((ch, lane), F32),
                       pltpu.VMEM((win, ch, lane), jnp.int32), pltpu.VMEM((win, ch, lane), jnp.int32),
                       pltpu.VMEM((win * nl,), F32), pltpu.VMEM((win,), F32),
                       pltpu.SemaphoreType.DMA, pltpu.SemaphoreType.DMA, pltpu.SemaphoreType.DMA])
    def peer(tab_hbm, ids_hbm, g_hbm, h2_hbm, y_hbm,
             idx_v, g_v, x_v, y_v, rows_a, rows_b, part_v, w_v, sem_a, sem_b, sem_x):
        wid = lax.axis_index("subcore") * nc + lax.axis_index("core")
        bufs = ((rows_a, sem_a), (rows_b, sem_b))
        lanes = lax.iota(jnp.int32, nl)
        zero = jnp.zeros((nl,), F32)

        def chunk_copies(tok, to_hbm):
            if to_hbm:
                return [pltpu.make_async_copy(y_v.at[j], y_hbm.at[j, tok], sem_x) for j in range(ch)]
            return [pltpu.make_async_copy(h2_hbm.at[j, tok], x_v.at[j], sem_x) for j in range(ch)]

        def fetch(q):
            rows, sem = bufs[q % 2]
            return pltpu.make_async_copy(tab_hbm.at[idx_v.at[pl.ds(q * win, win)]], rows, sem)

        def word(rows, r, cc):
            return rows[r, cc // cpr, pl.ds((cc % cpr) * nl, nl)]

        @pl.loop(0, tpw)
        def _(ti):
            tok = wid * tpw + ti
            loads = chunk_copies(tok, False)
            for cp in loads:
                cp.start()
            pltpu.sync_copy(ids_hbm.at[tok], idx_v)
            pltpu.sync_copy(g_hbm.at[tok], g_v)
            for cp in loads:
                cp.wait()
            for cc in range(nchunk):
                y_v[cc // cpr, pl.ds((cc % cpr) * nl, nl)] = zero
            fetch(0).start()
            for q in range(nq):
                rows = bufs[q % 2][0]
                fetch(q).wait()
                if q + 1 < nq:
                    fetch(q + 1).start()

                for half in range(2):
                    xs = [x_v[(half * hc + c) // cpr, pl.ds(((half * hc + c) % cpr) * nl, nl)]
                          for c in range(hc)]

                    @pl.loop(0, win)
                    def _(r):
                        accs = [None] * 4
                        for c in range(hc):
                            u = lax.bitcast_convert_type(word(rows, r, half * hc + c) & hi_mask, F32)
                            t = u * xs[c]
                            accs[c % 4] = t if accs[c % 4] is None else accs[c % 4] + t
                        acc = (accs[0] + accs[1]) + (accs[2] + accs[3])
                        po = pl.multiple_of(r * nl, nl)
                        if half == 0:
                            part_v[pl.ds(po, nl)] = acc
                        else:
                            part_v[pl.ds(po, nl)] = part_v[pl.ds(po, nl)] + acc

                for grp in range(win // nl):
                    s = zero
                    for rr in range(nl):
                        tot = jnp.sum(part_v[pl.ds((grp * nl + rr) * nl, nl)])
                        s = jnp.where(lanes == rr, tot, s)
                    z = 0.7978845608028654 * (s + 0.044715 * (s * s * s))
                    tanh = 1.0 - 2.0 / (jnp.exp(2.0 * z) + 1.0)
                    gate = g_v[pl.ds(q * win + grp * nl, nl)]
                    w_v[pl.ds(grp * nl, nl)] = gate * (s * (0.5 * (1.0 + tanh)))

                for half in range(2):
                    def body(r, yacc):
                        wr = plsc.load_gather(w_v, [jnp.full((nl,), r, jnp.int32)])
                        out = []
                        for c in range(hc):
                            v = lax.bitcast_convert_type(word(rows, r, half * hc + c) << 16, F32)
                            out.append(yacc[c] + wr * v)
                        return tuple(out)

                    yacc = lax.fori_loop(0, win, body, tuple(zero for _ in range(hc)))
                    for c in range(hc):
                        cc = half * hc + c
                        sl = (cc // cpr, pl.ds((cc % cpr) * nl, nl))
                        y_v[sl] = y_v[sl] + yacc[c]
            stores = chunk_copies(tok, True)
            for cp in stores:
                cp.start()
            for cp in stores:
                cp.wait()

    return peer(table3, ids, gates, h2c)


def _residual_kernel(x1_ref, y_ref, mod_ref, x2_hbm, after_hbm, o_ref):
    del x2_hbm
    del after_hbm
    d = x1_ref.shape[1]
    g2 = mod_ref[0][:, 5 * d:6 * d]
    for j in range(d // LANE):
        sl = slice(j * LANE, (j + 1) * LANE)
        o_ref[:, sl] = x1_ref[:, sl] + g2[:, sl] * y_ref[j]


def _residual(x1, y, mod_l, row, x2, after):
    n, d = x1.shape
    tok = pl.BlockSpec((PEER_TB, d), lambda i: (i, 0))
    any_spec = pl.BlockSpec(memory_space=pl.ANY)
    return pl.pallas_call(
        _residual_kernel,
        grid=(y.shape[1] // PEER_TB,),
        in_specs=[tok, pl.BlockSpec((d // LANE, PEER_TB, LANE), lambda i: (0, i, 0)),
                  pl.BlockSpec((1, 1, mod_l.shape[-1]), lambda i: (row, 0, 0)),
                  any_spec, any_spec],
        out_specs=tok,
        out_shape=jax.ShapeDtypeStruct((n, d), F32),
        input_output_aliases={3: 0},
        compiler_params=_params(("arbitrary",)),
        name="residual",
    )(x1, y, mod_l, x2, after)


def _peer_kernel(ids_hbm, gt_ref, h2_ref, x1_ref, mod_ref, tab_hbm, o_ref,
                 ids_s, buf, sem_i, sem_r, *, first_block):
    d = x1_ref.shape[1]
    ch = d // LANE
    pitch = ch + 1
    nsub = x1_ref.shape[0] // PEER_SUB
    nk = gt_ref.shape[0]
    nids = PEER_SUB * nk
    i = pl.program_id(0) + first_block
    g2 = mod_ref[0][:, 5 * d:6 * d]
    tok_lane = lax.broadcasted_iota(jnp.int32, gt_ref.shape, 1)

    def ids_copy(j, slot):
        start = pl.multiple_of((i * nsub + j) * nids, nids)
        return pltpu.make_async_copy(ids_hbm.at[pl.ds(start, nids)],
                                     ids_s.at[pl.ds(slot * nids, nids)], sem_i.at[slot])

    def row_copy(slot, e, f):
        src = tab_hbm.at[pl.ds(pl.multiple_of(e * ch, ch), ch), :]
        dst = buf.at[slot, pl.ds(f * pitch, ch), :]
        return pltpu.make_async_copy(src, dst, sem_r.at[slot])

    def issue_rows(slot):
        for t in range(PEER_SUB):
            def body(kk, _):
                for r in range(8):
                    f = t * nk + kk * 8 + r
                    row_copy(slot, ids_s[slot * nids + f], f).start(priority=r % 2)
                return 0

            lax.fori_loop(0, nk // 8, body, 0)

    def wait_rows(slot):
        done = buf.at[slot, pl.ds(0, nids * ch), :]
        pltpu.make_async_copy(done, done, sem_r.at[slot]).wait()

    def compute(slot, j):
        base = pl.multiple_of(j * PEER_SUB, PEER_SUB)
        h8 = h2_ref[pl.ds(base, PEER_SUB), :]
        ys = []
        for t in range(PEER_SUB):
            chunk = lambda s: buf[slot, pl.ds(t * nk * pitch + s, nk, stride=pitch), :]
            gcol = jnp.sum(jnp.where(tok_lane == base + t, gt_ref[...], 0.0), axis=-1, keepdims=True)
            ys.append(_peer_token_mix(chunk, h8[t:t + 1, :], gcol, ch))
        y8 = jnp.concatenate(ys, axis=0)
        o_ref[pl.ds(base, PEER_SUB), :] = x1_ref[pl.ds(base, PEER_SUB), :] + g2 * y8

    first = ids_copy(0, 0)
    first.start()
    first.wait()
    issue_rows(0)
    ids_copy(1, 1).start()

    def pair(jj, _):
        j0 = 2 * jj
        ids_copy(j0 + 1, 1).wait()
        issue_rows(1)

        @pl.when(j0 + 2 < nsub)
        def _():
            ids_copy(j0 + 2, 0).start()

        wait_rows(0)
        compute(0, j0)

        @pl.when(j0 + 2 < nsub)
        def _():
            ids_copy(j0 + 2, 0).wait()
            issue_rows(0)

        @pl.when(j0 + 3 < nsub)
        def _():
            ids_copy(j0 + 3, 1).start()

        wait_rows(1)
        compute(1, j0 + 1)
        return 0

    lax.fori_loop(0, nsub // 2, pair, 0)


def _pack_tables(peer_u, peer_v):
    e, d = peer_u.shape
    ub = lax.bitcast_convert_type(peer_u.astype(BF16), jnp.uint16).astype(jnp.uint32)
    vb = lax.bitcast_convert_type(peer_v.astype(BF16), jnp.uint16).astype(jnp.uint32)
    words = lax.bitcast_convert_type((ub << 16) | vb, jnp.int32)
    return words.reshape(e, d // LANE, LANE)


def _peer(x1, h2, ids, gt, mod_l, row, table, tok0):
    n, d = x1.shape
    nk = gt.shape[0]
    b0 = tok0 // PEER_TB
    nb = n // PEER_TB - b0
    tok = pl.BlockSpec((PEER_TB, d), lambda i: (i + b0, 0))
    any_spec = pl.BlockSpec(memory_space=pl.ANY)
    return pl.pallas_call(
        functools.partial(_peer_kernel, first_block=b0),
        grid=(nb,),
        in_specs=[any_spec,
                  pl.BlockSpec((nk, PEER_TB), lambda i: (0, i + b0)),
                  tok, tok,
                  pl.BlockSpec((1, 1, mod_l.shape[-1]), lambda i: (row, 0, 0)),
                  any_spec],
        out_specs=tok,
        out_shape=jax.ShapeDtypeStruct((n, d), F32),
        scratch_shapes=[pltpu.SMEM((2 * PEER_SUB * nk,), jnp.int32),
                        pltpu.VMEM((2, PEER_SUB * nk * (d // LANE + 1), LANE), jnp.int32),
                        pltpu.SemaphoreType.DMA((2,)),
                        pltpu.SemaphoreType.DMA((2,))],
        compiler_params=_params(("arbitrary",)),
        name="peer",
    )(ids.reshape(n * nk), gt, h2, x1, mod_l, table.reshape(-1, LANE))


def _pad_heads(w, width):
    pad = [(0, 0)] * (w.ndim - 1) + [(0, LANE - width)]
    w = jnp.pad(w, pad)
    return w.reshape(w.shape[:-2] + (HW,))


def _head_gain(g, width):
    depth = g.shape[0]
    g = jnp.pad(g, ((0, 0), (0, LANE - width)))
    return jnp.tile(g, (1, HEADS)).reshape(depth, 1, HW)


def _rope_tables(seq):
    t = np.arange(seq)
    half = MLA_ROPE // 2
    inv = ROPE_THETA ** (-np.arange(0, half, 2, dtype=np.float32) / half)
    cos = np.ones((seq, LANE), np.float32)
    sin = np.zeros((seq, LANE), np.float32)
    for off, pos in ((MLA_NOPE, t // GRID_W), (MLA_NOPE + half, t % GRID_W)):
        ang = pos.astype(np.float32)[:, None] * inv[None, :]
        q = half // 2
        cos[:, off:off + q] = np.cos(ang)
        cos[:, off + q:off + half] = np.cos(ang)
        sin[:, off:off + q] = -np.sin(ang)
        sin[:, off + q:off + half] = np.sin(ang)
    return jnp.asarray(cos), jnp.asarray(sin)


def _nat_bias(rel_bias):
    v = np.arange(WIN_R)[:, None]
    j = np.arange(WIN_R)[None, :]
    dr = j - v + WIN_R - 1
    cq = np.arange(GRID_W)[:, None]
    kc = np.arange(GRID_W)[None, :]
    cstart = np.clip(cq - WIN_C // 2, 0, GRID_W - WIN_C)
    ok = (kc >= cstart) & (kc < cstart + WIN_C)
    dc = np.clip(kc - cq + WIN_C - 1, 0, 2 * WIN_C - 2)
    b = rel_bias[:, :, dr]
    b = b[..., dc]
    b = jnp.where(jnp.asarray(ok)[None, None, None, None], b, NEG_INF)
    b = jnp.transpose(b, (0, 2, 1, 4, 3, 5))
    return b.reshape(b.shape[0], WIN_R, HEADS, GRID_W, WIN_R * GRID_W)


def _layer_weights(w_in, na_q_norm, na_k_norm, mla_cq_norm, mla_ckv_norm, mla_w_uq, mla_w_ukv,
                   mla_q_norm, mla_k_norm, w_out, pool_w, pool_scale, norm1, norm2,
                   peer_wq, peer_subkeys):
    depth, d, _ = w_in.shape
    na_w = HEADS * NA_DH
    segs = np.cumsum([0, na_w, na_w, na_w, 256, 256, 128, MLA_ROPE])
    part = lambda i: w_in[:, :, segs[i]:segs[i + 1]]
    heads = lambda w: _pad_heads(w.reshape(depth, d, HEADS, NA_DH), NA_DH)
    w_in_p = jnp.concatenate(
        [heads(part(0)), heads(part(1)), heads(part(2)), part(3), part(4), part(5),
         jnp.pad(part(6), ((0, 0), (0, 0), (0, LANE - MLA_ROPE)))], axis=-1).astype(BF16)

    w_uq = _pad_heads(mla_w_uq, MLA_QK).astype(BF16)
    k_nope = _pad_heads(mla_w_ukv[..., :MLA_NOPE], MLA_NOPE)
    eye = np.zeros((MLA_ROPE, HEADS, LANE), np.float32)
    for h in range(HEADS):
        eye[np.arange(MLA_ROPE), h, MLA_NOPE + np.arange(MLA_ROPE)] = 1.0
    eye = jnp.broadcast_to(jnp.asarray(eye.reshape(MLA_ROPE, HW)), (depth, MLA_ROPE, HW))
    zer = jnp.zeros((depth, 256 - 128 - MLA_ROPE, HW), F32)
    w_k = jnp.concatenate([k_nope, eye, zer], axis=1).astype(BF16)
    w_v = jnp.concatenate([_pad_heads(mla_w_ukv[..., MLA_NOPE:], MLA_V),
                           jnp.zeros((depth, 128, HW), F32)], axis=1).astype(BF16)

    mix_w = HEADS * NA_DH
    w_o_na = jnp.pad(w_out[:, :mix_w].reshape(depth, HEADS, NA_DH, d),
                     ((0, 0), (0, 0), (0, LANE - NA_DH), (0, 0))).reshape(depth, HW, d).astype(BF16)
    w_o_pool = w_out[:, mix_w:mix_w + 256].astype(BF16)
    w_o_mla = jnp.pad(w_out[:, mix_w + 256:].reshape(depth, HEADS, MLA_V, d),
                      ((0, 0), (0, 0), (0, LANE - MLA_V), (0, 0))).reshape(depth, HW, d).astype(BF16)
    ng = len(POOL_WINDOWS)
    pw = jnp.zeros((depth, ng * POOL_G, ng * POOL_G), F32)
    for g in range(ng):
        pw = pw.at[:, g * POOL_G:(g + 1) * POOL_G, g * POOL_G:(g + 1) * POOL_G].set(pool_w[:, g])

    half = peer_subkeys.shape[-1]
    sk = jnp.stack([jnp.pad(peer_subkeys[:, 0], ((0, 0), (0, 0), (0, LANE - half))),
                    jnp.pad(peer_subkeys[:, 1], ((0, 0), (0, 0), (LANE - half, 0)))], axis=1).astype(BF16)

    return dict(
        w_in=w_in_p, w_uq=w_uq, w_k=w_k, w_v=w_v,
        g_q=_head_gain(na_q_norm, NA_DH), g_k=_head_gain(na_k_norm, NA_DH),
        g_cq=mla_cq_norm[:, None, :], g_ckv=mla_ckv_norm[:, None, :],
        g_qm=_head_gain(mla_q_norm, MLA_QK), g_km=_head_gain(mla_k_norm, MLA_QK),
        w_o_na=w_o_na, w_o_pool=w_o_pool, w_o_mla=w_o_mla,
        pool_w=pw.astype(BF16), pool_scale=pool_scale[:, None, :],
        norm1=norm1[:, None, :], norm2=norm2[:, None, :],
        peer_wq=peer_wq.astype(BF16), peer_sk=sk)


def kernel(x_prompt, x_sample, c, cache_nat_k, cache_nat_v, cache_mla_ckv, cache_mla_krope, c_ctx, w_mod, b_mod, norm1, norm2, w_in, na_q_norm, na_k_norm, na_rel_bias, pool_w, pool_scale, mla_cq_norm, mla_ckv_norm, mla_w_uq, mla_w_ukv, mla_q_norm, mla_k_norm, w_out, peer_wq, peer_subkeys, peer_u, peer_v):
    batch, seq, d = x_prompt.shape
    db, ds, _ = x_sample.shape
    depth = w_mod.shape[0]
    past = cache_nat_k.shape[2]
    assert seq == TB and ds % TB == 0 and ds % (GRID_W * WIN_R) == 0 and db + 1 <= 8

    cond8 = jnp.concatenate([c_ctx[None, :], c, jnp.zeros((8 - 1 - db, d), F32)], axis=0)
    mod = _modulation(cond8, w_mod, b_mod).reshape(depth, 8, 1, 6 * d)

    lw_all = _layer_weights(w_in, na_q_norm, na_k_norm, mla_cq_norm, mla_ckv_norm, mla_w_uq,
                            mla_w_ukv, mla_q_norm, mla_k_norm, w_out, pool_w, pool_scale,
                            norm1, norm2, peer_wq, peer_subkeys)
    bias_all = _nat_bias(na_rel_bias)
    tables = [_pack_tables(peer_u[l], peer_v[l]) for l in range(depth)]
    cos_lat, sin_lat = _rope_tables(ds)
    cos_ctx = jnp.ones((TB, LANE), F32)
    sin_ctx = jnp.zeros((TB, LANE), F32)

    ck = jnp.concatenate([cache_mla_ckv, cache_mla_krope,
                          jnp.zeros(cache_mla_ckv.shape[:-1] + (256 - 128 - MLA_ROPE,), F32)],
                         axis=-1).astype(BF16)
    kc_mla, vc_mla = _cache_kv(ck, lw_all["w_k"], lw_all["w_v"], lw_all["g_km"])
    kc_na = _pad_heads(cache_nat_k, NA_DH).astype(BF16)
    vc_na = _pad_heads(cache_nat_v, NA_DH).astype(BF16)

    xs = [x_prompt.reshape(batch * seq, d)] + [x_sample[b] for b in range(db)]
    one_row = max(batch * seq, ds) + 1
    lat_bpm = ds // TB
    ks, vs, ckvs, krs = [], [], [], []
    pending = None
    after = xs[0]

    def join(item, follow):
        si, x1, y_sc, x2, mod_l = item
        xs[si] = _residual(x1, y_sc, mod_l, si, x2, follow)
        return xs[si]

    for l in range(depth):
        lw = {k: v[l] for k, v in lw_all.items()}
        mod_l = mod[l]
        for si in range(db + 1):
            x = xs[si]
            if si == 0:
                (qn, kn, vn, knf, vnf, p, qm, km, vm, ckv, kr) = _in_proj(
                    x, mod_l, 0, one_row, lw, cos_ctx, sin_ctx, 1, after)
                on, om = _ctx_attn(qn, kn, vn, qm, km, vm, seq)
                x1, h2, ids, gt, gn, h2c = _out_proj(x, on, om, p, mod_l, 0, one_row, lw, seq)
                ks.append(knf.reshape(batch, seq, HEADS, LANE)[..., :NA_DH])
                vs.append(vnf.reshape(batch, seq, HEADS, LANE)[..., :NA_DH])
                ckvs.append(ckv.reshape(batch, seq, 128))
                krs.append(kr.reshape(batch, seq, LANE)[..., :MLA_ROPE])
            else:
                b = si - 1
                (qn, kn, vn, _, _, p, qm, km, vm, _, _) = _in_proj(
                    x, mod_l, si, one_row, lw, cos_lat, sin_lat, lat_bpm, after)
                on = _nat_attn(qn, kn, vn, kc_na[b:b + 1, l], vc_na[b:b + 1, l], bias_all[l], 1)
                om = _lat_mla(qm, km, vm, kc_mla[b:b + 1, l], vc_mla[b:b + 1, l], 1)
                x1, h2, ids, gt, gn, h2c = _out_proj(x, on, om, p, mod_l, si, one_row, lw, ds)
            n_sc = x.shape[0] * SC_SHARE[0] // SC_SHARE[1] // PEER_TB * PEER_TB
            y_sc = _sc_peer(tables[l], ids, gn, h2c, n_sc)
            x2 = _peer(x1, h2, ids, gt, mod_l, si, tables[l], n_sc)
            after = x2 if pending is None else join(pending, x2)
            pending = (si, x1, y_sc, x2, mod_l)
    join(pending, pending[1])

    return (xs[0].reshape(batch, seq, d), jnp.stack(xs[1:], axis=0),
            jnp.stack(ks, axis=1), jnp.stack(vs, axis=1),
            jnp.stack(ckvs, axis=1), jnp.stack(krs, axis=1))
```

```python
import functools

import numpy as np
import jax
import jax.numpy as jnp
from jax import lax
from jax.experimental import pallas as pl
from jax.experimental.pallas import tpu as pltpu
from jax.experimental.pallas import tpu_sc as plsc

F32 = jnp.float32
BF16 = jnp.bfloat16

EPS = 1e-6
ROPE_THETA = 10000.0
NEG_INF = -1e30
GRID_W = 64
HEADS = 6
NA_DH = 64
WIN_R = 8
WIN_C = 16
POOL_WINDOWS = (2, 4, 8, 16)
POOL_G = 64
MLA_NOPE = 64
MLA_ROPE = 32
MLA_QK = MLA_NOPE + MLA_ROPE
MLA_V = 64
PEER_HEADS = 8
PEER_NKEYS = 128
PEER_TOPK = 16
LANE = 128
HW = HEADS * LANE
TB = 256
TQ = 256
PEER_TB = 128
PEER_SUB = 8
VMEM_LIMIT = 56 * 1024 * 1024
SC_SHARE = (23, 32)

_CQ, _CK, _CV = 0, HW, 2 * HW
_CP = 3 * HW
_CCQ = _CP + 256
_CCKV = _CCQ + 256
_CKR = _CCKV + 128
IN_W = _CKR + 128


def _params(sem, vmem=VMEM_LIMIT):
    return pltpu.CompilerParams(dimension_semantics=sem, vmem_limit_bytes=vmem)


def _const_spec(shape):
    n = len(shape)
    return pl.BlockSpec(shape, lambda *_: (0,) * n)


def _nt_dot(a, b):
    return lax.dot_general(a, b, (((1,), (1,)), ((), ())), preferred_element_type=F32)


def _mod_kernel(c_ref, w_ref, b_ref, o_ref):
    c = c_ref[...]
    s = c / (1.0 + jnp.exp(-c))
    o_ref[0] = jnp.dot(s, w_ref[0], preferred_element_type=F32,
                       precision=lax.Precision.HIGHEST) + b_ref[0]


def _modulation(cond8, w_mod, b_mod):
    depth, d, n6 = w_mod.shape
    tn = n6 // 4
    return pl.pallas_call(
        _mod_kernel,
        grid=(depth, n6 // tn),
        in_specs=[_const_spec((8, d)),
                  pl.BlockSpec((1, d, tn), lambda l, j: (l, 0, j)),
                  pl.BlockSpec((1, 1, tn), lambda l, j: (l, 0, j))],
        out_specs=pl.BlockSpec((1, 8, tn), lambda l, j: (l, 0, j)),
        out_shape=jax.ShapeDtypeStruct((depth, 8, n6), F32),
        compiler_params=_params(("arbitrary", "arbitrary")),
        name="modulation",
    )(cond8, w_mod, b_mod.reshape(depth, 1, n6))


def _rms(z, gain):
    return z * lax.rsqrt(jnp.mean(z * z, axis=-1, keepdims=True) + EPS) * gain


def _head_rms(zh, gain_h, n_real):
    ms = jnp.sum(zh * zh, axis=-1, keepdims=True) * (1.0 / n_real)
    return zh * lax.rsqrt(ms + EPS) * gain_h


def _rope(zh, cos, sin, is_x1):
    rot = jnp.where(is_x1, pltpu.roll(zh, LANE - 8, 1), pltpu.roll(zh, 8, 1))
    return zh * cos + rot * sin


def _is_x1(rows):
    lane = lax.broadcasted_iota(jnp.int32, (rows, LANE), 1)
    first = jnp.where(lane >= MLA_NOPE, jnp.where(lane < MLA_NOPE + 8, 1, 0), 0)
    second = jnp.where(lane >= MLA_NOPE + 16, jnp.where(lane < MLA_NOPE + 24, 1, 0), 0)
    return (first + second) > 0


def _mla_kv(ck, wk_ref, wv_ref, gk_ref, cos, sin, km_ref, vm_ref):
    rows = ck.shape[0]
    kk = jnp.dot(ck, wk_ref[...], preferred_element_type=F32)
    is_x1 = _is_x1(rows)
    for h in range(HEADS):
        sl = slice(h * LANE, (h + 1) * LANE)
        kh = _head_rms(kk[:, sl], gk_ref[:, sl], MLA_QK)
        km_ref[:, sl] = _rope(kh, cos, sin, is_x1).astype(BF16)
    vm_ref[...] = jnp.dot(ck, wv_ref[...], preferred_element_type=F32).astype(BF16)


def _in_kernel(x_ref, mod_ref, n1_ref, w_ref, wuq_ref, wk_ref, wv_ref,
               gq_ref, gk_ref, gcq_ref, gckv_ref, gqm_ref, gkm_ref, cos_ref, sin_ref, after_hbm,
               qn_ref, kn_ref, vn_ref, knf_ref, vnf_ref, p_ref,
               qm_ref, km_ref, vm_ref, ckv_ref, kr_ref):
    del after_hbm
    d = x_ref.shape[1]
    rows = x_ref.shape[0]
    mod = mod_ref[0]
    sh1 = mod[:, 0:d]
    sc1 = mod[:, d:2 * d]
    h = _rms(x_ref[...], n1_ref[...]) * (1.0 + sc1) + sh1
    hb = h.astype(BF16)

    def proj(lo, hi):
        return jnp.dot(hb, w_ref[:, lo:hi], preferred_element_type=F32)

    cos = cos_ref[...]
    sin = sin_ref[...]
    is_x1 = _is_x1(rows)

    zq = proj(_CQ, _CQ + HW)
    zk = proj(_CK, _CK + HW)
    for hh in range(HEADS):
        sl = slice(hh * LANE, (hh + 1) * LANE)
        qn_ref[:, sl] = (_head_rms(zq[:, sl], gq_ref[:, sl], NA_DH) * (NA_DH ** -0.5)).astype(BF16)
        kh = _head_rms(zk[:, sl], gk_ref[:, sl], NA_DH)
        knf_ref[:, sl] = kh
        kn_ref[:, sl] = kh.astype(BF16)
    zv = proj(_CV, _CV + HW)
    vnf_ref[...] = zv
    vn_ref[...] = zv.astype(BF16)
    p_ref[...] = proj(_CP, _CP + 256)

    cq = _rms(proj(_CCQ, _CCQ + 256), gcq_ref[...])
    zqm = jnp.dot(cq.astype(BF16), wuq_ref[...], preferred_element_type=F32)
    for hh in range(HEADS):
        sl = slice(hh * LANE, (hh + 1) * LANE)
        qh = _head_rms(zqm[:, sl], gqm_ref[:, sl], MLA_QK)
        qm_ref[:, sl] = (_rope(qh, cos, sin, is_x1) * (MLA_QK ** -0.5)).astype(BF16)

    ckv = _rms(proj(_CCKV, _CCKV + 128), gckv_ref[...])
    kr = proj(_CKR, _CKR + 128)
    ckv_ref[...] = ckv
    kr_ref[...] = kr
    ck = jnp.concatenate([ckv, kr], axis=-1).astype(BF16)
    _mla_kv(ck, wk_ref, wv_ref, gkm_ref, cos, sin, km_ref, vm_ref)


def _in_proj(x, mod_l, row_off, bpm, lw, cos_t, sin_t, rope_blocks, after):
    n, d = x.shape
    nb = n // TB
    tok = lambda w: pl.BlockSpec((TB, w), lambda i: (i, 0))
    rope_spec = pl.BlockSpec((TB, LANE), lambda i: (i % rope_blocks, 0))
    in_specs = [tok(d),
                pl.BlockSpec((1, 1, mod_l.shape[-1]), lambda i: (row_off + i // bpm, 0, 0)),
                _const_spec((1, d)), _const_spec((d, IN_W)), _const_spec((256, HW)),
                _const_spec((256, HW)), _const_spec((256, HW)),
                _const_spec((1, HW)), _const_spec((1, HW)), _const_spec((1, 256)),
                _const_spec((1, 128)), _const_spec((1, HW)), _const_spec((1, HW)),
                rope_spec, rope_spec, pl.BlockSpec(memory_space=pl.ANY)]
    widths = [(HW, BF16), (HW, BF16), (HW, BF16), (HW, F32), (HW, F32), (256, F32),
              (HW, BF16), (HW, BF16), (HW, BF16), (128, F32), (128, F32)]
    return pl.pallas_call(
        _in_kernel,
        grid=(nb,),
        in_specs=in_specs,
        out_specs=[tok(w) for w, _ in widths],
        out_shape=[jax.ShapeDtypeStruct((n, w), dt) for w, dt in widths],
        compiler_params=_params(("arbitrary",)),
        name="in_proj",
    )(x, mod_l, lw["norm1"], lw["w_in"], lw["w_uq"], lw["w_k"], lw["w_v"],
      lw["g_q"], lw["g_k"], lw["g_cq"], lw["g_ckv"], lw["g_qm"], lw["g_km"], cos_t, sin_t, after)


def _cache_kernel(ck_ref, wk_ref, wv_ref, gk_ref, km_ref, vm_ref):
    rows = ck_ref.shape[2]
    cos = jnp.ones((rows, LANE), F32)
    sin = jnp.zeros((rows, LANE), F32)
    _mla_kv(ck_ref[0, 0], wk_ref.at[0], wv_ref.at[0], gk_ref.at[0], cos, sin,
            km_ref.at[0, 0], vm_ref.at[0, 0])


def _cache_kv(ck, w_k, w_v, g_km):
    db, depth, p, _ = ck.shape
    spec = lambda w: pl.BlockSpec((1, 1, p, w), lambda b, l: (b, l, 0, 0))
    wspec = lambda r: pl.BlockSpec((1, r, HW), lambda b, l: (l, 0, 0))
    return pl.pallas_call(
        _cache_kernel,
        grid=(db, depth),
        in_specs=[spec(256), wspec(256), wspec(256), wspec(1)],
        out_specs=[spec(HW), spec(HW)],
        out_shape=[jax.ShapeDtypeStruct((db, depth, p, HW), BF16)] * 2,
        compiler_params=_params(("arbitrary", "arbitrary")),
        name="cache_kv",
    )(ck, w_k, w_v, g_km)


def _softmax_av(s_list, v_list):
    m = s_list[0].max(axis=-1, keepdims=True)
    for s in s_list[1:]:
        m = jnp.maximum(m, s.max(axis=-1, keepdims=True))
    acc = None
    den = None
    for s, v in zip(s_list, v_list):
        p = jnp.exp(s - m)
        l = p.sum(axis=-1, keepdims=True)
        o = jnp.dot(p.astype(BF16), v, preferred_element_type=F32)
        acc = o if acc is None else acc + o
        den = l if den is None else den + l
    return acc / den


def _ctx_attn_kernel(qn, kn, vn, qm, km, vm, on, om):
    for q, k, v, o in ((qn, kn, vn, on), (qm, km, vm, om)):
        for h in range(HEADS):
            sl = slice(h * LANE, (h + 1) * LANE)
            s = _nt_dot(q[:, sl], k[:, sl])
            o[:, sl] = _softmax_av([s], [v[:, sl]]).astype(BF16)


def _ctx_attn(qn, kn, vn, qm, km, vm, seq):
    n = qn.shape[0]
    spec = pl.BlockSpec((seq, HW), lambda i: (i, 0))
    return pl.pallas_call(
        _ctx_attn_kernel,
        grid=(n // seq,),
        in_specs=[spec] * 6,
        out_specs=[spec] * 2,
        out_shape=[jax.ShapeDtypeStruct((n, HW), BF16)] * 2,
        compiler_params=_params(("arbitrary",)),
        name="ctx_attn",
    )(qn, kn, vn, qm, km, vm)


def _lat_mla_kernel(q, k, v, kc, vc, o):
    s1 = _nt_dot(q[...], k[...])
    s2 = _nt_dot(q[...], kc[0])
    o[...] = _softmax_av([s1, s2], [v[...], vc[0]]).astype(BF16)


def _lat_mla(qm, km, vm, kc, vc, db):
    n = qm.shape[0]
    ds = n // db
    nq = ds // TQ
    qspec = pl.BlockSpec((TQ, LANE), lambda b, h, i: (b * nq + i, h))
    kspec = pl.BlockSpec((ds, LANE), lambda b, h, i: (b, h))
    cspec = pl.BlockSpec((1, kc.shape[1], LANE), lambda b, h, i: (b, 0, h))
    return pl.pallas_call(
        _lat_mla_kernel,
        grid=(db, HEADS, nq),
        in_specs=[qspec, kspec, kspec, cspec, cspec],
        out_specs=qspec,
        out_shape=jax.ShapeDtypeStruct((n, HW), BF16),
        compiler_params=_params(("arbitrary",) * 3),
        name="lat_mla",
    )(qm, km, vm, kc, vc)


def _nat_kernel(q, k, v, kc, vc, bias, o, *, rows):
    r = pl.program_id(1)
    rs = jnp.clip(r - WIN_R // 2, 0, rows - WIN_R)
    start = pl.multiple_of(rs * GRID_W, GRID_W)
    band = WIN_R * GRID_W
    for h in range(HEADS):
        sl = slice(h * LANE, (h + 1) * LANE)
        qh = q[:, sl]
        s1 = _nt_dot(qh, k[pl.ds(start, band), sl]) + bias[0, h]
        s2 = _nt_dot(qh, kc[0, :, sl])
        o[:, sl] = _softmax_av([s1, s2], [v[pl.ds(start, band), sl], vc[0, :, sl]]).astype(BF16)


def _nat_attn(qn, kn, vn, kc, vc, bias, db):
    n = qn.shape[0]
    ds = n // db
    rows = ds // GRID_W
    band = WIN_R * GRID_W

    def variant(r):
        return jnp.where(r < WIN_R // 2, r, jnp.where(r > rows - WIN_R // 2, r - (rows - WIN_R), WIN_R // 2))

    qspec = pl.BlockSpec((GRID_W, HW), lambda b, r: (b * rows + r, 0))
    kspec = pl.BlockSpec((ds, HW), lambda b, r: (b, 0))
    cspec = pl.BlockSpec((1, kc.shape[1], HW), lambda b, r: (b, 0, 0))
    bspec = pl.BlockSpec((1, HEADS, GRID_W, band), lambda b, r: (variant(r), 0, 0, 0))
    return pl.pallas_call(
        functools.partial(_nat_kernel, rows=rows),
        grid=(db, rows),
        in_specs=[qspec, kspec, kspec, cspec, cspec, bspec],
        out_specs=qspec,
        out_shape=jax.ShapeDtypeStruct((n, HW), BF16),
        compiler_params=_params(("arbitrary", "arbitrary")),
        name="nat_attn",
    )(qn, kn, vn, kc, vc, bias)


def _split3(x):
    hi = x.astype(BF16)
    r = x - hi.astype(F32)
    mid = r.astype(BF16)
    lo = (r - mid.astype(F32)).astype(BF16)
    return hi, mid, lo


def _pool(p_prev, p_cur, p_next, posb, seq_len):
    rows = p_cur.shape[0]
    halo = p_prev.shape[0]
    ext = rows + 2 * halo
    pext = jnp.concatenate([p_prev, p_cur, p_next], axis=0)
    parts = _split3(pext)
    t = posb + lax.broadcasted_iota(jnp.int32, (rows, ext), 0)
    s = posb - halo + lax.broadcasted_iota(jnp.int32, (rows, ext), 1)
    tcol = posb + lax.broadcasted_iota(jnp.int32, (rows, 1), 0)
    grp = lax.broadcasted_iota(jnp.int32, (rows, 256), 1) // POOL_G
    d = jnp.zeros((rows, 256), F32)
    for gi, w in enumerate(POOL_WINDOWS):
        lo = jnp.maximum(t - w // 2, 0)
        hi = jnp.minimum(t + (w - w // 2), seq_len)
        sel = jnp.where(s >= lo, jnp.where(s < hi, 1.0, 0.0), 0.0).astype(BF16)
        tot = sum(jnp.dot(sel, part, preferred_element_type=F32) for part in parts)
        cnt = (jnp.minimum(tcol + (w - w // 2), seq_len) - jnp.maximum(tcol - w // 2, 0)).astype(F32)
        d = jnp.where(grp == gi, tot / cnt - p_cur, d)
    return d


def _first_max(x, pos, sentinel):
    m = jnp.max(x, axis=0, keepdims=True)
    idx = jnp.min(jnp.where(x == m, pos, sentinel), axis=0, keepdims=True)
    return m, idx


def _topk_stage1(qh, sk_ref):
    c = qh.shape[0]
    key_pos = lax.broadcasted_iota(jnp.int32, (PEER_NKEYS, c), 0).astype(F32)
    row16 = lax.broadcasted_iota(jnp.int32, (PEER_TOPK, c), 0)
    neg = jnp.float32(-jnp.inf)
    s0 = _nt_dot(sk_ref[0], qh)
    s1 = _nt_dot(sk_ref[1], qh)

    def stage1(a, carry):
        out = []
        for s, sv, si in (carry[0:3], carry[3:6]):
            m, idx = _first_max(s, key_pos, float(PEER_NKEYS))
            out += [jnp.where(key_pos == idx, neg, s),
                    jnp.where(row16 == a, m, sv), jnp.where(row16 == a, idx, si)]
        return tuple(out)

    zf = jnp.zeros((PEER_TOPK, c), F32)
    _, sv0, si0, _, sv1, si1 = lax.fori_loop(0, PEER_TOPK, stage1, (s0, zf, zf, s1, zf, zf))
    return sv0, si0, sv1, si1


def _topk_pieces(sv0, sv1):
    c = sv0.shape[1]
    neg = jnp.float32(-jnp.inf)
    sub8 = lax.broadcasted_iota(jnp.int32, (8, c), 0)
    sub8f = sub8.astype(F32)
    cs, cf = [], []
    for a in range(8):
        nb = PEER_TOPK // (a + 1)
        for b0 in range(0, nb, 8):
            val = sv0[a:a + 1] + sv1[b0:b0 + 8]
            if nb - b0 < 8:
                val = jnp.where(sub8 < nb - b0, val, neg)
            cs.append(val)
            cf.append(sub8f + float(a * PEER_TOPK + b0))
    cs.append(sv0[8:16] + sv1[0:1])
    cf.append((sub8f + 8.0) * float(PEER_TOPK))
    return cs, cf


def _topk_stage2(chains, cf):
    npc = len(cf)
    c = cf[0].shape[1]
    row16 = lax.broadcasted_iota(jnp.int32, (PEER_TOPK, c), 0)
    neg = jnp.float32(-jnp.inf)
    nflat = float(PEER_TOPK * PEER_TOPK)
    zf = jnp.zeros((PEER_TOPK, c), F32)

    def step(k, carry):
        out = []
        for ch in range(len(chains)):
            vals = carry[ch * (npc + 2):ch * (npc + 2) + npc]
            tv, tp = carry[ch * (npc + 2) + npc], carry[ch * (npc + 2) + npc + 1]
            m = vals[0]
            for v in vals[1:]:
                m = jnp.maximum(m, v)
            m = jnp.max(m, axis=0, keepdims=True)
            pos = None
            for v, f in zip(vals, cf):
                cand = jnp.where(v == m, f, nflat)
                pos = cand if pos is None else jnp.minimum(pos, cand)
            pos = jnp.min(pos, axis=0, keepdims=True)
            out += [jnp.where(f == pos, neg, v) for v, f in zip(vals, cf)]
            out += [jnp.where(row16 == k, m, tv), jnp.where(row16 == k, pos, tp)]
        return tuple(out)

    init = []
    for cs in chains:
        init += list(cs) + [zf, zf]
    res = lax.fori_loop(0, PEER_TOPK, step, tuple(init))
    return [(res[ch * (npc + 2) + npc], res[ch * (npc + 2) + npc + 1]) for ch in range(len(chains))]


def _topk_ids(tp, si0, si1):
    a = jnp.floor(tp * (1.0 / PEER_TOPK))
    b = tp - a * float(PEER_TOPK)
    ea = jnp.zeros_like(tp)
    eb = jnp.zeros_like(tp)
    for j in range(PEER_TOPK):
        ea = jnp.where(a == float(j), si0[j:j + 1], ea)
        eb = jnp.where(b == float(j), si1[j:j + 1], eb)
    return ea * float(PEER_NKEYS) + eb


def _out_kernel(on_ref, om_ref, pc_ref, pp_ref, pn_ref, x_ref, mod_ref,
                won_ref, wop_ref, wom_ref, pw_ref, ps_ref, n2_ref, wq_ref, sk_ref,
                x1_ref, h2_ref, ids_ref, gt_ref, gn_ref, h2c_ref, q_scr, idt_scr, *, bps, seq_len):
    d = x_ref.shape[1]
    rows = x_ref.shape[0]
    i = pl.program_id(0)
    mod = mod_ref[0]
    g1 = mod[:, 2 * d:3 * d]
    sh2 = mod[:, 3 * d:4 * d]
    sc2 = mod[:, 4 * d:5 * d]

    posb = (i % bps) * rows
    dpool = _pool(pp_ref[...], pc_ref[...], pn_ref[...], posb, seq_len)
    ypool = jnp.dot(dpool.astype(BF16), pw_ref[...], preferred_element_type=F32) * ps_ref[...]
    mix = (jnp.dot(on_ref[...], won_ref[...], preferred_element_type=F32)
           + jnp.dot(ypool.astype(BF16), wop_ref[...], preferred_element_type=F32)
           + jnp.dot(om_ref[...], wom_ref[...], preferred_element_type=F32))
    x1 = x_ref[...] + g1 * mix
    x1_ref[...] = x1
    h2 = _rms(x1, n2_ref[...]) * (1.0 + sc2) + sh2
    h2_ref[...] = h2
    for j in range(d // LANE):
        h2c_ref[j] = h2[:, j * LANE:(j + 1) * LANE]

    q = jnp.dot(h2.astype(BF16), wq_ref[...], preferred_element_type=F32)
    for hh in range(PEER_HEADS):
        q_scr[hh] = q[:, hh * LANE:(hh + 1) * LANE].astype(BF16)

    chunks = range(0, rows, LANE)

    def head(hh, _):
        sorted_keys = [_topk_stage1(q_scr[hh, c0:c0 + LANE, :], sk_ref) for c0 in chunks]
        pieces = [_topk_pieces(sv0, sv1) for sv0, _, sv1, _ in sorted_keys]
        picked = _topk_stage2([cs for cs, _ in pieces], pieces[0][1])
        r0 = pl.multiple_of(hh * PEER_TOPK, PEER_TOPK)
        for c0, (tv, tp), (_, si0, _, si1) in zip(chunks, picked, sorted_keys):
            ex = jnp.exp(tv - tv[0:1])
            gt_ref[pl.ds(r0, PEER_TOPK), c0:c0 + LANE] = ex / jnp.sum(ex, axis=0, keepdims=True)
            idt_scr[pl.ds(r0, PEER_TOPK), c0:c0 + LANE] = _topk_ids(tp, si0, si1)
        return 0

    lax.fori_loop(0, PEER_HEADS, head, 0)
    ids_ref[...] = idt_scr[...].T.astype(jnp.int32)
    gn_ref[...] = gt_ref[...].T


def _out_proj(x, on, om, p, mod_l, row_off, bpm, lw, seq_len):
    n, d = x.shape
    nb = n // TB
    bps = seq_len // TB
    halo = 8
    hb = TB // halo
    tok = lambda w: pl.BlockSpec((TB, w), lambda i: (i, 0))
    in_specs = [tok(HW), tok(HW), tok(256),
                pl.BlockSpec((halo, 256), lambda i: (jnp.maximum(i * hb - 1, 0), 0)),
                pl.BlockSpec((halo, 256), lambda i: (jnp.minimum((i + 1) * hb, n // halo - 1), 0)),
                tok(d),
                pl.BlockSpec((1, 1, mod_l.shape[-1]), lambda i: (row_off + i // bpm, 0, 0)),
                _const_spec((HW, d)), _const_spec((256, d)), _const_spec((HW, d)),
                _const_spec((256, 256)), _const_spec((1, 256)), _const_spec((1, d)),
                _const_spec((d, PEER_HEADS * LANE)), _const_spec((2, PEER_NKEYS, LANE))]
    nk = PEER_HEADS * PEER_TOPK
    return pl.pallas_call(
        functools.partial(_out_kernel, bps=bps, seq_len=seq_len),
        grid=(nb,),
        in_specs=in_specs,
        out_specs=[tok(d), tok(d), tok(nk), pl.BlockSpec((nk, TB), lambda i: (0, i)), tok(nk),
                   pl.BlockSpec((d // LANE, TB, LANE), lambda i: (0, i, 0))],
        out_shape=[jax.ShapeDtypeStruct((n, d), F32), jax.ShapeDtypeStruct((n, d), F32),
                   jax.ShapeDtypeStruct((n, nk), jnp.int32), jax.ShapeDtypeStruct((nk, n), F32),
                   jax.ShapeDtypeStruct((n, nk), F32),
                   jax.ShapeDtypeStruct((d // LANE, n, LANE), F32)],
        scratch_shapes=[pltpu.VMEM((PEER_HEADS, TB, LANE), BF16), pltpu.VMEM((nk, TB), F32)],
        compiler_params=_params(("arbitrary",)),
        name="out_proj",
    )(on, om, p, p, p, x, mod_l, lw["w_o_na"], lw["w_o_pool"], lw["w_o_mla"],
      lw["pool_w"], lw["pool_scale"], lw["norm2"], lw["peer_wq"], lw["peer_sk"])


def _gelu_tanh(x):
    return x * (0.5 * (1.0 + jnp.tanh(0.7978845608028654 * (x + 0.044715 * (x * x * x)))))


def _peer_token_mix(chunk, hrow, gcol, ch):
    acc = None
    for s in range(ch):
        us = lax.bitcast_convert_type(chunk(s) & jnp.int32(-65536), F32)
        term = us * hrow[:, s * LANE:(s + 1) * LANE]
        acc = term if acc is None else acc + term
    wgt = gcol * _gelu_tanh(jnp.sum(acc, axis=-1, keepdims=True))
    parts = []
    for s in range(ch):
        vs = lax.bitcast_convert_type(chunk(s) << 16, F32)
        parts.append(jnp.sum(vs * wgt, axis=0, keepdims=True))
    return jnp.concatenate(parts, axis=-1)


def _sc_peer(table3, ids, gates, h2c, n):
    ch, _, lane = h2c.shape
    nk = ids.shape[1]
    info = plsc.get_sparse_core_info()
    nc, nw, nl = info.num_cores, info.num_cores * info.num_subcores, info.num_lanes
    tpw = n // nw
    win = 32
    nq = nk // win
    cpr = lane // nl
    nchunk = ch * cpr
    hc = nchunk // 2
    assert n % nw == 0 and nk % win == 0 and win % nl == 0
    mesh = plsc.VectorSubcoreMesh(core_axis_name="core", subcore_axis_name="subcore")
    hi_mask = jnp.int32(-65536)

    @functools.partial(
        pl.kernel, mesh=mesh,
        out_type=jax.ShapeDtypeStruct((ch, n, lane), F32),
        compiler_params=pltpu.CompilerParams(needs_layout_passes=False),
        scratch_types=[pltpu.VMEM((nk,), jnp.int32), pltpu.VMEM((nk,), F32),
                       pltpu.VMEM((ch, lane), F32), pltpu.VMEM((ch, lane), F32),
                       pltpu.VMEM((win, ch, lane), jnp.int32), pltpu.VMEM((win, ch, lane), jnp.int32),
                       pltpu.VMEM((win * nl,), F32), pltpu.VMEM((win,), F32),
                       pltpu.SemaphoreType.DMA, pltpu.SemaphoreType.DMA, pltpu.SemaphoreType.DMA])
    def peer(tab_hbm, ids_hbm, g_hbm, h2_hbm, y_hbm,
             idx_v, g_v, x_v, y_v, rows_a, rows_b, part_v, w_v, sem_a, sem_b, sem_x):
        wid = lax.axis_index("subcore") * nc + lax.axis_index("core")
        bufs = ((rows_a, sem_a), (rows_b, sem_b))
        lanes = lax.iota(jnp.int32, nl)
        zero = jnp.zeros((nl,), F32)

        def chunk_copies(tok, to_hbm):
            if to_hbm:
                return [pltpu.make_async_copy(y_v.at[j], y_hbm.at[j, tok], sem_x) for j in range(ch)]
            return [pltpu.make_async_copy(h2_hbm.at[j, tok], x_v.at[j], sem_x) for j in range(ch)]

        def fetch(q):
            rows, sem = bufs[q % 2]
            return pltpu.make_async_copy(tab_hbm.at[idx_v.at[pl.ds(q * win, win)]], rows, sem)

        def word(rows, r, cc):
            return rows[r, cc // cpr, pl.ds((cc % cpr) * nl, nl)]

        @pl.loop(0, tpw)
        def _(ti):
            tok = wid * tpw + ti
            loads = chunk_copies(tok, False)
            for cp in loads:
                cp.start()
            pltpu.sync_copy(ids_hbm.at[tok], idx_v)
            pltpu.sync_copy(g_hbm.at[tok], g_v)
            for cp in loads:
                cp.wait()
            for cc in range(nchunk):
                y_v[cc // cpr, pl.ds((cc % cpr) * nl, nl)] = zero
            fetch(0).start()
            for q in range(nq):
                rows = bufs[q % 2][0]
                fetch(q).wait()
                if q + 1 < nq:
                    fetch(q + 1).start()

                for half in range(2):
                    xs = [x_v[(half * hc + c) // cpr, pl.ds(((half * hc + c) % cpr) * nl, nl)]
                          for c in range(hc)]

                    @pl.loop(0, win)
                    def _(r):
                        accs = [None] * 4
                        for c in range(hc):
                            u = lax.bitcast_convert_type(word(rows, r, half * hc + c) & hi_mask, F32)
                            t = u * xs[c]
                            accs[c % 4] = t if accs[c % 4] is None else accs[c % 4] + t
                        acc = (accs[0] + accs[1]) + (accs[2] + accs[3])
                        po = pl.multiple_of(r * nl, nl)
                        if half == 0:
                            part_v[pl.ds(po, nl)] = acc
                        else:
                            part_v[pl.ds(po, nl)] = part_v[pl.ds(po, nl)] + acc

                for grp in range(win // nl):
                    s = zero
                    for rr in range(nl):
                        tot = jnp.sum(part_v[pl.ds((grp * nl + rr) * nl, nl)])
                        s = jnp.where(lanes == rr, tot, s)
                    z = 0.7978845608028654 * (s + 0.044715 * (s * s * s))
                    tanh = 1.0 - 2.0 / (jnp.exp(2.0 * z) + 1.0)
                    gate = g_v[pl.ds(q * win + grp * nl, nl)]
                    w_v[pl.ds(grp * nl, nl)] = gate * (s * (0.5 * (1.0 + tanh)))

                for half in range(2):
                    def body(r, yacc):
                        wr = plsc.load_gather(w_v, [jnp.full((nl,), r, jnp.int32)])
                        out = []
                        for c in range(hc):
                            v = lax.bitcast_convert_type(word(rows, r, half * hc + c) << 16, F32)
                            out.append(yacc[c] + wr * v)
                        return tuple(out)

                    yacc = lax.fori_loop(0, win, body, tuple(zero for _ in range(hc)))
                    for c in range(hc):
                        cc = half * hc + c
                        sl = (cc // cpr, pl.ds((cc % cpr) * nl, nl))
                        y_v[sl] = y_v[sl] + yacc[c]
            stores = chunk_copies(tok, True)
            for cp in stores:
                cp.start()
            for cp in stores:
                cp.wait()

    return peer(table3, ids, gates, h2c)


def _residual_kernel(x1_ref, y_ref, mod_ref, x2_hbm, after_hbm, o_ref):
    del x2_hbm
    del after_hbm
    d = x1_ref.shape[1]
    g2 = mod_ref[0][:, 5 * d:6 * d]
    for j in range(d // LANE):
        sl = slice(j * LANE, (j + 1) * LANE)
        o_ref[:, sl] = x1_ref[:, sl] + g2[:, sl] * y_ref[j]


def _residual(x1, y, mod_l, row, x2, after):
    n, d = x1.shape
    tok = pl.BlockSpec((PEER_TB, d), lambda i: (i, 0))
    any_spec = pl.BlockSpec(memory_space=pl.ANY)
    return pl.pallas_call(
        _residual_kernel,
        grid=(y.shape[1] // PEER_TB,),
        in_specs=[tok, pl.BlockSpec((d // LANE, PEER_TB, LANE), lambda i: (0, i, 0)),
                  pl.BlockSpec((1, 1, mod_l.shape[-1]), lambda i: (row, 0, 0)),
                  any_spec, any_spec],
        out_specs=tok,
        out_shape=jax.ShapeDtypeStruct((n, d), F32),
        input_output_aliases={3: 0},
        compiler_params=_params(("arbitrary",)),
        name="residual",
    )(x1, y, mod_l, x2, after)


def _peer_kernel(ids_hbm, gt_ref, h2_ref, x1_ref, mod_ref, tab_hbm, o_ref,
                 ids_s, buf, sem_i, sem_r, *, first_block):
    d = x1_ref.shape[1]
    ch = d // LANE
    pitch = ch + 1
    nsub = x1_ref.shape[0] // PEER_SUB
    nk = gt_ref.shape[0]
    nids = PEER_SUB * nk
    i = pl.program_id(0) + first_block
    g2 = mod_ref[0][:, 5 * d:6 * d]
    tok_lane = lax.broadcasted_iota(jnp.int32, gt_ref.shape, 1)

    def ids_copy(j, slot):
        start = pl.multiple_of((i * nsub + j) * nids, nids)
        return pltpu.make_async_copy(ids_hbm.at[pl.ds(start, nids)],
                                     ids_s.at[pl.ds(slot * nids, nids)], sem_i.at[slot])

    def row_copy(slot, e, f):
        src = tab_hbm.at[pl.ds(pl.multiple_of(e * ch, ch), ch), :]
        dst = buf.at[slot, pl.ds(f * pitch, ch), :]
        return pltpu.make_async_copy(src, dst, sem_r.at[slot])

    def issue_rows(slot):
        for t in range(PEER_SUB):
            def body(kk, _):
                for r in range(8):
                    f = t * nk + kk * 8 + r
                    row_copy(slot, ids_s[slot * nids + f], f).start(priority=r % 2)
                return 0

            lax.fori_loop(0, nk // 8, body, 0)

    def wait_rows(slot):
        done = buf.at[slot, pl.ds(0, nids * ch), :]
        pltpu.make_async_copy(done, done, sem_r.at[slot]).wait()

    def compute(slot, j):
        base = pl.multiple_of(j * PEER_SUB, PEER_SUB)
        h8 = h2_ref[pl.ds(base, PEER_SUB), :]
        ys = []
        for t in range(PEER_SUB):
            chunk = lambda s: buf[slot, pl.ds(t * nk * pitch + s, nk, stride=pitch), :]
            gcol = jnp.sum(jnp.where(tok_lane == base + t, gt_ref[...], 0.0), axis=-1, keepdims=True)
            ys.append(_peer_token_mix(chunk, h8[t:t + 1, :], gcol, ch))
        y8 = jnp.concatenate(ys, axis=0)
        o_ref[pl.ds(base, PEER_SUB), :] = x1_ref[pl.ds(base, PEER_SUB), :] + g2 * y8

    first = ids_copy(0, 0)
    first.start()
    first.wait()
    issue_rows(0)
    ids_copy(1, 1).start()

    def pair(jj, _):
        j0 = 2 * jj
        ids_copy(j0 + 1, 1).wait()
        issue_rows(1)

        @pl.when(j0 + 2 < nsub)
        def _():
            ids_copy(j0 + 2, 0).start()

        wait_rows(0)
        compute(0, j0)

        @pl.when(j0 + 2 < nsub)
        def _():
            ids_copy(j0 + 2, 0).wait()
            issue_rows(0)

        @pl.when(j0 + 3 < nsub)
        def _():
            ids_copy(j0 + 3, 1).start()

        wait_rows(1)
        compute(1, j0 + 1)
        return 0

    lax.fori_loop(0, nsub // 2, pair, 0)


def _pack_tables(peer_u, peer_v):
    e, d = peer_u.shape
    ub = lax.bitcast_convert_type(peer_u.astype(BF16), jnp.uint16).astype(jnp.uint32)
    vb = lax.bitcast_convert_type(peer_v.astype(BF16), jnp.uint16).astype(jnp.uint32)
    words = lax.bitcast_convert_type((ub << 16) | vb, jnp.int32)
    return words.reshape(e, d // LANE, LANE)


def _peer(x1, h2, ids, gt, mod_l, row, table, tok0):
    n, d = x1.shape
    nk = gt.shape[0]
    b0 = tok0 // PEER_TB
    nb = n // PEER_TB - b0
    tok = pl.BlockSpec((PEER_TB, d), lambda i: (i + b0, 0))
    any_spec = pl.BlockSpec(memory_space=pl.ANY)
    return pl.pallas_call(
        functools.partial(_peer_kernel, first_block=b0),
        grid=(nb,),
        in_specs=[any_spec,
                  pl.BlockSpec((nk, PEER_TB), lambda i: (0, i + b0)),
                  tok, tok,
                  pl.BlockSpec((1, 1, mod_l.shape[-1]), lambda i: (row, 0, 0)),
                  any_spec],
        out_specs=tok,
        out_shape=jax.ShapeDtypeStruct((n, d), F32),
        scratch_shapes=[pltpu.SMEM((2 * PEER_SUB * nk,), jnp.int32),
                        pltpu.VMEM((2, PEER_SUB * nk * (d // LANE + 1), LANE), jnp.int32),
                        pltpu.SemaphoreType.DMA((2,)),
                        pltpu.SemaphoreType.DMA((2,))],
        compiler_params=_params(("arbitrary",)),
        name="peer",
    )(ids.reshape(n * nk), gt, h2, x1, mod_l, table.reshape(-1, LANE))


def _pad_heads(w, width):
    pad = [(0, 0)] * (w.ndim - 1) + [(0, LANE - width)]
    w = jnp.pad(w, pad)
    return w.reshape(w.shape[:-2] + (HW,))


def _head_gain(g, width):
    depth = g.shape[0]
    g = jnp.pad(g, ((0, 0), (0, LANE - width)))
    return jnp.tile(g, (1, HEADS)).reshape(depth, 1, HW)


def _rope_tables(seq):
    t = np.arange(seq)
    half = MLA_ROPE // 2
    inv = ROPE_THETA ** (-np.arange(0, half, 2, dtype=np.float32) / half)
    cos = np.ones((seq, LANE), np.float32)
    sin = np.zeros((seq, LANE), np.float32)
    for off, pos in ((MLA_NOPE, t // GRID_W), (MLA_NOPE + half, t % GRID_W)):
        ang = pos.astype(np.float32)[:, None] * inv[None, :]
        q = half // 2
        cos[:, off:off + q] = np.cos(ang)
        cos[:, off + q:off + half] = np.cos(ang)
        sin[:, off:off + q] = -np.sin(ang)
        sin[:, off + q:off + half] = np.sin(ang)
    return jnp.asarray(cos), jnp.asarray(sin)


def _nat_bias(rel_bias):
    v = np.arange(WIN_R)[:, None]
    j = np.arange(WIN_R)[None, :]
    dr = j - v + WIN_R - 1
    cq = np.arange(GRID_W)[:, None]
    kc = np.arange(GRID_W)[None, :]
    cstart = np.clip(cq - WIN_C // 2, 0, GRID_W - WIN_C)
    ok = (kc >= cstart) & (kc < cstart + WIN_C)
    dc = np.clip(kc - cq + WIN_C - 1, 0, 2 * WIN_C - 2)
    b = rel_bias[:, :, dr]
    b = b[..., dc]
    b = jnp.where(jnp.asarray(ok)[None, None, None, None], b, NEG_INF)
    b = jnp.transpose(b, (0, 2, 1, 4, 3, 5))
    return b.reshape(b.shape[0], WIN_R, HEADS, GRID_W, WIN_R * GRID_W)


def _layer_weights(w_in, na_q_norm, na_k_norm, mla_cq_norm, mla_ckv_norm, mla_w_uq, mla_w_ukv,
                   mla_q_norm, mla_k_norm, w_out, pool_w, pool_scale, norm1, norm2,
                   peer_wq, peer_subkeys):
    depth, d, _ = w_in.shape
    na_w = HEADS * NA_DH
    segs = np.cumsum([0, na_w, na_w, na_w, 256, 256, 128, MLA_ROPE])
    part = lambda i: w_in[:, :, segs[i]:segs[i + 1]]
    heads = lambda w: _pad_heads(w.reshape(depth, d, HEADS, NA_DH), NA_DH)
    w_in_p = jnp.concatenate(
        [heads(part(0)), heads(part(1)), heads(part(2)), part(3), part(4), part(5),
         jnp.pad(part(6), ((0, 0), (0, 0), (0, LANE - MLA_ROPE)))], axis=-1).astype(BF16)

    w_uq = _pad_heads(mla_w_uq, MLA_QK).astype(BF16)
    k_nope = _pad_heads(mla_w_ukv[..., :MLA_NOPE], MLA_NOPE)
    eye = np.zeros((MLA_ROPE, HEADS, LANE), np.float32)
    for h in range(HEADS):
        eye[np.arange(MLA_ROPE), h, MLA_NOPE + np.arange(MLA_ROPE)] = 1.0
    eye = jnp.broadcast_to(jnp.asarray(eye.reshape(MLA_ROPE, HW)), (depth, MLA_ROPE, HW))
    zer = jnp.zeros((depth, 256 - 128 - MLA_ROPE, HW), F32)
    w_k = jnp.concatenate([k_nope, eye, zer], axis=1).astype(BF16)
    w_v = jnp.concatenate([_pad_heads(mla_w_ukv[..., MLA_NOPE:], MLA_V),
                           jnp.zeros((depth, 128, HW), F32)], axis=1).astype(BF16)

    mix_w = HEADS * NA_DH
    w_o_na = jnp.pad(w_out[:, :mix_w].reshape(depth, HEADS, NA_DH, d),
                     ((0, 0), (0, 0), (0, LANE - NA_DH), (0, 0))).reshape(depth, HW, d).astype(BF16)
    w_o_pool = w_out[:, mix_w:mix_w + 256].astype(BF16)
    w_o_mla = jnp.pad(w_out[:, mix_w + 256:].reshape(depth, HEADS, MLA_V, d),
                      ((0, 0), (0, 0), (0, LANE - MLA_V), (0, 0))).reshape(depth, HW, d).astype(BF16)
    ng = len(POOL_WINDOWS)
    pw = jnp.zeros((depth, ng * POOL_G, ng * POOL_G), F32)
    for g in range(ng):
        pw = pw.at[:, g * POOL_G:(g + 1) * POOL_G, g * POOL_G:(g + 1) * POOL_G].set(pool_w[:, g])

    half = peer_subkeys.shape[-1]
    sk = jnp.stack([jnp.pad(peer_subkeys[:, 0], ((0, 0), (0, 0), (0, LANE - half))),
                    jnp.pad(peer_subkeys[:, 1], ((0, 0), (0, 0), (LANE - half, 0)))], axis=1).astype(BF16)

    return dict(
        w_in=w_in_p, w_uq=w_uq, w_k=w_k, w_v=w_v,
        g_q=_head_gain(na_q_norm, NA_DH), g_k=_head_gain(na_k_norm, NA_DH),
        g_cq=mla_cq_norm[:, None, :], g_ckv=mla_ckv_norm[:, None, :],
        g_qm=_head_gain(mla_q_norm, MLA_QK), g_km=_head_gain(mla_k_norm, MLA_QK),
        w_o_na=w_o_na, w_o_pool=w_o_pool, w_o_mla=w_o_mla,
        pool_w=pw.astype(BF16), pool_scale=pool_scale[:, None, :],
        norm1=norm1[:, None, :], norm2=norm2[:, None, :],
        peer_wq=peer_wq.astype(BF16), peer_sk=sk)


def kernel(x_prompt, x_sample, c, cache_nat_k, cache_nat_v, cache_mla_ckv, cache_mla_krope, c_ctx, w_mod, b_mod, norm1, norm2, w_in, na_q_norm, na_k_norm, na_rel_bias, pool_w, pool_scale, mla_cq_norm, mla_ckv_norm, mla_w_uq, mla_w_ukv, mla_q_norm, mla_k_norm, w_out, peer_wq, peer_subkeys, peer_u, peer_v):
    batch, seq, d = x_prompt.shape
    db, ds, _ = x_sample.shape
    depth = w_mod.shape[0]
    past = cache_nat_k.shape[2]
    assert seq == TB and ds % TB == 0 and ds % (GRID_W * WIN_R) == 0 and db + 1 <= 8

    cond8 = jnp.concatenate([c_ctx[None, :], c, jnp.zeros((8 - 1 - db, d), F32)], axis=0)
    mod = _modulation(cond8, w_mod, b_mod).reshape(depth, 8, 1, 6 * d)

    lw_all = _layer_weights(w_in, na_q_norm, na_k_norm, mla_cq_norm, mla_ckv_norm, mla_w_uq,
                            mla_w_ukv, mla_q_norm, mla_k_norm, w_out, pool_w, pool_scale,
                            norm1, norm2, peer_wq, peer_subkeys)
    bias_all = _nat_bias(na_rel_bias)
    tables = [_pack_tables(peer_u[l], peer_v[l]) for l in range(depth)]
    cos_lat, sin_lat = _rope_tables(ds)
    cos_ctx = jnp.ones((TB, LANE), F32)
    sin_ctx = jnp.zeros((TB, LANE), F32)

    ck = jnp.concatenate([cache_mla_ckv, cache_mla_krope,
                          jnp.zeros(cache_mla_ckv.shape[:-1] + (256 - 128 - MLA_ROPE,), F32)],
                         axis=-1).astype(BF16)
    kc_mla, vc_mla = _cache_kv(ck, lw_all["w_k"], lw_all["w_v"], lw_all["g_km"])
    kc_na = _pad_heads(cache_nat_k, NA_DH).astype(BF16)
    vc_na = _pad_heads(cache_nat_v, NA_DH).astype(BF16)

    xs = [x_prompt.reshape(batch * seq, d)] + [x_sample[b] for b in range(db)]
    one_row = max(batch * seq, ds) + 1
    lat_bpm = ds // TB
    ks, vs, ckvs, krs = [], [], [], []
    pending = None
    after = xs[0]

    def join(item, follow):
        si, x1, y_sc, x2, mod_l = item
        xs[si] = _residual(x1, y_sc, mod_l, si, x2, follow)
        return xs[si]

    for l in range(depth):
        lw = {k: v[l] for k, v in lw_all.items()}
        mod_l = mod[l]
        for si in range(db + 1):
            x = xs[si]
            if si == 0:
                (qn, kn, vn, knf, vnf, p, qm, km, vm, ckv, kr) = _in_proj(
                    x, mod_l, 0, one_row, lw, cos_ctx, sin_ctx, 1, after)
                on, om = _ctx_attn(qn, kn, vn, qm, km, vm, seq)
                x1, h2, ids, gt, gn, h2c = _out_proj(x, on, om, p, mod_l, 0, one_row, lw, seq)
                ks.append(knf.reshape(batch, seq, HEADS, LANE)[..., :NA_DH])
                vs.append(vnf.reshape(batch, seq, HEADS, LANE)[..., :NA_DH])
                ckvs.append(ckv.reshape(batch, seq, 128))
                krs.append(kr.reshape(batch, seq, LANE)[..., :MLA_ROPE])
            else:
                b = si - 1
                (qn, kn, vn, _, _, p, qm, km, vm, _, _) = _in_proj(
                    x, mod_l, si, one_row, lw, cos_lat, sin_lat, lat_bpm, after)
                on = _nat_attn(qn, kn, vn, kc_na[b:b + 1, l], vc_na[b:b + 1, l], bias_all[l], 1)
                om = _lat_mla(qm, km, vm, kc_mla[b:b + 1, l], vc_mla[b:b + 1, l], 1)
                x1, h2, ids, gt, gn, h2c = _out_proj(x, on, om, p, mod_l, si, one_row, lw, ds)
            n_sc = x.shape[0] * SC_SHARE[0] // SC_SHARE[1] // PEER_TB * PEER_TB
            y_sc = _sc_peer(tables[l], ids, gn, h2c, n_sc)
            x2 = _peer(x1, h2, ids, gt, mod_l, si, tables[l], n_sc)
            after = x2 if pending is None else join(pending, x2)
            pending = (si, x1, y_sc, x2, mod_l)
    join(pending, pending[1])

    return (xs[0].reshape(batch, seq, d), jnp.stack(xs[1:], axis=0),
            jnp.stack(ks, axis=1), jnp.stack(vs, axis=1),
            jnp.stack(ckvs, axis=1), jnp.stack(krs, axis=1))
```

```python
import functools

import numpy as np
import jax
import jax.numpy as jnp
from jax import lax
from jax.experimental import pallas as pl
from jax.experimental.pallas import tpu as pltpu
from jax.experimental.pallas import tpu_sc as plsc

F32 = jnp.float32
BF16 = jnp.bfloat16

EPS = 1e-6
ROPE_THETA = 10000.0
NEG_INF = -1e30
GRID_W = 64
HEADS = 6
NA_DH = 64
WIN_R = 8
WIN_C = 16
POOL_WINDOWS = (2, 4, 8, 16)
POOL_G = 64
MLA_NOPE = 64
MLA_ROPE = 32
MLA_QK = MLA_NOPE + MLA_ROPE
MLA_V = 64
PEER_HEADS = 8
PEER_NKEYS = 128
PEER_TOPK = 16
LANE = 128
HW = HEADS * LANE
TB = 256
TQ = 256
PEER_TB = 128
PEER_SUB = 8
VMEM_LIMIT = 56 * 1024 * 1024
SC_SHARE = (23, 32)
ROWS_PER_ITER = 2

_CQ, _CK, _CV = 0, HW, 2 * HW
_CP = 3 * HW
_CCQ = _CP + 256
_CCKV = _CCQ + 256
_CKR = _CCKV + 128
IN_W = _CKR + 128


def _params(sem, vmem=VMEM_LIMIT):
    return pltpu.CompilerParams(dimension_semantics=sem, vmem_limit_bytes=vmem)


def _const_spec(shape):
    n = len(shape)
    return pl.BlockSpec(shape, lambda *_: (0,) * n)


def _nt_dot(a, b):
    return lax.dot_general(a, b, (((1,), (1,)), ((), ())), preferred_element_type=F32)


def _mod_kernel(c_ref, w_ref, b_ref, o_ref):
    c = c_ref[...]
    s = c / (1.0 + jnp.exp(-c))
    o_ref[0] = jnp.dot(s, w_ref[0], preferred_element_type=F32,
                       precision=lax.Precision.HIGHEST) + b_ref[0]


def _modulation(cond8, w_mod, b_mod):
    depth, d, n6 = w_mod.shape
    tn = n6 // 4
    return pl.pallas_call(
        _mod_kernel,
        grid=(depth, n6 // tn),
        in_specs=[_const_spec((8, d)),
                  pl.BlockSpec((1, d, tn), lambda l, j: (l, 0, j)),
                  pl.BlockSpec((1, 1, tn), lambda l, j: (l, 0, j))],
        out_specs=pl.BlockSpec((1, 8, tn), lambda l, j: (l, 0, j)),
        out_shape=jax.ShapeDtypeStruct((depth, 8, n6), F32),
        compiler_params=_params(("arbitrary", "arbitrary")),
        name="modulation",
    )(cond8, w_mod, b_mod.reshape(depth, 1, n6))


def _rms(z, gain):
    return z * lax.rsqrt(jnp.mean(z * z, axis=-1, keepdims=True) + EPS) * gain


def _head_rms(zh, gain_h, n_real):
    ms = jnp.sum(zh * zh, axis=-1, keepdims=True) * (1.0 / n_real)
    return zh * lax.rsqrt(ms + EPS) * gain_h


def _rope(zh, cos, sin, is_x1):
    rot = jnp.where(is_x1, pltpu.roll(zh, LANE - 8, 1), pltpu.roll(zh, 8, 1))
    return zh * cos + rot * sin


def _is_x1(rows):
    lane = lax.broadcasted_iota(jnp.int32, (rows, LANE), 1)
    first = jnp.where(lane >= MLA_NOPE, jnp.where(lane < MLA_NOPE + 8, 1, 0), 0)
    second = jnp.where(lane >= MLA_NOPE + 16, jnp.where(lane < MLA_NOPE + 24, 1, 0), 0)
    return (first + second) > 0


def _mla_kv(ck, wk_ref, wv_ref, gk_ref, cos, sin, km_ref, vm_ref):
    rows = ck.shape[0]
    kk = jnp.dot(ck, wk_ref[...], preferred_element_type=F32)
    is_x1 = _is_x1(rows)
    for h in range(HEADS):
        sl = slice(h * LANE, (h + 1) * LANE)
        kh = _head_rms(kk[:, sl], gk_ref[:, sl], MLA_QK)
        km_ref[:, sl] = _rope(kh, cos, sin, is_x1).astype(BF16)
    vm_ref[...] = jnp.dot(ck, wv_ref[...], preferred_element_type=F32).astype(BF16)


def _in_kernel(x_ref, mod_ref, n1_ref, w_ref, wuq_ref, wk_ref, wv_ref,
               gq_ref, gk_ref, gcq_ref, gckv_ref, gqm_ref, gkm_ref, cos_ref, sin_ref, after_hbm,
               qn_ref, kn_ref, vn_ref, knf_ref, vnf_ref, p_ref,
               qm_ref, km_ref, vm_ref, ckv_ref, kr_ref):
    del after_hbm
    d = x_ref.shape[1]
    rows = x_ref.shape[0]
    mod = mod_ref[0]
    sh1 = mod[:, 0:d]
    sc1 = mod[:, d:2 * d]
    h = _rms(x_ref[...], n1_ref[...]) * (1.0 + sc1) + sh1
    hb = h.astype(BF16)

    def proj(lo, hi):
        return jnp.dot(hb, w_ref[:, lo:hi], preferred_element_type=F32)

    cos = cos_ref[...]
    sin = sin_ref[...]
    is_x1 = _is_x1(rows)

    zq = proj(_CQ, _CQ + HW)
    zk = proj(_CK, _CK + HW)
    for hh in range(HEADS):
        sl = slice(hh * LANE, (hh + 1) * LANE)
        qn_ref[:, sl] = (_head_rms(zq[:, sl], gq_ref[:, sl], NA_DH) * (NA_DH ** -0.5)).astype(BF16)
        kh = _head_rms(zk[:, sl], gk_ref[:, sl], NA_DH)
        knf_ref[:, sl] = kh
        kn_ref[:, sl] = kh.astype(BF16)
    zv = proj(_CV, _CV + HW)
    vnf_ref[...] = zv
    vn_ref[...] = zv.astype(BF16)
    p_ref[...] = proj(_CP, _CP + 256)

    cq = _rms(proj(_CCQ, _CCQ + 256), gcq_ref[...])
    zqm = jnp.dot(cq.astype(BF16), wuq_ref[...], preferred_element_type=F32)
    for hh in range(HEADS):
        sl = slice(hh * LANE, (hh + 1) * LANE)
        qh = _head_rms(zqm[:, sl], gqm_ref[:, sl], MLA_QK)
        qm_ref[:, sl] = (_rope(qh, cos, sin, is_x1) * (MLA_QK ** -0.5)).astype(BF16)

    ckv = _rms(proj(_CCKV, _CCKV + 128), gckv_ref[...])
    kr = proj(_CKR, _CKR + 128)
    ckv_ref[...] = ckv
    kr_ref[...] = kr
    ck = jnp.concatenate([ckv, kr], axis=-1).astype(BF16)
    _mla_kv(ck, wk_ref, wv_ref, gkm_ref, cos, sin, km_ref, vm_ref)


def _in_proj(x, mod_l, row_off, bpm, lw, cos_t, sin_t, rope_blocks, after):
    n, d = x.shape
    nb = n // TB
    tok = lambda w: pl.BlockSpec((TB, w), lambda i: (i, 0))
    rope_spec = pl.BlockSpec((TB, LANE), lambda i: (i % rope_blocks, 0))
    in_specs = [tok(d),
                pl.BlockSpec((1, 1, mod_l.shape[-1]), lambda i: (row_off + i // bpm, 0, 0)),
                _const_spec((1, d)), _const_spec((d, IN_W)), _const_spec((256, HW)),
                _const_spec((256, HW)), _const_spec((256, HW)),
                _const_spec((1, HW)), _const_spec((1, HW)), _const_spec((1, 256)),
                _const_spec((1, 128)), _const_spec((1, HW)), _const_spec((1, HW)),
                rope_spec, rope_spec, pl.BlockSpec(memory_space=pl.ANY)]
    widths = [(HW, BF16), (HW, BF16), (HW, BF16), (HW, F32), (HW, F32), (256, F32),
              (HW, BF16), (HW, BF16), (HW, BF16), (128, F32), (128, F32)]
    return pl.pallas_call(
        _in_kernel,
        grid=(nb,),
        in_specs=in_specs,
        out_specs=[tok(w) for w, _ in widths],
        out_shape=[jax.ShapeDtypeStruct((n, w), dt) for w, dt in widths],
        compiler_params=_params(("arbitrary",)),
        name="in_proj",
    )(x, mod_l, lw["norm1"], lw["w_in"], lw["w_uq"], lw["w_k"], lw["w_v"],
      lw["g_q"], lw["g_k"], lw["g_cq"], lw["g_ckv"], lw["g_qm"], lw["g_km"], cos_t, sin_t, after)


def _cache_kernel(ck_ref, wk_ref, wv_ref, gk_ref, km_ref, vm_ref):
    rows = ck_ref.shape[2]
    cos = jnp.ones((rows, LANE), F32)
    sin = jnp.zeros((rows, LANE), F32)
    _mla_kv(ck_ref[0, 0], wk_ref.at[0], wv_ref.at[0], gk_ref.at[0], cos, sin,
            km_ref.at[0, 0], vm_ref.at[0, 0])


def _cache_kv(ck, w_k, w_v, g_km):
    db, depth, p, _ = ck.shape
    spec = lambda w: pl.BlockSpec((1, 1, p, w), lambda b, l: (b, l, 0, 0))
    wspec = lambda r: pl.BlockSpec((1, r, HW), lambda b, l: (l, 0, 0))
    return pl.pallas_call(
        _cache_kernel,
        grid=(db, depth),
        in_specs=[spec(256), wspec(256), wspec(256), wspec(1)],
        out_specs=[spec(HW), spec(HW)],
        out_shape=[jax.ShapeDtypeStruct((db, depth, p, HW), BF16)] * 2,
        compiler_params=_params(("arbitrary", "arbitrary")),
        name="cache_kv",
    )(ck, w_k, w_v, g_km)


def _softmax_av(s_list, v_list):
    m = s_list[0].max(axis=-1, keepdims=True)
    for s in s_list[1:]:
        m = jnp.maximum(m, s.max(axis=-1, keepdims=True))
    acc = None
    den = None
    for s, v in zip(s_list, v_list):
        p = jnp.exp(s - m)
        l = p.sum(axis=-1, keepdims=True)
        o = jnp.dot(p.astype(BF16), v, preferred_element_type=F32)
        acc = o if acc is None else acc + o
        den = l if den is None else den + l
    return acc / den


def _ctx_attn_kernel(qn, kn, vn, qm, km, vm, on, om):
    for q, k, v, o in ((qn, kn, vn, on), (qm, km, vm, om)):
        for h in range(HEADS):
            sl = slice(h * LANE, (h + 1) * LANE)
            s = _nt_dot(q[:, sl], k[:, sl])
            o[:, sl] = _softmax_av([s], [v[:, sl]]).astype(BF16)


def _ctx_attn(qn, kn, vn, qm, km, vm, seq):
    n = qn.shape[0]
    spec = pl.BlockSpec((seq, HW), lambda i: (i, 0))
    return pl.pallas_call(
        _ctx_attn_kernel,
        grid=(n // seq,),
        in_specs=[spec] * 6,
        out_specs=[spec] * 2,
        out_shape=[jax.ShapeDtypeStruct((n, HW), BF16)] * 2,
        compiler_params=_params(("arbitrary",)),
        name="ctx_attn",
    )(qn, kn, vn, qm, km, vm)


def _lat_mla_kernel(q, k, v, kc, vc, o):
    s1 = _nt_dot(q[...], k[...])
    s2 = _nt_dot(q[...], kc[0])
    o[...] = _softmax_av([s1, s2], [v[...], vc[0]]).astype(BF16)


def _lat_mla(qm, km, vm, kc, vc, db):
    n = qm.shape[0]
    ds = n // db
    nq = ds // TQ
    qspec = pl.BlockSpec((TQ, LANE), lambda b, h, i: (b * nq + i, h))
    kspec = pl.BlockSpec((ds, LANE), lambda b, h, i: (b, h))
    cspec = pl.BlockSpec((1, kc.shape[1], LANE), lambda b, h, i: (b, 0, h))
    return pl.pallas_call(
        _lat_mla_kernel,
        grid=(db, HEADS, nq),
        in_specs=[qspec, kspec, kspec, cspec, cspec],
        out_specs=qspec,
        out_shape=jax.ShapeDtypeStruct((n, HW), BF16),
        compiler_params=_params(("arbitrary",) * 3),
        name="lat_mla",
    )(qm, km, vm, kc, vc)


def _nat_kernel(q, k, v, kc, vc, bias, o, *, rows):
    r = pl.program_id(1)
    rs = jnp.clip(r - WIN_R // 2, 0, rows - WIN_R)
    start = pl.multiple_of(rs * GRID_W, GRID_W)
    band = WIN_R * GRID_W
    for h in range(HEADS):
        sl = slice(h * LANE, (h + 1) * LANE)
        qh = q[:, sl]
        s1 = _nt_dot(qh, k[pl.ds(start, band), sl]) + bias[0, h]
        s2 = _nt_dot(qh, kc[0, :, sl])
        o[:, sl] = _softmax_av([s1, s2], [v[pl.ds(start, band), sl], vc[0, :, sl]]).astype(BF16)


def _nat_attn(qn, kn, vn, kc, vc, bias, db):
    n = qn.shape[0]
    ds = n // db
    rows = ds // GRID_W
    band = WIN_R * GRID_W

    def variant(r):
        return jnp.where(r < WIN_R // 2, r, jnp.where(r > rows - WIN_R // 2, r - (rows - WIN_R), WIN_R // 2))

    qspec = pl.BlockSpec((GRID_W, HW), lambda b, r: (b * rows + r, 0))
    kspec = pl.BlockSpec((ds, HW), lambda b, r: (b, 0))
    cspec = pl.BlockSpec((1, kc.shape[1], HW), lambda b, r: (b, 0, 0))
    bspec = pl.BlockSpec((1, HEADS, GRID_W, band), lambda b, r: (variant(r), 0, 0, 0))
    return pl.pallas_call(
        functools.partial(_nat_kernel, rows=rows),
        grid=(db, rows),
        in_specs=[qspec, kspec, kspec, cspec, cspec, bspec],
        out_specs=qspec,
        out_shape=jax.ShapeDtypeStruct((n, HW), BF16),
        compiler_params=_params(("arbitrary", "arbitrary")),
        name="nat_attn",
    )(qn, kn, vn, kc, vc, bias)


def _split3(x):
    hi = x.astype(BF16)
    r = x - hi.astype(F32)
    mid = r.astype(BF16)
    lo = (r - mid.astype(F32)).astype(BF16)
    return hi, mid, lo


def _pool(p_prev, p_cur, p_next, posb, seq_len):
    rows = p_cur.shape[0]
    halo = p_prev.shape[0]
    ext = rows + 2 * halo
    pext = jnp.concatenate([p_prev, p_cur, p_next], axis=0)
    parts = _split3(pext)
    t = posb + lax.broadcasted_iota(jnp.int32, (rows, ext), 0)
    s = posb - halo + lax.broadcasted_iota(jnp.int32, (rows, ext), 1)
    tcol = posb + lax.broadcasted_iota(jnp.int32, (rows, 1), 0)
    grp = lax.broadcasted_iota(jnp.int32, (rows, 256), 1) // POOL_G
    d = jnp.zeros((rows, 256), F32)
    for gi, w in enumerate(POOL_WINDOWS):
        lo = jnp.maximum(t - w // 2, 0)
        hi = jnp.minimum(t + (w - w // 2), seq_len)
        sel = jnp.where(s >= lo, jnp.where(s < hi, 1.0, 0.0), 0.0).astype(BF16)
        tot = sum(jnp.dot(sel, part, preferred_element_type=F32) for part in parts)
        cnt = (jnp.minimum(tcol + (w - w // 2), seq_len) - jnp.maximum(tcol - w // 2, 0)).astype(F32)
        d = jnp.where(grp == gi, tot / cnt - p_cur, d)
    return d


def _first_max(x, pos, sentinel):
    m = jnp.max(x, axis=0, keepdims=True)
    idx = jnp.min(jnp.where(x == m, pos, sentinel), axis=0, keepdims=True)
    return m, idx


def _topk_stage1(qh, sk_ref):
    c = qh.shape[0]
    key_pos = lax.broadcasted_iota(jnp.int32, (PEER_NKEYS, c), 0).astype(F32)
    row16 = lax.broadcasted_iota(jnp.int32, (PEER_TOPK, c), 0)
    neg = jnp.float32(-jnp.inf)
    s0 = _nt_dot(sk_ref[0], qh)
    s1 = _nt_dot(sk_ref[1], qh)

    def stage1(a, carry):
        out = []
        for s, sv, si in (carry[0:3], carry[3:6]):
            m, idx = _first_max(s, key_pos, float(PEER_NKEYS))
            out += [jnp.where(key_pos == idx, neg, s),
                    jnp.where(row16 == a, m, sv), jnp.where(row16 == a, idx, si)]
        return tuple(out)

    zf = jnp.zeros((PEER_TOPK, c), F32)
    _, sv0, si0, _, sv1, si1 = lax.fori_loop(0, PEER_TOPK, stage1, (s0, zf, zf, s1, zf, zf))
    return sv0, si0, sv1, si1


def _topk_pieces(sv0, sv1):
    c = sv0.shape[1]
    neg = jnp.float32(-jnp.inf)
    sub8 = lax.broadcasted_iota(jnp.int32, (8, c), 0)
    sub8f = sub8.astype(F32)
    cs, cf = [], []
    for a in range(8):
        nb = PEER_TOPK // (a + 1)
        for b0 in range(0, nb, 8):
            val = sv0[a:a + 1] + sv1[b0:b0 + 8]
            if nb - b0 < 8:
                val = jnp.where(sub8 < nb - b0, val, neg)
            cs.append(val)
            cf.append(sub8f + float(a * PEER_TOPK + b0))
    cs.append(sv0[8:16] + sv1[0:1])
    cf.append((sub8f + 8.0) * float(PEER_TOPK))
    return cs, cf


def _topk_stage2(chains, cf):
    npc = len(cf)
    c = cf[0].shape[1]
    row16 = lax.broadcasted_iota(jnp.int32, (PEER_TOPK, c), 0)
    neg = jnp.float32(-jnp.inf)
    nflat = float(PEER_TOPK * PEER_TOPK)
    zf = jnp.zeros((PEER_TOPK, c), F32)

    def step(k, carry):
        out = []
        for ch in range(len(chains)):
            vals = carry[ch * (npc + 2):ch * (npc + 2) + npc]
            tv, tp = carry[ch * (npc + 2) + npc], carry[ch * (npc + 2) + npc + 1]
            m = vals[0]
            for v in vals[1:]:
                m = jnp.maximum(m, v)
            m = jnp.max(m, axis=0, keepdims=True)
            pos = None
            for v, f in zip(vals, cf):
                cand = jnp.where(v == m, f, nflat)
                pos = cand if pos is None else jnp.minimum(pos, cand)
            pos = jnp.min(pos, axis=0, keepdims=True)
            out += [jnp.where(f == pos, neg, v) for v, f in zip(vals, cf)]
            out += [jnp.where(row16 == k, m, tv), jnp.where(row16 == k, pos, tp)]
        return tuple(out)

    init = []
    for cs in chains:
        init += list(cs) + [zf, zf]
    res = lax.fori_loop(0, PEER_TOPK, step, tuple(init))
    return [(res[ch * (npc + 2) + npc], res[ch * (npc + 2) + npc + 1]) for ch in range(len(chains))]


def _topk_ids(tp, si0, si1):
    a = jnp.floor(tp * (1.0 / PEER_TOPK))
    b = tp - a * float(PEER_TOPK)
    ea = jnp.zeros_like(tp)
    eb = jnp.zeros_like(tp)
    for j in range(PEER_TOPK):
        ea = jnp.where(a == float(j), si0[j:j + 1], ea)
        eb = jnp.where(b == float(j), si1[j:j + 1], eb)
    return ea * float(PEER_NKEYS) + eb


def _out_kernel(on_ref, om_ref, pc_ref, pp_ref, pn_ref, x_ref, mod_ref,
                won_ref, wop_ref, wom_ref, pw_ref, ps_ref, n2_ref, wq_ref, sk_ref,
                x1_ref, h2_ref, ids_ref, gt_ref, gn_ref, h2c_ref, q_scr, idt_scr, *, bps, seq_len):
    d = x_ref.shape[1]
    rows = x_ref.shape[0]
    i = pl.program_id(0)
    mod = mod_ref[0]
    g1 = mod[:, 2 * d:3 * d]
    sh2 = mod[:, 3 * d:4 * d]
    sc2 = mod[:, 4 * d:5 * d]

    posb = (i % bps) * rows
    dpool = _pool(pp_ref[...], pc_ref[...], pn_ref[...], posb, seq_len)
    ypool = jnp.dot(dpool.astype(BF16), pw_ref[...], preferred_element_type=F32) * ps_ref[...]
    mix = (jnp.dot(on_ref[...], won_ref[...], preferred_element_type=F32)
           + jnp.dot(ypool.astype(BF16), wop_ref[...], preferred_element_type=F32)
           + jnp.dot(om_ref[...], wom_ref[...], preferred_element_type=F32))
    x1 = x_ref[...] + g1 * mix
    x1_ref[...] = x1
    h2 = _rms(x1, n2_ref[...]) * (1.0 + sc2) + sh2
    h2_ref[...] = h2
    for j in range(d // LANE):
        h2c_ref[j] = h2[:, j * LANE:(j + 1) * LANE]

    q = jnp.dot(h2.astype(BF16), wq_ref[...], preferred_element_type=F32)
    for hh in range(PEER_HEADS):
        q_scr[hh] = q[:, hh * LANE:(hh + 1) * LANE].astype(BF16)

    chunks = range(0, rows, LANE)

    def head(hh, _):
        sorted_keys = [_topk_stage1(q_scr[hh, c0:c0 + LANE, :], sk_ref) for c0 in chunks]
        pieces = [_topk_pieces(sv0, sv1) for sv0, _, sv1, _ in sorted_keys]
        picked = _topk_stage2([cs for cs, _ in pieces], pieces[0][1])
        r0 = pl.multiple_of(hh * PEER_TOPK, PEER_TOPK)
        for c0, (tv, tp), (_, si0, _, si1) in zip(chunks, picked, sorted_keys):
            ex = jnp.exp(tv - tv[0:1])
            gt_ref[pl.ds(r0, PEER_TOPK), c0:c0 + LANE] = ex / jnp.sum(ex, axis=0, keepdims=True)
            idt_scr[pl.ds(r0, PEER_TOPK), c0:c0 + LANE] = _topk_ids(tp, si0, si1)
        return 0

    lax.fori_loop(0, PEER_HEADS, head, 0)
    ids_ref[...] = idt_scr[...].T.astype(jnp.int32)
    gn_ref[...] = gt_ref[...].T


def _out_proj(x, on, om, p, mod_l, row_off, bpm, lw, seq_len):
    n, d = x.shape
    nb = n // TB
    bps = seq_len // TB
    halo = 8
    hb = TB // halo
    tok = lambda w: pl.BlockSpec((TB, w), lambda i: (i, 0))
    in_specs = [tok(HW), tok(HW), tok(256),
                pl.BlockSpec((halo, 256), lambda i: (jnp.maximum(i * hb - 1, 0), 0)),
                pl.BlockSpec((halo, 256), lambda i: (jnp.minimum((i + 1) * hb, n // halo - 1), 0)),
                tok(d),
                pl.BlockSpec((1, 1, mod_l.shape[-1]), lambda i: (row_off + i // bpm, 0, 0)),
                _const_spec((HW, d)), _const_spec((256, d)), _const_spec((HW, d)),
                _const_spec((256, 256)), _const_spec((1, 256)), _const_spec((1, d)),
                _const_spec((d, PEER_HEADS * LANE)), _const_spec((2, PEER_NKEYS, LANE))]
    nk = PEER_HEADS * PEER_TOPK
    return pl.pallas_call(
        functools.partial(_out_kernel, bps=bps, seq_len=seq_len),
        grid=(nb,),
        in_specs=in_specs,
        out_specs=[tok(d), tok(d), tok(nk), pl.BlockSpec((nk, TB), lambda i: (0, i)), tok(nk),
                   pl.BlockSpec((d // LANE, TB, LANE), lambda i: (0, i, 0))],
        out_shape=[jax.ShapeDtypeStruct((n, d), F32), jax.ShapeDtypeStruct((n, d), F32),
                   jax.ShapeDtypeStruct((n, nk), jnp.int32), jax.ShapeDtypeStruct((nk, n), F32),
                   jax.ShapeDtypeStruct((n, nk), F32),
                   jax.ShapeDtypeStruct((d // LANE, n, LANE), F32)],
        scratch_shapes=[pltpu.VMEM((PEER_HEADS, TB, LANE), BF16), pltpu.VMEM((nk, TB), F32)],
        compiler_params=_params(("arbitrary",)),
        name="out_proj",
    )(on, om, p, p, p, x, mod_l, lw["w_o_na"], lw["w_o_pool"], lw["w_o_mla"],
      lw["pool_w"], lw["pool_scale"], lw["norm2"], lw["peer_wq"], lw["peer_sk"])


def _gelu_tanh(x):
    return x * (0.5 * (1.0 + jnp.tanh(0.7978845608028654 * (x + 0.044715 * (x * x * x)))))


def _peer_token_mix(chunk, hrow, gcol, ch):
    acc = None
    for s in range(ch):
        us = lax.bitcast_convert_type(chunk(s) & jnp.int32(-65536), F32)
        term = us * hrow[:, s * LANE:(s + 1) * LANE]
        acc = term if acc is None else acc + term
    wgt = gcol * _gelu_tanh(jnp.sum(acc, axis=-1, keepdims=True))
    parts = []
    for s in range(ch):
        vs = lax.bitcast_convert_type(chunk(s) << 16, F32)
        parts.append(jnp.sum(vs * wgt, axis=0, keepdims=True))
    return jnp.concatenate(parts, axis=-1)


def _sc_peer(table3, ids, gates, h2c, n):
    ch, _, lane = h2c.shape
    nk = ids.shape[1]
    info = plsc.get_sparse_core_info()
    nc, nw, nl = info.num_cores, info.num_cores * info.num_subcores, info.num_lanes
    tpw = n // nw
    win = 32
    nq = nk // win
    cpr = lane // nl
    nchunk = ch * cpr
    hc = nchunk // 2
    assert n % nw == 0 and nk % win == 0 and win % nl == 0
    mesh = plsc.VectorSubcoreMesh(core_axis_name="core", subcore_axis_name="subcore")
    hi_mask = jnp.int32(-65536)

    @functools.partial(
        pl.kernel, mesh=mesh,
        out_type=jax.ShapeDtypeStruct((ch, n, lane), F32),
        compiler_params=pltpu.CompilerParams(needs_layout_passes=False),
        scratch_types=[pltpu.VMEM((nk,), jnp.int32), pltpu.VMEM((nk,), F32),
                       pltpu.VMEM((ch, lane), F32), pltpu.VMEM((ch, lane), F32),
                       pltpu.VMEM((win, ch, lane), jnp.int32), pltpu.VMEM((win, ch, lane), jnp.int32),
                       pltpu.VMEM((win * nl,), F32), pltpu.VMEM((win,), F32),
                       pltpu.SemaphoreType.DMA, pltpu.SemaphoreType.DMA, pltpu.SemaphoreType.DMA])
    def peer(tab_hbm, ids_hbm, g_hbm, h2_hbm, y_hbm,
             idx_v, g_v, x_v, y_v, rows_a, rows_b, part_v, w_v, sem_a, sem_b, sem_x):
        wid = lax.axis_index("subcore") * nc + lax.axis_index("core")
        bufs = ((rows_a, sem_a), (rows_b, sem_b))
        lanes = lax.iota(jnp.int32, nl)
        zero = jnp.zeros((nl,), F32)

        def chunk_copies(tok, to_hbm):
            if to_hbm:
                return [pltpu.make_async_copy(y_v.at[j], y_hbm.at[j, tok], sem_x) for j in range(ch)]
            return [pltpu.make_async_copy(h2_hbm.at[j, tok], x_v.at[j], sem_x) for j in range(ch)]

        def fetch(q):
            rows, sem = bufs[q % 2]
            return pltpu.make_async_copy(tab_hbm.at[idx_v.at[pl.ds(q * win, win)]], rows, sem)

        def word(rows, r, cc):
            return rows[r, cc // cpr, pl.ds((cc % cpr) * nl, nl)]

        @pl.loop(0, tpw)
        def _(ti):
            tok = wid * tpw + ti
            loads = chunk_copies(tok, False)
            for cp in loads:
                cp.start()
            pltpu.sync_copy(ids_hbm.at[tok], idx_v)
            pltpu.sync_copy(g_hbm.at[tok], g_v)
            for cp in loads:
                cp.wait()
            for cc in range(nchunk):
                y_v[cc // cpr, pl.ds((cc % cpr) * nl, nl)] = zero
            fetch(0).start()
            for q in range(nq):
                rows = bufs[q % 2][0]
                fetch(q).wait()
                if q + 1 < nq:
                    fetch(q + 1).start()

                for half in range(2):
                    xs = [x_v[(half * hc + c) // cpr, pl.ds(((half * hc + c) % cpr) * nl, nl)]
                          for c in range(hc)]

                    @pl.loop(0, win, step=ROWS_PER_ITER)
                    def _(r0):
                        accs = [[None] * 4 for _ in range(ROWS_PER_ITER)]
                        for c in range(hc):
                            for k in range(ROWS_PER_ITER):
                                u = lax.bitcast_convert_type(
                                    word(rows, r0 + k, half * hc + c) & hi_mask, F32)
                                t = u * xs[c]
                                accs[k][c % 4] = t if accs[k][c % 4] is None else accs[k][c % 4] + t
                        for k in range(ROWS_PER_ITER):
                            acc = (accs[k][0] + accs[k][1]) + (accs[k][2] + accs[k][3])
                            po = pl.multiple_of((r0 + k) * nl, nl)
                            if half == 0:
                                part_v[pl.ds(po, nl)] = acc
                            else:
                                part_v[pl.ds(po, nl)] = part_v[pl.ds(po, nl)] + acc

                for grp in range(win // nl):
                    s = zero
                    for rr in range(nl):
                        tot = jnp.sum(part_v[pl.ds((grp * nl + rr) * nl, nl)])
                        s = jnp.where(lanes == rr, tot, s)
                    z = 0.7978845608028654 * (s + 0.044715 * (s * s * s))
                    tanh = 1.0 - 2.0 / (jnp.exp(2.0 * z) + 1.0)
                    gate = g_v[pl.ds(q * win + grp * nl, nl)]
                    w_v[pl.ds(grp * nl, nl)] = gate * (s * (0.5 * (1.0 + tanh)))

                for half in range(2):
                    def body(i, yacc):
                        r0 = i * ROWS_PER_ITER
                        wr = [plsc.load_gather(w_v, [jnp.full((nl,), r0 + k, jnp.int32)])
                              for k in range(ROWS_PER_ITER)]
                        out = []
                        for c in range(hc):
                            y = yacc[c]
                            for k in range(ROWS_PER_ITER):
                                v = lax.bitcast_convert_type(
                                    word(rows, r0 + k, half * hc + c) << 16, F32)
                                y = y + wr[k] * v
                            out.append(y)
                        return tuple(out)

                    yacc = lax.fori_loop(0, win // ROWS_PER_ITER, body,
                                         tuple(zero for _ in range(hc)))
                    for c in range(hc):
                        cc = half * hc + c
                        sl = (cc // cpr, pl.ds((cc % cpr) * nl, nl))
                        y_v[sl] = y_v[sl] + yacc[c]
            stores = chunk_copies(tok, True)
            for cp in stores:
                cp.start()
            for cp in stores:
                cp.wait()

    return peer(table3, ids, gates, h2c)


def _residual_kernel(x1_ref, y_ref, mod_ref, x2_hbm, after_hbm, o_ref):
    del x2_hbm
    del after_hbm
    d = x1_ref.shape[1]
    g2 = mod_ref[0][:, 5 * d:6 * d]
    for j in range(d // LANE):
        sl = slice(j * LANE, (j + 1) * LANE)
        o_ref[:, sl] = x1_ref[:, sl] + g2[:, sl] * y_ref[j]


def _residual(x1, y, mod_l, row, x2, after):
    n, d = x1.shape
    tok = pl.BlockSpec((PEER_TB, d), lambda i: (i, 0))
    any_spec = pl.BlockSpec(memory_space=pl.ANY)
    return pl.pallas_call(
        _residual_kernel,
        grid=(y.shape[1] // PEER_TB,),
        in_specs=[tok, pl.BlockSpec((d // LANE, PEER_TB, LANE), lambda i: (0, i, 0)),
                  pl.BlockSpec((1, 1, mod_l.shape[-1]), lambda i: (row, 0, 0)),
                  any_spec, any_spec],
        out_specs=tok,
        out_shape=jax.ShapeDtypeStruct((n, d), F32),
        input_output_aliases={3: 0},
        compiler_params=_params(("arbitrary",)),
        name="residual",
    )(x1, y, mod_l, x2, after)


def _peer_kernel(ids_hbm, gt_ref, h2_ref, x1_ref, mod_ref, tab_hbm, o_ref,
                 ids_s, buf, sem_i, sem_r, *, first_block):
    d = x1_ref.shape[1]
    ch = d // LANE
    pitch = ch + 1
    nsub = x1_ref.shape[0] // PEER_SUB
    nk = gt_ref.shape[0]
    nids = PEER_SUB * nk
    i = pl.program_id(0) + first_block
    g2 = mod_ref[0][:, 5 * d:6 * d]
    tok_lane = lax.broadcasted_iota(jnp.int32, gt_ref.shape, 1)

    def ids_copy(j, slot):
        start = pl.multiple_of((i * nsub + j) * nids, nids)
        return pltpu.make_async_copy(ids_hbm.at[pl.ds(start, nids)],
                                     ids_s.at[pl.ds(slot * nids, nids)], sem_i.at[slot])

    def row_copy(slot, e, f):
        src = tab_hbm.at[pl.ds(pl.multiple_of(e * ch, ch), ch), :]
        dst = buf.at[slot, pl.ds(f * pitch, ch), :]
        return pltpu.make_async_copy(src, dst, sem_r.at[slot])

    def issue_rows(slot):
        for t in range(PEER_SUB):
            def body(kk, _):
                for r in range(8):
                    f = t * nk + kk * 8 + r
                    row_copy(slot, ids_s[slot * nids + f], f).start(priority=r % 2)
                return 0

            lax.fori_loop(0, nk // 8, body, 0)

    def wait_rows(slot):
        done = buf.at[slot, pl.ds(0, nids * ch), :]
        pltpu.make_async_copy(done, done, sem_r.at[slot]).wait()

    def compute(slot, j):
        base = pl.multiple_of(j * PEER_SUB, PEER_SUB)
        h8 = h2_ref[pl.ds(base, PEER_SUB), :]
        ys = []
        for t in range(PEER_SUB):
            chunk = lambda s: buf[slot, pl.ds(t * nk * pitch + s, nk, stride=pitch), :]
            gcol = jnp.sum(jnp.where(tok_lane == base + t, gt_ref[...], 0.0), axis=-1, keepdims=True)
            ys.append(_peer_token_mix(chunk, h8[t:t + 1, :], gcol, ch))
        y8 = jnp.concatenate(ys, axis=0)
        o_ref[pl.ds(base, PEER_SUB), :] = x1_ref[pl.ds(base, PEER_SUB), :] + g2 * y8

    first = ids_copy(0, 0)
    first.start()
    first.wait()
    issue_rows(0)
    ids_copy(1, 1).start()

    def pair(jj, _):
        j0 = 2 * jj
        ids_copy(j0 + 1, 1).wait()
        issue_rows(1)

        @pl.when(j0 + 2 < nsub)
        def _():
            ids_copy(j0 + 2, 0).start()

        wait_rows(0)
        compute(0, j0)

        @pl.when(j0 + 2 < nsub)
        def _():
            ids_copy(j0 + 2, 0).wait()
            issue_rows(0)

        @pl.when(j0 + 3 < nsub)
        def _():
            ids_copy(j0 + 3, 1).start()

        wait_rows(1)
        compute(1, j0 + 1)
        return 0

    lax.fori_loop(0, nsub // 2, pair, 0)


def _pack_tables(peer_u, peer_v):
    e, d = peer_u.shape
    ub = lax.bitcast_convert_type(peer_u.astype(BF16), jnp.uint16).astype(jnp.uint32)
    vb = lax.bitcast_convert_type(peer_v.astype(BF16), jnp.uint16).astype(jnp.uint32)
    words = lax.bitcast_convert_type((ub << 16) | vb, jnp.int32)
    return words.reshape(e, d // LANE, LANE)


def _peer(x1, h2, ids, gt, mod_l, row, table, tok0):
    n, d = x1.shape
    nk = gt.shape[0]
    b0 = tok0 // PEER_TB
    nb = n // PEER_TB - b0
    tok = pl.BlockSpec((PEER_TB, d), lambda i: (i + b0, 0))
    any_spec = pl.BlockSpec(memory_space=pl.ANY)
    return pl.pallas_call(
        functools.partial(_peer_kernel, first_block=b0),
        grid=(nb,),
        in_specs=[any_spec,
                  pl.BlockSpec((nk, PEER_TB), lambda i: (0, i + b0)),
                  tok, tok,
                  pl.BlockSpec((1, 1, mod_l.shape[-1]), lambda i: (row, 0, 0)),
                  any_spec],
        out_specs=tok,
        out_shape=jax.ShapeDtypeStruct((n, d), F32),
        scratch_shapes=[pltpu.SMEM((2 * PEER_SUB * nk,), jnp.int32),
                        pltpu.VMEM((2, PEER_SUB * nk * (d // LANE + 1), LANE), jnp.int32),
                        pltpu.SemaphoreType.DMA((2,)),
                        pltpu.SemaphoreType.DMA((2,))],
        compiler_params=_params(("arbitrary",)),
        name="peer",
    )(ids.reshape(n * nk), gt, h2, x1, mod_l, table.reshape(-1, LANE))


def _pad_heads(w, width):
    pad = [(0, 0)] * (w.ndim - 1) + [(0, LANE - width)]
    w = jnp.pad(w, pad)
    return w.reshape(w.shape[:-2] + (HW,))


def _head_gain(g, width):
    depth = g.shape[0]
    g = jnp.pad(g, ((0, 0), (0, LANE - width)))
    return jnp.tile(g, (1, HEADS)).reshape(depth, 1, HW)


def _rope_tables(seq):
    t = np.arange(seq)
    half = MLA_ROPE // 2
    inv = ROPE_THETA ** (-np.arange(0, half, 2, dtype=np.float32) / half)
    cos = np.ones((seq, LANE), np.float32)
    sin = np.zeros((seq, LANE), np.float32)
    for off, pos in ((MLA_NOPE, t // GRID_W), (MLA_NOPE + half, t % GRID_W)):
        ang = pos.astype(np.float32)[:, None] * inv[None, :]
        q = half // 2
        cos[:, off:off + q] = np.cos(ang)
        cos[:, off + q:off + half] = np.cos(ang)
        sin[:, off:off + q] = -np.sin(ang)
        sin[:, off + q:off + half] = np.sin(ang)
    return jnp.asarray(cos), jnp.asarray(sin)


def _nat_bias(rel_bias):
    v = np.arange(WIN_R)[:, None]
    j = np.arange(WIN_R)[None, :]
    dr = j - v + WIN_R - 1
    cq = np.arange(GRID_W)[:, None]
    kc = np.arange(GRID_W)[None, :]
    cstart = np.clip(cq - WIN_C // 2, 0, GRID_W - WIN_C)
    ok = (kc >= cstart) & (kc < cstart + WIN_C)
    dc = np.clip(kc - cq + WIN_C - 1, 0, 2 * WIN_C - 2)
    b = rel_bias[:, :, dr]
    b = b[..., dc]
    b = jnp.where(jnp.asarray(ok)[None, None, None, None], b, NEG_INF)
    b = jnp.transpose(b, (0, 2, 1, 4, 3, 5))
    return b.reshape(b.shape[0], WIN_R, HEADS, GRID_W, WIN_R * GRID_W)


def _layer_weights(w_in, na_q_norm, na_k_norm, mla_cq_norm, mla_ckv_norm, mla_w_uq, mla_w_ukv,
                   mla_q_norm, mla_k_norm, w_out, pool_w, pool_scale, norm1, norm2,
                   peer_wq, peer_subkeys):
    depth, d, _ = w_in.shape
    na_w = HEADS * NA_DH
    segs = np.cumsum([0, na_w, na_w, na_w, 256, 256, 128, MLA_ROPE])
    part = lambda i: w_in[:, :, segs[i]:segs[i + 1]]
    heads = lambda w: _pad_heads(w.reshape(depth, d, HEADS, NA_DH), NA_DH)
    w_in_p = jnp.concatenate(
        [heads(part(0)), heads(part(1)), heads(part(2)), part(3), part(4), part(5),
         jnp.pad(part(6), ((0, 0), (0, 0), (0, LANE - MLA_ROPE)))], axis=-1).astype(BF16)

    w_uq = _pad_heads(mla_w_uq, MLA_QK).astype(BF16)
    k_nope = _pad_heads(mla_w_ukv[..., :MLA_NOPE], MLA_NOPE)
    eye = np.zeros((MLA_ROPE, HEADS, LANE), np.float32)
    for h in range(HEADS):
        eye[np.arange(MLA_ROPE), h, MLA_NOPE + np.arange(MLA_ROPE)] = 1.0
    eye = jnp.broadcast_to(jnp.asarray(eye.reshape(MLA_ROPE, HW)), (depth, MLA_ROPE, HW))
    zer = jnp.zeros((depth, 256 - 128 - MLA_ROPE, HW), F32)
    w_k = jnp.concatenate([k_nope, eye, zer], axis=1).astype(BF16)
    w_v = jnp.concatenate([_pad_heads(mla_w_ukv[..., MLA_NOPE:], MLA_V),
                           jnp.zeros((depth, 128, HW), F32)], axis=1).astype(BF16)

    mix_w = HEADS * NA_DH
    w_o_na = jnp.pad(w_out[:, :mix_w].reshape(depth, HEADS, NA_DH, d),
                     ((0, 0), (0, 0), (0, LANE - NA_DH), (0, 0))).reshape(depth, HW, d).astype(BF16)
    w_o_pool = w_out[:, mix_w:mix_w + 256].astype(BF16)
    w_o_mla = jnp.pad(w_out[:, mix_w + 256:].reshape(depth, HEADS, MLA_V, d),
                      ((0, 0), (0, 0), (0, LANE - MLA_V), (0, 0))).reshape(depth, HW, d).astype(BF16)
    ng = len(POOL_WINDOWS)
    pw = jnp.zeros((depth, ng * POOL_G, ng * POOL_G), F32)
    for g in range(ng):
        pw = pw.at[:, g * POOL_G:(g + 1) * POOL_G, g * POOL_G:(g + 1) * POOL_G].set(pool_w[:, g])

    half = peer_subkeys.shape[-1]
    sk = jnp.stack([jnp.pad(peer_subkeys[:, 0], ((0, 0), (0, 0), (0, LANE - half))),
                    jnp.pad(peer_subkeys[:, 1], ((0, 0), (0, 0), (LANE - half, 0)))], axis=1).astype(BF16)

    return dict(
        w_in=w_in_p, w_uq=w_uq, w_k=w_k, w_v=w_v,
        g_q=_head_gain(na_q_norm, NA_DH), g_k=_head_gain(na_k_norm, NA_DH),
        g_cq=mla_cq_norm[:, None, :], g_ckv=mla_ckv_norm[:, None, :],
        g_qm=_head_gain(mla_q_norm, MLA_QK), g_km=_head_gain(mla_k_norm, MLA_QK),
        w_o_na=w_o_na, w_o_pool=w_o_pool, w_o_mla=w_o_mla,
        pool_w=pw.astype(BF16), pool_scale=pool_scale[:, None, :],
        norm1=norm1[:, None, :], norm2=norm2[:, None, :],
        peer_wq=peer_wq.astype(BF16), peer_sk=sk)


def kernel(x_prompt, x_sample, c, cache_nat_k, cache_nat_v, cache_mla_ckv, cache_mla_krope, c_ctx, w_mod, b_mod, norm1, norm2, w_in, na_q_norm, na_k_norm, na_rel_bias, pool_w, pool_scale, mla_cq_norm, mla_ckv_norm, mla_w_uq, mla_w_ukv, mla_q_norm, mla_k_norm, w_out, peer_wq, peer_subkeys, peer_u, peer_v):
    batch, seq, d = x_prompt.shape
    db, ds, _ = x_sample.shape
    depth = w_mod.shape[0]
    past = cache_nat_k.shape[2]
    assert seq == TB and ds % TB == 0 and ds % (GRID_W * WIN_R) == 0 and db + 1 <= 8

    cond8 = jnp.concatenate([c_ctx[None, :], c, jnp.zeros((8 - 1 - db, d), F32)], axis=0)
    mod = _modulation(cond8, w_mod, b_mod).reshape(depth, 8, 1, 6 * d)

    lw_all = _layer_weights(w_in, na_q_norm, na_k_norm, mla_cq_norm, mla_ckv_norm, mla_w_uq,
                            mla_w_ukv, mla_q_norm, mla_k_norm, w_out, pool_w, pool_scale,
                            norm1, norm2, peer_wq, peer_subkeys)
    bias_all = _nat_bias(na_rel_bias)
    tables = [_pack_tables(peer_u[l], peer_v[l]) for l in range(depth)]
    cos_lat, sin_lat = _rope_tables(ds)
    cos_ctx = jnp.ones((TB, LANE), F32)
    sin_ctx = jnp.zeros((TB, LANE), F32)

    ck = jnp.concatenate([cache_mla_ckv, cache_mla_krope,
                          jnp.zeros(cache_mla_ckv.shape[:-1] + (256 - 128 - MLA_ROPE,), F32)],
                         axis=-1).astype(BF16)
    kc_mla, vc_mla = _cache_kv(ck, lw_all["w_k"], lw_all["w_v"], lw_all["g_km"])
    kc_na = _pad_heads(cache_nat_k, NA_DH).astype(BF16)
    vc_na = _pad_heads(cache_nat_v, NA_DH).astype(BF16)

    xs = [x_prompt.reshape(batch * seq, d)] + [x_sample[b] for b in range(db)]
    one_row = max(batch * seq, ds) + 1
    lat_bpm = ds // TB
    ks, vs, ckvs, krs = [], [], [], []
    pending = None
    after = xs[0]

    def join(item, follow):
        si, x1, y_sc, x2, mod_l = item
        xs[si] = _residual(x1, y_sc, mod_l, si, x2, follow)
        return xs[si]

    for l in range(depth):
        lw = {k: v[l] for k, v in lw_all.items()}
        mod_l = mod[l]
        for si in range(db + 1):
            x = xs[si]
            if si == 0:
                (qn, kn, vn, knf, vnf, p, qm, km, vm, ckv, kr) = _in_proj(
                    x, mod_l, 0, one_row, lw, cos_ctx, sin_ctx, 1, after)
                on, om = _ctx_attn(qn, kn, vn, qm, km, vm, seq)
                x1, h2, ids, gt, gn, h2c = _out_proj(x, on, om, p, mod_l, 0, one_row, lw, seq)
                ks.append(knf.reshape(batch, seq, HEADS, LANE)[..., :NA_DH])
                vs.append(vnf.reshape(batch, seq, HEADS, LANE)[..., :NA_DH])
                ckvs.append(ckv.reshape(batch, seq, 128))
                krs.append(kr.reshape(batch, seq, LANE)[..., :MLA_ROPE])
            else:
                b = si - 1
                (qn, kn, vn, _, _, p, qm, km, vm, _, _) = _in_proj(
                    x, mod_l, si, one_row, lw, cos_lat, sin_lat, lat_bpm, after)
                on = _nat_attn(qn, kn, vn, kc_na[b:b + 1, l], vc_na[b:b + 1, l], bias_all[l], 1)
                om = _lat_mla(qm, km, vm, kc_mla[b:b + 1, l], vc_mla[b:b + 1, l], 1)
                x1, h2, ids, gt, gn, h2c = _out_proj(x, on, om, p, mod_l, si, one_row, lw, ds)
            n_sc = x.shape[0] * SC_SHARE[0] // SC_SHARE[1] // PEER_TB * PEER_TB
            y_sc = _sc_peer(tables[l], ids, gn, h2c, n_sc)
            x2 = _peer(x1, h2, ids, gt, mod_l, si, tables[l], n_sc)
            after = x2 if pending is None else join(pending, x2)
            pending = (si, x1, y_sc, x2, mod_l)
    join(pending, pending[1])

    return (xs[0].reshape(batch, seq, d), jnp.stack(xs[1:], axis=0),
            jnp.stack(ks, axis=1), jnp.stack(vs, axis=1),
            jnp.stack(ckvs, axis=1), jnp.stack(krs, axis=1))
```

```python
import functools

import numpy as np
import jax
import jax.numpy as jnp
from jax import lax
from jax.experimental import pallas as pl
from jax.experimental.pallas import tpu as pltpu
from jax.experimental.pallas import tpu_sc as plsc

F32 = jnp.float32
BF16 = jnp.bfloat16

EPS = 1e-6
ROPE_THETA = 10000.0
NEG_INF = -1e30
GRID_W = 64
HEADS = 6
NA_DH = 64
WIN_R = 8
WIN_C = 16
POOL_WINDOWS = (2, 4, 8, 16)
POOL_G = 64
MLA_NOPE = 64
MLA_ROPE = 32
MLA_QK = MLA_NOPE + MLA_ROPE
MLA_V = 64
PEER_HEADS = 8
PEER_NKEYS = 128
PEER_TOPK = 16
LANE = 128
HW = HEADS * LANE
TB = 256
TQ = 256
PEER_TB = 128
PEER_SUB = 8
VMEM_LIMIT = 56 * 1024 * 1024
SC_SHARE = (23, 32)
ROWS_PER_ITER = 2

_CQ, _CK, _CV = 0, HW, 2 * HW
_CP = 3 * HW
_CCQ = _CP + 256
_CCKV = _CCQ + 256
_CKR = _CCKV + 128
IN_W = _CKR + 128


def _params(sem, vmem=VMEM_LIMIT):
    return pltpu.CompilerParams(dimension_semantics=sem, vmem_limit_bytes=vmem)


def _const_spec(shape):
    n = len(shape)
    return pl.BlockSpec(shape, lambda *_: (0,) * n)


def _nt_dot(a, b):
    return lax.dot_general(a, b, (((1,), (1,)), ((), ())), preferred_element_type=F32)


def _mod_kernel(c_ref, w_ref, b_ref, o_ref):
    c = c_ref[...]
    s = c / (1.0 + jnp.exp(-c))
    o_ref[0] = jnp.dot(s, w_ref[0], preferred_element_type=F32,
                       precision=lax.Precision.HIGHEST) + b_ref[0]


def _modulation(cond8, w_mod, b_mod):
    depth, d, n6 = w_mod.shape
    tn = n6 // 4
    return pl.pallas_call(
        _mod_kernel,
        grid=(depth, n6 // tn),
        in_specs=[_const_spec((8, d)),
                  pl.BlockSpec((1, d, tn), lambda l, j: (l, 0, j)),
                  pl.BlockSpec((1, 1, tn), lambda l, j: (l, 0, j))],
        out_specs=pl.BlockSpec((1, 8, tn), lambda l, j: (l, 0, j)),
        out_shape=jax.ShapeDtypeStruct((depth, 8, n6), F32),
        compiler_params=_params(("arbitrary", "arbitrary")),
        name="modulation",
    )(cond8, w_mod, b_mod.reshape(depth, 1, n6))


def _rms(z, gain):
    return z * lax.rsqrt(jnp.mean(z * z, axis=-1, keepdims=True) + EPS) * gain


def _head_rms(zh, gain_h, n_real):
    ms = jnp.sum(zh * zh, axis=-1, keepdims=True) * (1.0 / n_real)
    return zh * lax.rsqrt(ms + EPS) * gain_h


def _rope(zh, cos, sin, is_x1):
    rot = jnp.where(is_x1, pltpu.roll(zh, LANE - 8, 1), pltpu.roll(zh, 8, 1))
    return zh * cos + rot * sin


def _is_x1(rows):
    lane = lax.broadcasted_iota(jnp.int32, (rows, LANE), 1)
    first = jnp.where(lane >= MLA_NOPE, jnp.where(lane < MLA_NOPE + 8, 1, 0), 0)
    second = jnp.where(lane >= MLA_NOPE + 16, jnp.where(lane < MLA_NOPE + 24, 1, 0), 0)
    return (first + second) > 0


def _mla_kv(ck, wk_ref, wv_ref, gk_ref, cos, sin, km_ref, vm_ref):
    rows = ck.shape[0]
    kk = jnp.dot(ck, wk_ref[...], preferred_element_type=F32)
    is_x1 = _is_x1(rows)
    for h in range(HEADS):
        sl = slice(h * LANE, (h + 1) * LANE)
        kh = _head_rms(kk[:, sl], gk_ref[:, sl], MLA_QK)
        km_ref[:, sl] = _rope(kh, cos, sin, is_x1).astype(BF16)
    vm_ref[...] = jnp.dot(ck, wv_ref[...], preferred_element_type=F32).astype(BF16)


def _in_kernel(x_ref, mod_ref, n1_ref, w_ref, wuq_ref, wk_ref, wv_ref,
               gq_ref, gk_ref, gcq_ref, gckv_ref, gqm_ref, gkm_ref, cos_ref, sin_ref, after_hbm,
               qn_ref, kn_ref, vn_ref, knf_ref, vnf_ref, p_ref,
               qm_ref, km_ref, vm_ref, ckv_ref, kr_ref):
    del after_hbm
    d = x_ref.shape[1]
    rows = x_ref.shape[0]
    mod = mod_ref[0]
    sh1 = mod[:, 0:d]
    sc1 = mod[:, d:2 * d]
    h = _rms(x_ref[...], n1_ref[...]) * (1.0 + sc1) + sh1
    hb = h.astype(BF16)

    def proj(lo, hi):
        return jnp.dot(hb, w_ref[:, lo:hi], preferred_element_type=F32)

    cos = cos_ref[...]
    sin = sin_ref[...]
    is_x1 = _is_x1(rows)

    zq = proj(_CQ, _CQ + HW)
    zk = proj(_CK, _CK + HW)
    for hh in range(HEADS):
        sl = slice(hh * LANE, (hh + 1) * LANE)
        qn_ref[:, sl] = (_head_rms(zq[:, sl], gq_ref[:, sl], NA_DH) * (NA_DH ** -0.5)).astype(BF16)
        kh = _head_rms(zk[:, sl], gk_ref[:, sl], NA_DH)
        knf_ref[:, sl] = kh
        kn_ref[:, sl] = kh.astype(BF16)
    zv = proj(_CV, _CV + HW)
    vnf_ref[...] = zv
    vn_ref[...] = zv.astype(BF16)
    p_ref[...] = proj(_CP, _CP + 256)

    cq = _rms(proj(_CCQ, _CCQ + 256), gcq_ref[...])
    zqm = jnp.dot(cq.astype(BF16), wuq_ref[...], preferred_element_type=F32)
    for hh in range(HEADS):
        sl = slice(hh * LANE, (hh + 1) * LANE)
        qh = _head_rms(zqm[:, sl], gqm_ref[:, sl], MLA_QK)
        qm_ref[:, sl] = (_rope(qh, cos, sin, is_x1) * (MLA_QK ** -0.5)).astype(BF16)

    ckv = _rms(proj(_CCKV, _CCKV + 128), gckv_ref[...])
    kr = proj(_CKR, _CKR + 128)
    ckv_ref[...] = ckv
    kr_ref[...] = kr
    ck = jnp.concatenate([ckv, kr], axis=-1).astype(BF16)
    _mla_kv(ck, wk_ref, wv_ref, gkm_ref, cos, sin, km_ref, vm_ref)


def _in_proj(x, mod_l, row_off, bpm, lw, cos_t, sin_t, rope_blocks, after):
    n, d = x.shape
    nb = n // TB
    tok = lambda w: pl.BlockSpec((TB, w), lambda i: (i, 0))
    rope_spec = pl.BlockSpec((TB, LANE), lambda i: (i % rope_blocks, 0))
    in_specs = [tok(d),
                pl.BlockSpec((1, 1, mod_l.shape[-1]), lambda i: (row_off + i // bpm, 0, 0)),
                _const_spec((1, d)), _const_spec((d, IN_W)), _const_spec((256, HW)),
                _const_spec((256, HW)), _const_spec((256, HW)),
                _const_spec((1, HW)), _const_spec((1, HW)), _const_spec((1, 256)),
                _const_spec((1, 128)), _const_spec((1, HW)), _const_spec((1, HW)),
                rope_spec, rope_spec, pl.BlockSpec(memory_space=pl.ANY)]
    widths = [(HW, BF16), (HW, BF16), (HW, BF16), (HW, F32), (HW, F32), (256, F32),
              (HW, BF16), (HW, BF16), (HW, BF16), (128, F32), (128, F32)]
    return pl.pallas_call(
        _in_kernel,
        grid=(nb,),
        in_specs=in_specs,
        out_specs=[tok(w) for w, _ in widths],
        out_shape=[jax.ShapeDtypeStruct((n, w), dt) for w, dt in widths],
        compiler_params=_params(("arbitrary",)),
        name="in_proj",
    )(x, mod_l, lw["norm1"], lw["w_in"], lw["w_uq"], lw["w_k"], lw["w_v"],
      lw["g_q"], lw["g_k"], lw["g_cq"], lw["g_ckv"], lw["g_qm"], lw["g_km"], cos_t, sin_t, after)


def _cache_kernel(ck_ref, wk_ref, wv_ref, gk_ref, km_ref, vm_ref):
    rows = ck_ref.shape[2]
    cos = jnp.ones((rows, LANE), F32)
    sin = jnp.zeros((rows, LANE), F32)
    _mla_kv(ck_ref[0, 0], wk_ref.at[0], wv_ref.at[0], gk_ref.at[0], cos, sin,
            km_ref.at[0, 0], vm_ref.at[0, 0])


def _cache_kv(ck, w_k, w_v, g_km):
    db, depth, p, _ = ck.shape
    spec = lambda w: pl.BlockSpec((1, 1, p, w), lambda b, l: (b, l, 0, 0))
    wspec = lambda r: pl.BlockSpec((1, r, HW), lambda b, l: (l, 0, 0))
    return pl.pallas_call(
        _cache_kernel,
        grid=(db, depth),
        in_specs=[spec(256), wspec(256), wspec(256), wspec(1)],
        out_specs=[spec(HW), spec(HW)],
        out_shape=[jax.ShapeDtypeStruct((db, depth, p, HW), BF16)] * 2,
        compiler_params=_params(("arbitrary", "arbitrary")),
        name="cache_kv",
    )(ck, w_k, w_v, g_km)


def _softmax_av(s_list, v_list):
    m = s_list[0].max(axis=-1, keepdims=True)
    for s in s_list[1:]:
        m = jnp.maximum(m, s.max(axis=-1, keepdims=True))
    acc = None
    den = None
    for s, v in zip(s_list, v_list):
        p = jnp.exp(s - m)
        l = p.sum(axis=-1, keepdims=True)
        o = jnp.dot(p.astype(BF16), v, preferred_element_type=F32)
        acc = o if acc is None else acc + o
        den = l if den is None else den + l
    return acc / den


def _ctx_attn_kernel(qn, kn, vn, qm, km, vm, on, om):
    for q, k, v, o in ((qn, kn, vn, on), (qm, km, vm, om)):
        for h in range(HEADS):
            sl = slice(h * LANE, (h + 1) * LANE)
            s = _nt_dot(q[:, sl], k[:, sl])
            o[:, sl] = _softmax_av([s], [v[:, sl]]).astype(BF16)


def _ctx_attn(qn, kn, vn, qm, km, vm, seq):
    n = qn.shape[0]
    spec = pl.BlockSpec((seq, HW), lambda i: (i, 0))
    return pl.pallas_call(
        _ctx_attn_kernel,
        grid=(n // seq,),
        in_specs=[spec] * 6,
        out_specs=[spec] * 2,
        out_shape=[jax.ShapeDtypeStruct((n, HW), BF16)] * 2,
        compiler_params=_params(("arbitrary",)),
        name="ctx_attn",
    )(qn, kn, vn, qm, km, vm)


def _lat_mla_kernel(q, k, v, kc, vc, o):
    s1 = _nt_dot(q[...], k[...])
    s2 = _nt_dot(q[...], kc[0])
    o[...] = _softmax_av([s1, s2], [v[...], vc[0]]).astype(BF16)


def _lat_mla(qm, km, vm, kc, vc, db):
    n = qm.shape[0]
    ds = n // db
    nq = ds // TQ
    qspec = pl.BlockSpec((TQ, LANE), lambda b, h, i: (b * nq + i, h))
    kspec = pl.BlockSpec((ds, LANE), lambda b, h, i: (b, h))
    cspec = pl.BlockSpec((1, kc.shape[1], LANE), lambda b, h, i: (b, 0, h))
    return pl.pallas_call(
        _lat_mla_kernel,
        grid=(db, HEADS, nq),
        in_specs=[qspec, kspec, kspec, cspec, cspec],
        out_specs=qspec,
        out_shape=jax.ShapeDtypeStruct((n, HW), BF16),
        compiler_params=_params(("arbitrary",) * 3),
        name="lat_mla",
    )(qm, km, vm, kc, vc)


def _nat_kernel(q, k, v, kc, vc, bias, o, *, rows):
    r = pl.program_id(1)
    rs = jnp.clip(r - WIN_R // 2, 0, rows - WIN_R)
    start = pl.multiple_of(rs * GRID_W, GRID_W)
    band = WIN_R * GRID_W
    for h in range(HEADS):
        sl = slice(h * LANE, (h + 1) * LANE)
        qh = q[:, sl]
        s1 = _nt_dot(qh, k[pl.ds(start, band), sl]) + bias[0, h]
        s2 = _nt_dot(qh, kc[0, :, sl])
        o[:, sl] = _softmax_av([s1, s2], [v[pl.ds(start, band), sl], vc[0, :, sl]]).astype(BF16)


def _nat_attn(qn, kn, vn, kc, vc, bias, db):
    n = qn.shape[0]
    ds = n // db
    rows = ds // GRID_W
    band = WIN_R * GRID_W

    def variant(r):
        return jnp.where(r < WIN_R // 2, r, jnp.where(r > rows - WIN_R // 2, r - (rows - WIN_R), WIN_R // 2))

    qspec = pl.BlockSpec((GRID_W, HW), lambda b, r: (b * rows + r, 0))
    kspec = pl.BlockSpec((ds, HW), lambda b, r: (b, 0))
    cspec = pl.BlockSpec((1, kc.shape[1], HW), lambda b, r: (b, 0, 0))
    bspec = pl.BlockSpec((1, HEADS, GRID_W, band), lambda b, r: (variant(r), 0, 0, 0))
    return pl.pallas_call(
        functools.partial(_nat_kernel, rows=rows),
        grid=(db, rows),
        in_specs=[qspec, kspec, kspec, cspec, cspec, bspec],
        out_specs=qspec,
        out_shape=jax.ShapeDtypeStruct((n, HW), BF16),
        compiler_params=_params(("arbitrary", "arbitrary")),
        name="nat_attn",
    )(qn, kn, vn, kc, vc, bias)


def _split3(x):
    hi = x.astype(BF16)
    r = x - hi.astype(F32)
    mid = r.astype(BF16)
    lo = (r - mid.astype(F32)).astype(BF16)
    return hi, mid, lo


def _pool(p_prev, p_cur, p_next, posb, seq_len):
    rows = p_cur.shape[0]
    halo = p_prev.shape[0]
    ext = rows + 2 * halo
    pext = jnp.concatenate([p_prev, p_cur, p_next], axis=0)
    parts = _split3(pext)
    t = posb + lax.broadcasted_iota(jnp.int32, (rows, ext), 0)
    s = posb - halo + lax.broadcasted_iota(jnp.int32, (rows, ext), 1)
    tcol = posb + lax.broadcasted_iota(jnp.int32, (rows, 1), 0)
    grp = lax.broadcasted_iota(jnp.int32, (rows, 256), 1) // POOL_G
    d = jnp.zeros((rows, 256), F32)
    for gi, w in enumerate(POOL_WINDOWS):
        lo = jnp.maximum(t - w // 2, 0)
        hi = jnp.minimum(t + (w - w // 2), seq_len)
        sel = jnp.where(s >= lo, jnp.where(s < hi, 1.0, 0.0), 0.0).astype(BF16)
        tot = sum(jnp.dot(sel, part, preferred_element_type=F32) for part in parts)
        cnt = (jnp.minimum(tcol + (w - w // 2), seq_len) - jnp.maximum(tcol - w // 2, 0)).astype(F32)
        d = jnp.where(grp == gi, tot / cnt - p_cur, d)
    return d


def _first_max(x, pos, sentinel):
    m = jnp.max(x, axis=0, keepdims=True)
    idx = jnp.min(jnp.where(x == m, pos, sentinel), axis=0, keepdims=True)
    return m, idx


def _topk_stage1(qh, sk_ref):
    c = qh.shape[0]
    key_pos = lax.broadcasted_iota(jnp.int32, (PEER_NKEYS, c), 0).astype(F32)
    row16 = lax.broadcasted_iota(jnp.int32, (PEER_TOPK, c), 0)
    neg = jnp.float32(-jnp.inf)
    s0 = _nt_dot(sk_ref[0], qh)
    s1 = _nt_dot(sk_ref[1], qh)

    def stage1(a, carry):
        out = []
        for s, sv, si in (carry[0:3], carry[3:6]):
            m, idx = _first_max(s, key_pos, float(PEER_NKEYS))
            out += [jnp.where(key_pos == idx, neg, s),
                    jnp.where(row16 == a, m, sv), jnp.where(row16 == a, idx, si)]
        return tuple(out)

    zf = jnp.zeros((PEER_TOPK, c), F32)
    _, sv0, si0, _, sv1, si1 = lax.fori_loop(0, PEER_TOPK, stage1, (s0, zf, zf, s1, zf, zf))
    return sv0, si0, sv1, si1


def _topk_pieces(sv0, sv1):
    c = sv0.shape[1]
    neg = jnp.float32(-jnp.inf)
    sub8 = lax.broadcasted_iota(jnp.int32, (8, c), 0)
    sub8f = sub8.astype(F32)
    cs, cf = [], []
    for a in range(8):
        nb = PEER_TOPK // (a + 1)
        for b0 in range(0, nb, 8):
            val = sv0[a:a + 1] + sv1[b0:b0 + 8]
            if nb - b0 < 8:
                val = jnp.where(sub8 < nb - b0, val, neg)
            cs.append(val)
            cf.append(sub8f + float(a * PEER_TOPK + b0))
    cs.append(sv0[8:16] + sv1[0:1])
    cf.append((sub8f + 8.0) * float(PEER_TOPK))
    return cs, cf


def _topk_stage2(chains, cf):
    npc = len(cf)
    c = cf[0].shape[1]
    row16 = lax.broadcasted_iota(jnp.int32, (PEER_TOPK, c), 0)
    neg = jnp.float32(-jnp.inf)
    nflat = float(PEER_TOPK * PEER_TOPK)
    zf = jnp.zeros((PEER_TOPK, c), F32)

    def step(k, carry):
        out = []
        for ch in range(len(chains)):
            vals = carry[ch * (npc + 2):ch * (npc + 2) + npc]
            tv, tp = carry[ch * (npc + 2) + npc], carry[ch * (npc + 2) + npc + 1]
            m = vals[0]
            for v in vals[1:]:
                m = jnp.maximum(m, v)
            m = jnp.max(m, axis=0, keepdims=True)
            pos = None
            for v, f in zip(vals, cf):
                cand = jnp.where(v == m, f, nflat)
                pos = cand if pos is None else jnp.minimum(pos, cand)
            pos = jnp.min(pos, axis=0, keepdims=True)
            out += [jnp.where(f == pos, neg, v) for v, f in zip(vals, cf)]
            out += [jnp.where(row16 == k, m, tv), jnp.where(row16 == k, pos, tp)]
        return tuple(out)

    init = []
    for cs in chains:
        init += list(cs) + [zf, zf]
    res = lax.fori_loop(0, PEER_TOPK, step, tuple(init))
    return [(res[ch * (npc + 2) + npc], res[ch * (npc + 2) + npc + 1]) for ch in range(len(chains))]


def _topk_ids(tp, si0, si1):
    a = jnp.floor(tp * (1.0 / PEER_TOPK))
    b = tp - a * float(PEER_TOPK)
    ea = jnp.zeros_like(tp)
    eb = jnp.zeros_like(tp)
    for j in range(PEER_TOPK):
        ea = jnp.where(a == float(j), si0[j:j + 1], ea)
        eb = jnp.where(b == float(j), si1[j:j + 1], eb)
    return ea * float(PEER_NKEYS) + eb


def _out_kernel(on_ref, om_ref, pc_ref, pp_ref, pn_ref, x_ref, mod_ref,
                won_ref, wop_ref, wom_ref, pw_ref, ps_ref, n2_ref, wq_ref, sk_ref,
                x1_ref, h2_ref, ids_ref, gt_ref, gn_ref, h2c_ref, q_scr, idt_scr, *, bps, seq_len):
    d = x_ref.shape[1]
    rows = x_ref.shape[0]
    i = pl.program_id(0)
    mod = mod_ref[0]
    g1 = mod[:, 2 * d:3 * d]
    sh2 = mod[:, 3 * d:4 * d]
    sc2 = mod[:, 4 * d:5 * d]

    posb = (i % bps) * rows
    dpool = _pool(pp_ref[...], pc_ref[...], pn_ref[...], posb, seq_len)
    ypool = jnp.dot(dpool.astype(BF16), pw_ref[...], preferred_element_type=F32) * ps_ref[...]
    mix = (jnp.dot(on_ref[...], won_ref[...], preferred_element_type=F32)
           + jnp.dot(ypool.astype(BF16), wop_ref[...], preferred_element_type=F32)
           + jnp.dot(om_ref[...], wom_ref[...], preferred_element_type=F32))
    x1 = x_ref[...] + g1 * mix
    x1_ref[...] = x1
    h2 = _rms(x1, n2_ref[...]) * (1.0 + sc2) + sh2
    h2_ref[...] = h2
    for j in range(d // LANE):
        h2c_ref[j] = h2[:, j * LANE:(j + 1) * LANE]

    q = jnp.dot(h2.astype(BF16), wq_ref[...], preferred_element_type=F32)
    for hh in range(PEER_HEADS):
        q_scr[hh] = q[:, hh * LANE:(hh + 1) * LANE].astype(BF16)

    chunks = range(0, rows, LANE)

    def head(hh, _):
        sorted_keys = [_topk_stage1(q_scr[hh, c0:c0 + LANE, :], sk_ref) for c0 in chunks]
        pieces = [_topk_pieces(sv0, sv1) for sv0, _, sv1, _ in sorted_keys]
        picked = _topk_stage2([cs for cs, _ in pieces], pieces[0][1])
        r0 = pl.multiple_of(hh * PEER_TOPK, PEER_TOPK)
        for c0, (tv, tp), (_, si0, _, si1) in zip(chunks, picked, sorted_keys):
            ex = jnp.exp(tv - tv[0:1])
            gt_ref[pl.ds(r0, PEER_TOPK), c0:c0 + LANE] = ex / jnp.sum(ex, axis=0, keepdims=True)
            idt_scr[pl.ds(r0, PEER_TOPK), c0:c0 + LANE] = _topk_ids(tp, si0, si1)
        return 0

    lax.fori_loop(0, PEER_HEADS, head, 0)
    ids_ref[...] = idt_scr[...].T.astype(jnp.int32)
    gn_ref[...] = gt_ref[...].T


def _out_proj(x, on, om, p, mod_l, row_off, bpm, lw, seq_len):
    n, d = x.shape
    nb = n // TB
    bps = seq_len // TB
    halo = 8
    hb = TB // halo
    tok = lambda w: pl.BlockSpec((TB, w), lambda i: (i, 0))
    in_specs = [tok(HW), tok(HW), tok(256),
                pl.BlockSpec((halo, 256), lambda i: (jnp.maximum(i * hb - 1, 0), 0)),
                pl.BlockSpec((halo, 256), lambda i: (jnp.minimum((i + 1) * hb, n // halo - 1), 0)),
                tok(d),
                pl.BlockSpec((1, 1, mod_l.shape[-1]), lambda i: (row_off + i // bpm, 0, 0)),
                _const_spec((HW, d)), _const_spec((256, d)), _const_spec((HW, d)),
                _const_spec((256, 256)), _const_spec((1, 256)), _const_spec((1, d)),
                _const_spec((d, PEER_HEADS * LANE)), _const_spec((2, PEER_NKEYS, LANE))]
    nk = PEER_HEADS * PEER_TOPK
    return pl.pallas_call(
        functools.partial(_out_kernel, bps=bps, seq_len=seq_len),
        grid=(nb,),
        in_specs=in_specs,
        out_specs=[tok(d), tok(d), tok(nk), pl.BlockSpec((nk, TB), lambda i: (0, i)), tok(nk),
                   pl.BlockSpec((d // LANE, TB, LANE), lambda i: (0, i, 0))],
        out_shape=[jax.ShapeDtypeStruct((n, d), F32), jax.ShapeDtypeStruct((n, d), F32),
                   jax.ShapeDtypeStruct((n, nk), jnp.int32), jax.ShapeDtypeStruct((nk, n), F32),
                   jax.ShapeDtypeStruct((n, nk), F32),
                   jax.ShapeDtypeStruct((d // LANE, n, LANE), F32)],
        scratch_shapes=[pltpu.VMEM((PEER_HEADS, TB, LANE), BF16), pltpu.VMEM((nk, TB), F32)],
        compiler_params=_params(("arbitrary",)),
        name="out_proj",
    )(on, om, p, p, p, x, mod_l, lw["w_o_na"], lw["w_o_pool"], lw["w_o_mla"],
      lw["pool_w"], lw["pool_scale"], lw["norm2"], lw["peer_wq"], lw["peer_sk"])


def _gelu_tanh(x):
    return x * (0.5 * (1.0 + jnp.tanh(0.7978845608028654 * (x + 0.044715 * (x * x * x)))))


def _peer_token_mix(chunk, hrow, gcol, ch):
    acc = None
    for s in range(ch):
        us = lax.bitcast_convert_type(chunk(s) & jnp.int32(-65536), F32)
        term = us * hrow[:, s * LANE:(s + 1) * LANE]
        acc = term if acc is None else acc + term
    wgt = gcol * _gelu_tanh(jnp.sum(acc, axis=-1, keepdims=True))
    parts = []
    for s in range(ch):
        vs = lax.bitcast_convert_type(chunk(s) << 16, F32)
        parts.append(jnp.sum(vs * wgt, axis=0, keepdims=True))
    return jnp.concatenate(parts, axis=-1)


def _sc_peer(table3, ids, gates, h2c, n):
    ch, _, lane = h2c.shape
    nk = ids.shape[1]
    info = plsc.get_sparse_core_info()
    nc, nw, nl = info.num_cores, info.num_cores * info.num_subcores, info.num_lanes
    tpw = n // nw
    win = 32
    nq = nk // win
    cpr = lane // nl
    nchunk = ch * cpr
    hc = nchunk // 2
    assert n % nw == 0 and nk % win == 0 and win % nl == 0
    mesh = plsc.VectorSubcoreMesh(core_axis_name="core", subcore_axis_name="subcore")
    hi_mask = jnp.int32(-65536)

    @functools.partial(
        pl.kernel, mesh=mesh,
        out_type=jax.ShapeDtypeStruct((ch, n, lane), F32),
        compiler_params=pltpu.CompilerParams(needs_layout_passes=False),
        scratch_types=[pltpu.VMEM((nk,), jnp.int32), pltpu.VMEM((nk,), F32),
                       pltpu.VMEM((ch, lane), F32), pltpu.VMEM((ch, lane), F32),
                       pltpu.VMEM((win, ch, lane), jnp.int32), pltpu.VMEM((win, ch, lane), jnp.int32),
                       pltpu.VMEM((win * nl,), F32), pltpu.VMEM((win,), F32),
                       pltpu.SemaphoreType.DMA, pltpu.SemaphoreType.DMA, pltpu.SemaphoreType.DMA])
    def peer(tab_hbm, ids_hbm, g_hbm, h2_hbm, y_hbm,
             idx_v, g_v, x_v, y_v, rows_a, rows_b, part_v, w_v, sem_a, sem_b, sem_x):
        wid = lax.axis_index("subcore") * nc + lax.axis_index("core")
        bufs = ((rows_a, sem_a), (rows_b, sem_b))
        lanes = lax.iota(jnp.int32, nl)
        zero = jnp.zeros((nl,), F32)

        def chunk_copies(tok, to_hbm):
            if to_hbm:
                return [pltpu.make_async_copy(y_v.at[j], y_hbm.at[j, tok], sem_x) for j in range(ch)]
            return [pltpu.make_async_copy(h2_hbm.at[j, tok], x_v.at[j], sem_x) for j in range(ch)]

        def fetch(q):
            rows, sem = bufs[q % 2]
            return pltpu.make_async_copy(tab_hbm.at[idx_v.at[pl.ds(q * win, win)]], rows, sem)

        def word(rows, r, cc):
            return rows[r, cc // cpr, pl.ds((cc % cpr) * nl, nl)]

        @pl.loop(0, tpw)
        def _(ti):
            tok = wid * tpw + ti
            loads = chunk_copies(tok, False)
            for cp in loads:
                cp.start()
            pltpu.sync_copy(ids_hbm.at[tok], idx_v)
            pltpu.sync_copy(g_hbm.at[tok], g_v)
            for cp in loads:
                cp.wait()
            for cc in range(nchunk):
                y_v[cc // cpr, pl.ds((cc % cpr) * nl, nl)] = zero
            fetch(0).start()
            for q in range(nq):
                rows = bufs[q % 2][0]
                fetch(q).wait()
                if q + 1 < nq:
                    fetch(q + 1).start()

                for half in range(2):
                    xs = [x_v[(half * hc + c) // cpr, pl.ds(((half * hc + c) % cpr) * nl, nl)]
                          for c in range(hc)]

                    @pl.loop(0, win, step=ROWS_PER_ITER)
                    def _(r0):
                        accs = [[None] * 4 for _ in range(ROWS_PER_ITER)]
                        for c in range(hc):
                            for k in range(ROWS_PER_ITER):
                                u = lax.bitcast_convert_type(
                                    word(rows, r0 + k, half * hc + c) & hi_mask, F32)
                                t = u * xs[c]
                                accs[k][c % 4] = t if accs[k][c % 4] is None else accs[k][c % 4] + t
                        for k in range(ROWS_PER_ITER):
                            acc = (accs[k][0] + accs[k][1]) + (accs[k][2] + accs[k][3])
                            po = pl.multiple_of((r0 + k) * nl, nl)
                            if half == 0:
                                part_v[pl.ds(po, nl)] = acc
                            else:
                                part_v[pl.ds(po, nl)] = part_v[pl.ds(po, nl)] + acc

                for grp in range(win // nl):
                    s = zero
                    for rr in range(nl):
                        tot = jnp.sum(part_v[pl.ds((grp * nl + rr) * nl, nl)])
                        s = jnp.where(lanes == rr, tot, s)
                    z = 0.7978845608028654 * (s + 0.044715 * (s * s * s))
                    tanh = 1.0 - 2.0 / (jnp.exp(2.0 * z) + 1.0)
                    gate = g_v[pl.ds(q * win + grp * nl, nl)]
                    w_v[pl.ds(grp * nl, nl)] = gate * (s * (0.5 * (1.0 + tanh)))

                for half in range(2):
                    def body(r, yacc):
                        wr = plsc.load_gather(w_v, [jnp.full((nl,), r, jnp.int32)])
                        out = []
                        for c in range(hc):
                            v = lax.bitcast_convert_type(word(rows, r, half * hc + c) << 16, F32)
                            out.append(yacc[c] + wr * v)
                        return tuple(out)

                    yacc = lax.fori_loop(0, win, body, tuple(zero for _ in range(hc)))
                    for c in range(hc):
                        cc = half * hc + c
                        sl = (cc // cpr, pl.ds((cc % cpr) * nl, nl))
                        y_v[sl] = y_v[sl] + yacc[c]
            stores = chunk_copies(tok, True)
            for cp in stores:
                cp.start()
            for cp in stores:
                cp.wait()

    return peer(table3, ids, gates, h2c)


def _residual_kernel(x1_ref, y_ref, mod_ref, x2_hbm, after_hbm, o_ref):
    del x2_hbm
    del after_hbm
    d = x1_ref.shape[1]
    g2 = mod_ref[0][:, 5 * d:6 * d]
    for j in range(d // LANE):
        sl = slice(j * LANE, (j + 1) * LANE)
        o_ref[:, sl] = x1_ref[:, sl] + g2[:, sl] * y_ref[j]


def _residual(x1, y, mod_l, row, x2, after):
    n, d = x1.shape
    tok = pl.BlockSpec((PEER_TB, d), lambda i: (i, 0))
    any_spec = pl.BlockSpec(memory_space=pl.ANY)
    return pl.pallas_call(
        _residual_kernel,
        grid=(y.shape[1] // PEER_TB,),
        in_specs=[tok, pl.BlockSpec((d // LANE, PEER_TB, LANE), lambda i: (0, i, 0)),
                  pl.BlockSpec((1, 1, mod_l.shape[-1]), lambda i: (row, 0, 0)),
                  any_spec, any_spec],
        out_specs=tok,
        out_shape=jax.ShapeDtypeStruct((n, d), F32),
        input_output_aliases={3: 0},
        compiler_params=_params(("arbitrary",)),
        name="residual",
    )(x1, y, mod_l, x2, after)


def _peer_kernel(ids_hbm, gt_ref, h2_ref, x1_ref, mod_ref, tab_hbm, o_ref,
                 ids_s, buf, sem_i, sem_r, *, first_block):
    d = x1_ref.shape[1]
    ch = d // LANE
    pitch = ch + 1
    nsub = x1_ref.shape[0] // PEER_SUB
    nk = gt_ref.shape[0]
    nids = PEER_SUB * nk
    i = pl.program_id(0) + first_block
    g2 = mod_ref[0][:, 5 * d:6 * d]
    tok_lane = lax.broadcasted_iota(jnp.int32, gt_ref.shape, 1)

    def ids_copy(j, slot):
        start = pl.multiple_of((i * nsub + j) * nids, nids)
        return pltpu.make_async_copy(ids_hbm.at[pl.ds(start, nids)],
                                     ids_s.at[pl.ds(slot * nids, nids)], sem_i.at[slot])

    def row_copy(slot, e, f):
        src = tab_hbm.at[pl.ds(pl.multiple_of(e * ch, ch), ch), :]
        dst = buf.at[slot, pl.ds(f * pitch, ch), :]
        return pltpu.make_async_copy(src, dst, sem_r.at[slot])

    def issue_rows(slot):
        for t in range(PEER_SUB):
            def body(kk, _):
                for r in range(8):
                    f = t * nk + kk * 8 + r
                    row_copy(slot, ids_s[slot * nids + f], f).start(priority=r % 2)
                return 0

            lax.fori_loop(0, nk // 8, body, 0)

    def wait_rows(slot):
        done = buf.at[slot, pl.ds(0, nids * ch), :]
        pltpu.make_async_copy(done, done, sem_r.at[slot]).wait()

    def compute(slot, j):
        base = pl.multiple_of(j * PEER_SUB, PEER_SUB)
        h8 = h2_ref[pl.ds(base, PEER_SUB), :]
        ys = []
        for t in range(PEER_SUB):
            chunk = lambda s: buf[slot, pl.ds(t * nk * pitch + s, nk, stride=pitch), :]
            gcol = jnp.sum(jnp.where(tok_lane == base + t, gt_ref[...], 0.0), axis=-1, keepdims=True)
            ys.append(_peer_token_mix(chunk, h8[t:t + 1, :], gcol, ch))
        y8 = jnp.concatenate(ys, axis=0)
        o_ref[pl.ds(base, PEER_SUB), :] = x1_ref[pl.ds(base, PEER_SUB), :] + g2 * y8

    first = ids_copy(0, 0)
    first.start()
    first.wait()
    issue_rows(0)
    ids_copy(1, 1).start()

    def pair(jj, _):
        j0 = 2 * jj
        ids_copy(j0 + 1, 1).wait()
        issue_rows(1)

        @pl.when(j0 + 2 < nsub)
        def _():
            ids_copy(j0 + 2, 0).start()

        wait_rows(0)
        compute(0, j0)

        @pl.when(j0 + 2 < nsub)
        def _():
            ids_copy(j0 + 2, 0).wait()
            issue_rows(0)

        @pl.when(j0 + 3 < nsub)
        def _():
            ids_copy(j0 + 3, 1).start()

        wait_rows(1)
        compute(1, j0 + 1)
        return 0

    lax.fori_loop(0, nsub // 2, pair, 0)


def _pack_tables(peer_u, peer_v):
    e, d = peer_u.shape
    ub = lax.bitcast_convert_type(peer_u.astype(BF16), jnp.uint16).astype(jnp.uint32)
    vb = lax.bitcast_convert_type(peer_v.astype(BF16), jnp.uint16).astype(jnp.uint32)
    words = lax.bitcast_convert_type((ub << 16) | vb, jnp.int32)
    return words.reshape(e, d // LANE, LANE)


def _peer(x1, h2, ids, gt, mod_l, row, table, tok0):
    n, d = x1.shape
    nk = gt.shape[0]
    b0 = tok0 // PEER_TB
    nb = n // PEER_TB - b0
    tok = pl.BlockSpec((PEER_TB, d), lambda i: (i + b0, 0))
    any_spec = pl.BlockSpec(memory_space=pl.ANY)
    return pl.pallas_call(
        functools.partial(_peer_kernel, first_block=b0),
        grid=(nb,),
        in_specs=[any_spec,
                  pl.BlockSpec((nk, PEER_TB), lambda i: (0, i + b0)),
                  tok, tok,
                  pl.BlockSpec((1, 1, mod_l.shape[-1]), lambda i: (row, 0, 0)),
                  any_spec],
        out_specs=tok,
        out_shape=jax.ShapeDtypeStruct((n, d), F32),
        scratch_shapes=[pltpu.SMEM((2 * PEER_SUB * nk,), jnp.int32),
                        pltpu.VMEM((2, PEER_SUB * nk * (d // LANE + 1), LANE), jnp.int32),
                        pltpu.SemaphoreType.DMA((2,)),
                        pltpu.SemaphoreType.DMA((2,))],
        compiler_params=_params(("arbitrary",)),
        name="peer",
    )(ids.reshape(n * nk), gt, h2, x1, mod_l, table.reshape(-1, LANE))


def _pad_heads(w, width):
    pad = [(0, 0)] * (w.ndim - 1) + [(0, LANE - width)]
    w = jnp.pad(w, pad)
    return w.reshape(w.shape[:-2] + (HW,))


def _head_gain(g, width):
    depth = g.shape[0]
    g = jnp.pad(g, ((0, 0), (0, LANE - width)))
    return jnp.tile(g, (1, HEADS)).reshape(depth, 1, HW)


def _rope_tables(seq):
    t = np.arange(seq)
    half = MLA_ROPE // 2
    inv = ROPE_THETA ** (-np.arange(0, half, 2, dtype=np.float32) / half)
    cos = np.ones((seq, LANE), np.float32)
    sin = np.zeros((seq, LANE), np.float32)
    for off, pos in ((MLA_NOPE, t // GRID_W), (MLA_NOPE + half, t % GRID_W)):
        ang = pos.astype(np.float32)[:, None] * inv[None, :]
        q = half // 2
        cos[:, off:off + q] = np.cos(ang)
        cos[:, off + q:off + half] = np.cos(ang)
        sin[:, off:off + q] = -np.sin(ang)
        sin[:, off + q:off + half] = np.sin(ang)
    return jnp.asarray(cos), jnp.asarray(sin)


def _nat_bias(rel_bias):
    v = np.arange(WIN_R)[:, None]
    j = np.arange(WIN_R)[None, :]
    dr = j - v + WIN_R - 1
    cq = np.arange(GRID_W)[:, None]
    kc = np.arange(GRID_W)[None, :]
    cstart = np.clip(cq - WIN_C // 2, 0, GRID_W - WIN_C)
    ok = (kc >= cstart) & (kc < cstart + WIN_C)
    dc = np.clip(kc - cq + WIN_C - 1, 0, 2 * WIN_C - 2)
    b = rel_bias[:, :, dr]
    b = b[..., dc]
    b = jnp.where(jnp.asarray(ok)[None, None, None, None], b, NEG_INF)
    b = jnp.transpose(b, (0, 2, 1, 4, 3, 5))
    return b.reshape(b.shape[0], WIN_R, HEADS, GRID_W, WIN_R * GRID_W)


def _layer_weights(w_in, na_q_norm, na_k_norm, mla_cq_norm, mla_ckv_norm, mla_w_uq, mla_w_ukv,
                   mla_q_norm, mla_k_norm, w_out, pool_w, pool_scale, norm1, norm2,
                   peer_wq, peer_subkeys):
    depth, d, _ = w_in.shape
    na_w = HEADS * NA_DH
    segs = np.cumsum([0, na_w, na_w, na_w, 256, 256, 128, MLA_ROPE])
    part = lambda i: w_in[:, :, segs[i]:segs[i + 1]]
    heads = lambda w: _pad_heads(w.reshape(depth, d, HEADS, NA_DH), NA_DH)
    w_in_p = jnp.concatenate(
        [heads(part(0)), heads(part(1)), heads(part(2)), part(3), part(4), part(5),
         jnp.pad(part(6), ((0, 0), (0, 0), (0, LANE - MLA_ROPE)))], axis=-1).astype(BF16)

    w_uq = _pad_heads(mla_w_uq, MLA_QK).astype(BF16)
    k_nope = _pad_heads(mla_w_ukv[..., :MLA_NOPE], MLA_NOPE)
    eye = np.zeros((MLA_ROPE, HEADS, LANE), np.float32)
    for h in range(HEADS):
        eye[np.arange(MLA_ROPE), h, MLA_NOPE + np.arange(MLA_ROPE)] = 1.0
    eye = jnp.broadcast_to(jnp.asarray(eye.reshape(MLA_ROPE, HW)), (depth, MLA_ROPE, HW))
    zer = jnp.zeros((depth, 256 - 128 - MLA_ROPE, HW), F32)
    w_k = jnp.concatenate([k_nope, eye, zer], axis=1).astype(BF16)
    w_v = jnp.concatenate([_pad_heads(mla_w_ukv[..., MLA_NOPE:], MLA_V),
                           jnp.zeros((depth, 128, HW), F32)], axis=1).astype(BF16)

    mix_w = HEADS * NA_DH
    w_o_na = jnp.pad(w_out[:, :mix_w].reshape(depth, HEADS, NA_DH, d),
                     ((0, 0), (0, 0), (0, LANE - NA_DH), (0, 0))).reshape(depth, HW, d).astype(BF16)
    w_o_pool = w_out[:, mix_w:mix_w + 256].astype(BF16)
    w_o_mla = jnp.pad(w_out[:, mix_w + 256:].reshape(depth, HEADS, MLA_V, d),
                      ((0, 0), (0, 0), (0, LANE - MLA_V), (0, 0))).reshape(depth, HW, d).astype(BF16)
    ng = len(POOL_WINDOWS)
    pw = jnp.zeros((depth, ng * POOL_G, ng * POOL_G), F32)
    for g in range(ng):
        pw = pw.at[:, g * POOL_G:(g + 1) * POOL_G, g * POOL_G:(g + 1) * POOL_G].set(pool_w[:, g])

    half = peer_subkeys.shape[-1]
    sk = jnp.stack([jnp.pad(peer_subkeys[:, 0], ((0, 0), (0, 0), (0, LANE - half))),
                    jnp.pad(peer_subkeys[:, 1], ((0, 0), (0, 0), (LANE - half, 0)))], axis=1).astype(BF16)

    return dict(
        w_in=w_in_p, w_uq=w_uq, w_k=w_k, w_v=w_v,
        g_q=_head_gain(na_q_norm, NA_DH), g_k=_head_gain(na_k_norm, NA_DH),
        g_cq=mla_cq_norm[:, None, :], g_ckv=mla_ckv_norm[:, None, :],
        g_qm=_head_gain(mla_q_norm, MLA_QK), g_km=_head_gain(mla_k_norm, MLA_QK),
        w_o_na=w_o_na, w_o_pool=w_o_pool, w_o_mla=w_o_mla,
        pool_w=pw.astype(BF16), pool_scale=pool_scale[:, None, :],
        norm1=norm1[:, None, :], norm2=norm2[:, None, :],
        peer_wq=peer_wq.astype(BF16), peer_sk=sk)


def kernel(x_prompt, x_sample, c, cache_nat_k, cache_nat_v, cache_mla_ckv, cache_mla_krope, c_ctx, w_mod, b_mod, norm1, norm2, w_in, na_q_norm, na_k_norm, na_rel_bias, pool_w, pool_scale, mla_cq_norm, mla_ckv_norm, mla_w_uq, mla_w_ukv, mla_q_norm, mla_k_norm, w_out, peer_wq, peer_subkeys, peer_u, peer_v):
    batch, seq, d = x_prompt.shape
    db, ds, _ = x_sample.shape
    depth = w_mod.shape[0]
    past = cache_nat_k.shape[2]
    assert seq == TB and ds % TB == 0 and ds % (GRID_W * WIN_R) == 0 and db + 1 <= 8

    cond8 = jnp.concatenate([c_ctx[None, :], c, jnp.zeros((8 - 1 - db, d), F32)], axis=0)
    mod = _modulation(cond8, w_mod, b_mod).reshape(depth, 8, 1, 6 * d)

    lw_all = _layer_weights(w_in, na_q_norm, na_k_norm, mla_cq_norm, mla_ckv_norm, mla_w_uq,
                            mla_w_ukv, mla_q_norm, mla_k_norm, w_out, pool_w, pool_scale,
                            norm1, norm2, peer_wq, peer_subkeys)
    bias_all = _nat_bias(na_rel_bias)
    tables = [_pack_tables(peer_u[l], peer_v[l]) for l in range(depth)]
    cos_lat, sin_lat = _rope_tables(ds)
    cos_ctx = jnp.ones((TB, LANE), F32)
    sin_ctx = jnp.zeros((TB, LANE), F32)

    ck = jnp.concatenate([cache_mla_ckv, cache_mla_krope,
                          jnp.zeros(cache_mla_ckv.shape[:-1] + (256 - 128 - MLA_ROPE,), F32)],
                         axis=-1).astype(BF16)
    kc_mla, vc_mla = _cache_kv(ck, lw_all["w_k"], lw_all["w_v"], lw_all["g_km"])
    kc_na = _pad_heads(cache_nat_k, NA_DH).astype(BF16)
    vc_na = _pad_heads(cache_nat_v, NA_DH).astype(BF16)

    xs = [x_prompt.reshape(batch * seq, d)] + [x_sample[b] for b in range(db)]
    one_row = max(batch * seq, ds) + 1
    lat_bpm = ds // TB
    ks, vs, ckvs, krs = [], [], [], []
    pending = None
    after = xs[0]

    def join(item, follow):
        si, x1, y_sc, x2, mod_l = item
        xs[si] = _residual(x1, y_sc, mod_l, si, x2, follow)
        return xs[si]

    for l in range(depth):
        lw = {k: v[l] for k, v in lw_all.items()}
        mod_l = mod[l]
        for si in range(db + 1):
            x = xs[si]
            if si == 0:
                (qn, kn, vn, knf, vnf, p, qm, km, vm, ckv, kr) = _in_proj(
                    x, mod_l, 0, one_row, lw, cos_ctx, sin_ctx, 1, after)
                on, om = _ctx_attn(qn, kn, vn, qm, km, vm, seq)
                x1, h2, ids, gt, gn, h2c = _out_proj(x, on, om, p, mod_l, 0, one_row, lw, seq)
                ks.append(knf.reshape(batch, seq, HEADS, LANE)[..., :NA_DH])
                vs.append(vnf.reshape(batch, seq, HEADS, LANE)[..., :NA_DH])
                ckvs.append(ckv.reshape(batch, seq, 128))
                krs.append(kr.reshape(batch, seq, LANE)[..., :MLA_ROPE])
            else:
                b = si - 1
                (qn, kn, vn, _, _, p, qm, km, vm, _, _) = _in_proj(
                    x, mod_l, si, one_row, lw, cos_lat, sin_lat, lat_bpm, after)
                on = _nat_attn(qn, kn, vn, kc_na[b:b + 1, l], vc_na[b:b + 1, l], bias_all[l], 1)
                om = _lat_mla(qm, km, vm, kc_mla[b:b + 1, l], vc_mla[b:b + 1, l], 1)
                x1, h2, ids, gt, gn, h2c = _out_proj(x, on, om, p, mod_l, si, one_row, lw, ds)
            n_sc = x.shape[0] * SC_SHARE[0] // SC_SHARE[1] // PEER_TB * PEER_TB
            y_sc = _sc_peer(tables[l], ids, gn, h2c, n_sc)
            x2 = _peer(x1, h2, ids, gt, mod_l, si, tables[l], n_sc)
            after = x2 if pending is None else join(pending, x2)
            pending = (si, x1, y_sc, x2, mod_l)
    join(pending, pending[1])

    return (xs[0].reshape(batch, seq, d), jnp.stack(xs[1:], axis=0),
            jnp.stack(ks, axis=1), jnp.stack(vs, axis=1),
            jnp.stack(ckvs, axis=1), jnp.stack(krs, axis=1))
```

```python
import functools

import numpy as np
import jax
import jax.numpy as jnp
from jax import lax
from jax.experimental import pallas as pl
from jax.experimental.pallas import tpu as pltpu
from jax.experimental.pallas import tpu_sc as plsc

F32 = jnp.float32
BF16 = jnp.bfloat16

EPS = 1e-6
ROPE_THETA = 10000.0
NEG_INF = -1e30
GRID_W = 64
HEADS = 6
NA_DH = 64
WIN_R = 8
WIN_C = 16
POOL_WINDOWS = (2, 4, 8, 16)
POOL_G = 64
MLA_NOPE = 64
MLA_ROPE = 32
MLA_QK = MLA_NOPE + MLA_ROPE
MLA_V = 64
PEER_HEADS = 8
PEER_NKEYS = 128
PEER_TOPK = 16
LANE = 128
HW = HEADS * LANE
TB = 256
TQ = 256
PEER_TB = 128
PEER_SUB = 8
VMEM_LIMIT = 56 * 1024 * 1024
SC_SHARE = (20, 32)

_CQ, _CK, _CV = 0, HW, 2 * HW
_CP = 3 * HW
_CCQ = _CP + 256
_CCKV = _CCQ + 256
_CKR = _CCKV + 128
IN_W = _CKR + 128


def _params(sem, vmem=VMEM_LIMIT):
    return pltpu.CompilerParams(dimension_semantics=sem, vmem_limit_bytes=vmem)


def _const_spec(shape):
    n = len(shape)
    return pl.BlockSpec(shape, lambda *_: (0,) * n)


def _nt_dot(a, b):
    return lax.dot_general(a, b, (((1,), (1,)), ((), ())), preferred_element_type=F32)


def _mod_kernel(c_ref, w_ref, b_ref, o_ref):
    c = c_ref[...]
    s = c / (1.0 + jnp.exp(-c))
    o_ref[0] = jnp.dot(s, w_ref[0], preferred_element_type=F32,
                       precision=lax.Precision.HIGHEST) + b_ref[0]


def _modulation(cond8, w_mod, b_mod):
    depth, d, n6 = w_mod.shape
    tn = n6 // 4
    return pl.pallas_call(
        _mod_kernel,
        grid=(depth, n6 // tn),
        in_specs=[_const_spec((8, d)),
                  pl.BlockSpec((1, d, tn), lambda l, j: (l, 0, j)),
                  pl.BlockSpec((1, 1, tn), lambda l, j: (l, 0, j))],
        out_specs=pl.BlockSpec((1, 8, tn), lambda l, j: (l, 0, j)),
        out_shape=jax.ShapeDtypeStruct((depth, 8, n6), F32),
        compiler_params=_params(("arbitrary", "arbitrary")),
        name="modulation",
    )(cond8, w_mod, b_mod.reshape(depth, 1, n6))


def _rms(z, gain):
    return z * lax.rsqrt(jnp.mean(z * z, axis=-1, keepdims=True) + EPS) * gain


def _head_rms(zh, gain_h, n_real):
    ms = jnp.sum(zh * zh, axis=-1, keepdims=True) * (1.0 / n_real)
    return zh * lax.rsqrt(ms + EPS) * gain_h


def _rope(zh, cos, sin, is_x1):
    rot = jnp.where(is_x1, pltpu.roll(zh, LANE - 8, 1), pltpu.roll(zh, 8, 1))
    return zh * cos + rot * sin


def _is_x1(rows):
    lane = lax.broadcasted_iota(jnp.int32, (rows, LANE), 1)
    first = jnp.where(lane >= MLA_NOPE, jnp.where(lane < MLA_NOPE + 8, 1, 0), 0)
    second = jnp.where(lane >= MLA_NOPE + 16, jnp.where(lane < MLA_NOPE + 24, 1, 0), 0)
    return (first + second) > 0


def _mla_kv(ck, wk_ref, wv_ref, gk_ref, cos, sin, km_ref, vm_ref):
    rows = ck.shape[0]
    kk = jnp.dot(ck, wk_ref[...], preferred_element_type=F32)
    is_x1 = _is_x1(rows)
    for h in range(HEADS):
        sl = slice(h * LANE, (h + 1) * LANE)
        kh = _head_rms(kk[:, sl], gk_ref[:, sl], MLA_QK)
        km_ref[:, sl] = _rope(kh, cos, sin, is_x1).astype(BF16)
    vm_ref[...] = jnp.dot(ck, wv_ref[...], preferred_element_type=F32).astype(BF16)


def _in_kernel(x_ref, mod_ref, n1_ref, w_ref, wuq_ref, wk_ref, wv_ref,
               gq_ref, gk_ref, gcq_ref, gckv_ref, gqm_ref, gkm_ref, cos_ref, sin_ref, after_hbm,
               qn_ref, kn_ref, vn_ref, knf_ref, vnf_ref, p_ref,
               qm_ref, km_ref, vm_ref, ckv_ref, kr_ref):
    del after_hbm
    d = x_ref.shape[1]
    rows = x_ref.shape[0]
    mod = mod_ref[0]
    sh1 = mod[:, 0:d]
    sc1 = mod[:, d:2 * d]
    h = _rms(x_ref[...], n1_ref[...]) * (1.0 + sc1) + sh1
    hb = h.astype(BF16)

    def proj(lo, hi):
        return jnp.dot(hb, w_ref[:, lo:hi], preferred_element_type=F32)

    cos = cos_ref[...]
    sin = sin_ref[...]
    is_x1 = _is_x1(rows)

    zq = proj(_CQ, _CQ + HW)
    zk = proj(_CK, _CK + HW)
    for hh in range(HEADS):
        sl = slice(hh * LANE, (hh + 1) * LANE)
        qn_ref[:, sl] = (_head_rms(zq[:, sl], gq_ref[:, sl], NA_DH) * (NA_DH ** -0.5)).astype(BF16)
        kh = _head_rms(zk[:, sl], gk_ref[:, sl], NA_DH)
        knf_ref[:, sl] = kh
        kn_ref[:, sl] = kh.astype(BF16)
    zv = proj(_CV, _CV + HW)
    vnf_ref[...] = zv
    vn_ref[...] = zv.astype(BF16)
    p_ref[...] = proj(_CP, _CP + 256)

    cq = _rms(proj(_CCQ, _CCQ + 256), gcq_ref[...])
    zqm = jnp.dot(cq.astype(BF16), wuq_ref[...], preferred_element_type=F32)
    for hh in range(HEADS):
        sl = slice(hh * LANE, (hh + 1) * LANE)
        qh = _head_rms(zqm[:, sl], gqm_ref[:, sl], MLA_QK)
        qm_ref[:, sl] = (_rope(qh, cos, sin, is_x1) * (MLA_QK ** -0.5)).astype(BF16)

    ckv = _rms(proj(_CCKV, _CCKV + 128), gckv_ref[...])
    kr = proj(_CKR, _CKR + 128)
    ckv_ref[...] = ckv
    kr_ref[...] = kr
    ck = jnp.concatenate([ckv, kr], axis=-1).astype(BF16)
    _mla_kv(ck, wk_ref, wv_ref, gkm_ref, cos, sin, km_ref, vm_ref)


def _in_proj(x, mod_l, row_off, bpm, lw, cos_t, sin_t, rope_blocks, after):
    n, d = x.shape
    nb = n // TB
    tok = lambda w: pl.BlockSpec((TB, w), lambda i: (i, 0))
    rope_spec = pl.BlockSpec((TB, LANE), lambda i: (i % rope_blocks, 0))
    in_specs = [tok(d),
                pl.BlockSpec((1, 1, mod_l.shape[-1]), lambda i: (row_off + i // bpm, 0, 0)),
                _const_spec((1, d)), _const_spec((d, IN_W)), _const_spec((256, HW)),
                _const_spec((256, HW)), _const_spec((256, HW)),
                _const_spec((1, HW)), _const_spec((1, HW)), _const_spec((1, 256)),
                _const_spec((1, 128)), _const_spec((1, HW)), _const_spec((1, HW)),
                rope_spec, rope_spec, pl.BlockSpec(memory_space=pl.ANY)]
    widths = [(HW, BF16), (HW, BF16), (HW, BF16), (HW, F32), (HW, F32), (256, F32),
              (HW, BF16), (HW, BF16), (HW, BF16), (128, F32), (128, F32)]
    return pl.pallas_call(
        _in_kernel,
        grid=(nb,),
        in_specs=in_specs,
        out_specs=[tok(w) for w, _ in widths],
        out_shape=[jax.ShapeDtypeStruct((n, w), dt) for w, dt in widths],
        compiler_params=_params(("arbitrary",)),
        name="in_proj",
    )(x, mod_l, lw["norm1"], lw["w_in"], lw["w_uq"], lw["w_k"], lw["w_v"],
      lw["g_q"], lw["g_k"], lw["g_cq"], lw["g_ckv"], lw["g_qm"], lw["g_km"], cos_t, sin_t, after)


def _cache_kernel(ck_ref, wk_ref, wv_ref, gk_ref, km_ref, vm_ref):
    rows = ck_ref.shape[2]
    cos = jnp.ones((rows, LANE), F32)
    sin = jnp.zeros((rows, LANE), F32)
    _mla_kv(ck_ref[0, 0], wk_ref.at[0], wv_ref.at[0], gk_ref.at[0], cos, sin,
            km_ref.at[0, 0], vm_ref.at[0, 0])


def _cache_kv(ck, w_k, w_v, g_km):
    db, depth, p, _ = ck.shape
    spec = lambda w: pl.BlockSpec((1, 1, p, w), lambda b, l: (b, l, 0, 0))
    wspec = lambda r: pl.BlockSpec((1, r, HW), lambda b, l: (l, 0, 0))
    return pl.pallas_call(
        _cache_kernel,
        grid=(db, depth),
        in_specs=[spec(256), wspec(256), wspec(256), wspec(1)],
        out_specs=[spec(HW), spec(HW)],
        out_shape=[jax.ShapeDtypeStruct((db, depth, p, HW), BF16)] * 2,
        compiler_params=_params(("arbitrary", "arbitrary")),
        name="cache_kv",
    )(ck, w_k, w_v, g_km)


def _softmax_av(s_list, v_list):
    m = s_list[0].max(axis=-1, keepdims=True)
    for s in s_list[1:]:
        m = jnp.maximum(m, s.max(axis=-1, keepdims=True))
    acc = None
    den = None
    for s, v in zip(s_list, v_list):
        p = jnp.exp(s - m)
        l = p.sum(axis=-1, keepdims=True)
        o = jnp.dot(p.astype(BF16), v, preferred_element_type=F32)
        acc = o if acc is None else acc + o
        den = l if den is None else den + l
    return acc / den


def _ctx_attn_kernel(qn, kn, vn, qm, km, vm, on, om):
    for q, k, v, o in ((qn, kn, vn, on), (qm, km, vm, om)):
        for h in range(HEADS):
            sl = slice(h * LANE, (h + 1) * LANE)
            s = _nt_dot(q[:, sl], k[:, sl])
            o[:, sl] = _softmax_av([s], [v[:, sl]]).astype(BF16)


def _ctx_attn(qn, kn, vn, qm, km, vm, seq):
    n = qn.shape[0]
    spec = pl.BlockSpec((seq, HW), lambda i: (i, 0))
    return pl.pallas_call(
        _ctx_attn_kernel,
        grid=(n // seq,),
        in_specs=[spec] * 6,
        out_specs=[spec] * 2,
        out_shape=[jax.ShapeDtypeStruct((n, HW), BF16)] * 2,
        compiler_params=_params(("arbitrary",)),
        name="ctx_attn",
    )(qn, kn, vn, qm, km, vm)


def _lat_mla_kernel(q, k, v, kc, vc, o):
    s1 = _nt_dot(q[...], k[...])
    s2 = _nt_dot(q[...], kc[0])
    o[...] = _softmax_av([s1, s2], [v[...], vc[0]]).astype(BF16)


def _lat_mla(qm, km, vm, kc, vc, db):
    n = qm.shape[0]
    ds = n // db
    nq = ds // TQ
    qspec = pl.BlockSpec((TQ, LANE), lambda b, h, i: (b * nq + i, h))
    kspec = pl.BlockSpec((ds, LANE), lambda b, h, i: (b, h))
    cspec = pl.BlockSpec((1, kc.shape[1], LANE), lambda b, h, i: (b, 0, h))
    return pl.pallas_call(
        _lat_mla_kernel,
        grid=(db, HEADS, nq),
        in_specs=[qspec, kspec, kspec, cspec, cspec],
        out_specs=qspec,
        out_shape=jax.ShapeDtypeStruct((n, HW), BF16),
        compiler_params=_params(("arbitrary",) * 3),
        name="lat_mla",
    )(qm, km, vm, kc, vc)


def _nat_kernel(q, k, v, kc, vc, bias, o, *, rows):
    r = pl.program_id(1)
    rs = jnp.clip(r - WIN_R // 2, 0, rows - WIN_R)
    start = pl.multiple_of(rs * GRID_W, GRID_W)
    band = WIN_R * GRID_W
    for h in range(HEADS):
        sl = slice(h * LANE, (h + 1) * LANE)
        qh = q[:, sl]
        s1 = _nt_dot(qh, k[pl.ds(start, band), sl]) + bias[0, h]
        s2 = _nt_dot(qh, kc[0, :, sl])
        o[:, sl] = _softmax_av([s1, s2], [v[pl.ds(start, band), sl], vc[0, :, sl]]).astype(BF16)


def _nat_attn(qn, kn, vn, kc, vc, bias, db):
    n = qn.shape[0]
    ds = n // db
    rows = ds // GRID_W
    band = WIN_R * GRID_W

    def variant(r):
        return jnp.where(r < WIN_R // 2, r, jnp.where(r > rows - WIN_R // 2, r - (rows - WIN_R), WIN_R // 2))

    qspec = pl.BlockSpec((GRID_W, HW), lambda b, r: (b * rows + r, 0))
    kspec = pl.BlockSpec((ds, HW), lambda b, r: (b, 0))
    cspec = pl.BlockSpec((1, kc.shape[1], HW), lambda b, r: (b, 0, 0))
    bspec = pl.BlockSpec((1, HEADS, GRID_W, band), lambda b, r: (variant(r), 0, 0, 0))
    return pl.pallas_call(
        functools.partial(_nat_kernel, rows=rows),
        grid=(db, rows),
        in_specs=[qspec, kspec, kspec, cspec, cspec, bspec],
        out_specs=qspec,
        out_shape=jax.ShapeDtypeStruct((n, HW), BF16),
        compiler_params=_params(("arbitrary", "arbitrary")),
        name="nat_attn",
    )(qn, kn, vn, kc, vc, bias)


def _split3(x):
    hi = x.astype(BF16)
    r = x - hi.astype(F32)
    mid = r.astype(BF16)
    lo = (r - mid.astype(F32)).astype(BF16)
    return hi, mid, lo


def _pool(p_prev, p_cur, p_next, posb, seq_len):
    rows = p_cur.shape[0]
    halo = p_prev.shape[0]
    ext = rows + 2 * halo
    pext = jnp.concatenate([p_prev, p_cur, p_next], axis=0)
    parts = _split3(pext)
    t = posb + lax.broadcasted_iota(jnp.int32, (rows, ext), 0)
    s = posb - halo + lax.broadcasted_iota(jnp.int32, (rows, ext), 1)
    tcol = posb + lax.broadcasted_iota(jnp.int32, (rows, 1), 0)
    grp = lax.broadcasted_iota(jnp.int32, (rows, 256), 1) // POOL_G
    d = jnp.zeros((rows, 256), F32)
    for gi, w in enumerate(POOL_WINDOWS):
        lo = jnp.maximum(t - w // 2, 0)
        hi = jnp.minimum(t + (w - w // 2), seq_len)
        sel = jnp.where(s >= lo, jnp.where(s < hi, 1.0, 0.0), 0.0).astype(BF16)
        tot = sum(jnp.dot(sel, part, preferred_element_type=F32) for part in parts)
        cnt = (jnp.minimum(tcol + (w - w // 2), seq_len) - jnp.maximum(tcol - w // 2, 0)).astype(F32)
        d = jnp.where(grp == gi, tot / cnt - p_cur, d)
    return d


def _first_max(x, pos, sentinel):
    m = jnp.max(x, axis=0, keepdims=True)
    idx = jnp.min(jnp.where(x == m, pos, sentinel), axis=0, keepdims=True)
    return m, idx


def _topk_stage1(qh, sk_ref):
    c = qh.shape[0]
    key_pos = lax.broadcasted_iota(jnp.int32, (PEER_NKEYS, c), 0).astype(F32)
    row16 = lax.broadcasted_iota(jnp.int32, (PEER_TOPK, c), 0)
    neg = jnp.float32(-jnp.inf)
    s0 = _nt_dot(sk_ref[0], qh)
    s1 = _nt_dot(sk_ref[1], qh)

    def stage1(a, carry):
        out = []
        for s, sv, si in (carry[0:3], carry[3:6]):
            m, idx = _first_max(s, key_pos, float(PEER_NKEYS))
            out += [jnp.where(key_pos == idx, neg, s),
                    jnp.where(row16 == a, m, sv), jnp.where(row16 == a, idx, si)]
        return tuple(out)

    zf = jnp.zeros((PEER_TOPK, c), F32)
    _, sv0, si0, _, sv1, si1 = lax.fori_loop(0, PEER_TOPK, stage1, (s0, zf, zf, s1, zf, zf))
    return sv0, si0, sv1, si1


def _topk_pieces(sv0, sv1):
    c = sv0.shape[1]
    neg = jnp.float32(-jnp.inf)
    sub8 = lax.broadcasted_iota(jnp.int32, (8, c), 0)
    sub8f = sub8.astype(F32)
    cs, cf = [], []
    for a in range(8):
        nb = PEER_TOPK // (a + 1)
        for b0 in range(0, nb, 8):
            val = sv0[a:a + 1] + sv1[b0:b0 + 8]
            if nb - b0 < 8:
                val = jnp.where(sub8 < nb - b0, val, neg)
            cs.append(val)
            cf.append(sub8f + float(a * PEER_TOPK + b0))
    cs.append(sv0[8:16] + sv1[0:1])
    cf.append((sub8f + 8.0) * float(PEER_TOPK))
    return cs, cf


def _topk_stage2(chains, cf):
    npc = len(cf)
    c = cf[0].shape[1]
    row16 = lax.broadcasted_iota(jnp.int32, (PEER_TOPK, c), 0)
    neg = jnp.float32(-jnp.inf)
    nflat = float(PEER_TOPK * PEER_TOPK)
    zf = jnp.zeros((PEER_TOPK, c), F32)

    def step(k, carry):
        out = []
        for ch in range(len(chains)):
            vals = carry[ch * (npc + 2):ch * (npc + 2) + npc]
            tv, tp = carry[ch * (npc + 2) + npc], carry[ch * (npc + 2) + npc + 1]
            m = vals[0]
            for v in vals[1:]:
                m = jnp.maximum(m, v)
            m = jnp.max(m, axis=0, keepdims=True)
            pos = None
            for v, f in zip(vals, cf):
                cand = jnp.where(v == m, f, nflat)
                pos = cand if pos is None else jnp.minimum(pos, cand)
            pos = jnp.min(pos, axis=0, keepdims=True)
            out += [jnp.where(f == pos, neg, v) for v, f in zip(vals, cf)]
            out += [jnp.where(row16 == k, m, tv), jnp.where(row16 == k, pos, tp)]
        return tuple(out)

    init = []
    for cs in chains:
        init += list(cs) + [zf, zf]
    res = lax.fori_loop(0, PEER_TOPK, step, tuple(init))
    return [(res[ch * (npc + 2) + npc], res[ch * (npc + 2) + npc + 1]) for ch in range(len(chains))]


def _topk_ids(tp, si0, si1):
    a = jnp.floor(tp * (1.0 / PEER_TOPK))
    b = tp - a * float(PEER_TOPK)
    ea = jnp.zeros_like(tp)
    eb = jnp.zeros_like(tp)
    for j in range(PEER_TOPK):
        ea = jnp.where(a == float(j), si0[j:j + 1], ea)
        eb = jnp.where(b == float(j), si1[j:j + 1], eb)
    return ea * float(PEER_NKEYS) + eb


def _out_kernel(on_ref, om_ref, pc_ref, pp_ref, pn_ref, x_ref, mod_ref,
                won_ref, wop_ref, wom_ref, pw_ref, ps_ref, n2_ref, wq_ref, sk_ref,
                x1_ref, h2_ref, ids_ref, gt_ref, gn_ref, h2c_ref, q_scr, idt_scr, *, bps, seq_len):
    d = x_ref.shape[1]
    rows = x_ref.shape[0]
    i = pl.program_id(0)
    mod = mod_ref[0]
    g1 = mod[:, 2 * d:3 * d]
    sh2 = mod[:, 3 * d:4 * d]
    sc2 = mod[:, 4 * d:5 * d]

    posb = (i % bps) * rows
    dpool = _pool(pp_ref[...], pc_ref[...], pn_ref[...], posb, seq_len)
    ypool = jnp.dot(dpool.astype(BF16), pw_ref[...], preferred_element_type=F32) * ps_ref[...]
    mix = (jnp.dot(on_ref[...], won_ref[...], preferred_element_type=F32)
           + jnp.dot(ypool.astype(BF16), wop_ref[...], preferred_element_type=F32)
           + jnp.dot(om_ref[...], wom_ref[...], preferred_element_type=F32))
    x1 = x_ref[...] + g1 * mix
    x1_ref[...] = x1
    h2 = _rms(x1, n2_ref[...]) * (1.0 + sc2) + sh2
    h2_ref[...] = h2
    for j in range(d // LANE):
        h2c_ref[j] = h2[:, j * LANE:(j + 1) * LANE]

    q = jnp.dot(h2.astype(BF16), wq_ref[...], preferred_element_type=F32)
    for hh in range(PEER_HEADS):
        q_scr[hh] = q[:, hh * LANE:(hh + 1) * LANE].astype(BF16)

    chunks = range(0, rows, LANE)

    def head(hh, _):
        sorted_keys = [_topk_stage1(q_scr[hh, c0:c0 + LANE, :], sk_ref) for c0 in chunks]
        pieces = [_topk_pieces(sv0, sv1) for sv0, _, sv1, _ in sorted_keys]
        picked = _topk_stage2([cs for cs, _ in pieces], pieces[0][1])
        r0 = pl.multiple_of(hh * PEER_TOPK, PEER_TOPK)
        for c0, (tv, tp), (_, si0, _, si1) in zip(chunks, picked, sorted_keys):
            ex = jnp.exp(tv - tv[0:1])
            gt_ref[pl.ds(r0, PEER_TOPK), c0:c0 + LANE] = ex / jnp.sum(ex, axis=0, keepdims=True)
            idt_scr[pl.ds(r0, PEER_TOPK), c0:c0 + LANE] = _topk_ids(tp, si0, si1)
        return 0

    lax.fori_loop(0, PEER_HEADS, head, 0)
    ids_ref[...] = idt_scr[...].T.astype(jnp.int32)
    gn_ref[...] = gt_ref[...].T


def _out_proj(x, on, om, p, mod_l, row_off, bpm, lw, seq_len):
    n, d = x.shape
    nb = n // TB
    bps = seq_len // TB
    halo = 8
    hb = TB // halo
    tok = lambda w: pl.BlockSpec((TB, w), lambda i: (i, 0))
    in_specs = [tok(HW), tok(HW), tok(256),
                pl.BlockSpec((halo, 256), lambda i: (jnp.maximum(i * hb - 1, 0), 0)),
                pl.BlockSpec((halo, 256), lambda i: (jnp.minimum((i + 1) * hb, n // halo - 1), 0)),
                tok(d),
                pl.BlockSpec((1, 1, mod_l.shape[-1]), lambda i: (row_off + i // bpm, 0, 0)),
                _const_spec((HW, d)), _const_spec((256, d)), _const_spec((HW, d)),
                _const_spec((256, 256)), _const_spec((1, 256)), _const_spec((1, d)),
                _const_spec((d, PEER_HEADS * LANE)), _const_spec((2, PEER_NKEYS, LANE))]
    nk = PEER_HEADS * PEER_TOPK
    return pl.pallas_call(
        functools.partial(_out_kernel, bps=bps, seq_len=seq_len),
        grid=(nb,),
        in_specs=in_specs,
        out_specs=[tok(d), tok(d), tok(nk), pl.BlockSpec((nk, TB), lambda i: (0, i)), tok(nk),
                   pl.BlockSpec((d // LANE, TB, LANE), lambda i: (0, i, 0))],
        out_shape=[jax.ShapeDtypeStruct((n, d), F32), jax.ShapeDtypeStruct((n, d), F32),
                   jax.ShapeDtypeStruct((n, nk), jnp.int32), jax.ShapeDtypeStruct((nk, n), F32),
                   jax.ShapeDtypeStruct((n, nk), F32),
                   jax.ShapeDtypeStruct((d // LANE, n, LANE), F32)],
        scratch_shapes=[pltpu.VMEM((PEER_HEADS, TB, LANE), BF16), pltpu.VMEM((nk, TB), F32)],
        compiler_params=_params(("arbitrary",)),
        name="out_proj",
    )(on, om, p, p, p, x, mod_l, lw["w_o_na"], lw["w_o_pool"], lw["w_o_mla"],
      lw["pool_w"], lw["pool_scale"], lw["norm2"], lw["peer_wq"], lw["peer_sk"])


def _gelu_tanh(x):
    return x * (0.5 * (1.0 + jnp.tanh(0.7978845608028654 * (x + 0.044715 * (x * x * x)))))


def _peer_token_mix(chunk, hrow, gcol, ch):
    acc = None
    for s in range(ch):
        us = lax.bitcast_convert_type(chunk(s) & jnp.int32(-65536), F32)
        term = us * hrow[:, s * LANE:(s + 1) * LANE]
        acc = term if acc is None else acc + term
    wgt = gcol * _gelu_tanh(jnp.sum(acc, axis=-1, keepdims=True))
    parts = []
    for s in range(ch):
        vs = lax.bitcast_convert_type(chunk(s) << 16, F32)
        parts.append(jnp.sum(vs * wgt, axis=0, keepdims=True))
    return jnp.concatenate(parts, axis=-1)


def _peer_staged_kernel(rows_ref, gt_ref, h2_ref, x1_ref, mod_ref, x2_hbm, o_ref, *, tok0):
    del x2_hbm
    d = x1_ref.shape[1]
    ch = d // LANE
    nk = gt_ref.shape[0]
    g2 = mod_ref[0][:, 5 * d:6 * d]
    tok_lane = lax.broadcasted_iota(jnp.int32, gt_ref.shape, 1)
    base = (tok0 + pl.program_id(0) * PEER_SUB) % PEER_TB
    h8 = h2_ref[...]
    ys = []
    for t in range(PEER_SUB):
        chunk = lambda s: rows_ref[pl.ds(t * nk * ch + s, nk, stride=ch), :]
        gcol = jnp.sum(jnp.where(tok_lane == base + t, gt_ref[...], 0.0), axis=-1, keepdims=True)
        ys.append(_peer_token_mix(chunk, h8[t:t + 1, :], gcol, ch))
    o_ref[...] = x1_ref[...] + g2 * jnp.concatenate(ys, axis=0)


def _peer_staged(x1, h2, rows, gt, mod_l, row, tok0, x2):
    n, d = x1.shape
    nk = gt.shape[0]
    per = PEER_SUB * nk * (d // LANE)
    j0 = tok0 // PEER_SUB
    tok = pl.BlockSpec((PEER_SUB, d), lambda j: (j + j0, 0))
    return pl.pallas_call(
        functools.partial(_peer_staged_kernel, tok0=tok0),
        grid=(rows.shape[0] // per,),
        in_specs=[pl.BlockSpec((per, LANE), lambda j: (j, 0)),
                  pl.BlockSpec((nk, PEER_TB), lambda j: (0, (tok0 + j * PEER_SUB) // PEER_TB)),
                  tok, tok,
                  pl.BlockSpec((1, 1, mod_l.shape[-1]), lambda j: (row, 0, 0)),
                  pl.BlockSpec(memory_space=pl.ANY)],
        out_specs=tok,
        out_shape=jax.ShapeDtypeStruct((n, d), F32),
        input_output_aliases={5: 0},
        compiler_params=_params(("arbitrary",)),
        name="peer_staged",
    )(rows, gt, h2, x1, mod_l, x2)


def _sc_peer(table3, ids, gates, h2c, n, n_st):
    ch, _, lane = h2c.shape
    nk = ids.shape[1]
    info = plsc.get_sparse_core_info()
    nc, nw, nl = info.num_cores, info.num_cores * info.num_subcores, info.num_lanes
    tpw = n // nw
    spw = n_st // nw
    win = 32
    nq = nk // win
    cpr = lane // nl
    nchunk = ch * cpr
    hc = nchunk // 2
    assert n % nw == 0 and nk % win == 0 and win % nl == 0 and nq == 4
    assert n_st % nw == 0 and spw * nq == 2 * tpw
    mesh = plsc.VectorSubcoreMesh(core_axis_name="core", subcore_axis_name="subcore")
    hi_mask = jnp.int32(-65536)

    @functools.partial(
        pl.kernel, mesh=mesh,
        out_type=(jax.ShapeDtypeStruct((ch, n, lane), F32),
                  jax.ShapeDtypeStruct((n_st * nk, ch, lane), jnp.int32)),
        compiler_params=pltpu.CompilerParams(needs_layout_passes=False),
        scratch_types=[pltpu.VMEM((nk,), jnp.int32), pltpu.VMEM((nk,), F32),
                       pltpu.VMEM((ch, lane), F32), pltpu.VMEM((ch, lane), F32),
                       pltpu.VMEM((win, ch, lane), jnp.int32), pltpu.VMEM((win, ch, lane), jnp.int32),
                       pltpu.VMEM((win * nl,), F32), pltpu.VMEM((win,), F32),
                       pltpu.VMEM((nk,), jnp.int32), pltpu.VMEM((win, ch, lane), jnp.int32),
                       pltpu.SemaphoreType.DMA, pltpu.SemaphoreType.DMA, pltpu.SemaphoreType.DMA,
                       pltpu.SemaphoreType.DMA, pltpu.SemaphoreType.DMA])
    def peer(tab_hbm, ids_hbm, g_hbm, h2_hbm, y_hbm, stage_hbm,
             idx_v, g_v, x_v, y_v, rows_a, rows_b, part_v, w_v, idx_s, rows_c,
             sem_a, sem_b, sem_x, sem_g, sem_w):
        wid = lax.axis_index("subcore") * nc + lax.axis_index("core")
        bufs = ((rows_a, sem_a), (rows_b, sem_b))
        lanes = lax.iota(jnp.int32, nl)
        zero = jnp.zeros((nl,), F32)

        def stage_gather(w):
            src = tab_hbm.at[idx_s.at[pl.ds(pl.multiple_of(w * win, win), win)]]
            return pltpu.make_async_copy(src, rows_c, sem_g)

        def stage_write(s, w):
            dst = stage_hbm.at[pl.ds(pl.multiple_of((s * nq + w) * win, win), win)]
            return pltpu.make_async_copy(rows_c, dst, sem_w)

        def chunk_copies(tok, to_hbm):
            if to_hbm:
                return [pltpu.make_async_copy(y_v.at[j], y_hbm.at[j, tok], sem_x) for j in range(ch)]
            return [pltpu.make_async_copy(h2_hbm.at[j, tok], x_v.at[j], sem_x) for j in range(ch)]

        def fetch(q):
            rows, sem = bufs[q % 2]
            return pltpu.make_async_copy(tab_hbm.at[idx_v.at[pl.ds(q * win, win)]], rows, sem)

        def word(rows, r, cc):
            return rows[r, cc // cpr, pl.ds((cc % cpr) * nl, nl)]

        @pl.loop(0, tpw)
        def _(ti):
            tok = wid * tpw + ti
            loads = chunk_copies(tok, False)
            for cp in loads:
                cp.start()
            pltpu.sync_copy(ids_hbm.at[tok], idx_v)
            pltpu.sync_copy(g_hbm.at[tok], g_v)
            for cp in loads:
                cp.wait()
            for cc in range(nchunk):
                y_v[cc // cpr, pl.ds((cc % cpr) * nl, nl)] = zero
            fetch(0).start()
            st_tok = wid * spw + ti // 2
            w0 = (ti % 2) * 2
            for q in range(nq):
                rows = bufs[q % 2][0]
                fetch(q).wait()
                if q + 1 < nq:
                    fetch(q + 1).start()

                if q == 0:
                    @pl.when(ti > 0)
                    def _():
                        stage_write(st_tok, w0).wait()

                    @pl.when(ti % 2 == 0)
                    def _():
                        pltpu.sync_copy(ids_hbm.at[n + st_tok], idx_s)

                    stage_gather(w0).start()
                elif q == 1:
                    stage_gather(w0).wait()
                    stage_write(st_tok, w0).start()
                elif q == 2:
                    stage_write(st_tok, w0).wait()
                    stage_gather(w0 + 1).start()
                else:
                    stage_gather(w0 + 1).wait()
                    stage_write(st_tok, w0 + 1).start()

                for half in range(2):
                    xs = [x_v[(half * hc + c) // cpr, pl.ds(((half * hc + c) % cpr) * nl, nl)]
                          for c in range(hc)]

                    @pl.loop(0, win)
                    def _(r):
                        accs = [None] * 4
                        for c in range(hc):
                            u = lax.bitcast_convert_type(word(rows, r, half * hc + c) & hi_mask, F32)
                            t = u * xs[c]
                            accs[c % 4] = t if accs[c % 4] is None else accs[c % 4] + t
                        acc = (accs[0] + accs[1]) + (accs[2] + accs[3])
                        po = pl.multiple_of(r * nl, nl)
                        if half == 0:
                            part_v[pl.ds(po, nl)] = acc
                        else:
                            part_v[pl.ds(po, nl)] = part_v[pl.ds(po, nl)] + acc

                for grp in range(win // nl):
                    s = zero
                    for rr in range(nl):
                        tot = jnp.sum(part_v[pl.ds((grp * nl + rr) * nl, nl)])
                        s = jnp.where(lanes == rr, tot, s)
                    z = 0.7978845608028654 * (s + 0.044715 * (s * s * s))
                    tanh = 1.0 - 2.0 / (jnp.exp(2.0 * z) + 1.0)
                    gate = g_v[pl.ds(q * win + grp * nl, nl)]
                    w_v[pl.ds(grp * nl, nl)] = gate * (s * (0.5 * (1.0 + tanh)))

                for half in range(2):
                    def body(r, yacc):
                        wr = plsc.load_gather(w_v, [jnp.full((nl,), r, jnp.int32)])
                        out = []
                        for c in range(hc):
                            v = lax.bitcast_convert_type(word(rows, r, half * hc + c) << 16, F32)
                            out.append(yacc[c] + wr * v)
                        return tuple(out)

                    yacc = lax.fori_loop(0, win, body, tuple(zero for _ in range(hc)))
                    for c in range(hc):
                        cc = half * hc + c
                        sl = (cc // cpr, pl.ds((cc % cpr) * nl, nl))
                        y_v[sl] = y_v[sl] + yacc[c]
            stores = chunk_copies(tok, True)
            for cp in stores:
                cp.start()
            for cp in stores:
                cp.wait()

        stage_write(0, 0).wait()

    return peer(table3, ids, gates, h2c)


def _residual_kernel(x1_ref, y_ref, mod_ref, x2_hbm, after_hbm, o_ref):
    del x2_hbm
    del after_hbm
    d = x1_ref.shape[1]
    g2 = mod_ref[0][:, 5 * d:6 * d]
    for j in range(d // LANE):
        sl = slice(j * LANE, (j + 1) * LANE)
        o_ref[:, sl] = x1_ref[:, sl] + g2[:, sl] * y_ref[j]


def _residual(x1, y, mod_l, row, x2, after):
    n, d = x1.shape
    tok = pl.BlockSpec((PEER_TB, d), lambda i: (i, 0))
    any_spec = pl.BlockSpec(memory_space=pl.ANY)
    return pl.pallas_call(
        _residual_kernel,
        grid=(y.shape[1] // PEER_TB,),
        in_specs=[tok, pl.BlockSpec((d // LANE, PEER_TB, LANE), lambda i: (0, i, 0)),
                  pl.BlockSpec((1, 1, mod_l.shape[-1]), lambda i: (row, 0, 0)),
                  any_spec, any_spec],
        out_specs=tok,
        out_shape=jax.ShapeDtypeStruct((n, d), F32),
        input_output_aliases={3: 0},
        compiler_params=_params(("arbitrary",)),
        name="residual",
    )(x1, y, mod_l, x2, after)


def _peer_kernel(ids_hbm, gt_ref, h2_ref, x1_ref, mod_ref, tab_hbm, o_ref,
                 ids_s, buf, sem_i, sem_r, *, first_block):
    d = x1_ref.shape[1]
    ch = d // LANE
    pitch = ch + 1
    nsub = x1_ref.shape[0] // PEER_SUB
    nk = gt_ref.shape[0]
    nids = PEER_SUB * nk
    i = pl.program_id(0) + first_block
    g2 = mod_ref[0][:, 5 * d:6 * d]
    tok_lane = lax.broadcasted_iota(jnp.int32, gt_ref.shape, 1)

    def ids_copy(j, slot):
        start = pl.multiple_of((i * nsub + j) * nids, nids)
        return pltpu.make_async_copy(ids_hbm.at[pl.ds(start, nids)],
                                     ids_s.at[pl.ds(slot * nids, nids)], sem_i.at[slot])

    def row_copy(slot, e, f):
        src = tab_hbm.at[pl.ds(pl.multiple_of(e * ch, ch), ch), :]
        dst = buf.at[slot, pl.ds(f * pitch, ch), :]
        return pltpu.make_async_copy(src, dst, sem_r.at[slot])

    def issue_rows(slot):
        for t in range(PEER_SUB):
            def body(kk, _):
                for r in range(8):
                    f = t * nk + kk * 8 + r
                    row_copy(slot, ids_s[slot * nids + f], f).start(priority=r % 2)
                return 0

            lax.fori_loop(0, nk // 8, body, 0)

    def wait_rows(slot):
        done = buf.at[slot, pl.ds(0, nids * ch), :]
        pltpu.make_async_copy(done, done, sem_r.at[slot]).wait()

    def compute(slot, j):
        base = pl.multiple_of(j * PEER_SUB, PEER_SUB)
        h8 = h2_ref[pl.ds(base, PEER_SUB), :]
        ys = []
        for t in range(PEER_SUB):
            chunk = lambda s: buf[slot, pl.ds(t * nk * pitch + s, nk, stride=pitch), :]
            gcol = jnp.sum(jnp.where(tok_lane == base + t, gt_ref[...], 0.0), axis=-1, keepdims=True)
            ys.append(_peer_token_mix(chunk, h8[t:t + 1, :], gcol, ch))
        y8 = jnp.concatenate(ys, axis=0)
        o_ref[pl.ds(base, PEER_SUB), :] = x1_ref[pl.ds(base, PEER_SUB), :] + g2 * y8

    first = ids_copy(0, 0)
    first.start()
    first.wait()
    issue_rows(0)
    ids_copy(1, 1).start()

    def pair(jj, _):
        j0 = 2 * jj
        ids_copy(j0 + 1, 1).wait()
        issue_rows(1)

        @pl.when(j0 + 2 < nsub)
        def _():
            ids_copy(j0 + 2, 0).start()

        wait_rows(0)
        compute(0, j0)

        @pl.when(j0 + 2 < nsub)
        def _():
            ids_copy(j0 + 2, 0).wait()
            issue_rows(0)

        @pl.when(j0 + 3 < nsub)
        def _():
            ids_copy(j0 + 3, 1).start()

        wait_rows(1)
        compute(1, j0 + 1)
        return 0

    lax.fori_loop(0, nsub // 2, pair, 0)


def _pack_tables(peer_u, peer_v):
    e, d = peer_u.shape
    ub = lax.bitcast_convert_type(peer_u.astype(BF16), jnp.uint16).astype(jnp.uint32)
    vb = lax.bitcast_convert_type(peer_v.astype(BF16), jnp.uint16).astype(jnp.uint32)
    words = lax.bitcast_convert_type((ub << 16) | vb, jnp.int32)
    return words.reshape(e, d // LANE, LANE)


def _peer(x1, h2, ids, gt, mod_l, row, table, tok0):
    n, d = x1.shape
    nk = gt.shape[0]
    b0 = tok0 // PEER_TB
    nb = n // PEER_TB - b0
    tok = pl.BlockSpec((PEER_TB, d), lambda i: (i + b0, 0))
    any_spec = pl.BlockSpec(memory_space=pl.ANY)
    return pl.pallas_call(
        functools.partial(_peer_kernel, first_block=b0),
        grid=(nb,),
        in_specs=[any_spec,
                  pl.BlockSpec((nk, PEER_TB), lambda i: (0, i + b0)),
                  tok, tok,
                  pl.BlockSpec((1, 1, mod_l.shape[-1]), lambda i: (row, 0, 0)),
                  any_spec],
        out_specs=tok,
        out_shape=jax.ShapeDtypeStruct((n, d), F32),
        scratch_shapes=[pltpu.SMEM((2 * PEER_SUB * nk,), jnp.int32),
                        pltpu.VMEM((2, PEER_SUB * nk * (d // LANE + 1), LANE), jnp.int32),
                        pltpu.SemaphoreType.DMA((2,)),
                        pltpu.SemaphoreType.DMA((2,))],
        compiler_params=_params(("arbitrary",)),
        name="peer",
    )(ids.reshape(n * nk), gt, h2, x1, mod_l, table.reshape(-1, LANE))


def _pad_heads(w, width):
    pad = [(0, 0)] * (w.ndim - 1) + [(0, LANE - width)]
    w = jnp.pad(w, pad)
    return w.reshape(w.shape[:-2] + (HW,))


def _head_gain(g, width):
    depth = g.shape[0]
    g = jnp.pad(g, ((0, 0), (0, LANE - width)))
    return jnp.tile(g, (1, HEADS)).reshape(depth, 1, HW)


def _rope_tables(seq):
    t = np.arange(seq)
    half = MLA_ROPE // 2
    inv = ROPE_THETA ** (-np.arange(0, half, 2, dtype=np.float32) / half)
    cos = np.ones((seq, LANE), np.float32)
    sin = np.zeros((seq, LANE), np.float32)
    for off, pos in ((MLA_NOPE, t // GRID_W), (MLA_NOPE + half, t % GRID_W)):
        ang = pos.astype(np.float32)[:, None] * inv[None, :]
        q = half // 2
        cos[:, off:off + q] = np.cos(ang)
        cos[:, off + q:off + half] = np.cos(ang)
        sin[:, off:off + q] = -np.sin(ang)
        sin[:, off + q:off + half] = np.sin(ang)
    return jnp.asarray(cos), jnp.asarray(sin)


def _nat_bias(rel_bias):
    v = np.arange(WIN_R)[:, None]
    j = np.arange(WIN_R)[None, :]
    dr = j - v + WIN_R - 1
    cq = np.arange(GRID_W)[:, None]
    kc = np.arange(GRID_W)[None, :]
    cstart = np.clip(cq - WIN_C // 2, 0, GRID_W - WIN_C)
    ok = (kc >= cstart) & (kc < cstart + WIN_C)
    dc = np.clip(kc - cq + WIN_C - 1, 0, 2 * WIN_C - 2)
    b = rel_bias[:, :, dr]
    b = b[..., dc]
    b = jnp.where(jnp.asarray(ok)[None, None, None, None], b, NEG_INF)
    b = jnp.transpose(b, (0, 2, 1, 4, 3, 5))
    return b.reshape(b.shape[0], WIN_R, HEADS, GRID_W, WIN_R * GRID_W)


def _layer_weights(w_in, na_q_norm, na_k_norm, mla_cq_norm, mla_ckv_norm, mla_w_uq, mla_w_ukv,
                   mla_q_norm, mla_k_norm, w_out, pool_w, pool_scale, norm1, norm2,
                   peer_wq, peer_subkeys):
    depth, d, _ = w_in.shape
    na_w = HEADS * NA_DH
    segs = np.cumsum([0, na_w, na_w, na_w, 256, 256, 128, MLA_ROPE])
    part = lambda i: w_in[:, :, segs[i]:segs[i + 1]]
    heads = lambda w: _pad_heads(w.reshape(depth, d, HEADS, NA_DH), NA_DH)
    w_in_p = jnp.concatenate(
        [heads(part(0)), heads(part(1)), heads(part(2)), part(3), part(4), part(5),
         jnp.pad(part(6), ((0, 0), (0, 0), (0, LANE - MLA_ROPE)))], axis=-1).astype(BF16)

    w_uq = _pad_heads(mla_w_uq, MLA_QK).astype(BF16)
    k_nope = _pad_heads(mla_w_ukv[..., :MLA_NOPE], MLA_NOPE)
    eye = np.zeros((MLA_ROPE, HEADS, LANE), np.float32)
    for h in range(HEADS):
        eye[np.arange(MLA_ROPE), h, MLA_NOPE + np.arange(MLA_ROPE)] = 1.0
    eye = jnp.broadcast_to(jnp.asarray(eye.reshape(MLA_ROPE, HW)), (depth, MLA_ROPE, HW))
    zer = jnp.zeros((depth, 256 - 128 - MLA_ROPE, HW), F32)
    w_k = jnp.concatenate([k_nope, eye, zer], axis=1).astype(BF16)
    w_v = jnp.concatenate([_pad_heads(mla_w_ukv[..., MLA_NOPE:], MLA_V),
                           jnp.zeros((depth, 128, HW), F32)], axis=1).astype(BF16)

    mix_w = HEADS * NA_DH
    w_o_na = jnp.pad(w_out[:, :mix_w].reshape(depth, HEADS, NA_DH, d),
                     ((0, 0), (0, 0), (0, LANE - NA_DH), (0, 0))).reshape(depth, HW, d).astype(BF16)
    w_o_pool = w_out[:, mix_w:mix_w + 256].astype(BF16)
    w_o_mla = jnp.pad(w_out[:, mix_w + 256:].reshape(depth, HEADS, MLA_V, d),
                      ((0, 0), (0, 0), (0, LANE - MLA_V), (0, 0))).reshape(depth, HW, d).astype(BF16)
    ng = len(POOL_WINDOWS)
    pw = jnp.zeros((depth, ng * POOL_G, ng * POOL_G), F32)
    for g in range(ng):
        pw = pw.at[:, g * POOL_G:(g + 1) * POOL_G, g * POOL_G:(g + 1) * POOL_G].set(pool_w[:, g])

    half = peer_subkeys.shape[-1]
    sk = jnp.stack([jnp.pad(peer_subkeys[:, 0], ((0, 0), (0, 0), (0, LANE - half))),
                    jnp.pad(peer_subkeys[:, 1], ((0, 0), (0, 0), (LANE - half, 0)))], axis=1).astype(BF16)

    return dict(
        w_in=w_in_p, w_uq=w_uq, w_k=w_k, w_v=w_v,
        g_q=_head_gain(na_q_norm, NA_DH), g_k=_head_gain(na_k_norm, NA_DH),
        g_cq=mla_cq_norm[:, None, :], g_ckv=mla_ckv_norm[:, None, :],
        g_qm=_head_gain(mla_q_norm, MLA_QK), g_km=_head_gain(mla_k_norm, MLA_QK),
        w_o_na=w_o_na, w_o_pool=w_o_pool, w_o_mla=w_o_mla,
        pool_w=pw.astype(BF16), pool_scale=pool_scale[:, None, :],
        norm1=norm1[:, None, :], norm2=norm2[:, None, :],
        peer_wq=peer_wq.astype(BF16), peer_sk=sk)


def kernel(x_prompt, x_sample, c, cache_nat_k, cache_nat_v, cache_mla_ckv, cache_mla_krope, c_ctx, w_mod, b_mod, norm1, norm2, w_in, na_q_norm, na_k_norm, na_rel_bias, pool_w, pool_scale, mla_cq_norm, mla_ckv_norm, mla_w_uq, mla_w_ukv, mla_q_norm, mla_k_norm, w_out, peer_wq, peer_subkeys, peer_u, peer_v):
    batch, seq, d = x_prompt.shape
    db, ds, _ = x_sample.shape
    depth = w_mod.shape[0]
    past = cache_nat_k.shape[2]
    assert seq == TB and ds % TB == 0 and ds % (GRID_W * WIN_R) == 0 and db + 1 <= 8

    cond8 = jnp.concatenate([c_ctx[None, :], c, jnp.zeros((8 - 1 - db, d), F32)], axis=0)
    mod = _modulation(cond8, w_mod, b_mod).reshape(depth, 8, 1, 6 * d)

    lw_all = _layer_weights(w_in, na_q_norm, na_k_norm, mla_cq_norm, mla_ckv_norm, mla_w_uq,
                            mla_w_ukv, mla_q_norm, mla_k_norm, w_out, pool_w, pool_scale,
                            norm1, norm2, peer_wq, peer_subkeys)
    bias_all = _nat_bias(na_rel_bias)
    tables = [_pack_tables(peer_u[l], peer_v[l]) for l in range(depth)]
    cos_lat, sin_lat = _rope_tables(ds)
    cos_ctx = jnp.ones((TB, LANE), F32)
    sin_ctx = jnp.zeros((TB, LANE), F32)

    ck = jnp.concatenate([cache_mla_ckv, cache_mla_krope,
                          jnp.zeros(cache_mla_ckv.shape[:-1] + (256 - 128 - MLA_ROPE,), F32)],
                         axis=-1).astype(BF16)
    kc_mla, vc_mla = _cache_kv(ck, lw_all["w_k"], lw_all["w_v"], lw_all["g_km"])
    kc_na = _pad_heads(cache_nat_k, NA_DH).astype(BF16)
    vc_na = _pad_heads(cache_nat_v, NA_DH).astype(BF16)

    xs = [x_prompt.reshape(batch * seq, d)] + [x_sample[b] for b in range(db)]
    one_row = max(batch * seq, ds) + 1
    lat_bpm = ds // TB
    ks, vs, ckvs, krs = [], [], [], []
    pending = None
    after = xs[0]

    def join(item, follow):
        si, x1, h2, gt, y_sc, staged, n_sc, x2, mod_l = item
        x2 = _peer_staged(x1, h2, staged.reshape(-1, LANE), gt, mod_l, si, n_sc, x2)
        xs[si] = _residual(x1, y_sc, mod_l, si, x2, follow)
        return xs[si]

    for l in range(depth):
        lw = {k: v[l] for k, v in lw_all.items()}
        mod_l = mod[l]
        for si in range(db + 1):
            x = xs[si]
            if si == 0:
                (qn, kn, vn, knf, vnf, p, qm, km, vm, ckv, kr) = _in_proj(
                    x, mod_l, 0, one_row, lw, cos_ctx, sin_ctx, 1, after)
                on, om = _ctx_attn(qn, kn, vn, qm, km, vm, seq)
                x1, h2, ids, gt, gn, h2c = _out_proj(x, on, om, p, mod_l, 0, one_row, lw, seq)
                ks.append(knf.reshape(batch, seq, HEADS, LANE)[..., :NA_DH])
                vs.append(vnf.reshape(batch, seq, HEADS, LANE)[..., :NA_DH])
                ckvs.append(ckv.reshape(batch, seq, 128))
                krs.append(kr.reshape(batch, seq, LANE)[..., :MLA_ROPE])
            else:
                b = si - 1
                (qn, kn, vn, _, _, p, qm, km, vm, _, _) = _in_proj(
                    x, mod_l, si, one_row, lw, cos_lat, sin_lat, lat_bpm, after)
                on = _nat_attn(qn, kn, vn, kc_na[b:b + 1, l], vc_na[b:b + 1, l], bias_all[l], 1)
                om = _lat_mla(qm, km, vm, kc_mla[b:b + 1, l], vc_mla[b:b + 1, l], 1)
                x1, h2, ids, gt, gn, h2c = _out_proj(x, on, om, p, mod_l, si, one_row, lw, ds)
            n_sc = x.shape[0] * SC_SHARE[0] // SC_SHARE[1] // PEER_TB * PEER_TB
            n_st = n_sc // 2
            y_sc, staged = _sc_peer(tables[l], ids, gn, h2c, n_sc, n_st)
            x2 = _peer(x1, h2, ids, gt, mod_l, si, tables[l], n_sc + n_st)
            after = x2 if pending is None else join(pending, x2)
            pending = (si, x1, h2, gt, y_sc, staged, n_sc, x2, mod_l)
    join(pending, pending[1])

    return (xs[0].reshape(batch, seq, d), jnp.stack(xs[1:], axis=0),
            jnp.stack(ks, axis=1), jnp.stack(vs, axis=1),
            jnp.stack(ckvs, axis=1), jnp.stack(krs, axis=1))
```

```python
import functools

import numpy as np
import jax
import jax.numpy as jnp
from jax import lax
from jax.experimental import pallas as pl
from jax.experimental.pallas import tpu as pltpu
from jax.experimental.pallas import tpu_sc as plsc

F32 = jnp.float32
BF16 = jnp.bfloat16

EPS = 1e-6
ROPE_THETA = 10000.0
NEG_INF = -1e30
GRID_W = 64
HEADS = 6
NA_DH = 64
WIN_R = 8
WIN_C = 16
POOL_WINDOWS = (2, 4, 8, 16)
POOL_G = 64
MLA_NOPE = 64
MLA_ROPE = 32
MLA_QK = MLA_NOPE + MLA_ROPE
MLA_V = 64
PEER_HEADS = 8
PEER_NKEYS = 128
PEER_TOPK = 16
LANE = 128
HW = HEADS * LANE
TB = 256
TQ = 256
PEER_TB = 128
PEER_SUB = 8
VMEM_LIMIT = 56 * 1024 * 1024
SC_SHARE = (3, 4)

_CQ, _CK, _CV = 0, HW, 2 * HW
_CP = 3 * HW
_CCQ = _CP + 256
_CCKV = _CCQ + 256
_CKR = _CCKV + 128
IN_W = _CKR + 128


def _params(sem, vmem=VMEM_LIMIT):
    return pltpu.CompilerParams(dimension_semantics=sem, vmem_limit_bytes=vmem)


def _const_spec(shape):
    n = len(shape)
    return pl.BlockSpec(shape, lambda *_: (0,) * n)


def _nt_dot(a, b):
    return lax.dot_general(a, b, (((1,), (1,)), ((), ())), preferred_element_type=F32)


def _mod_kernel(c_ref, w_ref, b_ref, o_ref):
    c = c_ref[...]
    s = c / (1.0 + jnp.exp(-c))
    o_ref[0] = jnp.dot(s, w_ref[0], preferred_element_type=F32,
                       precision=lax.Precision.HIGHEST) + b_ref[0]


def _modulation(cond8, w_mod, b_mod):
    depth, d, n6 = w_mod.shape
    tn = n6 // 4
    return pl.pallas_call(
        _mod_kernel,
        grid=(depth, n6 // tn),
        in_specs=[_const_spec((8, d)),
                  pl.BlockSpec((1, d, tn), lambda l, j: (l, 0, j)),
                  pl.BlockSpec((1, 1, tn), lambda l, j: (l, 0, j))],
        out_specs=pl.BlockSpec((1, 8, tn), lambda l, j: (l, 0, j)),
        out_shape=jax.ShapeDtypeStruct((depth, 8, n6), F32),
        compiler_params=_params(("arbitrary", "arbitrary")),
        name="modulation",
    )(cond8, w_mod, b_mod.reshape(depth, 1, n6))


def _rms(z, gain):
    return z * lax.rsqrt(jnp.mean(z * z, axis=-1, keepdims=True) + EPS) * gain


def _head_rms(zh, gain_h, n_real):
    ms = jnp.sum(zh * zh, axis=-1, keepdims=True) * (1.0 / n_real)
    return zh * lax.rsqrt(ms + EPS) * gain_h


def _rope(zh, cos, sin, is_x1):
    rot = jnp.where(is_x1, pltpu.roll(zh, LANE - 8, 1), pltpu.roll(zh, 8, 1))
    return zh * cos + rot * sin


def _is_x1(rows):
    lane = lax.broadcasted_iota(jnp.int32, (rows, LANE), 1)
    first = jnp.where(lane >= MLA_NOPE, jnp.where(lane < MLA_NOPE + 8, 1, 0), 0)
    second = jnp.where(lane >= MLA_NOPE + 16, jnp.where(lane < MLA_NOPE + 24, 1, 0), 0)
    return (first + second) > 0


def _mla_kv(ck, wk_ref, wv_ref, gk_ref, cos, sin, km_ref, vm_ref):
    rows = ck.shape[0]
    kk = jnp.dot(ck, wk_ref[...], preferred_element_type=F32)
    is_x1 = _is_x1(rows)
    for h in range(HEADS):
        sl = slice(h * LANE, (h + 1) * LANE)
        kh = _head_rms(kk[:, sl], gk_ref[:, sl], MLA_QK)
        km_ref[:, sl] = _rope(kh, cos, sin, is_x1).astype(BF16)
    vm_ref[...] = jnp.dot(ck, wv_ref[...], preferred_element_type=F32).astype(BF16)


def _in_kernel(x_ref, mod_ref, n1_ref, w_ref, wuq_ref, wk_ref, wv_ref,
               gq_ref, gk_ref, gcq_ref, gckv_ref, gqm_ref, gkm_ref, cos_ref, sin_ref, after_hbm,
               qn_ref, kn_ref, vn_ref, knf_ref, vnf_ref, p_ref,
               qm_ref, km_ref, vm_ref, ckv_ref, kr_ref):
    del after_hbm
    d = x_ref.shape[1]
    rows = x_ref.shape[0]
    mod = mod_ref[0]
    sh1 = mod[:, 0:d]
    sc1 = mod[:, d:2 * d]
    h = _rms(x_ref[...], n1_ref[...]) * (1.0 + sc1) + sh1
    hb = h.astype(BF16)

    def proj(lo, hi):
        return jnp.dot(hb, w_ref[:, lo:hi], preferred_element_type=F32)

    cos = cos_ref[...]
    sin = sin_ref[...]
    is_x1 = _is_x1(rows)

    zq = proj(_CQ, _CQ + HW)
    zk = proj(_CK, _CK + HW)
    for hh in range(HEADS):
        sl = slice(hh * LANE, (hh + 1) * LANE)
        qn_ref[:, sl] = (_head_rms(zq[:, sl], gq_ref[:, sl], NA_DH) * (NA_DH ** -0.5)).astype(BF16)
        kh = _head_rms(zk[:, sl], gk_ref[:, sl], NA_DH)
        knf_ref[:, sl] = kh
        kn_ref[:, sl] = kh.astype(BF16)
    zv = proj(_CV, _CV + HW)
    vnf_ref[...] = zv
    vn_ref[...] = zv.astype(BF16)
    p_ref[...] = proj(_CP, _CP + 256)

    cq = _rms(proj(_CCQ, _CCQ + 256), gcq_ref[...])
    zqm = jnp.dot(cq.astype(BF16), wuq_ref[...], preferred_element_type=F32)
    for hh in range(HEADS):
        sl = slice(hh * LANE, (hh + 1) * LANE)
        qh = _head_rms(zqm[:, sl], gqm_ref[:, sl], MLA_QK)
        qm_ref[:, sl] = (_rope(qh, cos, sin, is_x1) * (MLA_QK ** -0.5)).astype(BF16)

    ckv = _rms(proj(_CCKV, _CCKV + 128), gckv_ref[...])
    kr = proj(_CKR, _CKR + 128)
    ckv_ref[...] = ckv
    kr_ref[...] = kr
    ck = jnp.concatenate([ckv, kr], axis=-1).astype(BF16)
    _mla_kv(ck, wk_ref, wv_ref, gkm_ref, cos, sin, km_ref, vm_ref)


def _in_proj(x, mod_l, row_off, bpm, lw, cos_t, sin_t, rope_blocks, after):
    n, d = x.shape
    nb = n // TB
    tok = lambda w: pl.BlockSpec((TB, w), lambda i: (i, 0))
    rope_spec = pl.BlockSpec((TB, LANE), lambda i: (i % rope_blocks, 0))
    in_specs = [tok(d),
                pl.BlockSpec((1, 1, mod_l.shape[-1]), lambda i: (row_off + i // bpm, 0, 0)),
                _const_spec((1, d)), _const_spec((d, IN_W)), _const_spec((256, HW)),
                _const_spec((256, HW)), _const_spec((256, HW)),
                _const_spec((1, HW)), _const_spec((1, HW)), _const_spec((1, 256)),
                _const_spec((1, 128)), _const_spec((1, HW)), _const_spec((1, HW)),
                rope_spec, rope_spec, pl.BlockSpec(memory_space=pl.ANY)]
    widths = [(HW, BF16), (HW, BF16), (HW, BF16), (HW, F32), (HW, F32), (256, F32),
              (HW, BF16), (HW, BF16), (HW, BF16), (128, F32), (128, F32)]
    return pl.pallas_call(
        _in_kernel,
        grid=(nb,),
        in_specs=in_specs,
        out_specs=[tok(w) for w, _ in widths],
        out_shape=[jax.ShapeDtypeStruct((n, w), dt) for w, dt in widths],
        compiler_params=_params(("arbitrary",)),
        name="in_proj",
    )(x, mod_l, lw["norm1"], lw["w_in"], lw["w_uq"], lw["w_k"], lw["w_v"],
      lw["g_q"], lw["g_k"], lw["g_cq"], lw["g_ckv"], lw["g_qm"], lw["g_km"], cos_t, sin_t, after)


def _cache_kernel(ck_ref, wk_ref, wv_ref, gk_ref, km_ref, vm_ref):
    rows = ck_ref.shape[2]
    cos = jnp.ones((rows, LANE), F32)
    sin = jnp.zeros((rows, LANE), F32)
    _mla_kv(ck_ref[0, 0], wk_ref.at[0], wv_ref.at[0], gk_ref.at[0], cos, sin,
            km_ref.at[0, 0], vm_ref.at[0, 0])


def _cache_kv(ck, w_k, w_v, g_km):
    db, depth, p, _ = ck.shape
    spec = lambda w: pl.BlockSpec((1, 1, p, w), lambda b, l: (b, l, 0, 0))
    wspec = lambda r: pl.BlockSpec((1, r, HW), lambda b, l: (l, 0, 0))
    return pl.pallas_call(
        _cache_kernel,
        grid=(db, depth),
        in_specs=[spec(256), wspec(256), wspec(256), wspec(1)],
        out_specs=[spec(HW), spec(HW)],
        out_shape=[jax.ShapeDtypeStruct((db, depth, p, HW), BF16)] * 2,
        compiler_params=_params(("arbitrary", "arbitrary")),
        name="cache_kv",
    )(ck, w_k, w_v, g_km)


def _softmax_av(s_list, v_list):
    m = s_list[0].max(axis=-1, keepdims=True)
    for s in s_list[1:]:
        m = jnp.maximum(m, s.max(axis=-1, keepdims=True))
    acc = None
    den = None
    for s, v in zip(s_list, v_list):
        p = jnp.exp(s - m)
        l = p.sum(axis=-1, keepdims=True)
        o = jnp.dot(p.astype(BF16), v, preferred_element_type=F32)
        acc = o if acc is None else acc + o
        den = l if den is None else den + l
    return acc / den


def _ctx_attn_kernel(qn, kn, vn, qm, km, vm, on, om):
    for q, k, v, o in ((qn, kn, vn, on), (qm, km, vm, om)):
        for h in range(HEADS):
            sl = slice(h * LANE, (h + 1) * LANE)
            s = _nt_dot(q[:, sl], k[:, sl])
            o[:, sl] = _softmax_av([s], [v[:, sl]]).astype(BF16)


def _ctx_attn(qn, kn, vn, qm, km, vm, seq):
    n = qn.shape[0]
    spec = pl.BlockSpec((seq, HW), lambda i: (i, 0))
    return pl.pallas_call(
        _ctx_attn_kernel,
        grid=(n // seq,),
        in_specs=[spec] * 6,
        out_specs=[spec] * 2,
        out_shape=[jax.ShapeDtypeStruct((n, HW), BF16)] * 2,
        compiler_params=_params(("arbitrary",)),
        name="ctx_attn",
    )(qn, kn, vn, qm, km, vm)


def _lat_mla_kernel(q, k, v, kc, vc, o):
    s1 = _nt_dot(q[...], k[...])
    s2 = _nt_dot(q[...], kc[0])
    o[...] = _softmax_av([s1, s2], [v[...], vc[0]]).astype(BF16)


def _lat_mla(qm, km, vm, kc, vc, db):
    n = qm.shape[0]
    ds = n // db
    nq = ds // TQ
    qspec = pl.BlockSpec((TQ, LANE), lambda b, h, i: (b * nq + i, h))
    kspec = pl.BlockSpec((ds, LANE), lambda b, h, i: (b, h))
    cspec = pl.BlockSpec((1, kc.shape[1], LANE), lambda b, h, i: (b, 0, h))
    return pl.pallas_call(
        _lat_mla_kernel,
        grid=(db, HEADS, nq),
        in_specs=[qspec, kspec, kspec, cspec, cspec],
        out_specs=qspec,
        out_shape=jax.ShapeDtypeStruct((n, HW), BF16),
        compiler_params=_params(("arbitrary",) * 3),
        name="lat_mla",
    )(qm, km, vm, kc, vc)


def _nat_kernel(q, k, v, kc, vc, bias, o, *, rows):
    r = pl.program_id(1)
    rs = jnp.clip(r - WIN_R // 2, 0, rows - WIN_R)
    start = pl.multiple_of(rs * GRID_W, GRID_W)
    band = WIN_R * GRID_W
    for h in range(HEADS):
        sl = slice(h * LANE, (h + 1) * LANE)
        qh = q[:, sl]
        s1 = _nt_dot(qh, k[pl.ds(start, band), sl]) + bias[0, h]
        s2 = _nt_dot(qh, kc[0, :, sl])
        o[:, sl] = _softmax_av([s1, s2], [v[pl.ds(start, band), sl], vc[0, :, sl]]).astype(BF16)


def _nat_attn(qn, kn, vn, kc, vc, bias, db):
    n = qn.shape[0]
    ds = n // db
    rows = ds // GRID_W
    band = WIN_R * GRID_W

    def variant(r):
        return jnp.where(r < WIN_R // 2, r, jnp.where(r > rows - WIN_R // 2, r - (rows - WIN_R), WIN_R // 2))

    qspec = pl.BlockSpec((GRID_W, HW), lambda b, r: (b * rows + r, 0))
    kspec = pl.BlockSpec((ds, HW), lambda b, r: (b, 0))
    cspec = pl.BlockSpec((1, kc.shape[1], HW), lambda b, r: (b, 0, 0))
    bspec = pl.BlockSpec((1, HEADS, GRID_W, band), lambda b, r: (variant(r), 0, 0, 0))
    return pl.pallas_call(
        functools.partial(_nat_kernel, rows=rows),
        grid=(db, rows),
        in_specs=[qspec, kspec, kspec, cspec, cspec, bspec],
        out_specs=qspec,
        out_shape=jax.ShapeDtypeStruct((n, HW), BF16),
        compiler_params=_params(("arbitrary", "arbitrary")),
        name="nat_attn",
    )(qn, kn, vn, kc, vc, bias)


def _split3(x):
    hi = x.astype(BF16)
    r = x - hi.astype(F32)
    mid = r.astype(BF16)
    lo = (r - mid.astype(F32)).astype(BF16)
    return hi, mid, lo


def _pool(p_prev, p_cur, p_next, posb, seq_len):
    rows = p_cur.shape[0]
    halo = p_prev.shape[0]
    ext = rows + 2 * halo
    pext = jnp.concatenate([p_prev, p_cur, p_next], axis=0)
    parts = _split3(pext)
    t = posb + lax.broadcasted_iota(jnp.int32, (rows, ext), 0)
    s = posb - halo + lax.broadcasted_iota(jnp.int32, (rows, ext), 1)
    tcol = posb + lax.broadcasted_iota(jnp.int32, (rows, 1), 0)
    grp = lax.broadcasted_iota(jnp.int32, (rows, 256), 1) // POOL_G
    d = jnp.zeros((rows, 256), F32)
    for gi, w in enumerate(POOL_WINDOWS):
        lo = jnp.maximum(t - w // 2, 0)
        hi = jnp.minimum(t + (w - w // 2), seq_len)
        sel = jnp.where(s >= lo, jnp.where(s < hi, 1.0, 0.0), 0.0).astype(BF16)
        tot = sum(jnp.dot(sel, part, preferred_element_type=F32) for part in parts)
        cnt = (jnp.minimum(tcol + (w - w // 2), seq_len) - jnp.maximum(tcol - w // 2, 0)).astype(F32)
        d = jnp.where(grp == gi, tot / cnt - p_cur, d)
    return d


def _first_max(x, pos, sentinel):
    m = jnp.max(x, axis=0, keepdims=True)
    idx = jnp.min(jnp.where(x == m, pos, sentinel), axis=0, keepdims=True)
    return m, idx


def _topk_stage1(qh, sk_ref):
    c = qh.shape[0]
    key_pos = lax.broadcasted_iota(jnp.int32, (PEER_NKEYS, c), 0).astype(F32)
    row16 = lax.broadcasted_iota(jnp.int32, (PEER_TOPK, c), 0)
    neg = jnp.float32(-jnp.inf)
    s0 = _nt_dot(sk_ref[0], qh)
    s1 = _nt_dot(sk_ref[1], qh)

    def stage1(a, carry):
        out = []
        for s, sv, si in (carry[0:3], carry[3:6]):
            m, idx = _first_max(s, key_pos, float(PEER_NKEYS))
            out += [jnp.where(key_pos == idx, neg, s),
                    jnp.where(row16 == a, m, sv), jnp.where(row16 == a, idx, si)]
        return tuple(out)

    zf = jnp.zeros((PEER_TOPK, c), F32)
    _, sv0, si0, _, sv1, si1 = lax.fori_loop(0, PEER_TOPK, stage1, (s0, zf, zf, s1, zf, zf))
    return sv0, si0, sv1, si1


def _topk_pieces(sv0, sv1):
    c = sv0.shape[1]
    neg = jnp.float32(-jnp.inf)
    sub8 = lax.broadcasted_iota(jnp.int32, (8, c), 0)
    sub8f = sub8.astype(F32)
    cs, cf = [], []
    for a in range(8):
        nb = PEER_TOPK // (a + 1)
        for b0 in range(0, nb, 8):
            val = sv0[a:a + 1] + sv1[b0:b0 + 8]
            if nb - b0 < 8:
                val = jnp.where(sub8 < nb - b0, val, neg)
            cs.append(val)
            cf.append(sub8f + float(a * PEER_TOPK + b0))
    cs.append(sv0[8:16] + sv1[0:1])
    cf.append((sub8f + 8.0) * float(PEER_TOPK))
    return cs, cf


def _topk_stage2(chains, cf):
    npc = len(cf)
    c = cf[0].shape[1]
    row16 = lax.broadcasted_iota(jnp.int32, (PEER_TOPK, c), 0)
    neg = jnp.float32(-jnp.inf)
    nflat = float(PEER_TOPK * PEER_TOPK)
    zf = jnp.zeros((PEER_TOPK, c), F32)

    def step(k, carry):
        out = []
        for ch in range(len(chains)):
            vals = carry[ch * (npc + 2):ch * (npc + 2) + npc]
            tv, tp = carry[ch * (npc + 2) + npc], carry[ch * (npc + 2) + npc + 1]
            m = vals[0]
            for v in vals[1:]:
                m = jnp.maximum(m, v)
            m = jnp.max(m, axis=0, keepdims=True)
            pos = None
            for v, f in zip(vals, cf):
                cand = jnp.where(v == m, f, nflat)
                pos = cand if pos is None else jnp.minimum(pos, cand)
            pos = jnp.min(pos, axis=0, keepdims=True)
            out += [jnp.where(f == pos, neg, v) for v, f in zip(vals, cf)]
            out += [jnp.where(row16 == k, m, tv), jnp.where(row16 == k, pos, tp)]
        return tuple(out)

    init = []
    for cs in chains:
        init += list(cs) + [zf, zf]
    res = lax.fori_loop(0, PEER_TOPK, step, tuple(init))
    return [(res[ch * (npc + 2) + npc], res[ch * (npc + 2) + npc + 1]) for ch in range(len(chains))]


def _topk_ids(tp, si0, si1):
    a = jnp.floor(tp * (1.0 / PEER_TOPK))
    b = tp - a * float(PEER_TOPK)
    ea = jnp.zeros_like(tp)
    eb = jnp.zeros_like(tp)
    for j in range(PEER_TOPK):
        ea = jnp.where(a == float(j), si0[j:j + 1], ea)
        eb = jnp.where(b == float(j), si1[j:j + 1], eb)
    return ea * float(PEER_NKEYS) + eb


def _out_kernel(on_ref, om_ref, pc_ref, pp_ref, pn_ref, x_ref, mod_ref,
                won_ref, wop_ref, wom_ref, pw_ref, ps_ref, n2_ref, wq_ref, sk_ref,
                x1_ref, h2_ref, ids_ref, gt_ref, gn_ref, h2c_ref, q_scr, idt_scr, *, bps, seq_len):
    d = x_ref.shape[1]
    rows = x_ref.shape[0]
    i = pl.program_id(0)
    mod = mod_ref[0]
    g1 = mod[:, 2 * d:3 * d]
    sh2 = mod[:, 3 * d:4 * d]
    sc2 = mod[:, 4 * d:5 * d]

    posb = (i % bps) * rows
    dpool = _pool(pp_ref[...], pc_ref[...], pn_ref[...], posb, seq_len)
    ypool = jnp.dot(dpool.astype(BF16), pw_ref[...], preferred_element_type=F32) * ps_ref[...]
    mix = (jnp.dot(on_ref[...], won_ref[...], preferred_element_type=F32)
           + jnp.dot(ypool.astype(BF16), wop_ref[...], preferred_element_type=F32)
           + jnp.dot(om_ref[...], wom_ref[...], preferred_element_type=F32))
    x1 = x_ref[...] + g1 * mix
    x1_ref[...] = x1
    h2 = _rms(x1, n2_ref[...]) * (1.0 + sc2) + sh2
    h2_ref[...] = h2
    for j in range(d // LANE):
        h2c_ref[j] = h2[:, j * LANE:(j + 1) * LANE]

    q = jnp.dot(h2.astype(BF16), wq_ref[...], preferred_element_type=F32)
    for hh in range(PEER_HEADS):
        q_scr[hh] = q[:, hh * LANE:(hh + 1) * LANE].astype(BF16)

    chunks = range(0, rows, LANE)

    def head(hh, _):
        sorted_keys = [_topk_stage1(q_scr[hh, c0:c0 + LANE, :], sk_ref) for c0 in chunks]
        pieces = [_topk_pieces(sv0, sv1) for sv0, _, sv1, _ in sorted_keys]
        picked = _topk_stage2([cs for cs, _ in pieces], pieces[0][1])
        r0 = pl.multiple_of(hh * PEER_TOPK, PEER_TOPK)
        for c0, (tv, tp), (_, si0, _, si1) in zip(chunks, picked, sorted_keys):
            ex = jnp.exp(tv - tv[0:1])
            gt_ref[pl.ds(r0, PEER_TOPK), c0:c0 + LANE] = ex / jnp.sum(ex, axis=0, keepdims=True)
            idt_scr[pl.ds(r0, PEER_TOPK), c0:c0 + LANE] = _topk_ids(tp, si0, si1)
        return 0

    lax.fori_loop(0, PEER_HEADS, head, 0)
    ids_ref[...] = idt_scr[...].T.astype(jnp.int32)
    gn_ref[...] = gt_ref[...].T


def _out_proj(x, on, om, p, mod_l, row_off, bpm, lw, seq_len):
    n, d = x.shape
    nb = n // TB
    bps = seq_len // TB
    halo = 8
    hb = TB // halo
    tok = lambda w: pl.BlockSpec((TB, w), lambda i: (i, 0))
    in_specs = [tok(HW), tok(HW), tok(256),
                pl.BlockSpec((halo, 256), lambda i: (jnp.maximum(i * hb - 1, 0), 0)),
                pl.BlockSpec((halo, 256), lambda i: (jnp.minimum((i + 1) * hb, n // halo - 1), 0)),
                tok(d),
                pl.BlockSpec((1, 1, mod_l.shape[-1]), lambda i: (row_off + i // bpm, 0, 0)),
                _const_spec((HW, d)), _const_spec((256, d)), _const_spec((HW, d)),
                _const_spec((256, 256)), _const_spec((1, 256)), _const_spec((1, d)),
                _const_spec((d, PEER_HEADS * LANE)), _const_spec((2, PEER_NKEYS, LANE))]
    nk = PEER_HEADS * PEER_TOPK
    return pl.pallas_call(
        functools.partial(_out_kernel, bps=bps, seq_len=seq_len),
        grid=(nb,),
        in_specs=in_specs,
        out_specs=[tok(d), tok(d), tok(nk), pl.BlockSpec((nk, TB), lambda i: (0, i)), tok(nk),
                   pl.BlockSpec((d // LANE, TB, LANE), lambda i: (0, i, 0))],
        out_shape=[jax.ShapeDtypeStruct((n, d), F32), jax.ShapeDtypeStruct((n, d), F32),
                   jax.ShapeDtypeStruct((n, nk), jnp.int32), jax.ShapeDtypeStruct((nk, n), F32),
                   jax.ShapeDtypeStruct((n, nk), F32),
                   jax.ShapeDtypeStruct((d // LANE, n, LANE), F32)],
        scratch_shapes=[pltpu.VMEM((PEER_HEADS, TB, LANE), BF16), pltpu.VMEM((nk, TB), F32)],
        compiler_params=_params(("arbitrary",)),
        name="out_proj",
    )(on, om, p, p, p, x, mod_l, lw["w_o_na"], lw["w_o_pool"], lw["w_o_mla"],
      lw["pool_w"], lw["pool_scale"], lw["norm2"], lw["peer_wq"], lw["peer_sk"])


def _gelu_tanh(x):
    return x * (0.5 * (1.0 + jnp.tanh(0.7978845608028654 * (x + 0.044715 * (x * x * x)))))


def _peer_token_mix(chunk, hrow, gcol, ch):
    acc = None
    for s in range(ch):
        us = lax.bitcast_convert_type(chunk(s) & jnp.int32(-65536), F32)
        term = us * hrow[:, s * LANE:(s + 1) * LANE]
        acc = term if acc is None else acc + term
    wgt = gcol * _gelu_tanh(jnp.sum(acc, axis=-1, keepdims=True))
    parts = []
    for s in range(ch):
        vs = lax.bitcast_convert_type(chunk(s) << 16, F32)
        parts.append(jnp.sum(vs * wgt, axis=0, keepdims=True))
    return jnp.concatenate(parts, axis=-1)


def _sc_peer(table3, ids, gates, h2c, n):
    ch, _, lane = h2c.shape
    nk = ids.shape[1]
    info = plsc.get_sparse_core_info()
    nc, nw, nl = info.num_cores, info.num_cores * info.num_subcores, info.num_lanes
    tpw = n // nw
    win = 32
    nq = nk // win
    cpr = lane // nl
    nchunk = ch * cpr
    hc = nchunk // 2
    assert n % nw == 0 and nk % win == 0 and win % nl == 0
    mesh = plsc.VectorSubcoreMesh(core_axis_name="core", subcore_axis_name="subcore")
    hi_mask = jnp.int32(-65536)

    @functools.partial(
        pl.kernel, mesh=mesh,
        out_type=jax.ShapeDtypeStruct((ch, n, lane), F32),
        compiler_params=pltpu.CompilerParams(needs_layout_passes=False),
        scratch_types=[pltpu.VMEM((nk,), jnp.int32), pltpu.VMEM((nk,), F32),
                       pltpu.VMEM((ch, lane), F32), pltpu.VMEM((ch, lane), F32),
                       pltpu.VMEM((win, ch, lane), jnp.int32), pltpu.VMEM((win, ch, lane), jnp.int32),
                       pltpu.VMEM((win * nl,), F32), pltpu.VMEM((win,), F32),
                       pltpu.SemaphoreType.DMA, pltpu.SemaphoreType.DMA, pltpu.SemaphoreType.DMA])
    def peer(tab_hbm, ids_hbm, g_hbm, h2_hbm, y_hbm,
             idx_v, g_v, x_v, y_v, rows_a, rows_b, part_v, w_v, sem_a, sem_b, sem_x):
        wid = lax.axis_index("subcore") * nc + lax.axis_index("core")
        bufs = ((rows_a, sem_a), (rows_b, sem_b))
        lanes = lax.iota(jnp.int32, nl)
        zero = jnp.zeros((nl,), F32)

        def chunk_copies(tok, to_hbm):
            if to_hbm:
                return [pltpu.make_async_copy(y_v.at[j], y_hbm.at[j, tok], sem_x) for j in range(ch)]
            return [pltpu.make_async_copy(h2_hbm.at[j, tok], x_v.at[j], sem_x) for j in range(ch)]

        def fetch(q):
            rows, sem = bufs[q % 2]
            return pltpu.make_async_copy(tab_hbm.at[idx_v.at[pl.ds(q * win, win)]], rows, sem)

        def word(rows, r, cc):
            return rows[r, cc // cpr, pl.ds((cc % cpr) * nl, nl)]

        @pl.loop(0, tpw)
        def _(ti):
            tok = wid * tpw + ti
            loads = chunk_copies(tok, False)
            for cp in loads:
                cp.start()
            pltpu.sync_copy(ids_hbm.at[tok], idx_v)
            pltpu.sync_copy(g_hbm.at[tok], g_v)
            for cp in loads:
                cp.wait()
            for cc in range(nchunk):
                y_v[cc // cpr, pl.ds((cc % cpr) * nl, nl)] = zero
            fetch(0).start()
            for q in range(nq):
                rows = bufs[q % 2][0]
                fetch(q).wait()
                if q + 1 < nq:
                    fetch(q + 1).start()

                for half in range(2):
                    xs = [x_v[(half * hc + c) // cpr, pl.ds(((half * hc + c) % cpr) * nl, nl)]
                          for c in range(hc)]

                    @pl.loop(0, win, step=2)
                    def _(r0):
                        pos = [pl.multiple_of((r0 + k) * nl, nl) for k in range(2)]
                        accs = [[part_v[pl.ds(pos[k], nl)] if half else None, None, None, None]
                                for k in range(2)]
                        for c in range(hc):
                            for k in range(2):
                                u = lax.bitcast_convert_type(
                                    word(rows, r0 + k, half * hc + c) & hi_mask, F32)
                                t = u * xs[c]
                                accs[k][c % 4] = t if accs[k][c % 4] is None else accs[k][c % 4] + t
                        for k in range(2):
                            part_v[pl.ds(pos[k], nl)] = (accs[k][0] + accs[k][1]) + (accs[k][2] + accs[k][3])

                for grp in range(win // nl):
                    s = zero
                    for rr in range(nl):
                        tot = jnp.sum(part_v[pl.ds((grp * nl + rr) * nl, nl)])
                        s = jnp.where(lanes == rr, tot, s)
                    z = 0.7978845608028654 * (s + 0.044715 * (s * s * s))
                    tanh = 1.0 - 2.0 / (jnp.exp(2.0 * z) + 1.0)
                    gate = g_v[pl.ds(q * win + grp * nl, nl)]
                    w_v[pl.ds(grp * nl, nl)] = gate * (s * (0.5 * (1.0 + tanh)))

                for half in range(2):
                    def body(r, yacc):
                        wr = plsc.load_gather(w_v, [jnp.full((nl,), r, jnp.int32)])
                        out = []
                        for c in range(hc):
                            v = lax.bitcast_convert_type(word(rows, r, half * hc + c) << 16, F32)
                            out.append(yacc[c] + wr * v)
                        return tuple(out)

                    yacc = lax.fori_loop(0, win, body, tuple(zero for _ in range(hc)))
                    for c in range(hc):
                        cc = half * hc + c
                        sl = (cc // cpr, pl.ds((cc % cpr) * nl, nl))
                        y_v[sl] = y_v[sl] + yacc[c]
            stores = chunk_copies(tok, True)
            for cp in stores:
                cp.start()
            for cp in stores:
                cp.wait()

    return peer(table3, ids, gates, h2c)


def _residual_kernel(x1_ref, y_ref, mod_ref, x2_hbm, after_hbm, o_ref):
    del x2_hbm
    del after_hbm
    d = x1_ref.shape[1]
    g2 = mod_ref[0][:, 5 * d:6 * d]
    for j in range(d // LANE):
        sl = slice(j * LANE, (j + 1) * LANE)
        o_ref[:, sl] = x1_ref[:, sl] + g2[:, sl] * y_ref[j]


def _residual(x1, y, mod_l, row, x2, after):
    n, d = x1.shape
    tok = pl.BlockSpec((PEER_TB, d), lambda i: (i, 0))
    any_spec = pl.BlockSpec(memory_space=pl.ANY)
    return pl.pallas_call(
        _residual_kernel,
        grid=(y.shape[1] // PEER_TB,),
        in_specs=[tok, pl.BlockSpec((d // LANE, PEER_TB, LANE), lambda i: (0, i, 0)),
                  pl.BlockSpec((1, 1, mod_l.shape[-1]), lambda i: (row, 0, 0)),
                  any_spec, any_spec],
        out_specs=tok,
        out_shape=jax.ShapeDtypeStruct((n, d), F32),
        input_output_aliases={3: 0},
        compiler_params=_params(("arbitrary",)),
        name="residual",
    )(x1, y, mod_l, x2, after)


def _peer_kernel(ids_hbm, gt_ref, h2_ref, x1_ref, mod_ref, tab_hbm, o_ref,
                 ids_s, buf, sem_i, sem_r, *, first_block):
    d = x1_ref.shape[1]
    ch = d // LANE
    pitch = ch + 1
    nsub = x1_ref.shape[0] // PEER_SUB
    nk = gt_ref.shape[0]
    nids = PEER_SUB * nk
    i = pl.program_id(0) + first_block
    g2 = mod_ref[0][:, 5 * d:6 * d]
    tok_lane = lax.broadcasted_iota(jnp.int32, gt_ref.shape, 1)

    def ids_copy(j, slot):
        start = pl.multiple_of((i * nsub + j) * nids, nids)
        return pltpu.make_async_copy(ids_hbm.at[pl.ds(start, nids)],
                                     ids_s.at[pl.ds(slot * nids, nids)], sem_i.at[slot])

    def row_copy(slot, e, f):
        src = tab_hbm.at[pl.ds(pl.multiple_of(e * ch, ch), ch), :]
        dst = buf.at[slot, pl.ds(f * pitch, ch), :]
        return pltpu.make_async_copy(src, dst, sem_r.at[slot])

    def issue_rows(slot):
        for t in range(PEER_SUB):
            def body(kk, _):
                for r in range(8):
                    f = t * nk + kk * 8 + r
                    row_copy(slot, ids_s[slot * nids + f], f).start(priority=r % 2)
                return 0

            lax.fori_loop(0, nk // 8, body, 0)

    def wait_rows(slot):
        done = buf.at[slot, pl.ds(0, nids * ch), :]
        pltpu.make_async_copy(done, done, sem_r.at[slot]).wait()

    def compute(slot, j):
        base = pl.multiple_of(j * PEER_SUB, PEER_SUB)
        h8 = h2_ref[pl.ds(base, PEER_SUB), :]
        ys = []
        for t in range(PEER_SUB):
            chunk = lambda s: buf[slot, pl.ds(t * nk * pitch + s, nk, stride=pitch), :]
            gcol = jnp.sum(jnp.where(tok_lane == base + t, gt_ref[...], 0.0), axis=-1, keepdims=True)
            ys.append(_peer_token_mix(chunk, h8[t:t + 1, :], gcol, ch))
        y8 = jnp.concatenate(ys, axis=0)
        o_ref[pl.ds(base, PEER_SUB), :] = x1_ref[pl.ds(base, PEER_SUB), :] + g2 * y8

    first = ids_copy(0, 0)
    first.start()
    first.wait()
    issue_rows(0)
    ids_copy(1, 1).start()

    def pair(jj, _):
        j0 = 2 * jj
        ids_copy(j0 + 1, 1).wait()
        issue_rows(1)

        @pl.when(j0 + 2 < nsub)
        def _():
            ids_copy(j0 + 2, 0).start()

        wait_rows(0)
        compute(0, j0)

        @pl.when(j0 + 2 < nsub)
        def _():
            ids_copy(j0 + 2, 0).wait()
            issue_rows(0)

        @pl.when(j0 + 3 < nsub)
        def _():
            ids_copy(j0 + 3, 1).start()

        wait_rows(1)
        compute(1, j0 + 1)
        return 0

    lax.fori_loop(0, nsub // 2, pair, 0)


def _pack_tables(peer_u, peer_v):
    e, d = peer_u.shape
    ub = lax.bitcast_convert_type(peer_u.astype(BF16), jnp.uint16).astype(jnp.uint32)
    vb = lax.bitcast_convert_type(peer_v.astype(BF16), jnp.uint16).astype(jnp.uint32)
    words = lax.bitcast_convert_type((ub << 16) | vb, jnp.int32)
    return words.reshape(e, d // LANE, LANE)


def _peer(x1, h2, ids, gt, mod_l, row, table, tok0):
    n, d = x1.shape
    nk = gt.shape[0]
    b0 = tok0 // PEER_TB
    nb = n // PEER_TB - b0
    tok = pl.BlockSpec((PEER_TB, d), lambda i: (i + b0, 0))
    any_spec = pl.BlockSpec(memory_space=pl.ANY)
    return pl.pallas_call(
        functools.partial(_peer_kernel, first_block=b0),
        grid=(nb,),
        in_specs=[any_spec,
                  pl.BlockSpec((nk, PEER_TB), lambda i: (0, i + b0)),
                  tok, tok,
                  pl.BlockSpec((1, 1, mod_l.shape[-1]), lambda i: (row, 0, 0)),
                  any_spec],
        out_specs=tok,
        out_shape=jax.ShapeDtypeStruct((n, d), F32),
        scratch_shapes=[pltpu.SMEM((2 * PEER_SUB * nk,), jnp.int32),
                        pltpu.VMEM((2, PEER_SUB * nk * (d // LANE + 1), LANE), jnp.int32),
                        pltpu.SemaphoreType.DMA((2,)),
                        pltpu.SemaphoreType.DMA((2,))],
        compiler_params=_params(("arbitrary",)),
        name="peer",
    )(ids.reshape(n * nk), gt, h2, x1, mod_l, table.reshape(-1, LANE))


def _pad_heads(w, width):
    pad = [(0, 0)] * (w.ndim - 1) + [(0, LANE - width)]
    w = jnp.pad(w, pad)
    return w.reshape(w.shape[:-2] + (HW,))


def _head_gain(g, width):
    depth = g.shape[0]
    g = jnp.pad(g, ((0, 0), (0, LANE - width)))
    return jnp.tile(g, (1, HEADS)).reshape(depth, 1, HW)


def _rope_tables(seq):
    t = np.arange(seq)
    half = MLA_ROPE // 2
    inv = ROPE_THETA ** (-np.arange(0, half, 2, dtype=np.float32) / half)
    cos = np.ones((seq, LANE), np.float32)
    sin = np.zeros((seq, LANE), np.float32)
    for off, pos in ((MLA_NOPE, t // GRID_W), (MLA_NOPE + half, t % GRID_W)):
        ang = pos.astype(np.float32)[:, None] * inv[None, :]
        q = half // 2
        cos[:, off:off + q] = np.cos(ang)
        cos[:, off + q:off + half] = np.cos(ang)
        sin[:, off:off + q] = -np.sin(ang)
        sin[:, off + q:off + half] = np.sin(ang)
    return jnp.asarray(cos), jnp.asarray(sin)


def _nat_bias(rel_bias):
    v = np.arange(WIN_R)[:, None]
    j = np.arange(WIN_R)[None, :]
    dr = j - v + WIN_R - 1
    cq = np.arange(GRID_W)[:, None]
    kc = np.arange(GRID_W)[None, :]
    cstart = np.clip(cq - WIN_C // 2, 0, GRID_W - WIN_C)
    ok = (kc >= cstart) & (kc < cstart + WIN_C)
    dc = np.clip(kc - cq + WIN_C - 1, 0, 2 * WIN_C - 2)
    b = rel_bias[:, :, dr]
    b = b[..., dc]
    b = jnp.where(jnp.asarray(ok)[None, None, None, None], b, NEG_INF)
    b = jnp.transpose(b, (0, 2, 1, 4, 3, 5))
    return b.reshape(b.shape[0], WIN_R, HEADS, GRID_W, WIN_R * GRID_W)


def _layer_weights(w_in, na_q_norm, na_k_norm, mla_cq_norm, mla_ckv_norm, mla_w_uq, mla_w_ukv,
                   mla_q_norm, mla_k_norm, w_out, pool_w, pool_scale, norm1, norm2,
                   peer_wq, peer_subkeys):
    depth, d, _ = w_in.shape
    na_w = HEADS * NA_DH
    segs = np.cumsum([0, na_w, na_w, na_w, 256, 256, 128, MLA_ROPE])
    part = lambda i: w_in[:, :, segs[i]:segs[i + 1]]
    heads = lambda w: _pad_heads(w.reshape(depth, d, HEADS, NA_DH), NA_DH)
    w_in_p = jnp.concatenate(
        [heads(part(0)), heads(part(1)), heads(part(2)), part(3), part(4), part(5),
         jnp.pad(part(6), ((0, 0), (0, 0), (0, LANE - MLA_ROPE)))], axis=-1).astype(BF16)

    w_uq = _pad_heads(mla_w_uq, MLA_QK).astype(BF16)
    k_nope = _pad_heads(mla_w_ukv[..., :MLA_NOPE], MLA_NOPE)
    eye = np.zeros((MLA_ROPE, HEADS, LANE), np.float32)
    for h in range(HEADS):
        eye[np.arange(MLA_ROPE), h, MLA_NOPE + np.arange(MLA_ROPE)] = 1.0
    eye = jnp.broadcast_to(jnp.asarray(eye.reshape(MLA_ROPE, HW)), (depth, MLA_ROPE, HW))
    zer = jnp.zeros((depth, 256 - 128 - MLA_ROPE, HW), F32)
    w_k = jnp.concatenate([k_nope, eye, zer], axis=1).astype(BF16)
    w_v = jnp.concatenate([_pad_heads(mla_w_ukv[..., MLA_NOPE:], MLA_V),
                           jnp.zeros((depth, 128, HW), F32)], axis=1).astype(BF16)

    mix_w = HEADS * NA_DH
    w_o_na = jnp.pad(w_out[:, :mix_w].reshape(depth, HEADS, NA_DH, d),
                     ((0, 0), (0, 0), (0, LANE - NA_DH), (0, 0))).reshape(depth, HW, d).astype(BF16)
    w_o_pool = w_out[:, mix_w:mix_w + 256].astype(BF16)
    w_o_mla = jnp.pad(w_out[:, mix_w + 256:].reshape(depth, HEADS, MLA_V, d),
                      ((0, 0), (0, 0), (0, LANE - MLA_V), (0, 0))).reshape(depth, HW, d).astype(BF16)
    ng = len(POOL_WINDOWS)
    pw = jnp.zeros((depth, ng * POOL_G, ng * POOL_G), F32)
    for g in range(ng):
        pw = pw.at[:, g * POOL_G:(g + 1) * POOL_G, g * POOL_G:(g + 1) * POOL_G].set(pool_w[:, g])

    half = peer_subkeys.shape[-1]
    sk = jnp.stack([jnp.pad(peer_subkeys[:, 0], ((0, 0), (0, 0), (0, LANE - half))),
                    jnp.pad(peer_subkeys[:, 1], ((0, 0), (0, 0), (LANE - half, 0)))], axis=1).astype(BF16)

    return dict(
        w_in=w_in_p, w_uq=w_uq, w_k=w_k, w_v=w_v,
        g_q=_head_gain(na_q_norm, NA_DH), g_k=_head_gain(na_k_norm, NA_DH),
        g_cq=mla_cq_norm[:, None, :], g_ckv=mla_ckv_norm[:, None, :],
        g_qm=_head_gain(mla_q_norm, MLA_QK), g_km=_head_gain(mla_k_norm, MLA_QK),
        w_o_na=w_o_na, w_o_pool=w_o_pool, w_o_mla=w_o_mla,
        pool_w=pw.astype(BF16), pool_scale=pool_scale[:, None, :],
        norm1=norm1[:, None, :], norm2=norm2[:, None, :],
        peer_wq=peer_wq.astype(BF16), peer_sk=sk)


def kernel(x_prompt, x_sample, c, cache_nat_k, cache_nat_v, cache_mla_ckv, cache_mla_krope, c_ctx, w_mod, b_mod, norm1, norm2, w_in, na_q_norm, na_k_norm, na_rel_bias, pool_w, pool_scale, mla_cq_norm, mla_ckv_norm, mla_w_uq, mla_w_ukv, mla_q_norm, mla_k_norm, w_out, peer_wq, peer_subkeys, peer_u, peer_v):
    batch, seq, d = x_prompt.shape
    db, ds, _ = x_sample.shape
    depth = w_mod.shape[0]
    past = cache_nat_k.shape[2]
    assert seq == TB and ds % TB == 0 and ds % (GRID_W * WIN_R) == 0 and db + 1 <= 8

    cond8 = jnp.concatenate([c_ctx[None, :], c, jnp.zeros((8 - 1 - db, d), F32)], axis=0)
    mod = _modulation(cond8, w_mod, b_mod).reshape(depth, 8, 1, 6 * d)

    lw_all = _layer_weights(w_in, na_q_norm, na_k_norm, mla_cq_norm, mla_ckv_norm, mla_w_uq,
                            mla_w_ukv, mla_q_norm, mla_k_norm, w_out, pool_w, pool_scale,
                            norm1, norm2, peer_wq, peer_subkeys)
    bias_all = _nat_bias(na_rel_bias)
    tables = [_pack_tables(peer_u[l], peer_v[l]) for l in range(depth)]
    cos_lat, sin_lat = _rope_tables(ds)
    cos_ctx = jnp.ones((TB, LANE), F32)
    sin_ctx = jnp.zeros((TB, LANE), F32)

    ck = jnp.concatenate([cache_mla_ckv, cache_mla_krope,
                          jnp.zeros(cache_mla_ckv.shape[:-1] + (256 - 128 - MLA_ROPE,), F32)],
                         axis=-1).astype(BF16)
    kc_mla, vc_mla = _cache_kv(ck, lw_all["w_k"], lw_all["w_v"], lw_all["g_km"])
    kc_na = _pad_heads(cache_nat_k, NA_DH).astype(BF16)
    vc_na = _pad_heads(cache_nat_v, NA_DH).astype(BF16)

    xs = [x_prompt.reshape(batch * seq, d)] + [x_sample[b] for b in range(db)]
    one_row = max(batch * seq, ds) + 1
    lat_bpm = ds // TB
    ks, vs, ckvs, krs = [], [], [], []
    pending = None
    after = xs[0]

    def join(item, follow):
        si, x1, y_sc, x2, mod_l = item
        xs[si] = _residual(x1, y_sc, mod_l, si, x2, follow)
        return xs[si]

    for l in range(depth):
        lw = {k: v[l] for k, v in lw_all.items()}
        mod_l = mod[l]
        for si in range(db + 1):
            x = xs[si]
            if si == 0:
                (qn, kn, vn, knf, vnf, p, qm, km, vm, ckv, kr) = _in_proj(
                    x, mod_l, 0, one_row, lw, cos_ctx, sin_ctx, 1, after)
                on, om = _ctx_attn(qn, kn, vn, qm, km, vm, seq)
                x1, h2, ids, gt, gn, h2c = _out_proj(x, on, om, p, mod_l, 0, one_row, lw, seq)
                ks.append(knf.reshape(batch, seq, HEADS, LANE)[..., :NA_DH])
                vs.append(vnf.reshape(batch, seq, HEADS, LANE)[..., :NA_DH])
                ckvs.append(ckv.reshape(batch, seq, 128))
                krs.append(kr.reshape(batch, seq, LANE)[..., :MLA_ROPE])
            else:
                b = si - 1
                (qn, kn, vn, _, _, p, qm, km, vm, _, _) = _in_proj(
                    x, mod_l, si, one_row, lw, cos_lat, sin_lat, lat_bpm, after)
                on = _nat_attn(qn, kn, vn, kc_na[b:b + 1, l], vc_na[b:b + 1, l], bias_all[l], 1)
                om = _lat_mla(qm, km, vm, kc_mla[b:b + 1, l], vc_mla[b:b + 1, l], 1)
                x1, h2, ids, gt, gn, h2c = _out_proj(x, on, om, p, mod_l, si, one_row, lw, ds)
            n_sc = x.shape[0] * SC_SHARE[0] // SC_SHARE[1] // PEER_TB * PEER_TB
            y_sc = _sc_peer(tables[l], ids, gn, h2c, n_sc)
            x2 = _peer(x1, h2, ids, gt, mod_l, si, tables[l], n_sc)
            after = x2 if pending is None else join(pending, x2)
            pending = (si, x1, y_sc, x2, mod_l)
    join(pending, pending[1])

    return (xs[0].reshape(batch, seq, d), jnp.stack(xs[1:], axis=0),
            jnp.stack(ks, axis=1), jnp.stack(vs, axis=1),
            jnp.stack(ckvs, axis=1), jnp.stack(krs, axis=1))
```

```python
import functools

import numpy as np
import jax
import jax.numpy as jnp
from jax import lax
from jax.experimental import pallas as pl
from jax.experimental.pallas import tpu as pltpu
from jax.experimental.pallas import tpu_sc as plsc

F32 = jnp.float32
BF16 = jnp.bfloat16

EPS = 1e-6
ROPE_THETA = 10000.0
NEG_INF = -1e30
GRID_W = 64
HEADS = 6
NA_DH = 64
WIN_R = 8
WIN_C = 16
POOL_WINDOWS = (2, 4, 8, 16)
POOL_G = 64
MLA_NOPE = 64
MLA_ROPE = 32
MLA_QK = MLA_NOPE + MLA_ROPE
MLA_V = 64
PEER_HEADS = 8
PEER_NKEYS = 128
PEER_TOPK = 16
LANE = 128
HW = HEADS * LANE
TB = 256
TQ = 256
PEER_TB = 128
PEER_SUB = 8
VMEM_LIMIT = 56 * 1024 * 1024
SC_SHARE = (3, 4)

_CQ, _CK, _CV = 0, HW, 2 * HW
_CP = 3 * HW
_CCQ = _CP + 256
_CCKV = _CCQ + 256
_CKR = _CCKV + 128
IN_W = _CKR + 128


def _params(sem, vmem=VMEM_LIMIT):
    return pltpu.CompilerParams(dimension_semantics=sem, vmem_limit_bytes=vmem)


def _const_spec(shape):
    n = len(shape)
    return pl.BlockSpec(shape, lambda *_: (0,) * n)


def _nt_dot(a, b):
    return lax.dot_general(a, b, (((1,), (1,)), ((), ())), preferred_element_type=F32)


def _mod_kernel(c_ref, w_ref, b_ref, o_ref):
    c = c_ref[...]
    s = c / (1.0 + jnp.exp(-c))
    o_ref[0] = jnp.dot(s, w_ref[0], preferred_element_type=F32,
                       precision=lax.Precision.HIGHEST) + b_ref[0]


def _modulation(cond8, w_mod, b_mod):
    depth, d, n6 = w_mod.shape
    tn = n6 // 4
    return pl.pallas_call(
        _mod_kernel,
        grid=(depth, n6 // tn),
        in_specs=[_const_spec((8, d)),
                  pl.BlockSpec((1, d, tn), lambda l, j: (l, 0, j)),
                  pl.BlockSpec((1, 1, tn), lambda l, j: (l, 0, j))],
        out_specs=pl.BlockSpec((1, 8, tn), lambda l, j: (l, 0, j)),
        out_shape=jax.ShapeDtypeStruct((depth, 8, n6), F32),
        compiler_params=_params(("arbitrary", "arbitrary")),
        name="modulation",
    )(cond8, w_mod, b_mod.reshape(depth, 1, n6))


def _rms(z, gain):
    return z * lax.rsqrt(jnp.mean(z * z, axis=-1, keepdims=True) + EPS) * gain


def _head_rms(zh, gain_h, n_real):
    ms = jnp.sum(zh * zh, axis=-1, keepdims=True) * (1.0 / n_real)
    return zh * lax.rsqrt(ms + EPS) * gain_h


def _rope(zh, cos, sin, is_x1):
    rot = jnp.where(is_x1, pltpu.roll(zh, LANE - 8, 1), pltpu.roll(zh, 8, 1))
    return zh * cos + rot * sin


def _is_x1(rows):
    lane = lax.broadcasted_iota(jnp.int32, (rows, LANE), 1)
    first = jnp.where(lane >= MLA_NOPE, jnp.where(lane < MLA_NOPE + 8, 1, 0), 0)
    second = jnp.where(lane >= MLA_NOPE + 16, jnp.where(lane < MLA_NOPE + 24, 1, 0), 0)
    return (first + second) > 0


def _mla_kv(ck, wk_ref, wv_ref, gk_ref, cos, sin, km_ref, vm_ref):
    rows = ck.shape[0]
    kk = jnp.dot(ck, wk_ref[...], preferred_element_type=F32)
    is_x1 = _is_x1(rows)
    for h in range(HEADS):
        sl = slice(h * LANE, (h + 1) * LANE)
        kh = _head_rms(kk[:, sl], gk_ref[:, sl], MLA_QK)
        km_ref[:, sl] = _rope(kh, cos, sin, is_x1).astype(BF16)
    vm_ref[...] = jnp.dot(ck, wv_ref[...], preferred_element_type=F32).astype(BF16)


def _in_kernel(x_ref, mod_ref, n1_ref, w_ref, wuq_ref, wk_ref, wv_ref,
               gq_ref, gk_ref, gcq_ref, gckv_ref, gqm_ref, gkm_ref, cos_ref, sin_ref, after_hbm,
               qn_ref, kn_ref, vn_ref, knf_ref, vnf_ref, p_ref,
               qm_ref, km_ref, vm_ref, ckv_ref, kr_ref):
    del after_hbm
    d = x_ref.shape[1]
    rows = x_ref.shape[0]
    mod = mod_ref[0]
    sh1 = mod[:, 0:d]
    sc1 = mod[:, d:2 * d]
    h = _rms(x_ref[...], n1_ref[...]) * (1.0 + sc1) + sh1
    hb = h.astype(BF16)

    def proj(lo, hi):
        return jnp.dot(hb, w_ref[:, lo:hi], preferred_element_type=F32)

    cos = cos_ref[...]
    sin = sin_ref[...]
    is_x1 = _is_x1(rows)

    zq = proj(_CQ, _CQ + HW)
    zk = proj(_CK, _CK + HW)
    for hh in range(HEADS):
        sl = slice(hh * LANE, (hh + 1) * LANE)
        qn_ref[:, sl] = (_head_rms(zq[:, sl], gq_ref[:, sl], NA_DH) * (NA_DH ** -0.5)).astype(BF16)
        kh = _head_rms(zk[:, sl], gk_ref[:, sl], NA_DH)
        knf_ref[:, sl] = kh
        kn_ref[:, sl] = kh.astype(BF16)
    zv = proj(_CV, _CV + HW)
    vnf_ref[...] = zv
    vn_ref[...] = zv.astype(BF16)
    p_ref[...] = proj(_CP, _CP + 256)

    cq = _rms(proj(_CCQ, _CCQ + 256), gcq_ref[...])
    zqm = jnp.dot(cq.astype(BF16), wuq_ref[...], preferred_element_type=F32)
    for hh in range(HEADS):
        sl = slice(hh * LANE, (hh + 1) * LANE)
        qh = _head_rms(zqm[:, sl], gqm_ref[:, sl], MLA_QK)
        qm_ref[:, sl] = (_rope(qh, cos, sin, is_x1) * (MLA_QK ** -0.5)).astype(BF16)

    ckv = _rms(proj(_CCKV, _CCKV + 128), gckv_ref[...])
    kr = proj(_CKR, _CKR + 128)
    ckv_ref[...] = ckv
    kr_ref[...] = kr
    ck = jnp.concatenate([ckv, kr], axis=-1).astype(BF16)
    _mla_kv(ck, wk_ref, wv_ref, gkm_ref, cos, sin, km_ref, vm_ref)


def _in_proj(x, mod_l, row_off, bpm, lw, cos_t, sin_t, rope_blocks, after):
    n, d = x.shape
    nb = n // TB
    tok = lambda w: pl.BlockSpec((TB, w), lambda i: (i, 0))
    rope_spec = pl.BlockSpec((TB, LANE), lambda i: (i % rope_blocks, 0))
    in_specs = [tok(d),
                pl.BlockSpec((1, 1, mod_l.shape[-1]), lambda i: (row_off + i // bpm, 0, 0)),
                _const_spec((1, d)), _const_spec((d, IN_W)), _const_spec((256, HW)),
                _const_spec((256, HW)), _const_spec((256, HW)),
                _const_spec((1, HW)), _const_spec((1, HW)), _const_spec((1, 256)),
                _const_spec((1, 128)), _const_spec((1, HW)), _const_spec((1, HW)),
                rope_spec, rope_spec, pl.BlockSpec(memory_space=pl.ANY)]
    widths = [(HW, BF16), (HW, BF16), (HW, BF16), (HW, F32), (HW, F32), (256, F32),
              (HW, BF16), (HW, BF16), (HW, BF16), (128, F32), (128, F32)]
    return pl.pallas_call(
        _in_kernel,
        grid=(nb,),
        in_specs=in_specs,
        out_specs=[tok(w) for w, _ in widths],
        out_shape=[jax.ShapeDtypeStruct((n, w), dt) for w, dt in widths],
        compiler_params=_params(("arbitrary",)),
        name="in_proj",
    )(x, mod_l, lw["norm1"], lw["w_in"], lw["w_uq"], lw["w_k"], lw["w_v"],
      lw["g_q"], lw["g_k"], lw["g_cq"], lw["g_ckv"], lw["g_qm"], lw["g_km"], cos_t, sin_t, after)


def _cache_kernel(ck_ref, wk_ref, wv_ref, gk_ref, km_ref, vm_ref):
    rows = ck_ref.shape[2]
    cos = jnp.ones((rows, LANE), F32)
    sin = jnp.zeros((rows, LANE), F32)
    _mla_kv(ck_ref[0, 0], wk_ref.at[0], wv_ref.at[0], gk_ref.at[0], cos, sin,
            km_ref.at[0, 0], vm_ref.at[0, 0])


def _cache_kv(ck, w_k, w_v, g_km):
    db, depth, p, _ = ck.shape
    spec = lambda w: pl.BlockSpec((1, 1, p, w), lambda b, l: (b, l, 0, 0))
    wspec = lambda r: pl.BlockSpec((1, r, HW), lambda b, l: (l, 0, 0))
    return pl.pallas_call(
        _cache_kernel,
        grid=(db, depth),
        in_specs=[spec(256), wspec(256), wspec(256), wspec(1)],
        out_specs=[spec(HW), spec(HW)],
        out_shape=[jax.ShapeDtypeStruct((db, depth, p, HW), BF16)] * 2,
        compiler_params=_params(("arbitrary", "arbitrary")),
        name="cache_kv",
    )(ck, w_k, w_v, g_km)


def _softmax_av(s_list, v_list):
    m = s_list[0].max(axis=-1, keepdims=True)
    for s in s_list[1:]:
        m = jnp.maximum(m, s.max(axis=-1, keepdims=True))
    acc = None
    den = None
    for s, v in zip(s_list, v_list):
        p = jnp.exp(s - m)
        l = p.sum(axis=-1, keepdims=True)
        o = jnp.dot(p.astype(BF16), v, preferred_element_type=F32)
        acc = o if acc is None else acc + o
        den = l if den is None else den + l
    return acc / den


def _ctx_attn_kernel(qn, kn, vn, qm, km, vm, on, om):
    for q, k, v, o in ((qn, kn, vn, on), (qm, km, vm, om)):
        for h in range(HEADS):
            sl = slice(h * LANE, (h + 1) * LANE)
            s = _nt_dot(q[:, sl], k[:, sl])
            o[:, sl] = _softmax_av([s], [v[:, sl]]).astype(BF16)


def _ctx_attn(qn, kn, vn, qm, km, vm, seq):
    n = qn.shape[0]
    spec = pl.BlockSpec((seq, HW), lambda i: (i, 0))
    return pl.pallas_call(
        _ctx_attn_kernel,
        grid=(n // seq,),
        in_specs=[spec] * 6,
        out_specs=[spec] * 2,
        out_shape=[jax.ShapeDtypeStruct((n, HW), BF16)] * 2,
        compiler_params=_params(("arbitrary",)),
        name="ctx_attn",
    )(qn, kn, vn, qm, km, vm)


def _lat_mla_kernel(q, k, v, kc, vc, o):
    s1 = _nt_dot(q[...], k[...])
    s2 = _nt_dot(q[...], kc[0])
    o[...] = _softmax_av([s1, s2], [v[...], vc[0]]).astype(BF16)


def _lat_mla(qm, km, vm, kc, vc, db):
    n = qm.shape[0]
    ds = n // db
    nq = ds // TQ
    qspec = pl.BlockSpec((TQ, LANE), lambda b, h, i: (b * nq + i, h))
    kspec = pl.BlockSpec((ds, LANE), lambda b, h, i: (b, h))
    cspec = pl.BlockSpec((1, kc.shape[1], LANE), lambda b, h, i: (b, 0, h))
    return pl.pallas_call(
        _lat_mla_kernel,
        grid=(db, HEADS, nq),
        in_specs=[qspec, kspec, kspec, cspec, cspec],
        out_specs=qspec,
        out_shape=jax.ShapeDtypeStruct((n, HW), BF16),
        compiler_params=_params(("arbitrary",) * 3),
        name="lat_mla",
    )(qm, km, vm, kc, vc)


def _nat_kernel(q, k, v, kc, vc, bias, o, *, rows):
    r = pl.program_id(1)
    rs = jnp.clip(r - WIN_R // 2, 0, rows - WIN_R)
    start = pl.multiple_of(rs * GRID_W, GRID_W)
    band = WIN_R * GRID_W
    for h in range(HEADS):
        sl = slice(h * LANE, (h + 1) * LANE)
        qh = q[:, sl]
        s1 = _nt_dot(qh, k[pl.ds(start, band), sl]) + bias[0, h]
        s2 = _nt_dot(qh, kc[0, :, sl])
        o[:, sl] = _softmax_av([s1, s2], [v[pl.ds(start, band), sl], vc[0, :, sl]]).astype(BF16)


def _nat_attn(qn, kn, vn, kc, vc, bias, db):
    n = qn.shape[0]
    ds = n // db
    rows = ds // GRID_W
    band = WIN_R * GRID_W

    def variant(r):
        return jnp.where(r < WIN_R // 2, r, jnp.where(r > rows - WIN_R // 2, r - (rows - WIN_R), WIN_R // 2))

    qspec = pl.BlockSpec((GRID_W, HW), lambda b, r: (b * rows + r, 0))
    kspec = pl.BlockSpec((ds, HW), lambda b, r: (b, 0))
    cspec = pl.BlockSpec((1, kc.shape[1], HW), lambda b, r: (b, 0, 0))
    bspec = pl.BlockSpec((1, HEADS, GRID_W, band), lambda b, r: (variant(r), 0, 0, 0))
    return pl.pallas_call(
        functools.partial(_nat_kernel, rows=rows),
        grid=(db, rows),
        in_specs=[qspec, kspec, kspec, cspec, cspec, bspec],
        out_specs=qspec,
        out_shape=jax.ShapeDtypeStruct((n, HW), BF16),
        compiler_params=_params(("arbitrary", "arbitrary")),
        name="nat_attn",
    )(qn, kn, vn, kc, vc, bias)


def _split3(x):
    hi = x.astype(BF16)
    r = x - hi.astype(F32)
    mid = r.astype(BF16)
    lo = (r - mid.astype(F32)).astype(BF16)
    return hi, mid, lo


def _pool(p_prev, p_cur, p_next, posb, seq_len):
    rows = p_cur.shape[0]
    halo = p_prev.shape[0]
    ext = rows + 2 * halo
    pext = jnp.concatenate([p_prev, p_cur, p_next], axis=0)
    parts = _split3(pext)
    t = posb + lax.broadcasted_iota(jnp.int32, (rows, ext), 0)
    s = posb - halo + lax.broadcasted_iota(jnp.int32, (rows, ext), 1)
    tcol = posb + lax.broadcasted_iota(jnp.int32, (rows, 1), 0)
    grp = lax.broadcasted_iota(jnp.int32, (rows, 256), 1) // POOL_G
    d = jnp.zeros((rows, 256), F32)
    for gi, w in enumerate(POOL_WINDOWS):
        lo = jnp.maximum(t - w // 2, 0)
        hi = jnp.minimum(t + (w - w // 2), seq_len)
        sel = jnp.where(s >= lo, jnp.where(s < hi, 1.0, 0.0), 0.0).astype(BF16)
        tot = sum(jnp.dot(sel, part, preferred_element_type=F32) for part in parts)
        cnt = (jnp.minimum(tcol + (w - w // 2), seq_len) - jnp.maximum(tcol - w // 2, 0)).astype(F32)
        d = jnp.where(grp == gi, tot / cnt - p_cur, d)
    return d


def _first_max(x, pos, sentinel):
    m = jnp.max(x, axis=0, keepdims=True)
    idx = jnp.min(jnp.where(x == m, pos, sentinel), axis=0, keepdims=True)
    return m, idx


def _topk_stage1(qh, sk_ref):
    c = qh.shape[0]
    key_pos = lax.broadcasted_iota(jnp.int32, (PEER_NKEYS, c), 0).astype(F32)
    row16 = lax.broadcasted_iota(jnp.int32, (PEER_TOPK, c), 0)
    neg = jnp.float32(-jnp.inf)
    s0 = _nt_dot(sk_ref[0], qh)
    s1 = _nt_dot(sk_ref[1], qh)

    def stage1(a, carry):
        out = []
        for s, sv, si in (carry[0:3], carry[3:6]):
            m, idx = _first_max(s, key_pos, float(PEER_NKEYS))
            out += [jnp.where(key_pos == idx, neg, s),
                    jnp.where(row16 == a, m, sv), jnp.where(row16 == a, idx, si)]
        return tuple(out)

    zf = jnp.zeros((PEER_TOPK, c), F32)
    _, sv0, si0, _, sv1, si1 = lax.fori_loop(0, PEER_TOPK, stage1, (s0, zf, zf, s1, zf, zf))
    return sv0, si0, sv1, si1


def _topk_pieces(sv0, sv1):
    c = sv0.shape[1]
    neg = jnp.float32(-jnp.inf)
    sub8 = lax.broadcasted_iota(jnp.int32, (8, c), 0)
    sub8f = sub8.astype(F32)
    cs, cf = [], []
    for a in range(8):
        nb = PEER_TOPK // (a + 1)
        for b0 in range(0, nb, 8):
            val = sv0[a:a + 1] + sv1[b0:b0 + 8]
            if nb - b0 < 8:
                val = jnp.where(sub8 < nb - b0, val, neg)
            cs.append(val)
            cf.append(sub8f + float(a * PEER_TOPK + b0))
    cs.append(sv0[8:16] + sv1[0:1])
    cf.append((sub8f + 8.0) * float(PEER_TOPK))
    return cs, cf


def _topk_stage2(chains, cf):
    npc = len(cf)
    c = cf[0].shape[1]
    row16 = lax.broadcasted_iota(jnp.int32, (PEER_TOPK, c), 0)
    neg = jnp.float32(-jnp.inf)
    nflat = float(PEER_TOPK * PEER_TOPK)
    zf = jnp.zeros((PEER_TOPK, c), F32)

    def step(k, carry):
        out = []
        for ch in range(len(chains)):
            vals = carry[ch * (npc + 2):ch * (npc + 2) + npc]
            tv, tp = carry[ch * (npc + 2) + npc], carry[ch * (npc + 2) + npc + 1]
            m = vals[0]
            for v in vals[1:]:
                m = jnp.maximum(m, v)
            m = jnp.max(m, axis=0, keepdims=True)
            pos = None
            for v, f in zip(vals, cf):
                cand = jnp.where(v == m, f, nflat)
                pos = cand if pos is None else jnp.minimum(pos, cand)
            pos = jnp.min(pos, axis=0, keepdims=True)
            out += [jnp.where(f == pos, neg, v) for v, f in zip(vals, cf)]
            out += [jnp.where(row16 == k, m, tv), jnp.where(row16 == k, pos, tp)]
        return tuple(out)

    init = []
    for cs in chains:
        init += list(cs) + [zf, zf]
    res = lax.fori_loop(0, PEER_TOPK, step, tuple(init))
    return [(res[ch * (npc + 2) + npc], res[ch * (npc + 2) + npc + 1]) for ch in range(len(chains))]


def _topk_ids(tp, si0, si1):
    a = jnp.floor(tp * (1.0 / PEER_TOPK))
    b = tp - a * float(PEER_TOPK)
    ea = jnp.zeros_like(tp)
    eb = jnp.zeros_like(tp)
    for j in range(PEER_TOPK):
        ea = jnp.where(a == float(j), si0[j:j + 1], ea)
        eb = jnp.where(b == float(j), si1[j:j + 1], eb)
    return ea * float(PEER_NKEYS) + eb


def _out_kernel(on_ref, om_ref, pc_ref, pp_ref, pn_ref, x_ref, mod_ref,
                won_ref, wop_ref, wom_ref, pw_ref, ps_ref, n2_ref, wq_ref, sk_ref,
                x1_ref, h2_ref, ids_ref, gt_ref, gn_ref, h2c_ref, q_scr, idt_scr, *, bps, seq_len):
    d = x_ref.shape[1]
    rows = x_ref.shape[0]
    i = pl.program_id(0)
    mod = mod_ref[0]
    g1 = mod[:, 2 * d:3 * d]
    sh2 = mod[:, 3 * d:4 * d]
    sc2 = mod[:, 4 * d:5 * d]

    posb = (i % bps) * rows
    dpool = _pool(pp_ref[...], pc_ref[...], pn_ref[...], posb, seq_len)
    ypool = jnp.dot(dpool.astype(BF16), pw_ref[...], preferred_element_type=F32) * ps_ref[...]
    mix = (jnp.dot(on_ref[...], won_ref[...], preferred_element_type=F32)
           + jnp.dot(ypool.astype(BF16), wop_ref[...], preferred_element_type=F32)
           + jnp.dot(om_ref[...], wom_ref[...], preferred_element_type=F32))
    x1 = x_ref[...] + g1 * mix
    x1_ref[...] = x1
    h2 = _rms(x1, n2_ref[...]) * (1.0 + sc2) + sh2
    h2_ref[...] = h2
    for j in range(d // LANE):
        h2c_ref[j] = h2[:, j * LANE:(j + 1) * LANE]

    q = jnp.dot(h2.astype(BF16), wq_ref[...], preferred_element_type=F32)
    for hh in range(PEER_HEADS):
        q_scr[hh] = q[:, hh * LANE:(hh + 1) * LANE].astype(BF16)

    chunks = range(0, rows, LANE)

    def head(hh, _):
        sorted_keys = [_topk_stage1(q_scr[hh, c0:c0 + LANE, :], sk_ref) for c0 in chunks]
        pieces = [_topk_pieces(sv0, sv1) for sv0, _, sv1, _ in sorted_keys]
        picked = _topk_stage2([cs for cs, _ in pieces], pieces[0][1])
        r0 = pl.multiple_of(hh * PEER_TOPK, PEER_TOPK)
        for c0, (tv, tp), (_, si0, _, si1) in zip(chunks, picked, sorted_keys):
            ex = jnp.exp(tv - tv[0:1])
            gt_ref[pl.ds(r0, PEER_TOPK), c0:c0 + LANE] = ex / jnp.sum(ex, axis=0, keepdims=True)
            idt_scr[pl.ds(r0, PEER_TOPK), c0:c0 + LANE] = _topk_ids(tp, si0, si1)
        return 0

    lax.fori_loop(0, PEER_HEADS, head, 0)
    ids_ref[...] = idt_scr[...].T.astype(jnp.int32)
    gn_ref[...] = gt_ref[...].T


def _out_proj(x, on, om, p, mod_l, row_off, bpm, lw, seq_len):
    n, d = x.shape
    nb = n // TB
    bps = seq_len // TB
    halo = 8
    hb = TB // halo
    tok = lambda w: pl.BlockSpec((TB, w), lambda i: (i, 0))
    in_specs = [tok(HW), tok(HW), tok(256),
                pl.BlockSpec((halo, 256), lambda i: (jnp.maximum(i * hb - 1, 0), 0)),
                pl.BlockSpec((halo, 256), lambda i: (jnp.minimum((i + 1) * hb, n // halo - 1), 0)),
                tok(d),
                pl.BlockSpec((1, 1, mod_l.shape[-1]), lambda i: (row_off + i // bpm, 0, 0)),
                _const_spec((HW, d)), _const_spec((256, d)), _const_spec((HW, d)),
                _const_spec((256, 256)), _const_spec((1, 256)), _const_spec((1, d)),
                _const_spec((d, PEER_HEADS * LANE)), _const_spec((2, PEER_NKEYS, LANE))]
    nk = PEER_HEADS * PEER_TOPK
    return pl.pallas_call(
        functools.partial(_out_kernel, bps=bps, seq_len=seq_len),
        grid=(nb,),
        in_specs=in_specs,
        out_specs=[tok(d), tok(d), tok(nk), pl.BlockSpec((nk, TB), lambda i: (0, i)), tok(nk),
                   pl.BlockSpec((d // LANE, TB, LANE), lambda i: (0, i, 0))],
        out_shape=[jax.ShapeDtypeStruct((n, d), F32), jax.ShapeDtypeStruct((n, d), F32),
                   jax.ShapeDtypeStruct((n, nk), jnp.int32), jax.ShapeDtypeStruct((nk, n), F32),
                   jax.ShapeDtypeStruct((n, nk), F32),
                   jax.ShapeDtypeStruct((d // LANE, n, LANE), F32)],
        scratch_shapes=[pltpu.VMEM((PEER_HEADS, TB, LANE), BF16), pltpu.VMEM((nk, TB), F32)],
        compiler_params=_params(("arbitrary",)),
        name="out_proj",
    )(on, om, p, p, p, x, mod_l, lw["w_o_na"], lw["w_o_pool"], lw["w_o_mla"],
      lw["pool_w"], lw["pool_scale"], lw["norm2"], lw["peer_wq"], lw["peer_sk"])


def _gelu_tanh(x):
    return x * (0.5 * (1.0 + jnp.tanh(0.7978845608028654 * (x + 0.044715 * (x * x * x)))))


def _peer_token_mix(chunk, hrow, gcol, ch):
    acc = None
    for s in range(ch):
        us = lax.bitcast_convert_type(chunk(s) & jnp.int32(-65536), F32)
        term = us * hrow[:, s * LANE:(s + 1) * LANE]
        acc = term if acc is None else acc + term
    wgt = gcol * _gelu_tanh(jnp.sum(acc, axis=-1, keepdims=True))
    parts = []
    for s in range(ch):
        vs = lax.bitcast_convert_type(chunk(s) << 16, F32)
        parts.append(jnp.sum(vs * wgt, axis=0, keepdims=True))
    return jnp.concatenate(parts, axis=-1)


def _sc_peer(table3, ids, gates, h2c, n):
    ch, _, lane = h2c.shape
    nk = ids.shape[1]
    info = plsc.get_sparse_core_info()
    nc, nw, nl = info.num_cores, info.num_cores * info.num_subcores, info.num_lanes
    tpw = n // nw
    win = 32
    nq = nk // win
    cpr = lane // nl
    nchunk = ch * cpr
    hc = nchunk // 2
    assert n % nw == 0 and nk % win == 0 and win % nl == 0
    mesh = plsc.VectorSubcoreMesh(core_axis_name="core", subcore_axis_name="subcore")
    hi_mask = jnp.int32(-65536)

    @functools.partial(
        pl.kernel, mesh=mesh,
        out_type=jax.ShapeDtypeStruct((ch, n, lane), F32),
        compiler_params=pltpu.CompilerParams(needs_layout_passes=False),
        scratch_types=[pltpu.VMEM((nk,), jnp.int32), pltpu.VMEM((nk,), F32),
                       pltpu.VMEM((ch, lane), F32), pltpu.VMEM((ch, lane), F32),
                       pltpu.VMEM((win, ch, lane), jnp.int32), pltpu.VMEM((win, ch, lane), jnp.int32),
                       pltpu.VMEM((win * nl,), F32), pltpu.VMEM((win,), F32),
                       pltpu.SemaphoreType.DMA, pltpu.SemaphoreType.DMA, pltpu.SemaphoreType.DMA])
    def peer(tab_hbm, ids_hbm, g_hbm, h2_hbm, y_hbm,
             idx_v, g_v, x_v, y_v, rows_a, rows_b, part_v, w_v, sem_a, sem_b, sem_x):
        wid = lax.axis_index("subcore") * nc + lax.axis_index("core")
        bufs = ((rows_a, sem_a), (rows_b, sem_b))
        lanes = lax.iota(jnp.int32, nl)
        zero = jnp.zeros((nl,), F32)

        def chunk_copies(tok, to_hbm):
            if to_hbm:
                return [pltpu.make_async_copy(y_v.at[j], y_hbm.at[j, tok], sem_x) for j in range(ch)]
            return [pltpu.make_async_copy(h2_hbm.at[j, tok], x_v.at[j], sem_x) for j in range(ch)]

        def fetch(q):
            rows, sem = bufs[q % 2]
            return pltpu.make_async_copy(tab_hbm.at[idx_v.at[pl.ds(q * win, win)]], rows, sem)

        def word(rows, r, cc):
            return rows[r, cc // cpr, pl.ds((cc % cpr) * nl, nl)]

        @pl.loop(0, tpw)
        def _(ti):
            tok = wid * tpw + ti
            loads = chunk_copies(tok, False)
            for cp in loads:
                cp.start()
            pltpu.sync_copy(ids_hbm.at[tok], idx_v)
            pltpu.sync_copy(g_hbm.at[tok], g_v)
            for cp in loads:
                cp.wait()
            for cc in range(nchunk):
                y_v[cc // cpr, pl.ds((cc % cpr) * nl, nl)] = zero
            fetch(0).start()
            for q in range(nq):
                rows = bufs[q % 2][0]
                fetch(q).wait()
                if q + 1 < nq:
                    fetch(q + 1).start()

                for half in range(2):
                    xs = [x_v[(half * hc + c) // cpr, pl.ds(((half * hc + c) % cpr) * nl, nl)]
                          for c in range(hc)]

                    @pl.loop(0, win, step=2)
                    def _(r0):
                        pos = [pl.multiple_of((r0 + k) * nl, nl) for k in range(2)]
                        accs = [[part_v[pl.ds(pos[k], nl)] if half else None, None, None, None]
                                for k in range(2)]
                        for c in range(hc):
                            for k in range(2):
                                u = lax.bitcast_convert_type(
                                    word(rows, r0 + k, half * hc + c) & hi_mask, F32)
                                t = u * xs[c]
                                accs[k][c % 4] = t if accs[k][c % 4] is None else accs[k][c % 4] + t
                        for k in range(2):
                            part_v[pl.ds(pos[k], nl)] = (accs[k][0] + accs[k][1]) + (accs[k][2] + accs[k][3])

                for grp in range(win // nl):
                    s = zero
                    for rr in range(nl):
                        tot = jnp.sum(part_v[pl.ds((grp * nl + rr) * nl, nl)])
                        s = jnp.where(lanes == rr, tot, s)
                    z = 0.7978845608028654 * (s + 0.044715 * (s * s * s))
                    tanh = 1.0 - 2.0 / (jnp.exp(2.0 * z) + 1.0)
                    gate = g_v[pl.ds(q * win + grp * nl, nl)]
                    w_v[pl.ds(grp * nl, nl)] = gate * (s * (0.5 * (1.0 + tanh)))

                vc = nchunk // 4
                for part in range(4):
                    def body(i, yacc):
                        r0 = i * 4
                        wr = [plsc.load_gather(w_v, [jnp.full((nl,), r0 + k, jnp.int32)])
                              for k in range(4)]
                        out = []
                        for c in range(vc):
                            t = [wr[k] * lax.bitcast_convert_type(
                                word(rows, r0 + k, part * vc + c) << 16, F32) for k in range(4)]
                            out.append(yacc[c] + ((t[0] + t[1]) + (t[2] + t[3])))
                        return tuple(out)

                    yacc = lax.fori_loop(0, win // 4, body, tuple(zero for _ in range(vc)))
                    for c in range(vc):
                        cc = part * vc + c
                        sl = (cc // cpr, pl.ds((cc % cpr) * nl, nl))
                        y_v[sl] = y_v[sl] + yacc[c]
            stores = chunk_copies(tok, True)
            for cp in stores:
                cp.start()
            for cp in stores:
                cp.wait()

    return peer(table3, ids, gates, h2c)


def _residual_kernel(x1_ref, y_ref, mod_ref, x2_hbm, after_hbm, o_ref):
    del x2_hbm
    del after_hbm
    d = x1_ref.shape[1]
    g2 = mod_ref[0][:, 5 * d:6 * d]
    for j in range(d // LANE):
        sl = slice(j * LANE, (j + 1) * LANE)
        o_ref[:, sl] = x1_ref[:, sl] + g2[:, sl] * y_ref[j]


def _residual(x1, y, mod_l, row, x2, after):
    n, d = x1.shape
    tok = pl.BlockSpec((PEER_TB, d), lambda i: (i, 0))
    any_spec = pl.BlockSpec(memory_space=pl.ANY)
    return pl.pallas_call(
        _residual_kernel,
        grid=(y.shape[1] // PEER_TB,),
        in_specs=[tok, pl.BlockSpec((d // LANE, PEER_TB, LANE), lambda i: (0, i, 0)),
                  pl.BlockSpec((1, 1, mod_l.shape[-1]), lambda i: (row, 0, 0)),
                  any_spec, any_spec],
        out_specs=tok,
        out_shape=jax.ShapeDtypeStruct((n, d), F32),
        input_output_aliases={3: 0},
        compiler_params=_params(("arbitrary",)),
        name="residual",
    )(x1, y, mod_l, x2, after)


def _peer_kernel(ids_hbm, gt_ref, h2_ref, x1_ref, mod_ref, tab_hbm, o_ref,
                 ids_s, buf, sem_i, sem_r, *, first_block):
    d = x1_ref.shape[1]
    ch = d // LANE
    pitch = ch + 1
    nsub = x1_ref.shape[0] // PEER_SUB
    nk = gt_ref.shape[0]
    nids = PEER_SUB * nk
    i = pl.program_id(0) + first_block
    g2 = mod_ref[0][:, 5 * d:6 * d]
    tok_lane = lax.broadcasted_iota(jnp.int32, gt_ref.shape, 1)

    def ids_copy(j, slot):
        start = pl.multiple_of((i * nsub + j) * nids, nids)
        return pltpu.make_async_copy(ids_hbm.at[pl.ds(start, nids)],
                                     ids_s.at[pl.ds(slot * nids, nids)], sem_i.at[slot])

    def row_copy(slot, e, f):
        src = tab_hbm.at[pl.ds(pl.multiple_of(e * ch, ch), ch), :]
        dst = buf.at[slot, pl.ds(f * pitch, ch), :]
        return pltpu.make_async_copy(src, dst, sem_r.at[slot])

    def issue_rows(slot):
        for t in range(PEER_SUB):
            def body(kk, _):
                for r in range(8):
                    f = t * nk + kk * 8 + r
                    row_copy(slot, ids_s[slot * nids + f], f).start(priority=r % 2)
                return 0

            lax.fori_loop(0, nk // 8, body, 0)

    def wait_rows(slot):
        done = buf.at[slot, pl.ds(0, nids * ch), :]
        pltpu.make_async_copy(done, done, sem_r.at[slot]).wait()

    def compute(slot, j):
        base = pl.multiple_of(j * PEER_SUB, PEER_SUB)
        h8 = h2_ref[pl.ds(base, PEER_SUB), :]
        ys = []
        for t in range(PEER_SUB):
            chunk = lambda s: buf[slot, pl.ds(t * nk * pitch + s, nk, stride=pitch), :]
            gcol = jnp.sum(jnp.where(tok_lane == base + t, gt_ref[...], 0.0), axis=-1, keepdims=True)
            ys.append(_peer_token_mix(chunk, h8[t:t + 1, :], gcol, ch))
        y8 = jnp.concatenate(ys, axis=0)
        o_ref[pl.ds(base, PEER_SUB), :] = x1_ref[pl.ds(base, PEER_SUB), :] + g2 * y8

    first = ids_copy(0, 0)
    first.start()
    first.wait()
    issue_rows(0)
    ids_copy(1, 1).start()

    def pair(jj, _):
        j0 = 2 * jj
        ids_copy(j0 + 1, 1).wait()
        issue_rows(1)

        @pl.when(j0 + 2 < nsub)
        def _():
            ids_copy(j0 + 2, 0).start()

        wait_rows(0)
        compute(0, j0)

        @pl.when(j0 + 2 < nsub)
        def _():
            ids_copy(j0 + 2, 0).wait()
            issue_rows(0)

        @pl.when(j0 + 3 < nsub)
        def _():
            ids_copy(j0 + 3, 1).start()

        wait_rows(1)
        compute(1, j0 + 1)
        return 0

    lax.fori_loop(0, nsub // 2, pair, 0)


def _pack_tables(peer_u, peer_v):
    e, d = peer_u.shape
    ub = lax.bitcast_convert_type(peer_u.astype(BF16), jnp.uint16).astype(jnp.uint32)
    vb = lax.bitcast_convert_type(peer_v.astype(BF16), jnp.uint16).astype(jnp.uint32)
    words = lax.bitcast_convert_type((ub << 16) | vb, jnp.int32)
    return words.reshape(e, d // LANE, LANE)


def _peer(x1, h2, ids, gt, mod_l, row, table, tok0):
    n, d = x1.shape
    nk = gt.shape[0]
    b0 = tok0 // PEER_TB
    nb = n // PEER_TB - b0
    tok = pl.BlockSpec((PEER_TB, d), lambda i: (i + b0, 0))
    any_spec = pl.BlockSpec(memory_space=pl.ANY)
    return pl.pallas_call(
        functools.partial(_peer_kernel, first_block=b0),
        grid=(nb,),
        in_specs=[any_spec,
                  pl.BlockSpec((nk, PEER_TB), lambda i: (0, i + b0)),
                  tok, tok,
                  pl.BlockSpec((1, 1, mod_l.shape[-1]), lambda i: (row, 0, 0)),
                  any_spec],
        out_specs=tok,
        out_shape=jax.ShapeDtypeStruct((n, d), F32),
        scratch_shapes=[pltpu.SMEM((2 * PEER_SUB * nk,), jnp.int32),
                        pltpu.VMEM((2, PEER_SUB * nk * (d // LANE + 1), LANE), jnp.int32),
                        pltpu.SemaphoreType.DMA((2,)),
                        pltpu.SemaphoreType.DMA((2,))],
        compiler_params=_params(("arbitrary",)),
        name="peer",
    )(ids.reshape(n * nk), gt, h2, x1, mod_l, table.reshape(-1, LANE))


def _pad_heads(w, width):
    pad = [(0, 0)] * (w.ndim - 1) + [(0, LANE - width)]
    w = jnp.pad(w, pad)
    return w.reshape(w.shape[:-2] + (HW,))


def _head_gain(g, width):
    depth = g.shape[0]
    g = jnp.pad(g, ((0, 0), (0, LANE - width)))
    return jnp.tile(g, (1, HEADS)).reshape(depth, 1, HW)


def _rope_tables(seq):
    t = np.arange(seq)
    half = MLA_ROPE // 2
    inv = ROPE_THETA ** (-np.arange(0, half, 2, dtype=np.float32) / half)
    cos = np.ones((seq, LANE), np.float32)
    sin = np.zeros((seq, LANE), np.float32)
    for off, pos in ((MLA_NOPE, t // GRID_W), (MLA_NOPE + half, t % GRID_W)):
        ang = pos.astype(np.float32)[:, None] * inv[None, :]
        q = half // 2
        cos[:, off:off + q] = np.cos(ang)
        cos[:, off + q:off + half] = np.cos(ang)
        sin[:, off:off + q] = -np.sin(ang)
        sin[:, off + q:off + half] = np.sin(ang)
    return jnp.asarray(cos), jnp.asarray(sin)


def _nat_bias(rel_bias):
    v = np.arange(WIN_R)[:, None]
    j = np.arange(WIN_R)[None, :]
    dr = j - v + WIN_R - 1
    cq = np.arange(GRID_W)[:, None]
    kc = np.arange(GRID_W)[None, :]
    cstart = np.clip(cq - WIN_C // 2, 0, GRID_W - WIN_C)
    ok = (kc >= cstart) & (kc < cstart + WIN_C)
    dc = np.clip(kc - cq + WIN_C - 1, 0, 2 * WIN_C - 2)
    b = rel_bias[:, :, dr]
    b = b[..., dc]
    b = jnp.where(jnp.asarray(ok)[None, None, None, None], b, NEG_INF)
    b = jnp.transpose(b, (0, 2, 1, 4, 3, 5))
    return b.reshape(b.shape[0], WIN_R, HEADS, GRID_W, WIN_R * GRID_W)


def _layer_weights(w_in, na_q_norm, na_k_norm, mla_cq_norm, mla_ckv_norm, mla_w_uq, mla_w_ukv,
                   mla_q_norm, mla_k_norm, w_out, pool_w, pool_scale, norm1, norm2,
                   peer_wq, peer_subkeys):
    depth, d, _ = w_in.shape
    na_w = HEADS * NA_DH
    segs = np.cumsum([0, na_w, na_w, na_w, 256, 256, 128, MLA_ROPE])
    part = lambda i: w_in[:, :, segs[i]:segs[i + 1]]
    heads = lambda w: _pad_heads(w.reshape(depth, d, HEADS, NA_DH), NA_DH)
    w_in_p = jnp.concatenate(
        [heads(part(0)), heads(part(1)), heads(part(2)), part(3), part(4), part(5),
         jnp.pad(part(6), ((0, 0), (0, 0), (0, LANE - MLA_ROPE)))], axis=-1).astype(BF16)

    w_uq = _pad_heads(mla_w_uq, MLA_QK).astype(BF16)
    k_nope = _pad_heads(mla_w_ukv[..., :MLA_NOPE], MLA_NOPE)
    eye = np.zeros((MLA_ROPE, HEADS, LANE), np.float32)
    for h in range(HEADS):
        eye[np.arange(MLA_ROPE), h, MLA_NOPE + np.arange(MLA_ROPE)] = 1.0
    eye = jnp.broadcast_to(jnp.asarray(eye.reshape(MLA_ROPE, HW)), (depth, MLA_ROPE, HW))
    zer = jnp.zeros((depth, 256 - 128 - MLA_ROPE, HW), F32)
    w_k = jnp.concatenate([k_nope, eye, zer], axis=1).astype(BF16)
    w_v = jnp.concatenate([_pad_heads(mla_w_ukv[..., MLA_NOPE:], MLA_V),
                           jnp.zeros((depth, 128, HW), F32)], axis=1).astype(BF16)

    mix_w = HEADS * NA_DH
    w_o_na = jnp.pad(w_out[:, :mix_w].reshape(depth, HEADS, NA_DH, d),
                     ((0, 0), (0, 0), (0, LANE - NA_DH), (0, 0))).reshape(depth, HW, d).astype(BF16)
    w_o_pool = w_out[:, mix_w:mix_w + 256].astype(BF16)
    w_o_mla = jnp.pad(w_out[:, mix_w + 256:].reshape(depth, HEADS, MLA_V, d),
                      ((0, 0), (0, 0), (0, LANE - MLA_V), (0, 0))).reshape(depth, HW, d).astype(BF16)
    ng = len(POOL_WINDOWS)
    pw = jnp.zeros((depth, ng * POOL_G, ng * POOL_G), F32)
    for g in range(ng):
        pw = pw.at[:, g * POOL_G:(g + 1) * POOL_G, g * POOL_G:(g + 1) * POOL_G].set(pool_w[:, g])

    half = peer_subkeys.shape[-1]
    sk = jnp.stack([jnp.pad(peer_subkeys[:, 0], ((0, 0), (0, 0), (0, LANE - half))),
                    jnp.pad(peer_subkeys[:, 1], ((0, 0), (0, 0), (LANE - half, 0)))], axis=1).astype(BF16)

    return dict(
        w_in=w_in_p, w_uq=w_uq, w_k=w_k, w_v=w_v,
        g_q=_head_gain(na_q_norm, NA_DH), g_k=_head_gain(na_k_norm, NA_DH),
        g_cq=mla_cq_norm[:, None, :], g_ckv=mla_ckv_norm[:, None, :],
        g_qm=_head_gain(mla_q_norm, MLA_QK), g_km=_head_gain(mla_k_norm, MLA_QK),
        w_o_na=w_o_na, w_o_pool=w_o_pool, w_o_mla=w_o_mla,
        pool_w=pw.astype(BF16), pool_scale=pool_scale[:, None, :],
        norm1=norm1[:, None, :], norm2=norm2[:, None, :],
        peer_wq=peer_wq.astype(BF16), peer_sk=sk)


def kernel(x_prompt, x_sample, c, cache_nat_k, cache_nat_v, cache_mla_ckv, cache_mla_krope, c_ctx, w_mod, b_mod, norm1, norm2, w_in, na_q_norm, na_k_norm, na_rel_bias, pool_w, pool_scale, mla_cq_norm, mla_ckv_norm, mla_w_uq, mla_w_ukv, mla_q_norm, mla_k_norm, w_out, peer_wq, peer_subkeys, peer_u, peer_v):
    batch, seq, d = x_prompt.shape
    db, ds, _ = x_sample.shape
    depth = w_mod.shape[0]
    past = cache_nat_k.shape[2]
    assert seq == TB and ds % TB == 0 and ds % (GRID_W * WIN_R) == 0 and db + 1 <= 8

    cond8 = jnp.concatenate([c_ctx[None, :], c, jnp.zeros((8 - 1 - db, d), F32)], axis=0)
    mod = _modulation(cond8, w_mod, b_mod).reshape(depth, 8, 1, 6 * d)

    lw_all = _layer_weights(w_in, na_q_norm, na_k_norm, mla_cq_norm, mla_ckv_norm, mla_w_uq,
                            mla_w_ukv, mla_q_norm, mla_k_norm, w_out, pool_w, pool_scale,
                            norm1, norm2, peer_wq, peer_subkeys)
    bias_all = _nat_bias(na_rel_bias)
    tables = [_pack_tables(peer_u[l], peer_v[l]) for l in range(depth)]
    cos_lat, sin_lat = _rope_tables(ds)
    cos_ctx = jnp.ones((TB, LANE), F32)
    sin_ctx = jnp.zeros((TB, LANE), F32)

    ck = jnp.concatenate([cache_mla_ckv, cache_mla_krope,
                          jnp.zeros(cache_mla_ckv.shape[:-1] + (256 - 128 - MLA_ROPE,), F32)],
                         axis=-1).astype(BF16)
    kc_mla, vc_mla = _cache_kv(ck, lw_all["w_k"], lw_all["w_v"], lw_all["g_km"])
    kc_na = _pad_heads(cache_nat_k, NA_DH).astype(BF16)
    vc_na = _pad_heads(cache_nat_v, NA_DH).astype(BF16)

    xs = [x_prompt.reshape(batch * seq, d)] + [x_sample[b] for b in range(db)]
    one_row = max(batch * seq, ds) + 1
    lat_bpm = ds // TB
    ks, vs, ckvs, krs = [], [], [], []
    pending = None
    after = xs[0]

    def join(item, follow):
        si, x1, y_sc, x2, mod_l = item
        xs[si] = _residual(x1, y_sc, mod_l, si, x2, follow)
        return xs[si]

    for l in range(depth):
        lw = {k: v[l] for k, v in lw_all.items()}
        mod_l = mod[l]
        for si in range(db + 1):
            x = xs[si]
            if si == 0:
                (qn, kn, vn, knf, vnf, p, qm, km, vm, ckv, kr) = _in_proj(
                    x, mod_l, 0, one_row, lw, cos_ctx, sin_ctx, 1, after)
                on, om = _ctx_attn(qn, kn, vn, qm, km, vm, seq)
                x1, h2, ids, gt, gn, h2c = _out_proj(x, on, om, p, mod_l, 0, one_row, lw, seq)
                ks.append(knf.reshape(batch, seq, HEADS, LANE)[..., :NA_DH])
                vs.append(vnf.reshape(batch, seq, HEADS, LANE)[..., :NA_DH])
                ckvs.append(ckv.reshape(batch, seq, 128))
                krs.append(kr.reshape(batch, seq, LANE)[..., :MLA_ROPE])
            else:
                b = si - 1
                (qn, kn, vn, _, _, p, qm, km, vm, _, _) = _in_proj(
                    x, mod_l, si, one_row, lw, cos_lat, sin_lat, lat_bpm, after)
                on = _nat_attn(qn, kn, vn, kc_na[b:b + 1, l], vc_na[b:b + 1, l], bias_all[l], 1)
                om = _lat_mla(qm, km, vm, kc_mla[b:b + 1, l], vc_mla[b:b + 1, l], 1)
                x1, h2, ids, gt, gn, h2c = _out_proj(x, on, om, p, mod_l, si, one_row, lw, ds)
            n_sc = x.shape[0] * SC_SHARE[0] // SC_SHARE[1] // PEER_TB * PEER_TB
            y_sc = _sc_peer(tables[l], ids, gn, h2c, n_sc)
            x2 = _peer(x1, h2, ids, gt, mod_l, si, tables[l], n_sc)
            after = x2 if pending is None else join(pending, x2)
            pending = (si, x1, y_sc, x2, mod_l)
    join(pending, pending[1])

    return (xs[0].reshape(batch, seq, d), jnp.stack(xs[1:], axis=0),
            jnp.stack(ks, axis=1), jnp.stack(vs, axis=1),
            jnp.stack(ckvs, axis=1), jnp.stack(krs, axis=1))
```

```python
import functools

import numpy as np
import jax
import jax.numpy as jnp
from jax import lax
from jax.experimental import pallas as pl
from jax.experimental.pallas import tpu as pltpu
from jax.experimental.pallas import tpu_sc as plsc

F32 = jnp.float32
BF16 = jnp.bfloat16

EPS = 1e-6
ROPE_THETA = 10000.0
NEG_INF = -1e30
GRID_W = 64
HEADS = 6
NA_DH = 64
WIN_R = 8
WIN_C = 16
POOL_WINDOWS = (2, 4, 8, 16)
POOL_G = 64
MLA_NOPE = 64
MLA_ROPE = 32
MLA_QK = MLA_NOPE + MLA_ROPE
MLA_V = 64
PEER_HEADS = 8
PEER_NKEYS = 128
PEER_TOPK = 16
LANE = 128
HW = HEADS * LANE
TB = 256
TQ = 256
PEER_TB = 128
PEER_SUB = 8
VMEM_LIMIT = 56 * 1024 * 1024
SC_SHARE = (3, 4)

_CQ, _CK, _CV = 0, HW, 2 * HW
_CP = 3 * HW
_CCQ = _CP + 256
_CCKV = _CCQ + 256
_CKR = _CCKV + 128
IN_W = _CKR + 128


def _params(sem, vmem=VMEM_LIMIT):
    return pltpu.CompilerParams(dimension_semantics=sem, vmem_limit_bytes=vmem)


def _const_spec(shape):
    n = len(shape)
    return pl.BlockSpec(shape, lambda *_: (0,) * n)


def _nt_dot(a, b):
    return lax.dot_general(a, b, (((1,), (1,)), ((), ())), preferred_element_type=F32)


def _mod_kernel(c_ref, w_ref, b_ref, o_ref):
    c = c_ref[...]
    s = c / (1.0 + jnp.exp(-c))
    o_ref[0] = jnp.dot(s, w_ref[0], preferred_element_type=F32,
                       precision=lax.Precision.HIGHEST) + b_ref[0]


def _modulation(cond8, w_mod, b_mod):
    depth, d, n6 = w_mod.shape
    tn = n6 // 4
    return pl.pallas_call(
        _mod_kernel,
        grid=(depth, n6 // tn),
        in_specs=[_const_spec((8, d)),
                  pl.BlockSpec((1, d, tn), lambda l, j: (l, 0, j)),
                  pl.BlockSpec((1, 1, tn), lambda l, j: (l, 0, j))],
        out_specs=pl.BlockSpec((1, 8, tn), lambda l, j: (l, 0, j)),
        out_shape=jax.ShapeDtypeStruct((depth, 8, n6), F32),
        compiler_params=_params(("arbitrary", "arbitrary")),
        name="modulation",
    )(cond8, w_mod, b_mod.reshape(depth, 1, n6))


def _rms(z, gain):
    return z * lax.rsqrt(jnp.mean(z * z, axis=-1, keepdims=True) + EPS) * gain


def _head_rms(zh, gain_h, n_real):
    ms = jnp.sum(zh * zh, axis=-1, keepdims=True) * (1.0 / n_real)
    return zh * lax.rsqrt(ms + EPS) * gain_h


def _rope(zh, cos, sin, is_x1):
    rot = jnp.where(is_x1, pltpu.roll(zh, LANE - 8, 1), pltpu.roll(zh, 8, 1))
    return zh * cos + rot * sin


def _is_x1(rows):
    lane = lax.broadcasted_iota(jnp.int32, (rows, LANE), 1)
    first = jnp.where(lane >= MLA_NOPE, jnp.where(lane < MLA_NOPE + 8, 1, 0), 0)
    second = jnp.where(lane >= MLA_NOPE + 16, jnp.where(lane < MLA_NOPE + 24, 1, 0), 0)
    return (first + second) > 0


def _mla_kv(ck, wk_ref, wv_ref, gk_ref, cos, sin, km_ref, vm_ref):
    rows = ck.shape[0]
    kk = jnp.dot(ck, wk_ref[...], preferred_element_type=F32)
    is_x1 = _is_x1(rows)
    for h in range(HEADS):
        sl = slice(h * LANE, (h + 1) * LANE)
        kh = _head_rms(kk[:, sl], gk_ref[:, sl], MLA_QK)
        km_ref[:, sl] = _rope(kh, cos, sin, is_x1).astype(BF16)
    vm_ref[...] = jnp.dot(ck, wv_ref[...], preferred_element_type=F32).astype(BF16)


def _in_kernel(x_ref, mod_ref, n1_ref, w_ref, wuq_ref, wk_ref, wv_ref,
               gq_ref, gk_ref, gcq_ref, gckv_ref, gqm_ref, gkm_ref, cos_ref, sin_ref, after_hbm,
               qn_ref, kn_ref, vn_ref, knf_ref, vnf_ref, p_ref,
               qm_ref, km_ref, vm_ref, ckv_ref, kr_ref):
    del after_hbm
    d = x_ref.shape[1]
    rows = x_ref.shape[0]
    mod = mod_ref[0]
    sh1 = mod[:, 0:d]
    sc1 = mod[:, d:2 * d]
    h = _rms(x_ref[...], n1_ref[...]) * (1.0 + sc1) + sh1
    hb = h.astype(BF16)

    def proj(lo, hi):
        return jnp.dot(hb, w_ref[:, lo:hi], preferred_element_type=F32)

    cos = cos_ref[...]
    sin = sin_ref[...]
    is_x1 = _is_x1(rows)

    zq = proj(_CQ, _CQ + HW)
    zk = proj(_CK, _CK + HW)
    for hh in range(HEADS):
        sl = slice(hh * LANE, (hh + 1) * LANE)
        qn_ref[:, sl] = (_head_rms(zq[:, sl], gq_ref[:, sl], NA_DH) * (NA_DH ** -0.5)).astype(BF16)
        kh = _head_rms(zk[:, sl], gk_ref[:, sl], NA_DH)
        knf_ref[:, sl] = kh
        kn_ref[:, sl] = kh.astype(BF16)
    zv = proj(_CV, _CV + HW)
    vnf_ref[...] = zv
    vn_ref[...] = zv.astype(BF16)
    p_ref[...] = proj(_CP, _CP + 256)

    cq = _rms(proj(_CCQ, _CCQ + 256), gcq_ref[...])
    zqm = jnp.dot(cq.astype(BF16), wuq_ref[...], preferred_element_type=F32)
    for hh in range(HEADS):
        sl = slice(hh * LANE, (hh + 1) * LANE)
        qh = _head_rms(zqm[:, sl], gqm_ref[:, sl], MLA_QK)
        qm_ref[:, sl] = (_rope(qh, cos, sin, is_x1) * (MLA_QK ** -0.5)).astype(BF16)

    ckv = _rms(proj(_CCKV, _CCKV + 128), gckv_ref[...])
    kr = proj(_CKR, _CKR + 128)
    ckv_ref[...] = ckv
    kr_ref[...] = kr
    ck = jnp.concatenate([ckv, kr], axis=-1).astype(BF16)
    _mla_kv(ck, wk_ref, wv_ref, gkm_ref, cos, sin, km_ref, vm_ref)


def _in_proj(x, mod_l, row_off, bpm, lw, cos_t, sin_t, rope_blocks, after):
    n, d = x.shape
    nb = n // TB
    tok = lambda w: pl.BlockSpec((TB, w), lambda i: (i, 0))
    rope_spec = pl.BlockSpec((TB, LANE), lambda i: (i % rope_blocks, 0))
    in_specs = [tok(d),
                pl.BlockSpec((1, 1, mod_l.shape[-1]), lambda i: (row_off + i // bpm, 0, 0)),
                _const_spec((1, d)), _const_spec((d, IN_W)), _const_spec((256, HW)),
                _const_spec((256, HW)), _const_spec((256, HW)),
                _const_spec((1, HW)), _const_spec((1, HW)), _const_spec((1, 256)),
                _const_spec((1, 128)), _const_spec((1, HW)), _const_spec((1, HW)),
                rope_spec, rope_spec, pl.BlockSpec(memory_space=pl.ANY)]
    widths = [(HW, BF16), (HW, BF16), (HW, BF16), (HW, F32), (HW, F32), (256, F32),
              (HW, BF16), (HW, BF16), (HW, BF16), (128, F32), (128, F32)]
    return pl.pallas_call(
        _in_kernel,
        grid=(nb,),
        in_specs=in_specs,
        out_specs=[tok(w) for w, _ in widths],
        out_shape=[jax.ShapeDtypeStruct((n, w), dt) for w, dt in widths],
        compiler_params=_params(("arbitrary",)),
        name="in_proj",
    )(x, mod_l, lw["norm1"], lw["w_in"], lw["w_uq"], lw["w_k"], lw["w_v"],
      lw["g_q"], lw["g_k"], lw["g_cq"], lw["g_ckv"], lw["g_qm"], lw["g_km"], cos_t, sin_t, after)


def _cache_kernel(ck_ref, wk_ref, wv_ref, gk_ref, km_ref, vm_ref):
    rows = ck_ref.shape[2]
    cos = jnp.ones((rows, LANE), F32)
    sin = jnp.zeros((rows, LANE), F32)
    _mla_kv(ck_ref[0, 0], wk_ref.at[0], wv_ref.at[0], gk_ref.at[0], cos, sin,
            km_ref.at[0, 0], vm_ref.at[0, 0])


def _cache_kv(ck, w_k, w_v, g_km):
    db, depth, p, _ = ck.shape
    spec = lambda w: pl.BlockSpec((1, 1, p, w), lambda b, l: (b, l, 0, 0))
    wspec = lambda r: pl.BlockSpec((1, r, HW), lambda b, l: (l, 0, 0))
    return pl.pallas_call(
        _cache_kernel,
        grid=(db, depth),
        in_specs=[spec(256), wspec(256), wspec(256), wspec(1)],
        out_specs=[spec(HW), spec(HW)],
        out_shape=[jax.ShapeDtypeStruct((db, depth, p, HW), BF16)] * 2,
        compiler_params=_params(("arbitrary", "arbitrary")),
        name="cache_kv",
    )(ck, w_k, w_v, g_km)


def _softmax_av(s_list, v_list):
    m = s_list[0].max(axis=-1, keepdims=True)
    for s in s_list[1:]:
        m = jnp.maximum(m, s.max(axis=-1, keepdims=True))
    acc = None
    den = None
    for s, v in zip(s_list, v_list):
        p = jnp.exp(s - m)
        l = p.sum(axis=-1, keepdims=True)
        o = jnp.dot(p.astype(BF16), v, preferred_element_type=F32)
        acc = o if acc is None else acc + o
        den = l if den is None else den + l
    return acc / den


def _ctx_attn_kernel(qn, kn, vn, qm, km, vm, on, om):
    for q, k, v, o in ((qn, kn, vn, on), (qm, km, vm, om)):
        for h in range(HEADS):
            sl = slice(h * LANE, (h + 1) * LANE)
            s = _nt_dot(q[:, sl], k[:, sl])
            o[:, sl] = _softmax_av([s], [v[:, sl]]).astype(BF16)


def _ctx_attn(qn, kn, vn, qm, km, vm, seq):
    n = qn.shape[0]
    spec = pl.BlockSpec((seq, HW), lambda i: (i, 0))
    return pl.pallas_call(
        _ctx_attn_kernel,
        grid=(n // seq,),
        in_specs=[spec] * 6,
        out_specs=[spec] * 2,
        out_shape=[jax.ShapeDtypeStruct((n, HW), BF16)] * 2,
        compiler_params=_params(("arbitrary",)),
        name="ctx_attn",
    )(qn, kn, vn, qm, km, vm)


def _lat_mla_kernel(q, k, v, kc, vc, o):
    s1 = _nt_dot(q[...], k[...])
    s2 = _nt_dot(q[...], kc[0])
    o[...] = _softmax_av([s1, s2], [v[...], vc[0]]).astype(BF16)


def _lat_mla(qm, km, vm, kc, vc, db):
    n = qm.shape[0]
    ds = n // db
    nq = ds // TQ
    qspec = pl.BlockSpec((TQ, LANE), lambda b, h, i: (b * nq + i, h))
    kspec = pl.BlockSpec((ds, LANE), lambda b, h, i: (b, h))
    cspec = pl.BlockSpec((1, kc.shape[1], LANE), lambda b, h, i: (b, 0, h))
    return pl.pallas_call(
        _lat_mla_kernel,
        grid=(db, HEADS, nq),
        in_specs=[qspec, kspec, kspec, cspec, cspec],
        out_specs=qspec,
        out_shape=jax.ShapeDtypeStruct((n, HW), BF16),
        compiler_params=_params(("arbitrary",) * 3),
        name="lat_mla",
    )(qm, km, vm, kc, vc)


def _nat_kernel(q, k, v, kc, vc, bias, o, *, rows):
    r = pl.program_id(1)
    rs = jnp.clip(r - WIN_R // 2, 0, rows - WIN_R)
    start = pl.multiple_of(rs * GRID_W, GRID_W)
    band = WIN_R * GRID_W
    for h in range(HEADS):
        sl = slice(h * LANE, (h + 1) * LANE)
        qh = q[:, sl]
        s1 = _nt_dot(qh, k[pl.ds(start, band), sl]) + bias[0, h]
        s2 = _nt_dot(qh, kc[0, :, sl])
        o[:, sl] = _softmax_av([s1, s2], [v[pl.ds(start, band), sl], vc[0, :, sl]]).astype(BF16)


def _nat_attn(qn, kn, vn, kc, vc, bias, db):
    n = qn.shape[0]
    ds = n // db
    rows = ds // GRID_W
    band = WIN_R * GRID_W

    def variant(r):
        return jnp.where(r < WIN_R // 2, r, jnp.where(r > rows - WIN_R // 2, r - (rows - WIN_R), WIN_R // 2))

    qspec = pl.BlockSpec((GRID_W, HW), lambda b, r: (b * rows + r, 0))
    kspec = pl.BlockSpec((ds, HW), lambda b, r: (b, 0))
    cspec = pl.BlockSpec((1, kc.shape[1], HW), lambda b, r: (b, 0, 0))
    bspec = pl.BlockSpec((1, HEADS, GRID_W, band), lambda b, r: (variant(r), 0, 0, 0))
    return pl.pallas_call(
        functools.partial(_nat_kernel, rows=rows),
        grid=(db, rows),
        in_specs=[qspec, kspec, kspec, cspec, cspec, bspec],
        out_specs=qspec,
        out_shape=jax.ShapeDtypeStruct((n, HW), BF16),
        compiler_params=_params(("arbitrary", "arbitrary")),
        name="nat_attn",
    )(qn, kn, vn, kc, vc, bias)


def _split3(x):
    hi = x.astype(BF16)
    r = x - hi.astype(F32)
    mid = r.astype(BF16)
    lo = (r - mid.astype(F32)).astype(BF16)
    return hi, mid, lo


def _pool(p_prev, p_cur, p_next, posb, seq_len):
    rows = p_cur.shape[0]
    halo = p_prev.shape[0]
    ext = rows + 2 * halo
    pext = jnp.concatenate([p_prev, p_cur, p_next], axis=0)
    parts = _split3(pext)
    t = posb + lax.broadcasted_iota(jnp.int32, (rows, ext), 0)
    s = posb - halo + lax.broadcasted_iota(jnp.int32, (rows, ext), 1)
    tcol = posb + lax.broadcasted_iota(jnp.int32, (rows, 1), 0)
    grp = lax.broadcasted_iota(jnp.int32, (rows, 256), 1) // POOL_G
    d = jnp.zeros((rows, 256), F32)
    for gi, w in enumerate(POOL_WINDOWS):
        lo = jnp.maximum(t - w // 2, 0)
        hi = jnp.minimum(t + (w - w // 2), seq_len)
        sel = jnp.where(s >= lo, jnp.where(s < hi, 1.0, 0.0), 0.0).astype(BF16)
        tot = sum(jnp.dot(sel, part, preferred_element_type=F32) for part in parts)
        cnt = (jnp.minimum(tcol + (w - w // 2), seq_len) - jnp.maximum(tcol - w // 2, 0)).astype(F32)
        d = jnp.where(grp == gi, tot / cnt - p_cur, d)
    return d


def _first_max(x, pos, sentinel):
    m = jnp.max(x, axis=0, keepdims=True)
    idx = jnp.min(jnp.where(x == m, pos, sentinel), axis=0, keepdims=True)
    return m, idx


def _topk_stage1(qh, sk_ref):
    c = qh.shape[0]
    key_pos = lax.broadcasted_iota(jnp.int32, (PEER_NKEYS, c), 0).astype(F32)
    row16 = lax.broadcasted_iota(jnp.int32, (PEER_TOPK, c), 0)
    neg = jnp.float32(-jnp.inf)
    s0 = _nt_dot(sk_ref[0], qh)
    s1 = _nt_dot(sk_ref[1], qh)

    def stage1(a, carry):
        out = []
        for s, sv, si in (carry[0:3], carry[3:6]):
            m, idx = _first_max(s, key_pos, float(PEER_NKEYS))
            out += [jnp.where(key_pos == idx, neg, s),
                    jnp.where(row16 == a, m, sv), jnp.where(row16 == a, idx, si)]
        return tuple(out)

    zf = jnp.zeros((PEER_TOPK, c), F32)
    _, sv0, si0, _, sv1, si1 = lax.fori_loop(0, PEER_TOPK, stage1, (s0, zf, zf, s1, zf, zf))
    return sv0, si0, sv1, si1


def _topk_pieces(sv0, sv1):
    c = sv0.shape[1]
    neg = jnp.float32(-jnp.inf)
    sub8 = lax.broadcasted_iota(jnp.int32, (8, c), 0)
    sub8f = sub8.astype(F32)
    cs, cf = [], []
    for a in range(8):
        nb = PEER_TOPK // (a + 1)
        for b0 in range(0, nb, 8):
            val = sv0[a:a + 1] + sv1[b0:b0 + 8]
            if nb - b0 < 8:
                val = jnp.where(sub8 < nb - b0, val, neg)
            cs.append(val)
            cf.append(sub8f + float(a * PEER_TOPK + b0))
    cs.append(sv0[8:16] + sv1[0:1])
    cf.append((sub8f + 8.0) * float(PEER_TOPK))
    return cs, cf


def _topk_stage2(chains, cf):
    npc = len(cf)
    c = cf[0].shape[1]
    row16 = lax.broadcasted_iota(jnp.int32, (PEER_TOPK, c), 0)
    neg = jnp.float32(-jnp.inf)
    nflat = float(PEER_TOPK * PEER_TOPK)
    zf = jnp.zeros((PEER_TOPK, c), F32)

    def step(k, carry):
        out = []
        for ch in range(len(chains)):
            vals = carry[ch * (npc + 2):ch * (npc + 2) + npc]
            tv, tp = carry[ch * (npc + 2) + npc], carry[ch * (npc + 2) + npc + 1]
            m = vals[0]
            for v in vals[1:]:
                m = jnp.maximum(m, v)
            m = jnp.max(m, axis=0, keepdims=True)
            pos = None
            for v, f in zip(vals, cf):
                cand = jnp.where(v == m, f, nflat)
                pos = cand if pos is None else jnp.minimum(pos, cand)
            pos = jnp.min(pos, axis=0, keepdims=True)
            out += [jnp.where(f == pos, neg, v) for v, f in zip(vals, cf)]
            out += [jnp.where(row16 == k, m, tv), jnp.where(row16 == k, pos, tp)]
        return tuple(out)

    init = []
    for cs in chains:
        init += list(cs) + [zf, zf]
    res = lax.fori_loop(0, PEER_TOPK, step, tuple(init))
    return [(res[ch * (npc + 2) + npc], res[ch * (npc + 2) + npc + 1]) for ch in range(len(chains))]


def _topk_ids(tp, si0, si1):
    a = jnp.floor(tp * (1.0 / PEER_TOPK))
    b = tp - a * float(PEER_TOPK)
    ea = jnp.zeros_like(tp)
    eb = jnp.zeros_like(tp)
    for j in range(PEER_TOPK):
        ea = jnp.where(a == float(j), si0[j:j + 1], ea)
        eb = jnp.where(b == float(j), si1[j:j + 1], eb)
    return ea * float(PEER_NKEYS) + eb


def _out_kernel(on_ref, om_ref, pc_ref, pp_ref, pn_ref, x_ref, mod_ref,
                won_ref, wop_ref, wom_ref, pw_ref, ps_ref, n2_ref, wq_ref, sk_ref,
                x1_ref, h2_ref, ids_ref, gt_ref, gn_ref, h2c_ref, q_scr, idt_scr, *, bps, seq_len):
    d = x_ref.shape[1]
    rows = x_ref.shape[0]
    i = pl.program_id(0)
    mod = mod_ref[0]
    g1 = mod[:, 2 * d:3 * d]
    sh2 = mod[:, 3 * d:4 * d]
    sc2 = mod[:, 4 * d:5 * d]

    posb = (i % bps) * rows
    dpool = _pool(pp_ref[...], pc_ref[...], pn_ref[...], posb, seq_len)
    ypool = jnp.dot(dpool.astype(BF16), pw_ref[...], preferred_element_type=F32) * ps_ref[...]
    mix = (jnp.dot(on_ref[...], won_ref[...], preferred_element_type=F32)
           + jnp.dot(ypool.astype(BF16), wop_ref[...], preferred_element_type=F32)
           + jnp.dot(om_ref[...], wom_ref[...], preferred_element_type=F32))
    x1 = x_ref[...] + g1 * mix
    x1_ref[...] = x1
    h2 = _rms(x1, n2_ref[...]) * (1.0 + sc2) + sh2
    h2_ref[...] = h2
    for j in range(d // LANE):
        h2c_ref[j] = h2[:, j * LANE:(j + 1) * LANE]

    q = jnp.dot(h2.astype(BF16), wq_ref[...], preferred_element_type=F32)
    for hh in range(PEER_HEADS):
        q_scr[hh] = q[:, hh * LANE:(hh + 1) * LANE].astype(BF16)

    chunks = range(0, rows, LANE)

    def head(hh, _):
        sorted_keys = [_topk_stage1(q_scr[hh, c0:c0 + LANE, :], sk_ref) for c0 in chunks]
        pieces = [_topk_pieces(sv0, sv1) for sv0, _, sv1, _ in sorted_keys]
        picked = _topk_stage2([cs for cs, _ in pieces], pieces[0][1])
        r0 = pl.multiple_of(hh * PEER_TOPK, PEER_TOPK)
        for c0, (tv, tp), (_, si0, _, si1) in zip(chunks, picked, sorted_keys):
            ex = jnp.exp(tv - tv[0:1])
            gt_ref[pl.ds(r0, PEER_TOPK), c0:c0 + LANE] = ex / jnp.sum(ex, axis=0, keepdims=True)
            idt_scr[pl.ds(r0, PEER_TOPK), c0:c0 + LANE] = _topk_ids(tp, si0, si1)
        return 0

    lax.fori_loop(0, PEER_HEADS, head, 0)
    ids_ref[...] = idt_scr[...].T.astype(jnp.int32)
    gn_ref[...] = gt_ref[...].T


def _out_proj(x, on, om, p, mod_l, row_off, bpm, lw, seq_len):
    n, d = x.shape
    nb = n // TB
    bps = seq_len // TB
    halo = 8
    hb = TB // halo
    tok = lambda w: pl.BlockSpec((TB, w), lambda i: (i, 0))
    in_specs = [tok(HW), tok(HW), tok(256),
                pl.BlockSpec((halo, 256), lambda i: (jnp.maximum(i * hb - 1, 0), 0)),
                pl.BlockSpec((halo, 256), lambda i: (jnp.minimum((i + 1) * hb, n // halo - 1), 0)),
                tok(d),
                pl.BlockSpec((1, 1, mod_l.shape[-1]), lambda i: (row_off + i // bpm, 0, 0)),
                _const_spec((HW, d)), _const_spec((256, d)), _const_spec((HW, d)),
                _const_spec((256, 256)), _const_spec((1, 256)), _const_spec((1, d)),
                _const_spec((d, PEER_HEADS * LANE)), _const_spec((2, PEER_NKEYS, LANE))]
    nk = PEER_HEADS * PEER_TOPK
    return pl.pallas_call(
        functools.partial(_out_kernel, bps=bps, seq_len=seq_len),
        grid=(nb,),
        in_specs=in_specs,
        out_specs=[tok(d), tok(d), tok(nk), pl.BlockSpec((nk, TB), lambda i: (0, i)), tok(nk),
                   pl.BlockSpec((d // LANE, TB, LANE), lambda i: (0, i, 0))],
        out_shape=[jax.ShapeDtypeStruct((n, d), F32), jax.ShapeDtypeStruct((n, d), F32),
                   jax.ShapeDtypeStruct((n, nk), jnp.int32), jax.ShapeDtypeStruct((nk, n), F32),
                   jax.ShapeDtypeStruct((n, nk), F32),
                   jax.ShapeDtypeStruct((d // LANE, n, LANE), F32)],
        scratch_shapes=[pltpu.VMEM((PEER_HEADS, TB, LANE), BF16), pltpu.VMEM((nk, TB), F32)],
        compiler_params=_params(("arbitrary",)),
        name="out_proj",
    )(on, om, p, p, p, x, mod_l, lw["w_o_na"], lw["w_o_pool"], lw["w_o_mla"],
      lw["pool_w"], lw["pool_scale"], lw["norm2"], lw["peer_wq"], lw["peer_sk"])


def _gelu_tanh(x):
    return x * (0.5 * (1.0 + jnp.tanh(0.7978845608028654 * (x + 0.044715 * (x * x * x)))))


def _peer_token_mix(chunk, hrow, gcol, ch):
    acc = None
    for s in range(ch):
        us = lax.bitcast_convert_type(chunk(s) & jnp.int32(-65536), F32)
        term = us * hrow[:, s * LANE:(s + 1) * LANE]
        acc = term if acc is None else acc + term
    wgt = gcol * _gelu_tanh(jnp.sum(acc, axis=-1, keepdims=True))
    parts = []
    for s in range(ch):
        vs = lax.bitcast_convert_type(chunk(s) << 16, F32)
        parts.append(jnp.sum(vs * wgt, axis=0, keepdims=True))
    return jnp.concatenate(parts, axis=-1)


def _sc_peer(table3, ids, gates, h2c, n):
    ch, _, lane = h2c.shape
    nk = ids.shape[1]
    info = plsc.get_sparse_core_info()
    nc, nw, nl = info.num_cores, info.num_cores * info.num_subcores, info.num_lanes
    tpw = n // nw
    win = 32
    nq = nk // win
    cpr = lane // nl
    nchunk = ch * cpr
    hc = nchunk // 2
    assert n % nw == 0 and nk % win == 0 and win % nl == 0
    mesh = plsc.VectorSubcoreMesh(core_axis_name="core", subcore_axis_name="subcore")
    hi_mask = jnp.int32(-65536)

    @functools.partial(
        pl.kernel, mesh=mesh,
        out_type=jax.ShapeDtypeStruct((ch, n, lane), F32),
        compiler_params=pltpu.CompilerParams(needs_layout_passes=False),
        scratch_types=[pltpu.VMEM((nk,), jnp.int32), pltpu.VMEM((nk,), F32),
                       pltpu.VMEM((ch, lane), F32), pltpu.VMEM((ch, lane), F32),
                       pltpu.VMEM((win, ch, lane), jnp.int32), pltpu.VMEM((win, ch, lane), jnp.int32),
                       pltpu.VMEM((win * nl,), F32), pltpu.VMEM((win,), F32),
                       pltpu.SemaphoreType.DMA, pltpu.SemaphoreType.DMA, pltpu.SemaphoreType.DMA])
    def peer(tab_hbm, ids_hbm, g_hbm, h2_hbm, y_hbm,
             idx_v, g_v, x_v, y_v, rows_a, rows_b, part_v, w_v, sem_a, sem_b, sem_x):
        wid = lax.axis_index("subcore") * nc + lax.axis_index("core")
        bufs = ((rows_a, sem_a), (rows_b, sem_b))
        lanes = lax.iota(jnp.int32, nl)
        zero = jnp.zeros((nl,), F32)

        def chunk_copies(tok, to_hbm):
            if to_hbm:
                return [pltpu.make_async_copy(y_v.at[j], y_hbm.at[j, tok], sem_x) for j in range(ch)]
            return [pltpu.make_async_copy(h2_hbm.at[j, tok], x_v.at[j], sem_x) for j in range(ch)]

        def fetch(q):
            rows, sem = bufs[q % 2]
            return pltpu.make_async_copy(tab_hbm.at[idx_v.at[pl.ds(q * win, win)]], rows, sem)

        def word(rows, r, cc):
            return rows[r, cc // cpr, pl.ds((cc % cpr) * nl, nl)]

        @pl.loop(0, tpw)
        def _(ti):
            tok = wid * tpw + ti
            loads = chunk_copies(tok, False)
            for cp in loads:
                cp.start()
            pltpu.sync_copy(ids_hbm.at[tok], idx_v)
            pltpu.sync_copy(g_hbm.at[tok], g_v)
            for cp in loads:
                cp.wait()
            for cc in range(nchunk):
                y_v[cc // cpr, pl.ds((cc % cpr) * nl, nl)] = zero
            fetch(0).start()
            for q in range(nq):
                rows = bufs[q % 2][0]
                fetch(q).wait()
                if q + 1 < nq:
                    fetch(q + 1).start()

                for half in range(2):
                    xs = [x_v[(half * hc + c) // cpr, pl.ds(((half * hc + c) % cpr) * nl, nl)]
                          for c in range(hc)]

                    @pl.loop(0, win, step=2)
                    def _(r0):
                        pos = [pl.multiple_of((r0 + k) * nl, nl) for k in range(2)]
                        accs = [[part_v[pl.ds(pos[k], nl)] if half else None, None, None, None]
                                for k in range(2)]
                        for c in range(hc):
                            for k in range(2):
                                u = lax.bitcast_convert_type(
                                    word(rows, r0 + k, half * hc + c) & hi_mask, F32)
                                t = u * xs[c]
                                accs[k][c % 4] = t if accs[k][c % 4] is None else accs[k][c % 4] + t
                        for k in range(2):
                            part_v[pl.ds(pos[k], nl)] = (accs[k][0] + accs[k][1]) + (accs[k][2] + accs[k][3])

                for grp in range(win // nl):
                    s = zero
                    for rr in range(nl):
                        tot = jnp.sum(part_v[pl.ds((grp * nl + rr) * nl, nl)])
                        s = jnp.where(lanes == rr, tot, s)
                    z = 0.7978845608028654 * (s + 0.044715 * (s * s * s))
                    tanh = 1.0 - 2.0 / (jnp.exp(2.0 * z) + 1.0)
                    gate = g_v[pl.ds(q * win + grp * nl, nl)]
                    w_v[pl.ds(grp * nl, nl)] = gate * (s * (0.5 * (1.0 + tanh)))

                for half in range(2):
                    def body(r, yacc):
                        wr = plsc.load_gather(w_v, [jnp.full((nl,), r, jnp.int32)])
                        out = []
                        for c in range(hc):
                            v = lax.bitcast_convert_type(word(rows, r, half * hc + c) << 16, F32)
                            out.append(yacc[c] + wr * v)
                        return tuple(out)

                    yacc = lax.fori_loop(0, win, body, tuple(zero for _ in range(hc)))
                    for c in range(hc):
                        cc = half * hc + c
                        sl = (cc // cpr, pl.ds((cc % cpr) * nl, nl))
                        y_v[sl] = y_v[sl] + yacc[c]
            stores = chunk_copies(tok, True)
            for cp in stores:
                cp.start()
            for cp in stores:
                cp.wait()

    return peer(table3, ids, gates, h2c)


def _residual_kernel(x1_ref, y_ref, mod_ref, x2_hbm, after_hbm, o_ref):
    del x2_hbm
    del after_hbm
    d = x1_ref.shape[1]
    g2 = mod_ref[0][:, 5 * d:6 * d]
    for j in range(d // LANE):
        sl = slice(j * LANE, (j + 1) * LANE)
        o_ref[:, sl] = x1_ref[:, sl] + g2[:, sl] * y_ref[j]


def _residual(x1, y, mod_l, row, x2, after):
    n, d = x1.shape
    tok = pl.BlockSpec((PEER_TB, d), lambda i: (i, 0))
    any_spec = pl.BlockSpec(memory_space=pl.ANY)
    return pl.pallas_call(
        _residual_kernel,
        grid=(y.shape[1] // PEER_TB,),
        in_specs=[tok, pl.BlockSpec((d // LANE, PEER_TB, LANE), lambda i: (0, i, 0)),
                  pl.BlockSpec((1, 1, mod_l.shape[-1]), lambda i: (row, 0, 0)),
                  any_spec, any_spec],
        out_specs=tok,
        out_shape=jax.ShapeDtypeStruct((n, d), F32),
        input_output_aliases={3: 0},
        compiler_params=_params(("arbitrary",)),
        name="residual",
    )(x1, y, mod_l, x2, after)


def _peer_kernel(ids_hbm, gt_ref, h2_ref, x1_ref, mod_ref, tab_hbm, o_ref,
                 ids_s, buf, sem_i, sem_r, *, first_block):
    d = x1_ref.shape[1]
    ch = d // LANE
    pitch = ch + 1
    nsub = x1_ref.shape[0] // PEER_SUB
    nk = gt_ref.shape[0]
    nids = PEER_SUB * nk
    i = pl.program_id(0) + first_block
    g2 = mod_ref[0][:, 5 * d:6 * d]
    tok_lane = lax.broadcasted_iota(jnp.int32, gt_ref.shape, 1)

    def ids_copy(j, slot):
        start = pl.multiple_of((i * nsub + j) * nids, nids)
        return pltpu.make_async_copy(ids_hbm.at[pl.ds(start, nids)],
                                     ids_s.at[pl.ds(slot * nids, nids)], sem_i.at[slot])

    def row_copy(slot, e, f):
        src = tab_hbm.at[pl.ds(pl.multiple_of(e * ch, ch), ch), :]
        dst = buf.at[slot, pl.ds(f * pitch, ch), :]
        return pltpu.make_async_copy(src, dst, sem_r.at[slot])

    def issue_rows(slot):
        for t in range(PEER_SUB):
            def body(kk, _):
                for r in range(8):
                    f = t * nk + kk * 8 + r
                    row_copy(slot, ids_s[slot * nids + f], f).start(priority=r % 2)
                return 0

            lax.fori_loop(0, nk // 8, body, 0)

    def wait_rows(slot):
        done = buf.at[slot, pl.ds(0, nids * ch), :]
        pltpu.make_async_copy(done, done, sem_r.at[slot]).wait()

    def compute(slot, j):
        base = pl.multiple_of(j * PEER_SUB, PEER_SUB)
        h8 = h2_ref[pl.ds(base, PEER_SUB), :]
        ys = []
        for t in range(PEER_SUB):
            chunk = lambda s: buf[slot, pl.ds(t * nk * pitch + s, nk, stride=pitch), :]
            gcol = jnp.sum(jnp.where(tok_lane == base + t, gt_ref[...], 0.0), axis=-1, keepdims=True)
            ys.append(_peer_token_mix(chunk, h8[t:t + 1, :], gcol, ch))
        y8 = jnp.concatenate(ys, axis=0)
        o_ref[pl.ds(base, PEER_SUB), :] = x1_ref[pl.ds(base, PEER_SUB), :] + g2 * y8

    first = ids_copy(0, 0)
    first.start()
    first.wait()
    issue_rows(0)
    ids_copy(1, 1).start()

    def pair(jj, _):
        j0 = 2 * jj
        ids_copy(j0 + 1, 1).wait()
        issue_rows(1)

        @pl.when(j0 + 2 < nsub)
        def _():
            ids_copy(j0 + 2, 0).start()

        wait_rows(0)
        compute(0, j0)

        @pl.when(j0 + 2 < nsub)
        def _():
            ids_copy(j0 + 2, 0).wait()
            issue_rows(0)

        @pl.when(j0 + 3 < nsub)
        def _():
            ids_copy(j0 + 3, 1).start()

        wait_rows(1)
        compute(1, j0 + 1)
        return 0

    lax.fori_loop(0, nsub // 2, pair, 0)


def _pack_tables(peer_u, peer_v):
    e, d = peer_u.shape
    ub = lax.bitcast_convert_type(peer_u.astype(BF16), jnp.uint16).astype(jnp.uint32)
    vb = lax.bitcast_convert_type(peer_v.astype(BF16), jnp.uint16).astype(jnp.uint32)
    words = lax.bitcast_convert_type((ub << 16) | vb, jnp.int32)
    return words.reshape(e, d // LANE, LANE)


def _peer(x1, h2, ids, gt, mod_l, row, table, tok0):
    n, d = x1.shape
    nk = gt.shape[0]
    b0 = tok0 // PEER_TB
    nb = n // PEER_TB - b0
    tok = pl.BlockSpec((PEER_TB, d), lambda i: (i + b0, 0))
    any_spec = pl.BlockSpec(memory_space=pl.ANY)
    return pl.pallas_call(
        functools.partial(_peer_kernel, first_block=b0),
        grid=(nb,),
        in_specs=[any_spec,
                  pl.BlockSpec((nk, PEER_TB), lambda i: (0, i + b0)),
                  tok, tok,
                  pl.BlockSpec((1, 1, mod_l.shape[-1]), lambda i: (row, 0, 0)),
                  any_spec],
        out_specs=tok,
        out_shape=jax.ShapeDtypeStruct((n, d), F32),
        scratch_shapes=[pltpu.SMEM((2 * PEER_SUB * nk,), jnp.int32),
                        pltpu.VMEM((2, PEER_SUB * nk * (d // LANE + 1), LANE), jnp.int32),
                        pltpu.SemaphoreType.DMA((2,)),
                        pltpu.SemaphoreType.DMA((2,))],
        compiler_params=_params(("arbitrary",)),
        name="peer",
    )(ids.reshape(n * nk), gt, h2, x1, mod_l, table.reshape(-1, LANE))


def _pad_heads(w, width):
    pad = [(0, 0)] * (w.ndim - 1) + [(0, LANE - width)]
    w = jnp.pad(w, pad)
    return w.reshape(w.shape[:-2] + (HW,))


def _head_gain(g, width):
    depth = g.shape[0]
    g = jnp.pad(g, ((0, 0), (0, LANE - width)))
    return jnp.tile(g, (1, HEADS)).reshape(depth, 1, HW)


def _rope_tables(seq):
    t = np.arange(seq)
    half = MLA_ROPE // 2
    inv = ROPE_THETA ** (-np.arange(0, half, 2, dtype=np.float32) / half)
    cos = np.ones((seq, LANE), np.float32)
    sin = np.zeros((seq, LANE), np.float32)
    for off, pos in ((MLA_NOPE, t // GRID_W), (MLA_NOPE + half, t % GRID_W)):
        ang = pos.astype(np.float32)[:, None] * inv[None, :]
        q = half // 2
        cos[:, off:off + q] = np.cos(ang)
        cos[:, off + q:off + half] = np.cos(ang)
        sin[:, off:off + q] = -np.sin(ang)
        sin[:, off + q:off + half] = np.sin(ang)
    return jnp.asarray(cos), jnp.asarray(sin)


def _nat_bias(rel_bias):
    v = np.arange(WIN_R)[:, None]
    j = np.arange(WIN_R)[None, :]
    dr = j - v + WIN_R - 1
    cq = np.arange(GRID_W)[:, None]
    kc = np.arange(GRID_W)[None, :]
    cstart = np.clip(cq - WIN_C // 2, 0, GRID_W - WIN_C)
    ok = (kc >= cstart) & (kc < cstart + WIN_C)
    dc = np.clip(kc - cq + WIN_C - 1, 0, 2 * WIN_C - 2)
    b = rel_bias[:, :, dr]
    b = b[..., dc]
    b = jnp.where(jnp.asarray(ok)[None, None, None, None], b, NEG_INF)
    b = jnp.transpose(b, (0, 2, 1, 4, 3, 5))
    return b.reshape(b.shape[0], WIN_R, HEADS, GRID_W, WIN_R * GRID_W)


def _layer_weights(w_in, na_q_norm, na_k_norm, mla_cq_norm, mla_ckv_norm, mla_w_uq, mla_w_ukv,
                   mla_q_norm, mla_k_norm, w_out, pool_w, pool_scale, norm1, norm2,
                   peer_wq, peer_subkeys):
    depth, d, _ = w_in.shape
    na_w = HEADS * NA_DH
    segs = np.cumsum([0, na_w, na_w, na_w, 256, 256, 128, MLA_ROPE])
    part = lambda i: w_in[:, :, segs[i]:segs[i + 1]]
    heads = lambda w: _pad_heads(w.reshape(depth, d, HEADS, NA_DH), NA_DH)
    w_in_p = jnp.concatenate(
        [heads(part(0)), heads(part(1)), heads(part(2)), part(3), part(4), part(5),
         jnp.pad(part(6), ((0, 0), (0, 0), (0, LANE - MLA_ROPE)))], axis=-1).astype(BF16)

    w_uq = _pad_heads(mla_w_uq, MLA_QK).astype(BF16)
    k_nope = _pad_heads(mla_w_ukv[..., :MLA_NOPE], MLA_NOPE)
    eye = np.zeros((MLA_ROPE, HEADS, LANE), np.float32)
    for h in range(HEADS):
        eye[np.arange(MLA_ROPE), h, MLA_NOPE + np.arange(MLA_ROPE)] = 1.0
    eye = jnp.broadcast_to(jnp.asarray(eye.reshape(MLA_ROPE, HW)), (depth, MLA_ROPE, HW))
    zer = jnp.zeros((depth, 256 - 128 - MLA_ROPE, HW), F32)
    w_k = jnp.concatenate([k_nope, eye, zer], axis=1).astype(BF16)
    w_v = jnp.concatenate([_pad_heads(mla_w_ukv[..., MLA_NOPE:], MLA_V),
                           jnp.zeros((depth, 128, HW), F32)], axis=1).astype(BF16)

    mix_w = HEADS * NA_DH
    w_o_na = jnp.pad(w_out[:, :mix_w].reshape(depth, HEADS, NA_DH, d),
                     ((0, 0), (0, 0), (0, LANE - NA_DH), (0, 0))).reshape(depth, HW, d).astype(BF16)
    w_o_pool = w_out[:, mix_w:mix_w + 256].astype(BF16)
    w_o_mla = jnp.pad(w_out[:, mix_w + 256:].reshape(depth, HEADS, MLA_V, d),
                      ((0, 0), (0, 0), (0, LANE - MLA_V), (0, 0))).reshape(depth, HW, d).astype(BF16)
    ng = len(POOL_WINDOWS)
    pw = jnp.zeros((depth, ng * POOL_G, ng * POOL_G), F32)
    for g in range(ng):
        pw = pw.at[:, g * POOL_G:(g + 1) * POOL_G, g * POOL_G:(g + 1) * POOL_G].set(pool_w[:, g])

    half = peer_subkeys.shape[-1]
    sk = jnp.stack([jnp.pad(peer_subkeys[:, 0], ((0, 0), (0, 0), (0, LANE - half))),
                    jnp.pad(peer_subkeys[:, 1], ((0, 0), (0, 0), (LANE - half, 0)))], axis=1).astype(BF16)

    return dict(
        w_in=w_in_p, w_uq=w_uq, w_k=w_k, w_v=w_v,
        g_q=_head_gain(na_q_norm, NA_DH), g_k=_head_gain(na_k_norm, NA_DH),
        g_cq=mla_cq_norm[:, None, :], g_ckv=mla_ckv_norm[:, None, :],
        g_qm=_head_gain(mla_q_norm, MLA_QK), g_km=_head_gain(mla_k_norm, MLA_QK),
        w_o_na=w_o_na, w_o_pool=w_o_pool, w_o_mla=w_o_mla,
        pool_w=pw.astype(BF16), pool_scale=pool_scale[:, None, :],
        norm1=norm1[:, None, :], norm2=norm2[:, None, :],
        peer_wq=peer_wq.astype(BF16), peer_sk=sk)


def kernel(x_prompt, x_sample, c, cache_nat_k, cache_nat_v, cache_mla_ckv, cache_mla_krope, c_ctx, w_mod, b_mod, norm1, norm2, w_in, na_q_norm, na_k_norm, na_rel_bias, pool_w, pool_scale, mla_cq_norm, mla_ckv_norm, mla_w_uq, mla_w_ukv, mla_q_norm, mla_k_norm, w_out, peer_wq, peer_subkeys, peer_u, peer_v):
    batch, seq, d = x_prompt.shape
    db, ds, _ = x_sample.shape
    depth = w_mod.shape[0]
    past = cache_nat_k.shape[2]
    assert seq == TB and ds % TB == 0 and ds % (GRID_W * WIN_R) == 0 and db + 1 <= 8

    cond8 = jnp.concatenate([c_ctx[None, :], c, jnp.zeros((8 - 1 - db, d), F32)], axis=0)
    mod = _modulation(cond8, w_mod, b_mod).reshape(depth, 8, 1, 6 * d)

    lw_all = _layer_weights(w_in, na_q_norm, na_k_norm, mla_cq_norm, mla_ckv_norm, mla_w_uq,
                            mla_w_ukv, mla_q_norm, mla_k_norm, w_out, pool_w, pool_scale,
                            norm1, norm2, peer_wq, peer_subkeys)
    bias_all = _nat_bias(na_rel_bias)
    tables = [_pack_tables(peer_u[l], peer_v[l]) for l in range(depth)]
    cos_lat, sin_lat = _rope_tables(ds)
    cos_ctx = jnp.ones((TB, LANE), F32)
    sin_ctx = jnp.zeros((TB, LANE), F32)

    ck = jnp.concatenate([cache_mla_ckv, cache_mla_krope,
                          jnp.zeros(cache_mla_ckv.shape[:-1] + (256 - 128 - MLA_ROPE,), F32)],
                         axis=-1).astype(BF16)
    kc_mla, vc_mla = _cache_kv(ck, lw_all["w_k"], lw_all["w_v"], lw_all["g_km"])
    kc_na = _pad_heads(cache_nat_k, NA_DH).astype(BF16)
    vc_na = _pad_heads(cache_nat_v, NA_DH).astype(BF16)

    xs = [x_prompt.reshape(batch * seq, d)] + [x_sample[b] for b in range(db)]
    one_row = max(batch * seq, ds) + 1
    lat_bpm = ds // TB
    ks, vs, ckvs, krs = [], [], [], []
    pending = None
    after = xs[0]

    def join(item, follow):
        si, x1, y_sc, x2, mod_l = item
        xs[si] = _residual(x1, y_sc, mod_l, si, x2, follow)
        return xs[si]

    for l in range(depth):
        lw = {k: v[l] for k, v in lw_all.items()}
        mod_l = mod[l]
        for si in range(db + 1):
            x = xs[si]
            if si == 0:
                (qn, kn, vn, knf, vnf, p, qm, km, vm, ckv, kr) = _in_proj(
                    x, mod_l, 0, one_row, lw, cos_ctx, sin_ctx, 1, after)
                on, om = _ctx_attn(qn, kn, vn, qm, km, vm, seq)
                x1, h2, ids, gt, gn, h2c = _out_proj(x, on, om, p, mod_l, 0, one_row, lw, seq)
                ks.append(knf.reshape(batch, seq, HEADS, LANE)[..., :NA_DH])
                vs.append(vnf.reshape(batch, seq, HEADS, LANE)[..., :NA_DH])
                ckvs.append(ckv.reshape(batch, seq, 128))
                krs.append(kr.reshape(batch, seq, LANE)[..., :MLA_ROPE])
            else:
                b = si - 1
                (qn, kn, vn, _, _, p, qm, km, vm, _, _) = _in_proj(
                    x, mod_l, si, one_row, lw, cos_lat, sin_lat, lat_bpm, after)
                on = _nat_attn(qn, kn, vn, kc_na[b:b + 1, l], vc_na[b:b + 1, l], bias_all[l], 1)
                om = _lat_mla(qm, km, vm, kc_mla[b:b + 1, l], vc_mla[b:b + 1, l], 1)
                x1, h2, ids, gt, gn, h2c = _out_proj(x, on, om, p, mod_l, si, one_row, lw, ds)
            n_sc = x.shape[0] * SC_SHARE[0] // SC_SHARE[1] // PEER_TB * PEER_TB
            if (l + si) % 2:
                n_sc -= PEER_TB
            if l == depth - 1 and si == db:
                n_sc = x.shape[0] // 2
            y_sc = _sc_peer(tables[l], ids, gn, h2c, n_sc)
            x2 = _peer(x1, h2, ids, gt, mod_l, si, tables[l], n_sc)
            after = x2 if pending is None else join(pending, x2)
            pending = (si, x1, y_sc, x2, mod_l)
    join(pending, pending[1])

    return (xs[0].reshape(batch, seq, d), jnp.stack(xs[1:], axis=0),
            jnp.stack(ks, axis=1), jnp.stack(vs, axis=1),
            jnp.stack(ckvs, axis=1), jnp.stack(krs, axis=1))
```

```python
import functools

import numpy as np
import jax
import jax.numpy as jnp
from jax import lax
from jax.experimental import pallas as pl
from jax.experimental.pallas import tpu as pltpu
from jax.experimental.pallas import tpu_sc as plsc

F32 = jnp.float32
BF16 = jnp.bfloat16

EPS = 1e-6
ROPE_THETA = 10000.0
NEG_INF = -1e30
GRID_W = 64
HEADS = 6
NA_DH = 64
WIN_R = 8
WIN_C = 16
POOL_WINDOWS = (2, 4, 8, 16)
POOL_G = 64
MLA_NOPE = 64
MLA_ROPE = 32
MLA_QK = MLA_NOPE + MLA_ROPE
MLA_V = 64
PEER_HEADS = 8
PEER_NKEYS = 128
PEER_TOPK = 16
LANE = 128
HW = HEADS * LANE
TB = 256
TQ = 256
PEER_TB = 128
PEER_SUB = 8
VMEM_LIMIT = 56 * 1024 * 1024
SC_SHARE = (3, 4)

_CQ, _CK, _CV = 0, HW, 2 * HW
_CP = 3 * HW
_CCQ = _CP + 256
_CCKV = _CCQ + 256
_CKR = _CCKV + 128
IN_W = _CKR + 128


def _params(sem, vmem=VMEM_LIMIT):
    return pltpu.CompilerParams(dimension_semantics=sem, vmem_limit_bytes=vmem)


def _const_spec(shape):
    n = len(shape)
    return pl.BlockSpec(shape, lambda *_: (0,) * n)


def _nt_dot(a, b):
    return lax.dot_general(a, b, (((1,), (1,)), ((), ())), preferred_element_type=F32)


def _mod_kernel(c_ref, w_ref, b_ref, o_ref):
    c = c_ref[...]
    s = c / (1.0 + jnp.exp(-c))
    o_ref[0] = jnp.dot(s, w_ref[0], preferred_element_type=F32,
                       precision=lax.Precision.HIGHEST) + b_ref[0]


def _modulation(cond8, w_mod, b_mod):
    depth, d, n6 = w_mod.shape
    tn = n6 // 4
    return pl.pallas_call(
        _mod_kernel,
        grid=(depth, n6 // tn),
        in_specs=[_const_spec((8, d)),
                  pl.BlockSpec((1, d, tn), lambda l, j: (l, 0, j)),
                  pl.BlockSpec((1, 1, tn), lambda l, j: (l, 0, j))],
        out_specs=pl.BlockSpec((1, 8, tn), lambda l, j: (l, 0, j)),
        out_shape=jax.ShapeDtypeStruct((depth, 8, n6), F32),
        compiler_params=_params(("arbitrary", "arbitrary")),
        name="modulation",
    )(cond8, w_mod, b_mod.reshape(depth, 1, n6))


def _rms(z, gain):
    return z * lax.rsqrt(jnp.mean(z * z, axis=-1, keepdims=True) + EPS) * gain


def _head_rms(zh, gain_h, n_real):
    ms = jnp.sum(zh * zh, axis=-1, keepdims=True) * (1.0 / n_real)
    return zh * lax.rsqrt(ms + EPS) * gain_h


def _rope(zh, cos, sin, is_x1):
    rot = jnp.where(is_x1, pltpu.roll(zh, LANE - 8, 1), pltpu.roll(zh, 8, 1))
    return zh * cos + rot * sin


def _is_x1(rows):
    lane = lax.broadcasted_iota(jnp.int32, (rows, LANE), 1)
    first = jnp.where(lane >= MLA_NOPE, jnp.where(lane < MLA_NOPE + 8, 1, 0), 0)
    second = jnp.where(lane >= MLA_NOPE + 16, jnp.where(lane < MLA_NOPE + 24, 1, 0), 0)
    return (first + second) > 0


def _mla_kv(ck, wk_ref, wv_ref, gk_ref, cos, sin, km_ref, vm_ref):
    rows = ck.shape[0]
    kk = jnp.dot(ck, wk_ref[...], preferred_element_type=F32)
    is_x1 = _is_x1(rows)
    for h in range(HEADS):
        sl = slice(h * LANE, (h + 1) * LANE)
        kh = _head_rms(kk[:, sl], gk_ref[:, sl], MLA_QK)
        km_ref[:, sl] = _rope(kh, cos, sin, is_x1).astype(BF16)
    vm_ref[...] = jnp.dot(ck, wv_ref[...], preferred_element_type=F32).astype(BF16)


def _in_kernel(x_ref, mod_ref, n1_ref, w_ref, wuq_ref, wk_ref, wv_ref,
               gq_ref, gk_ref, gcq_ref, gckv_ref, gqm_ref, gkm_ref, cos_ref, sin_ref, after_hbm,
               qn_ref, kn_ref, vn_ref, knf_ref, vnf_ref, p_ref,
               qm_ref, km_ref, vm_ref, ckv_ref, kr_ref):
    del after_hbm
    d = x_ref.shape[1]
    rows = x_ref.shape[0]
    mod = mod_ref[0]
    sh1 = mod[:, 0:d]
    sc1 = mod[:, d:2 * d]
    h = _rms(x_ref[...], n1_ref[...]) * (1.0 + sc1) + sh1
    hb = h.astype(BF16)

    def proj(lo, hi):
        return jnp.dot(hb, w_ref[:, lo:hi], preferred_element_type=F32)

    cos = cos_ref[...]
    sin = sin_ref[...]
    is_x1 = _is_x1(rows)

    zq = proj(_CQ, _CQ + HW)
    zk = proj(_CK, _CK + HW)
    for hh in range(HEADS):
        sl = slice(hh * LANE, (hh + 1) * LANE)
        qn_ref[:, sl] = (_head_rms(zq[:, sl], gq_ref[:, sl], NA_DH) * (NA_DH ** -0.5)).astype(BF16)
        kh = _head_rms(zk[:, sl], gk_ref[:, sl], NA_DH)
        knf_ref[:, sl] = kh
        kn_ref[:, sl] = kh.astype(BF16)
    zv = proj(_CV, _CV + HW)
    vnf_ref[...] = zv
    vn_ref[...] = zv.astype(BF16)
    p_ref[...] = proj(_CP, _CP + 256)

    cq = _rms(proj(_CCQ, _CCQ + 256), gcq_ref[...])
    zqm = jnp.dot(cq.astype(BF16), wuq_ref[...], preferred_element_type=F32)
    for hh in range(HEADS):
        sl = slice(hh * LANE, (hh + 1) * LANE)
        qh = _head_rms(zqm[:, sl], gqm_ref[:, sl], MLA_QK)
        qm_ref[:, sl] = (_rope(qh, cos, sin, is_x1) * (MLA_QK ** -0.5)).astype(BF16)

    ckv = _rms(proj(_CCKV, _CCKV + 128), gckv_ref[...])
    kr = proj(_CKR, _CKR + 128)
    ckv_ref[...] = ckv
    kr_ref[...] = kr
    ck = jnp.concatenate([ckv, kr], axis=-1).astype(BF16)
    _mla_kv(ck, wk_ref, wv_ref, gkm_ref, cos, sin, km_ref, vm_ref)


def _in_proj(x, mod_l, row_off, bpm, lw, cos_t, sin_t, rope_blocks, after):
    n, d = x.shape
    nb = n // TB
    tok = lambda w: pl.BlockSpec((TB, w), lambda i: (i, 0))
    rope_spec = pl.BlockSpec((TB, LANE), lambda i: (i % rope_blocks, 0))
    in_specs = [tok(d),
                pl.BlockSpec((1, 1, mod_l.shape[-1]), lambda i: (row_off + i // bpm, 0, 0)),
                _const_spec((1, d)), _const_spec((d, IN_W)), _const_spec((256, HW)),
                _const_spec((256, HW)), _const_spec((256, HW)),
                _const_spec((1, HW)), _const_spec((1, HW)), _const_spec((1, 256)),
                _const_spec((1, 128)), _const_spec((1, HW)), _const_spec((1, HW)),
                rope_spec, rope_spec, pl.BlockSpec(memory_space=pl.ANY)]
    widths = [(HW, BF16), (HW, BF16), (HW, BF16), (HW, F32), (HW, F32), (256, F32),
              (HW, BF16), (HW, BF16), (HW, BF16), (128, F32), (128, F32)]
    return pl.pallas_call(
        _in_kernel,
        grid=(nb,),
        in_specs=in_specs,
        out_specs=[tok(w) for w, _ in widths],
        out_shape=[jax.ShapeDtypeStruct((n, w), dt) for w, dt in widths],
        compiler_params=_params(("arbitrary",)),
        name="in_proj",
    )(x, mod_l, lw["norm1"], lw["w_in"], lw["w_uq"], lw["w_k"], lw["w_v"],
      lw["g_q"], lw["g_k"], lw["g_cq"], lw["g_ckv"], lw["g_qm"], lw["g_km"], cos_t, sin_t, after)


def _cache_kernel(ck_ref, wk_ref, wv_ref, gk_ref, km_ref, vm_ref):
    rows = ck_ref.shape[2]
    cos = jnp.ones((rows, LANE), F32)
    sin = jnp.zeros((rows, LANE), F32)
    _mla_kv(ck_ref[0, 0], wk_ref.at[0], wv_ref.at[0], gk_ref.at[0], cos, sin,
            km_ref.at[0, 0], vm_ref.at[0, 0])


def _cache_kv(ck, w_k, w_v, g_km):
    db, depth, p, _ = ck.shape
    spec = lambda w: pl.BlockSpec((1, 1, p, w), lambda b, l: (b, l, 0, 0))
    wspec = lambda r: pl.BlockSpec((1, r, HW), lambda b, l: (l, 0, 0))
    return pl.pallas_call(
        _cache_kernel,
        grid=(db, depth),
        in_specs=[spec(256), wspec(256), wspec(256), wspec(1)],
        out_specs=[spec(HW), spec(HW)],
        out_shape=[jax.ShapeDtypeStruct((db, depth, p, HW), BF16)] * 2,
        compiler_params=_params(("arbitrary", "arbitrary")),
        name="cache_kv",
    )(ck, w_k, w_v, g_km)


def _softmax_av(s_list, v_list):
    m = s_list[0].max(axis=-1, keepdims=True)
    for s in s_list[1:]:
        m = jnp.maximum(m, s.max(axis=-1, keepdims=True))
    acc = None
    den = None
    for s, v in zip(s_list, v_list):
        p = jnp.exp(s - m)
        l = p.sum(axis=-1, keepdims=True)
        o = jnp.dot(p.astype(BF16), v, preferred_element_type=F32)
        acc = o if acc is None else acc + o
        den = l if den is None else den + l
    return acc / den


def _ctx_attn_kernel(qn, kn, vn, qm, km, vm, on, om):
    for q, k, v, o in ((qn, kn, vn, on), (qm, km, vm, om)):
        for h in range(HEADS):
            sl = slice(h * LANE, (h + 1) * LANE)
            s = _nt_dot(q[:, sl], k[:, sl])
            o[:, sl] = _softmax_av([s], [v[:, sl]]).astype(BF16)


def _ctx_attn(qn, kn, vn, qm, km, vm, seq):
    n = qn.shape[0]
    spec = pl.BlockSpec((seq, HW), lambda i: (i, 0))
    return pl.pallas_call(
        _ctx_attn_kernel,
        grid=(n // seq,),
        in_specs=[spec] * 6,
        out_specs=[spec] * 2,
        out_shape=[jax.ShapeDtypeStruct((n, HW), BF16)] * 2,
        compiler_params=_params(("arbitrary",)),
        name="ctx_attn",
    )(qn, kn, vn, qm, km, vm)


def _lat_mla_kernel(q, k, v, kc, vc, o):
    s1 = _nt_dot(q[...], k[...])
    s2 = _nt_dot(q[...], kc[0])
    o[...] = _softmax_av([s1, s2], [v[...], vc[0]]).astype(BF16)


def _lat_mla(qm, km, vm, kc, vc, db):
    n = qm.shape[0]
    ds = n // db
    nq = ds // TQ
    qspec = pl.BlockSpec((TQ, LANE), lambda b, h, i: (b * nq + i, h))
    kspec = pl.BlockSpec((ds, LANE), lambda b, h, i: (b, h))
    cspec = pl.BlockSpec((1, kc.shape[1], LANE), lambda b, h, i: (b, 0, h))
    return pl.pallas_call(
        _lat_mla_kernel,
        grid=(db, HEADS, nq),
        in_specs=[qspec, kspec, kspec, cspec, cspec],
        out_specs=qspec,
        out_shape=jax.ShapeDtypeStruct((n, HW), BF16),
        compiler_params=_params(("arbitrary",) * 3),
        name="lat_mla",
    )(qm, km, vm, kc, vc)


def _nat_kernel(q, k, v, kc, vc, bias, o, *, rows):
    r = pl.program_id(1)
    rs = jnp.clip(r - WIN_R // 2, 0, rows - WIN_R)
    start = pl.multiple_of(rs * GRID_W, GRID_W)
    band = WIN_R * GRID_W
    for h in range(HEADS):
        sl = slice(h * LANE, (h + 1) * LANE)
        qh = q[:, sl]
        s1 = _nt_dot(qh, k[pl.ds(start, band), sl]) + bias[0, h]
        s2 = _nt_dot(qh, kc[0, :, sl])
        o[:, sl] = _softmax_av([s1, s2], [v[pl.ds(start, band), sl], vc[0, :, sl]]).astype(BF16)


def _nat_attn(qn, kn, vn, kc, vc, bias, db):
    n = qn.shape[0]
    ds = n // db
    rows = ds // GRID_W
    band = WIN_R * GRID_W

    def variant(r):
        return jnp.where(r < WIN_R // 2, r, jnp.where(r > rows - WIN_R // 2, r - (rows - WIN_R), WIN_R // 2))

    qspec = pl.BlockSpec((GRID_W, HW), lambda b, r: (b * rows + r, 0))
    kspec = pl.BlockSpec((ds, HW), lambda b, r: (b, 0))
    cspec = pl.BlockSpec((1, kc.shape[1], HW), lambda b, r: (b, 0, 0))
    bspec = pl.BlockSpec((1, HEADS, GRID_W, band), lambda b, r: (variant(r), 0, 0, 0))
    return pl.pallas_call(
        functools.partial(_nat_kernel, rows=rows),
        grid=(db, rows),
        in_specs=[qspec, kspec, kspec, cspec, cspec, bspec],
        out_specs=qspec,
        out_shape=jax.ShapeDtypeStruct((n, HW), BF16),
        compiler_params=_params(("arbitrary", "arbitrary")),
        name="nat_attn",
    )(qn, kn, vn, kc, vc, bias)


def _split3(x):
    hi = x.astype(BF16)
    r = x - hi.astype(F32)
    mid = r.astype(BF16)
    lo = (r - mid.astype(F32)).astype(BF16)
    return hi, mid, lo


def _pool(p_prev, p_cur, p_next, posb, seq_len):
    rows = p_cur.shape[0]
    halo = p_prev.shape[0]
    ext = rows + 2 * halo
    pext = jnp.concatenate([p_prev, p_cur, p_next], axis=0)
    parts = _split3(pext)
    t = posb + lax.broadcasted_iota(jnp.int32, (rows, ext), 0)
    s = posb - halo + lax.broadcasted_iota(jnp.int32, (rows, ext), 1)
    tcol = posb + lax.broadcasted_iota(jnp.int32, (rows, 1), 0)
    grp = lax.broadcasted_iota(jnp.int32, (rows, 256), 1) // POOL_G
    d = jnp.zeros((rows, 256), F32)
    for gi, w in enumerate(POOL_WINDOWS):
        lo = jnp.maximum(t - w // 2, 0)
        hi = jnp.minimum(t + (w - w // 2), seq_len)
        sel = jnp.where(s >= lo, jnp.where(s < hi, 1.0, 0.0), 0.0).astype(BF16)
        tot = sum(jnp.dot(sel, part, preferred_element_type=F32) for part in parts)
        cnt = (jnp.minimum(tcol + (w - w // 2), seq_len) - jnp.maximum(tcol - w // 2, 0)).astype(F32)
        d = jnp.where(grp == gi, tot / cnt - p_cur, d)
    return d


def _first_max(x, pos, sentinel):
    m = jnp.max(x, axis=0, keepdims=True)
    idx = jnp.min(jnp.where(x == m, pos, sentinel), axis=0, keepdims=True)
    return m, idx


def _topk_stage1(qh, sk_ref):
    c = qh.shape[0]
    key_pos = lax.broadcasted_iota(jnp.int32, (PEER_NKEYS, c), 0).astype(F32)
    row16 = lax.broadcasted_iota(jnp.int32, (PEER_TOPK, c), 0)
    neg = jnp.float32(-jnp.inf)
    s0 = _nt_dot(sk_ref[0], qh)
    s1 = _nt_dot(sk_ref[1], qh)

    def stage1(a, carry):
        out = []
        for s, sv, si in (carry[0:3], carry[3:6]):
            m, idx = _first_max(s, key_pos, float(PEER_NKEYS))
            out += [jnp.where(key_pos == idx, neg, s),
                    jnp.where(row16 == a, m, sv), jnp.where(row16 == a, idx, si)]
        return tuple(out)

    zf = jnp.zeros((PEER_TOPK, c), F32)
    _, sv0, si0, _, sv1, si1 = lax.fori_loop(0, PEER_TOPK, stage1, (s0, zf, zf, s1, zf, zf))
    return sv0, si0, sv1, si1


def _topk_pieces(sv0, sv1):
    c = sv0.shape[1]
    neg = jnp.float32(-jnp.inf)
    sub8 = lax.broadcasted_iota(jnp.int32, (8, c), 0)
    sub8f = sub8.astype(F32)
    cs, cf = [], []
    for a in range(8):
        nb = PEER_TOPK // (a + 1)
        for b0 in range(0, nb, 8):
            val = sv0[a:a + 1] + sv1[b0:b0 + 8]
            if nb - b0 < 8:
                val = jnp.where(sub8 < nb - b0, val, neg)
            cs.append(val)
            cf.append(sub8f + float(a * PEER_TOPK + b0))
    cs.append(sv0[8:16] + sv1[0:1])
    cf.append((sub8f + 8.0) * float(PEER_TOPK))
    return cs, cf


def _topk_stage2(chains, cf):
    npc = len(cf)
    c = cf[0].shape[1]
    row16 = lax.broadcasted_iota(jnp.int32, (PEER_TOPK, c), 0)
    neg = jnp.float32(-jnp.inf)
    nflat = float(PEER_TOPK * PEER_TOPK)
    zf = jnp.zeros((PEER_TOPK, c), F32)

    def step(k, carry):
        out = []
        for ch in range(len(chains)):
            vals = carry[ch * (npc + 2):ch * (npc + 2) + npc]
            tv, tp = carry[ch * (npc + 2) + npc], carry[ch * (npc + 2) + npc + 1]
            m = vals[0]
            for v in vals[1:]:
                m = jnp.maximum(m, v)
            m = jnp.max(m, axis=0, keepdims=True)
            pos = None
            for v, f in zip(vals, cf):
                cand = jnp.where(v == m, f, nflat)
                pos = cand if pos is None else jnp.minimum(pos, cand)
            pos = jnp.min(pos, axis=0, keepdims=True)
            out += [jnp.where(f == pos, neg, v) for v, f in zip(vals, cf)]
            out += [jnp.where(row16 == k, m, tv), jnp.where(row16 == k, pos, tp)]
        return tuple(out)

    init = []
    for cs in chains:
        init += list(cs) + [zf, zf]
    res = lax.fori_loop(0, PEER_TOPK, step, tuple(init))
    return [(res[ch * (npc + 2) + npc], res[ch * (npc + 2) + npc + 1]) for ch in range(len(chains))]


def _topk_ids(tp, si0, si1):
    a = jnp.floor(tp * (1.0 / PEER_TOPK))
    b = tp - a * float(PEER_TOPK)
    ea = jnp.zeros_like(tp)
    eb = jnp.zeros_like(tp)
    for j in range(PEER_TOPK):
        ea = jnp.where(a == float(j), si0[j:j + 1], ea)
        eb = jnp.where(b == float(j), si1[j:j + 1], eb)
    return ea * float(PEER_NKEYS) + eb


def _out_kernel(on_ref, om_ref, pc_ref, pp_ref, pn_ref, x_ref, mod_ref,
                won_ref, wop_ref, wom_ref, pw_ref, ps_ref, n2_ref, wq_ref, sk_ref,
                x1_ref, h2_ref, ids_ref, gt_ref, gn_ref, h2c_ref, q_scr, idt_scr, *, bps, seq_len):
    d = x_ref.shape[1]
    rows = x_ref.shape[0]
    i = pl.program_id(0)
    mod = mod_ref[0]
    g1 = mod[:, 2 * d:3 * d]
    sh2 = mod[:, 3 * d:4 * d]
    sc2 = mod[:, 4 * d:5 * d]

    posb = (i % bps) * rows
    dpool = _pool(pp_ref[...], pc_ref[...], pn_ref[...], posb, seq_len)
    ypool = jnp.dot(dpool.astype(BF16), pw_ref[...], preferred_element_type=F32) * ps_ref[...]
    mix = (jnp.dot(on_ref[...], won_ref[...], preferred_element_type=F32)
           + jnp.dot(ypool.astype(BF16), wop_ref[...], preferred_element_type=F32)
           + jnp.dot(om_ref[...], wom_ref[...], preferred_element_type=F32))
    x1 = x_ref[...] + g1 * mix
    x1_ref[...] = x1
    h2 = _rms(x1, n2_ref[...]) * (1.0 + sc2) + sh2
    h2_ref[...] = h2
    for j in range(d // LANE):
        h2c_ref[j] = h2[:, j * LANE:(j + 1) * LANE]

    q = jnp.dot(h2.astype(BF16), wq_ref[...], preferred_element_type=F32)
    for hh in range(PEER_HEADS):
        q_scr[hh] = q[:, hh * LANE:(hh + 1) * LANE].astype(BF16)

    chunks = range(0, rows, LANE)

    def head(hh, _):
        sorted_keys = [_topk_stage1(q_scr[hh, c0:c0 + LANE, :], sk_ref) for c0 in chunks]
        pieces = [_topk_pieces(sv0, sv1) for sv0, _, sv1, _ in sorted_keys]
        picked = _topk_stage2([cs for cs, _ in pieces], pieces[0][1])
        r0 = pl.multiple_of(hh * PEER_TOPK, PEER_TOPK)
        for c0, (tv, tp), (_, si0, _, si1) in zip(chunks, picked, sorted_keys):
            ex = jnp.exp(tv - tv[0:1])
            gt_ref[pl.ds(r0, PEER_TOPK), c0:c0 + LANE] = ex / jnp.sum(ex, axis=0, keepdims=True)
            idt_scr[pl.ds(r0, PEER_TOPK), c0:c0 + LANE] = _topk_ids(tp, si0, si1)
        return 0

    lax.fori_loop(0, PEER_HEADS, head, 0)
    ids_ref[...] = idt_scr[...].T.astype(jnp.int32)
    gn_ref[...] = gt_ref[...].T


def _out_proj(x, on, om, p, mod_l, row_off, bpm, lw, seq_len):
    n, d = x.shape
    nb = n // TB
    bps = seq_len // TB
    halo = 8
    hb = TB // halo
    tok = lambda w: pl.BlockSpec((TB, w), lambda i: (i, 0))
    in_specs = [tok(HW), tok(HW), tok(256),
                pl.BlockSpec((halo, 256), lambda i: (jnp.maximum(i * hb - 1, 0), 0)),
                pl.BlockSpec((halo, 256), lambda i: (jnp.minimum((i + 1) * hb, n // halo - 1), 0)),
                tok(d),
                pl.BlockSpec((1, 1, mod_l.shape[-1]), lambda i: (row_off + i // bpm, 0, 0)),
                _const_spec((HW, d)), _const_spec((256, d)), _const_spec((HW, d)),
                _const_spec((256, 256)), _const_spec((1, 256)), _const_spec((1, d)),
                _const_spec((d, PEER_HEADS * LANE)), _const_spec((2, PEER_NKEYS, LANE))]
    nk = PEER_HEADS * PEER_TOPK
    return pl.pallas_call(
        functools.partial(_out_kernel, bps=bps, seq_len=seq_len),
        grid=(nb,),
        in_specs=in_specs,
        out_specs=[tok(d), tok(d), tok(nk), pl.BlockSpec((nk, TB), lambda i: (0, i)), tok(nk),
                   pl.BlockSpec((d // LANE, TB, LANE), lambda i: (0, i, 0))],
        out_shape=[jax.ShapeDtypeStruct((n, d), F32), jax.ShapeDtypeStruct((n, d), F32),
                   jax.ShapeDtypeStruct((n, nk), jnp.int32), jax.ShapeDtypeStruct((nk, n), F32),
                   jax.ShapeDtypeStruct((n, nk), F32),
                   jax.ShapeDtypeStruct((d // LANE, n, LANE), F32)],
        scratch_shapes=[pltpu.VMEM((PEER_HEADS, TB, LANE), BF16), pltpu.VMEM((nk, TB), F32)],
        compiler_params=_params(("arbitrary",)),
        name="out_proj",
    )(on, om, p, p, p, x, mod_l, lw["w_o_na"], lw["w_o_pool"], lw["w_o_mla"],
      lw["pool_w"], lw["pool_scale"], lw["norm2"], lw["peer_wq"], lw["peer_sk"])


def _gelu_tanh(x):
    return x * (0.5 * (1.0 + jnp.tanh(0.7978845608028654 * (x + 0.044715 * (x * x * x)))))


def _peer_token_mix(chunk, hrow, gcol, ch):
    acc = None
    for s in range(ch):
        us = lax.bitcast_convert_type(chunk(s) & jnp.int32(-65536), F32)
        term = us * hrow[:, s * LANE:(s + 1) * LANE]
        acc = term if acc is None else acc + term
    wgt = gcol * _gelu_tanh(jnp.sum(acc, axis=-1, keepdims=True))
    parts = []
    for s in range(ch):
        vs = lax.bitcast_convert_type(chunk(s) << 16, F32)
        parts.append(jnp.sum(vs * wgt, axis=0, keepdims=True))
    return jnp.concatenate(parts, axis=-1)


def _sc_peer(table3, ids, gates, h2c, n):
    ch, _, lane = h2c.shape
    nk = ids.shape[1]
    info = plsc.get_sparse_core_info()
    nc, nw, nl = info.num_cores, info.num_cores * info.num_subcores, info.num_lanes
    tpw = n // nw
    win = 32
    nq = nk // win
    cpr = lane // nl
    nchunk = ch * cpr
    hc = nchunk // 2
    assert n % nw == 0 and nk % win == 0 and win % nl == 0
    mesh = plsc.VectorSubcoreMesh(core_axis_name="core", subcore_axis_name="subcore")
    hi_mask = jnp.int32(-65536)

    @functools.partial(
        pl.kernel, mesh=mesh,
        out_type=jax.ShapeDtypeStruct((ch, n, lane), F32),
        compiler_params=pltpu.CompilerParams(needs_layout_passes=False),
        scratch_types=[pltpu.VMEM((nk,), jnp.int32), pltpu.VMEM((nk,), F32),
                       pltpu.VMEM((ch, lane), F32), pltpu.VMEM((ch, lane), F32),
                       pltpu.VMEM((win, ch, lane), jnp.int32), pltpu.VMEM((win, ch, lane), jnp.int32),
                       pltpu.VMEM((win * nl,), F32), pltpu.VMEM((win,), F32),
                       pltpu.SemaphoreType.DMA, pltpu.SemaphoreType.DMA, pltpu.SemaphoreType.DMA])
    def peer(tab_hbm, ids_hbm, g_hbm, h2_hbm, y_hbm,
             idx_v, g_v, x_v, y_v, rows_a, rows_b, part_v, w_v, sem_a, sem_b, sem_x):
        wid = lax.axis_index("subcore") * nc + lax.axis_index("core")
        bufs = ((rows_a, sem_a), (rows_b, sem_b))
        lanes = lax.iota(jnp.int32, nl)
        zero = jnp.zeros((nl,), F32)

        def chunk_copies(tok, to_hbm):
            if to_hbm:
                return [pltpu.make_async_copy(y_v.at[j], y_hbm.at[j, tok], sem_x) for j in range(ch)]
            return [pltpu.make_async_copy(h2_hbm.at[j, tok], x_v.at[j], sem_x) for j in range(ch)]

        def fetch(q):
            rows, sem = bufs[q % 2]
            return pltpu.make_async_copy(tab_hbm.at[idx_v.at[pl.ds(q * win, win)]], rows, sem)

        def word(rows, r, cc):
            return rows[r, cc // cpr, pl.ds((cc % cpr) * nl, nl)]

        @pl.loop(0, tpw)
        def _(ti):
            tok = wid * tpw + ti
            loads = chunk_copies(tok, False)
            for cp in loads:
                cp.start()
            pltpu.sync_copy(ids_hbm.at[tok], idx_v)
            pltpu.sync_copy(g_hbm.at[tok], g_v)
            for cp in loads:
                cp.wait()
            for cc in range(nchunk):
                y_v[cc // cpr, pl.ds((cc % cpr) * nl, nl)] = zero
            fetch(0).start()
            for q in range(nq):
                rows = bufs[q % 2][0]
                fetch(q).wait()
                if q + 1 < nq:
                    fetch(q + 1).start()

                for half in range(2):
                    xs = [x_v[(half * hc + c) // cpr, pl.ds(((half * hc + c) % cpr) * nl, nl)]
                          for c in range(hc)]

                    @pl.loop(0, win, step=2)
                    def _(r0):
                        pos = [pl.multiple_of((r0 + k) * nl, nl) for k in range(2)]
                        accs = [[part_v[pl.ds(pos[k], nl)] if half else None, None, None, None]
                                for k in range(2)]
                        for c in range(hc):
                            for k in range(2):
                                u = lax.bitcast_convert_type(
                                    word(rows, r0 + k, half * hc + c) & hi_mask, F32)
                                t = u * xs[c]
                                accs[k][c % 4] = t if accs[k][c % 4] is None else accs[k][c % 4] + t
                        for k in range(2):
                            part_v[pl.ds(pos[k], nl)] = (accs[k][0] + accs[k][1]) + (accs[k][2] + accs[k][3])

                for grp in range(win // nl):
                    s = zero
                    for rr in range(nl):
                        tot = jnp.sum(part_v[pl.ds((grp * nl + rr) * nl, nl)])
                        s = jnp.where(lanes == rr, tot, s)
                    z = 0.7978845608028654 * (s + 0.044715 * (s * s * s))
                    tanh = 1.0 - 2.0 / (jnp.exp(2.0 * z) + 1.0)
                    gate = g_v[pl.ds(q * win + grp * nl, nl)]
                    w_v[pl.ds(grp * nl, nl)] = gate * (s * (0.5 * (1.0 + tanh)))

                for half in range(2):
                    def body(r, yacc):
                        wr = plsc.load_gather(w_v, [jnp.full((nl,), r, jnp.int32)])
                        out = []
                        for c in range(hc):
                            v = lax.bitcast_convert_type(word(rows, r, half * hc + c) << 16, F32)
                            out.append(yacc[c] + wr * v)
                        return tuple(out)

                    yacc = lax.fori_loop(0, win, body, tuple(zero for _ in range(hc)))
                    for c in range(hc):
                        cc = half * hc + c
                        sl = (cc // cpr, pl.ds((cc % cpr) * nl, nl))
                        y_v[sl] = y_v[sl] + yacc[c]
            stores = chunk_copies(tok, True)
            for cp in stores:
                cp.start()
            for cp in stores:
                cp.wait()

    return peer(table3, ids, gates, h2c)


def _residual_kernel(x1_ref, y_ref, mod_ref, x2_hbm, after_hbm, o_ref):
    del x2_hbm
    del after_hbm
    d = x1_ref.shape[1]
    g2 = mod_ref[0][:, 5 * d:6 * d]
    for j in range(d // LANE):
        sl = slice(j * LANE, (j + 1) * LANE)
        o_ref[:, sl] = x1_ref[:, sl] + g2[:, sl] * y_ref[j]


def _residual(x1, y, mod_l, row, x2, after):
    n, d = x1.shape
    tok = pl.BlockSpec((PEER_TB, d), lambda i: (i, 0))
    any_spec = pl.BlockSpec(memory_space=pl.ANY)
    return pl.pallas_call(
        _residual_kernel,
        grid=(y.shape[1] // PEER_TB,),
        in_specs=[tok, pl.BlockSpec((d // LANE, PEER_TB, LANE), lambda i: (0, i, 0)),
                  pl.BlockSpec((1, 1, mod_l.shape[-1]), lambda i: (row, 0, 0)),
                  any_spec, any_spec],
        out_specs=tok,
        out_shape=jax.ShapeDtypeStruct((n, d), F32),
        input_output_aliases={3: 0},
        compiler_params=_params(("arbitrary",)),
        name="residual",
    )(x1, y, mod_l, x2, after)


def _peer_kernel(ids_hbm, gt_ref, h2_ref, x1_ref, mod_ref, tab_hbm, o_ref,
                 ids_s, buf, sem_i, sem_r, *, first_block):
    d = x1_ref.shape[1]
    ch = d // LANE
    pitch = ch + 1
    nsub = x1_ref.shape[0] // PEER_SUB
    nk = gt_ref.shape[0]
    nids = PEER_SUB * nk
    i = pl.program_id(0) + first_block
    g2 = mod_ref[0][:, 5 * d:6 * d]
    tok_lane = lax.broadcasted_iota(jnp.int32, gt_ref.shape, 1)

    def ids_copy(j, slot):
        start = pl.multiple_of((i * nsub + j) * nids, nids)
        return pltpu.make_async_copy(ids_hbm.at[pl.ds(start, nids)],
                                     ids_s.at[pl.ds(slot * nids, nids)], sem_i.at[slot])

    def row_copy(slot, e, f):
        src = tab_hbm.at[pl.ds(pl.multiple_of(e * ch, ch), ch), :]
        dst = buf.at[slot, pl.ds(f * pitch, ch), :]
        return pltpu.make_async_copy(src, dst, sem_r.at[slot])

    def issue_rows(slot):
        for t in range(PEER_SUB):
            def body(kk, _):
                for r in range(8):
                    f = t * nk + kk * 8 + r
                    row_copy(slot, ids_s[slot * nids + f], f).start(priority=r % 2)
                return 0

            lax.fori_loop(0, nk // 8, body, 0)

    def wait_rows(slot):
        done = buf.at[slot, pl.ds(0, nids * ch), :]
        pltpu.make_async_copy(done, done, sem_r.at[slot]).wait()

    def compute(slot, j):
        base = pl.multiple_of(j * PEER_SUB, PEER_SUB)
        h8 = h2_ref[pl.ds(base, PEER_SUB), :]
        ys = []
        for t in range(PEER_SUB):
            chunk = lambda s: buf[slot, pl.ds(t * nk * pitch + s, nk, stride=pitch), :]
            gcol = jnp.sum(jnp.where(tok_lane == base + t, gt_ref[...], 0.0), axis=-1, keepdims=True)
            ys.append(_peer_token_mix(chunk, h8[t:t + 1, :], gcol, ch))
        y8 = jnp.concatenate(ys, axis=0)
        o_ref[pl.ds(base, PEER_SUB), :] = x1_ref[pl.ds(base, PEER_SUB), :] + g2 * y8

    first = ids_copy(0, 0)
    first.start()
    first.wait()
    issue_rows(0)
    ids_copy(1, 1).start()

    def pair(jj, _):
        j0 = 2 * jj
        ids_copy(j0 + 1, 1).wait()
        issue_rows(1)

        @pl.when(j0 + 2 < nsub)
        def _():
            ids_copy(j0 + 2, 0).start()

        wait_rows(0)
        compute(0, j0)

        @pl.when(j0 + 2 < nsub)
        def _():
            ids_copy(j0 + 2, 0).wait()
            issue_rows(0)

        @pl.when(j0 + 3 < nsub)
        def _():
            ids_copy(j0 + 3, 1).start()

        wait_rows(1)
        compute(1, j0 + 1)
        return 0

    lax.fori_loop(0, nsub // 2, pair, 0)


def _pack_tables(peer_u, peer_v):
    e, d = peer_u.shape
    ub = lax.bitcast_convert_type(peer_u.astype(BF16), jnp.uint16).astype(jnp.uint32)
    vb = lax.bitcast_convert_type(peer_v.astype(BF16), jnp.uint16).astype(jnp.uint32)
    words = lax.bitcast_convert_type((ub << 16) | vb, jnp.int32)
    return words.reshape(e, d // LANE, LANE)


def _peer(x1, h2, ids, gt, mod_l, row, table, tok0):
    n, d = x1.shape
    nk = gt.shape[0]
    b0 = tok0 // PEER_TB
    nb = n // PEER_TB - b0
    tok = pl.BlockSpec((PEER_TB, d), lambda i: (i + b0, 0))
    any_spec = pl.BlockSpec(memory_space=pl.ANY)
    return pl.pallas_call(
        functools.partial(_peer_kernel, first_block=b0),
        grid=(nb,),
        in_specs=[any_spec,
                  pl.BlockSpec((nk, PEER_TB), lambda i: (0, i + b0)),
                  tok, tok,
                  pl.BlockSpec((1, 1, mod_l.shape[-1]), lambda i: (row, 0, 0)),
                  any_spec],
        out_specs=tok,
        out_shape=jax.ShapeDtypeStruct((n, d), F32),
        scratch_shapes=[pltpu.SMEM((2 * PEER_SUB * nk,), jnp.int32),
                        pltpu.VMEM((2, PEER_SUB * nk * (d // LANE + 1), LANE), jnp.int32),
                        pltpu.SemaphoreType.DMA((2,)),
                        pltpu.SemaphoreType.DMA((2,))],
        compiler_params=_params(("arbitrary",)),
        name="peer",
    )(ids.reshape(n * nk), gt, h2, x1, mod_l, table.reshape(-1, LANE))


def _pad_heads(w, width):
    pad = [(0, 0)] * (w.ndim - 1) + [(0, LANE - width)]
    w = jnp.pad(w, pad)
    return w.reshape(w.shape[:-2] + (HW,))


def _head_gain(g, width):
    depth = g.shape[0]
    g = jnp.pad(g, ((0, 0), (0, LANE - width)))
    return jnp.tile(g, (1, HEADS)).reshape(depth, 1, HW)


def _rope_tables(seq):
    t = np.arange(seq)
    half = MLA_ROPE // 2
    inv = ROPE_THETA ** (-np.arange(0, half, 2, dtype=np.float32) / half)
    cos = np.ones((seq, LANE), np.float32)
    sin = np.zeros((seq, LANE), np.float32)
    for off, pos in ((MLA_NOPE, t // GRID_W), (MLA_NOPE + half, t % GRID_W)):
        ang = pos.astype(np.float32)[:, None] * inv[None, :]
        q = half // 2
        cos[:, off:off + q] = np.cos(ang)
        cos[:, off + q:off + half] = np.cos(ang)
        sin[:, off:off + q] = -np.sin(ang)
        sin[:, off + q:off + half] = np.sin(ang)
    return jnp.asarray(cos), jnp.asarray(sin)


def _nat_bias(rel_bias):
    v = np.arange(WIN_R)[:, None]
    j = np.arange(WIN_R)[None, :]
    dr = j - v + WIN_R - 1
    cq = np.arange(GRID_W)[:, None]
    kc = np.arange(GRID_W)[None, :]
    cstart = np.clip(cq - WIN_C // 2, 0, GRID_W - WIN_C)
    ok = (kc >= cstart) & (kc < cstart + WIN_C)
    dc = np.clip(kc - cq + WIN_C - 1, 0, 2 * WIN_C - 2)
    b = rel_bias[:, :, dr]
    b = b[..., dc]
    b = jnp.where(jnp.asarray(ok)[None, None, None, None], b, NEG_INF)
    b = jnp.transpose(b, (0, 2, 1, 4, 3, 5))
    return b.reshape(b.shape[0], WIN_R, HEADS, GRID_W, WIN_R * GRID_W)


def _layer_weights(w_in, na_q_norm, na_k_norm, mla_cq_norm, mla_ckv_norm, mla_w_uq, mla_w_ukv,
                   mla_q_norm, mla_k_norm, w_out, pool_w, pool_scale, norm1, norm2,
                   peer_wq, peer_subkeys):
    depth, d, _ = w_in.shape
    na_w = HEADS * NA_DH
    segs = np.cumsum([0, na_w, na_w, na_w, 256, 256, 128, MLA_ROPE])
    part = lambda i: w_in[:, :, segs[i]:segs[i + 1]]
    heads = lambda w: _pad_heads(w.reshape(depth, d, HEADS, NA_DH), NA_DH)
    w_in_p = jnp.concatenate(
        [heads(part(0)), heads(part(1)), heads(part(2)), part(3), part(4), part(5),
         jnp.pad(part(6), ((0, 0), (0, 0), (0, LANE - MLA_ROPE)))], axis=-1).astype(BF16)

    w_uq = _pad_heads(mla_w_uq, MLA_QK).astype(BF16)
    k_nope = _pad_heads(mla_w_ukv[..., :MLA_NOPE], MLA_NOPE)
    eye = np.zeros((MLA_ROPE, HEADS, LANE), np.float32)
    for h in range(HEADS):
        eye[np.arange(MLA_ROPE), h, MLA_NOPE + np.arange(MLA_ROPE)] = 1.0
    eye = jnp.broadcast_to(jnp.asarray(eye.reshape(MLA_ROPE, HW)), (depth, MLA_ROPE, HW))
    zer = jnp.zeros((depth, 256 - 128 - MLA_ROPE, HW), F32)
    w_k = jnp.concatenate([k_nope, eye, zer], axis=1).astype(BF16)
    w_v = jnp.concatenate([_pad_heads(mla_w_ukv[..., MLA_NOPE:], MLA_V),
                           jnp.zeros((depth, 128, HW), F32)], axis=1).astype(BF16)

    mix_w = HEADS * NA_DH
    w_o_na = jnp.pad(w_out[:, :mix_w].reshape(depth, HEADS, NA_DH, d),
                     ((0, 0), (0, 0), (0, LANE - NA_DH), (0, 0))).reshape(depth, HW, d).astype(BF16)
    w_o_pool = w_out[:, mix_w:mix_w + 256].astype(BF16)
    w_o_mla = jnp.pad(w_out[:, mix_w + 256:].reshape(depth, HEADS, MLA_V, d),
                      ((0, 0), (0, 0), (0, LANE - MLA_V), (0, 0))).reshape(depth, HW, d).astype(BF16)
    ng = len(POOL_WINDOWS)
    pw = jnp.zeros((depth, ng * POOL_G, ng * POOL_G), F32)
    for g in range(ng):
        pw = pw.at[:, g * POOL_G:(g + 1) * POOL_G, g * POOL_G:(g + 1) * POOL_G].set(pool_w[:, g])

    half = peer_subkeys.shape[-1]
    sk = jnp.stack([jnp.pad(peer_subkeys[:, 0], ((0, 0), (0, 0), (0, LANE - half))),
                    jnp.pad(peer_subkeys[:, 1], ((0, 0), (0, 0), (LANE - half, 0)))], axis=1).astype(BF16)

    return dict(
        w_in=w_in_p, w_uq=w_uq, w_k=w_k, w_v=w_v,
        g_q=_head_gain(na_q_norm, NA_DH), g_k=_head_gain(na_k_norm, NA_DH),
        g_cq=mla_cq_norm[:, None, :], g_ckv=mla_ckv_norm[:, None, :],
        g_qm=_head_gain(mla_q_norm, MLA_QK), g_km=_head_gain(mla_k_norm, MLA_QK),
        w_o_na=w_o_na, w_o_pool=w_o_pool, w_o_mla=w_o_mla,
        pool_w=pw.astype(BF16), pool_scale=pool_scale[:, None, :],
        norm1=norm1[:, None, :], norm2=norm2[:, None, :],
        peer_wq=peer_wq.astype(BF16), peer_sk=sk)


def kernel(x_prompt, x_sample, c, cache_nat_k, cache_nat_v, cache_mla_ckv, cache_mla_krope, c_ctx, w_mod, b_mod, norm1, norm2, w_in, na_q_norm, na_k_norm, na_rel_bias, pool_w, pool_scale, mla_cq_norm, mla_ckv_norm, mla_w_uq, mla_w_ukv, mla_q_norm, mla_k_norm, w_out, peer_wq, peer_subkeys, peer_u, peer_v):
    batch, seq, d = x_prompt.shape
    db, ds, _ = x_sample.shape
    depth = w_mod.shape[0]
    past = cache_nat_k.shape[2]
    assert seq == TB and ds % TB == 0 and ds % (GRID_W * WIN_R) == 0 and db + 1 <= 8

    cond8 = jnp.concatenate([c_ctx[None, :], c, jnp.zeros((8 - 1 - db, d), F32)], axis=0)
    mod = _modulation(cond8, w_mod, b_mod).reshape(depth, 8, 1, 6 * d)

    lw_all = _layer_weights(w_in, na_q_norm, na_k_norm, mla_cq_norm, mla_ckv_norm, mla_w_uq,
                            mla_w_ukv, mla_q_norm, mla_k_norm, w_out, pool_w, pool_scale,
                            norm1, norm2, peer_wq, peer_subkeys)
    bias_all = _nat_bias(na_rel_bias)
    tables = [_pack_tables(peer_u[l], peer_v[l]) for l in range(depth)]
    cos_lat, sin_lat = _rope_tables(ds)
    cos_ctx = jnp.ones((TB, LANE), F32)
    sin_ctx = jnp.zeros((TB, LANE), F32)

    ck = jnp.concatenate([cache_mla_ckv, cache_mla_krope,
                          jnp.zeros(cache_mla_ckv.shape[:-1] + (256 - 128 - MLA_ROPE,), F32)],
                         axis=-1).astype(BF16)
    kc_mla, vc_mla = _cache_kv(ck, lw_all["w_k"], lw_all["w_v"], lw_all["g_km"])
    kc_na = _pad_heads(cache_nat_k, NA_DH).astype(BF16)
    vc_na = _pad_heads(cache_nat_v, NA_DH).astype(BF16)

    xs = [x_prompt.reshape(batch * seq, d)] + [x_sample[b] for b in range(db)]
    one_row = max(batch * seq, ds) + 1
    lat_bpm = ds // TB
    ks, vs, ckvs, krs = [], [], [], []
    pending = None
    after = xs[0]

    def join(item, follow):
        si, x1, y_sc, x2, mod_l = item
        xs[si] = _residual(x1, y_sc, mod_l, si, x2, follow)
        return xs[si]

    for l in range(depth):
        lw = {k: v[l] for k, v in lw_all.items()}
        mod_l = mod[l]
        for si in range(db + 1):
            x = xs[si]
            if si == 0:
                (qn, kn, vn, knf, vnf, p, qm, km, vm, ckv, kr) = _in_proj(
                    x, mod_l, 0, one_row, lw, cos_ctx, sin_ctx, 1, after)
                on, om = _ctx_attn(qn, kn, vn, qm, km, vm, seq)
                x1, h2, ids, gt, gn, h2c = _out_proj(x, on, om, p, mod_l, 0, one_row, lw, seq)
                ks.append(knf.reshape(batch, seq, HEADS, LANE)[..., :NA_DH])
                vs.append(vnf.reshape(batch, seq, HEADS, LANE)[..., :NA_DH])
                ckvs.append(ckv.reshape(batch, seq, 128))
                krs.append(kr.reshape(batch, seq, LANE)[..., :MLA_ROPE])
            else:
                b = si - 1
                (qn, kn, vn, _, _, p, qm, km, vm, _, _) = _in_proj(
                    x, mod_l, si, one_row, lw, cos_lat, sin_lat, lat_bpm, after)
                on = _nat_attn(qn, kn, vn, kc_na[b:b + 1, l], vc_na[b:b + 1, l], bias_all[l], 1)
                om = _lat_mla(qm, km, vm, kc_mla[b:b + 1, l], vc_mla[b:b + 1, l], 1)
                x1, h2, ids, gt, gn, h2c = _out_proj(x, on, om, p, mod_l, si, one_row, lw, ds)
            n_sc = x.shape[0] * SC_SHARE[0] // SC_SHARE[1] // PEER_TB * PEER_TB
            if (l + si) % 4 == 1:
                n_sc -= PEER_TB
            if l == depth - 1 and si == db:
                n_sc = x.shape[0] // 2
            y_sc = _sc_peer(tables[l], ids, gn, h2c, n_sc)
            x2 = _peer(x1, h2, ids, gt, mod_l, si, tables[l], n_sc)
            after = x2 if pending is None else join(pending, x2)
            pending = (si, x1, y_sc, x2, mod_l)
    join(pending, pending[1])

    return (xs[0].reshape(batch, seq, d), jnp.stack(xs[1:], axis=0),
            jnp.stack(ks, axis=1), jnp.stack(vs, axis=1),
            jnp.stack(ckvs, axis=1), jnp.stack(krs, axis=1))
```

```python
import functools

import numpy as np
import jax
import jax.numpy as jnp
from jax import lax
from jax.experimental import pallas as pl
from jax.experimental.pallas import tpu as pltpu
from jax.experimental.pallas import tpu_sc as plsc

F32 = jnp.float32
BF16 = jnp.bfloat16

EPS = 1e-6
ROPE_THETA = 10000.0
NEG_INF = -1e30
GRID_W = 64
HEADS = 6
NA_DH = 64
WIN_R = 8
WIN_C = 16
POOL_WINDOWS = (2, 4, 8, 16)
POOL_G = 64
MLA_NOPE = 64
MLA_ROPE = 32
MLA_QK = MLA_NOPE + MLA_ROPE
MLA_V = 64
PEER_HEADS = 8
PEER_NKEYS = 128
PEER_TOPK = 16
LANE = 128
HW = HEADS * LANE
TB = 256
TQ = 256
PEER_TB = 128
PEER_SUB = 8
VMEM_LIMIT = 56 * 1024 * 1024
SC_SHARE = (3, 4)

_CQ, _CK, _CV = 0, HW, 2 * HW
_CP = 3 * HW
_CCQ = _CP + 256
_CCKV = _CCQ + 256
_CKR = _CCKV + 128
IN_W = _CKR + 128


def _params(sem, vmem=VMEM_LIMIT):
    return pltpu.CompilerParams(dimension_semantics=sem, vmem_limit_bytes=vmem)


def _const_spec(shape):
    n = len(shape)
    return pl.BlockSpec(shape, lambda *_: (0,) * n)


def _nt_dot(a, b):
    return lax.dot_general(a, b, (((1,), (1,)), ((), ())), preferred_element_type=F32)


def _mod_kernel(c_ref, w_ref, b_ref, o_ref):
    c = c_ref[...]
    s = c / (1.0 + jnp.exp(-c))
    o_ref[0] = jnp.dot(s, w_ref[0], preferred_element_type=F32,
                       precision=lax.Precision.HIGHEST) + b_ref[0]


def _modulation(cond8, w_mod, b_mod):
    depth, d, n6 = w_mod.shape
    tn = n6 // 4
    return pl.pallas_call(
        _mod_kernel,
        grid=(depth, n6 // tn),
        in_specs=[_const_spec((8, d)),
                  pl.BlockSpec((1, d, tn), lambda l, j: (l, 0, j)),
                  pl.BlockSpec((1, 1, tn), lambda l, j: (l, 0, j))],
        out_specs=pl.BlockSpec((1, 8, tn), lambda l, j: (l, 0, j)),
        out_shape=jax.ShapeDtypeStruct((depth, 8, n6), F32),
        compiler_params=_params(("arbitrary", "arbitrary")),
        name="modulation",
    )(cond8, w_mod, b_mod.reshape(depth, 1, n6))


def _rms(z, gain):
    return z * lax.rsqrt(jnp.mean(z * z, axis=-1, keepdims=True) + EPS) * gain


def _head_rms(zh, gain_h, n_real):
    ms = jnp.sum(zh * zh, axis=-1, keepdims=True) * (1.0 / n_real)
    return zh * lax.rsqrt(ms + EPS) * gain_h


def _rope(zh, cos, sin, is_x1):
    rot = jnp.where(is_x1, pltpu.roll(zh, LANE - 8, 1), pltpu.roll(zh, 8, 1))
    return zh * cos + rot * sin


def _is_x1(rows):
    lane = lax.broadcasted_iota(jnp.int32, (rows, LANE), 1)
    first = jnp.where(lane >= MLA_NOPE, jnp.where(lane < MLA_NOPE + 8, 1, 0), 0)
    second = jnp.where(lane >= MLA_NOPE + 16, jnp.where(lane < MLA_NOPE + 24, 1, 0), 0)
    return (first + second) > 0


def _mla_kv(ck, wk_ref, wv_ref, gk_ref, cos, sin, km_ref, vm_ref):
    rows = ck.shape[0]
    kk = jnp.dot(ck, wk_ref[...], preferred_element_type=F32)
    is_x1 = _is_x1(rows)
    for h in range(HEADS):
        sl = slice(h * LANE, (h + 1) * LANE)
        kh = _head_rms(kk[:, sl], gk_ref[:, sl], MLA_QK)
        km_ref[:, sl] = _rope(kh, cos, sin, is_x1).astype(BF16)
    vm_ref[...] = jnp.dot(ck, wv_ref[...], preferred_element_type=F32).astype(BF16)


def _in_kernel(x_ref, mod_ref, n1_ref, w_ref, wuq_ref, wk_ref, wv_ref,
               gq_ref, gk_ref, gcq_ref, gckv_ref, gqm_ref, gkm_ref, cos_ref, sin_ref, after_hbm,
               qn_ref, kn_ref, vn_ref, knf_ref, vnf_ref, p_ref,
               qm_ref, km_ref, vm_ref, ckv_ref, kr_ref):
    del after_hbm
    d = x_ref.shape[1]
    rows = x_ref.shape[0]
    mod = mod_ref[0]
    sh1 = mod[:, 0:d]
    sc1 = mod[:, d:2 * d]
    h = _rms(x_ref[...], n1_ref[...]) * (1.0 + sc1) + sh1
    hb = h.astype(BF16)

    def proj(lo, hi):
        return jnp.dot(hb, w_ref[:, lo:hi], preferred_element_type=F32)

    cos = cos_ref[...]
    sin = sin_ref[...]
    is_x1 = _is_x1(rows)

    zq = proj(_CQ, _CQ + HW)
    zk = proj(_CK, _CK + HW)
    for hh in range(HEADS):
        sl = slice(hh * LANE, (hh + 1) * LANE)
        qn_ref[:, sl] = (_head_rms(zq[:, sl], gq_ref[:, sl], NA_DH) * (NA_DH ** -0.5)).astype(BF16)
        kh = _head_rms(zk[:, sl], gk_ref[:, sl], NA_DH)
        knf_ref[:, sl] = kh
        kn_ref[:, sl] = kh.astype(BF16)
    zv = proj(_CV, _CV + HW)
    vnf_ref[...] = zv
    vn_ref[...] = zv.astype(BF16)
    p_ref[...] = proj(_CP, _CP + 256)

    cq = _rms(proj(_CCQ, _CCQ + 256), gcq_ref[...])
    zqm = jnp.dot(cq.astype(BF16), wuq_ref[...], preferred_element_type=F32)
    for hh in range(HEADS):
        sl = slice(hh * LANE, (hh + 1) * LANE)
        qh = _head_rms(zqm[:, sl], gqm_ref[:, sl], MLA_QK)
        qm_ref[:, sl] = (_rope(qh, cos, sin, is_x1) * (MLA_QK ** -0.5)).astype(BF16)

    ckv = _rms(proj(_CCKV, _CCKV + 128), gckv_ref[...])
    kr = proj(_CKR, _CKR + 128)
    ckv_ref[...] = ckv
    kr_ref[...] = kr
    ck = jnp.concatenate([ckv, kr], axis=-1).astype(BF16)
    _mla_kv(ck, wk_ref, wv_ref, gkm_ref, cos, sin, km_ref, vm_ref)


def _in_proj(x, mod_l, row_off, bpm, lw, cos_t, sin_t, rope_blocks, after):
    n, d = x.shape
    nb = n // TB
    tok = lambda w: pl.BlockSpec((TB, w), lambda i: (i, 0))
    rope_spec = pl.BlockSpec((TB, LANE), lambda i: (i % rope_blocks, 0))
    in_specs = [tok(d),
                pl.BlockSpec((1, 1, mod_l.shape[-1]), lambda i: (row_off + i // bpm, 0, 0)),
                _const_spec((1, d)), _const_spec((d, IN_W)), _const_spec((256, HW)),
                _const_spec((256, HW)), _const_spec((256, HW)),
                _const_spec((1, HW)), _const_spec((1, HW)), _const_spec((1, 256)),
                _const_spec((1, 128)), _const_spec((1, HW)), _const_spec((1, HW)),
                rope_spec, rope_spec, pl.BlockSpec(memory_space=pl.ANY)]
    widths = [(HW, BF16), (HW, BF16), (HW, BF16), (HW, F32), (HW, F32), (256, F32),
              (HW, BF16), (HW, BF16), (HW, BF16), (128, F32), (128, F32)]
    return pl.pallas_call(
        _in_kernel,
        grid=(nb,),
        in_specs=in_specs,
        out_specs=[tok(w) for w, _ in widths],
        out_shape=[jax.ShapeDtypeStruct((n, w), dt) for w, dt in widths],
        compiler_params=_params(("arbitrary",)),
        name="in_proj",
    )(x, mod_l, lw["norm1"], lw["w_in"], lw["w_uq"], lw["w_k"], lw["w_v"],
      lw["g_q"], lw["g_k"], lw["g_cq"], lw["g_ckv"], lw["g_qm"], lw["g_km"], cos_t, sin_t, after)


def _cache_kernel(ck_ref, wk_ref, wv_ref, gk_ref, km_ref, vm_ref):
    rows = ck_ref.shape[2]
    cos = jnp.ones((rows, LANE), F32)
    sin = jnp.zeros((rows, LANE), F32)
    _mla_kv(ck_ref[0, 0], wk_ref.at[0], wv_ref.at[0], gk_ref.at[0], cos, sin,
            km_ref.at[0, 0], vm_ref.at[0, 0])


def _cache_kv(ck, w_k, w_v, g_km):
    db, depth, p, _ = ck.shape
    spec = lambda w: pl.BlockSpec((1, 1, p, w), lambda b, l: (b, l, 0, 0))
    wspec = lambda r: pl.BlockSpec((1, r, HW), lambda b, l: (l, 0, 0))
    return pl.pallas_call(
        _cache_kernel,
        grid=(db, depth),
        in_specs=[spec(256), wspec(256), wspec(256), wspec(1)],
        out_specs=[spec(HW), spec(HW)],
        out_shape=[jax.ShapeDtypeStruct((db, depth, p, HW), BF16)] * 2,
        compiler_params=_params(("arbitrary", "arbitrary")),
        name="cache_kv",
    )(ck, w_k, w_v, g_km)


def _softmax_av(s_list, v_list):
    m = s_list[0].max(axis=-1, keepdims=True)
    for s in s_list[1:]:
        m = jnp.maximum(m, s.max(axis=-1, keepdims=True))
    acc = None
    den = None
    for s, v in zip(s_list, v_list):
        p = jnp.exp(s - m)
        l = p.sum(axis=-1, keepdims=True)
        o = jnp.dot(p.astype(BF16), v, preferred_element_type=F32)
        acc = o if acc is None else acc + o
        den = l if den is None else den + l
    return acc / den


def _ctx_attn_kernel(qn, kn, vn, qm, km, vm, on, om):
    for q, k, v, o in ((qn, kn, vn, on), (qm, km, vm, om)):
        for h in range(HEADS):
            sl = slice(h * LANE, (h + 1) * LANE)
            s = _nt_dot(q[:, sl], k[:, sl])
            o[:, sl] = _softmax_av([s], [v[:, sl]]).astype(BF16)


def _ctx_attn(qn, kn, vn, qm, km, vm, seq):
    n = qn.shape[0]
    spec = pl.BlockSpec((seq, HW), lambda i: (i, 0))
    return pl.pallas_call(
        _ctx_attn_kernel,
        grid=(n // seq,),
        in_specs=[spec] * 6,
        out_specs=[spec] * 2,
        out_shape=[jax.ShapeDtypeStruct((n, HW), BF16)] * 2,
        compiler_params=_params(("arbitrary",)),
        name="ctx_attn",
    )(qn, kn, vn, qm, km, vm)


def _lat_mla_kernel(q, k, v, kc, vc, o):
    s1 = _nt_dot(q[...], k[...])
    s2 = _nt_dot(q[...], kc[0])
    o[...] = _softmax_av([s1, s2], [v[...], vc[0]]).astype(BF16)


def _lat_mla(qm, km, vm, kc, vc, db):
    n = qm.shape[0]
    ds = n // db
    nq = ds // TQ
    qspec = pl.BlockSpec((TQ, LANE), lambda b, h, i: (b * nq + i, h))
    kspec = pl.BlockSpec((ds, LANE), lambda b, h, i: (b, h))
    cspec = pl.BlockSpec((1, kc.shape[1], LANE), lambda b, h, i: (b, 0, h))
    return pl.pallas_call(
        _lat_mla_kernel,
        grid=(db, HEADS, nq),
        in_specs=[qspec, kspec, kspec, cspec, cspec],
        out_specs=qspec,
        out_shape=jax.ShapeDtypeStruct((n, HW), BF16),
        compiler_params=_params(("arbitrary",) * 3),
        name="lat_mla",
    )(qm, km, vm, kc, vc)


def _nat_kernel(q, k, v, kc, vc, bias, o, *, rows):
    r = pl.program_id(1)
    rs = jnp.clip(r - WIN_R // 2, 0, rows - WIN_R)
    start = pl.multiple_of(rs * GRID_W, GRID_W)
    band = WIN_R * GRID_W
    for h in range(HEADS):
        sl = slice(h * LANE, (h + 1) * LANE)
        qh = q[:, sl]
        s1 = _nt_dot(qh, k[pl.ds(start, band), sl]) + bias[0, h]
        s2 = _nt_dot(qh, kc[0, :, sl])
        o[:, sl] = _softmax_av([s1, s2], [v[pl.ds(start, band), sl], vc[0, :, sl]]).astype(BF16)


def _nat_attn(qn, kn, vn, kc, vc, bias, db):
    n = qn.shape[0]
    ds = n // db
    rows = ds // GRID_W
    band = WIN_R * GRID_W

    def variant(r):
        return jnp.where(r < WIN_R // 2, r, jnp.where(r > rows - WIN_R // 2, r - (rows - WIN_R), WIN_R // 2))

    qspec = pl.BlockSpec((GRID_W, HW), lambda b, r: (b * rows + r, 0))
    kspec = pl.BlockSpec((ds, HW), lambda b, r: (b, 0))
    cspec = pl.BlockSpec((1, kc.shape[1], HW), lambda b, r: (b, 0, 0))
    bspec = pl.BlockSpec((1, HEADS, GRID_W, band), lambda b, r: (variant(r), 0, 0, 0))
    return pl.pallas_call(
        functools.partial(_nat_kernel, rows=rows),
        grid=(db, rows),
        in_specs=[qspec, kspec, kspec, cspec, cspec, bspec],
        out_specs=qspec,
        out_shape=jax.ShapeDtypeStruct((n, HW), BF16),
        compiler_params=_params(("arbitrary", "arbitrary")),
        name="nat_attn",
    )(qn, kn, vn, kc, vc, bias)


def _split3(x):
    hi = x.astype(BF16)
    r = x - hi.astype(F32)
    mid = r.astype(BF16)
    lo = (r - mid.astype(F32)).astype(BF16)
    return hi, mid, lo


def _pool(p_prev, p_cur, p_next, posb, seq_len):
    rows = p_cur.shape[0]
    halo = p_prev.shape[0]
    ext = rows + 2 * halo
    pext = jnp.concatenate([p_prev, p_cur, p_next], axis=0)
    parts = _split3(pext)
    t = posb + lax.broadcasted_iota(jnp.int32, (rows, ext), 0)
    s = posb - halo + lax.broadcasted_iota(jnp.int32, (rows, ext), 1)
    tcol = posb + lax.broadcasted_iota(jnp.int32, (rows, 1), 0)
    grp = lax.broadcasted_iota(jnp.int32, (rows, 256), 1) // POOL_G
    d = jnp.zeros((rows, 256), F32)
    for gi, w in enumerate(POOL_WINDOWS):
        lo = jnp.maximum(t - w // 2, 0)
        hi = jnp.minimum(t + (w - w // 2), seq_len)
        sel = jnp.where(s >= lo, jnp.where(s < hi, 1.0, 0.0), 0.0).astype(BF16)
        tot = sum(jnp.dot(sel, part, preferred_element_type=F32) for part in parts)
        cnt = (jnp.minimum(tcol + (w - w // 2), seq_len) - jnp.maximum(tcol - w // 2, 0)).astype(F32)
        d = jnp.where(grp == gi, tot / cnt - p_cur, d)
    return d


def _first_max(x, pos, sentinel):
    m = jnp.max(x, axis=0, keepdims=True)
    idx = jnp.min(jnp.where(x == m, pos, sentinel), axis=0, keepdims=True)
    return m, idx


def _topk_stage1(qh, sk_ref):
    c = qh.shape[0]
    key_pos = lax.broadcasted_iota(jnp.int32, (PEER_NKEYS, c), 0).astype(F32)
    row16 = lax.broadcasted_iota(jnp.int32, (PEER_TOPK, c), 0)
    neg = jnp.float32(-jnp.inf)
    s0 = _nt_dot(sk_ref[0], qh)
    s1 = _nt_dot(sk_ref[1], qh)

    def stage1(a, carry):
        out = []
        for s, sv, si in (carry[0:3], carry[3:6]):
            m, idx = _first_max(s, key_pos, float(PEER_NKEYS))
            out += [jnp.where(key_pos == idx, neg, s),
                    jnp.where(row16 == a, m, sv), jnp.where(row16 == a, idx, si)]
        return tuple(out)

    zf = jnp.zeros((PEER_TOPK, c), F32)
    _, sv0, si0, _, sv1, si1 = lax.fori_loop(0, PEER_TOPK, stage1, (s0, zf, zf, s1, zf, zf))
    return sv0, si0, sv1, si1


def _topk_pieces(sv0, sv1):
    c = sv0.shape[1]
    neg = jnp.float32(-jnp.inf)
    sub8 = lax.broadcasted_iota(jnp.int32, (8, c), 0)
    sub8f = sub8.astype(F32)
    cs, cf = [], []
    for a in range(8):
        nb = PEER_TOPK // (a + 1)
        for b0 in range(0, nb, 8):
            val = sv0[a:a + 1] + sv1[b0:b0 + 8]
            if nb - b0 < 8:
                val = jnp.where(sub8 < nb - b0, val, neg)
            cs.append(val)
            cf.append(sub8f + float(a * PEER_TOPK + b0))
    cs.append(sv0[8:16] + sv1[0:1])
    cf.append((sub8f + 8.0) * float(PEER_TOPK))
    return cs, cf


def _topk_stage2(chains, cf):
    npc = len(cf)
    c = cf[0].shape[1]
    row16 = lax.broadcasted_iota(jnp.int32, (PEER_TOPK, c), 0)
    neg = jnp.float32(-jnp.inf)
    nflat = float(PEER_TOPK * PEER_TOPK)
    zf = jnp.zeros((PEER_TOPK, c), F32)

    def step(k, carry):
        out = []
        for ch in range(len(chains)):
            vals = carry[ch * (npc + 2):ch * (npc + 2) + npc]
            tv, tp = carry[ch * (npc + 2) + npc], carry[ch * (npc + 2) + npc + 1]
            m = vals[0]
            for v in vals[1:]:
                m = jnp.maximum(m, v)
            m = jnp.max(m, axis=0, keepdims=True)
            pos = None
            for v, f in zip(vals, cf):
                cand = jnp.where(v == m, f, nflat)
                pos = cand if pos is None else jnp.minimum(pos, cand)
            pos = jnp.min(pos, axis=0, keepdims=True)
            out += [jnp.where(f == pos, neg, v) for v, f in zip(vals, cf)]
            out += [jnp.where(row16 == k, m, tv), jnp.where(row16 == k, pos, tp)]
        return tuple(out)

    init = []
    for cs in chains:
        init += list(cs) + [zf, zf]
    res = lax.fori_loop(0, PEER_TOPK, step, tuple(init))
    return [(res[ch * (npc + 2) + npc], res[ch * (npc + 2) + npc + 1]) for ch in range(len(chains))]


def _topk_ids(tp, si0, si1):
    a = jnp.floor(tp * (1.0 / PEER_TOPK))
    b = tp - a * float(PEER_TOPK)
    ea = jnp.zeros_like(tp)
    eb = jnp.zeros_like(tp)
    for j in range(PEER_TOPK):
        ea = jnp.where(a == float(j), si0[j:j + 1], ea)
        eb = jnp.where(b == float(j), si1[j:j + 1], eb)
    return ea * float(PEER_NKEYS) + eb


def _out_kernel(on_ref, om_ref, pc_ref, pp_ref, pn_ref, x_ref, mod_ref,
                won_ref, wop_ref, wom_ref, pw_ref, ps_ref, n2_ref, wq_ref, sk_ref,
                x1_ref, h2_ref, ids_ref, gt_ref, gn_ref, h2c_ref, q_scr, idt_scr, *, bps, seq_len):
    d = x_ref.shape[1]
    rows = x_ref.shape[0]
    i = pl.program_id(0)
    mod = mod_ref[0]
    g1 = mod[:, 2 * d:3 * d]
    sh2 = mod[:, 3 * d:4 * d]
    sc2 = mod[:, 4 * d:5 * d]

    posb = (i % bps) * rows
    dpool = _pool(pp_ref[...], pc_ref[...], pn_ref[...], posb, seq_len)
    ypool = jnp.dot(dpool.astype(BF16), pw_ref[...], preferred_element_type=F32) * ps_ref[...]
    mix = (jnp.dot(on_ref[...], won_ref[...], preferred_element_type=F32)
           + jnp.dot(ypool.astype(BF16), wop_ref[...], preferred_element_type=F32)
           + jnp.dot(om_ref[...], wom_ref[...], preferred_element_type=F32))
    x1 = x_ref[...] + g1 * mix
    x1_ref[...] = x1
    h2 = _rms(x1, n2_ref[...]) * (1.0 + sc2) + sh2
    h2_ref[...] = h2
    for j in range(d // LANE):
        h2c_ref[j] = h2[:, j * LANE:(j + 1) * LANE]

    q = jnp.dot(h2.astype(BF16), wq_ref[...], preferred_element_type=F32)
    for hh in range(PEER_HEADS):
        q_scr[hh] = q[:, hh * LANE:(hh + 1) * LANE].astype(BF16)

    chunks = range(0, rows, LANE)

    def head(hh, _):
        sorted_keys = [_topk_stage1(q_scr[hh, c0:c0 + LANE, :], sk_ref) for c0 in chunks]
        pieces = [_topk_pieces(sv0, sv1) for sv0, _, sv1, _ in sorted_keys]
        picked = _topk_stage2([cs for cs, _ in pieces], pieces[0][1])
        r0 = pl.multiple_of(hh * PEER_TOPK, PEER_TOPK)
        for c0, (tv, tp), (_, si0, _, si1) in zip(chunks, picked, sorted_keys):
            ex = jnp.exp(tv - tv[0:1])
            gt_ref[pl.ds(r0, PEER_TOPK), c0:c0 + LANE] = ex / jnp.sum(ex, axis=0, keepdims=True)
            idt_scr[pl.ds(r0, PEER_TOPK), c0:c0 + LANE] = _topk_ids(tp, si0, si1)
        return 0

    lax.fori_loop(0, PEER_HEADS, head, 0)
    ids_ref[...] = idt_scr[...].T.astype(jnp.int32)
    gn_ref[...] = gt_ref[...].T


def _out_proj(x, on, om, p, mod_l, row_off, bpm, lw, seq_len):
    n, d = x.shape
    nb = n // TB
    bps = seq_len // TB
    halo = 8
    hb = TB // halo
    tok = lambda w: pl.BlockSpec((TB, w), lambda i: (i, 0))
    in_specs = [tok(HW), tok(HW), tok(256),
                pl.BlockSpec((halo, 256), lambda i: (jnp.maximum(i * hb - 1, 0), 0)),
                pl.BlockSpec((halo, 256), lambda i: (jnp.minimum((i + 1) * hb, n // halo - 1), 0)),
                tok(d),
                pl.BlockSpec((1, 1, mod_l.shape[-1]), lambda i: (row_off + i // bpm, 0, 0)),
                _const_spec((HW, d)), _const_spec((256, d)), _const_spec((HW, d)),
                _const_spec((256, 256)), _const_spec((1, 256)), _const_spec((1, d)),
                _const_spec((d, PEER_HEADS * LANE)), _const_spec((2, PEER_NKEYS, LANE))]
    nk = PEER_HEADS * PEER_TOPK
    return pl.pallas_call(
        functools.partial(_out_kernel, bps=bps, seq_len=seq_len),
        grid=(nb,),
        in_specs=in_specs,
        out_specs=[tok(d), tok(d), tok(nk), pl.BlockSpec((nk, TB), lambda i: (0, i)), tok(nk),
                   pl.BlockSpec((d // LANE, TB, LANE), lambda i: (0, i, 0))],
        out_shape=[jax.ShapeDtypeStruct((n, d), F32), jax.ShapeDtypeStruct((n, d), F32),
                   jax.ShapeDtypeStruct((n, nk), jnp.int32), jax.ShapeDtypeStruct((nk, n), F32),
                   jax.ShapeDtypeStruct((n, nk), F32),
                   jax.ShapeDtypeStruct((d // LANE, n, LANE), F32)],
        scratch_shapes=[pltpu.VMEM((PEER_HEADS, TB, LANE), BF16), pltpu.VMEM((nk, TB), F32)],
        compiler_params=_params(("arbitrary",)),
        name="out_proj",
    )(on, om, p, p, p, x, mod_l, lw["w_o_na"], lw["w_o_pool"], lw["w_o_mla"],
      lw["pool_w"], lw["pool_scale"], lw["norm2"], lw["peer_wq"], lw["peer_sk"])


def _gelu_tanh(x):
    return x * (0.5 * (1.0 + jnp.tanh(0.7978845608028654 * (x + 0.044715 * (x * x * x)))))


def _peer_token_mix(chunk, hrow, gcol, ch):
    acc = None
    for s in range(ch):
        us = lax.bitcast_convert_type(chunk(s) & jnp.int32(-65536), F32)
        term = us * hrow[:, s * LANE:(s + 1) * LANE]
        acc = term if acc is None else acc + term
    wgt = gcol * _gelu_tanh(jnp.sum(acc, axis=-1, keepdims=True))
    parts = []
    for s in range(ch):
        vs = lax.bitcast_convert_type(chunk(s) << 16, F32)
        parts.append(jnp.sum(vs * wgt, axis=0, keepdims=True))
    return jnp.concatenate(parts, axis=-1)


def _sc_peer(table3, ids, gates, h2c, n):
    ch, _, lane = h2c.shape
    nk = ids.shape[1]
    info = plsc.get_sparse_core_info()
    nc, nw, nl = info.num_cores, info.num_cores * info.num_subcores, info.num_lanes
    tpw = n // nw
    win = 32
    nq = nk // win
    cpr = lane // nl
    nchunk = ch * cpr
    hc = nchunk // 2
    assert n % nw == 0 and nk % win == 0 and win % nl == 0
    mesh = plsc.VectorSubcoreMesh(core_axis_name="core", subcore_axis_name="subcore")
    hi_mask = jnp.int32(-65536)

    @functools.partial(
        pl.kernel, mesh=mesh,
        out_type=jax.ShapeDtypeStruct((ch, n, lane), F32),
        compiler_params=pltpu.CompilerParams(needs_layout_passes=False),
        scratch_types=[pltpu.VMEM((nk,), jnp.int32), pltpu.VMEM((nk,), F32),
                       pltpu.VMEM((ch, lane), F32), pltpu.VMEM((ch, lane), F32),
                       pltpu.VMEM((win, ch, lane), jnp.int32), pltpu.VMEM((win, ch, lane), jnp.int32),
                       pltpu.VMEM((win * nl,), F32), pltpu.VMEM((win,), F32),
                       pltpu.SemaphoreType.DMA, pltpu.SemaphoreType.DMA, pltpu.SemaphoreType.DMA])
    def peer(tab_hbm, ids_hbm, g_hbm, h2_hbm, y_hbm,
             idx_v, g_v, x_v, y_v, rows_a, rows_b, part_v, w_v, sem_a, sem_b, sem_x):
        wid = lax.axis_index("subcore") * nc + lax.axis_index("core")
        bufs = ((rows_a, sem_a), (rows_b, sem_b))
        lanes = lax.iota(jnp.int32, nl)
        zero = jnp.zeros((nl,), F32)

        def chunk_copies(tok, to_hbm):
            if to_hbm:
                return [pltpu.make_async_copy(y_v.at[j], y_hbm.at[j, tok], sem_x) for j in range(ch)]
            return [pltpu.make_async_copy(h2_hbm.at[j, tok], x_v.at[j], sem_x) for j in range(ch)]

        def fetch(q):
            rows, sem = bufs[q % 2]
            return pltpu.make_async_copy(tab_hbm.at[idx_v.at[pl.ds(q * win, win)]], rows, sem)

        def word(rows, r, cc):
            return rows[r, cc // cpr, pl.ds((cc % cpr) * nl, nl)]

        @pl.loop(0, tpw)
        def _(ti):
            tok = wid * tpw + ti
            loads = chunk_copies(tok, False)
            for cp in loads:
                cp.start()
            pltpu.sync_copy(ids_hbm.at[tok], idx_v)
            pltpu.sync_copy(g_hbm.at[tok], g_v)
            for cp in loads:
                cp.wait()
            for cc in range(nchunk):
                y_v[cc // cpr, pl.ds((cc % cpr) * nl, nl)] = zero
            fetch(0).start()
            for q in range(nq):
                rows = bufs[q % 2][0]
                fetch(q).wait()
                if q + 1 < nq:
                    fetch(q + 1).start()

                for half in range(2):
                    xs = [x_v[(half * hc + c) // cpr, pl.ds(((half * hc + c) % cpr) * nl, nl)]
                          for c in range(hc)]

                    @pl.loop(0, win, step=2)
                    def _(r0):
                        pos = [pl.multiple_of((r0 + k) * nl, nl) for k in range(2)]
                        accs = [[part_v[pl.ds(pos[k], nl)] if half else None, None, None, None]
                                for k in range(2)]
                        for c in range(hc):
                            for k in range(2):
                                u = lax.bitcast_convert_type(
                                    word(rows, r0 + k, half * hc + c) & hi_mask, F32)
                                t = u * xs[c]
                                accs[k][c % 4] = t if accs[k][c % 4] is None else accs[k][c % 4] + t
                        for k in range(2):
                            part_v[pl.ds(pos[k], nl)] = (accs[k][0] + accs[k][1]) + (accs[k][2] + accs[k][3])

                for grp in range(win // nl):
                    s = zero
                    for rr in range(nl):
                        tot = jnp.sum(part_v[pl.ds((grp * nl + rr) * nl, nl)])
                        s = jnp.where(lanes == rr, tot, s)
                    z = 0.7978845608028654 * (s + 0.044715 * (s * s * s))
                    tanh = 1.0 - 2.0 / (jnp.exp(2.0 * z) + 1.0)
                    gate = g_v[pl.ds(q * win + grp * nl, nl)]
                    w_v[pl.ds(grp * nl, nl)] = gate * (s * (0.5 * (1.0 + tanh)))

                for half in range(2):
                    def body(r, yacc):
                        wr = plsc.load_gather(w_v, [jnp.full((nl,), r, jnp.int32)])
                        out = []
                        for c in range(hc):
                            v = lax.bitcast_convert_type(word(rows, r, half * hc + c) << 16, F32)
                            out.append(yacc[c] + wr * v)
                        return tuple(out)

                    yacc = lax.fori_loop(0, win, body, tuple(zero for _ in range(hc)))
                    for c in range(hc):
                        cc = half * hc + c
                        sl = (cc // cpr, pl.ds((cc % cpr) * nl, nl))
                        y_v[sl] = y_v[sl] + yacc[c]
            stores = chunk_copies(tok, True)
            for cp in stores:
                cp.start()
            for cp in stores:
                cp.wait()

    return peer(table3, ids, gates, h2c)


def _residual_kernel(x1_ref, y_ref, mod_ref, x2_hbm, after_hbm, o_ref):
    del x2_hbm
    del after_hbm
    d = x1_ref.shape[1]
    g2 = mod_ref[0][:, 5 * d:6 * d]
    for j in range(d // LANE):
        sl = slice(j * LANE, (j + 1) * LANE)
        o_ref[:, sl] = x1_ref[:, sl] + g2[:, sl] * y_ref[j]


def _residual(x1, y, mod_l, row, x2, after):
    n, d = x1.shape
    tok = pl.BlockSpec((PEER_TB, d), lambda i: (i, 0))
    any_spec = pl.BlockSpec(memory_space=pl.ANY)
    return pl.pallas_call(
        _residual_kernel,
        grid=(y.shape[1] // PEER_TB,),
        in_specs=[tok, pl.BlockSpec((d // LANE, PEER_TB, LANE), lambda i: (0, i, 0)),
                  pl.BlockSpec((1, 1, mod_l.shape[-1]), lambda i: (row, 0, 0)),
                  any_spec, any_spec],
        out_specs=tok,
        out_shape=jax.ShapeDtypeStruct((n, d), F32),
        input_output_aliases={3: 0},
        compiler_params=_params(("arbitrary",)),
        name="residual",
    )(x1, y, mod_l, x2, after)


def _peer_kernel(ids_hbm, gt_ref, h2_ref, x1_ref, mod_ref, tab_hbm, o_ref,
                 ids_s, buf, sem_i, sem_r, *, first_block):
    d = x1_ref.shape[1]
    ch = d // LANE
    pitch = ch + 1
    nsub = x1_ref.shape[0] // PEER_SUB
    nk = gt_ref.shape[0]
    nids = PEER_SUB * nk
    i = pl.program_id(0) + first_block
    g2 = mod_ref[0][:, 5 * d:6 * d]
    tok_lane = lax.broadcasted_iota(jnp.int32, gt_ref.shape, 1)

    def ids_copy(j, slot):
        start = pl.multiple_of((i * nsub + j) * nids, nids)
        return pltpu.make_async_copy(ids_hbm.at[pl.ds(start, nids)],
                                     ids_s.at[pl.ds(slot * nids, nids)], sem_i.at[slot])

    def row_copy(slot, e, f):
        src = tab_hbm.at[e]
        dst = buf.at[slot, pl.ds(f * pitch, ch), :]
        return pltpu.make_async_copy(src, dst, sem_r.at[slot])

    def issue_rows(slot):
        for t in range(PEER_SUB):
            def body(kk, _):
                for r in range(8):
                    f = t * nk + kk * 8 + r
                    row_copy(slot, ids_s[slot * nids + f], f).start(priority=r % 2)
                return 0

            lax.fori_loop(0, nk // 8, body, 0)

    def wait_rows(slot):
        done = buf.at[slot, pl.ds(0, nids * ch), :]
        pltpu.make_async_copy(done, done, sem_r.at[slot]).wait()

    def compute(slot, j):
        base = pl.multiple_of(j * PEER_SUB, PEER_SUB)
        h8 = h2_ref[pl.ds(base, PEER_SUB), :]
        ys = []
        for t in range(PEER_SUB):
            chunk = lambda s: buf[slot, pl.ds(t * nk * pitch + s, nk, stride=pitch), :]
            gcol = jnp.sum(jnp.where(tok_lane == base + t, gt_ref[...], 0.0), axis=-1, keepdims=True)
            ys.append(_peer_token_mix(chunk, h8[t:t + 1, :], gcol, ch))
        y8 = jnp.concatenate(ys, axis=0)
        o_ref[pl.ds(base, PEER_SUB), :] = x1_ref[pl.ds(base, PEER_SUB), :] + g2 * y8

    first = ids_copy(0, 0)
    first.start()
    first.wait()
    issue_rows(0)
    ids_copy(1, 1).start()

    def pair(jj, _):
        j0 = 2 * jj
        ids_copy(j0 + 1, 1).wait()
        issue_rows(1)

        @pl.when(j0 + 2 < nsub)
        def _():
            ids_copy(j0 + 2, 0).start()

        wait_rows(0)
        compute(0, j0)

        @pl.when(j0 + 2 < nsub)
        def _():
            ids_copy(j0 + 2, 0).wait()
            issue_rows(0)

        @pl.when(j0 + 3 < nsub)
        def _():
            ids_copy(j0 + 3, 1).start()

        wait_rows(1)
        compute(1, j0 + 1)
        return 0

    lax.fori_loop(0, nsub // 2, pair, 0)


def _pack_tables(peer_u, peer_v):
    e, d = peer_u.shape
    ub = lax.bitcast_convert_type(peer_u.astype(BF16), jnp.uint16).astype(jnp.uint32)
    vb = lax.bitcast_convert_type(peer_v.astype(BF16), jnp.uint16).astype(jnp.uint32)
    words = lax.bitcast_convert_type((ub << 16) | vb, jnp.int32)
    return words.reshape(e, d // LANE, LANE)


def _peer(x1, h2, ids, gt, mod_l, row, table, tok0):
    n, d = x1.shape
    nk = gt.shape[0]
    b0 = tok0 // PEER_TB
    nb = n // PEER_TB - b0
    tok = pl.BlockSpec((PEER_TB, d), lambda i: (i + b0, 0))
    any_spec = pl.BlockSpec(memory_space=pl.ANY)
    return pl.pallas_call(
        functools.partial(_peer_kernel, first_block=b0),
        grid=(nb,),
        in_specs=[any_spec,
                  pl.BlockSpec((nk, PEER_TB), lambda i: (0, i + b0)),
                  tok, tok,
                  pl.BlockSpec((1, 1, mod_l.shape[-1]), lambda i: (row, 0, 0)),
                  any_spec],
        out_specs=tok,
        out_shape=jax.ShapeDtypeStruct((n, d), F32),
        scratch_shapes=[pltpu.SMEM((2 * PEER_SUB * nk,), jnp.int32),
                        pltpu.VMEM((2, PEER_SUB * nk * (d // LANE + 1), LANE), jnp.int32),
                        pltpu.SemaphoreType.DMA((2,)),
                        pltpu.SemaphoreType.DMA((2,))],
        compiler_params=_params(("arbitrary",)),
        name="peer",
    )(ids.reshape(n * nk), gt, h2, x1, mod_l, table)


def _pad_heads(w, width):
    pad = [(0, 0)] * (w.ndim - 1) + [(0, LANE - width)]
    w = jnp.pad(w, pad)
    return w.reshape(w.shape[:-2] + (HW,))


def _head_gain(g, width):
    depth = g.shape[0]
    g = jnp.pad(g, ((0, 0), (0, LANE - width)))
    return jnp.tile(g, (1, HEADS)).reshape(depth, 1, HW)


def _rope_tables(seq):
    t = np.arange(seq)
    half = MLA_ROPE // 2
    inv = ROPE_THETA ** (-np.arange(0, half, 2, dtype=np.float32) / half)
    cos = np.ones((seq, LANE), np.float32)
    sin = np.zeros((seq, LANE), np.float32)
    for off, pos in ((MLA_NOPE, t // GRID_W), (MLA_NOPE + half, t % GRID_W)):
        ang = pos.astype(np.float32)[:, None] * inv[None, :]
        q = half // 2
        cos[:, off:off + q] = np.cos(ang)
        cos[:, off + q:off + half] = np.cos(ang)
        sin[:, off:off + q] = -np.sin(ang)
        sin[:, off + q:off + half] = np.sin(ang)
    return jnp.asarray(cos), jnp.asarray(sin)


def _nat_bias(rel_bias):
    v = np.arange(WIN_R)[:, None]
    j = np.arange(WIN_R)[None, :]
    dr = j - v + WIN_R - 1
    cq = np.arange(GRID_W)[:, None]
    kc = np.arange(GRID_W)[None, :]
    cstart = np.clip(cq - WIN_C // 2, 0, GRID_W - WIN_C)
    ok = (kc >= cstart) & (kc < cstart + WIN_C)
    dc = np.clip(kc - cq + WIN_C - 1, 0, 2 * WIN_C - 2)
    b = rel_bias[:, :, dr]
    b = b[..., dc]
    b = jnp.where(jnp.asarray(ok)[None, None, None, None], b, NEG_INF)
    b = jnp.transpose(b, (0, 2, 1, 4, 3, 5))
    return b.reshape(b.shape[0], WIN_R, HEADS, GRID_W, WIN_R * GRID_W)


def _layer_weights(w_in, na_q_norm, na_k_norm, mla_cq_norm, mla_ckv_norm, mla_w_uq, mla_w_ukv,
                   mla_q_norm, mla_k_norm, w_out, pool_w, pool_scale, norm1, norm2,
                   peer_wq, peer_subkeys):
    depth, d, _ = w_in.shape
    na_w = HEADS * NA_DH
    segs = np.cumsum([0, na_w, na_w, na_w, 256, 256, 128, MLA_ROPE])
    part = lambda i: w_in[:, :, segs[i]:segs[i + 1]]
    heads = lambda w: _pad_heads(w.reshape(depth, d, HEADS, NA_DH), NA_DH)
    w_in_p = jnp.concatenate(
        [heads(part(0)), heads(part(1)), heads(part(2)), part(3), part(4), part(5),
         jnp.pad(part(6), ((0, 0), (0, 0), (0, LANE - MLA_ROPE)))], axis=-1).astype(BF16)

    w_uq = _pad_heads(mla_w_uq, MLA_QK).astype(BF16)
    k_nope = _pad_heads(mla_w_ukv[..., :MLA_NOPE], MLA_NOPE)
    eye = np.zeros((MLA_ROPE, HEADS, LANE), np.float32)
    for h in range(HEADS):
        eye[np.arange(MLA_ROPE), h, MLA_NOPE + np.arange(MLA_ROPE)] = 1.0
    eye = jnp.broadcast_to(jnp.asarray(eye.reshape(MLA_ROPE, HW)), (depth, MLA_ROPE, HW))
    zer = jnp.zeros((depth, 256 - 128 - MLA_ROPE, HW), F32)
    w_k = jnp.concatenate([k_nope, eye, zer], axis=1).astype(BF16)
    w_v = jnp.concatenate([_pad_heads(mla_w_ukv[..., MLA_NOPE:], MLA_V),
                           jnp.zeros((depth, 128, HW), F32)], axis=1).astype(BF16)

    mix_w = HEADS * NA_DH
    w_o_na = jnp.pad(w_out[:, :mix_w].reshape(depth, HEADS, NA_DH, d),
                     ((0, 0), (0, 0), (0, LANE - NA_DH), (0, 0))).reshape(depth, HW, d).astype(BF16)
    w_o_pool = w_out[:, mix_w:mix_w + 256].astype(BF16)
    w_o_mla = jnp.pad(w_out[:, mix_w + 256:].reshape(depth, HEADS, MLA_V, d),
                      ((0, 0), (0, 0), (0, LANE - MLA_V), (0, 0))).reshape(depth, HW, d).astype(BF16)
    ng = len(POOL_WINDOWS)
    pw = jnp.zeros((depth, ng * POOL_G, ng * POOL_G), F32)
    for g in range(ng):
        pw = pw.at[:, g * POOL_G:(g + 1) * POOL_G, g * POOL_G:(g + 1) * POOL_G].set(pool_w[:, g])

    half = peer_subkeys.shape[-1]
    sk = jnp.stack([jnp.pad(peer_subkeys[:, 0], ((0, 0), (0, 0), (0, LANE - half))),
                    jnp.pad(peer_subkeys[:, 1], ((0, 0), (0, 0), (LANE - half, 0)))], axis=1).astype(BF16)

    return dict(
        w_in=w_in_p, w_uq=w_uq, w_k=w_k, w_v=w_v,
        g_q=_head_gain(na_q_norm, NA_DH), g_k=_head_gain(na_k_norm, NA_DH),
        g_cq=mla_cq_norm[:, None, :], g_ckv=mla_ckv_norm[:, None, :],
        g_qm=_head_gain(mla_q_norm, MLA_QK), g_km=_head_gain(mla_k_norm, MLA_QK),
        w_o_na=w_o_na, w_o_pool=w_o_pool, w_o_mla=w_o_mla,
        pool_w=pw.astype(BF16), pool_scale=pool_scale[:, None, :],
        norm1=norm1[:, None, :], norm2=norm2[:, None, :],
        peer_wq=peer_wq.astype(BF16), peer_sk=sk)


def kernel(x_prompt, x_sample, c, cache_nat_k, cache_nat_v, cache_mla_ckv, cache_mla_krope, c_ctx, w_mod, b_mod, norm1, norm2, w_in, na_q_norm, na_k_norm, na_rel_bias, pool_w, pool_scale, mla_cq_norm, mla_ckv_norm, mla_w_uq, mla_w_ukv, mla_q_norm, mla_k_norm, w_out, peer_wq, peer_subkeys, peer_u, peer_v):
    batch, seq, d = x_prompt.shape
    db, ds, _ = x_sample.shape
    depth = w_mod.shape[0]
    past = cache_nat_k.shape[2]
    assert seq == TB and ds % TB == 0 and ds % (GRID_W * WIN_R) == 0 and db + 1 <= 8

    cond8 = jnp.concatenate([c_ctx[None, :], c, jnp.zeros((8 - 1 - db, d), F32)], axis=0)
    mod = _modulation(cond8, w_mod, b_mod).reshape(depth, 8, 1, 6 * d)

    lw_all = _layer_weights(w_in, na_q_norm, na_k_norm, mla_cq_norm, mla_ckv_norm, mla_w_uq,
                            mla_w_ukv, mla_q_norm, mla_k_norm, w_out, pool_w, pool_scale,
                            norm1, norm2, peer_wq, peer_subkeys)
    bias_all = _nat_bias(na_rel_bias)
    tables = [_pack_tables(peer_u[l], peer_v[l]) for l in range(depth)]
    cos_lat, sin_lat = _rope_tables(ds)
    cos_ctx = jnp.ones((TB, LANE), F32)
    sin_ctx = jnp.zeros((TB, LANE), F32)

    ck = jnp.concatenate([cache_mla_ckv, cache_mla_krope,
                          jnp.zeros(cache_mla_ckv.shape[:-1] + (256 - 128 - MLA_ROPE,), F32)],
                         axis=-1).astype(BF16)
    kc_mla, vc_mla = _cache_kv(ck, lw_all["w_k"], lw_all["w_v"], lw_all["g_km"])
    kc_na = _pad_heads(cache_nat_k, NA_DH).astype(BF16)
    vc_na = _pad_heads(cache_nat_v, NA_DH).astype(BF16)

    xs = [x_prompt.reshape(batch * seq, d)] + [x_sample[b] for b in range(db)]
    one_row = max(batch * seq, ds) + 1
    lat_bpm = ds // TB
    ks, vs, ckvs, krs = [], [], [], []
    pending = None
    after = xs[0]

    def join(item, follow):
        si, x1, y_sc, x2, mod_l = item
        xs[si] = _residual(x1, y_sc, mod_l, si, x2, follow)
        return xs[si]

    for l in range(depth):
        lw = {k: v[l] for k, v in lw_all.items()}
        mod_l = mod[l]
        for si in range(db + 1):
            x = xs[si]
            if si == 0:
                (qn, kn, vn, knf, vnf, p, qm, km, vm, ckv, kr) = _in_proj(
                    x, mod_l, 0, one_row, lw, cos_ctx, sin_ctx, 1, after)
                on, om = _ctx_attn(qn, kn, vn, qm, km, vm, seq)
                x1, h2, ids, gt, gn, h2c = _out_proj(x, on, om, p, mod_l, 0, one_row, lw, seq)
                ks.append(knf.reshape(batch, seq, HEADS, LANE)[..., :NA_DH])
                vs.append(vnf.reshape(batch, seq, HEADS, LANE)[..., :NA_DH])
                ckvs.append(ckv.reshape(batch, seq, 128))
                krs.append(kr.reshape(batch, seq, LANE)[..., :MLA_ROPE])
            else:
                b = si - 1
                (qn, kn, vn, _, _, p, qm, km, vm, _, _) = _in_proj(
                    x, mod_l, si, one_row, lw, cos_lat, sin_lat, lat_bpm, after)
                on = _nat_attn(qn, kn, vn, kc_na[b:b + 1, l], vc_na[b:b + 1, l], bias_all[l], 1)
                om = _lat_mla(qm, km, vm, kc_mla[b:b + 1, l], vc_mla[b:b + 1, l], 1)
                x1, h2, ids, gt, gn, h2c = _out_proj(x, on, om, p, mod_l, si, one_row, lw, ds)
            n_sc = x.shape[0] * SC_SHARE[0] // SC_SHARE[1] // PEER_TB * PEER_TB
            if (l + si) % 2:
                n_sc -= PEER_TB
            if l == depth - 1 and si == db:
                n_sc = x.shape[0] // 2
            y_sc = _sc_peer(tables[l], ids, gn, h2c, n_sc)
            x2 = _peer(x1, h2, ids, gt, mod_l, si, tables[l], n_sc)
            after = x2 if pending is None else join(pending, x2)
            pending = (si, x1, y_sc, x2, mod_l)
    join(pending, pending[1])

    return (xs[0].reshape(batch, seq, d), jnp.stack(xs[1:], axis=0),
            jnp.stack(ks, axis=1), jnp.stack(vs, axis=1),
            jnp.stack(ckvs, axis=1), jnp.stack(krs, axis=1))
```
